```python
import math
import jax, jax.numpy as jnp
from jax import lax
import numpy as np

D_MODEL = 1024
BATCH = 8
SEQ = 4096
DEPTH = 2

N_MIXERS = 2
N_A = (DEPTH + 1) // 2
N_B = DEPTH // 2
SHORT_CONV_K = 3
CONFORMER_K = 31
FFN_CONV_K = 3
D_FF = 2816
N_ADA = 6
LN_EPS = 1e-5
DEEPNORM_ALPHA = (2.0 * DEPTH) ** 0.25
DEEPNORM_BETA = (8.0 * DEPTH) ** -0.25

kernel_name = "hybrid_shortconv_conformer_convffn_deepnorm_adaln"


def layer_norm(x, g, b):
    xf = x.astype(jnp.float32)
    mu = jnp.mean(xf, axis=-1, keepdims=True)
    var = jnp.mean(jnp.square(xf - mu), axis=-1, keepdims=True)
    y = (xf - mu) * lax.rsqrt(var + LN_EPS) * g.astype(jnp.float32) + b.astype(jnp.float32)
    return y.astype(x.dtype)


def causal_dwconv(x, w, b):
    k = w.shape[0]
    y = lax.conv_general_dilated(
        x, w[:, None, :].astype(x.dtype), window_strides=(1,), padding=[(k - 1, 0)],
        dimension_numbers=("NWC", "WIO", "NWC"), feature_group_count=x.shape[-1])
    return y + b.astype(x.dtype)


def modulate(h, shift, scale):
    return h * (1.0 + scale) + shift


def short_gated_conv(h, w_in, conv_w, conv_b, w_out):
    bcv = jnp.einsum("bsd,de->bse", h, w_in)
    gb, gc, v = jnp.split(bcv, 3, axis=-1)
    y = gb * causal_dwconv(gc * v, conv_w, conv_b)
    return jnp.einsum("bsd,de->bse", y, w_out)


def conformer_conv(h, w_pw1, b_pw1, conv_w, conv_b, ln_g, ln_b, w_pw2, b_pw2):
    u = jnp.einsum("bsd,de->bse", h, w_pw1) + b_pw1
    a, g = jnp.split(u, 2, axis=-1)
    a = a * jax.nn.sigmoid(g)
    a = causal_dwconv(a, conv_w, conv_b)
    a = jax.nn.silu(layer_norm(a, ln_g, ln_b))
    return jnp.einsum("bsd,de->bse", a, w_pw2) + b_pw2


def conv_ffn(h, w_up, conv_w, conv_b, w_gate, w_down):
    u = causal_dwconv(jnp.einsum("bsd,df->bsf", h, w_up), conv_w, conv_b)
    v = jnp.einsum("bsd,df->bsf", h, w_gate)
    return jnp.einsum("bsf,fd->bsd", jax.nn.silu(u) * v, w_down)


def _fwd_setup_inputs(seed: int = 0) -> dict:
    key = jax.random.key(seed)
    ks = iter(jax.random.split(key, 40))
    D, F = D_MODEL, D_FF

    def nrm(shape, std):
        return jax.random.normal(next(ks), shape, jnp.float32) * std

    def gain(shape):
        return 1.0 + nrm(shape, 0.02)

    return {
        "x": nrm((BATCH, SEQ, D), 1.0),
        "c": nrm((BATCH, D), 1.0),
        "ada_w": nrm((DEPTH, D, N_ADA * D), D ** -0.5),
        "ada_b": nrm((DEPTH, N_ADA * D), 0.02),
        "ln_tok_g": gain((DEPTH, D)),
        "ln_tok_b": nrm((DEPTH, D), 0.02),
        "ln_ch_g": gain((DEPTH, D)),
        "ln_ch_b": nrm((DEPTH, D), 0.02),
        "a_w_in": nrm((N_A, D, 3 * D), D ** -0.5),
        "a_conv_w": nrm((N_A, SHORT_CONV_K, D), SHORT_CONV_K ** -0.5),
        "a_conv_b": nrm((N_A, D), 0.02),
        "a_w_out": nrm((N_A, D, D), DEEPNORM_BETA * D ** -0.5),
        "b_w_pw1": nrm((N_B, D, 2 * D), D ** -0.5),
        "b_b_pw1": nrm((N_B, 2 * D), 0.02),
        "b_conv_w": nrm((N_B, CONFORMER_K, D), CONFORMER_K ** -0.5),
        "b_conv_b": nrm((N_B, D), 0.02),
        "b_ln_g": gain((N_B, D)),
        "b_ln_b": nrm((N_B, D), 0.02),
        "b_w_pw2": nrm((N_B, D, D), DEEPNORM_BETA * D ** -0.5),
        "b_b_pw2": nrm((N_B, D), 0.02),
        "f_w_up": nrm((DEPTH, D, F), D ** -0.5),
        "f_conv_w": nrm((DEPTH, FFN_CONV_K, F), FFN_CONV_K ** -0.5),
        "f_conv_b": nrm((DEPTH, F), 0.02),
        "f_w_gate": nrm((DEPTH, D, F), D ** -0.5),
        "f_w_down": nrm((DEPTH, F, D), DEEPNORM_BETA * F ** -0.5),
    }


def _fwd_reference(x, c, ada_w, ada_b, ln_tok_g, ln_tok_b, ln_ch_g, ln_ch_b,
              a_w_in, a_conv_w, a_conv_b, a_w_out,
              b_w_pw1, b_b_pw1, b_conv_w, b_conv_b, b_ln_g, b_ln_b, b_w_pw2, b_b_pw2,
              f_w_up, f_conv_w, f_conv_b, f_w_gate, f_w_down):
    c_act = jax.nn.silu(c)
    for i in range(DEPTH):
        mod = jnp.einsum("bd,de->be", c_act, ada_w[i]) + ada_b[i]
        sh_t, sc_t, g_t, sh_c, sc_c, g_c = [m[:, None, :] for m in jnp.split(mod, N_ADA, axis=-1)]

        h = modulate(x, sh_t, sc_t)
        j = i // N_MIXERS
        if i % N_MIXERS == 0:
            y = short_gated_conv(h, a_w_in[j], a_conv_w[j], a_conv_b[j], a_w_out[j])
        else:
            y = conformer_conv(h, b_w_pw1[j], b_b_pw1[j], b_conv_w[j], b_conv_b[j],
                               b_ln_g[j], b_ln_b[j], b_w_pw2[j], b_b_pw2[j])
        x = layer_norm(DEEPNORM_ALPHA * x + g_t * y, ln_tok_g[i], ln_tok_b[i])

        h = modulate(x, sh_c, sc_c)
        y = conv_ffn(h, f_w_up[i], f_conv_w[i], f_conv_b[i], f_w_gate[i], f_w_down[i])
        x = layer_norm(DEEPNORM_ALPHA * x + g_c * y, ln_ch_g[i], ln_ch_b[i])
    return x


import jax as _jax
import jax.numpy as _jnp

TWIN_FORMAT = 'train_step'
FWD_PARAMS = ['x', 'c', 'ada_w', 'ada_b', 'ln_tok_g', 'ln_tok_b', 'ln_ch_g', 'ln_ch_b', 'a_w_in', 'a_conv_w', 'a_conv_b', 'a_w_out', 'b_w_pw1', 'b_b_pw1', 'b_conv_w', 'b_conv_b', 'b_ln_g', 'b_ln_b', 'b_w_pw2', 'b_b_pw2', 'f_w_up', 'f_conv_w', 'f_conv_b', 'f_w_gate', 'f_w_down']
TWIN_WEIGHTS = ['ada_w', 'ada_b', 'ln_tok_g', 'ln_tok_b', 'ln_ch_g', 'ln_ch_b', 'a_w_in', 'a_conv_w', 'a_conv_b', 'a_w_out', 'b_w_pw1', 'b_b_pw1', 'b_conv_w', 'b_conv_b', 'b_ln_g', 'b_ln_b', 'b_w_pw2', 'b_b_pw2', 'f_w_up', 'f_conv_w', 'f_conv_b', 'f_w_gate', 'f_w_down']
TWIN_DIFF_INPUT = 'x'
TWIN_INPUTS = ['x', 'c', 'ada_w', 'ada_b', 'ln_tok_g', 'ln_tok_b', 'ln_ch_g', 'ln_ch_b', 'a_w_in', 'a_conv_w', 'a_conv_b', 'a_w_out', 'b_w_pw1', 'b_b_pw1', 'b_conv_w', 'b_conv_b', 'b_ln_g', 'b_ln_b', 'b_w_pw2', 'b_b_pw2', 'f_w_up', 'f_conv_w', 'f_conv_b', 'f_w_gate', 'f_w_down', 'loss_target', 'm_ada_w', 'm_ada_b', 'm_ln_tok_g', 'm_ln_tok_b', 'm_ln_ch_g', 'm_ln_ch_b', 'm_a_w_in', 'm_a_conv_w', 'm_a_conv_b', 'm_a_w_out', 'm_b_w_pw1', 'm_b_b_pw1', 'm_b_conv_w', 'm_b_conv_b', 'm_b_ln_g', 'm_b_ln_b', 'm_b_w_pw2', 'm_b_b_pw2', 'm_f_w_up', 'm_f_conv_w', 'm_f_conv_b', 'm_f_w_gate', 'm_f_w_down', 'v_ada_w', 'v_ada_b', 'v_ln_tok_g', 'v_ln_tok_b', 'v_ln_ch_g', 'v_ln_ch_b', 'v_a_w_in', 'v_a_conv_w', 'v_a_conv_b', 'v_a_w_out', 'v_b_w_pw1', 'v_b_b_pw1', 'v_b_conv_w', 'v_b_conv_b', 'v_b_ln_g', 'v_b_ln_b', 'v_b_w_pw2', 'v_b_b_pw2', 'v_f_w_up', 'v_f_conv_w', 'v_f_conv_b', 'v_f_w_gate', 'v_f_w_down']
TWIN_OUTPUTS = ['loss', 'grad_x', 'grad_ada_w', 'grad_ada_b', 'grad_ln_tok_g', 'grad_ln_tok_b', 'grad_ln_ch_g', 'grad_ln_ch_b', 'grad_a_w_in', 'grad_a_conv_w', 'grad_a_conv_b', 'grad_a_w_out', 'grad_b_w_pw1', 'grad_b_b_pw1', 'grad_b_conv_w', 'grad_b_conv_b', 'grad_b_ln_g', 'grad_b_ln_b', 'grad_b_w_pw2', 'grad_b_b_pw2', 'grad_f_w_up', 'grad_f_conv_w', 'grad_f_conv_b', 'grad_f_w_gate', 'grad_f_w_down', 'delta_ada_w', 'delta_ada_b', 'delta_ln_tok_g', 'delta_ln_tok_b', 'delta_ln_ch_g', 'delta_ln_ch_b', 'delta_a_w_in', 'delta_a_conv_w', 'delta_a_conv_b', 'delta_a_w_out', 'delta_b_w_pw1', 'delta_b_b_pw1', 'delta_b_conv_w', 'delta_b_conv_b', 'delta_b_ln_g', 'delta_b_ln_b', 'delta_b_w_pw2', 'delta_b_b_pw2', 'delta_f_w_up', 'delta_f_conv_w', 'delta_f_conv_b', 'delta_f_w_gate', 'delta_f_w_down', 'new_m_ada_w', 'new_m_ada_b', 'new_m_ln_tok_g', 'new_m_ln_tok_b', 'new_m_ln_ch_g', 'new_m_ln_ch_b', 'new_m_a_w_in', 'new_m_a_conv_w', 'new_m_a_conv_b', 'new_m_a_w_out', 'new_m_b_w_pw1', 'new_m_b_b_pw1', 'new_m_b_conv_w', 'new_m_b_conv_b', 'new_m_b_ln_g', 'new_m_b_ln_b', 'new_m_b_w_pw2', 'new_m_b_b_pw2', 'new_m_f_w_up', 'new_m_f_conv_w', 'new_m_f_conv_b', 'new_m_f_w_gate', 'new_m_f_w_down', 'new_v_ada_w', 'new_v_ada_b', 'new_v_ln_tok_g', 'new_v_ln_tok_b', 'new_v_ln_ch_g', 'new_v_ln_ch_b', 'new_v_a_w_in', 'new_v_a_conv_w', 'new_v_a_conv_b', 'new_v_a_w_out', 'new_v_b_w_pw1', 'new_v_b_b_pw1', 'new_v_b_conv_w', 'new_v_b_conv_b', 'new_v_b_ln_g', 'new_v_b_ln_b', 'new_v_b_w_pw2', 'new_v_b_b_pw2', 'new_v_f_w_up', 'new_v_f_conv_w', 'new_v_f_conv_b', 'new_v_f_w_gate', 'new_v_f_w_down']
TWIN_LEAF_KINDS = {'loss': 'loss', 'grad_x': 'grad_x', 'grad_ada_w': 'grad_w', 'grad_ada_b': 'grad_w', 'grad_ln_tok_g': 'grad_w', 'grad_ln_tok_b': 'grad_w', 'grad_ln_ch_g': 'grad_w', 'grad_ln_ch_b': 'grad_w', 'grad_a_w_in': 'grad_w', 'grad_a_conv_w': 'grad_w', 'grad_a_conv_b': 'grad_w', 'grad_a_w_out': 'grad_w', 'grad_b_w_pw1': 'grad_w', 'grad_b_b_pw1': 'grad_w', 'grad_b_conv_w': 'grad_w', 'grad_b_conv_b': 'grad_w', 'grad_b_ln_g': 'grad_w', 'grad_b_ln_b': 'grad_w', 'grad_b_w_pw2': 'grad_w', 'grad_b_b_pw2': 'grad_w', 'grad_f_w_up': 'grad_w', 'grad_f_conv_w': 'grad_w', 'grad_f_conv_b': 'grad_w', 'grad_f_w_gate': 'grad_w', 'grad_f_w_down': 'grad_w', 'delta_ada_w': 'delta_w', 'delta_ada_b': 'delta_w', 'delta_ln_tok_g': 'delta_w', 'delta_ln_tok_b': 'delta_w', 'delta_ln_ch_g': 'delta_w', 'delta_ln_ch_b': 'delta_w', 'delta_a_w_in': 'delta_w', 'delta_a_conv_w': 'delta_w', 'delta_a_conv_b': 'delta_w', 'delta_a_w_out': 'delta_w', 'delta_b_w_pw1': 'delta_w', 'delta_b_b_pw1': 'delta_w', 'delta_b_conv_w': 'delta_w', 'delta_b_conv_b': 'delta_w', 'delta_b_ln_g': 'delta_w', 'delta_b_ln_b': 'delta_w', 'delta_b_w_pw2': 'delta_w', 'delta_b_b_pw2': 'delta_w', 'delta_f_w_up': 'delta_w', 'delta_f_conv_w': 'delta_w', 'delta_f_conv_b': 'delta_w', 'delta_f_w_gate': 'delta_w', 'delta_f_w_down': 'delta_w', 'new_m_ada_w': 'new_m', 'new_m_ada_b': 'new_m', 'new_m_ln_tok_g': 'new_m', 'new_m_ln_tok_b': 'new_m', 'new_m_ln_ch_g': 'new_m', 'new_m_ln_ch_b': 'new_m', 'new_m_a_w_in': 'new_m', 'new_m_a_conv_w': 'new_m', 'new_m_a_conv_b': 'new_m', 'new_m_a_w_out': 'new_m', 'new_m_b_w_pw1': 'new_m', 'new_m_b_b_pw1': 'new_m', 'new_m_b_conv_w': 'new_m', 'new_m_b_conv_b': 'new_m', 'new_m_b_ln_g': 'new_m', 'new_m_b_ln_b': 'new_m', 'new_m_b_w_pw2': 'new_m', 'new_m_b_b_pw2': 'new_m', 'new_m_f_w_up': 'new_m', 'new_m_f_conv_w': 'new_m', 'new_m_f_conv_b': 'new_m', 'new_m_f_w_gate': 'new_m', 'new_m_f_w_down': 'new_m', 'new_v_ada_w': 'new_v', 'new_v_ada_b': 'new_v', 'new_v_ln_tok_g': 'new_v', 'new_v_ln_tok_b': 'new_v', 'new_v_ln_ch_g': 'new_v', 'new_v_ln_ch_b': 'new_v', 'new_v_a_w_in': 'new_v', 'new_v_a_conv_w': 'new_v', 'new_v_a_conv_b': 'new_v', 'new_v_a_w_out': 'new_v', 'new_v_b_w_pw1': 'new_v', 'new_v_b_b_pw1': 'new_v', 'new_v_b_conv_w': 'new_v', 'new_v_b_conv_b': 'new_v', 'new_v_b_ln_g': 'new_v', 'new_v_b_ln_b': 'new_v', 'new_v_b_w_pw2': 'new_v', 'new_v_b_b_pw2': 'new_v', 'new_v_f_w_up': 'new_v', 'new_v_f_conv_w': 'new_v', 'new_v_f_conv_b': 'new_v', 'new_v_f_w_gate': 'new_v', 'new_v_f_w_down': 'new_v'}


def _forward(args):
    return _fwd_reference(*[args[k] for k in FWD_PARAMS])


def _output_shape():
    out = _jax.eval_shape(lambda: _forward(_fwd_setup_inputs(0)))
    return out.shape, out.dtype

N_MICROBATCH = 1
ADAM_LR = 0.001
ADAM_B1 = 0.9
ADAM_B2 = 0.999
ADAM_EPS = 1e-08
ADAM_WD = 0.01
ADAM_STEP = 10
PER_EXAMPLE_BATCH_AXIS = {'x': 0, 'c': 0, 'loss_target': 0}
SHARED_INPUTS = []
_WEIGHT_DTYPES = {'ada_w': _jnp.float32, 'ada_b': _jnp.float32, 'ln_tok_g': _jnp.float32, 'ln_tok_b': _jnp.float32, 'ln_ch_g': _jnp.float32, 'ln_ch_b': _jnp.float32, 'a_w_in': _jnp.float32, 'a_conv_w': _jnp.float32, 'a_conv_b': _jnp.float32, 'a_w_out': _jnp.float32, 'b_w_pw1': _jnp.float32, 'b_b_pw1': _jnp.float32, 'b_conv_w': _jnp.float32, 'b_conv_b': _jnp.float32, 'b_ln_g': _jnp.float32, 'b_ln_b': _jnp.float32, 'b_w_pw2': _jnp.float32, 'b_b_pw2': _jnp.float32, 'f_w_up': _jnp.float32, 'f_conv_w': _jnp.float32, 'f_conv_b': _jnp.float32, 'f_w_gate': _jnp.float32, 'f_w_down': _jnp.float32}
MOMENT_SCALE = {'ada_w': 5.105525e-02, 'ada_b': 8.760364e-02, 'ln_tok_g': 9.419239e-01, 'ln_tok_b': 4.169765e-01, 'ln_ch_g': 2.296687e+01, 'ln_ch_b': 1.517240e+00, 'a_w_in': 8.529237e-02, 'a_conv_w': 8.622131e-02, 'a_conv_b': 5.145493e-02, 'a_w_out': 1.738682e-01, 'b_w_pw1': 1.978836e-02, 'b_b_pw1': 1.567643e-02, 'b_conv_w': 2.547433e-02, 'b_conv_b': 3.827476e-02, 'b_ln_g': 2.938879e-02, 'b_ln_b': 3.137155e-02, 'b_w_pw2': 5.319229e-02, 'b_b_pw2': 1.219318e-01, 'f_w_up': 2.737709e-02, 'f_conv_w': 2.723525e-02, 'f_conv_b': 2.038051e-02, 'f_w_gate': 2.661174e-02, 'f_w_down': 8.854781e-02}


def _to_microbatches(a, axis):
    t = _jnp.moveaxis(a, axis, 0)
    t = t.reshape((N_MICROBATCH, t.shape[0] // N_MICROBATCH) + t.shape[1:])
    return _jnp.moveaxis(t, 1, axis + 1)


def setup_inputs(seed: int = 0) -> dict:
    inp = _fwd_setup_inputs(seed)
    key = _jax.random.fold_in(_jax.random.key(seed), 7919)
    shape, _ = _output_shape()
    out = dict(inp)
    out["loss_target"] = _jax.random.normal(_jax.random.fold_in(key, 0), shape, _jnp.float32)
    for i, name in enumerate(TWIN_WEIGHTS):
        w = inp[name].astype(_jnp.float32)
        if MOMENT_SCALE is None:
            s = _jnp.sqrt(_jnp.mean(_jnp.square(w)) + 1e-30)
        else:
            s = MOMENT_SCALE[name]
        km, kv = _jax.random.split(_jax.random.fold_in(key, i + 1))
        out[name] = w
        out["m_" + name] = s * _jax.random.normal(km, w.shape, _jnp.float32)
        out["v_" + name] = (s * s) * _jax.random.uniform(kv, w.shape, _jnp.float32, 0.5, 1.5)
    if N_MICROBATCH > 1:
        for name, axis in PER_EXAMPLE_BATCH_AXIS.items():
            out[name] = _to_microbatches(out[name], axis)
    return {'x': out['x'], 'c': out['c'], 'ada_w': out['ada_w'], 'ada_b': out['ada_b'], 'ln_tok_g': out['ln_tok_g'], 'ln_tok_b': out['ln_tok_b'], 'ln_ch_g': out['ln_ch_g'], 'ln_ch_b': out['ln_ch_b'], 'a_w_in': out['a_w_in'], 'a_conv_w': out['a_conv_w'], 'a_conv_b': out['a_conv_b'], 'a_w_out': out['a_w_out'], 'b_w_pw1': out['b_w_pw1'], 'b_b_pw1': out['b_b_pw1'], 'b_conv_w': out['b_conv_w'], 'b_conv_b': out['b_conv_b'], 'b_ln_g': out['b_ln_g'], 'b_ln_b': out['b_ln_b'], 'b_w_pw2': out['b_w_pw2'], 'b_b_pw2': out['b_b_pw2'], 'f_w_up': out['f_w_up'], 'f_conv_w': out['f_conv_w'], 'f_conv_b': out['f_conv_b'], 'f_w_gate': out['f_w_gate'], 'f_w_down': out['f_w_down'], 'loss_target': out['loss_target'], 'm_ada_w': out['m_ada_w'], 'm_ada_b': out['m_ada_b'], 'm_ln_tok_g': out['m_ln_tok_g'], 'm_ln_tok_b': out['m_ln_tok_b'], 'm_ln_ch_g': out['m_ln_ch_g'], 'm_ln_ch_b': out['m_ln_ch_b'], 'm_a_w_in': out['m_a_w_in'], 'm_a_conv_w': out['m_a_conv_w'], 'm_a_conv_b': out['m_a_conv_b'], 'm_a_w_out': out['m_a_w_out'], 'm_b_w_pw1': out['m_b_w_pw1'], 'm_b_b_pw1': out['m_b_b_pw1'], 'm_b_conv_w': out['m_b_conv_w'], 'm_b_conv_b': out['m_b_conv_b'], 'm_b_ln_g': out['m_b_ln_g'], 'm_b_ln_b': out['m_b_ln_b'], 'm_b_w_pw2': out['m_b_w_pw2'], 'm_b_b_pw2': out['m_b_b_pw2'], 'm_f_w_up': out['m_f_w_up'], 'm_f_conv_w': out['m_f_conv_w'], 'm_f_conv_b': out['m_f_conv_b'], 'm_f_w_gate': out['m_f_w_gate'], 'm_f_w_down': out['m_f_w_down'], 'v_ada_w': out['v_ada_w'], 'v_ada_b': out['v_ada_b'], 'v_ln_tok_g': out['v_ln_tok_g'], 'v_ln_tok_b': out['v_ln_tok_b'], 'v_ln_ch_g': out['v_ln_ch_g'], 'v_ln_ch_b': out['v_ln_ch_b'], 'v_a_w_in': out['v_a_w_in'], 'v_a_conv_w': out['v_a_conv_w'], 'v_a_conv_b': out['v_a_conv_b'], 'v_a_w_out': out['v_a_w_out'], 'v_b_w_pw1': out['v_b_w_pw1'], 'v_b_b_pw1': out['v_b_b_pw1'], 'v_b_conv_w': out['v_b_conv_w'], 'v_b_conv_b': out['v_b_conv_b'], 'v_b_ln_g': out['v_b_ln_g'], 'v_b_ln_b': out['v_b_ln_b'], 'v_b_w_pw2': out['v_b_w_pw2'], 'v_b_b_pw2': out['v_b_b_pw2'], 'v_f_w_up': out['v_f_w_up'], 'v_f_conv_w': out['v_f_conv_w'], 'v_f_conv_b': out['v_f_conv_b'], 'v_f_w_gate': out['v_f_w_gate'], 'v_f_w_down': out['v_f_w_down']}


def _loss(weights, diff, rest, loss_target):
    with _jax.named_scope("forward"):
        args = {**rest, TWIN_DIFF_INPUT: diff, **{k: w.astype(_WEIGHT_DTYPES[k]) for k, w in weights.items()}}
        y = _forward(args)
    with _jax.named_scope("loss_head"):
        err = _jnp.square(y.astype(_jnp.float32) - loss_target)
        return 0.5 * _jnp.sum(_jnp.mean(err, axis=-1)) if err.ndim else 0.5 * err


def _adamw(w, g, m, v):
    m = ADAM_B1 * m + (1.0 - ADAM_B1) * g
    v = ADAM_B2 * v + (1.0 - ADAM_B2) * _jnp.square(g)
    m_hat = m / (1.0 - ADAM_B1 ** ADAM_STEP)
    v_hat = v / (1.0 - ADAM_B2 ** ADAM_STEP)
    delta = -ADAM_LR * (m_hat / (_jnp.sqrt(v_hat) + ADAM_EPS) + ADAM_WD * w)
    return delta, m, v


def reference(x, c, ada_w, ada_b, ln_tok_g, ln_tok_b, ln_ch_g, ln_ch_b, a_w_in, a_conv_w, a_conv_b, a_w_out, b_w_pw1, b_b_pw1, b_conv_w, b_conv_b, b_ln_g, b_ln_b, b_w_pw2, b_b_pw2, f_w_up, f_conv_w, f_conv_b, f_w_gate, f_w_down, loss_target, m_ada_w, m_ada_b, m_ln_tok_g, m_ln_tok_b, m_ln_ch_g, m_ln_ch_b, m_a_w_in, m_a_conv_w, m_a_conv_b, m_a_w_out, m_b_w_pw1, m_b_b_pw1, m_b_conv_w, m_b_conv_b, m_b_ln_g, m_b_ln_b, m_b_w_pw2, m_b_b_pw2, m_f_w_up, m_f_conv_w, m_f_conv_b, m_f_w_gate, m_f_w_down, v_ada_w, v_ada_b, v_ln_tok_g, v_ln_tok_b, v_ln_ch_g, v_ln_ch_b, v_a_w_in, v_a_conv_w, v_a_conv_b, v_a_w_out, v_b_w_pw1, v_b_b_pw1, v_b_conv_w, v_b_conv_b, v_b_ln_g, v_b_ln_b, v_b_w_pw2, v_b_b_pw2, v_f_w_up, v_f_conv_w, v_f_conv_b, v_f_w_gate, v_f_w_down):
    given = dict(x=x, c=c, ada_w=ada_w, ada_b=ada_b, ln_tok_g=ln_tok_g, ln_tok_b=ln_tok_b, ln_ch_g=ln_ch_g, ln_ch_b=ln_ch_b, a_w_in=a_w_in, a_conv_w=a_conv_w, a_conv_b=a_conv_b, a_w_out=a_w_out, b_w_pw1=b_w_pw1, b_b_pw1=b_b_pw1, b_conv_w=b_conv_w, b_conv_b=b_conv_b, b_ln_g=b_ln_g, b_ln_b=b_ln_b, b_w_pw2=b_w_pw2, b_b_pw2=b_b_pw2, f_w_up=f_w_up, f_conv_w=f_conv_w, f_conv_b=f_conv_b, f_w_gate=f_w_gate, f_w_down=f_w_down, loss_target=loss_target, m_ada_w=m_ada_w, m_ada_b=m_ada_b, m_ln_tok_g=m_ln_tok_g, m_ln_tok_b=m_ln_tok_b, m_ln_ch_g=m_ln_ch_g, m_ln_ch_b=m_ln_ch_b, m_a_w_in=m_a_w_in, m_a_conv_w=m_a_conv_w, m_a_conv_b=m_a_conv_b, m_a_w_out=m_a_w_out, m_b_w_pw1=m_b_w_pw1, m_b_b_pw1=m_b_b_pw1, m_b_conv_w=m_b_conv_w, m_b_conv_b=m_b_conv_b, m_b_ln_g=m_b_ln_g, m_b_ln_b=m_b_ln_b, m_b_w_pw2=m_b_w_pw2, m_b_b_pw2=m_b_b_pw2, m_f_w_up=m_f_w_up, m_f_conv_w=m_f_conv_w, m_f_conv_b=m_f_conv_b, m_f_w_gate=m_f_w_gate, m_f_w_down=m_f_w_down, v_ada_w=v_ada_w, v_ada_b=v_ada_b, v_ln_tok_g=v_ln_tok_g, v_ln_tok_b=v_ln_tok_b, v_ln_ch_g=v_ln_ch_g, v_ln_ch_b=v_ln_ch_b, v_a_w_in=v_a_w_in, v_a_conv_w=v_a_conv_w, v_a_conv_b=v_a_conv_b, v_a_w_out=v_a_w_out, v_b_w_pw1=v_b_w_pw1, v_b_b_pw1=v_b_b_pw1, v_b_conv_w=v_b_conv_w, v_b_conv_b=v_b_conv_b, v_b_ln_g=v_b_ln_g, v_b_ln_b=v_b_ln_b, v_b_w_pw2=v_b_w_pw2, v_b_b_pw2=v_b_b_pw2, v_f_w_up=v_f_w_up, v_f_conv_w=v_f_conv_w, v_f_conv_b=v_f_conv_b, v_f_w_gate=v_f_w_gate, v_f_w_down=v_f_w_down)
    weights = {n: given[n] for n in TWIN_WEIGHTS}
    shared = {n: given[n] for n in SHARED_INPUTS}
    per_example = {n: given[n] for n in ['x', 'c']}
    grad_fn = _jax.value_and_grad(_loss, argnums=(0, 1))

    def one_microbatch(ex, loss_target):
        ex = dict(ex)
        diff = ex.pop(TWIN_DIFF_INPUT)
        return grad_fn(weights, diff, {**shared, **ex}, loss_target)

    if N_MICROBATCH == 1:
        loss, (grad_w, grad_x) = one_microbatch(per_example, given["loss_target"])
    else:
        def body(carry, xs):
            loss_sum, grad_sum = carry
            l_k, (gw_k, gx_k) = one_microbatch(xs[0], xs[1])
            with _jax.named_scope("update"):
                return (loss_sum + l_k, _jax.tree.map(_jnp.add, grad_sum, gw_k)), gx_k

        init = (_jnp.zeros((), _jnp.float32), _jax.tree.map(_jnp.zeros_like, weights))
        (loss, grad_w), grad_x = _jax.lax.scan(body, init, (per_example, given["loss_target"]))
    with _jax.named_scope("update"):
        delta_w, new_m, new_v = {}, {}, {}
        for n in TWIN_WEIGHTS:
            delta_w[n], new_m[n], new_v[n] = _adamw(weights[n], grad_w[n], given["m_" + n], given["v_" + n])
    return (loss, grad_x, *[grad_w[n] for n in TWIN_WEIGHTS], *[delta_w[n] for n in TWIN_WEIGHTS],
            *[new_m[n] for n in TWIN_WEIGHTS], *[new_v[n] for n in TWIN_WEIGHTS])
```

```python
import functools

import jax
import jax.numpy as jnp
from jax import lax
from jax.experimental import pallas as pl
from jax.experimental.pallas import tpu as pltpu

NDEV = 8
MESH_AXES = ("x", "y", "c")
LANES = 128
SUBLANES = 8
VMEM_LIMIT = 56 * 1024 * 1024
LN_EPS = 1e-5
SHORT_PAD = 16
LONG_PAD = 32
CHUNK = 16
ADAM_LR, ADAM_B1, ADAM_B2, ADAM_EPS, ADAM_WD, ADAM_STEP = 0.001, 0.9, 0.999, 1e-08, 0.01, 10

F32 = jnp.float32
BF16 = jnp.bfloat16
MESH = pl.DeviceIdType.MESH
NT = (((1,), (1,)), ((), ()))
TN = (((0,), (0,)), ((), ()))


def _tile(n, target):
    best = None
    for t in range(SUBLANES, min(n, target) + 1, SUBLANES):
        if n % t == 0:
            best = t
    return best if best is not None else n


def _full(shape):
    nd = len(shape)
    return pl.BlockSpec(shape, lambda *_: (0,) * nd)


def _cp(*sem):
    return pltpu.CompilerParams(dimension_semantics=sem, vmem_limit_bytes=VMEM_LIMIT)


def _sigmoid(x):
    return 1.0 / (1.0 + jnp.exp(-x))


def _peer(x, y, c, d):
    return ((1 - x) if d & 4 else x, (1 - y) if d & 2 else y, (1 - c) if d & 1 else c)


def _lin(p):
    return 4 * p[0] + 2 * p[1] + p[2]


def _exchange(srcs, gather, name):
    n = len(srcs)

    def body(*refs):
        src_refs, out_refs = refs[:n], refs[n:2 * n]
        send_sems, recv_sems, local_sems = refs[2 * n:]
        x, y, c = (lax.axis_index(a) for a in MESH_AXES)
        me = _lin((x, y, c))
        local, sends, recvs = [], [], []
        for i in range(n):
            mine = src_refs[i] if gather else src_refs[i].at[me]
            cp = pltpu.make_async_copy(mine, out_refs[i].at[me], local_sems.at[i])
            cp.start()
            local.append(cp)
            for d in range(1, NDEV):
                peer = _peer(x, y, c, d)
                k = i * (NDEV - 1) + d - 1
                src = src_refs[i] if gather else src_refs[i].at[_lin(peer)]
                snd = pltpu.make_async_remote_copy(
                    src_ref=src, dst_ref=out_refs[i].at[me], send_sem=send_sems.at[k],
                    recv_sem=recv_sems.at[k], device_id=peer, device_id_type=MESH)
                snd.start()
                sends.append(snd)
                recvs.append(pltpu.make_async_remote_copy(
                    src_ref=src, dst_ref=out_refs[i].at[_lin(peer)], send_sem=send_sems.at[k],
                    recv_sem=recv_sems.at[k], device_id=peer, device_id_type=MESH))
        for snd, rcv in zip(sends, recvs):
            snd.wait_send()
            rcv.wait_recv()
        for cp in local:
            cp.wait()

    out_shape = [jax.ShapeDtypeStruct(((NDEV,) + s.shape) if gather else s.shape, s.dtype) for s in srcs]
    any_spec = pl.BlockSpec(memory_space=pl.ANY)
    return pl.pallas_call(
        body, name=name, out_shape=out_shape,
        in_specs=[any_spec] * n, out_specs=[any_spec] * n,
        scratch_shapes=[pltpu.SemaphoreType.DMA((n * (NDEV - 1),)),
                        pltpu.SemaphoreType.DMA((n * (NDEV - 1),)),
                        pltpu.SemaphoreType.DMA((n,))],
    )(*srcs)


def _mm_fwd(x, sc, sh, bias, wg, widxs, name):
    s_len, kdim = x.shape
    n = wg.shape[-1]
    ncol = NDEV * n
    tm = _tile(s_len, 256)
    nw = len(widxs)

    def body(x_ref, sc_ref, sh_ref, b_ref, *rest):
        w_refs, o_refs = rest[:nw], rest[nw:]
        h = (x_ref[...] * (1.0 + sc_ref[...]) + sh_ref[...]).astype(BF16)
        for w_ref, o_ref in zip(w_refs, o_refs):
            for k in range(NDEV):
                cols = slice(k * n, (k + 1) * n)
                o_ref[:, cols] = jnp.dot(h, w_ref[k], preferred_element_type=F32) + b_ref[:, cols]

    w_specs = [pl.BlockSpec((NDEV, None, kdim, n), functools.partial(lambda i, w: (0, w, 0, 0), w=w))
               for w in widxs]
    return pl.pallas_call(
        body, name=name, grid=(s_len // tm,),
        in_specs=[pl.BlockSpec((tm, kdim), lambda i: (i, 0)), _full((1, kdim)), _full((1, kdim)),
                  _full((1, ncol))] + w_specs,
        out_specs=[pl.BlockSpec((tm, ncol), lambda i: (i, 0))] * nw,
        out_shape=[jax.ShapeDtypeStruct((s_len, ncol), F32)] * nw,
        compiler_params=_cp("parallel"),
    )(x, sc, sh, bias, *([wg] * nw))


def _mm_ln(a, wg, r, ridx, xres, gate, gam, bet, bias, alpha, name):
    s_len = a.shape[0]
    d = wg.shape[-1]
    tm = _tile(s_len, 256)

    def body(a_ref, w_ref, x_ref, g_ref, gam_ref, bet_ref, b_ref, y_ref, xo_ref, xh_ref, rs_ref):
        acc = None
        for k in range(NDEV):
            p = jnp.dot(a_ref[:, k * r:(k + 1) * r], w_ref[k], preferred_element_type=F32)
            acc = p if acc is None else acc + p
        y = acc + b_ref[...]
        z = alpha * x_ref[...] + g_ref[...] * y
        mu = jnp.mean(z, axis=-1, keepdims=True)
        zc = z - mu
        var = jnp.mean(zc * zc, axis=-1, keepdims=True)
        rstd = lax.rsqrt(var + LN_EPS)
        xh = zc * rstd
        y_ref[...] = y
        xh_ref[...] = xh
        rs_ref[...] = rstd
        xo_ref[...] = xh * gam_ref[...] + bet_ref[...]

    row = pl.BlockSpec((tm, d), lambda i: (i, 0))
    vec = _full((1, d))
    return pl.pallas_call(
        body, name=name, grid=(s_len // tm,),
        in_specs=[pl.BlockSpec((tm, NDEV * r), lambda i: (i, 0)),
                  pl.BlockSpec((NDEV, r, d), lambda i: (0, ridx, 0)), row, vec, vec, vec, vec],
        out_specs=[row, row, row, pl.BlockSpec((tm, 1), lambda i: (i, 0))],
        out_shape=[jax.ShapeDtypeStruct((s_len, d), F32)] * 3 + [jax.ShapeDtypeStruct((s_len, 1), F32)],
        compiler_params=_cp("parallel"),
    )(a, wg, xres, gate, gam, bet, bias)


def _ln_bwd(dxo, xh, rstd, gam, y, gate, alpha, name):
    s_len, d = dxo.shape
    tm = _tile(s_len, 256)

    def body(d_ref, xh_ref, rs_ref, gam_ref, y_ref, g_ref, dy_ref, dres_ref, acc_ref):
        @pl.when(pl.program_id(0) == 0)
        def _():
            acc_ref[...] = jnp.zeros_like(acc_ref)

        dxo_t = d_ref[...]
        xh_t = xh_ref[...]
        dxh = dxo_t * gam_ref[...]
        m1 = jnp.mean(dxh, axis=-1, keepdims=True)
        m2 = jnp.mean(dxh * xh_t, axis=-1, keepdims=True)
        dz = rs_ref[...] * (dxh - m1 - xh_t * m2)
        dy = g_ref[...] * dz
        dy_ref[...] = dy.astype(BF16)
        dres_ref[...] = alpha * dz
        acc_ref[0:1, :] += jnp.sum(dxo_t * xh_t, axis=0, keepdims=True)
        acc_ref[1:2, :] += jnp.sum(dxo_t, axis=0, keepdims=True)
        acc_ref[2:3, :] += jnp.sum(dz * y_ref[...], axis=0, keepdims=True)
        acc_ref[3:4, :] += jnp.sum(dy, axis=0, keepdims=True)

    row = pl.BlockSpec((tm, d), lambda i: (i, 0))
    vec = _full((1, d))
    return pl.pallas_call(
        body, name=name, grid=(s_len // tm,),
        in_specs=[row, row, pl.BlockSpec((tm, 1), lambda i: (i, 0)), vec, row, vec],
        out_specs=[row, row, _full((SUBLANES, d))],
        out_shape=[jax.ShapeDtypeStruct((s_len, d), BF16), jax.ShapeDtypeStruct((s_len, d), F32),
                   jax.ShapeDtypeStruct((SUBLANES, d), F32)],
        compiler_params=_cp("arbitrary"),
    )(dxo, xh, rstd, gam, y, gate)


def _mm_nt_row(dy, wg, r, ridx, name):
    s_len, d = dy.shape
    tm = _tile(s_len, 256)

    def body(dy_ref, w_ref, o_ref):
        g = dy_ref[...]
        for k in range(NDEV):
            o_ref[:, k * r:(k + 1) * r] = lax.dot_general(g, w_ref[k], NT, preferred_element_type=F32)

    return pl.pallas_call(
        body, name=name, grid=(s_len // tm,),
        in_specs=[pl.BlockSpec((tm, d), lambda i: (i, 0)), pl.BlockSpec((NDEV, r, d), lambda i: (0, ridx, 0))],
        out_specs=pl.BlockSpec((tm, NDEV * r), lambda i: (i, 0)),
        out_shape=jax.ShapeDtypeStruct((s_len, NDEV * r), F32),
        compiler_params=_cp("parallel"),
    )(dy, wg)


def _mm_nt_mod(dos, wg, widxs, xin, sc, dres, name):
    s_len, kdim = xin.shape
    n = wg.shape[-1]
    tm = _tile(s_len, 256)
    nw = len(widxs)

    def body(*refs):
        do_refs, w_refs = refs[:nw], refs[nw:2 * nw]
        x_ref, sc_ref, dres_ref, dx_ref, acc_ref = refs[2 * nw:]

        @pl.when(pl.program_id(0) == 0)
        def _():
            acc_ref[...] = jnp.zeros_like(acc_ref)

        dh = None
        for do_ref, w_ref in zip(do_refs, w_refs):
            for k in range(NDEV):
                p = lax.dot_general(do_ref[:, k * n:(k + 1) * n], w_ref[k], NT, preferred_element_type=F32)
                dh = p if dh is None else dh + p
        dx_ref[...] = dh * (1.0 + sc_ref[...]) + dres_ref[...]
        acc_ref[0:1, :] += jnp.sum(dh * x_ref[...], axis=0, keepdims=True)
        acc_ref[1:2, :] += jnp.sum(dh, axis=0, keepdims=True)

    row = pl.BlockSpec((tm, kdim), lambda i: (i, 0))
    w_specs = [pl.BlockSpec((NDEV, None, kdim, n), functools.partial(lambda i, w: (0, w, 0, 0), w=w))
               for w in widxs]
    return pl.pallas_call(
        body, name=name, grid=(s_len // tm,),
        in_specs=[pl.BlockSpec((tm, NDEV * n), lambda i: (i, 0))] * nw + w_specs + [row, _full((1, kdim)), row],
        out_specs=[row, _full((SUBLANES, kdim))],
        out_shape=[jax.ShapeDtypeStruct((s_len, kdim), F32), jax.ShapeDtypeStruct((SUBLANES, kdim), F32)],
        compiler_params=_cp("arbitrary"),
    )(*dos, *([wg] * nw), xin, sc, dres)


def _mm_tn_col(x, sc, sh, do, name):
    s_len, kdim = x.shape
    n = do.shape[1] // NDEV
    ts = _tile(s_len, 512)

    def body(x_ref, sc_ref, sh_ref, do_ref, o_ref):
        @pl.when(pl.program_id(0) == 0)
        def _():
            o_ref[...] = jnp.zeros_like(o_ref)

        h = (x_ref[...] * (1.0 + sc_ref[...]) + sh_ref[...]).astype(BF16)
        for k in range(NDEV):
            o_ref[k] += lax.dot_general(h, do_ref[:, k * n:(k + 1) * n], TN, preferred_element_type=F32)

    return pl.pallas_call(
        body, name=name, grid=(s_len // ts,),
        in_specs=[pl.BlockSpec((ts, kdim), lambda i: (i, 0)), _full((1, kdim)), _full((1, kdim)),
                  pl.BlockSpec((ts, NDEV * n), lambda i: (i, 0))],
        out_specs=_full((NDEV, kdim, n)),
        out_shape=jax.ShapeDtypeStruct((NDEV, kdim, n), F32),
        compiler_params=_cp("arbitrary"),
    )(x, sc, sh, do)


def _mm_tn_row(a, dy, r, name):
    s_len, d = dy.shape
    ts = _tile(s_len, 512)

    def body(a_ref, dy_ref, o_ref):
        @pl.when(pl.program_id(0) == 0)
        def _():
            o_ref[...] = jnp.zeros_like(o_ref)

        g = dy_ref[...]
        for k in range(NDEV):
            o_ref[k] += lax.dot_general(a_ref[:, k * r:(k + 1) * r], g, TN, preferred_element_type=F32)

    return pl.pallas_call(
        body, name=name, grid=(s_len // ts,),
        in_specs=[pl.BlockSpec((ts, NDEV * r), lambda i: (i, 0)), pl.BlockSpec((ts, d), lambda i: (i, 0))],
        out_specs=_full((NDEV, r, d)),
        out_shape=jax.ShapeDtypeStruct((NDEV, r, d), F32),
        compiler_params=_cp("arbitrary"),
    )(a, dy)


def _prev_spec(ts, pad, cb, col):
    return pl.BlockSpec((pad, cb), lambda *g: (jnp.maximum(g[-1] * (ts // pad) - 1, 0), col(g)))


def _next_spec(ts, pad, cb, col, s_len):
    return pl.BlockSpec((pad, cb), lambda *g: (jnp.minimum((g[-1] + 1) * (ts // pad), s_len // pad - 1), col(g)))


def _conv_fwd_rows(buf_ref, w_ref, b_ref, ktaps, pad, r0, rows):
    acc = None
    for j in range(ktaps):
        term = w_ref[ktaps - 1 - j:ktaps - j, :] * buf_ref[pad - j + r0:pad - j + r0 + rows, :]
        acc = term if acc is None else acc + term
    return acc + b_ref[...]


def _conv_bwd_rows(dbuf_ref, x_rows, w_ref, dwacc_ref, ktaps, r0, rows):
    acc = None
    for j in range(ktaps):
        sl = dbuf_ref[j + r0:j + r0 + rows, :]
        term = w_ref[ktaps - 1 - j:ktaps - j, :] * sl
        acc = term if acc is None else acc + term
        prod = x_rows * sl
        fold = prod[0:SUBLANES]
        for q in range(1, rows // SUBLANES):
            fold = fold + prod[q * SUBLANES:(q + 1) * SUBLANES]
        tap = ktaps - 1 - j
        dwacc_ref[tap * SUBLANES:(tap + 1) * SUBLANES, :] += fold
    return acc


def _flush_dw(dwacc_ref, dw_ref, ktaps):
    for tap in range(ktaps):
        dw_ref[tap:tap + 1, :] = jnp.sum(dwacc_ref[tap * SUBLANES:(tap + 1) * SUBLANES, :], axis=0, keepdims=True)


def _gateconv_fwd(bcv, cw, cb, name):
    s_len, d3 = bcv.shape
    d = d3 // 3
    ktaps = cw.shape[0]
    pad = SHORT_PAD
    ts = _tile(s_len, 256)

    def body(gb_ref, gc_ref, v_ref, gcp_ref, vp_ref, w_ref, b_ref, o_ref, pbuf):
        s = pl.program_id(0)
        pbuf[0:pad, :] = jnp.where(s > 0, gcp_ref[...] * vp_ref[...], 0.0)
        pbuf[pad:pad + ts, :] = gc_ref[...] * v_ref[...]
        for r0 in range(0, ts, CHUNK):
            q = _conv_fwd_rows(pbuf, w_ref, b_ref, ktaps, pad, r0, CHUNK)
            o_ref[r0:r0 + CHUNK, :] = (gb_ref[r0:r0 + CHUNK, :] * q).astype(BF16)

    def cur(part):
        return pl.BlockSpec((ts, d), lambda s: (s, part))

    return pl.pallas_call(
        body, name=name, grid=(s_len // ts,),
        in_specs=[cur(0), cur(1), cur(2),
                  _prev_spec(ts, pad, d, lambda g: 1), _prev_spec(ts, pad, d, lambda g: 2),
                  _full((ktaps, d)), _full((1, d))],
        out_specs=pl.BlockSpec((ts, d), lambda s: (s, 0)),
        out_shape=jax.ShapeDtypeStruct((s_len, d), BF16),
        scratch_shapes=[pltpu.VMEM((pad + ts, d), F32)],
        compiler_params=_cp("parallel"),
    )(bcv, bcv, bcv, bcv, bcv, cw, cb)


def _gateconv_bwd(bcv, dy0, cw, cb, name):
    s_len, d3 = bcv.shape
    d = d3 // 3
    ktaps = cw.shape[0]
    pad = SHORT_PAD
    ts = _tile(s_len, 256)
    nsteps = s_len // ts

    def body(gb_ref, gc_ref, v_ref, gcp_ref, vp_ref, gbn_ref, dy_ref, dyn_ref, w_ref, b_ref,
             o_ref, dw_ref, db_ref, pbuf, dqbuf, dwacc):
        s = pl.program_id(0)

        @pl.when(s == 0)
        def _():
            dwacc[...] = jnp.zeros_like(dwacc)
            db_ref[...] = jnp.zeros_like(db_ref)

        pbuf[0:pad, :] = jnp.where(s > 0, gcp_ref[...] * vp_ref[...], 0.0)
        pbuf[pad:pad + ts, :] = gc_ref[...] * v_ref[...]
        dq = dy_ref[...] * gb_ref[...]
        dqbuf[0:ts, :] = dq
        dqbuf[ts:ts + pad, :] = jnp.where(s < nsteps - 1, dyn_ref[...] * gbn_ref[...], 0.0)
        db_ref[...] += jnp.sum(dq, axis=0, keepdims=True)
        for r0 in range(0, ts, CHUNK):
            rows = slice(r0, r0 + CHUNK)
            q = _conv_fwd_rows(pbuf, w_ref, b_ref, ktaps, pad, r0, CHUNK)
            o_ref[rows, 0:d] = (dy_ref[rows, :] * q).astype(BF16)
            dp = _conv_bwd_rows(dqbuf, pbuf[pad + r0:pad + r0 + CHUNK, :], w_ref, dwacc, ktaps, r0, CHUNK)
            o_ref[rows, d:2 * d] = (dp * v_ref[rows, :]).astype(BF16)
            o_ref[rows, 2 * d:3 * d] = (dp * gc_ref[rows, :]).astype(BF16)

        @pl.when(s == nsteps - 1)
        def _():
            _flush_dw(dwacc, dw_ref, ktaps)

    def cur(part):
        return pl.BlockSpec((ts, d), lambda s: (s, part))

    return pl.pallas_call(
        body, name=name, grid=(nsteps,),
        in_specs=[cur(0), cur(1), cur(2),
                  _prev_spec(ts, pad, d, lambda g: 1), _prev_spec(ts, pad, d, lambda g: 2),
                  _next_spec(ts, pad, d, lambda g: 0, s_len),
                  cur(0), _next_spec(ts, pad, d, lambda g: 0, s_len),
                  _full((ktaps, d)), _full((1, d))],
        out_specs=[pl.BlockSpec((ts, d3), lambda s: (s, 0)), _full((ktaps, d)), _full((1, d))],
        out_shape=[jax.ShapeDtypeStruct((s_len, d3), BF16), jax.ShapeDtypeStruct((ktaps, d), F32),
                   jax.ShapeDtypeStruct((1, d), F32)],
        scratch_shapes=[pltpu.VMEM((pad + ts, d), F32), pltpu.VMEM((ts + pad, d), F32),
                        pltpu.VMEM((ktaps * SUBLANES, d), F32)],
        compiler_params=_cp("arbitrary"),
    )(bcv, bcv, bcv, bcv, bcv, bcv, dy0, dy0, cw, cb)


def _ffn_mid_fwd(u0, vg, cw, cb, name):
    s_len, f = u0.shape
    ktaps = cw.shape[0]
    pad = SHORT_PAD
    ts = _tile(s_len, 256)
    cbk = 1024 if f % 1024 == 0 else f

    def body(u_ref, up_ref, vg_ref, w_ref, b_ref, o_ref, ubuf):
        s = pl.program_id(1)
        ubuf[0:pad, :] = jnp.where(s > 0, up_ref[...], 0.0)
        ubuf[pad:pad + ts, :] = u_ref[...]
        for r0 in range(0, ts, CHUNK):
            u = _conv_fwd_rows(ubuf, w_ref, b_ref, ktaps, pad, r0, CHUNK)
            o_ref[r0:r0 + CHUNK, :] = (u * _sigmoid(u) * vg_ref[r0:r0 + CHUNK, :]).astype(BF16)

    cur = pl.BlockSpec((ts, cbk), lambda c, s: (s, c))
    return pl.pallas_call(
        body, name=name, grid=(f // cbk, s_len // ts),
        in_specs=[cur, _prev_spec(ts, pad, cbk, lambda g: g[0]), cur,
                  pl.BlockSpec((ktaps, cbk), lambda c, s: (0, c)), pl.BlockSpec((1, cbk), lambda c, s: (0, c))],
        out_specs=cur,
        out_shape=jax.ShapeDtypeStruct((s_len, f), BF16),
        scratch_shapes=[pltpu.VMEM((pad + ts, cbk), F32)],
        compiler_params=_cp("parallel", "parallel"),
    )(u0, u0, vg, cw, cb)


def _ffn_mid_bwd(u0, vg, dt, cw, cb, name):
    s_len, f = u0.shape
    ktaps = cw.shape[0]
    pad = SHORT_PAD
    ts = _tile(s_len, 256)
    nsteps = s_len // ts
    cbk = 1024 if f % 1024 == 0 else f

    def body(u_ref, up_ref, un_ref, vg_ref, vgn_ref, dt_ref, dtn_ref, w_ref, b_ref,
             du0_ref, dvg_ref, dw_ref, db_ref, ubuf, dubuf, dwacc):
        s = pl.program_id(1)

        @pl.when(s == 0)
        def _():
            dwacc[...] = jnp.zeros_like(dwacc)
            db_ref[...] = jnp.zeros_like(db_ref)

        ubuf[0:pad, :] = jnp.where(s > 0, up_ref[...], 0.0)
        ubuf[pad:pad + ts, :] = u_ref[...]
        ubuf[pad + ts:pad + ts + pad, :] = un_ref[...]
        last = s == nsteps - 1
        for r0 in range(0, ts + pad, CHUNK):
            u = _conv_fwd_rows(ubuf, w_ref, b_ref, ktaps, pad, r0, CHUNK)
            sg = _sigmoid(u)
            if r0 < ts:
                rows = slice(r0, r0 + CHUNK)
                dtr, vgr = dt_ref[rows, :], vg_ref[rows, :]
                dvg_ref[rows, :] = (dtr * u * sg).astype(BF16)
            else:
                rows = slice(r0 - ts, r0 - ts + CHUNK)
                dtr, vgr = jnp.where(last, 0.0, dtn_ref[rows, :]), vgn_ref[rows, :]
            dubuf[r0:r0 + CHUNK, :] = dtr * vgr * (sg * (1.0 + u * (1.0 - sg)))
        db_ref[...] += jnp.sum(dubuf[0:ts, :], axis=0, keepdims=True)
        for r0 in range(0, ts, CHUNK):
            du0 = _conv_bwd_rows(dubuf, u_ref[r0:r0 + CHUNK, :], w_ref, dwacc, ktaps, r0, CHUNK)
            du0_ref[r0:r0 + CHUNK, :] = du0.astype(BF16)

        @pl.when(last)
        def _():
            _flush_dw(dwacc, dw_ref, ktaps)

    cur = pl.BlockSpec((ts, cbk), lambda c, s: (s, c))
    prv = _prev_spec(ts, pad, cbk, lambda g: g[0])
    nxt = _next_spec(ts, pad, cbk, lambda g: g[0], s_len)
    return pl.pallas_call(
        body, name=name, grid=(f // cbk, nsteps),
        in_specs=[cur, prv, nxt, cur, nxt, cur, nxt,
                  pl.BlockSpec((ktaps, cbk), lambda c, s: (0, c)), pl.BlockSpec((1, cbk), lambda c, s: (0, c))],
        out_specs=[cur, cur, pl.BlockSpec((ktaps, cbk), lambda c, s: (0, c)),
                   pl.BlockSpec((1, cbk), lambda c, s: (0, c))],
        out_shape=[jax.ShapeDtypeStruct((s_len, f), BF16), jax.ShapeDtypeStruct((s_len, f), BF16),
                   jax.ShapeDtypeStruct((ktaps, f), F32), jax.ShapeDtypeStruct((1, f), F32)],
        scratch_shapes=[pltpu.VMEM((pad + ts + pad, cbk), F32), pltpu.VMEM((ts + pad, cbk), F32),
                        pltpu.VMEM((ktaps * SUBLANES, cbk), F32)],
        compiler_params=_cp("parallel", "arbitrary"),
    )(u0, u0, u0, vg, vg, dt, dt, cw, cb)


def _b_mid_fwd(ub, cw, cb, lng, lnb, name):
    s_len, d2 = ub.shape
    d = d2 // 2
    ktaps = cw.shape[0]
    pad = LONG_PAD
    ts = _tile(s_len, 256)

    def body(a_ref, g_ref, ap_ref, gp_ref, w_ref, b_ref, lng_ref, lnb_ref, a2_ref, a4_ref, abuf):
        s = pl.program_id(0)
        abuf[0:pad, :] = jnp.where(s > 0, ap_ref[...] * _sigmoid(gp_ref[...]), 0.0)
        abuf[pad:pad + ts, :] = a_ref[...] * _sigmoid(g_ref[...])
        for r0 in range(0, ts, CHUNK):
            a2_ref[r0:r0 + CHUNK, :] = _conv_fwd_rows(abuf, w_ref, b_ref, ktaps, pad, r0, CHUNK)
        a2 = a2_ref[...]
        mu = jnp.mean(a2, axis=-1, keepdims=True)
        ac = a2 - mu
        var = jnp.mean(ac * ac, axis=-1, keepdims=True)
        a3 = ac * lax.rsqrt(var + LN_EPS) * lng_ref[...] + lnb_ref[...]
        a4_ref[...] = (a3 * _sigmoid(a3)).astype(BF16)

    def cur(part):
        return pl.BlockSpec((ts, d), lambda s: (s, part))

    vec = _full((1, d))
    return pl.pallas_call(
        body, name=name, grid=(s_len // ts,),
        in_specs=[cur(0), cur(1), _prev_spec(ts, pad, d, lambda g: 0), _prev_spec(ts, pad, d, lambda g: 1),
                  _full((ktaps, d)), vec, vec, vec],
        out_specs=[cur(0), cur(0)],
        out_shape=[jax.ShapeDtypeStruct((s_len, d), F32), jax.ShapeDtypeStruct((s_len, d), BF16)],
        scratch_shapes=[pltpu.VMEM((pad + ts, d), F32)],
        compiler_params=_cp("parallel"),
    )(ub, ub, ub, ub, cw, cb, lng, lnb)


def _b_mid_bwd(ub, a2, da4, cw, lng, lnb, name):
    s_len, d2 = ub.shape
    d = d2 // 2
    ktaps = cw.shape[0]
    pad = LONG_PAD
    ts = _tile(s_len, 256)
    nsteps = s_len // ts

    def body(a_ref, g_ref, a2_ref, a2n_ref, da4_ref, da4n_ref, w_ref, lng_ref, lnb_ref,
             du_ref, dw_ref, db_ref, dlng_ref, dlnb_ref, dbias_ref, dabuf, dwacc):
        s = pl.program_id(0)
        last = s == nsteps - 1

        @pl.when(s == 0)
        def _():
            dwacc[...] = jnp.zeros_like(dwacc)
            for ref in (db_ref, dlng_ref, dlnb_ref, dbias_ref):
                ref[...] = jnp.zeros_like(ref)

        def ln_silu_bwd(a2_t, da4_t):
            mu = jnp.mean(a2_t, axis=-1, keepdims=True)
            ac = a2_t - mu
            var = jnp.mean(ac * ac, axis=-1, keepdims=True)
            rstd = lax.rsqrt(var + LN_EPS)
            ah = ac * rstd
            a3 = ah * lng_ref[...] + lnb_ref[...]
            sg = _sigmoid(a3)
            da3 = da4_t * (sg * (1.0 + a3 * (1.0 - sg)))
            dah = da3 * lng_ref[...]
            m1 = jnp.mean(dah, axis=-1, keepdims=True)
            m2 = jnp.mean(dah * ah, axis=-1, keepdims=True)
            return rstd * (dah - m1 - ah * m2), da3, ah

        da2, da3, ah = ln_silu_bwd(a2_ref[...], da4_ref[...])
        dabuf[0:ts, :] = da2
        dlng_ref[...] += jnp.sum(da3 * ah, axis=0, keepdims=True)
        dlnb_ref[...] += jnp.sum(da3, axis=0, keepdims=True)
        db_ref[...] += jnp.sum(da2, axis=0, keepdims=True)
        da2n, _, _ = ln_silu_bwd(a2n_ref[...], jnp.where(last, 0.0, da4n_ref[...]))
        dabuf[ts:ts + pad, :] = da2n
        for r0 in range(0, ts, CHUNK):
            rows = slice(r0, r0 + CHUNK)
            a_r, g_r = a_ref[rows, :], g_ref[rows, :]
            sg = _sigmoid(g_r)
            da1 = _conv_bwd_rows(dabuf, a_r * sg, w_ref, dwacc, ktaps, r0, CHUNK)
            da = da1 * sg
            dg = da1 * a_r * sg * (1.0 - sg)
            du_ref[rows, 0:d] = da.astype(BF16)
            du_ref[rows, d:2 * d] = dg.astype(BF16)
            dbias_ref[:, 0:d] += jnp.sum(da, axis=0, keepdims=True)
            dbias_ref[:, d:2 * d] += jnp.sum(dg, axis=0, keepdims=True)

        @pl.when(last)
        def _():
            _flush_dw(dwacc, dw_ref, ktaps)

    def cur(part):
        return pl.BlockSpec((ts, d), lambda s: (s, part))

    vec = _full((1, d))
    nxt = _next_spec(ts, pad, d, lambda g: 0, s_len)
    return pl.pallas_call(
        body, name=name, grid=(nsteps,),
        in_specs=[cur(0), cur(1), cur(0), nxt, cur(0), nxt, _full((ktaps, d)), vec, vec],
        out_specs=[pl.BlockSpec((ts, d2), lambda s: (s, 0)), _full((ktaps, d)), vec, vec, vec, _full((1, d2))],
        out_shape=[jax.ShapeDtypeStruct((s_len, d2), BF16), jax.ShapeDtypeStruct((ktaps, d), F32),
                   jax.ShapeDtypeStruct((1, d), F32), jax.ShapeDtypeStruct((1, d), F32),
                   jax.ShapeDtypeStruct((1, d), F32), jax.ShapeDtypeStruct((1, d2), F32)],
        scratch_shapes=[pltpu.VMEM((ts + pad, d), F32), pltpu.VMEM((ktaps * SUBLANES, d), F32)],
        compiler_params=_cp("arbitrary"),
    )(ub, ub, a2, a2, da4, da4, cw, lng, lnb)


def _loss_head(xo, tgt, name):
    s_len, d = xo.shape
    tm = _tile(s_len, 512)

    def body(x_ref, t_ref, d_ref, l_ref):
        @pl.when(pl.program_id(0) == 0)
        def _():
            l_ref[...] = jnp.zeros_like(l_ref)

        e = x_ref[...] - t_ref[...]
        d_ref[...] = e * (1.0 / d)
        per_row = jnp.sum(e * e, axis=-1, keepdims=True) * (1.0 / d)
        l_ref[...] += 0.5 * jnp.sum(per_row, axis=0, keepdims=True)

    row = pl.BlockSpec((tm, d), lambda i: (i, 0))
    return pl.pallas_call(
        body, name=name, grid=(s_len // tm,),
        in_specs=[row, row], out_specs=[row, _full((1, LANES))],
        out_shape=[jax.ShapeDtypeStruct((s_len, d), F32), jax.ShapeDtypeStruct((1, LANES), F32)],
        compiler_params=_cp("arbitrary"),
    )(xo, tgt)


def _ada_fwd(c_all, ada_w, ada_b_loc, name):
    depth, d, n = ada_w.shape

    def body(c_ref, w_ref, b_ref, o_ref):
        c = c_ref[...]
        act = c * _sigmoid(c)
        o_ref[...] = jnp.dot(act, w_ref[...], preferred_element_type=F32,
                             precision=lax.Precision.HIGHEST) + b_ref[...]

    return pl.pallas_call(
        body, name=name, grid=(depth,),
        in_specs=[_full((NDEV, d)), pl.BlockSpec((None, d, n), lambda i: (i, 0, 0)),
                  pl.BlockSpec((None, 1, n), lambda i: (i, 0, 0))],
        out_specs=pl.BlockSpec((None, NDEV, n), lambda i: (i, 0, 0)),
        out_shape=jax.ShapeDtypeStruct((depth, NDEV, n), F32),
        compiler_params=_cp("parallel"),
    )(c_all, ada_w, ada_b_loc.reshape(depth, 1, n))


def _ada_bwd(c_all_t, dmod_cols, name):
    depth, _, n = dmod_cols.shape
    d = c_all_t.shape[0]

    def body(ct_ref, dm_ref, o_ref):
        ct = ct_ref[...]
        act = ct * _sigmoid(ct)
        acc = None
        for b in range(NDEV):
            term = act[:, b:b + 1] * dm_ref[b:b + 1, :]
            acc = term if acc is None else acc + term
        o_ref[...] = acc

    return pl.pallas_call(
        body, name=name, grid=(depth,),
        in_specs=[_full((d, NDEV)), pl.BlockSpec((None, NDEV, n), lambda i: (i, 0, 0))],
        out_specs=pl.BlockSpec((None, d, n), lambda i: (i, 0, 0)),
        out_shape=jax.ShapeDtypeStruct((depth, d, n), F32),
        compiler_params=_cp("parallel"),
    )(c_all_t, dmod_cols)


def _sum_parts(parts, name):
    _, rows, lanes = parts.shape

    def body(p_ref, o_ref):
        acc = p_ref[0]
        for k in range(1, NDEV):
            acc = acc + p_ref[k]
        o_ref[...] = acc

    return pl.pallas_call(
        body, name=name, in_specs=[_full(parts.shape)], out_specs=_full((rows, lanes)), grid=(1,),
        out_shape=jax.ShapeDtypeStruct((rows, lanes), F32), compiler_params=_cp("arbitrary"),
    )(parts)


def _adamw(w, gparts, m, v, name):
    nparts, rows, cols = gparts.shape
    tr = _tile(rows, 128)

    def body(w_ref, g_ref, m_ref, v_ref, go_ref, d_ref, mo_ref, vo_ref):
        g = g_ref[0]
        for p in range(1, nparts):
            g = g + g_ref[p]
        m1 = ADAM_B1 * m_ref[...] + (1.0 - ADAM_B1) * g
        v1 = ADAM_B2 * v_ref[...] + (1.0 - ADAM_B2) * (g * g)
        m_hat = m1 / (1.0 - ADAM_B1 ** ADAM_STEP)
        v_hat = v1 / (1.0 - ADAM_B2 ** ADAM_STEP)
        go_ref[...] = g
        mo_ref[...] = m1
        vo_ref[...] = v1
        d_ref[...] = -ADAM_LR * (m_hat / (jnp.sqrt(v_hat) + ADAM_EPS) + ADAM_WD * w_ref[...])

    blk = pl.BlockSpec((tr, cols), lambda i: (i, 0))
    return pl.pallas_call(
        body, name=name, grid=(rows // tr,),
        in_specs=[blk, pl.BlockSpec((nparts, tr, cols), lambda i: (0, i, 0)), blk, blk],
        out_specs=[blk] * 4, out_shape=[jax.ShapeDtypeStruct((rows, cols), F32)] * 4,
        compiler_params=_cp("parallel"),
    )(w, gparts, m, v)


def _pack(pieces):
    flat = jnp.concatenate([p.reshape(-1) for p in pieces])
    unit = SUBLANES * LANES
    padded = -(-flat.shape[0] // unit) * unit
    return jnp.pad(flat, (0, padded - flat.shape[0])).reshape(padded // LANES, LANES)


def _unpack(packed, shapes, lead=()):
    flat = packed.reshape(lead + (-1,))
    out, off = [], 0
    for s in shapes:
        size = 1
        for dim in s:
            size *= dim
        out.append(flat[..., off:off + size].reshape(lead + tuple(s)))
        off += size
    return out


def _pad_last(a, n):
    return jnp.pad(a, [(0, 0)] * (a.ndim - 1) + [(0, n - a.shape[-1])])


def kernel(x, c, ada_w, ada_b, ln_tok_g, ln_tok_b, ln_ch_g, ln_ch_b, a_w_in, a_conv_w, a_conv_b, a_w_out, b_w_pw1, b_b_pw1, b_conv_w, b_conv_b, b_ln_g, b_ln_b, b_w_pw2, b_b_pw2, f_w_up, f_conv_w, f_conv_b, f_w_gate, f_w_down, loss_target, m_ada_w, m_ada_b, m_ln_tok_g, m_ln_tok_b, m_ln_ch_g, m_ln_ch_b, m_a_w_in, m_a_conv_w, m_a_conv_b, m_a_w_out, m_b_w_pw1, m_b_b_pw1, m_b_conv_w, m_b_conv_b, m_b_ln_g, m_b_ln_b, m_b_w_pw2, m_b_b_pw2, m_f_w_up, m_f_conv_w, m_f_conv_b, m_f_w_gate, m_f_w_down, v_ada_w, v_ada_b, v_ln_tok_g, v_ln_tok_b, v_ln_ch_g, v_ln_ch_b, v_a_w_in, v_a_conv_w, v_a_conv_b, v_a_w_out, v_b_w_pw1, v_b_b_pw1, v_b_conv_w, v_b_conv_b, v_b_ln_g, v_b_ln_b, v_b_w_pw2, v_b_b_pw2, v_f_w_up, v_f_conv_w, v_f_conv_b, v_f_w_gate, v_f_w_down):
    weights = dict(ada_w=ada_w, ada_b=ada_b, ln_tok_g=ln_tok_g, ln_tok_b=ln_tok_b, ln_ch_g=ln_ch_g, ln_ch_b=ln_ch_b, a_w_in=a_w_in, a_conv_w=a_conv_w, a_conv_b=a_conv_b, a_w_out=a_w_out, b_w_pw1=b_w_pw1, b_b_pw1=b_b_pw1, b_conv_w=b_conv_w, b_conv_b=b_conv_b, b_ln_g=b_ln_g, b_ln_b=b_ln_b, b_w_pw2=b_w_pw2, b_b_pw2=b_b_pw2, f_w_up=f_w_up, f_conv_w=f_conv_w, f_conv_b=f_conv_b, f_w_gate=f_w_gate, f_w_down=f_w_down)
    mom_m = dict(ada_w=m_ada_w, ada_b=m_ada_b, ln_tok_g=m_ln_tok_g, ln_tok_b=m_ln_tok_b, ln_ch_g=m_ln_ch_g, ln_ch_b=m_ln_ch_b, a_w_in=m_a_w_in, a_conv_w=m_a_conv_w, a_conv_b=m_a_conv_b, a_w_out=m_a_w_out, b_w_pw1=m_b_w_pw1, b_b_pw1=m_b_b_pw1, b_conv_w=m_b_conv_w, b_conv_b=m_b_conv_b, b_ln_g=m_b_ln_g, b_ln_b=m_b_ln_b, b_w_pw2=m_b_w_pw2, b_b_pw2=m_b_b_pw2, f_w_up=m_f_w_up, f_conv_w=m_f_conv_w, f_conv_b=m_f_conv_b, f_w_gate=m_f_w_gate, f_w_down=m_f_w_down)
    mom_v = dict(ada_w=v_ada_w, ada_b=v_ada_b, ln_tok_g=v_ln_tok_g, ln_tok_b=v_ln_tok_b, ln_ch_g=v_ln_ch_g, ln_ch_b=v_ln_ch_b, a_w_in=v_a_w_in, a_conv_w=v_a_conv_w, a_conv_b=v_a_conv_b, a_w_out=v_a_w_out, b_w_pw1=v_b_w_pw1, b_b_pw1=v_b_b_pw1, b_conv_w=v_b_conv_w, b_conv_b=v_b_conv_b, b_ln_g=v_b_ln_g, b_ln_b=v_b_ln_b, b_w_pw2=v_b_w_pw2, b_b_pw2=v_b_b_pw2, f_w_up=v_f_w_up, f_conv_w=v_f_conv_w, f_conv_b=v_f_conv_b, f_w_gate=v_f_w_gate, f_w_down=v_f_w_down)
    names = list(weights)

    depth, d, n_ada = ada_w.shape
    assert depth == 2 and a_w_in.shape[0] == 1 and b_w_pw1.shape[0] == 1
    s_len = x.shape[1]
    f_loc = f_w_up.shape[-1]
    f_pad = -(-f_loc // LANES) * LANES
    f_all = NDEV * f_pad
    d_loc = d // NDEV
    ka, kb, kf = a_conv_w.shape[1], b_conv_w.shape[1], f_conv_w.shape[1]
    alpha = (2.0 * depth) ** 0.25
    assert a_w_in.shape[-1] == f_pad and f_pad % d_loc == 0
    me = 4 * lax.axis_index("x") + 2 * lax.axis_index("y") + lax.axis_index("c")

    small_shapes = [(d,), (ka, d_loc), (2 * d_loc,), (kb, d_loc), (d_loc,), (d_loc,), (d_loc,), (d_loc,),
                    (depth, kf, f_pad)]
    small_loc = _pack([c[0], a_conv_w[0], b_b_pw1[0], b_conv_w[0], b_conv_b[0], b_ln_g[0], b_ln_b[0],
                       b_b_pw2[0], _pad_last(f_conv_w, f_pad)])
    col_loc = jnp.stack([a_w_in[0], _pad_last(f_w_up[0], f_pad), _pad_last(f_w_up[1], f_pad),
                         _pad_last(f_w_gate[0], f_pad), _pad_last(f_w_gate[1], f_pad)]).astype(BF16)
    pw1_loc = b_w_pw1.astype(BF16)
    down_pad = jnp.pad(f_w_down, ((0, 0), (0, f_pad - f_loc), (0, 0)))
    row_loc = jnp.concatenate([down_pad[0], down_pad[1], a_w_out[0], b_w_pw2[0]], axis=0).astype(BF16)
    g_small, g_col, g_pw1, g_row = _exchange([small_loc, col_loc, pw1_loc, row_loc], True, "gather_weights")
    ridx_out = 2 * f_pad // d_loc
    ridx_pw2 = ridx_out + 1

    (c_all, acw_g, bb1_g, bcw_g, bcb_g, blg_g, blb_g, bb2_g, fcw_g) = _unpack(g_small, small_shapes, (NDEV,))
    a_cw = acw_g.transpose(1, 0, 2).reshape(ka, d)
    b_cw = bcw_g.transpose(1, 0, 2).reshape(kb, d)
    b_b1 = bb1_g.reshape(1, 2 * d)
    b_cb, b_lg, b_lb, b_b2 = (t.reshape(1, d) for t in (bcb_g, blg_g, blb_g, bb2_g))
    f_cw = fcw_g.transpose(1, 2, 0, 3).reshape(depth, kf, f_all)
    f_cb = _pad_last(f_conv_b.reshape(depth, NDEV, f_loc), f_pad).reshape(depth, 1, f_all)

    ada_b_loc = lax.dynamic_slice(ada_b, (0, me * n_ada), (depth, n_ada))
    mod_part = _ada_fwd(c_all, ada_w, ada_b_loc, "ada_fwd")
    mod_g, = _exchange([mod_part.reshape(depth * NDEV, n_ada)], True, "gather_mod")
    mod_all = mod_g.reshape(NDEV, depth, NDEV, n_ada).transpose(1, 2, 0, 3).reshape(depth, NDEV, 6 * d)
    mod = lax.dynamic_slice(mod_all, (0, me, 0), (depth, 1, 6 * d))[:, 0]

    def mod_rows(i):
        return [mod[i:i + 1, j * d:(j + 1) * d] for j in range(6)]

    zeros_d = jnp.zeros((1, d), F32)
    zeros_f = jnp.zeros((1, f_all), F32)
    x0 = x[0]

    sh_t0, sc_t0, g_t0, sh_c0, sc_c0, g_c0 = mod_rows(0)
    sh_t1, sc_t1, g_t1, sh_c1, sc_c1, g_c1 = mod_rows(1)

    bcv, = _mm_fwd(x0, sc_t0, sh_t0, jnp.zeros((1, 3 * d), F32), g_col, (0,), "a_in_fwd")
    y0 = _gateconv_fwd(bcv, a_cw, a_conv_b, "a_conv_fwd")
    y_a, x1, xh1, rs1 = _mm_ln(y0, g_row, d_loc, ridx_out, x0, g_t0, ln_tok_g[0:1], ln_tok_b[0:1], zeros_d,
                               alpha, "a_out_ln_fwd")

    def ffn_fwd(xin, sc, sh, gate, gam, bet, layer, tag):
        u0, vg = _mm_fwd(xin, sc, sh, zeros_f, g_col, (1 + layer, 3 + layer), "f_upgate_fwd" + tag)
        t = _ffn_mid_fwd(u0, vg, f_cw[layer], f_cb[layer], "f_mid_fwd" + tag)
        y, xo, xh, rs = _mm_ln(t, g_row, f_pad, layer, xin, gate, gam, bet, zeros_d, alpha, "f_down_ln_fwd" + tag)
        return u0, vg, t, y, xo, xh, rs

    u0_0, vg_0, t_0, y_f0, x2, xh2, rs2 = ffn_fwd(x1, sc_c0, sh_c0, g_c0, ln_ch_g[0:1], ln_ch_b[0:1], 0, "0")

    ub, = _mm_fwd(x2, sc_t1, sh_t1, b_b1, g_pw1, (0,), "b_pw1_fwd")
    a2, a4 = _b_mid_fwd(ub, b_cw, b_cb, b_lg, b_lb, "b_mid_fwd")
    y_b, x3, xh3, rs3 = _mm_ln(a4, g_row, d_loc, ridx_pw2, x2, g_t1, ln_tok_g[1:2], ln_tok_b[1:2], b_b2,
                               alpha, "b_pw2_ln_fwd")
    u0_1, vg_1, t_1, y_f1, x4, xh4, rs4 = ffn_fwd(x3, sc_c1, sh_c1, g_c1, ln_ch_g[1:2], ln_ch_b[1:2], 1, "1")

    dx4, loss_part = _loss_head(x4, loss_target[0], "loss_head")
    loss = lax.psum(loss_part[0, 0], MESH_AXES)

    def ffn_bwd(dxo, xin, sc, sh, gate, gam, u0, vg, t, y, xh, rs, layer, tag):
        dy, dres, acc = _ln_bwd(dxo, xh, rs, gam, y, gate, alpha, "f_ln_bwd" + tag)
        dt = _mm_nt_row(dy, g_row, f_pad, layer, "f_down_dx" + tag)
        dw_down = _mm_tn_row(t, dy, f_pad, "f_down_dw" + tag)
        du0, dvg, dcw, dcb = _ffn_mid_bwd(u0, vg, dt, f_cw[layer], f_cb[layer], "f_mid_bwd" + tag)
        dxin, acc2 = _mm_nt_mod([du0, dvg], g_col, (1 + layer, 3 + layer), xin, sc, dres, "f_upgate_dx" + tag)
        dw_up = _mm_tn_col(xin, sc, sh, du0, "f_up_dw" + tag)
        dw_gate = _mm_tn_col(xin, sc, sh, dvg, "f_gate_dw" + tag)
        return dxin, acc, acc2, dw_up, dw_gate, dw_down, dcw, dcb

    dx3, accf1, acc2f1, dw_up1, dw_gate1, dw_down1, dfcw1, dfcb1 = ffn_bwd(
        dx4, x3, sc_c1, sh_c1, g_c1, ln_ch_g[1:2], u0_1, vg_1, t_1, y_f1, xh4, rs4, 1, "1")

    dy, dres, accb = _ln_bwd(dx3, xh3, rs3, ln_tok_g[1:2], y_b, g_t1, alpha, "b_ln_bwd")
    da4 = _mm_nt_row(dy, g_row, d_loc, ridx_pw2, "b_pw2_dx")
    dw_pw2 = _mm_tn_row(a4, dy, d_loc, "b_pw2_dw")
    du, dbcw, dbcb, dblg, dblb, dbb1 = _b_mid_bwd(ub, a2, da4, b_cw, b_lg, b_lb, "b_mid_bwd")
    dx2, acc2b = _mm_nt_mod([du], g_pw1, (0,), x2, sc_t1, dres, "b_pw1_dx")
    dw_pw1 = _mm_tn_col(x2, sc_t1, sh_t1, du, "b_pw1_dw")

    dx1, accf0, acc2f0, dw_up0, dw_gate0, dw_down0, dfcw0, dfcb0 = ffn_bwd(
        dx2, x1, sc_c0, sh_c0, g_c0, ln_ch_g[0:1], u0_0, vg_0, t_0, y_f0, xh2, rs2, 0, "0")

    dy, dres, acca = _ln_bwd(dx1, xh1, rs1, ln_tok_g[0:1], y_a, g_t0, alpha, "a_ln_bwd")
    dy0 = _mm_nt_row(dy, g_row, d_loc, ridx_out, "a_out_dx")
    dw_out = _mm_tn_row(y0, dy, d_loc, "a_out_dw")
    dbcv, dacw, dacb = _gateconv_bwd(bcv, dy0, a_cw, a_conv_b, "a_conv_bwd")
    dx0, acc2a = _mm_nt_mod([dbcv], g_col, (0,), x0, sc_t0, dres, "a_in_dx")
    dw_in = _mm_tn_col(x0, sc_t0, sh_t0, dbcv, "a_in_dw")

    def dmod_row(acc2_t, acc_t, acc2_c, acc_c):
        return jnp.concatenate([acc2_t[1], acc2_t[0], acc_t[2], acc2_c[1], acc2_c[0], acc_c[2]])

    dmod = jnp.stack([dmod_row(acc2a, acca, acc2f0, accf0), dmod_row(acc2b, accb, acc2f1, accf1)])

    def unpad_f(a):
        return a.reshape(a.shape[:-1] + (NDEV, f_pad))[..., :f_loc].reshape(a.shape[:-1] + (NDEV * f_loc,))

    small_grads = [
        dmod,
        jnp.stack([acca[0], accb[0]]), jnp.stack([acca[1], accb[1]]),
        jnp.stack([accf0[0], accf1[0]]), jnp.stack([accf0[1], accf1[1]]),
        dacb,
        unpad_f(jnp.concatenate([dfcb0, dfcb1], axis=0)),
        dacw, dbb1, dbcw, dbcb, dblg, dblb, accb[3:4],
        jnp.stack([dfcw0, dfcw1]),
    ]
    small_grad_shapes = [tuple(g.shape) for g in small_grads]
    sg_all, = _exchange([_pack(small_grads)], True, "gather_small_grads")
    sg_sum = _sum_parts(sg_all, "sum_small_grads")
    (g_ada_b, g_ltg, g_ltb, g_lcg, g_lcb, g_acb, g_fcb, g_acw, g_bb1, g_bcw, g_bcb, g_blg, g_blb, g_bb2,
     g_fcw) = _unpack(sg_sum, small_grad_shapes)

    def my_cols(a, width):
        return lax.dynamic_slice_in_dim(a, me * width, width, axis=a.ndim - 1)

    g_fcw_loc = my_cols(g_fcw, f_pad)[..., :f_loc]
    small = dict(
        ada_b=g_ada_b, ln_tok_g=g_ltg, ln_tok_b=g_ltb, ln_ch_g=g_lcg, ln_ch_b=g_lcb, a_conv_b=g_acb, f_conv_b=g_fcb,
        a_conv_w=my_cols(g_acw, d_loc)[None], b_b_pw1=my_cols(g_bb1, 2 * d_loc), b_conv_w=my_cols(g_bcw, d_loc)[None],
        b_conv_b=my_cols(g_bcb, d_loc), b_ln_g=my_cols(g_blg, d_loc), b_ln_b=my_cols(g_blb, d_loc),
        b_b_pw2=my_cols(g_bb2, d_loc), f_conv_w=g_fcw_loc)

    dmod_all = sg_all.reshape(NDEV, -1)[:, :depth * 6 * d].reshape(NDEV, depth, 6 * d)
    dmod_cols = my_cols(dmod_all, n_ada).transpose(1, 0, 2)
    g_ada_w = _ada_bwd(c_all.T, dmod_cols, "ada_bwd")

    big_parts = dict(
        a_w_in=dw_in[:, None],
        a_w_out=dw_out[:, None],
        b_w_pw1=dw_pw1[:, None],
        b_w_pw2=dw_pw2[:, None],
        f_w_up=jnp.stack([dw_up0[..., :f_loc], dw_up1[..., :f_loc]], axis=1),
        f_w_gate=jnp.stack([dw_gate0[..., :f_loc], dw_gate1[..., :f_loc]], axis=1),
        f_w_down=jnp.stack([dw_down0[:, :f_loc], dw_down1[:, :f_loc]], axis=1),
    )
    big_names = list(big_parts)
    big_recv = _exchange([big_parts[k] for k in big_names], False, "scatter_grads")
    gparts = dict(zip(big_names, big_recv))
    gparts["ada_w"] = g_ada_w[None]
    for k, g in small.items():
        gparts[k] = g[None]

    grads, deltas, new_m, new_v = {}, {}, {}, {}
    for k in names:
        w = weights[k]
        cols = w.shape[-1]
        outs = _adamw(w.reshape(-1, cols), gparts[k].reshape(gparts[k].shape[0], -1, cols),
                      mom_m[k].reshape(-1, cols), mom_v[k].reshape(-1, cols), "adamw_" + k)
        grads[k], deltas[k], new_m[k], new_v[k] = (o.reshape(w.shape) for o in outs)

    return (loss, dx0[None], *[grads[k] for k in names], *[deltas[k] for k in names],
            *[new_m[k] for k in names], *[new_v[k] for k in names])
```

```python
import functools

import jax
import jax.numpy as jnp
from jax import lax
from jax.experimental import pallas as pl
from jax.experimental.pallas import tpu as pltpu

NDEV = 8
MESH_AXES = ("x", "y", "c")
LANES = 128
SUBLANES = 8
VMEM_LIMIT = 56 * 1024 * 1024
LN_EPS = 1e-5
SHORT_PAD = 16
LONG_PAD = 32
CHUNK = 16
ADAM_LR, ADAM_B1, ADAM_B2, ADAM_EPS, ADAM_WD, ADAM_STEP = 0.001, 0.9, 0.999, 1e-08, 0.01, 10

F32 = jnp.float32
BF16 = jnp.bfloat16
MESH = pl.DeviceIdType.MESH
NT = (((1,), (1,)), ((), ()))
TN = (((0,), (0,)), ((), ()))


def _tile(n, target, mult=SUBLANES):
    best = None
    for t in range(mult, min(n, target) + 1, mult):
        if n % t == 0:
            best = t
    return best if best is not None else n


def _full(shape):
    nd = len(shape)
    return pl.BlockSpec(shape, lambda *_: (0,) * nd)


def _cp(*sem):
    return pltpu.CompilerParams(dimension_semantics=sem, vmem_limit_bytes=VMEM_LIMIT)


def _sigmoid(x):
    return 1.0 / (1.0 + jnp.exp(-x))


def _peer(x, y, c, d):
    return ((1 - x) if d & 4 else x, (1 - y) if d & 2 else y, (1 - c) if d & 1 else c)


def _lin(p):
    return 4 * p[0] + 2 * p[1] + p[2]


def _remote_copies(src_refs, land_refs, send_sems, recv_sems, gather):
    x, y, c = (lax.axis_index(a) for a in MESH_AXES)
    me = _lin((x, y, c))
    sends, recvs = [], []
    for i, (src_ref, land_ref) in enumerate(zip(src_refs, land_refs)):
        for d in range(1, NDEV):
            peer = _peer(x, y, c, d)
            k = i * (NDEV - 1) + d - 1
            src = src_ref if gather else src_ref.at[_lin(peer)]
            for slot, out in ((me, sends), (_lin(peer), recvs)):
                out.append(pltpu.make_async_remote_copy(
                    src_ref=src, dst_ref=land_ref.at[slot], send_sem=send_sems.at[k], recv_sem=recv_sems.at[k],
                    device_id=peer, device_id_type=MESH))
    return sends, recvs


def _exchange(srcs, gather, name):
    n = len(srcs)

    def body(*refs):
        src_refs, out_refs = refs[:n], refs[n:2 * n]
        send_sems, recv_sems, local_sems = refs[2 * n:]
        me = _lin(tuple(lax.axis_index(a) for a in MESH_AXES))
        local = []
        for i in range(n):
            mine = src_refs[i] if gather else src_refs[i].at[me]
            cp = pltpu.make_async_copy(mine, out_refs[i].at[me], local_sems.at[i])
            cp.start()
            local.append(cp)
        sends, recvs = _remote_copies(src_refs, out_refs, send_sems, recv_sems, gather)
        for snd in sends:
            snd.start()
        for snd, rcv in zip(sends, recvs):
            snd.wait_send()
            rcv.wait_recv()
        for cp in local:
            cp.wait()

    out_shape = [jax.ShapeDtypeStruct(((NDEV,) + s.shape) if gather else s.shape, s.dtype) for s in srcs]
    any_spec = pl.BlockSpec(memory_space=pl.ANY)
    return pl.pallas_call(
        body, name=name, out_shape=out_shape,
        in_specs=[any_spec] * n, out_specs=[any_spec] * n,
        scratch_shapes=[pltpu.SemaphoreType.DMA((n * (NDEV - 1),)),
                        pltpu.SemaphoreType.DMA((n * (NDEV - 1),)),
                        pltpu.SemaphoreType.DMA((n,))],
    )(*srcs)


HBM_SPEC = pl.BlockSpec(memory_space=pltpu.HBM)
SEM_SPEC = pl.BlockSpec(memory_space=pltpu.SEMAPHORE)
SIDE_EFFECT = pltpu.SideEffectType.DATAFLOW_SIDE_EFFECTING


def _exchange_start(srcs, gather, name):
    n = len(srcs)
    me = _lin(tuple(lax.axis_index(a) for a in MESH_AXES))
    lands = []
    for s in srcs:
        own = s if gather else lax.dynamic_index_in_dim(s, me, 0, keepdims=False)
        shape = ((NDEV,) + s.shape) if gather else s.shape
        lands.append(lax.dynamic_update_index_in_dim(lax.empty(shape, s.dtype), own, me, 0))

    def body(*refs):
        src_refs, land_refs = refs[:n], refs[n:2 * n]
        send_sems, recv_sems, token = refs[2 * n], refs[2 * n + 1], refs[-1]
        sends, _ = _remote_copies(src_refs, land_refs, send_sems, recv_sems, gather)
        for snd in sends:
            snd.start()
        token[...] = jnp.zeros_like(token)

    operands = [pltpu.with_memory_space_constraint(a, pltpu.HBM) for a in list(srcs) + lands]
    nsem = n * (NDEV - 1)
    return pl.pallas_call(
        body, name=name,
        out_shape=(pltpu.SemaphoreType.DMA((nsem,)), pltpu.SemaphoreType.DMA((nsem,)),
                   *[pltpu.HBM(a.shape, a.dtype) for a in operands],
                   jax.ShapeDtypeStruct((SUBLANES, LANES), F32)),
        in_specs=[HBM_SPEC] * (2 * n),
        out_specs=(SEM_SPEC, SEM_SPEC, *([HBM_SPEC] * (2 * n)), pl.BlockSpec(memory_space=pltpu.VMEM)),
        input_output_aliases={i: 2 + i for i in range(2 * n)},
        compiler_params=pltpu.CompilerParams(has_side_effects=SIDE_EFFECT),
    )(*operands)


def _exchange_wait(handle, after, gather, name):
    send_sems, recv_sems, *thru = handle[:-1]
    n = len(thru) // 2

    def body(*refs):
        src_refs, land_refs = refs[:n], refs[n:2 * n]
        sends, recvs = _remote_copies(src_refs, land_refs, refs[2 * n], refs[2 * n + 1], gather)
        for snd, rcv in zip(sends, recvs):
            snd.wait_send()
            rcv.wait_recv()

    outs = pl.pallas_call(
        body, name=name, out_shape=tuple(pltpu.HBM(a.shape, a.dtype) for a in thru),
        in_specs=[HBM_SPEC] * (2 * n) + [SEM_SPEC, SEM_SPEC, pl.BlockSpec(memory_space=pl.ANY)],
        out_specs=[HBM_SPEC] * (2 * n), input_output_aliases={i: i for i in range(2 * n)},
        compiler_params=pltpu.CompilerParams(has_side_effects=SIDE_EFFECT),
    )(*thru, send_sems, recv_sems, after)
    return outs[n:]


def _after(value, dep):
    return lax.optimization_barrier((value, dep))[0]


def _mm_fwd(x, sc, sh, bias, wg, widxs, name):
    s_len, kdim = x.shape
    n = wg.shape[-1]
    ncol = NDEV * n
    tm = _tile(s_len, 256)
    nw = len(widxs)

    def body(x_ref, sc_ref, sh_ref, b_ref, *rest):
        w_refs, o_refs = rest[:nw], rest[nw:]
        h = (x_ref[...] * (1.0 + sc_ref[...]) + sh_ref[...]).astype(BF16)
        for w_ref, o_ref in zip(w_refs, o_refs):
            for k in range(NDEV):
                cols = slice(k * n, (k + 1) * n)
                o_ref[:, cols] = jnp.dot(h, w_ref[k], preferred_element_type=F32) + b_ref[:, cols]

    w_specs = [pl.BlockSpec((NDEV, None, kdim, n), functools.partial(lambda i, w: (0, w, 0, 0), w=w))
               for w in widxs]
    return pl.pallas_call(
        body, name=name, grid=(s_len // tm,),
        in_specs=[pl.BlockSpec((tm, kdim), lambda i: (i, 0)), _full((1, kdim)), _full((1, kdim)),
                  _full((1, ncol))] + w_specs,
        out_specs=[pl.BlockSpec((tm, ncol), lambda i: (i, 0))] * nw,
        out_shape=[jax.ShapeDtypeStruct((s_len, ncol), F32)] * nw,
        compiler_params=_cp("parallel"),
    )(x, sc, sh, bias, *([wg] * nw))


def _mm_ln(a, wg, r, ridx, xres, gate, gam, bet, bias, alpha, name):
    s_len = a.shape[0]
    d = wg.shape[-1]
    tm = _tile(s_len, 256)

    def body(a_ref, w_ref, x_ref, g_ref, gam_ref, bet_ref, b_ref, y_ref, xo_ref, xh_ref, rs_ref):
        acc = None
        for k in range(NDEV):
            p = jnp.dot(a_ref[:, k * r:(k + 1) * r], w_ref[k], preferred_element_type=F32)
            acc = p if acc is None else acc + p
        y = acc + b_ref[...]
        z = alpha * x_ref[...] + g_ref[...] * y
        mu = jnp.mean(z, axis=-1, keepdims=True)
        zc = z - mu
        var = jnp.mean(zc * zc, axis=-1, keepdims=True)
        rstd = lax.rsqrt(var + LN_EPS)
        xh = zc * rstd
        y_ref[...] = y
        xh_ref[...] = xh
        rs_ref[...] = rstd
        xo_ref[...] = xh * gam_ref[...] + bet_ref[...]

    row = pl.BlockSpec((tm, d), lambda i: (i, 0))
    vec = _full((1, d))
    return pl.pallas_call(
        body, name=name, grid=(s_len // tm,),
        in_specs=[pl.BlockSpec((tm, NDEV * r), lambda i: (i, 0)),
                  pl.BlockSpec((NDEV, r, d), lambda i: (0, ridx, 0)), row, vec, vec, vec, vec],
        out_specs=[row, row, row, pl.BlockSpec((tm, 1), lambda i: (i, 0))],
        out_shape=[jax.ShapeDtypeStruct((s_len, d), F32)] * 3 + [jax.ShapeDtypeStruct((s_len, 1), F32)],
        compiler_params=_cp("parallel"),
    )(a, wg, xres, gate, gam, bet, bias)


def _ln_bwd(dxo, xh, rstd, gam, y, gate, alpha, name):
    s_len, d = dxo.shape
    tm = _tile(s_len, 256)

    def body(d_ref, xh_ref, rs_ref, gam_ref, y_ref, g_ref, dy_ref, dres_ref, acc_ref):
        @pl.when(pl.program_id(0) == 0)
        def _():
            acc_ref[...] = jnp.zeros_like(acc_ref)

        dxo_t = d_ref[...]
        xh_t = xh_ref[...]
        dxh = dxo_t * gam_ref[...]
        m1 = jnp.mean(dxh, axis=-1, keepdims=True)
        m2 = jnp.mean(dxh * xh_t, axis=-1, keepdims=True)
        dz = rs_ref[...] * (dxh - m1 - xh_t * m2)
        dy = g_ref[...] * dz
        dy_ref[...] = dy.astype(BF16)
        dres_ref[...] = alpha * dz
        acc_ref[0:1, :] += jnp.sum(dxo_t * xh_t, axis=0, keepdims=True)
        acc_ref[1:2, :] += jnp.sum(dxo_t, axis=0, keepdims=True)
        acc_ref[2:3, :] += jnp.sum(dz * y_ref[...], axis=0, keepdims=True)
        acc_ref[3:4, :] += jnp.sum(dy, axis=0, keepdims=True)

    row = pl.BlockSpec((tm, d), lambda i: (i, 0))
    vec = _full((1, d))
    return pl.pallas_call(
        body, name=name, grid=(s_len // tm,),
        in_specs=[row, row, pl.BlockSpec((tm, 1), lambda i: (i, 0)), vec, row, vec],
        out_specs=[row, row, _full((SUBLANES, d))],
        out_shape=[jax.ShapeDtypeStruct((s_len, d), BF16), jax.ShapeDtypeStruct((s_len, d), F32),
                   jax.ShapeDtypeStruct((SUBLANES, d), F32)],
        compiler_params=_cp("arbitrary"),
    )(dxo, xh, rstd, gam, y, gate)


def _mm_nt_row(dy, wg, r, ridx, name):
    s_len, d = dy.shape
    tm = _tile(s_len, 256)

    def body(dy_ref, w_ref, o_ref):
        g = dy_ref[...]
        for k in range(NDEV):
            o_ref[:, k * r:(k + 1) * r] = lax.dot_general(g, w_ref[k], NT, preferred_element_type=F32)

    return pl.pallas_call(
        body, name=name, grid=(s_len // tm,),
        in_specs=[pl.BlockSpec((tm, d), lambda i: (i, 0)), pl.BlockSpec((NDEV, r, d), lambda i: (0, ridx, 0))],
        out_specs=pl.BlockSpec((tm, NDEV * r), lambda i: (i, 0)),
        out_shape=jax.ShapeDtypeStruct((s_len, NDEV * r), F32),
        compiler_params=_cp("parallel"),
    )(dy, wg)


def _mm_nt_mod(dos, wg, widxs, xin, sc, dres, name):
    s_len, kdim = xin.shape
    n = wg.shape[-1]
    tm = _tile(s_len, 256)
    nw = len(widxs)

    def body(*refs):
        do_refs, w_refs = refs[:nw], refs[nw:2 * nw]
        x_ref, sc_ref, dres_ref, dx_ref, acc_ref = refs[2 * nw:]

        @pl.when(pl.program_id(0) == 0)
        def _():
            acc_ref[...] = jnp.zeros_like(acc_ref)

        dh = None
        for do_ref, w_ref in zip(do_refs, w_refs):
            for k in range(NDEV):
                p = lax.dot_general(do_ref[:, k * n:(k + 1) * n], w_ref[k], NT, preferred_element_type=F32)
                dh = p if dh is None else dh + p
        dx_ref[...] = dh * (1.0 + sc_ref[...]) + dres_ref[...]
        acc_ref[0:1, :] += jnp.sum(dh * x_ref[...], axis=0, keepdims=True)
        acc_ref[1:2, :] += jnp.sum(dh, axis=0, keepdims=True)

    row = pl.BlockSpec((tm, kdim), lambda i: (i, 0))
    w_specs = [pl.BlockSpec((NDEV, None, kdim, n), functools.partial(lambda i, w: (0, w, 0, 0), w=w))
               for w in widxs]
    return pl.pallas_call(
        body, name=name, grid=(s_len // tm,),
        in_specs=[pl.BlockSpec((tm, NDEV * n), lambda i: (i, 0))] * nw + w_specs + [row, _full((1, kdim)), row],
        out_specs=[row, _full((SUBLANES, kdim))],
        out_shape=[jax.ShapeDtypeStruct((s_len, kdim), F32), jax.ShapeDtypeStruct((SUBLANES, kdim), F32)],
        compiler_params=_cp("arbitrary"),
    )(*dos, *([wg] * nw), xin, sc, dres)


def _mm_tn_col(x, sc, sh, do, name):
    s_len, kdim = x.shape
    n = do.shape[1] // NDEV
    ts = _tile(s_len, 512)
    nsteps = s_len // ts

    def body(x_ref, sc_ref, sh_ref, do_ref, o_ref, acc_ref):
        @pl.when(pl.program_id(0) == 0)
        def _():
            acc_ref[...] = jnp.zeros_like(acc_ref)

        h = (x_ref[...] * (1.0 + sc_ref[...]) + sh_ref[...]).astype(BF16)
        for k in range(NDEV):
            acc_ref[k] += lax.dot_general(h, do_ref[:, k * n:(k + 1) * n], TN, preferred_element_type=F32)

        @pl.when(pl.program_id(0) == nsteps - 1)
        def _():
            o_ref[...] = acc_ref[...].astype(BF16)

    return pl.pallas_call(
        body, name=name, grid=(nsteps,),
        in_specs=[pl.BlockSpec((ts, kdim), lambda i: (i, 0)), _full((1, kdim)), _full((1, kdim)),
                  pl.BlockSpec((ts, NDEV * n), lambda i: (i, 0))],
        out_specs=_full((NDEV, kdim, n)),
        out_shape=jax.ShapeDtypeStruct((NDEV, kdim, n), BF16),
        scratch_shapes=[pltpu.VMEM((NDEV, kdim, n), F32)],
        compiler_params=_cp("arbitrary"),
    )(x, sc, sh, do)


def _mm_tn_row(a, dy, r, name):
    s_len, d = dy.shape
    ts = _tile(s_len, 512)
    nsteps = s_len // ts

    def body(a_ref, dy_ref, o_ref, acc_ref):
        @pl.when(pl.program_id(0) == 0)
        def _():
            acc_ref[...] = jnp.zeros_like(acc_ref)

        g = dy_ref[...]
        for k in range(NDEV):
            acc_ref[k] += lax.dot_general(a_ref[:, k * r:(k + 1) * r], g, TN, preferred_element_type=F32)

        @pl.when(pl.program_id(0) == nsteps - 1)
        def _():
            o_ref[...] = acc_ref[...].astype(BF16)

    return pl.pallas_call(
        body, name=name, grid=(nsteps,),
        in_specs=[pl.BlockSpec((ts, NDEV * r), lambda i: (i, 0)), pl.BlockSpec((ts, d), lambda i: (i, 0))],
        out_specs=_full((NDEV, r, d)),
        out_shape=jax.ShapeDtypeStruct((NDEV, r, d), BF16),
        scratch_shapes=[pltpu.VMEM((NDEV, r, d), F32)],
        compiler_params=_cp("arbitrary"),
    )(a, dy)


def _prev_spec(ts, pad, cb, col):
    return pl.BlockSpec((pad, cb), lambda *g: (jnp.maximum(g[-1] * (ts // pad) - 1, 0), col(g)))


def _next_spec(ts, pad, cb, col, s_len):
    return pl.BlockSpec((pad, cb), lambda *g: (jnp.minimum((g[-1] + 1) * (ts // pad), s_len // pad - 1), col(g)))


def _conv_fwd_rows(buf_ref, w_ref, b_ref, ktaps, pad, r0, rows):
    acc = None
    for j in range(ktaps):
        term = w_ref[ktaps - 1 - j:ktaps - j, :] * buf_ref[pad - j + r0:pad - j + r0 + rows, :]
        acc = term if acc is None else acc + term
    return acc + b_ref[...]


def _conv_bwd_rows(dbuf_ref, x_rows, w_ref, dwacc_ref, ktaps, r0, rows):
    acc = None
    for j in range(ktaps):
        sl = dbuf_ref[j + r0:j + r0 + rows, :]
        term = w_ref[ktaps - 1 - j:ktaps - j, :] * sl
        acc = term if acc is None else acc + term
        prod = x_rows * sl
        fold = prod[0:SUBLANES]
        for q in range(1, rows // SUBLANES):
            fold = fold + prod[q * SUBLANES:(q + 1) * SUBLANES]
        tap = ktaps - 1 - j
        dwacc_ref[tap * SUBLANES:(tap + 1) * SUBLANES, :] += fold
    return acc


def _flush_dw(dwacc_ref, dw_ref, ktaps):
    for tap in range(ktaps):
        dw_ref[tap:tap + 1, :] = jnp.sum(dwacc_ref[tap * SUBLANES:(tap + 1) * SUBLANES, :], axis=0, keepdims=True)


def _gateconv_fwd(bcv, cw, cb, name):
    s_len, d3 = bcv.shape
    d = d3 // 3
    ktaps = cw.shape[0]
    pad = SHORT_PAD
    ts = _tile(s_len, 256)

    def body(gb_ref, gc_ref, v_ref, gcp_ref, vp_ref, w_ref, b_ref, o_ref, pbuf):
        s = pl.program_id(0)
        pbuf[0:pad, :] = jnp.where(s > 0, gcp_ref[...] * vp_ref[...], 0.0)
        pbuf[pad:pad + ts, :] = gc_ref[...] * v_ref[...]
        for r0 in range(0, ts, CHUNK):
            q = _conv_fwd_rows(pbuf, w_ref, b_ref, ktaps, pad, r0, CHUNK)
            o_ref[r0:r0 + CHUNK, :] = (gb_ref[r0:r0 + CHUNK, :] * q).astype(BF16)

    def cur(part):
        return pl.BlockSpec((ts, d), lambda s: (s, part))

    return pl.pallas_call(
        body, name=name, grid=(s_len // ts,),
        in_specs=[cur(0), cur(1), cur(2),
                  _prev_spec(ts, pad, d, lambda g: 1), _prev_spec(ts, pad, d, lambda g: 2),
                  _full((ktaps, d)), _full((1, d))],
        out_specs=pl.BlockSpec((ts, d), lambda s: (s, 0)),
        out_shape=jax.ShapeDtypeStruct((s_len, d), BF16),
        scratch_shapes=[pltpu.VMEM((pad + ts, d), F32)],
        compiler_params=_cp("parallel"),
    )(bcv, bcv, bcv, bcv, bcv, cw, cb)


def _gateconv_bwd(bcv, dy0, cw, cb, name):
    s_len, d3 = bcv.shape
    d = d3 // 3
    ktaps = cw.shape[0]
    pad = SHORT_PAD
    ts = _tile(s_len, 256)
    nsteps = s_len // ts

    def body(gb_ref, gc_ref, v_ref, gcp_ref, vp_ref, gbn_ref, dy_ref, dyn_ref, w_ref, b_ref,
             o_ref, dw_ref, db_ref, pbuf, dqbuf, dwacc):
        s = pl.program_id(0)

        @pl.when(s == 0)
        def _():
            dwacc[...] = jnp.zeros_like(dwacc)
            db_ref[...] = jnp.zeros_like(db_ref)

        pbuf[0:pad, :] = jnp.where(s > 0, gcp_ref[...] * vp_ref[...], 0.0)
        pbuf[pad:pad + ts, :] = gc_ref[...] * v_ref[...]
        dq = dy_ref[...] * gb_ref[...]
        dqbuf[0:ts, :] = dq
        dqbuf[ts:ts + pad, :] = jnp.where(s < nsteps - 1, dyn_ref[...] * gbn_ref[...], 0.0)
        db_ref[...] += jnp.sum(dq, axis=0, keepdims=True)
        for r0 in range(0, ts, CHUNK):
            rows = slice(r0, r0 + CHUNK)
            q = _conv_fwd_rows(pbuf, w_ref, b_ref, ktaps, pad, r0, CHUNK)
            o_ref[rows, 0:d] = (dy_ref[rows, :] * q).astype(BF16)
            dp = _conv_bwd_rows(dqbuf, pbuf[pad + r0:pad + r0 + CHUNK, :], w_ref, dwacc, ktaps, r0, CHUNK)
            o_ref[rows, d:2 * d] = (dp * v_ref[rows, :]).astype(BF16)
            o_ref[rows, 2 * d:3 * d] = (dp * gc_ref[rows, :]).astype(BF16)

        @pl.when(s == nsteps - 1)
        def _():
            _flush_dw(dwacc, dw_ref, ktaps)

    def cur(part):
        return pl.BlockSpec((ts, d), lambda s: (s, part))

    return pl.pallas_call(
        body, name=name, grid=(nsteps,),
        in_specs=[cur(0), cur(1), cur(2),
                  _prev_spec(ts, pad, d, lambda g: 1), _prev_spec(ts, pad, d, lambda g: 2),
                  _next_spec(ts, pad, d, lambda g: 0, s_len),
                  cur(0), _next_spec(ts, pad, d, lambda g: 0, s_len),
                  _full((ktaps, d)), _full((1, d))],
        out_specs=[pl.BlockSpec((ts, d3), lambda s: (s, 0)), _full((ktaps, d)), _full((1, d))],
        out_shape=[jax.ShapeDtypeStruct((s_len, d3), BF16), jax.ShapeDtypeStruct((ktaps, d), F32),
                   jax.ShapeDtypeStruct((1, d), F32)],
        scratch_shapes=[pltpu.VMEM((pad + ts, d), F32), pltpu.VMEM((ts + pad, d), F32),
                        pltpu.VMEM((ktaps * SUBLANES, d), F32)],
        compiler_params=_cp("arbitrary"),
    )(bcv, bcv, bcv, bcv, bcv, bcv, dy0, dy0, cw, cb)


def _ffn_mid_fwd(u0, vg, cw, cb, name):
    s_len, f = u0.shape
    ktaps = cw.shape[0]
    pad = SHORT_PAD
    ts = _tile(s_len, 256)
    cbk = 1024 if f % 1024 == 0 else f

    def body(u_ref, up_ref, vg_ref, w_ref, b_ref, o_ref, ubuf):
        s = pl.program_id(1)
        ubuf[0:pad, :] = jnp.where(s > 0, up_ref[...], 0.0)
        ubuf[pad:pad + ts, :] = u_ref[...]
        for r0 in range(0, ts, CHUNK):
            u = _conv_fwd_rows(ubuf, w_ref, b_ref, ktaps, pad, r0, CHUNK)
            o_ref[r0:r0 + CHUNK, :] = (u * _sigmoid(u) * vg_ref[r0:r0 + CHUNK, :]).astype(BF16)

    cur = pl.BlockSpec((ts, cbk), lambda c, s: (s, c))
    return pl.pallas_call(
        body, name=name, grid=(f // cbk, s_len // ts),
        in_specs=[cur, _prev_spec(ts, pad, cbk, lambda g: g[0]), cur,
                  pl.BlockSpec((ktaps, cbk), lambda c, s: (0, c)), pl.BlockSpec((1, cbk), lambda c, s: (0, c))],
        out_specs=cur,
        out_shape=jax.ShapeDtypeStruct((s_len, f), BF16),
        scratch_shapes=[pltpu.VMEM((pad + ts, cbk), F32)],
        compiler_params=_cp("parallel", "parallel"),
    )(u0, u0, vg, cw, cb)


def _ffn_mid_bwd(u0, vg, dt, cw, cb, name):
    s_len, f = u0.shape
    ktaps = cw.shape[0]
    pad = SHORT_PAD
    ts = _tile(s_len, 256)
    nsteps = s_len // ts
    cbk = 1024 if f % 1024 == 0 else f

    def body(u_ref, up_ref, un_ref, vg_ref, vgn_ref, dt_ref, dtn_ref, w_ref, b_ref,
             du0_ref, dvg_ref, dw_ref, db_ref, ubuf, dubuf, dwacc):
        s = pl.program_id(1)

        @pl.when(s == 0)
        def _():
            dwacc[...] = jnp.zeros_like(dwacc)
            db_ref[...] = jnp.zeros_like(db_ref)

        ubuf[0:pad, :] = jnp.where(s > 0, up_ref[...], 0.0)
        ubuf[pad:pad + ts, :] = u_ref[...]
        ubuf[pad + ts:pad + ts + pad, :] = un_ref[...]
        last = s == nsteps - 1
        for r0 in range(0, ts + pad, CHUNK):
            u = _conv_fwd_rows(ubuf, w_ref, b_ref, ktaps, pad, r0, CHUNK)
            sg = _sigmoid(u)
            if r0 < ts:
                rows = slice(r0, r0 + CHUNK)
                dtr, vgr = dt_ref[rows, :], vg_ref[rows, :]
                dvg_ref[rows, :] = (dtr * u * sg).astype(BF16)
            else:
                rows = slice(r0 - ts, r0 - ts + CHUNK)
                dtr, vgr = jnp.where(last, 0.0, dtn_ref[rows, :]), vgn_ref[rows, :]
            dubuf[r0:r0 + CHUNK, :] = dtr * vgr * (sg * (1.0 + u * (1.0 - sg)))
        db_ref[...] += jnp.sum(dubuf[0:ts, :], axis=0, keepdims=True)
        for r0 in range(0, ts, CHUNK):
            du0 = _conv_bwd_rows(dubuf, u_ref[r0:r0 + CHUNK, :], w_ref, dwacc, ktaps, r0, CHUNK)
            du0_ref[r0:r0 + CHUNK, :] = du0.astype(BF16)

        @pl.when(last)
        def _():
            _flush_dw(dwacc, dw_ref, ktaps)

    cur = pl.BlockSpec((ts, cbk), lambda c, s: (s, c))
    prv = _prev_spec(ts, pad, cbk, lambda g: g[0])
    nxt = _next_spec(ts, pad, cbk, lambda g: g[0], s_len)
    return pl.pallas_call(
        body, name=name, grid=(f // cbk, nsteps),
        in_specs=[cur, prv, nxt, cur, nxt, cur, nxt,
                  pl.BlockSpec((ktaps, cbk), lambda c, s: (0, c)), pl.BlockSpec((1, cbk), lambda c, s: (0, c))],
        out_specs=[cur, cur, pl.BlockSpec((ktaps, cbk), lambda c, s: (0, c)),
                   pl.BlockSpec((1, cbk), lambda c, s: (0, c))],
        out_shape=[jax.ShapeDtypeStruct((s_len, f), BF16), jax.ShapeDtypeStruct((s_len, f), BF16),
                   jax.ShapeDtypeStruct((ktaps, f), F32), jax.ShapeDtypeStruct((1, f), F32)],
        scratch_shapes=[pltpu.VMEM((pad + ts + pad, cbk), F32), pltpu.VMEM((ts + pad, cbk), F32),
                        pltpu.VMEM((ktaps * SUBLANES, cbk), F32)],
        compiler_params=_cp("parallel", "arbitrary"),
    )(u0, u0, u0, vg, vg, dt, dt, cw, cb)


def _b_mid_fwd(ub, cw, cb, lng, lnb, name):
    s_len, d2 = ub.shape
    d = d2 // 2
    ktaps = cw.shape[0]
    pad = LONG_PAD
    ts = _tile(s_len, 256)

    def body(a_ref, g_ref, ap_ref, gp_ref, w_ref, b_ref, lng_ref, lnb_ref, a2_ref, a4_ref, abuf):
        s = pl.program_id(0)
        abuf[0:pad, :] = jnp.where(s > 0, ap_ref[...] * _sigmoid(gp_ref[...]), 0.0)
        abuf[pad:pad + ts, :] = a_ref[...] * _sigmoid(g_ref[...])
        for r0 in range(0, ts, CHUNK):
            a2_ref[r0:r0 + CHUNK, :] = _conv_fwd_rows(abuf, w_ref, b_ref, ktaps, pad, r0, CHUNK)
        a2 = a2_ref[...]
        mu = jnp.mean(a2, axis=-1, keepdims=True)
        ac = a2 - mu
        var = jnp.mean(ac * ac, axis=-1, keepdims=True)
        a3 = ac * lax.rsqrt(var + LN_EPS) * lng_ref[...] + lnb_ref[...]
        a4_ref[...] = (a3 * _sigmoid(a3)).astype(BF16)

    def cur(part):
        return pl.BlockSpec((ts, d), lambda s: (s, part))

    vec = _full((1, d))
    return pl.pallas_call(
        body, name=name, grid=(s_len // ts,),
        in_specs=[cur(0), cur(1), _prev_spec(ts, pad, d, lambda g: 0), _prev_spec(ts, pad, d, lambda g: 1),
                  _full((ktaps, d)), vec, vec, vec],
        out_specs=[cur(0), cur(0)],
        out_shape=[jax.ShapeDtypeStruct((s_len, d), F32), jax.ShapeDtypeStruct((s_len, d), BF16)],
        scratch_shapes=[pltpu.VMEM((pad + ts, d), F32)],
        compiler_params=_cp("parallel"),
    )(ub, ub, ub, ub, cw, cb, lng, lnb)


def _b_mid_bwd(ub, a2, da4, cw, lng, lnb, name):
    s_len, d2 = ub.shape
    d = d2 // 2
    ktaps = cw.shape[0]
    pad = LONG_PAD
    ts = _tile(s_len, 256)
    nsteps = s_len // ts

    def body(a_ref, g_ref, a2_ref, a2n_ref, da4_ref, da4n_ref, w_ref, lng_ref, lnb_ref,
             du_ref, dw_ref, db_ref, dlng_ref, dlnb_ref, dbias_ref, dabuf, dwacc):
        s = pl.program_id(0)
        last = s == nsteps - 1

        @pl.when(s == 0)
        def _():
            dwacc[...] = jnp.zeros_like(dwacc)
            for ref in (db_ref, dlng_ref, dlnb_ref, dbias_ref):
                ref[...] = jnp.zeros_like(ref)

        def ln_silu_bwd(a2_t, da4_t):
            mu = jnp.mean(a2_t, axis=-1, keepdims=True)
            ac = a2_t - mu
            var = jnp.mean(ac * ac, axis=-1, keepdims=True)
            rstd = lax.rsqrt(var + LN_EPS)
            ah = ac * rstd
            a3 = ah * lng_ref[...] + lnb_ref[...]
            sg = _sigmoid(a3)
            da3 = da4_t * (sg * (1.0 + a3 * (1.0 - sg)))
            dah = da3 * lng_ref[...]
            m1 = jnp.mean(dah, axis=-1, keepdims=True)
            m2 = jnp.mean(dah * ah, axis=-1, keepdims=True)
            return rstd * (dah - m1 - ah * m2), da3, ah

        da2, da3, ah = ln_silu_bwd(a2_ref[...], da4_ref[...])
        dabuf[0:ts, :] = da2
        dlng_ref[...] += jnp.sum(da3 * ah, axis=0, keepdims=True)
        dlnb_ref[...] += jnp.sum(da3, axis=0, keepdims=True)
        db_ref[...] += jnp.sum(da2, axis=0, keepdims=True)
        da2n, _, _ = ln_silu_bwd(a2n_ref[...], jnp.where(last, 0.0, da4n_ref[...]))
        dabuf[ts:ts + pad, :] = da2n
        for r0 in range(0, ts, CHUNK):
            rows = slice(r0, r0 + CHUNK)
            a_r, g_r = a_ref[rows, :], g_ref[rows, :]
            sg = _sigmoid(g_r)
            da1 = _conv_bwd_rows(dabuf, a_r * sg, w_ref, dwacc, ktaps, r0, CHUNK)
            da = da1 * sg
            dg = da1 * a_r * sg * (1.0 - sg)
            du_ref[rows, 0:d] = da.astype(BF16)
            du_ref[rows, d:2 * d] = dg.astype(BF16)
            dbias_ref[:, 0:d] += jnp.sum(da, axis=0, keepdims=True)
            dbias_ref[:, d:2 * d] += jnp.sum(dg, axis=0, keepdims=True)

        @pl.when(last)
        def _():
            _flush_dw(dwacc, dw_ref, ktaps)

    def cur(part):
        return pl.BlockSpec((ts, d), lambda s: (s, part))

    vec = _full((1, d))
    nxt = _next_spec(ts, pad, d, lambda g: 0, s_len)
    return pl.pallas_call(
        body, name=name, grid=(nsteps,),
        in_specs=[cur(0), cur(1), cur(0), nxt, cur(0), nxt, _full((ktaps, d)), vec, vec],
        out_specs=[pl.BlockSpec((ts, d2), lambda s: (s, 0)), _full((ktaps, d)), vec, vec, vec, _full((1, d2))],
        out_shape=[jax.ShapeDtypeStruct((s_len, d2), BF16), jax.ShapeDtypeStruct((ktaps, d), F32),
                   jax.ShapeDtypeStruct((1, d), F32), jax.ShapeDtypeStruct((1, d), F32),
                   jax.ShapeDtypeStruct((1, d), F32), jax.ShapeDtypeStruct((1, d2), F32)],
        scratch_shapes=[pltpu.VMEM((ts + pad, d), F32), pltpu.VMEM((ktaps * SUBLANES, d), F32)],
        compiler_params=_cp("arbitrary"),
    )(ub, ub, a2, a2, da4, da4, cw, lng, lnb)


def _loss_head(xo, tgt, name):
    s_len, d = xo.shape
    tm = _tile(s_len, 512)

    def body(x_ref, t_ref, d_ref, l_ref):
        @pl.when(pl.program_id(0) == 0)
        def _():
            l_ref[...] = jnp.zeros_like(l_ref)

        e = x_ref[...] - t_ref[...]
        d_ref[...] = e * (1.0 / d)
        per_row = jnp.sum(e * e, axis=-1, keepdims=True) * (1.0 / d)
        l_ref[...] += 0.5 * jnp.sum(per_row, axis=0, keepdims=True)

    row = pl.BlockSpec((tm, d), lambda i: (i, 0))
    return pl.pallas_call(
        body, name=name, grid=(s_len // tm,),
        in_specs=[row, row], out_specs=[row, _full((1, LANES))],
        out_shape=[jax.ShapeDtypeStruct((s_len, d), F32), jax.ShapeDtypeStruct((1, LANES), F32)],
        compiler_params=_cp("arbitrary"),
    )(xo, tgt)


def _ada_fwd(c_all, ada_w, ada_b_loc, name):
    depth, d, n = ada_w.shape

    def body(c_ref, w_ref, b_ref, o_ref):
        c = c_ref[...]
        act = c * _sigmoid(c)
        o_ref[...] = jnp.dot(act, w_ref[...], preferred_element_type=F32,
                             precision=lax.Precision.HIGHEST) + b_ref[...]

    return pl.pallas_call(
        body, name=name, grid=(depth,),
        in_specs=[_full((NDEV, d)), pl.BlockSpec((None, d, n), lambda i: (i, 0, 0)),
                  pl.BlockSpec((None, 1, n), lambda i: (i, 0, 0))],
        out_specs=pl.BlockSpec((None, NDEV, n), lambda i: (i, 0, 0)),
        out_shape=jax.ShapeDtypeStruct((depth, NDEV, n), F32),
        compiler_params=_cp("parallel"),
    )(c_all, ada_w, ada_b_loc.reshape(depth, 1, n))


def _ada_bwd(c_all_t, dmod_cols, name):
    depth, _, n = dmod_cols.shape
    d = c_all_t.shape[0]

    def body(ct_ref, dm_ref, o_ref):
        ct = ct_ref[...]
        act = ct * _sigmoid(ct)
        acc = None
        for b in range(NDEV):
            term = act[:, b:b + 1] * dm_ref[b:b + 1, :]
            acc = term if acc is None else acc + term
        o_ref[...] = acc

    return pl.pallas_call(
        body, name=name, grid=(depth,),
        in_specs=[_full((d, NDEV)), pl.BlockSpec((None, NDEV, n), lambda i: (i, 0, 0))],
        out_specs=pl.BlockSpec((None, d, n), lambda i: (i, 0, 0)),
        out_shape=jax.ShapeDtypeStruct((depth, d, n), F32),
        compiler_params=_cp("parallel"),
    )(c_all_t, dmod_cols)


def _sum_parts(parts, name):
    _, rows, lanes = parts.shape

    def body(p_ref, o_ref):
        acc = p_ref[0]
        for k in range(1, NDEV):
            acc = acc + p_ref[k]
        o_ref[...] = acc

    return pl.pallas_call(
        body, name=name, in_specs=[_full(parts.shape)], out_specs=_full((rows, lanes)), grid=(1,),
        out_shape=jax.ShapeDtypeStruct((rows, lanes), F32), compiler_params=_cp("arbitrary"),
    )(parts)


def _adamw(w, glist, m, v, name):
    nl, rows, cols = w.shape
    tr = _tile(rows, 256, 2 * SUBLANES)

    def body(w_ref, *rest):
        g_refs = rest[:nl]
        m_ref, v_ref, go_ref, d_ref, mo_ref, vo_ref = rest[nl:]
        g = None
        for layer, g_ref in enumerate(g_refs):
            part = g_ref[0].astype(F32)
            for p in range(1, g_ref.shape[0]):
                part = part + g_ref[p].astype(F32)
            g = part if g is None else jnp.where(pl.program_id(0) == layer, part, g)
        m1 = ADAM_B1 * m_ref[...] + (1.0 - ADAM_B1) * g
        v1 = ADAM_B2 * v_ref[...] + (1.0 - ADAM_B2) * (g * g)
        m_hat = m1 / (1.0 - ADAM_B1 ** ADAM_STEP)
        v_hat = v1 / (1.0 - ADAM_B2 ** ADAM_STEP)
        go_ref[...] = g
        mo_ref[...] = m1
        vo_ref[...] = v1
        d_ref[...] = -ADAM_LR * (m_hat / (jnp.sqrt(v_hat) + ADAM_EPS) + ADAM_WD * w_ref[...])

    blk = pl.BlockSpec((None, tr, cols), lambda l, i: (l, i, 0))
    g_specs = [pl.BlockSpec((g.shape[0], tr, cols), lambda l, i: (0, i, 0)) for g in glist]
    return pl.pallas_call(
        body, name=name, grid=(nl, rows // tr),
        in_specs=[blk] + g_specs + [blk, blk],
        out_specs=[blk] * 4, out_shape=[jax.ShapeDtypeStruct((nl, rows, cols), F32)] * 4,
        compiler_params=_cp("parallel", "parallel"),
    )(w, *glist, m, v)


def _pack(pieces):
    flat = jnp.concatenate([p.reshape(-1) for p in pieces])
    unit = SUBLANES * LANES
    padded = -(-flat.shape[0] // unit) * unit
    return jnp.pad(flat, (0, padded - flat.shape[0])).reshape(padded // LANES, LANES)


def _unpack(packed, shapes, lead=()):
    flat = packed.reshape(lead + (-1,))
    out, off = [], 0
    for s in shapes:
        size = 1
        for dim in s:
            size *= dim
        out.append(flat[..., off:off + size].reshape(lead + tuple(s)))
        off += size
    return out


def _pad_last(a, n):
    return jnp.pad(a, [(0, 0)] * (a.ndim - 1) + [(0, n - a.shape[-1])])


def kernel(x, c, ada_w, ada_b, ln_tok_g, ln_tok_b, ln_ch_g, ln_ch_b, a_w_in, a_conv_w, a_conv_b, a_w_out, b_w_pw1, b_b_pw1, b_conv_w, b_conv_b, b_ln_g, b_ln_b, b_w_pw2, b_b_pw2, f_w_up, f_conv_w, f_conv_b, f_w_gate, f_w_down, loss_target, m_ada_w, m_ada_b, m_ln_tok_g, m_ln_tok_b, m_ln_ch_g, m_ln_ch_b, m_a_w_in, m_a_conv_w, m_a_conv_b, m_a_w_out, m_b_w_pw1, m_b_b_pw1, m_b_conv_w, m_b_conv_b, m_b_ln_g, m_b_ln_b, m_b_w_pw2, m_b_b_pw2, m_f_w_up, m_f_conv_w, m_f_conv_b, m_f_w_gate, m_f_w_down, v_ada_w, v_ada_b, v_ln_tok_g, v_ln_tok_b, v_ln_ch_g, v_ln_ch_b, v_a_w_in, v_a_conv_w, v_a_conv_b, v_a_w_out, v_b_w_pw1, v_b_b_pw1, v_b_conv_w, v_b_conv_b, v_b_ln_g, v_b_ln_b, v_b_w_pw2, v_b_b_pw2, v_f_w_up, v_f_conv_w, v_f_conv_b, v_f_w_gate, v_f_w_down):
    weights = dict(ada_w=ada_w, ada_b=ada_b, ln_tok_g=ln_tok_g, ln_tok_b=ln_tok_b, ln_ch_g=ln_ch_g, ln_ch_b=ln_ch_b, a_w_in=a_w_in, a_conv_w=a_conv_w, a_conv_b=a_conv_b, a_w_out=a_w_out, b_w_pw1=b_w_pw1, b_b_pw1=b_b_pw1, b_conv_w=b_conv_w, b_conv_b=b_conv_b, b_ln_g=b_ln_g, b_ln_b=b_ln_b, b_w_pw2=b_w_pw2, b_b_pw2=b_b_pw2, f_w_up=f_w_up, f_conv_w=f_conv_w, f_conv_b=f_conv_b, f_w_gate=f_w_gate, f_w_down=f_w_down)
    mom_m = dict(ada_w=m_ada_w, ada_b=m_ada_b, ln_tok_g=m_ln_tok_g, ln_tok_b=m_ln_tok_b, ln_ch_g=m_ln_ch_g, ln_ch_b=m_ln_ch_b, a_w_in=m_a_w_in, a_conv_w=m_a_conv_w, a_conv_b=m_a_conv_b, a_w_out=m_a_w_out, b_w_pw1=m_b_w_pw1, b_b_pw1=m_b_b_pw1, b_conv_w=m_b_conv_w, b_conv_b=m_b_conv_b, b_ln_g=m_b_ln_g, b_ln_b=m_b_ln_b, b_w_pw2=m_b_w_pw2, b_b_pw2=m_b_b_pw2, f_w_up=m_f_w_up, f_conv_w=m_f_conv_w, f_conv_b=m_f_conv_b, f_w_gate=m_f_w_gate, f_w_down=m_f_w_down)
    mom_v = dict(ada_w=v_ada_w, ada_b=v_ada_b, ln_tok_g=v_ln_tok_g, ln_tok_b=v_ln_tok_b, ln_ch_g=v_ln_ch_g, ln_ch_b=v_ln_ch_b, a_w_in=v_a_w_in, a_conv_w=v_a_conv_w, a_conv_b=v_a_conv_b, a_w_out=v_a_w_out, b_w_pw1=v_b_w_pw1, b_b_pw1=v_b_b_pw1, b_conv_w=v_b_conv_w, b_conv_b=v_b_conv_b, b_ln_g=v_b_ln_g, b_ln_b=v_b_ln_b, b_w_pw2=v_b_w_pw2, b_b_pw2=v_b_b_pw2, f_w_up=v_f_w_up, f_conv_w=v_f_conv_w, f_conv_b=v_f_conv_b, f_w_gate=v_f_w_gate, f_w_down=v_f_w_down)
    names = list(weights)

    depth, d, n_ada = ada_w.shape
    assert depth == 2 and a_w_in.shape[0] == 1 and b_w_pw1.shape[0] == 1
    s_len = x.shape[1]
    f_loc = f_w_up.shape[-1]
    f_pad = -(-f_loc // LANES) * LANES
    f_all = NDEV * f_pad
    d_loc = d // NDEV
    ka, kb, kf = a_conv_w.shape[1], b_conv_w.shape[1], f_conv_w.shape[1]
    alpha = (2.0 * depth) ** 0.25
    assert a_w_in.shape[-1] == f_pad and f_pad % d_loc == 0
    me = 4 * lax.axis_index("x") + 2 * lax.axis_index("y") + lax.axis_index("c")

    small_shapes = [(d,), (ka, d_loc), (2 * d_loc,), (kb, d_loc), (d_loc,), (d_loc,), (d_loc,), (d_loc,),
                    (depth, kf, f_pad)]
    small_loc = _pack([c[0], a_conv_w[0], b_b_pw1[0], b_conv_w[0], b_conv_b[0], b_ln_g[0], b_ln_b[0],
                       b_b_pw2[0], _pad_last(f_conv_w, f_pad)])
    up_pad = _pad_last(f_w_up, f_pad).astype(BF16)
    gate_pad = _pad_last(f_w_gate, f_pad).astype(BF16)
    down_pad = jnp.pad(f_w_down, ((0, 0), (0, f_pad - f_loc), (0, 0))).astype(BF16)
    col_f = [jnp.stack([up_pad[i], gate_pad[i]]) for i in range(depth)]
    row_b = jnp.concatenate([down_pad[1], b_w_pw2[0].astype(BF16)], axis=0)
    ridx_pw2 = f_pad // d_loc
    g_small, g_in, g_out = _exchange([small_loc, a_w_in.astype(BF16), a_w_out[0].astype(BF16)], True, "gather_first")
    gather_f0 = _exchange_start([col_f[0], down_pad[0]], True, "gather_f0_start")

    (c_all, acw_g, bb1_g, bcw_g, bcb_g, blg_g, blb_g, bb2_g, fcw_g) = _unpack(g_small, small_shapes, (NDEV,))
    a_cw = acw_g.transpose(1, 0, 2).reshape(ka, d)
    b_cw = bcw_g.transpose(1, 0, 2).reshape(kb, d)
    b_b1 = bb1_g.reshape(1, 2 * d)
    b_cb, b_lg, b_lb, b_b2 = (t.reshape(1, d) for t in (bcb_g, blg_g, blb_g, bb2_g))
    f_cw = fcw_g.transpose(1, 2, 0, 3).reshape(depth, kf, f_all)
    f_cb = _pad_last(f_conv_b.reshape(depth, NDEV, f_loc), f_pad).reshape(depth, 1, f_all)

    ada_b_loc = lax.dynamic_slice(ada_b, (0, me * n_ada), (depth, n_ada))
    mod_part = _ada_fwd(c_all, ada_w, ada_b_loc, "ada_fwd")
    mod_g, = _exchange([mod_part.reshape(depth * NDEV, n_ada)], True, "gather_mod")
    mod_all = mod_g.reshape(NDEV, depth, NDEV, n_ada).transpose(1, 2, 0, 3).reshape(depth, NDEV, 6 * d)
    mod = lax.dynamic_slice(mod_all, (0, me, 0), (depth, 1, 6 * d))[:, 0]

    def mod_rows(i):
        return [mod[i:i + 1, j * d:(j + 1) * d] for j in range(6)]

    zeros_d = jnp.zeros((1, d), F32)
    zeros_f = jnp.zeros((1, f_all), F32)
    x0 = x[0]

    sh_t0, sc_t0, g_t0, sh_c0, sc_c0, g_c0 = mod_rows(0)
    sh_t1, sc_t1, g_t1, sh_c1, sc_c1, g_c1 = mod_rows(1)

    sc_t0 = _after(sc_t0, gather_f0[-1])
    bcv, = _mm_fwd(x0, sc_t0, sh_t0, jnp.zeros((1, 3 * d), F32), g_in, (0,), "a_in_fwd")
    y0 = _gateconv_fwd(bcv, a_cw, a_conv_b, "a_conv_fwd")
    y_a, x1, xh1, rs1 = _mm_ln(y0, g_out, d_loc, 0, x0, g_t0, ln_tok_g[0:1], ln_tok_b[0:1], zeros_d,
                               alpha, "a_out_ln_fwd")

    def ffn_fwd(xin, sc, sh, gate, gam, bet, g_colf, g_rowf, layer, tag):
        u0, vg = _mm_fwd(xin, sc, sh, zeros_f, g_colf, (0, 1), "f_upgate_fwd" + tag)
        t = _ffn_mid_fwd(u0, vg, f_cw[layer], f_cb[layer], "f_mid_fwd" + tag)
        y, xo, xh, rs = _mm_ln(t, g_rowf, f_pad, 0, xin, gate, gam, bet, zeros_d, alpha, "f_down_ln_fwd" + tag)
        return u0, vg, t, y, xo, xh, rs

    g_colf0, g_rowf0 = _exchange_wait(gather_f0, x1, True, "gather_f0_wait")
    gather_1 = _exchange_start([_after(b_w_pw1.astype(BF16), g_rowf0), col_f[1], row_b], True, "gather_1_start")
    sc_c0 = _after(sc_c0, gather_1[-1])
    u0_0, vg_0, t_0, y_f0, x2, xh2, rs2 = ffn_fwd(x1, sc_c0, sh_c0, g_c0, ln_ch_g[0:1], ln_ch_b[0:1],
                                                  g_colf0, g_rowf0, 0, "0")

    g_pw1, g_colf1, g_rowb = _exchange_wait(gather_1, x2, True, "gather_1_wait")
    ub, = _mm_fwd(x2, sc_t1, sh_t1, b_b1, g_pw1, (0,), "b_pw1_fwd")
    a2, a4 = _b_mid_fwd(ub, b_cw, b_cb, b_lg, b_lb, "b_mid_fwd")
    y_b, x3, xh3, rs3 = _mm_ln(a4, g_rowb, d_loc, ridx_pw2, x2, g_t1, ln_tok_g[1:2], ln_tok_b[1:2], b_b2,
                               alpha, "b_pw2_ln_fwd")
    u0_1, vg_1, t_1, y_f1, x4, xh4, rs4 = ffn_fwd(x3, sc_c1, sh_c1, g_c1, ln_ch_g[1:2], ln_ch_b[1:2],
                                                  g_colf1, g_rowb, 1, "1")

    dx4, loss_part = _loss_head(x4, loss_target[0], "loss_head")
    loss = lax.psum(loss_part[0, 0], MESH_AXES)

    def ffn_bwd(dxo, xin, sc, sh, gate, gam, u0, vg, t, y, xh, rs, g_colf, g_rowf, layer, tag):
        dy, dres, acc = _ln_bwd(dxo, xh, rs, gam, y, gate, alpha, "f_ln_bwd" + tag)
        dt = _mm_nt_row(dy, g_rowf, f_pad, 0, "f_down_dx" + tag)
        dw_down = _mm_tn_row(t, dy, f_pad, "f_down_dw" + tag)
        du0, dvg, dcw, dcb = _ffn_mid_bwd(u0, vg, dt, f_cw[layer], f_cb[layer], "f_mid_bwd" + tag)
        dw_up = _mm_tn_col(xin, sc, sh, du0, "f_up_dw" + tag)
        dw_gate = _mm_tn_col(xin, sc, sh, dvg, "f_gate_dw" + tag)
        scatter = _exchange_start([dw_up[..., :f_loc], dw_gate[..., :f_loc], dw_down[:, :f_loc]], False,
                                  "scatter_f%s_start" % tag)
        dxin, acc2 = _mm_nt_mod([du0, dvg], g_colf, (0, 1), xin, _after(sc, scatter[-1]), dres, "f_upgate_dx" + tag)
        return dxin, acc, acc2, scatter, dcw, dcb

    dx3, accf1, acc2f1, scatter_f1, dfcw1, dfcb1 = ffn_bwd(
        dx4, x3, sc_c1, sh_c1, g_c1, ln_ch_g[1:2], u0_1, vg_1, t_1, y_f1, xh4, rs4, g_colf1, g_rowb, 1, "1")

    dy, dres, accb = _ln_bwd(dx3, xh3, rs3, ln_tok_g[1:2], y_b, g_t1, alpha, "b_ln_bwd")
    da4 = _mm_nt_row(dy, g_rowb, d_loc, ridx_pw2, "b_pw2_dx")
    dw_pw2 = _mm_tn_row(a4, dy, d_loc, "b_pw2_dw")
    du, dbcw, dbcb, dblg, dblb, dbb1 = _b_mid_bwd(ub, a2, da4, b_cw, b_lg, b_lb, "b_mid_bwd")
    dw_pw1 = _mm_tn_col(x2, sc_t1, sh_t1, du, "b_pw1_dw")
    scatter_b = _exchange_start([dw_pw1, dw_pw2], False, "scatter_b_start")
    dx2, acc2b = _mm_nt_mod([du], g_pw1, (0,), x2, _after(sc_t1, scatter_b[-1]), dres, "b_pw1_dx")

    dx1, accf0, acc2f0, scatter_f0, dfcw0, dfcb0 = ffn_bwd(
        dx2, x1, sc_c0, sh_c0, g_c0, ln_ch_g[0:1], u0_0, vg_0, t_0, y_f0, xh2, rs2, g_colf0, g_rowf0, 0, "0")

    dy, dres, acca = _ln_bwd(dx1, xh1, rs1, ln_tok_g[0:1], y_a, g_t0, alpha, "a_ln_bwd")
    dy0 = _mm_nt_row(dy, g_out, d_loc, 0, "a_out_dx")
    dw_out = _mm_tn_row(y0, dy, d_loc, "a_out_dw")
    dbcv, dacw, dacb = _gateconv_bwd(bcv, dy0, a_cw, a_conv_b, "a_conv_bwd")
    dw_in = _mm_tn_col(x0, sc_t0, sh_t0, dbcv, "a_in_dw")
    scatter_a = _exchange_start([dw_in, dw_out], False, "scatter_a_start")
    dx0, acc2a = _mm_nt_mod([dbcv], g_in, (0,), x0, _after(sc_t0, scatter_a[-1]), dres, "a_in_dx")

    def dmod_row(acc2_t, acc_t, acc2_c, acc_c):
        return jnp.concatenate([acc2_t[1], acc2_t[0], acc_t[2], acc2_c[1], acc2_c[0], acc_c[2]])

    dmod = jnp.stack([dmod_row(acc2a, acca, acc2f0, accf0), dmod_row(acc2b, accb, acc2f1, accf1)])

    def unpad_f(a):
        return a.reshape(a.shape[:-1] + (NDEV, f_pad))[..., :f_loc].reshape(a.shape[:-1] + (NDEV * f_loc,))

    small_grads = [
        dmod,
        jnp.stack([acca[0], accb[0]]), jnp.stack([acca[1], accb[1]]),
        jnp.stack([accf0[0], accf1[0]]), jnp.stack([accf0[1], accf1[1]]),
        dacb,
        unpad_f(jnp.concatenate([dfcb0, dfcb1], axis=0)),
        dacw, dbb1, dbcw, dbcb, dblg, dblb, accb[3:4],
        jnp.stack([dfcw0, dfcw1]),
    ]
    small_grad_shapes = [tuple(g.shape) for g in small_grads]
    sg_all, = _exchange([_pack(small_grads)], True, "gather_small_grads")
    sg_sum = _sum_parts(sg_all, "sum_small_grads")
    (g_ada_b, g_ltg, g_ltb, g_lcg, g_lcb, g_acb, g_fcb, g_acw, g_bb1, g_bcw, g_bcb, g_blg, g_blb, g_bb2,
     g_fcw) = _unpack(sg_sum, small_grad_shapes)

    def my_cols(a, width):
        return lax.dynamic_slice_in_dim(a, me * width, width, axis=a.ndim - 1)

    g_fcw_loc = my_cols(g_fcw, f_pad)[..., :f_loc]
    small = dict(
        ada_b=g_ada_b, ln_tok_g=g_ltg, ln_tok_b=g_ltb, ln_ch_g=g_lcg, ln_ch_b=g_lcb, a_conv_b=g_acb, f_conv_b=g_fcb,
        a_conv_w=my_cols(g_acw, d_loc)[None], b_b_pw1=my_cols(g_bb1, 2 * d_loc), b_conv_w=my_cols(g_bcw, d_loc)[None],
        b_conv_b=my_cols(g_bcb, d_loc), b_ln_g=my_cols(g_blg, d_loc), b_ln_b=my_cols(g_blb, d_loc),
        b_b_pw2=my_cols(g_bb2, d_loc), f_conv_w=g_fcw_loc)

    dmod_all = sg_all.reshape(NDEV, -1)[:, :depth * 6 * d].reshape(NDEV, depth, 6 * d)
    dmod_cols = my_cols(dmod_all, n_ada).transpose(1, 0, 2)
    g_ada_w = _ada_bwd(c_all.T, dmod_cols, "ada_bwd")

    grads, deltas, new_m, new_v = {}, {}, {}, {}

    def adamw(k, glist):
        w = weights[k]
        cols = w.shape[-1]
        nl = len(glist)
        outs = _adamw(w.reshape(nl, -1, cols), [g.reshape(g.shape[0], -1, cols) for g in glist],
                      mom_m[k].reshape(nl, -1, cols), mom_v[k].reshape(nl, -1, cols), "adamw_" + k)
        grads[k], deltas[k], new_m[k], new_v[k] = (o.reshape(w.shape) for o in outs)

    adamw("ada_w", [g_ada_w[0:1], g_ada_w[1:2]])
    for k, g in small.items():
        adamw(k, [g[None]])

    r_up1, r_gate1, r_down1 = _exchange_wait(scatter_f1, deltas["ada_w"], False, "scatter_f1_wait")
    r_pw1, r_pw2 = _exchange_wait(scatter_b, r_down1, False, "scatter_b_wait")
    adamw("b_w_pw1", [r_pw1])
    adamw("b_w_pw2", [r_pw2])
    r_up0, r_gate0, r_down0 = _exchange_wait(scatter_f0, deltas["b_w_pw2"], False, "scatter_f0_wait")
    adamw("f_w_up", [r_up0, r_up1])
    adamw("f_w_gate", [r_gate0, r_gate1])
    adamw("f_w_down", [r_down0, r_down1])
    r_in, r_out = _exchange_wait(scatter_a, deltas["f_w_down"], False, "scatter_a_wait")
    adamw("a_w_in", [r_in])
    adamw("a_w_out", [r_out])

    return (loss, dx0[None], *[grads[k] for k in names], *[deltas[k] for k in names],
            *[new_m[k] for k in names], *[new_v[k] for k in names])
```

```python
import functools

import jax
import jax.numpy as jnp
from jax import lax
from jax.experimental import pallas as pl
from jax.experimental.pallas import tpu as pltpu

NDEV = 8
MESH_AXES = ("x", "y", "c")
LANES = 128
SUBLANES = 8
VMEM_LIMIT = 56 * 1024 * 1024
LN_EPS = 1e-5
SHORT_PAD = 16
LONG_PAD = 32
CHUNK = 16
ADAM_LR, ADAM_B1, ADAM_B2, ADAM_EPS, ADAM_WD, ADAM_STEP = 0.001, 0.9, 0.999, 1e-08, 0.01, 10

F32 = jnp.float32
BF16 = jnp.bfloat16
MESH = pl.DeviceIdType.MESH
NT = (((1,), (1,)), ((), ()))
TN = (((0,), (0,)), ((), ()))


def _tile(n, target, mult=SUBLANES):
    best = None
    for t in range(mult, min(n, target) + 1, mult):
        if n % t == 0:
            best = t
    return best if best is not None else n


def _full(shape):
    nd = len(shape)
    return pl.BlockSpec(shape, lambda *_: (0,) * nd)


def _cp(*sem):
    return pltpu.CompilerParams(dimension_semantics=sem, vmem_limit_bytes=VMEM_LIMIT)


def _sigmoid(x):
    return 1.0 / (1.0 + jnp.exp(-x))


def _peer(x, y, c, d):
    return ((1 - x) if d & 4 else x, (1 - y) if d & 2 else y, (1 - c) if d & 1 else c)


def _lin(p):
    return 4 * p[0] + 2 * p[1] + p[2]


def _remote_copies(src_refs, land_refs, send_sems, recv_sems, gather):
    x, y, c = (lax.axis_index(a) for a in MESH_AXES)
    me = _lin((x, y, c))
    sends, recvs = [], []
    for i, (src_ref, land_ref) in enumerate(zip(src_refs, land_refs)):
        for d in range(1, NDEV):
            peer = _peer(x, y, c, d)
            k = i * (NDEV - 1) + d - 1
            src = src_ref if gather else src_ref.at[_lin(peer)]
            for slot, out in ((me, sends), (_lin(peer), recvs)):
                out.append(pltpu.make_async_remote_copy(
                    src_ref=src, dst_ref=land_ref.at[slot], send_sem=send_sems.at[k], recv_sem=recv_sems.at[k],
                    device_id=peer, device_id_type=MESH))
    return sends, recvs


def _exchange(srcs, gather, name):
    n = len(srcs)

    def body(*refs):
        src_refs, out_refs = refs[:n], refs[n:2 * n]
        send_sems, recv_sems, local_sems = refs[2 * n:]
        me = _lin(tuple(lax.axis_index(a) for a in MESH_AXES))
        local = []
        for i in range(n):
            mine = src_refs[i] if gather else src_refs[i].at[me]
            cp = pltpu.make_async_copy(mine, out_refs[i].at[me], local_sems.at[i])
            cp.start()
            local.append(cp)
        sends, recvs = _remote_copies(src_refs, out_refs, send_sems, recv_sems, gather)
        for snd in sends:
            snd.start()
        for snd, rcv in zip(sends, recvs):
            snd.wait_send()
            rcv.wait_recv()
        for cp in local:
            cp.wait()

    out_shape = [jax.ShapeDtypeStruct(((NDEV,) + s.shape) if gather else s.shape, s.dtype) for s in srcs]
    any_spec = pl.BlockSpec(memory_space=pl.ANY)
    return pl.pallas_call(
        body, name=name, out_shape=out_shape,
        in_specs=[any_spec] * n, out_specs=[any_spec] * n,
        scratch_shapes=[pltpu.SemaphoreType.DMA((n * (NDEV - 1),)),
                        pltpu.SemaphoreType.DMA((n * (NDEV - 1),)),
                        pltpu.SemaphoreType.DMA((n,))],
    )(*srcs)


HBM_SPEC = pl.BlockSpec(memory_space=pltpu.HBM)
SEM_SPEC = pl.BlockSpec(memory_space=pltpu.SEMAPHORE)
SIDE_EFFECT = pltpu.SideEffectType.DATAFLOW_SIDE_EFFECTING


def _exchange_start(srcs, gather, name):
    n = len(srcs)
    me = _lin(tuple(lax.axis_index(a) for a in MESH_AXES))
    lands = []
    for s in srcs:
        own = s if gather else lax.dynamic_index_in_dim(s, me, 0, keepdims=False)
        shape = ((NDEV,) + s.shape) if gather else s.shape
        lands.append(lax.dynamic_update_index_in_dim(lax.empty(shape, s.dtype), own, me, 0))

    def body(*refs):
        src_refs, land_refs = refs[:n], refs[n:2 * n]
        send_sems, recv_sems, token = refs[2 * n], refs[2 * n + 1], refs[-1]
        sends, _ = _remote_copies(src_refs, land_refs, send_sems, recv_sems, gather)
        for snd in sends:
            snd.start()
        token[...] = jnp.zeros_like(token)

    operands = [pltpu.with_memory_space_constraint(a, pltpu.HBM) for a in list(srcs) + lands]
    nsem = n * (NDEV - 1)
    return pl.pallas_call(
        body, name=name,
        out_shape=(pltpu.SemaphoreType.DMA((nsem,)), pltpu.SemaphoreType.DMA((nsem,)),
                   *[pltpu.HBM(a.shape, a.dtype) for a in operands],
                   jax.ShapeDtypeStruct((SUBLANES, LANES), F32)),
        in_specs=[HBM_SPEC] * (2 * n),
        out_specs=(SEM_SPEC, SEM_SPEC, *([HBM_SPEC] * (2 * n)), pl.BlockSpec(memory_space=pltpu.VMEM)),
        input_output_aliases={i: 2 + i for i in range(2 * n)},
        compiler_params=pltpu.CompilerParams(has_side_effects=SIDE_EFFECT),
    )(*operands)


def _exchange_wait(handle, after, gather, name):
    send_sems, recv_sems, *thru = handle[:-1]
    n = len(thru) // 2

    def body(*refs):
        src_refs, land_refs = refs[:n], refs[n:2 * n]
        sends, recvs = _remote_copies(src_refs, land_refs, refs[2 * n], refs[2 * n + 1], gather)
        for snd, rcv in zip(sends, recvs):
            snd.wait_send()
            rcv.wait_recv()
        refs[-1][...] = jnp.zeros_like(refs[-1])

    outs = pl.pallas_call(
        body, name=name,
        out_shape=(*[pltpu.HBM(a.shape, a.dtype) for a in thru], jax.ShapeDtypeStruct((SUBLANES, LANES), F32)),
        in_specs=[HBM_SPEC] * (2 * n) + [SEM_SPEC, SEM_SPEC, pl.BlockSpec(memory_space=pl.ANY)],
        out_specs=[HBM_SPEC] * (2 * n) + [pl.BlockSpec(memory_space=pltpu.VMEM)],
        input_output_aliases={i: i for i in range(2 * n)},
        compiler_params=pltpu.CompilerParams(has_side_effects=SIDE_EFFECT),
    )(*thru, send_sems, recv_sems, after)
    return outs[n:]


def _after(value, token):
    return value + token[0, 0]


def _mm_fwd(x, sc, sh, bias, wg, widxs, name):
    s_len, kdim = x.shape
    n = wg.shape[-1]
    ncol = NDEV * n
    tm = _tile(s_len, 256)
    nw = len(widxs)

    def body(x_ref, sc_ref, sh_ref, b_ref, *rest):
        w_refs, o_refs = rest[:nw], rest[nw:]
        h = (x_ref[...] * (1.0 + sc_ref[...]) + sh_ref[...]).astype(BF16)
        for w_ref, o_ref in zip(w_refs, o_refs):
            for k in range(NDEV):
                cols = slice(k * n, (k + 1) * n)
                o_ref[:, cols] = jnp.dot(h, w_ref[k], preferred_element_type=F32) + b_ref[:, cols]

    w_specs = [pl.BlockSpec((NDEV, None, kdim, n), functools.partial(lambda i, w: (0, w, 0, 0), w=w))
               for w in widxs]
    return pl.pallas_call(
        body, name=name, grid=(s_len // tm,),
        in_specs=[pl.BlockSpec((tm, kdim), lambda i: (i, 0)), _full((1, kdim)), _full((1, kdim)),
                  _full((1, ncol))] + w_specs,
        out_specs=[pl.BlockSpec((tm, ncol), lambda i: (i, 0))] * nw,
        out_shape=[jax.ShapeDtypeStruct((s_len, ncol), F32)] * nw,
        compiler_params=_cp("parallel"),
    )(x, sc, sh, bias, *([wg] * nw))


def _mm_ln(a, wg, r, ridx, xres, gate, gam, bet, bias, alpha, name):
    s_len = a.shape[0]
    d = wg.shape[-1]
    tm = _tile(s_len, 256)

    def body(a_ref, w_ref, x_ref, g_ref, gam_ref, bet_ref, b_ref, y_ref, xo_ref, xh_ref, rs_ref):
        acc = None
        for k in range(NDEV):
            p = jnp.dot(a_ref[:, k * r:(k + 1) * r], w_ref[k], preferred_element_type=F32)
            acc = p if acc is None else acc + p
        y = acc + b_ref[...]
        z = alpha * x_ref[...] + g_ref[...] * y
        mu = jnp.mean(z, axis=-1, keepdims=True)
        zc = z - mu
        var = jnp.mean(zc * zc, axis=-1, keepdims=True)
        rstd = lax.rsqrt(var + LN_EPS)
        xh = zc * rstd
        y_ref[...] = y
        xh_ref[...] = xh
        rs_ref[...] = rstd
        xo_ref[...] = xh * gam_ref[...] + bet_ref[...]

    row = pl.BlockSpec((tm, d), lambda i: (i, 0))
    vec = _full((1, d))
    return pl.pallas_call(
        body, name=name, grid=(s_len // tm,),
        in_specs=[pl.BlockSpec((tm, NDEV * r), lambda i: (i, 0)),
                  pl.BlockSpec((NDEV, r, d), lambda i: (0, ridx, 0)), row, vec, vec, vec, vec],
        out_specs=[row, row, row, pl.BlockSpec((tm, 1), lambda i: (i, 0))],
        out_shape=[jax.ShapeDtypeStruct((s_len, d), F32)] * 3 + [jax.ShapeDtypeStruct((s_len, 1), F32)],
        compiler_params=_cp("parallel"),
    )(a, wg, xres, gate, gam, bet, bias)


def _ln_bwd(dxo, xh, rstd, gam, y, gate, alpha, name):
    s_len, d = dxo.shape
    tm = _tile(s_len, 256)

    def body(d_ref, xh_ref, rs_ref, gam_ref, y_ref, g_ref, dy_ref, dres_ref, acc_ref):
        @pl.when(pl.program_id(0) == 0)
        def _():
            acc_ref[...] = jnp.zeros_like(acc_ref)

        dxo_t = d_ref[...]
        xh_t = xh_ref[...]
        dxh = dxo_t * gam_ref[...]
        m1 = jnp.mean(dxh, axis=-1, keepdims=True)
        m2 = jnp.mean(dxh * xh_t, axis=-1, keepdims=True)
        dz = rs_ref[...] * (dxh - m1 - xh_t * m2)
        dy = g_ref[...] * dz
        dy_ref[...] = dy.astype(BF16)
        dres_ref[...] = alpha * dz
        acc_ref[0:1, :] += jnp.sum(dxo_t * xh_t, axis=0, keepdims=True)
        acc_ref[1:2, :] += jnp.sum(dxo_t, axis=0, keepdims=True)
        acc_ref[2:3, :] += jnp.sum(dz * y_ref[...], axis=0, keepdims=True)
        acc_ref[3:4, :] += jnp.sum(dy, axis=0, keepdims=True)

    row = pl.BlockSpec((tm, d), lambda i: (i, 0))
    vec = _full((1, d))
    return pl.pallas_call(
        body, name=name, grid=(s_len // tm,),
        in_specs=[row, row, pl.BlockSpec((tm, 1), lambda i: (i, 0)), vec, row, vec],
        out_specs=[row, row, _full((SUBLANES, d))],
        out_shape=[jax.ShapeDtypeStruct((s_len, d), BF16), jax.ShapeDtypeStruct((s_len, d), F32),
                   jax.ShapeDtypeStruct((SUBLANES, d), F32)],
        compiler_params=_cp("arbitrary"),
    )(dxo, xh, rstd, gam, y, gate)


def _mm_nt_row(dy, wg, r, ridx, name):
    s_len, d = dy.shape
    tm = _tile(s_len, 256)

    def body(dy_ref, w_ref, o_ref):
        g = dy_ref[...]
        for k in range(NDEV):
            o_ref[:, k * r:(k + 1) * r] = lax.dot_general(g, w_ref[k], NT, preferred_element_type=F32)

    return pl.pallas_call(
        body, name=name, grid=(s_len // tm,),
        in_specs=[pl.BlockSpec((tm, d), lambda i: (i, 0)), pl.BlockSpec((NDEV, r, d), lambda i: (0, ridx, 0))],
        out_specs=pl.BlockSpec((tm, NDEV * r), lambda i: (i, 0)),
        out_shape=jax.ShapeDtypeStruct((s_len, NDEV * r), F32),
        compiler_params=_cp("parallel"),
    )(dy, wg)


def _mm_nt_mod(dos, wg, widxs, xin, sc, dres, name):
    s_len, kdim = xin.shape
    n = wg.shape[-1]
    tm = _tile(s_len, 256)
    nw = len(widxs)

    def body(*refs):
        do_refs, w_refs = refs[:nw], refs[nw:2 * nw]
        x_ref, sc_ref, dres_ref, dx_ref, acc_ref = refs[2 * nw:]

        @pl.when(pl.program_id(0) == 0)
        def _():
            acc_ref[...] = jnp.zeros_like(acc_ref)

        dh = None
        for do_ref, w_ref in zip(do_refs, w_refs):
            for k in range(NDEV):
                p = lax.dot_general(do_ref[:, k * n:(k + 1) * n], w_ref[k], NT, preferred_element_type=F32)
                dh = p if dh is None else dh + p
        dx_ref[...] = dh * (1.0 + sc_ref[...]) + dres_ref[...]
        acc_ref[0:1, :] += jnp.sum(dh * x_ref[...], axis=0, keepdims=True)
        acc_ref[1:2, :] += jnp.sum(dh, axis=0, keepdims=True)

    row = pl.BlockSpec((tm, kdim), lambda i: (i, 0))
    w_specs = [pl.BlockSpec((NDEV, None, kdim, n), functools.partial(lambda i, w: (0, w, 0, 0), w=w))
               for w in widxs]
    return pl.pallas_call(
        body, name=name, grid=(s_len // tm,),
        in_specs=[pl.BlockSpec((tm, NDEV * n), lambda i: (i, 0))] * nw + w_specs + [row, _full((1, kdim)), row],
        out_specs=[row, _full((SUBLANES, kdim))],
        out_shape=[jax.ShapeDtypeStruct((s_len, kdim), F32), jax.ShapeDtypeStruct((SUBLANES, kdim), F32)],
        compiler_params=_cp("arbitrary"),
    )(*dos, *([wg] * nw), xin, sc, dres)


def _mm_tn_col(x, sc, sh, do, name):
    s_len, kdim = x.shape
    n = do.shape[1] // NDEV
    ts = _tile(s_len, 512)
    nsteps = s_len // ts

    def body(x_ref, sc_ref, sh_ref, do_ref, o_ref, acc_ref):
        @pl.when(pl.program_id(0) == 0)
        def _():
            acc_ref[...] = jnp.zeros_like(acc_ref)

        h = (x_ref[...] * (1.0 + sc_ref[...]) + sh_ref[...]).astype(BF16)
        for k in range(NDEV):
            acc_ref[k] += lax.dot_general(h, do_ref[:, k * n:(k + 1) * n], TN, preferred_element_type=F32)

        @pl.when(pl.program_id(0) == nsteps - 1)
        def _():
            o_ref[...] = acc_ref[...].astype(BF16)

    return pl.pallas_call(
        body, name=name, grid=(nsteps,),
        in_specs=[pl.BlockSpec((ts, kdim), lambda i: (i, 0)), _full((1, kdim)), _full((1, kdim)),
                  pl.BlockSpec((ts, NDEV * n), lambda i: (i, 0))],
        out_specs=_full((NDEV, kdim, n)),
        out_shape=jax.ShapeDtypeStruct((NDEV, kdim, n), BF16),
        scratch_shapes=[pltpu.VMEM((NDEV, kdim, n), F32)],
        compiler_params=_cp("arbitrary"),
    )(x, sc, sh, do)


def _mm_tn_row(a, dy, r, name):
    s_len, d = dy.shape
    ts = _tile(s_len, 512)
    nsteps = s_len // ts

    def body(a_ref, dy_ref, o_ref, acc_ref):
        @pl.when(pl.program_id(0) == 0)
        def _():
            acc_ref[...] = jnp.zeros_like(acc_ref)

        g = dy_ref[...]
        for k in range(NDEV):
            acc_ref[k] += lax.dot_general(a_ref[:, k * r:(k + 1) * r], g, TN, preferred_element_type=F32)

        @pl.when(pl.program_id(0) == nsteps - 1)
        def _():
            o_ref[...] = acc_ref[...].astype(BF16)

    return pl.pallas_call(
        body, name=name, grid=(nsteps,),
        in_specs=[pl.BlockSpec((ts, NDEV * r), lambda i: (i, 0)), pl.BlockSpec((ts, d), lambda i: (i, 0))],
        out_specs=_full((NDEV, r, d)),
        out_shape=jax.ShapeDtypeStruct((NDEV, r, d), BF16),
        scratch_shapes=[pltpu.VMEM((NDEV, r, d), F32)],
        compiler_params=_cp("arbitrary"),
    )(a, dy)


def _prev_spec(ts, pad, cb, col):
    return pl.BlockSpec((pad, cb), lambda *g: (jnp.maximum(g[-1] * (ts // pad) - 1, 0), col(g)))


def _next_spec(ts, pad, cb, col, s_len):
    return pl.BlockSpec((pad, cb), lambda *g: (jnp.minimum((g[-1] + 1) * (ts // pad), s_len // pad - 1), col(g)))


def _conv_fwd_rows(buf_ref, w_ref, b_ref, ktaps, pad, r0, rows):
    acc = None
    for j in range(ktaps):
        term = w_ref[ktaps - 1 - j:ktaps - j, :] * buf_ref[pad - j + r0:pad - j + r0 + rows, :]
        acc = term if acc is None else acc + term
    return acc + b_ref[...]


def _conv_bwd_rows(dbuf_ref, x_rows, w_ref, dwacc_ref, ktaps, r0, rows):
    acc = None
    for j in range(ktaps):
        sl = dbuf_ref[j + r0:j + r0 + rows, :]
        term = w_ref[ktaps - 1 - j:ktaps - j, :] * sl
        acc = term if acc is None else acc + term
        prod = x_rows * sl
        fold = prod[0:SUBLANES]
        for q in range(1, rows // SUBLANES):
            fold = fold + prod[q * SUBLANES:(q + 1) * SUBLANES]
        tap = ktaps - 1 - j
        dwacc_ref[tap * SUBLANES:(tap + 1) * SUBLANES, :] += fold
    return acc


def _flush_dw(dwacc_ref, dw_ref, ktaps):
    for tap in range(ktaps):
        dw_ref[tap:tap + 1, :] = jnp.sum(dwacc_ref[tap * SUBLANES:(tap + 1) * SUBLANES, :], axis=0, keepdims=True)


def _gateconv_fwd(bcv, cw, cb, name):
    s_len, d3 = bcv.shape
    d = d3 // 3
    ktaps = cw.shape[0]
    pad = SHORT_PAD
    ts = _tile(s_len, 256)

    def body(gb_ref, gc_ref, v_ref, gcp_ref, vp_ref, w_ref, b_ref, o_ref, pbuf):
        s = pl.program_id(0)
        pbuf[0:pad, :] = jnp.where(s > 0, gcp_ref[...] * vp_ref[...], 0.0)
        pbuf[pad:pad + ts, :] = gc_ref[...] * v_ref[...]
        for r0 in range(0, ts, CHUNK):
            q = _conv_fwd_rows(pbuf, w_ref, b_ref, ktaps, pad, r0, CHUNK)
            o_ref[r0:r0 + CHUNK, :] = (gb_ref[r0:r0 + CHUNK, :] * q).astype(BF16)

    def cur(part):
        return pl.BlockSpec((ts, d), lambda s: (s, part))

    return pl.pallas_call(
        body, name=name, grid=(s_len // ts,),
        in_specs=[cur(0), cur(1), cur(2),
                  _prev_spec(ts, pad, d, lambda g: 1), _prev_spec(ts, pad, d, lambda g: 2),
                  _full((ktaps, d)), _full((1, d))],
        out_specs=pl.BlockSpec((ts, d), lambda s: (s, 0)),
        out_shape=jax.ShapeDtypeStruct((s_len, d), BF16),
        scratch_shapes=[pltpu.VMEM((pad + ts, d), F32)],
        compiler_params=_cp("parallel"),
    )(bcv, bcv, bcv, bcv, bcv, cw, cb)


def _gateconv_bwd(bcv, dy0, cw, cb, name):
    s_len, d3 = bcv.shape
    d = d3 // 3
    ktaps = cw.shape[0]
    pad = SHORT_PAD
    ts = _tile(s_len, 256)
    nsteps = s_len // ts

    def body(gb_ref, gc_ref, v_ref, gcp_ref, vp_ref, gbn_ref, dy_ref, dyn_ref, w_ref, b_ref,
             o_ref, dw_ref, db_ref, pbuf, dqbuf, dwacc):
        s = pl.program_id(0)

        @pl.when(s == 0)
        def _():
            dwacc[...] = jnp.zeros_like(dwacc)
            db_ref[...] = jnp.zeros_like(db_ref)

        pbuf[0:pad, :] = jnp.where(s > 0, gcp_ref[...] * vp_ref[...], 0.0)
        pbuf[pad:pad + ts, :] = gc_ref[...] * v_ref[...]
        dq = dy_ref[...] * gb_ref[...]
        dqbuf[0:ts, :] = dq
        dqbuf[ts:ts + pad, :] = jnp.where(s < nsteps - 1, dyn_ref[...] * gbn_ref[...], 0.0)
        db_ref[...] += jnp.sum(dq, axis=0, keepdims=True)
        for r0 in range(0, ts, CHUNK):
            rows = slice(r0, r0 + CHUNK)
            q = _conv_fwd_rows(pbuf, w_ref, b_ref, ktaps, pad, r0, CHUNK)
            o_ref[rows, 0:d] = (dy_ref[rows, :] * q).astype(BF16)
            dp = _conv_bwd_rows(dqbuf, pbuf[pad + r0:pad + r0 + CHUNK, :], w_ref, dwacc, ktaps, r0, CHUNK)
            o_ref[rows, d:2 * d] = (dp * v_ref[rows, :]).astype(BF16)
            o_ref[rows, 2 * d:3 * d] = (dp * gc_ref[rows, :]).astype(BF16)

        @pl.when(s == nsteps - 1)
        def _():
            _flush_dw(dwacc, dw_ref, ktaps)

    def cur(part):
        return pl.BlockSpec((ts, d), lambda s: (s, part))

    return pl.pallas_call(
        body, name=name, grid=(nsteps,),
        in_specs=[cur(0), cur(1), cur(2),
                  _prev_spec(ts, pad, d, lambda g: 1), _prev_spec(ts, pad, d, lambda g: 2),
                  _next_spec(ts, pad, d, lambda g: 0, s_len),
                  cur(0), _next_spec(ts, pad, d, lambda g: 0, s_len),
                  _full((ktaps, d)), _full((1, d))],
        out_specs=[pl.BlockSpec((ts, d3), lambda s: (s, 0)), _full((ktaps, d)), _full((1, d))],
        out_shape=[jax.ShapeDtypeStruct((s_len, d3), BF16), jax.ShapeDtypeStruct((ktaps, d), F32),
                   jax.ShapeDtypeStruct((1, d), F32)],
        scratch_shapes=[pltpu.VMEM((pad + ts, d), F32), pltpu.VMEM((ts + pad, d), F32),
                        pltpu.VMEM((ktaps * SUBLANES, d), F32)],
        compiler_params=_cp("arbitrary"),
    )(bcv, bcv, bcv, bcv, bcv, bcv, dy0, dy0, cw, cb)


def _ffn_mid_fwd(u0, vg, cw, cb, name):
    s_len, f = u0.shape
    ktaps = cw.shape[0]
    pad = SHORT_PAD
    ts = _tile(s_len, 256)
    cbk = 1024 if f % 1024 == 0 else f

    def body(u_ref, up_ref, vg_ref, w_ref, b_ref, o_ref, ubuf):
        s = pl.program_id(1)
        ubuf[0:pad, :] = jnp.where(s > 0, up_ref[...], 0.0)
        ubuf[pad:pad + ts, :] = u_ref[...]
        for r0 in range(0, ts, CHUNK):
            u = _conv_fwd_rows(ubuf, w_ref, b_ref, ktaps, pad, r0, CHUNK)
            o_ref[r0:r0 + CHUNK, :] = (u * _sigmoid(u) * vg_ref[r0:r0 + CHUNK, :]).astype(BF16)

    cur = pl.BlockSpec((ts, cbk), lambda c, s: (s, c))
    return pl.pallas_call(
        body, name=name, grid=(f // cbk, s_len // ts),
        in_specs=[cur, _prev_spec(ts, pad, cbk, lambda g: g[0]), cur,
                  pl.BlockSpec((ktaps, cbk), lambda c, s: (0, c)), pl.BlockSpec((1, cbk), lambda c, s: (0, c))],
        out_specs=cur,
        out_shape=jax.ShapeDtypeStruct((s_len, f), BF16),
        scratch_shapes=[pltpu.VMEM((pad + ts, cbk), F32)],
        compiler_params=_cp("parallel", "parallel"),
    )(u0, u0, vg, cw, cb)


def _ffn_mid_bwd(u0, vg, dt, cw, cb, name):
    s_len, f = u0.shape
    ktaps = cw.shape[0]
    pad = SHORT_PAD
    ts = _tile(s_len, 256)
    nsteps = s_len // ts
    cbk = 1024 if f % 1024 == 0 else f

    def body(u_ref, up_ref, un_ref, vg_ref, vgn_ref, dt_ref, dtn_ref, w_ref, b_ref,
             du0_ref, dvg_ref, dw_ref, db_ref, ubuf, dubuf, dwacc):
        s = pl.program_id(1)

        @pl.when(s == 0)
        def _():
            dwacc[...] = jnp.zeros_like(dwacc)
            db_ref[...] = jnp.zeros_like(db_ref)

        ubuf[0:pad, :] = jnp.where(s > 0, up_ref[...], 0.0)
        ubuf[pad:pad + ts, :] = u_ref[...]
        ubuf[pad + ts:pad + ts + pad, :] = un_ref[...]
        last = s == nsteps - 1
        for r0 in range(0, ts + pad, CHUNK):
            u = _conv_fwd_rows(ubuf, w_ref, b_ref, ktaps, pad, r0, CHUNK)
            sg = _sigmoid(u)
            if r0 < ts:
                rows = slice(r0, r0 + CHUNK)
                dtr, vgr = dt_ref[rows, :], vg_ref[rows, :]
                dvg_ref[rows, :] = (dtr * u * sg).astype(BF16)
            else:
                rows = slice(r0 - ts, r0 - ts + CHUNK)
                dtr, vgr = jnp.where(last, 0.0, dtn_ref[rows, :]), vgn_ref[rows, :]
            dubuf[r0:r0 + CHUNK, :] = dtr * vgr * (sg * (1.0 + u * (1.0 - sg)))
        db_ref[...] += jnp.sum(dubuf[0:ts, :], axis=0, keepdims=True)
        for r0 in range(0, ts, CHUNK):
            du0 = _conv_bwd_rows(dubuf, u_ref[r0:r0 + CHUNK, :], w_ref, dwacc, ktaps, r0, CHUNK)
            du0_ref[r0:r0 + CHUNK, :] = du0.astype(BF16)

        @pl.when(last)
        def _():
            _flush_dw(dwacc, dw_ref, ktaps)

    cur = pl.BlockSpec((ts, cbk), lambda c, s: (s, c))
    prv = _prev_spec(ts, pad, cbk, lambda g: g[0])
    nxt = _next_spec(ts, pad, cbk, lambda g: g[0], s_len)
    return pl.pallas_call(
        body, name=name, grid=(f // cbk, nsteps),
        in_specs=[cur, prv, nxt, cur, nxt, cur, nxt,
                  pl.BlockSpec((ktaps, cbk), lambda c, s: (0, c)), pl.BlockSpec((1, cbk), lambda c, s: (0, c))],
        out_specs=[cur, cur, pl.BlockSpec((ktaps, cbk), lambda c, s: (0, c)),
                   pl.BlockSpec((1, cbk), lambda c, s: (0, c))],
        out_shape=[jax.ShapeDtypeStruct((s_len, f), BF16), jax.ShapeDtypeStruct((s_len, f), BF16),
                   jax.ShapeDtypeStruct((ktaps, f), F32), jax.ShapeDtypeStruct((1, f), F32)],
        scratch_shapes=[pltpu.VMEM((pad + ts + pad, cbk), F32), pltpu.VMEM((ts + pad, cbk), F32),
                        pltpu.VMEM((ktaps * SUBLANES, cbk), F32)],
        compiler_params=_cp("parallel", "arbitrary"),
    )(u0, u0, u0, vg, vg, dt, dt, cw, cb)


def _b_mid_fwd(ub, cw, cb, lng, lnb, name):
    s_len, d2 = ub.shape
    d = d2 // 2
    ktaps = cw.shape[0]
    pad = LONG_PAD
    ts = _tile(s_len, 256)

    def body(a_ref, g_ref, ap_ref, gp_ref, w_ref, b_ref, lng_ref, lnb_ref, a2_ref, a4_ref, abuf):
        s = pl.program_id(0)
        abuf[0:pad, :] = jnp.where(s > 0, ap_ref[...] * _sigmoid(gp_ref[...]), 0.0)
        abuf[pad:pad + ts, :] = a_ref[...] * _sigmoid(g_ref[...])
        for r0 in range(0, ts, CHUNK):
            a2_ref[r0:r0 + CHUNK, :] = _conv_fwd_rows(abuf, w_ref, b_ref, ktaps, pad, r0, CHUNK)
        a2 = a2_ref[...]
        mu = jnp.mean(a2, axis=-1, keepdims=True)
        ac = a2 - mu
        var = jnp.mean(ac * ac, axis=-1, keepdims=True)
        a3 = ac * lax.rsqrt(var + LN_EPS) * lng_ref[...] + lnb_ref[...]
        a4_ref[...] = (a3 * _sigmoid(a3)).astype(BF16)

    def cur(part):
        return pl.BlockSpec((ts, d), lambda s: (s, part))

    vec = _full((1, d))
    return pl.pallas_call(
        body, name=name, grid=(s_len // ts,),
        in_specs=[cur(0), cur(1), _prev_spec(ts, pad, d, lambda g: 0), _prev_spec(ts, pad, d, lambda g: 1),
                  _full((ktaps, d)), vec, vec, vec],
        out_specs=[cur(0), cur(0)],
        out_shape=[jax.ShapeDtypeStruct((s_len, d), F32), jax.ShapeDtypeStruct((s_len, d), BF16)],
        scratch_shapes=[pltpu.VMEM((pad + ts, d), F32)],
        compiler_params=_cp("parallel"),
    )(ub, ub, ub, ub, cw, cb, lng, lnb)


def _b_mid_bwd(ub, a2, da4, cw, lng, lnb, name):
    s_len, d2 = ub.shape
    d = d2 // 2
    ktaps = cw.shape[0]
    pad = LONG_PAD
    ts = _tile(s_len, 256)
    nsteps = s_len // ts

    def body(a_ref, g_ref, a2_ref, a2n_ref, da4_ref, da4n_ref, w_ref, lng_ref, lnb_ref,
             du_ref, dw_ref, db_ref, dlng_ref, dlnb_ref, dbias_ref, dabuf, dwacc):
        s = pl.program_id(0)
        last = s == nsteps - 1

        @pl.when(s == 0)
        def _():
            dwacc[...] = jnp.zeros_like(dwacc)
            for ref in (db_ref, dlng_ref, dlnb_ref, dbias_ref):
                ref[...] = jnp.zeros_like(ref)

        def ln_silu_bwd(a2_t, da4_t):
            mu = jnp.mean(a2_t, axis=-1, keepdims=True)
            ac = a2_t - mu
            var = jnp.mean(ac * ac, axis=-1, keepdims=True)
            rstd = lax.rsqrt(var + LN_EPS)
            ah = ac * rstd
            a3 = ah * lng_ref[...] + lnb_ref[...]
            sg = _sigmoid(a3)
            da3 = da4_t * (sg * (1.0 + a3 * (1.0 - sg)))
            dah = da3 * lng_ref[...]
            m1 = jnp.mean(dah, axis=-1, keepdims=True)
            m2 = jnp.mean(dah * ah, axis=-1, keepdims=True)
            return rstd * (dah - m1 - ah * m2), da3, ah

        da2, da3, ah = ln_silu_bwd(a2_ref[...], da4_ref[...])
        dabuf[0:ts, :] = da2
        dlng_ref[...] += jnp.sum(da3 * ah, axis=0, keepdims=True)
        dlnb_ref[...] += jnp.sum(da3, axis=0, keepdims=True)
        db_ref[...] += jnp.sum(da2, axis=0, keepdims=True)
        da2n, _, _ = ln_silu_bwd(a2n_ref[...], jnp.where(last, 0.0, da4n_ref[...]))
        dabuf[ts:ts + pad, :] = da2n
        for r0 in range(0, ts, CHUNK):
            rows = slice(r0, r0 + CHUNK)
            a_r, g_r = a_ref[rows, :], g_ref[rows, :]
            sg = _sigmoid(g_r)
            da1 = _conv_bwd_rows(dabuf, a_r * sg, w_ref, dwacc, ktaps, r0, CHUNK)
            da = da1 * sg
            dg = da1 * a_r * sg * (1.0 - sg)
            du_ref[rows, 0:d] = da.astype(BF16)
            du_ref[rows, d:2 * d] = dg.astype(BF16)
            dbias_ref[:, 0:d] += jnp.sum(da, axis=0, keepdims=True)
            dbias_ref[:, d:2 * d] += jnp.sum(dg, axis=0, keepdims=True)

        @pl.when(last)
        def _():
            _flush_dw(dwacc, dw_ref, ktaps)

    def cur(part):
        return pl.BlockSpec((ts, d), lambda s: (s, part))

    vec = _full((1, d))
    nxt = _next_spec(ts, pad, d, lambda g: 0, s_len)
    return pl.pallas_call(
        body, name=name, grid=(nsteps,),
        in_specs=[cur(0), cur(1), cur(0), nxt, cur(0), nxt, _full((ktaps, d)), vec, vec],
        out_specs=[pl.BlockSpec((ts, d2), lambda s: (s, 0)), _full((ktaps, d)), vec, vec, vec, _full((1, d2))],
        out_shape=[jax.ShapeDtypeStruct((s_len, d2), BF16), jax.ShapeDtypeStruct((ktaps, d), F32),
                   jax.ShapeDtypeStruct((1, d), F32), jax.ShapeDtypeStruct((1, d), F32),
                   jax.ShapeDtypeStruct((1, d), F32), jax.ShapeDtypeStruct((1, d2), F32)],
        scratch_shapes=[pltpu.VMEM((ts + pad, d), F32), pltpu.VMEM((ktaps * SUBLANES, d), F32)],
        compiler_params=_cp("arbitrary"),
    )(ub, ub, a2, a2, da4, da4, cw, lng, lnb)


def _loss_head(xo, tgt, name):
    s_len, d = xo.shape
    tm = _tile(s_len, 512)

    def body(x_ref, t_ref, d_ref, l_ref):
        @pl.when(pl.program_id(0) == 0)
        def _():
            l_ref[...] = jnp.zeros_like(l_ref)

        e = x_ref[...] - t_ref[...]
        d_ref[...] = e * (1.0 / d)
        per_row = jnp.sum(e * e, axis=-1, keepdims=True) * (1.0 / d)
        l_ref[...] += 0.5 * jnp.sum(per_row, axis=0, keepdims=True)

    row = pl.BlockSpec((tm, d), lambda i: (i, 0))
    return pl.pallas_call(
        body, name=name, grid=(s_len // tm,),
        in_specs=[row, row], out_specs=[row, _full((1, LANES))],
        out_shape=[jax.ShapeDtypeStruct((s_len, d), F32), jax.ShapeDtypeStruct((1, LANES), F32)],
        compiler_params=_cp("arbitrary"),
    )(xo, tgt)


def _ada_fwd(c_all, ada_w, ada_b_loc, name):
    depth, d, n = ada_w.shape

    def body(c_ref, w_ref, b_ref, o_ref):
        c = c_ref[...]
        act = c * _sigmoid(c)
        o_ref[...] = jnp.dot(act, w_ref[...], preferred_element_type=F32,
                             precision=lax.Precision.HIGHEST) + b_ref[...]

    return pl.pallas_call(
        body, name=name, grid=(depth,),
        in_specs=[_full((NDEV, d)), pl.BlockSpec((None, d, n), lambda i: (i, 0, 0)),
                  pl.BlockSpec((None, 1, n), lambda i: (i, 0, 0))],
        out_specs=pl.BlockSpec((None, NDEV, n), lambda i: (i, 0, 0)),
        out_shape=jax.ShapeDtypeStruct((depth, NDEV, n), F32),
        compiler_params=_cp("parallel"),
    )(c_all, ada_w, ada_b_loc.reshape(depth, 1, n))


def _ada_bwd(c_all_t, dmod_cols, name):
    depth, _, n = dmod_cols.shape
    d = c_all_t.shape[0]

    def body(ct_ref, dm_ref, o_ref):
        ct = ct_ref[...]
        act = ct * _sigmoid(ct)
        acc = None
        for b in range(NDEV):
            term = act[:, b:b + 1] * dm_ref[b:b + 1, :]
            acc = term if acc is None else acc + term
        o_ref[...] = acc

    return pl.pallas_call(
        body, name=name, grid=(depth,),
        in_specs=[_full((d, NDEV)), pl.BlockSpec((None, NDEV, n), lambda i: (i, 0, 0))],
        out_specs=pl.BlockSpec((None, d, n), lambda i: (i, 0, 0)),
        out_shape=jax.ShapeDtypeStruct((depth, d, n), F32),
        compiler_params=_cp("parallel"),
    )(c_all_t, dmod_cols)


def _sum_parts(parts, name):
    _, rows, lanes = parts.shape

    def body(p_ref, o_ref):
        acc = p_ref[0]
        for k in range(1, NDEV):
            acc = acc + p_ref[k]
        o_ref[...] = acc

    return pl.pallas_call(
        body, name=name, in_specs=[_full(parts.shape)], out_specs=_full((rows, lanes)), grid=(1,),
        out_shape=jax.ShapeDtypeStruct((rows, lanes), F32), compiler_params=_cp("arbitrary"),
    )(parts)


def _adamw(w, glist, m, v, name):
    nl, rows, cols = w.shape
    tr = _tile(rows, 256, 2 * SUBLANES)

    def body(w_ref, *rest):
        g_refs = rest[:nl]
        m_ref, v_ref, go_ref, d_ref, mo_ref, vo_ref = rest[nl:]
        g = None
        for layer, g_ref in enumerate(g_refs):
            part = g_ref[0].astype(F32)
            for p in range(1, g_ref.shape[0]):
                part = part + g_ref[p].astype(F32)
            g = part if g is None else jnp.where(pl.program_id(0) == layer, part, g)
        m1 = ADAM_B1 * m_ref[...] + (1.0 - ADAM_B1) * g
        v1 = ADAM_B2 * v_ref[...] + (1.0 - ADAM_B2) * (g * g)
        m_hat = m1 / (1.0 - ADAM_B1 ** ADAM_STEP)
        v_hat = v1 / (1.0 - ADAM_B2 ** ADAM_STEP)
        go_ref[...] = g
        mo_ref[...] = m1
        vo_ref[...] = v1
        d_ref[...] = -ADAM_LR * (m_hat / (jnp.sqrt(v_hat) + ADAM_EPS) + ADAM_WD * w_ref[...])

    blk = pl.BlockSpec((None, tr, cols), lambda l, i: (l, i, 0))
    g_specs = [pl.BlockSpec((g.shape[0], tr, cols), lambda l, i: (0, i, 0)) for g in glist]
    return pl.pallas_call(
        body, name=name, grid=(nl, rows // tr),
        in_specs=[blk] + g_specs + [blk, blk],
        out_specs=[blk] * 4, out_shape=[jax.ShapeDtypeStruct((nl, rows, cols), F32)] * 4,
        compiler_params=_cp("parallel", "parallel"),
    )(w, *glist, m, v)


def _pack(pieces):
    flat = jnp.concatenate([p.reshape(-1) for p in pieces])
    unit = SUBLANES * LANES
    padded = -(-flat.shape[0] // unit) * unit
    return jnp.pad(flat, (0, padded - flat.shape[0])).reshape(padded // LANES, LANES)


def _unpack(packed, shapes, lead=()):
    flat = packed.reshape(lead + (-1,))
    out, off = [], 0
    for s in shapes:
        size = 1
        for dim in s:
            size *= dim
        out.append(flat[..., off:off + size].reshape(lead + tuple(s)))
        off += size
    return out


def _pad_last(a, n):
    return jnp.pad(a, [(0, 0)] * (a.ndim - 1) + [(0, n - a.shape[-1])])


def kernel(x, c, ada_w, ada_b, ln_tok_g, ln_tok_b, ln_ch_g, ln_ch_b, a_w_in, a_conv_w, a_conv_b, a_w_out, b_w_pw1, b_b_pw1, b_conv_w, b_conv_b, b_ln_g, b_ln_b, b_w_pw2, b_b_pw2, f_w_up, f_conv_w, f_conv_b, f_w_gate, f_w_down, loss_target, m_ada_w, m_ada_b, m_ln_tok_g, m_ln_tok_b, m_ln_ch_g, m_ln_ch_b, m_a_w_in, m_a_conv_w, m_a_conv_b, m_a_w_out, m_b_w_pw1, m_b_b_pw1, m_b_conv_w, m_b_conv_b, m_b_ln_g, m_b_ln_b, m_b_w_pw2, m_b_b_pw2, m_f_w_up, m_f_conv_w, m_f_conv_b, m_f_w_gate, m_f_w_down, v_ada_w, v_ada_b, v_ln_tok_g, v_ln_tok_b, v_ln_ch_g, v_ln_ch_b, v_a_w_in, v_a_conv_w, v_a_conv_b, v_a_w_out, v_b_w_pw1, v_b_b_pw1, v_b_conv_w, v_b_conv_b, v_b_ln_g, v_b_ln_b, v_b_w_pw2, v_b_b_pw2, v_f_w_up, v_f_conv_w, v_f_conv_b, v_f_w_gate, v_f_w_down):
    weights = dict(ada_w=ada_w, ada_b=ada_b, ln_tok_g=ln_tok_g, ln_tok_b=ln_tok_b, ln_ch_g=ln_ch_g, ln_ch_b=ln_ch_b, a_w_in=a_w_in, a_conv_w=a_conv_w, a_conv_b=a_conv_b, a_w_out=a_w_out, b_w_pw1=b_w_pw1, b_b_pw1=b_b_pw1, b_conv_w=b_conv_w, b_conv_b=b_conv_b, b_ln_g=b_ln_g, b_ln_b=b_ln_b, b_w_pw2=b_w_pw2, b_b_pw2=b_b_pw2, f_w_up=f_w_up, f_conv_w=f_conv_w, f_conv_b=f_conv_b, f_w_gate=f_w_gate, f_w_down=f_w_down)
    mom_m = dict(ada_w=m_ada_w, ada_b=m_ada_b, ln_tok_g=m_ln_tok_g, ln_tok_b=m_ln_tok_b, ln_ch_g=m_ln_ch_g, ln_ch_b=m_ln_ch_b, a_w_in=m_a_w_in, a_conv_w=m_a_conv_w, a_conv_b=m_a_conv_b, a_w_out=m_a_w_out, b_w_pw1=m_b_w_pw1, b_b_pw1=m_b_b_pw1, b_conv_w=m_b_conv_w, b_conv_b=m_b_conv_b, b_ln_g=m_b_ln_g, b_ln_b=m_b_ln_b, b_w_pw2=m_b_w_pw2, b_b_pw2=m_b_b_pw2, f_w_up=m_f_w_up, f_conv_w=m_f_conv_w, f_conv_b=m_f_conv_b, f_w_gate=m_f_w_gate, f_w_down=m_f_w_down)
    mom_v = dict(ada_w=v_ada_w, ada_b=v_ada_b, ln_tok_g=v_ln_tok_g, ln_tok_b=v_ln_tok_b, ln_ch_g=v_ln_ch_g, ln_ch_b=v_ln_ch_b, a_w_in=v_a_w_in, a_conv_w=v_a_conv_w, a_conv_b=v_a_conv_b, a_w_out=v_a_w_out, b_w_pw1=v_b_w_pw1, b_b_pw1=v_b_b_pw1, b_conv_w=v_b_conv_w, b_conv_b=v_b_conv_b, b_ln_g=v_b_ln_g, b_ln_b=v_b_ln_b, b_w_pw2=v_b_w_pw2, b_b_pw2=v_b_b_pw2, f_w_up=v_f_w_up, f_conv_w=v_f_conv_w, f_conv_b=v_f_conv_b, f_w_gate=v_f_w_gate, f_w_down=v_f_w_down)
    names = list(weights)

    depth, d, n_ada = ada_w.shape
    assert depth == 2 and a_w_in.shape[0] == 1 and b_w_pw1.shape[0] == 1
    s_len = x.shape[1]
    f_loc = f_w_up.shape[-1]
    f_pad = -(-f_loc // LANES) * LANES
    f_all = NDEV * f_pad
    d_loc = d // NDEV
    ka, kb, kf = a_conv_w.shape[1], b_conv_w.shape[1], f_conv_w.shape[1]
    alpha = (2.0 * depth) ** 0.25
    assert a_w_in.shape[-1] == f_pad and f_pad % d_loc == 0
    me = 4 * lax.axis_index("x") + 2 * lax.axis_index("y") + lax.axis_index("c")

    small_shapes = [(d,), (ka, d_loc), (2 * d_loc,), (kb, d_loc), (d_loc,), (d_loc,), (d_loc,), (d_loc,),
                    (depth, kf, f_pad)]
    small_loc = _pack([c[0], a_conv_w[0], b_b_pw1[0], b_conv_w[0], b_conv_b[0], b_ln_g[0], b_ln_b[0],
                       b_b_pw2[0], _pad_last(f_conv_w, f_pad)])
    up_pad = _pad_last(f_w_up, f_pad).astype(BF16)
    gate_pad = _pad_last(f_w_gate, f_pad).astype(BF16)
    down_pad = jnp.pad(f_w_down, ((0, 0), (0, f_pad - f_loc), (0, 0))).astype(BF16)
    col_f = [jnp.stack([up_pad[i], gate_pad[i]]) for i in range(depth)]
    row_b = jnp.concatenate([down_pad[1], b_w_pw2[0].astype(BF16)], axis=0)
    ridx_pw2 = f_pad // d_loc
    g_small, g_in, g_out = _exchange([small_loc, a_w_in.astype(BF16), a_w_out[0].astype(BF16)], True, "gather_first")
    gather_f0 = _exchange_start([col_f[0], down_pad[0]], True, "gather_f0_start")

    (c_all, acw_g, bb1_g, bcw_g, bcb_g, blg_g, blb_g, bb2_g, fcw_g) = _unpack(g_small, small_shapes, (NDEV,))
    a_cw = acw_g.transpose(1, 0, 2).reshape(ka, d)
    b_cw = bcw_g.transpose(1, 0, 2).reshape(kb, d)
    b_b1 = bb1_g.reshape(1, 2 * d)
    b_cb, b_lg, b_lb, b_b2 = (t.reshape(1, d) for t in (bcb_g, blg_g, blb_g, bb2_g))
    f_cw = fcw_g.transpose(1, 2, 0, 3).reshape(depth, kf, f_all)
    f_cb = _pad_last(f_conv_b.reshape(depth, NDEV, f_loc), f_pad).reshape(depth, 1, f_all)

    ada_b_loc = lax.dynamic_slice(ada_b, (0, me * n_ada), (depth, n_ada))
    mod_part = _ada_fwd(c_all, ada_w, ada_b_loc, "ada_fwd")
    mod_g, = _exchange([mod_part.reshape(depth * NDEV, n_ada)], True, "gather_mod")
    mod_all = mod_g.reshape(NDEV, depth, NDEV, n_ada).transpose(1, 2, 0, 3).reshape(depth, NDEV, 6 * d)
    mod = lax.dynamic_slice(mod_all, (0, me, 0), (depth, 1, 6 * d))[:, 0]

    def mod_rows(i):
        return [mod[i:i + 1, j * d:(j + 1) * d] for j in range(6)]

    zeros_d = jnp.zeros((1, d), F32)
    zeros_f = jnp.zeros((1, f_all), F32)
    x0 = x[0]

    sh_t0, sc_t0, g_t0, sh_c0, sc_c0, g_c0 = mod_rows(0)
    sh_t1, sc_t1, g_t1, sh_c1, sc_c1, g_c1 = mod_rows(1)

    sc_t0 = _after(sc_t0, gather_f0[-1])
    bcv, = _mm_fwd(x0, sc_t0, sh_t0, jnp.zeros((1, 3 * d), F32), g_in, (0,), "a_in_fwd")
    y0 = _gateconv_fwd(bcv, a_cw, a_conv_b, "a_conv_fwd")
    y_a, x1, xh1, rs1 = _mm_ln(y0, g_out, d_loc, 0, x0, g_t0, ln_tok_g[0:1], ln_tok_b[0:1], zeros_d,
                               alpha, "a_out_ln_fwd")

    def ffn_fwd(xin, sc, sh, gate, gam, bet, g_colf, g_rowf, layer, tag):
        u0, vg = _mm_fwd(xin, sc, sh, zeros_f, g_colf, (0, 1), "f_upgate_fwd" + tag)
        t = _ffn_mid_fwd(u0, vg, f_cw[layer], f_cb[layer], "f_mid_fwd" + tag)
        y, xo, xh, rs = _mm_ln(t, g_rowf, f_pad, 0, xin, gate, gam, bet, zeros_d, alpha, "f_down_ln_fwd" + tag)
        return u0, vg, t, y, xo, xh, rs

    g_colf0, g_rowf0, landed = _exchange_wait(gather_f0, x1, True, "gather_f0_wait")
    gather_1 = _exchange_start([_after(b_w_pw1, landed).astype(BF16), col_f[1], row_b], True, "gather_1_start")
    sc_c0 = _after(sc_c0, gather_1[-1])
    u0_0, vg_0, t_0, y_f0, x2, xh2, rs2 = ffn_fwd(x1, sc_c0, sh_c0, g_c0, ln_ch_g[0:1], ln_ch_b[0:1],
                                                  g_colf0, g_rowf0, 0, "0")

    g_pw1, g_colf1, g_rowb, _ = _exchange_wait(gather_1, x2, True, "gather_1_wait")
    ub, = _mm_fwd(x2, sc_t1, sh_t1, b_b1, g_pw1, (0,), "b_pw1_fwd")
    a2, a4 = _b_mid_fwd(ub, b_cw, b_cb, b_lg, b_lb, "b_mid_fwd")
    y_b, x3, xh3, rs3 = _mm_ln(a4, g_rowb, d_loc, ridx_pw2, x2, g_t1, ln_tok_g[1:2], ln_tok_b[1:2], b_b2,
                               alpha, "b_pw2_ln_fwd")
    u0_1, vg_1, t_1, y_f1, x4, xh4, rs4 = ffn_fwd(x3, sc_c1, sh_c1, g_c1, ln_ch_g[1:2], ln_ch_b[1:2],
                                                  g_colf1, g_rowb, 1, "1")

    dx4, loss_part = _loss_head(x4, loss_target[0], "loss_head")
    loss = lax.psum(loss_part[0, 0], MESH_AXES)

    def ffn_bwd(dxo, xin, sc, sh, gate, gam, u0, vg, t, y, xh, rs, g_colf, g_rowf, layer, tag):
        dy, dres, acc = _ln_bwd(dxo, xh, rs, gam, y, gate, alpha, "f_ln_bwd" + tag)
        dt = _mm_nt_row(dy, g_rowf, f_pad, 0, "f_down_dx" + tag)
        dw_down = _mm_tn_row(t, dy, f_pad, "f_down_dw" + tag)
        du0, dvg, dcw, dcb = _ffn_mid_bwd(u0, vg, dt, f_cw[layer], f_cb[layer], "f_mid_bwd" + tag)
        dw_up = _mm_tn_col(xin, sc, sh, du0, "f_up_dw" + tag)
        dw_gate = _mm_tn_col(xin, sc, sh, dvg, "f_gate_dw" + tag)
        scatter = _exchange_start([dw_up[..., :f_loc], dw_gate[..., :f_loc], dw_down[:, :f_loc]], False,
                                  "scatter_f%s_start" % tag)
        dxin, acc2 = _mm_nt_mod([du0, dvg], g_colf, (0, 1), xin, _after(sc, scatter[-1]), dres, "f_upgate_dx" + tag)
        return dxin, acc, acc2, scatter, dcw, dcb

    dx3, accf1, acc2f1, scatter_f1, dfcw1, dfcb1 = ffn_bwd(
        dx4, x3, sc_c1, sh_c1, g_c1, ln_ch_g[1:2], u0_1, vg_1, t_1, y_f1, xh4, rs4, g_colf1, g_rowb, 1, "1")

    dy, dres, accb = _ln_bwd(dx3, xh3, rs3, ln_tok_g[1:2], y_b, g_t1, alpha, "b_ln_bwd")
    da4 = _mm_nt_row(dy, g_rowb, d_loc, ridx_pw2, "b_pw2_dx")
    dw_pw2 = _mm_tn_row(a4, dy, d_loc, "b_pw2_dw")
    du, dbcw, dbcb, dblg, dblb, dbb1 = _b_mid_bwd(ub, a2, da4, b_cw, b_lg, b_lb, "b_mid_bwd")
    dw_pw1 = _mm_tn_col(x2, sc_t1, sh_t1, du, "b_pw1_dw")
    scatter_b = _exchange_start([dw_pw1, dw_pw2], False, "scatter_b_start")
    dx2, acc2b = _mm_nt_mod([du], g_pw1, (0,), x2, _after(sc_t1, scatter_b[-1]), dres, "b_pw1_dx")

    dx1, accf0, acc2f0, scatter_f0, dfcw0, dfcb0 = ffn_bwd(
        dx2, x1, sc_c0, sh_c0, g_c0, ln_ch_g[0:1], u0_0, vg_0, t_0, y_f0, xh2, rs2, g_colf0, g_rowf0, 0, "0")

    dy, dres, acca = _ln_bwd(dx1, xh1, rs1, ln_tok_g[0:1], y_a, g_t0, alpha, "a_ln_bwd")
    dy0 = _mm_nt_row(dy, g_out, d_loc, 0, "a_out_dx")
    dw_out = _mm_tn_row(y0, dy, d_loc, "a_out_dw")
    dbcv, dacw, dacb = _gateconv_bwd(bcv, dy0, a_cw, a_conv_b, "a_conv_bwd")
    dw_in = _mm_tn_col(x0, sc_t0, sh_t0, dbcv, "a_in_dw")
    scatter_a = _exchange_start([dw_in, dw_out], False, "scatter_a_start")
    dx0, acc2a = _mm_nt_mod([dbcv], g_in, (0,), x0, _after(sc_t0, scatter_a[-1]), dres, "a_in_dx")

    def dmod_row(acc2_t, acc_t, acc2_c, acc_c):
        return jnp.concatenate([acc2_t[1], acc2_t[0], acc_t[2], acc2_c[1], acc2_c[0], acc_c[2]])

    dmod = jnp.stack([dmod_row(acc2a, acca, acc2f0, accf0), dmod_row(acc2b, accb, acc2f1, accf1)])

    def unpad_f(a):
        return a.reshape(a.shape[:-1] + (NDEV, f_pad))[..., :f_loc].reshape(a.shape[:-1] + (NDEV * f_loc,))

    small_grads = [
        dmod,
        jnp.stack([acca[0], accb[0]]), jnp.stack([acca[1], accb[1]]),
        jnp.stack([accf0[0], accf1[0]]), jnp.stack([accf0[1], accf1[1]]),
        dacb,
        unpad_f(jnp.concatenate([dfcb0, dfcb1], axis=0)),
        dacw, dbb1, dbcw, dbcb, dblg, dblb, accb[3:4],
        jnp.stack([dfcw0, dfcw1]),
    ]
    small_grad_shapes = [tuple(g.shape) for g in small_grads]
    sg_all, = _exchange([_pack(small_grads)], True, "gather_small_grads")
    sg_sum = _sum_parts(sg_all, "sum_small_grads")
    (g_ada_b, g_ltg, g_ltb, g_lcg, g_lcb, g_acb, g_fcb, g_acw, g_bb1, g_bcw, g_bcb, g_blg, g_blb, g_bb2,
     g_fcw) = _unpack(sg_sum, small_grad_shapes)

    def my_cols(a, width):
        return lax.dynamic_slice_in_dim(a, me * width, width, axis=a.ndim - 1)

    g_fcw_loc = my_cols(g_fcw, f_pad)[..., :f_loc]
    small = dict(
        ada_b=g_ada_b, ln_tok_g=g_ltg, ln_tok_b=g_ltb, ln_ch_g=g_lcg, ln_ch_b=g_lcb, a_conv_b=g_acb, f_conv_b=g_fcb,
        a_conv_w=my_cols(g_acw, d_loc)[None], b_b_pw1=my_cols(g_bb1, 2 * d_loc), b_conv_w=my_cols(g_bcw, d_loc)[None],
        b_conv_b=my_cols(g_bcb, d_loc), b_ln_g=my_cols(g_blg, d_loc), b_ln_b=my_cols(g_blb, d_loc),
        b_b_pw2=my_cols(g_bb2, d_loc), f_conv_w=g_fcw_loc)

    dmod_all = sg_all.reshape(NDEV, -1)[:, :depth * 6 * d].reshape(NDEV, depth, 6 * d)
    dmod_cols = my_cols(dmod_all, n_ada).transpose(1, 0, 2)
    g_ada_w = _ada_bwd(c_all.T, dmod_cols, "ada_bwd")

    grads, deltas, new_m, new_v = {}, {}, {}, {}

    def adamw(k, glist):
        w = weights[k]
        cols = w.shape[-1]
        nl = len(glist)
        outs = _adamw(w.reshape(nl, -1, cols), [g.reshape(g.shape[0], -1, cols) for g in glist],
                      mom_m[k].reshape(nl, -1, cols), mom_v[k].reshape(nl, -1, cols), "adamw_" + k)
        grads[k], deltas[k], new_m[k], new_v[k] = (o.reshape(w.shape) for o in outs)

    adamw("ada_w", [g_ada_w[0:1], g_ada_w[1:2]])
    for k, g in small.items():
        adamw(k, [g[None]])

    r_up1, r_gate1, r_down1, _ = _exchange_wait(scatter_f1, deltas["ada_w"], False, "scatter_f1_wait")
    r_pw1, r_pw2, _ = _exchange_wait(scatter_b, r_down1, False, "scatter_b_wait")
    adamw("b_w_pw1", [r_pw1])
    adamw("b_w_pw2", [r_pw2])
    r_up0, r_gate0, r_down0, _ = _exchange_wait(scatter_f0, deltas["b_w_pw2"], False, "scatter_f0_wait")
    adamw("f_w_up", [r_up0, r_up1])
    adamw("f_w_gate", [r_gate0, r_gate1])
    adamw("f_w_down", [r_down0, r_down1])
    r_in, r_out, _ = _exchange_wait(scatter_a, deltas["f_w_down"], False, "scatter_a_wait")
    adamw("a_w_in", [r_in])
    adamw("a_w_out", [r_out])

    return (loss, dx0[None], *[grads[k] for k in names], *[deltas[k] for k in names],
            *[new_m[k] for k in names], *[new_v[k] for k in names])
```

```python
import functools

import jax
import jax.numpy as jnp
from jax import lax
from jax.experimental import pallas as pl
from jax.experimental.pallas import tpu as pltpu

NDEV = 8
MESH_AXES = ("x", "y", "c")
LANES = 128
SUBLANES = 8
VMEM_LIMIT = 56 * 1024 * 1024
LN_EPS = 1e-5
SHORT_PAD = 16
LONG_PAD = 32
CHUNK = 16
ADAM_LR, ADAM_B1, ADAM_B2, ADAM_EPS, ADAM_WD, ADAM_STEP = 0.001, 0.9, 0.999, 1e-08, 0.01, 10

F32 = jnp.float32
BF16 = jnp.bfloat16
MESH = pl.DeviceIdType.MESH
NT = (((1,), (1,)), ((), ()))
TN = (((0,), (0,)), ((), ()))


def _tile(n, target, mult=SUBLANES):
    best = None
    for t in range(mult, min(n, target) + 1, mult):
        if n % t == 0:
            best = t
    return best if best is not None else n


def _full(shape):
    nd = len(shape)
    return pl.BlockSpec(shape, lambda *_: (0,) * nd)


def _cp(*sem):
    return pltpu.CompilerParams(dimension_semantics=sem, vmem_limit_bytes=VMEM_LIMIT)


def _sigmoid(x):
    return 1.0 / (1.0 + jnp.exp(-x))


def _peer(x, y, c, d):
    return ((1 - x) if d & 4 else x, (1 - y) if d & 2 else y, (1 - c) if d & 1 else c)


def _lin(p):
    return 4 * p[0] + 2 * p[1] + p[2]


def _remote_copies(src_refs, land_refs, send_sems, recv_sems, gather):
    x, y, c = (lax.axis_index(a) for a in MESH_AXES)
    me = _lin((x, y, c))
    sends, recvs = [], []
    for i, (src_ref, land_ref) in enumerate(zip(src_refs, land_refs)):
        for d in range(1, NDEV):
            peer = _peer(x, y, c, d)
            k = i * (NDEV - 1) + d - 1
            src = src_ref if gather else src_ref.at[_lin(peer)]
            for slot, out in ((me, sends), (_lin(peer), recvs)):
                out.append(pltpu.make_async_remote_copy(
                    src_ref=src, dst_ref=land_ref.at[slot], send_sem=send_sems.at[k], recv_sem=recv_sems.at[k],
                    device_id=peer, device_id_type=MESH))
    return sends, recvs


def _exchange(srcs, gather, name):
    n = len(srcs)

    def body(*refs):
        src_refs, out_refs, token = refs[:n], refs[n:2 * n], refs[2 * n]
        send_sems, recv_sems, local_sems = refs[2 * n + 1:]
        me = _lin(tuple(lax.axis_index(a) for a in MESH_AXES))
        local = []
        for i in range(n):
            mine = src_refs[i] if gather else src_refs[i].at[me]
            cp = pltpu.make_async_copy(mine, out_refs[i].at[me], local_sems.at[i])
            cp.start()
            local.append(cp)
        sends, recvs = _remote_copies(src_refs, out_refs, send_sems, recv_sems, gather)
        for snd in sends:
            snd.start()
        token[...] = jnp.zeros_like(token)
        for snd, rcv in zip(sends, recvs):
            snd.wait_send()
            rcv.wait_recv()
        for cp in local:
            cp.wait()

    out_shape = [jax.ShapeDtypeStruct(((NDEV,) + s.shape) if gather else s.shape, s.dtype) for s in srcs]
    out_shape.append(jax.ShapeDtypeStruct((SUBLANES, LANES), F32))
    any_spec = pl.BlockSpec(memory_space=pl.ANY)
    return pl.pallas_call(
        body, name=name, out_shape=out_shape,
        in_specs=[any_spec] * n, out_specs=[any_spec] * n + [pl.BlockSpec(memory_space=pltpu.VMEM)],
        scratch_shapes=[pltpu.SemaphoreType.DMA((n * (NDEV - 1),)),
                        pltpu.SemaphoreType.DMA((n * (NDEV - 1),)),
                        pltpu.SemaphoreType.DMA((n,))],
    )(*srcs)


HBM_SPEC = pl.BlockSpec(memory_space=pltpu.HBM)
SEM_SPEC = pl.BlockSpec(memory_space=pltpu.SEMAPHORE)
SIDE_EFFECT = pltpu.SideEffectType.DATAFLOW_SIDE_EFFECTING


def _exchange_start(srcs, gather, name):
    n = len(srcs)
    me = _lin(tuple(lax.axis_index(a) for a in MESH_AXES))
    lands = []
    for s in srcs:
        own = s if gather else lax.dynamic_index_in_dim(s, me, 0, keepdims=False)
        shape = ((NDEV,) + s.shape) if gather else s.shape
        lands.append(lax.dynamic_update_index_in_dim(lax.empty(shape, s.dtype), own, me, 0))

    def body(*refs):
        src_refs, land_refs = refs[:n], refs[n:2 * n]
        send_sems, recv_sems, token = refs[2 * n], refs[2 * n + 1], refs[-1]
        sends, _ = _remote_copies(src_refs, land_refs, send_sems, recv_sems, gather)
        for snd in sends:
            snd.start()
        token[...] = jnp.zeros_like(token)

    operands = [pltpu.with_memory_space_constraint(a, pltpu.HBM) for a in list(srcs) + lands]
    nsem = n * (NDEV - 1)
    return pl.pallas_call(
        body, name=name,
        out_shape=(pltpu.SemaphoreType.DMA((nsem,)), pltpu.SemaphoreType.DMA((nsem,)),
                   *[pltpu.HBM(a.shape, a.dtype) for a in operands],
                   jax.ShapeDtypeStruct((SUBLANES, LANES), F32)),
        in_specs=[HBM_SPEC] * (2 * n),
        out_specs=(SEM_SPEC, SEM_SPEC, *([HBM_SPEC] * (2 * n)), pl.BlockSpec(memory_space=pltpu.VMEM)),
        input_output_aliases={i: 2 + i for i in range(2 * n)},
        compiler_params=pltpu.CompilerParams(has_side_effects=SIDE_EFFECT),
    )(*operands)


def _exchange_wait(handle, after, gather, name):
    send_sems, recv_sems, *thru = handle[:-1]
    n = len(thru) // 2

    def body(*refs):
        src_refs, land_refs = refs[:n], refs[n:2 * n]
        sends, recvs = _remote_copies(src_refs, land_refs, refs[2 * n], refs[2 * n + 1], gather)
        for snd, rcv in zip(sends, recvs):
            snd.wait_send()
            rcv.wait_recv()
        refs[-1][...] = jnp.zeros_like(refs[-1])

    outs = pl.pallas_call(
        body, name=name,
        out_shape=(*[pltpu.HBM(a.shape, a.dtype) for a in thru], jax.ShapeDtypeStruct((SUBLANES, LANES), F32)),
        in_specs=[HBM_SPEC] * (2 * n) + [SEM_SPEC, SEM_SPEC, pl.BlockSpec(memory_space=pl.ANY)],
        out_specs=[HBM_SPEC] * (2 * n) + [pl.BlockSpec(memory_space=pltpu.VMEM)],
        input_output_aliases={i: i for i in range(2 * n)},
        compiler_params=pltpu.CompilerParams(has_side_effects=SIDE_EFFECT),
    )(*thru, send_sems, recv_sems, after)
    return outs[n:]


def _after(value, token):
    return value + token[0, 0]


def _mm_fwd(x, sc, sh, bias, wg, widxs, name):
    s_len, kdim = x.shape
    n = wg.shape[-1]
    ncol = NDEV * n
    tm = _tile(s_len, 256)
    nw = len(widxs)

    def body(x_ref, sc_ref, sh_ref, b_ref, *rest):
        w_refs, o_refs = rest[:nw], rest[nw:]
        h = (x_ref[...] * (1.0 + sc_ref[...]) + sh_ref[...]).astype(BF16)
        for w_ref, o_ref in zip(w_refs, o_refs):
            for k in range(NDEV):
                cols = slice(k * n, (k + 1) * n)
                o_ref[:, cols] = jnp.dot(h, w_ref[k], preferred_element_type=F32) + b_ref[:, cols]

    w_specs = [pl.BlockSpec((NDEV, None, kdim, n), functools.partial(lambda i, w: (0, w, 0, 0), w=w))
               for w in widxs]
    return pl.pallas_call(
        body, name=name, grid=(s_len // tm,),
        in_specs=[pl.BlockSpec((tm, kdim), lambda i: (i, 0)), _full((1, kdim)), _full((1, kdim)),
                  _full((1, ncol))] + w_specs,
        out_specs=[pl.BlockSpec((tm, ncol), lambda i: (i, 0))] * nw,
        out_shape=[jax.ShapeDtypeStruct((s_len, ncol), F32)] * nw,
        compiler_params=_cp("parallel"),
    )(x, sc, sh, bias, *([wg] * nw))


def _mm_ln(a, wg, r, ridx, xres, gate, gam, bet, bias, alpha, name):
    s_len = a.shape[0]
    d = wg.shape[-1]
    tm = _tile(s_len, 256)

    def body(a_ref, w_ref, x_ref, g_ref, gam_ref, bet_ref, b_ref, y_ref, xo_ref, xh_ref, rs_ref):
        acc = None
        for k in range(NDEV):
            p = jnp.dot(a_ref[:, k * r:(k + 1) * r], w_ref[k], preferred_element_type=F32)
            acc = p if acc is None else acc + p
        y = acc + b_ref[...]
        z = alpha * x_ref[...] + g_ref[...] * y
        mu = jnp.mean(z, axis=-1, keepdims=True)
        zc = z - mu
        var = jnp.mean(zc * zc, axis=-1, keepdims=True)
        rstd = lax.rsqrt(var + LN_EPS)
        xh = zc * rstd
        y_ref[...] = y
        xh_ref[...] = xh
        rs_ref[...] = rstd
        xo_ref[...] = xh * gam_ref[...] + bet_ref[...]

    row = pl.BlockSpec((tm, d), lambda i: (i, 0))
    vec = _full((1, d))
    return pl.pallas_call(
        body, name=name, grid=(s_len // tm,),
        in_specs=[pl.BlockSpec((tm, NDEV * r), lambda i: (i, 0)),
                  pl.BlockSpec((NDEV, r, d), lambda i: (0, ridx, 0)), row, vec, vec, vec, vec],
        out_specs=[row, row, row, pl.BlockSpec((tm, 1), lambda i: (i, 0))],
        out_shape=[jax.ShapeDtypeStruct((s_len, d), F32)] * 3 + [jax.ShapeDtypeStruct((s_len, 1), F32)],
        compiler_params=_cp("parallel"),
    )(a, wg, xres, gate, gam, bet, bias)


def _ln_bwd(dxo, xh, rstd, gam, y, gate, alpha, name):
    s_len, d = dxo.shape
    tm = _tile(s_len, 256)

    def body(d_ref, xh_ref, rs_ref, gam_ref, y_ref, g_ref, dy_ref, dres_ref, acc_ref):
        @pl.when(pl.program_id(0) == 0)
        def _():
            acc_ref[...] = jnp.zeros_like(acc_ref)

        dxo_t = d_ref[...]
        xh_t = xh_ref[...]
        dxh = dxo_t * gam_ref[...]
        m1 = jnp.mean(dxh, axis=-1, keepdims=True)
        m2 = jnp.mean(dxh * xh_t, axis=-1, keepdims=True)
        dz = rs_ref[...] * (dxh - m1 - xh_t * m2)
        dy = g_ref[...] * dz
        dy_ref[...] = dy.astype(BF16)
        dres_ref[...] = alpha * dz
        acc_ref[0:1, :] += jnp.sum(dxo_t * xh_t, axis=0, keepdims=True)
        acc_ref[1:2, :] += jnp.sum(dxo_t, axis=0, keepdims=True)
        acc_ref[2:3, :] += jnp.sum(dz * y_ref[...], axis=0, keepdims=True)
        acc_ref[3:4, :] += jnp.sum(dy, axis=0, keepdims=True)

    row = pl.BlockSpec((tm, d), lambda i: (i, 0))
    vec = _full((1, d))
    return pl.pallas_call(
        body, name=name, grid=(s_len // tm,),
        in_specs=[row, row, pl.BlockSpec((tm, 1), lambda i: (i, 0)), vec, row, vec],
        out_specs=[row, row, _full((SUBLANES, d))],
        out_shape=[jax.ShapeDtypeStruct((s_len, d), BF16), jax.ShapeDtypeStruct((s_len, d), F32),
                   jax.ShapeDtypeStruct((SUBLANES, d), F32)],
        compiler_params=_cp("arbitrary"),
    )(dxo, xh, rstd, gam, y, gate)


def _mm_nt_row(dy, wg, r, ridx, name):
    s_len, d = dy.shape
    tm = _tile(s_len, 256)

    def body(dy_ref, w_ref, o_ref):
        g = dy_ref[...]
        for k in range(NDEV):
            o_ref[:, k * r:(k + 1) * r] = lax.dot_general(g, w_ref[k], NT, preferred_element_type=F32)

    return pl.pallas_call(
        body, name=name, grid=(s_len // tm,),
        in_specs=[pl.BlockSpec((tm, d), lambda i: (i, 0)), pl.BlockSpec((NDEV, r, d), lambda i: (0, ridx, 0))],
        out_specs=pl.BlockSpec((tm, NDEV * r), lambda i: (i, 0)),
        out_shape=jax.ShapeDtypeStruct((s_len, NDEV * r), F32),
        compiler_params=_cp("parallel"),
    )(dy, wg)


def _mm_nt_mod(dos, wg, widxs, xin, sc, dres, name):
    s_len, kdim = xin.shape
    n = wg.shape[-1]
    tm = _tile(s_len, 256)
    nw = len(widxs)

    def body(*refs):
        do_refs, w_refs = refs[:nw], refs[nw:2 * nw]
        x_ref, sc_ref, dres_ref, dx_ref, acc_ref = refs[2 * nw:]

        @pl.when(pl.program_id(0) == 0)
        def _():
            acc_ref[...] = jnp.zeros_like(acc_ref)

        dh = None
        for do_ref, w_ref in zip(do_refs, w_refs):
            for k in range(NDEV):
                p = lax.dot_general(do_ref[:, k * n:(k + 1) * n], w_ref[k], NT, preferred_element_type=F32)
                dh = p if dh is None else dh + p
        dx_ref[...] = dh * (1.0 + sc_ref[...]) + dres_ref[...]
        acc_ref[0:1, :] += jnp.sum(dh * x_ref[...], axis=0, keepdims=True)
        acc_ref[1:2, :] += jnp.sum(dh, axis=0, keepdims=True)

    row = pl.BlockSpec((tm, kdim), lambda i: (i, 0))
    w_specs = [pl.BlockSpec((NDEV, None, kdim, n), functools.partial(lambda i, w: (0, w, 0, 0), w=w))
               for w in widxs]
    return pl.pallas_call(
        body, name=name, grid=(s_len // tm,),
        in_specs=[pl.BlockSpec((tm, NDEV * n), lambda i: (i, 0))] * nw + w_specs + [row, _full((1, kdim)), row],
        out_specs=[row, _full((SUBLANES, kdim))],
        out_shape=[jax.ShapeDtypeStruct((s_len, kdim), F32), jax.ShapeDtypeStruct((SUBLANES, kdim), F32)],
        compiler_params=_cp("arbitrary"),
    )(*dos, *([wg] * nw), xin, sc, dres)


def _mm_tn_col(x, sc, sh, do, name):
    s_len, kdim = x.shape
    n = do.shape[1] // NDEV
    ts = _tile(s_len, 512)
    nsteps = s_len // ts

    def body(x_ref, sc_ref, sh_ref, do_ref, o_ref, acc_ref):
        @pl.when(pl.program_id(0) == 0)
        def _():
            acc_ref[...] = jnp.zeros_like(acc_ref)

        h = (x_ref[...] * (1.0 + sc_ref[...]) + sh_ref[...]).astype(BF16)
        for k in range(NDEV):
            acc_ref[k] += lax.dot_general(h, do_ref[:, k * n:(k + 1) * n], TN, preferred_element_type=F32)

        @pl.when(pl.program_id(0) == nsteps - 1)
        def _():
            o_ref[...] = acc_ref[...].astype(BF16)

    return pl.pallas_call(
        body, name=name, grid=(nsteps,),
        in_specs=[pl.BlockSpec((ts, kdim), lambda i: (i, 0)), _full((1, kdim)), _full((1, kdim)),
                  pl.BlockSpec((ts, NDEV * n), lambda i: (i, 0))],
        out_specs=_full((NDEV, kdim, n)),
        out_shape=jax.ShapeDtypeStruct((NDEV, kdim, n), BF16),
        scratch_shapes=[pltpu.VMEM((NDEV, kdim, n), F32)],
        compiler_params=_cp("arbitrary"),
    )(x, sc, sh, do)


def _mm_tn_col_t(x, sc, sh, do, rows_out, name):
    s_len, kdim = x.shape
    n = do.shape[1] // NDEV
    ts = _tile(s_len, 512)
    nsteps = s_len // ts

    def body(x_ref, sc_ref, sh_ref, do_ref, o_ref, acc_ref):
        @pl.when(pl.program_id(0) == 0)
        def _():
            acc_ref[...] = jnp.zeros_like(acc_ref)

        h = (x_ref[...] * (1.0 + sc_ref[...]) + sh_ref[...]).astype(BF16)
        for k in range(NDEV):
            acc_ref[k] += lax.dot_general(do_ref[:, k * n:(k + 1) * n], h, TN, preferred_element_type=F32)

        @pl.when(pl.program_id(0) == nsteps - 1)
        def _():
            o_ref[...] = acc_ref[:, 0:rows_out, :].astype(BF16)

    return pl.pallas_call(
        body, name=name, grid=(nsteps,),
        in_specs=[pl.BlockSpec((ts, kdim), lambda i: (i, 0)), _full((1, kdim)), _full((1, kdim)),
                  pl.BlockSpec((ts, NDEV * n), lambda i: (i, 0))],
        out_specs=_full((NDEV, rows_out, kdim)),
        out_shape=jax.ShapeDtypeStruct((NDEV, rows_out, kdim), BF16),
        scratch_shapes=[pltpu.VMEM((NDEV, n, kdim), F32)],
        compiler_params=_cp("arbitrary"),
    )(x, sc, sh, do)


def _mm_tn_row(a, dy, r, rows_out, name):
    s_len, d = dy.shape
    ts = _tile(s_len, 512)
    nsteps = s_len // ts

    def body(a_ref, dy_ref, o_ref, acc_ref):
        @pl.when(pl.program_id(0) == 0)
        def _():
            acc_ref[...] = jnp.zeros_like(acc_ref)

        g = dy_ref[...]
        for k in range(NDEV):
            acc_ref[k] += lax.dot_general(a_ref[:, k * r:(k + 1) * r], g, TN, preferred_element_type=F32)

        @pl.when(pl.program_id(0) == nsteps - 1)
        def _():
            o_ref[...] = acc_ref[:, 0:rows_out, :].astype(BF16)

    return pl.pallas_call(
        body, name=name, grid=(nsteps,),
        in_specs=[pl.BlockSpec((ts, NDEV * r), lambda i: (i, 0)), pl.BlockSpec((ts, d), lambda i: (i, 0))],
        out_specs=_full((NDEV, rows_out, d)),
        out_shape=jax.ShapeDtypeStruct((NDEV, rows_out, d), BF16),
        scratch_shapes=[pltpu.VMEM((NDEV, r, d), F32)],
        compiler_params=_cp("arbitrary"),
    )(a, dy)


def _prev_spec(ts, pad, cb, col):
    return pl.BlockSpec((pad, cb), lambda *g: (jnp.maximum(g[-1] * (ts // pad) - 1, 0), col(g)))


def _next_spec(ts, pad, cb, col, s_len):
    return pl.BlockSpec((pad, cb), lambda *g: (jnp.minimum((g[-1] + 1) * (ts // pad), s_len // pad - 1), col(g)))


def _direct(buf_ref):
    return lambda off, rows: buf_ref[off:off + rows, :]


def _make_shifts(sh_ref, nrows):
    for r in range(1, SUBLANES):
        sh_ref[r, 0:nrows - SUBLANES, :] = sh_ref[0, r:r + nrows - SUBLANES, :]


def _shifted(sh_ref):
    def read(off, rows):
        r = off % SUBLANES
        return sh_ref[r, off - r:off - r + rows, :]
    return read


def _conv_fwd_rows(read, w_ref, b_ref, ktaps, pad, r0, rows):
    acc = None
    for j in range(ktaps):
        term = w_ref[ktaps - 1 - j:ktaps - j, :] * read(pad - j + r0, rows)
        acc = term if acc is None else acc + term
    return acc + b_ref[...]


def _conv_bwd_rows(read, x_rows, w_ref, dwacc_ref, ktaps, r0, rows):
    acc = None
    for j in range(ktaps):
        sl = read(j + r0, rows)
        term = w_ref[ktaps - 1 - j:ktaps - j, :] * sl
        acc = term if acc is None else acc + term
        prod = x_rows * sl
        fold = prod[0:SUBLANES]
        for q in range(1, rows // SUBLANES):
            fold = fold + prod[q * SUBLANES:(q + 1) * SUBLANES]
        tap = ktaps - 1 - j
        dwacc_ref[tap * SUBLANES:(tap + 1) * SUBLANES, :] += fold
    return acc


def _flush_dw(dwacc_ref, dw_ref, ktaps):
    for tap in range(ktaps):
        dw_ref[tap:tap + 1, :] = jnp.sum(dwacc_ref[tap * SUBLANES:(tap + 1) * SUBLANES, :], axis=0, keepdims=True)


def _gateconv_fwd(bcv, cw, cb, name):
    s_len, d3 = bcv.shape
    d = d3 // 3
    ktaps = cw.shape[0]
    pad = SHORT_PAD
    ts = _tile(s_len, 256)

    def body(gb_ref, gc_ref, v_ref, gcp_ref, vp_ref, w_ref, b_ref, o_ref, pbuf):
        s = pl.program_id(0)
        pbuf[0:pad, :] = jnp.where(s > 0, gcp_ref[...] * vp_ref[...], 0.0)
        pbuf[pad:pad + ts, :] = gc_ref[...] * v_ref[...]
        for r0 in range(0, ts, CHUNK):
            q = _conv_fwd_rows(_direct(pbuf), w_ref, b_ref, ktaps, pad, r0, CHUNK)
            o_ref[r0:r0 + CHUNK, :] = (gb_ref[r0:r0 + CHUNK, :] * q).astype(BF16)

    def cur(part):
        return pl.BlockSpec((ts, d), lambda s: (s, part))

    return pl.pallas_call(
        body, name=name, grid=(s_len // ts,),
        in_specs=[cur(0), cur(1), cur(2),
                  _prev_spec(ts, pad, d, lambda g: 1), _prev_spec(ts, pad, d, lambda g: 2),
                  _full((ktaps, d)), _full((1, d))],
        out_specs=pl.BlockSpec((ts, d), lambda s: (s, 0)),
        out_shape=jax.ShapeDtypeStruct((s_len, d), BF16),
        scratch_shapes=[pltpu.VMEM((pad + ts, d), F32)],
        compiler_params=_cp("parallel"),
    )(bcv, bcv, bcv, bcv, bcv, cw, cb)


def _gateconv_bwd(bcv, dy0, cw, cb, name):
    s_len, d3 = bcv.shape
    d = d3 // 3
    ktaps = cw.shape[0]
    pad = SHORT_PAD
    ts = _tile(s_len, 256)
    nsteps = s_len // ts

    def body(gb_ref, gc_ref, v_ref, gcp_ref, vp_ref, gbn_ref, dy_ref, dyn_ref, w_ref, b_ref,
             o_ref, dw_ref, db_ref, pbuf, dqbuf, dwacc):
        s = pl.program_id(0)

        @pl.when(s == 0)
        def _():
            dwacc[...] = jnp.zeros_like(dwacc)
            db_ref[...] = jnp.zeros_like(db_ref)

        pbuf[0:pad, :] = jnp.where(s > 0, gcp_ref[...] * vp_ref[...], 0.0)
        pbuf[pad:pad + ts, :] = gc_ref[...] * v_ref[...]
        dq = dy_ref[...] * gb_ref[...]
        dqbuf[0:ts, :] = dq
        dqbuf[ts:ts + pad, :] = jnp.where(s < nsteps - 1, dyn_ref[...] * gbn_ref[...], 0.0)
        db_ref[...] += jnp.sum(dq, axis=0, keepdims=True)
        for r0 in range(0, ts, CHUNK):
            rows = slice(r0, r0 + CHUNK)
            q = _conv_fwd_rows(_direct(pbuf), w_ref, b_ref, ktaps, pad, r0, CHUNK)
            o_ref[rows, 0:d] = (dy_ref[rows, :] * q).astype(BF16)
            dp = _conv_bwd_rows(_direct(dqbuf), pbuf[pad + r0:pad + r0 + CHUNK, :], w_ref, dwacc, ktaps, r0, CHUNK)
            o_ref[rows, d:2 * d] = (dp * v_ref[rows, :]).astype(BF16)
            o_ref[rows, 2 * d:3 * d] = (dp * gc_ref[rows, :]).astype(BF16)

        @pl.when(s == nsteps - 1)
        def _():
            _flush_dw(dwacc, dw_ref, ktaps)

    def cur(part):
        return pl.BlockSpec((ts, d), lambda s: (s, part))

    return pl.pallas_call(
        body, name=name, grid=(nsteps,),
        in_specs=[cur(0), cur(1), cur(2),
                  _prev_spec(ts, pad, d, lambda g: 1), _prev_spec(ts, pad, d, lambda g: 2),
                  _next_spec(ts, pad, d, lambda g: 0, s_len),
                  cur(0), _next_spec(ts, pad, d, lambda g: 0, s_len),
                  _full((ktaps, d)), _full((1, d))],
        out_specs=[pl.BlockSpec((ts, d3), lambda s: (s, 0)), _full((ktaps, d)), _full((1, d))],
        out_shape=[jax.ShapeDtypeStruct((s_len, d3), BF16), jax.ShapeDtypeStruct((ktaps, d), F32),
                   jax.ShapeDtypeStruct((1, d), F32)],
        scratch_shapes=[pltpu.VMEM((pad + ts, d), F32), pltpu.VMEM((ts + pad, d), F32),
                        pltpu.VMEM((ktaps * SUBLANES, d), F32)],
        compiler_params=_cp("arbitrary"),
    )(bcv, bcv, bcv, bcv, bcv, bcv, dy0, dy0, cw, cb)


def _ffn_mid_fwd(u0, vg, cw, cb, name):
    s_len, f = u0.shape
    ktaps = cw.shape[0]
    pad = SHORT_PAD
    ts = _tile(s_len, 256)
    cbk = 1024 if f % 1024 == 0 else f

    def body(u_ref, up_ref, vg_ref, w_ref, b_ref, o_ref, ubuf):
        s = pl.program_id(1)
        ubuf[0:pad, :] = jnp.where(s > 0, up_ref[...], 0.0)
        ubuf[pad:pad + ts, :] = u_ref[...]
        for r0 in range(0, ts, CHUNK):
            u = _conv_fwd_rows(_direct(ubuf), w_ref, b_ref, ktaps, pad, r0, CHUNK)
            o_ref[r0:r0 + CHUNK, :] = (u * _sigmoid(u) * vg_ref[r0:r0 + CHUNK, :]).astype(BF16)

    cur = pl.BlockSpec((ts, cbk), lambda c, s: (s, c))
    return pl.pallas_call(
        body, name=name, grid=(f // cbk, s_len // ts),
        in_specs=[cur, _prev_spec(ts, pad, cbk, lambda g: g[0]), cur,
                  pl.BlockSpec((ktaps, cbk), lambda c, s: (0, c)), pl.BlockSpec((1, cbk), lambda c, s: (0, c))],
        out_specs=cur,
        out_shape=jax.ShapeDtypeStruct((s_len, f), BF16),
        scratch_shapes=[pltpu.VMEM((pad + ts, cbk), F32)],
        compiler_params=_cp("parallel", "parallel"),
    )(u0, u0, vg, cw, cb)


def _ffn_mid_bwd(u0, vg, dt, cw, cb, name):
    s_len, f = u0.shape
    ktaps = cw.shape[0]
    pad = SHORT_PAD
    ts = _tile(s_len, 256)
    nsteps = s_len // ts
    cbk = 1024 if f % 1024 == 0 else f

    def body(u_ref, up_ref, un_ref, vg_ref, vgn_ref, dt_ref, dtn_ref, w_ref, b_ref,
             du0_ref, dvg_ref, dw_ref, db_ref, ubuf, dubuf, dwacc):
        s = pl.program_id(1)

        @pl.when(s == 0)
        def _():
            dwacc[...] = jnp.zeros_like(dwacc)
            db_ref[...] = jnp.zeros_like(db_ref)

        ubuf[0:pad, :] = jnp.where(s > 0, up_ref[...], 0.0)
        ubuf[pad:pad + ts, :] = u_ref[...]
        ubuf[pad + ts:pad + ts + pad, :] = un_ref[...]
        last = s == nsteps - 1
        for r0 in range(0, ts + pad, CHUNK):
            u = _conv_fwd_rows(_direct(ubuf), w_ref, b_ref, ktaps, pad, r0, CHUNK)
            sg = _sigmoid(u)
            if r0 < ts:
                rows = slice(r0, r0 + CHUNK)
                dtr, vgr = dt_ref[rows, :], vg_ref[rows, :]
                dvg_ref[rows, :] = (dtr * u * sg).astype(BF16)
            else:
                rows = slice(r0 - ts, r0 - ts + CHUNK)
                dtr, vgr = jnp.where(last, 0.0, dtn_ref[rows, :]), vgn_ref[rows, :]
            dubuf[r0:r0 + CHUNK, :] = dtr * vgr * (sg * (1.0 + u * (1.0 - sg)))
        db_ref[...] += jnp.sum(dubuf[0:ts, :], axis=0, keepdims=True)
        for r0 in range(0, ts, CHUNK):
            du0 = _conv_bwd_rows(_direct(dubuf), u_ref[r0:r0 + CHUNK, :], w_ref, dwacc, ktaps, r0, CHUNK)
            du0_ref[r0:r0 + CHUNK, :] = du0.astype(BF16)

        @pl.when(last)
        def _():
            _flush_dw(dwacc, dw_ref, ktaps)

    cur = pl.BlockSpec((ts, cbk), lambda c, s: (s, c))
    prv = _prev_spec(ts, pad, cbk, lambda g: g[0])
    nxt = _next_spec(ts, pad, cbk, lambda g: g[0], s_len)
    return pl.pallas_call(
        body, name=name, grid=(f // cbk, nsteps),
        in_specs=[cur, prv, nxt, cur, nxt, cur, nxt,
                  pl.BlockSpec((ktaps, cbk), lambda c, s: (0, c)), pl.BlockSpec((1, cbk), lambda c, s: (0, c))],
        out_specs=[cur, cur, pl.BlockSpec((ktaps, cbk), lambda c, s: (0, c)),
                   pl.BlockSpec((1, cbk), lambda c, s: (0, c))],
        out_shape=[jax.ShapeDtypeStruct((s_len, f), BF16), jax.ShapeDtypeStruct((s_len, f), BF16),
                   jax.ShapeDtypeStruct((ktaps, f), F32), jax.ShapeDtypeStruct((1, f), F32)],
        scratch_shapes=[pltpu.VMEM((pad + ts + pad, cbk), F32), pltpu.VMEM((ts + pad, cbk), F32),
                        pltpu.VMEM((ktaps * SUBLANES, cbk), F32)],
        compiler_params=_cp("parallel", "arbitrary"),
    )(u0, u0, u0, vg, vg, dt, dt, cw, cb)


def _b_mid_fwd(ub, cw, cb, lng, lnb, name):
    s_len, d2 = ub.shape
    d = d2 // 2
    ktaps = cw.shape[0]
    pad = LONG_PAD
    ts = _tile(s_len, 256)

    def body(a_ref, g_ref, ap_ref, gp_ref, w_ref, b_ref, lng_ref, lnb_ref, a2_ref, a4_ref, abuf):
        s = pl.program_id(0)
        abuf[0, 0:pad, :] = jnp.where(s > 0, ap_ref[...] * _sigmoid(gp_ref[...]), 0.0)
        abuf[0, pad:pad + ts, :] = a_ref[...] * _sigmoid(g_ref[...])
        _make_shifts(abuf, pad + ts)
        for r0 in range(0, ts, CHUNK):
            a2_ref[r0:r0 + CHUNK, :] = _conv_fwd_rows(_shifted(abuf), w_ref, b_ref, ktaps, pad, r0, CHUNK)
        a2 = a2_ref[...]
        mu = jnp.mean(a2, axis=-1, keepdims=True)
        ac = a2 - mu
        var = jnp.mean(ac * ac, axis=-1, keepdims=True)
        a3 = ac * lax.rsqrt(var + LN_EPS) * lng_ref[...] + lnb_ref[...]
        a4_ref[...] = (a3 * _sigmoid(a3)).astype(BF16)

    def cur(part):
        return pl.BlockSpec((ts, d), lambda s: (s, part))

    vec = _full((1, d))
    return pl.pallas_call(
        body, name=name, grid=(s_len // ts,),
        in_specs=[cur(0), cur(1), _prev_spec(ts, pad, d, lambda g: 0), _prev_spec(ts, pad, d, lambda g: 1),
                  _full((ktaps, d)), vec, vec, vec],
        out_specs=[cur(0), cur(0)],
        out_shape=[jax.ShapeDtypeStruct((s_len, d), F32), jax.ShapeDtypeStruct((s_len, d), BF16)],
        scratch_shapes=[pltpu.VMEM((SUBLANES, pad + ts, d), F32)],
        compiler_params=_cp("parallel"),
    )(ub, ub, ub, ub, cw, cb, lng, lnb)


def _b_mid_bwd(ub, a2, da4, cw, lng, lnb, name):
    s_len, d2 = ub.shape
    d = d2 // 2
    ktaps = cw.shape[0]
    pad = LONG_PAD
    ts = _tile(s_len, 256)
    nsteps = s_len // ts

    def body(a_ref, g_ref, a2_ref, a2n_ref, da4_ref, da4n_ref, w_ref, lng_ref, lnb_ref,
             du_ref, dw_ref, db_ref, dlng_ref, dlnb_ref, dbias_ref, dabuf, dwacc):
        s = pl.program_id(0)
        last = s == nsteps - 1

        @pl.when(s == 0)
        def _():
            dwacc[...] = jnp.zeros_like(dwacc)
            for ref in (db_ref, dlng_ref, dlnb_ref, dbias_ref):
                ref[...] = jnp.zeros_like(ref)

        def ln_silu_bwd(a2_t, da4_t):
            mu = jnp.mean(a2_t, axis=-1, keepdims=True)
            ac = a2_t - mu
            var = jnp.mean(ac * ac, axis=-1, keepdims=True)
            rstd = lax.rsqrt(var + LN_EPS)
            ah = ac * rstd
            a3 = ah * lng_ref[...] + lnb_ref[...]
            sg = _sigmoid(a3)
            da3 = da4_t * (sg * (1.0 + a3 * (1.0 - sg)))
            dah = da3 * lng_ref[...]
            m1 = jnp.mean(dah, axis=-1, keepdims=True)
            m2 = jnp.mean(dah * ah, axis=-1, keepdims=True)
            return rstd * (dah - m1 - ah * m2), da3, ah

        da2, da3, ah = ln_silu_bwd(a2_ref[...], da4_ref[...])
        dabuf[0, 0:ts, :] = da2
        dlng_ref[...] += jnp.sum(da3 * ah, axis=0, keepdims=True)
        dlnb_ref[...] += jnp.sum(da3, axis=0, keepdims=True)
        db_ref[...] += jnp.sum(da2, axis=0, keepdims=True)
        da2n, _, _ = ln_silu_bwd(a2n_ref[...], jnp.where(last, 0.0, da4n_ref[...]))
        dabuf[0, ts:ts + pad, :] = da2n
        _make_shifts(dabuf, ts + pad)
        for r0 in range(0, ts, CHUNK):
            rows = slice(r0, r0 + CHUNK)
            a_r, g_r = a_ref[rows, :], g_ref[rows, :]
            sg = _sigmoid(g_r)
            da1 = _conv_bwd_rows(_shifted(dabuf), a_r * sg, w_ref, dwacc, ktaps, r0, CHUNK)
            da = da1 * sg
            dg = da1 * a_r * sg * (1.0 - sg)
            du_ref[rows, 0:d] = da.astype(BF16)
            du_ref[rows, d:2 * d] = dg.astype(BF16)
            dbias_ref[:, 0:d] += jnp.sum(da, axis=0, keepdims=True)
            dbias_ref[:, d:2 * d] += jnp.sum(dg, axis=0, keepdims=True)

        @pl.when(last)
        def _():
            _flush_dw(dwacc, dw_ref, ktaps)

    def cur(part):
        return pl.BlockSpec((ts, d), lambda s: (s, part))

    vec = _full((1, d))
    nxt = _next_spec(ts, pad, d, lambda g: 0, s_len)
    return pl.pallas_call(
        body, name=name, grid=(nsteps,),
        in_specs=[cur(0), cur(1), cur(0), nxt, cur(0), nxt, _full((ktaps, d)), vec, vec],
        out_specs=[pl.BlockSpec((ts, d2), lambda s: (s, 0)), _full((ktaps, d)), vec, vec, vec, _full((1, d2))],
        out_shape=[jax.ShapeDtypeStruct((s_len, d2), BF16), jax.ShapeDtypeStruct((ktaps, d), F32),
                   jax.ShapeDtypeStruct((1, d), F32), jax.ShapeDtypeStruct((1, d), F32),
                   jax.ShapeDtypeStruct((1, d), F32), jax.ShapeDtypeStruct((1, d2), F32)],
        scratch_shapes=[pltpu.VMEM((SUBLANES, ts + pad, d), F32), pltpu.VMEM((ktaps * SUBLANES, d), F32)],
        compiler_params=_cp("arbitrary"),
    )(ub, ub, a2, a2, da4, da4, cw, lng, lnb)


def _loss_head(xo, tgt, name):
    s_len, d = xo.shape
    tm = _tile(s_len, 512)

    def body(x_ref, t_ref, d_ref, l_ref):
        @pl.when(pl.program_id(0) == 0)
        def _():
            l_ref[...] = jnp.zeros_like(l_ref)

        e = x_ref[...] - t_ref[...]
        d_ref[...] = e * (1.0 / d)
        per_row = jnp.sum(e * e, axis=-1, keepdims=True) * (1.0 / d)
        l_ref[...] += 0.5 * jnp.sum(per_row, axis=0, keepdims=True)

    row = pl.BlockSpec((tm, d), lambda i: (i, 0))
    return pl.pallas_call(
        body, name=name, grid=(s_len // tm,),
        in_specs=[row, row], out_specs=[row, _full((1, LANES))],
        out_shape=[jax.ShapeDtypeStruct((s_len, d), F32), jax.ShapeDtypeStruct((1, LANES), F32)],
        compiler_params=_cp("arbitrary"),
    )(xo, tgt)


def _ada_fwd(c_all, ada_w, ada_b_loc, name):
    depth, d, n = ada_w.shape

    def body(c_ref, w_ref, b_ref, o_ref):
        c = c_ref[...]
        act = c * _sigmoid(c)
        o_ref[...] = jnp.dot(act, w_ref[...], preferred_element_type=F32,
                             precision=lax.Precision.HIGHEST) + b_ref[...]

    return pl.pallas_call(
        body, name=name, grid=(depth,),
        in_specs=[_full((NDEV, d)), pl.BlockSpec((None, d, n), lambda i: (i, 0, 0)),
                  pl.BlockSpec((None, 1, n), lambda i: (i, 0, 0))],
        out_specs=pl.BlockSpec((None, NDEV, n), lambda i: (i, 0, 0)),
        out_shape=jax.ShapeDtypeStruct((depth, NDEV, n), F32),
        compiler_params=_cp("parallel"),
    )(c_all, ada_w, ada_b_loc.reshape(depth, 1, n))


def _ada_bwd(c_all_t, dmod_cols, name):
    depth, _, n = dmod_cols.shape
    d = c_all_t.shape[0]

    def body(ct_ref, dm_ref, o_ref):
        ct = ct_ref[...]
        act = ct * _sigmoid(ct)
        acc = None
        for b in range(NDEV):
            term = act[:, b:b + 1] * dm_ref[b:b + 1, :]
            acc = term if acc is None else acc + term
        o_ref[...] = acc

    return pl.pallas_call(
        body, name=name, grid=(depth,),
        in_specs=[_full((d, NDEV)), pl.BlockSpec((None, NDEV, n), lambda i: (i, 0, 0))],
        out_specs=pl.BlockSpec((None, d, n), lambda i: (i, 0, 0)),
        out_shape=jax.ShapeDtypeStruct((depth, d, n), F32),
        compiler_params=_cp("parallel"),
    )(c_all_t, dmod_cols)


def _sum_parts(parts, name):
    _, rows, lanes = parts.shape

    def body(p_ref, o_ref):
        acc = p_ref[0]
        for k in range(1, NDEV):
            acc = acc + p_ref[k]
        o_ref[...] = acc

    return pl.pallas_call(
        body, name=name, in_specs=[_full(parts.shape)], out_specs=_full((rows, lanes)), grid=(1,),
        out_shape=jax.ShapeDtypeStruct((rows, lanes), F32), compiler_params=_cp("arbitrary"),
    )(parts)


def _adamw(w, glist, m, v, name):
    nl, rows, cols = w.shape
    tr = _tile(rows, 256, 2 * SUBLANES)

    def body(w_ref, *rest):
        g_refs = rest[:nl]
        m_ref, v_ref, go_ref, d_ref, mo_ref, vo_ref = rest[nl:]
        g = None
        for layer, g_ref in enumerate(g_refs):
            part = g_ref[0].astype(F32)
            for p in range(1, g_ref.shape[0]):
                part = part + g_ref[p].astype(F32)
            g = part if g is None else jnp.where(pl.program_id(0) == layer, part, g)
        m1 = ADAM_B1 * m_ref[...] + (1.0 - ADAM_B1) * g
        v1 = ADAM_B2 * v_ref[...] + (1.0 - ADAM_B2) * (g * g)
        m_hat = m1 / (1.0 - ADAM_B1 ** ADAM_STEP)
        v_hat = v1 / (1.0 - ADAM_B2 ** ADAM_STEP)
        go_ref[...] = g
        mo_ref[...] = m1
        vo_ref[...] = v1
        d_ref[...] = -ADAM_LR * (m_hat / (jnp.sqrt(v_hat) + ADAM_EPS) + ADAM_WD * w_ref[...])

    blk = pl.BlockSpec((None, tr, cols), lambda l, i: (l, i, 0))
    g_specs = [pl.BlockSpec((g.shape[0], tr, cols), lambda l, i: (0, i, 0)) for g in glist]
    return pl.pallas_call(
        body, name=name, grid=(nl, rows // tr),
        in_specs=[blk] + g_specs + [blk, blk],
        out_specs=[blk] * 4, out_shape=[jax.ShapeDtypeStruct((nl, rows, cols), F32)] * 4,
        compiler_params=_cp("parallel", "parallel"),
    )(w, *glist, m, v)


def _pack(pieces):
    flat = jnp.concatenate([p.reshape(-1) for p in pieces])
    unit = SUBLANES * LANES
    padded = -(-flat.shape[0] // unit) * unit
    return jnp.pad(flat, (0, padded - flat.shape[0])).reshape(padded // LANES, LANES)


def _unpack(packed, shapes, lead=()):
    flat = packed.reshape(lead + (-1,))
    out, off = [], 0
    for s in shapes:
        size = 1
        for dim in s:
            size *= dim
        out.append(flat[..., off:off + size].reshape(lead + tuple(s)))
        off += size
    return out


def _pad_last(a, n):
    return jnp.pad(a, [(0, 0)] * (a.ndim - 1) + [(0, n - a.shape[-1])])


def kernel(x, c, ada_w, ada_b, ln_tok_g, ln_tok_b, ln_ch_g, ln_ch_b, a_w_in, a_conv_w, a_conv_b, a_w_out, b_w_pw1, b_b_pw1, b_conv_w, b_conv_b, b_ln_g, b_ln_b, b_w_pw2, b_b_pw2, f_w_up, f_conv_w, f_conv_b, f_w_gate, f_w_down, loss_target, m_ada_w, m_ada_b, m_ln_tok_g, m_ln_tok_b, m_ln_ch_g, m_ln_ch_b, m_a_w_in, m_a_conv_w, m_a_conv_b, m_a_w_out, m_b_w_pw1, m_b_b_pw1, m_b_conv_w, m_b_conv_b, m_b_ln_g, m_b_ln_b, m_b_w_pw2, m_b_b_pw2, m_f_w_up, m_f_conv_w, m_f_conv_b, m_f_w_gate, m_f_w_down, v_ada_w, v_ada_b, v_ln_tok_g, v_ln_tok_b, v_ln_ch_g, v_ln_ch_b, v_a_w_in, v_a_conv_w, v_a_conv_b, v_a_w_out, v_b_w_pw1, v_b_b_pw1, v_b_conv_w, v_b_conv_b, v_b_ln_g, v_b_ln_b, v_b_w_pw2, v_b_b_pw2, v_f_w_up, v_f_conv_w, v_f_conv_b, v_f_w_gate, v_f_w_down):
    weights = dict(ada_w=ada_w, ada_b=ada_b, ln_tok_g=ln_tok_g, ln_tok_b=ln_tok_b, ln_ch_g=ln_ch_g, ln_ch_b=ln_ch_b, a_w_in=a_w_in, a_conv_w=a_conv_w, a_conv_b=a_conv_b, a_w_out=a_w_out, b_w_pw1=b_w_pw1, b_b_pw1=b_b_pw1, b_conv_w=b_conv_w, b_conv_b=b_conv_b, b_ln_g=b_ln_g, b_ln_b=b_ln_b, b_w_pw2=b_w_pw2, b_b_pw2=b_b_pw2, f_w_up=f_w_up, f_conv_w=f_conv_w, f_conv_b=f_conv_b, f_w_gate=f_w_gate, f_w_down=f_w_down)
    mom_m = dict(ada_w=m_ada_w, ada_b=m_ada_b, ln_tok_g=m_ln_tok_g, ln_tok_b=m_ln_tok_b, ln_ch_g=m_ln_ch_g, ln_ch_b=m_ln_ch_b, a_w_in=m_a_w_in, a_conv_w=m_a_conv_w, a_conv_b=m_a_conv_b, a_w_out=m_a_w_out, b_w_pw1=m_b_w_pw1, b_b_pw1=m_b_b_pw1, b_conv_w=m_b_conv_w, b_conv_b=m_b_conv_b, b_ln_g=m_b_ln_g, b_ln_b=m_b_ln_b, b_w_pw2=m_b_w_pw2, b_b_pw2=m_b_b_pw2, f_w_up=m_f_w_up, f_conv_w=m_f_conv_w, f_conv_b=m_f_conv_b, f_w_gate=m_f_w_gate, f_w_down=m_f_w_down)
    mom_v = dict(ada_w=v_ada_w, ada_b=v_ada_b, ln_tok_g=v_ln_tok_g, ln_tok_b=v_ln_tok_b, ln_ch_g=v_ln_ch_g, ln_ch_b=v_ln_ch_b, a_w_in=v_a_w_in, a_conv_w=v_a_conv_w, a_conv_b=v_a_conv_b, a_w_out=v_a_w_out, b_w_pw1=v_b_w_pw1, b_b_pw1=v_b_b_pw1, b_conv_w=v_b_conv_w, b_conv_b=v_b_conv_b, b_ln_g=v_b_ln_g, b_ln_b=v_b_ln_b, b_w_pw2=v_b_w_pw2, b_b_pw2=v_b_b_pw2, f_w_up=v_f_w_up, f_conv_w=v_f_conv_w, f_conv_b=v_f_conv_b, f_w_gate=v_f_w_gate, f_w_down=v_f_w_down)
    names = list(weights)

    depth, d, n_ada = ada_w.shape
    assert depth == 2 and a_w_in.shape[0] == 1 and b_w_pw1.shape[0] == 1
    s_len = x.shape[1]
    f_loc = f_w_up.shape[-1]
    f_pad = -(-f_loc // LANES) * LANES
    f_all = NDEV * f_pad
    d_loc = d // NDEV
    ka, kb, kf = a_conv_w.shape[1], b_conv_w.shape[1], f_conv_w.shape[1]
    alpha = (2.0 * depth) ** 0.25
    assert a_w_in.shape[-1] == f_pad and f_pad % d_loc == 0
    me = 4 * lax.axis_index("x") + 2 * lax.axis_index("y") + lax.axis_index("c")

    small_shapes = [(d,), (ka, d_loc), (2 * d_loc,), (kb, d_loc), (d_loc,), (d_loc,), (d_loc,), (d_loc,),
                    (depth, kf, f_pad)]
    small_loc = _pack([c[0], a_conv_w[0], b_b_pw1[0], b_conv_w[0], b_conv_b[0], b_ln_g[0], b_ln_b[0],
                       b_b_pw2[0], _pad_last(f_conv_w, f_pad)])
    g_small, g_in, first_done = _exchange([small_loc, a_w_in.astype(BF16)], True, "gather_first")
    gather_out = _exchange_start([_after(a_w_out[0], first_done).astype(BF16)], True, "gather_out_start")
    up_pad = _pad_last(_after(f_w_up, gather_out[-1]), f_pad).astype(BF16)
    gate_pad = _pad_last(f_w_gate, f_pad).astype(BF16)
    down_pad = jnp.pad(f_w_down, ((0, 0), (0, f_pad - f_loc), (0, 0))).astype(BF16)
    col_f = [jnp.stack([up_pad[i], gate_pad[i]]) for i in range(depth)]
    row_b = jnp.concatenate([down_pad[1], b_w_pw2[0].astype(BF16)], axis=0)
    ridx_pw2 = f_pad // d_loc
    gather_f0 = _exchange_start([col_f[0], down_pad[0]], True, "gather_f0_start")

    (c_all, acw_g, bb1_g, bcw_g, bcb_g, blg_g, blb_g, bb2_g, fcw_g) = _unpack(g_small, small_shapes, (NDEV,))
    a_cw = acw_g.transpose(1, 0, 2).reshape(ka, d)
    b_cw = bcw_g.transpose(1, 0, 2).reshape(kb, d)
    b_b1 = bb1_g.reshape(1, 2 * d)
    b_cb, b_lg, b_lb, b_b2 = (t.reshape(1, d) for t in (bcb_g, blg_g, blb_g, bb2_g))
    f_cw = fcw_g.transpose(1, 2, 0, 3).reshape(depth, kf, f_all)
    f_cb = _pad_last(f_conv_b.reshape(depth, NDEV, f_loc), f_pad).reshape(depth, 1, f_all)

    ada_b_loc = lax.dynamic_slice(ada_b, (0, me * n_ada), (depth, n_ada))
    mod_part = _ada_fwd(c_all, ada_w, ada_b_loc, "ada_fwd")
    mod_g, _ = _exchange([mod_part.reshape(depth * NDEV, n_ada)], True, "gather_mod")
    mod_all = mod_g.reshape(NDEV, depth, NDEV, n_ada).transpose(1, 2, 0, 3).reshape(depth, NDEV, 6 * d)
    mod = lax.dynamic_slice(mod_all, (0, me, 0), (depth, 1, 6 * d))[:, 0]

    def mod_rows(i):
        return [mod[i:i + 1, j * d:(j + 1) * d] for j in range(6)]

    zeros_d = jnp.zeros((1, d), F32)
    zeros_f = jnp.zeros((1, f_all), F32)
    x0 = x[0]

    sh_t0, sc_t0, g_t0, sh_c0, sc_c0, g_c0 = mod_rows(0)
    sh_t1, sc_t1, g_t1, sh_c1, sc_c1, g_c1 = mod_rows(1)

    sc_t0 = _after(sc_t0, gather_f0[-1])
    bcv, = _mm_fwd(x0, sc_t0, sh_t0, jnp.zeros((1, 3 * d), F32), g_in, (0,), "a_in_fwd")
    y0 = _gateconv_fwd(bcv, a_cw, a_conv_b, "a_conv_fwd")
    g_out, _ = _exchange_wait(gather_out, y0, True, "gather_out_wait")
    y_a, x1, xh1, rs1 = _mm_ln(y0, g_out, d_loc, 0, x0, g_t0, ln_tok_g[0:1], ln_tok_b[0:1], zeros_d,
                               alpha, "a_out_ln_fwd")

    def ffn_fwd(xin, sc, sh, gate, gam, bet, g_colf, g_rowf, layer, tag):
        u0, vg = _mm_fwd(xin, sc, sh, zeros_f, g_colf, (0, 1), "f_upgate_fwd" + tag)
        t = _ffn_mid_fwd(u0, vg, f_cw[layer], f_cb[layer], "f_mid_fwd" + tag)
        y, xo, xh, rs = _mm_ln(t, g_rowf, f_pad, 0, xin, gate, gam, bet, zeros_d, alpha, "f_down_ln_fwd" + tag)
        return u0, vg, t, y, xo, xh, rs

    g_colf0, g_rowf0, landed = _exchange_wait(gather_f0, x1, True, "gather_f0_wait")
    gather_1 = _exchange_start([_after(b_w_pw1, landed).astype(BF16), col_f[1], row_b], True, "gather_1_start")
    sc_c0 = _after(sc_c0, gather_1[-1])
    u0_0, vg_0, t_0, y_f0, x2, xh2, rs2 = ffn_fwd(x1, sc_c0, sh_c0, g_c0, ln_ch_g[0:1], ln_ch_b[0:1],
                                                  g_colf0, g_rowf0, 0, "0")

    g_pw1, g_colf1, g_rowb, _ = _exchange_wait(gather_1, x2, True, "gather_1_wait")
    ub, = _mm_fwd(x2, sc_t1, sh_t1, b_b1, g_pw1, (0,), "b_pw1_fwd")
    a2, a4 = _b_mid_fwd(ub, b_cw, b_cb, b_lg, b_lb, "b_mid_fwd")
    y_b, x3, xh3, rs3 = _mm_ln(a4, g_rowb, d_loc, ridx_pw2, x2, g_t1, ln_tok_g[1:2], ln_tok_b[1:2], b_b2,
                               alpha, "b_pw2_ln_fwd")
    u0_1, vg_1, t_1, y_f1, x4, xh4, rs4 = ffn_fwd(x3, sc_c1, sh_c1, g_c1, ln_ch_g[1:2], ln_ch_b[1:2],
                                                  g_colf1, g_rowb, 1, "1")

    dx4, loss_part = _loss_head(x4, loss_target[0], "loss_head")
    loss = lax.psum(loss_part[0, 0], MESH_AXES)

    def ffn_bwd(dxo, xin, sc, sh, gate, gam, u0, vg, t, y, xh, rs, g_colf, g_rowf, layer, tag):
        dy, dres, acc = _ln_bwd(dxo, xh, rs, gam, y, gate, alpha, "f_ln_bwd" + tag)
        dt = _mm_nt_row(dy, g_rowf, f_pad, 0, "f_down_dx" + tag)
        dw_down = _mm_tn_row(t, dy, f_pad, f_loc, "f_down_dw" + tag)
        du0, dvg, dcw, dcb = _ffn_mid_bwd(u0, vg, dt, f_cw[layer], f_cb[layer], "f_mid_bwd" + tag)
        dw_up = _mm_tn_col_t(xin, sc, sh, du0, f_loc, "f_up_dw" + tag)
        dw_gate = _mm_tn_col_t(xin, sc, sh, dvg, f_loc, "f_gate_dw" + tag)
        scatter = _exchange_start([dw_up, dw_gate, dw_down], False, "scatter_f%s_start" % tag)
        dxin, acc2 = _mm_nt_mod([du0, dvg], g_colf, (0, 1), xin, _after(sc, scatter[-1]), dres, "f_upgate_dx" + tag)
        return dxin, acc, acc2, scatter, dcw, dcb

    dx3, accf1, acc2f1, scatter_f1, dfcw1, dfcb1 = ffn_bwd(
        dx4, x3, sc_c1, sh_c1, g_c1, ln_ch_g[1:2], u0_1, vg_1, t_1, y_f1, xh4, rs4, g_colf1, g_rowb, 1, "1")

    dy, dres, accb = _ln_bwd(dx3, xh3, rs3, ln_tok_g[1:2], y_b, g_t1, alpha, "b_ln_bwd")
    da4 = _mm_nt_row(dy, g_rowb, d_loc, ridx_pw2, "b_pw2_dx")
    dw_pw2 = _mm_tn_row(a4, dy, d_loc, d_loc, "b_pw2_dw")
    du, dbcw, dbcb, dblg, dblb, dbb1 = _b_mid_bwd(ub, a2, da4, b_cw, b_lg, b_lb, "b_mid_bwd")
    dw_pw1 = _mm_tn_col(x2, sc_t1, sh_t1, du, "b_pw1_dw")
    scatter_b = _exchange_start([dw_pw1, dw_pw2], False, "scatter_b_start")
    dx2, acc2b = _mm_nt_mod([du], g_pw1, (0,), x2, _after(sc_t1, scatter_b[-1]), dres, "b_pw1_dx")

    dx1, accf0, acc2f0, scatter_f0, dfcw0, dfcb0 = ffn_bwd(
        dx2, x1, sc_c0, sh_c0, g_c0, ln_ch_g[0:1], u0_0, vg_0, t_0, y_f0, xh2, rs2, g_colf0, g_rowf0, 0, "0")

    dy, dres, acca = _ln_bwd(dx1, xh1, rs1, ln_tok_g[0:1], y_a, g_t0, alpha, "a_ln_bwd")
    dy0 = _mm_nt_row(dy, g_out, d_loc, 0, "a_out_dx")
    dw_out = _mm_tn_row(y0, dy, d_loc, d_loc, "a_out_dw")
    dbcv, dacw, dacb = _gateconv_bwd(bcv, dy0, a_cw, a_conv_b, "a_conv_bwd")
    dw_in = _mm_tn_col(x0, sc_t0, sh_t0, dbcv, "a_in_dw")
    scatter_a = _exchange_start([dw_in, dw_out], False, "scatter_a_start")
    dx0, acc2a = _mm_nt_mod([dbcv], g_in, (0,), x0, _after(sc_t0, scatter_a[-1]), dres, "a_in_dx")

    def dmod_row(acc2_t, acc_t, acc2_c, acc_c):
        return jnp.concatenate([acc2_t[1], acc2_t[0], acc_t[2], acc2_c[1], acc2_c[0], acc_c[2]])

    dmod = jnp.stack([dmod_row(acc2a, acca, acc2f0, accf0), dmod_row(acc2b, accb, acc2f1, accf1)])

    def unpad_f(a):
        return a.reshape(a.shape[:-1] + (NDEV, f_pad))[..., :f_loc].reshape(a.shape[:-1] + (NDEV * f_loc,))

    small_grads = [
        dmod,
        jnp.stack([acca[0], accb[0]]), jnp.stack([acca[1], accb[1]]),
        jnp.stack([accf0[0], accf1[0]]), jnp.stack([accf0[1], accf1[1]]),
        dacb,
        unpad_f(jnp.concatenate([dfcb0, dfcb1], axis=0)),
        dacw, dbb1, dbcw, dbcb, dblg, dblb, accb[3:4],
        jnp.stack([dfcw0, dfcw1]),
    ]
    small_grad_shapes = [tuple(g.shape) for g in small_grads]
    gather_small = _exchange_start([_pack(small_grads)], True, "gather_small_start")

    grads, deltas, new_m, new_v = {}, {}, {}, {}

    def adamw(k, glist, transposed=False):
        def view(a):
            a = jnp.swapaxes(a, 1, 2) if transposed else a
            return a.reshape(len(glist), -1, a.shape[-1])

        w = view(weights[k])
        outs = _adamw(w, [g.reshape(g.shape[0], -1, w.shape[-1]) for g in glist],
                      view(mom_m[k]), view(mom_v[k]), "adamw_" + k)
        if transposed:
            outs = [jnp.swapaxes(o, 1, 2) for o in outs]
        grads[k], deltas[k], new_m[k], new_v[k] = (o.reshape(weights[k].shape) for o in outs)

    r_up1, r_gate1, r_down1, _ = _exchange_wait(scatter_f1, dx0, False, "scatter_f1_wait")
    r_pw1, r_pw2, _ = _exchange_wait(scatter_b, r_down1, False, "scatter_b_wait")
    adamw("b_w_pw1", [r_pw1])
    adamw("b_w_pw2", [r_pw2])
    r_up0, r_gate0, r_down0, _ = _exchange_wait(scatter_f0, deltas["b_w_pw2"], False, "scatter_f0_wait")
    adamw("f_w_up", [r_up0, r_up1], transposed=True)
    adamw("f_w_gate", [r_gate0, r_gate1], transposed=True)
    adamw("f_w_down", [r_down0, r_down1])

    sg_all, _ = _exchange_wait(gather_small, deltas["f_w_down"], True, "gather_small_wait")
    sg_sum = _sum_parts(sg_all, "sum_small_grads")
    (g_ada_b, g_ltg, g_ltb, g_lcg, g_lcb, g_acb, g_fcb, g_acw, g_bb1, g_bcw, g_bcb, g_blg, g_blb, g_bb2,
     g_fcw) = _unpack(sg_sum, small_grad_shapes)

    def my_cols(a, width):
        return lax.dynamic_slice_in_dim(a, me * width, width, axis=a.ndim - 1)

    g_fcw_loc = my_cols(g_fcw, f_pad)[..., :f_loc]
    small = dict(
        ada_b=g_ada_b, ln_tok_g=g_ltg, ln_tok_b=g_ltb, ln_ch_g=g_lcg, ln_ch_b=g_lcb, a_conv_b=g_acb, f_conv_b=g_fcb,
        a_conv_w=my_cols(g_acw, d_loc)[None], b_b_pw1=my_cols(g_bb1, 2 * d_loc), b_conv_w=my_cols(g_bcw, d_loc)[None],
        b_conv_b=my_cols(g_bcb, d_loc), b_ln_g=my_cols(g_blg, d_loc), b_ln_b=my_cols(g_blb, d_loc),
        b_b_pw2=my_cols(g_bb2, d_loc), f_conv_w=g_fcw_loc)

    dmod_all = sg_all.reshape(NDEV, -1)[:, :depth * 6 * d].reshape(NDEV, depth, 6 * d)
    dmod_cols = my_cols(dmod_all, n_ada).transpose(1, 0, 2)
    g_ada_w = _ada_bwd(c_all.T, dmod_cols, "ada_bwd")

    adamw("ada_w", [g_ada_w[0:1], g_ada_w[1:2]])
    for k, g in small.items():
        adamw(k, [g[None]])

    r_in, r_out, _ = _exchange_wait(scatter_a, deltas["ada_w"], False, "scatter_a_wait")
    adamw("a_w_in", [r_in])
    adamw("a_w_out", [r_out])

    return (loss, dx0[None], *[grads[k] for k in names], *[deltas[k] for k in names],
            *[new_m[k] for k in names], *[new_v[k] for k in names])
```

```python
import functools

import jax
import jax.numpy as jnp
from jax import lax
from jax.experimental import pallas as pl
from jax.experimental.pallas import tpu as pltpu

NDEV = 8
MESH_AXES = ("x", "y", "c")
LANES = 128
SUBLANES = 8
VMEM_LIMIT = 56 * 1024 * 1024
LN_EPS = 1e-5
SHORT_PAD = 16
LONG_PAD = 32
CHUNK = 16
ADAM_LR, ADAM_B1, ADAM_B2, ADAM_EPS, ADAM_WD, ADAM_STEP = 0.001, 0.9, 0.999, 1e-08, 0.01, 10

F32 = jnp.float32
BF16 = jnp.bfloat16
MESH = pl.DeviceIdType.MESH
NT = (((1,), (1,)), ((), ()))
TN = (((0,), (0,)), ((), ()))


def _tile(n, target, mult=SUBLANES):
    best = None
    for t in range(mult, min(n, target) + 1, mult):
        if n % t == 0:
            best = t
    return best if best is not None else n


def _full(shape):
    nd = len(shape)
    return pl.BlockSpec(shape, lambda *_: (0,) * nd)


def _cp(*sem):
    return pltpu.CompilerParams(dimension_semantics=sem, vmem_limit_bytes=VMEM_LIMIT)


def _sigmoid(x):
    return 1.0 / (1.0 + jnp.exp(-x))


def _peer(x, y, c, d):
    return ((1 - x) if d & 4 else x, (1 - y) if d & 2 else y, (1 - c) if d & 1 else c)


def _lin(p):
    return 4 * p[0] + 2 * p[1] + p[2]


def _remote_copies(src_refs, land_refs, send_sems, recv_sems, gather):
    x, y, c = (lax.axis_index(a) for a in MESH_AXES)
    me = _lin((x, y, c))
    sends, recvs = [], []
    for i, (src_ref, land_ref) in enumerate(zip(src_refs, land_refs)):
        for d in range(1, NDEV):
            peer = _peer(x, y, c, d)
            k = i * (NDEV - 1) + d - 1
            src = src_ref if gather else src_ref.at[_lin(peer)]
            for slot, out in ((me, sends), (_lin(peer), recvs)):
                out.append(pltpu.make_async_remote_copy(
                    src_ref=src, dst_ref=land_ref.at[slot], send_sem=send_sems.at[k], recv_sem=recv_sems.at[k],
                    device_id=peer, device_id_type=MESH))
    return sends, recvs


def _exchange(srcs, gather, name):
    n = len(srcs)

    def body(*refs):
        src_refs, out_refs, token = refs[:n], refs[n:2 * n], refs[2 * n]
        send_sems, recv_sems, local_sems = refs[2 * n + 1:]
        me = _lin(tuple(lax.axis_index(a) for a in MESH_AXES))
        local = []
        for i in range(n):
            mine = src_refs[i] if gather else src_refs[i].at[me]
            cp = pltpu.make_async_copy(mine, out_refs[i].at[me], local_sems.at[i])
            cp.start()
            local.append(cp)
        sends, recvs = _remote_copies(src_refs, out_refs, send_sems, recv_sems, gather)
        for snd in sends:
            snd.start()
        token[...] = jnp.zeros_like(token)
        for snd, rcv in zip(sends, recvs):
            snd.wait_send()
            rcv.wait_recv()
        for cp in local:
            cp.wait()

    out_shape = [jax.ShapeDtypeStruct(((NDEV,) + s.shape) if gather else s.shape, s.dtype) for s in srcs]
    out_shape.append(jax.ShapeDtypeStruct((SUBLANES, LANES), F32))
    any_spec = pl.BlockSpec(memory_space=pl.ANY)
    return pl.pallas_call(
        body, name=name, out_shape=out_shape,
        in_specs=[any_spec] * n, out_specs=[any_spec] * n + [pl.BlockSpec(memory_space=pltpu.VMEM)],
        scratch_shapes=[pltpu.SemaphoreType.DMA((n * (NDEV - 1),)),
                        pltpu.SemaphoreType.DMA((n * (NDEV - 1),)),
                        pltpu.SemaphoreType.DMA((n,))],
    )(*srcs)


HBM_SPEC = pl.BlockSpec(memory_space=pltpu.HBM)
SEM_SPEC = pl.BlockSpec(memory_space=pltpu.SEMAPHORE)
SIDE_EFFECT = pltpu.SideEffectType.DATAFLOW_SIDE_EFFECTING


def _exchange_start(srcs, gather, name):
    n = len(srcs)
    me = _lin(tuple(lax.axis_index(a) for a in MESH_AXES))
    lands = []
    for s in srcs:
        own = s if gather else lax.dynamic_index_in_dim(s, me, 0, keepdims=False)
        shape = ((NDEV,) + s.shape) if gather else s.shape
        lands.append(lax.dynamic_update_index_in_dim(lax.empty(shape, s.dtype), own, me, 0))

    def body(*refs):
        src_refs, land_refs = refs[:n], refs[n:2 * n]
        send_sems, recv_sems, token = refs[2 * n], refs[2 * n + 1], refs[-1]
        sends, _ = _remote_copies(src_refs, land_refs, send_sems, recv_sems, gather)
        for snd in sends:
            snd.start()
        token[...] = jnp.zeros_like(token)

    operands = [pltpu.with_memory_space_constraint(a, pltpu.HBM) for a in list(srcs) + lands]
    nsem = n * (NDEV - 1)
    return pl.pallas_call(
        body, name=name,
        out_shape=(pltpu.SemaphoreType.DMA((nsem,)), pltpu.SemaphoreType.DMA((nsem,)),
                   *[pltpu.HBM(a.shape, a.dtype) for a in operands],
                   jax.ShapeDtypeStruct((SUBLANES, LANES), F32)),
        in_specs=[HBM_SPEC] * (2 * n),
        out_specs=(SEM_SPEC, SEM_SPEC, *([HBM_SPEC] * (2 * n)), pl.BlockSpec(memory_space=pltpu.VMEM)),
        input_output_aliases={i: 2 + i for i in range(2 * n)},
        compiler_params=pltpu.CompilerParams(has_side_effects=SIDE_EFFECT),
    )(*operands)


def _exchange_wait(handle, after, gather, name):
    send_sems, recv_sems, *thru = handle[:-1]
    n = len(thru) // 2

    def body(*refs):
        src_refs, land_refs = refs[:n], refs[n:2 * n]
        sends, recvs = _remote_copies(src_refs, land_refs, refs[2 * n], refs[2 * n + 1], gather)
        for snd, rcv in zip(sends, recvs):
            snd.wait_send()
            rcv.wait_recv()
        refs[-1][...] = jnp.zeros_like(refs[-1])

    outs = pl.pallas_call(
        body, name=name,
        out_shape=(*[pltpu.HBM(a.shape, a.dtype) for a in thru], jax.ShapeDtypeStruct((SUBLANES, LANES), F32)),
        in_specs=[HBM_SPEC] * (2 * n) + [SEM_SPEC, SEM_SPEC, pl.BlockSpec(memory_space=pl.ANY)],
        out_specs=[HBM_SPEC] * (2 * n) + [pl.BlockSpec(memory_space=pltpu.VMEM)],
        input_output_aliases={i: i for i in range(2 * n)},
        compiler_params=pltpu.CompilerParams(has_side_effects=SIDE_EFFECT),
    )(*thru, send_sems, recv_sems, after)
    return outs[n:]


def _after(value, token):
    return value + token[0, 0]


def _mm_fwd(x, sc, sh, bias, wg, widxs, name):
    s_len, kdim = x.shape
    n = wg.shape[-1]
    ncol = NDEV * n
    tm = _tile(s_len, 256)
    nw = len(widxs)

    def body(x_ref, sc_ref, sh_ref, b_ref, *rest):
        w_refs, o_refs = rest[:nw], rest[nw:]
        h = (x_ref[...] * (1.0 + sc_ref[...]) + sh_ref[...]).astype(BF16)
        for w_ref, o_ref in zip(w_refs, o_refs):
            for k in range(NDEV):
                cols = slice(k * n, (k + 1) * n)
                o_ref[:, cols] = (jnp.dot(h, w_ref[k], preferred_element_type=F32) + b_ref[:, cols]).astype(BF16)

    w_specs = [pl.BlockSpec((NDEV, None, kdim, n), functools.partial(lambda i, w: (0, w, 0, 0), w=w))
               for w in widxs]
    return pl.pallas_call(
        body, name=name, grid=(s_len // tm,),
        in_specs=[pl.BlockSpec((tm, kdim), lambda i: (i, 0)), _full((1, kdim)), _full((1, kdim)),
                  _full((1, ncol))] + w_specs,
        out_specs=[pl.BlockSpec((tm, ncol), lambda i: (i, 0))] * nw,
        out_shape=[jax.ShapeDtypeStruct((s_len, ncol), BF16)] * nw,
        compiler_params=_cp("parallel"),
    )(x, sc, sh, bias, *([wg] * nw))


def _mm_ln(a, wg, r, ridx, xres, gate, gam, bet, bias, alpha, name):
    s_len = a.shape[0]
    d = wg.shape[-1]
    tm = _tile(s_len, 256)

    def body(a_ref, w_ref, x_ref, g_ref, gam_ref, bet_ref, b_ref, y_ref, xo_ref, xh_ref, rs_ref):
        acc = None
        for k in range(NDEV):
            p = jnp.dot(a_ref[:, k * r:(k + 1) * r], w_ref[k], preferred_element_type=F32)
            acc = p if acc is None else acc + p
        y = acc + b_ref[...]
        z = alpha * x_ref[...] + g_ref[...] * y
        mu = jnp.mean(z, axis=-1, keepdims=True)
        zc = z - mu
        var = jnp.mean(zc * zc, axis=-1, keepdims=True)
        rstd = lax.rsqrt(var + LN_EPS)
        xh = zc * rstd
        y_ref[...] = y.astype(BF16)
        xh_ref[...] = xh
        rs_ref[...] = rstd
        xo_ref[...] = xh * gam_ref[...] + bet_ref[...]

    row = pl.BlockSpec((tm, d), lambda i: (i, 0))
    vec = _full((1, d))
    return pl.pallas_call(
        body, name=name, grid=(s_len // tm,),
        in_specs=[pl.BlockSpec((tm, NDEV * r), lambda i: (i, 0)),
                  pl.BlockSpec((NDEV, r, d), lambda i: (0, ridx, 0)), row, vec, vec, vec, vec],
        out_specs=[row, row, row, pl.BlockSpec((tm, 1), lambda i: (i, 0))],
        out_shape=[jax.ShapeDtypeStruct((s_len, d), BF16)] + [jax.ShapeDtypeStruct((s_len, d), F32)] * 2
        + [jax.ShapeDtypeStruct((s_len, 1), F32)],
        compiler_params=_cp("parallel"),
    )(a, wg, xres, gate, gam, bet, bias)


def _ln_bwd(dxo, xh, rstd, gam, y, gate, alpha, name):
    s_len, d = dxo.shape
    tm = _tile(s_len, 256)

    def body(d_ref, xh_ref, rs_ref, gam_ref, y_ref, g_ref, dy_ref, dres_ref, acc_ref):
        @pl.when(pl.program_id(0) == 0)
        def _():
            acc_ref[...] = jnp.zeros_like(acc_ref)

        dxo_t = d_ref[...]
        xh_t = xh_ref[...]
        dxh = dxo_t * gam_ref[...]
        m1 = jnp.mean(dxh, axis=-1, keepdims=True)
        m2 = jnp.mean(dxh * xh_t, axis=-1, keepdims=True)
        dz = rs_ref[...] * (dxh - m1 - xh_t * m2)
        dy = g_ref[...] * dz
        dy_ref[...] = dy.astype(BF16)
        dres_ref[...] = alpha * dz
        acc_ref[0:1, :] += jnp.sum(dxo_t * xh_t, axis=0, keepdims=True)
        acc_ref[1:2, :] += jnp.sum(dxo_t, axis=0, keepdims=True)
        acc_ref[2:3, :] += jnp.sum(dz * y_ref[...].astype(F32), axis=0, keepdims=True)
        acc_ref[3:4, :] += jnp.sum(dy, axis=0, keepdims=True)

    row = pl.BlockSpec((tm, d), lambda i: (i, 0))
    vec = _full((1, d))
    return pl.pallas_call(
        body, name=name, grid=(s_len // tm,),
        in_specs=[row, row, pl.BlockSpec((tm, 1), lambda i: (i, 0)), vec, row, vec],
        out_specs=[row, row, _full((SUBLANES, d))],
        out_shape=[jax.ShapeDtypeStruct((s_len, d), BF16), jax.ShapeDtypeStruct((s_len, d), F32),
                   jax.ShapeDtypeStruct((SUBLANES, d), F32)],
        compiler_params=_cp("arbitrary"),
    )(dxo, xh, rstd, gam, y, gate)


def _mm_nt_row(dy, wg, r, ridx, name):
    s_len, d = dy.shape
    tm = _tile(s_len, 256)

    def body(dy_ref, w_ref, o_ref):
        g = dy_ref[...]
        for k in range(NDEV):
            o_ref[:, k * r:(k + 1) * r] = lax.dot_general(g, w_ref[k], NT,
                                                          preferred_element_type=F32).astype(BF16)

    return pl.pallas_call(
        body, name=name, grid=(s_len // tm,),
        in_specs=[pl.BlockSpec((tm, d), lambda i: (i, 0)), pl.BlockSpec((NDEV, r, d), lambda i: (0, ridx, 0))],
        out_specs=pl.BlockSpec((tm, NDEV * r), lambda i: (i, 0)),
        out_shape=jax.ShapeDtypeStruct((s_len, NDEV * r), BF16),
        compiler_params=_cp("parallel"),
    )(dy, wg)


def _mm_nt_mod(dos, wg, widxs, xin, sc, dres, name):
    s_len, kdim = xin.shape
    n = wg.shape[-1]
    tm = _tile(s_len, 256)
    nw = len(widxs)

    def body(*refs):
        do_refs, w_refs = refs[:nw], refs[nw:2 * nw]
        x_ref, sc_ref, dres_ref, dx_ref, acc_ref = refs[2 * nw:]

        @pl.when(pl.program_id(0) == 0)
        def _():
            acc_ref[...] = jnp.zeros_like(acc_ref)

        dh = None
        for do_ref, w_ref in zip(do_refs, w_refs):
            for k in range(NDEV):
                p = lax.dot_general(do_ref[:, k * n:(k + 1) * n], w_ref[k], NT, preferred_element_type=F32)
                dh = p if dh is None else dh + p
        dx_ref[...] = dh * (1.0 + sc_ref[...]) + dres_ref[...]
        acc_ref[0:1, :] += jnp.sum(dh * x_ref[...], axis=0, keepdims=True)
        acc_ref[1:2, :] += jnp.sum(dh, axis=0, keepdims=True)

    row = pl.BlockSpec((tm, kdim), lambda i: (i, 0))
    w_specs = [pl.BlockSpec((NDEV, None, kdim, n), functools.partial(lambda i, w: (0, w, 0, 0), w=w))
               for w in widxs]
    return pl.pallas_call(
        body, name=name, grid=(s_len // tm,),
        in_specs=[pl.BlockSpec((tm, NDEV * n), lambda i: (i, 0))] * nw + w_specs + [row, _full((1, kdim)), row],
        out_specs=[row, _full((SUBLANES, kdim))],
        out_shape=[jax.ShapeDtypeStruct((s_len, kdim), F32), jax.ShapeDtypeStruct((SUBLANES, kdim), F32)],
        compiler_params=_cp("arbitrary"),
    )(*dos, *([wg] * nw), xin, sc, dres)


def _mm_tn_col(x, sc, sh, do, name):
    s_len, kdim = x.shape
    n = do.shape[1] // NDEV
    ts = _tile(s_len, 512)
    nsteps = s_len // ts

    def body(x_ref, sc_ref, sh_ref, do_ref, o_ref, acc_ref):
        @pl.when(pl.program_id(0) == 0)
        def _():
            acc_ref[...] = jnp.zeros_like(acc_ref)

        h = (x_ref[...] * (1.0 + sc_ref[...]) + sh_ref[...]).astype(BF16)
        for k in range(NDEV):
            acc_ref[k] += lax.dot_general(h, do_ref[:, k * n:(k + 1) * n], TN, preferred_element_type=F32)

        @pl.when(pl.program_id(0) == nsteps - 1)
        def _():
            o_ref[...] = acc_ref[...].astype(BF16)

    return pl.pallas_call(
        body, name=name, grid=(nsteps,),
        in_specs=[pl.BlockSpec((ts, kdim), lambda i: (i, 0)), _full((1, kdim)), _full((1, kdim)),
                  pl.BlockSpec((ts, NDEV * n), lambda i: (i, 0))],
        out_specs=_full((NDEV, kdim, n)),
        out_shape=jax.ShapeDtypeStruct((NDEV, kdim, n), BF16),
        scratch_shapes=[pltpu.VMEM((NDEV, kdim, n), F32)],
        compiler_params=_cp("arbitrary"),
    )(x, sc, sh, do)


def _mm_tn_col_t(x, sc, sh, do, rows_out, name):
    s_len, kdim = x.shape
    n = do.shape[1] // NDEV
    ts = _tile(s_len, 512)
    nsteps = s_len // ts

    def body(x_ref, sc_ref, sh_ref, do_ref, o_ref, acc_ref):
        @pl.when(pl.program_id(0) == 0)
        def _():
            acc_ref[...] = jnp.zeros_like(acc_ref)

        h = (x_ref[...] * (1.0 + sc_ref[...]) + sh_ref[...]).astype(BF16)
        for k in range(NDEV):
            acc_ref[k] += lax.dot_general(do_ref[:, k * n:(k + 1) * n], h, TN, preferred_element_type=F32)

        @pl.when(pl.program_id(0) == nsteps - 1)
        def _():
            o_ref[...] = acc_ref[:, 0:rows_out, :].astype(BF16)

    return pl.pallas_call(
        body, name=name, grid=(nsteps,),
        in_specs=[pl.BlockSpec((ts, kdim), lambda i: (i, 0)), _full((1, kdim)), _full((1, kdim)),
                  pl.BlockSpec((ts, NDEV * n), lambda i: (i, 0))],
        out_specs=_full((NDEV, rows_out, kdim)),
        out_shape=jax.ShapeDtypeStruct((NDEV, rows_out, kdim), BF16),
        scratch_shapes=[pltpu.VMEM((NDEV, n, kdim), F32)],
        compiler_params=_cp("arbitrary"),
    )(x, sc, sh, do)


def _mm_tn_row(a, dy, r, rows_out, name):
    s_len, d = dy.shape
    ts = _tile(s_len, 512)
    nsteps = s_len // ts

    def body(a_ref, dy_ref, o_ref, acc_ref):
        @pl.when(pl.program_id(0) == 0)
        def _():
            acc_ref[...] = jnp.zeros_like(acc_ref)

        g = dy_ref[...]
        for k in range(NDEV):
            acc_ref[k] += lax.dot_general(a_ref[:, k * r:(k + 1) * r], g, TN, preferred_element_type=F32)

        @pl.when(pl.program_id(0) == nsteps - 1)
        def _():
            o_ref[...] = acc_ref[:, 0:rows_out, :].astype(BF16)

    return pl.pallas_call(
        body, name=name, grid=(nsteps,),
        in_specs=[pl.BlockSpec((ts, NDEV * r), lambda i: (i, 0)), pl.BlockSpec((ts, d), lambda i: (i, 0))],
        out_specs=_full((NDEV, rows_out, d)),
        out_shape=jax.ShapeDtypeStruct((NDEV, rows_out, d), BF16),
        scratch_shapes=[pltpu.VMEM((NDEV, r, d), F32)],
        compiler_params=_cp("arbitrary"),
    )(a, dy)


def _prev_spec(ts, pad, cb, col):
    return pl.BlockSpec((pad, cb), lambda *g: (jnp.maximum(g[-1] * (ts // pad) - 1, 0), col(g)))


def _next_spec(ts, pad, cb, col, s_len):
    return pl.BlockSpec((pad, cb), lambda *g: (jnp.minimum((g[-1] + 1) * (ts // pad), s_len // pad - 1), col(g)))


class _F32Loads:
    def __init__(self, ref):
        self.ref = ref

    def __getitem__(self, idx):
        return self.ref[idx].astype(F32)


def _direct(buf_ref):
    return lambda off, rows: buf_ref[off:off + rows, :]


def _make_shifts(sh_ref, nrows):
    for r in range(1, SUBLANES):
        sh_ref[r, 0:nrows - SUBLANES, :] = sh_ref[0, r:r + nrows - SUBLANES, :]


def _shifted(sh_ref):
    def read(off, rows):
        r = off % SUBLANES
        return sh_ref[r, off - r:off - r + rows, :]
    return read


def _conv_fwd_rows(read, w_ref, b_ref, ktaps, pad, r0, rows):
    acc = None
    for j in range(ktaps):
        term = w_ref[ktaps - 1 - j:ktaps - j, :] * read(pad - j + r0, rows)
        acc = term if acc is None else acc + term
    return acc + b_ref[...]


def _conv_bwd_rows(read, x_rows, w_ref, dwacc_ref, ktaps, r0, rows):
    acc = None
    for j in range(ktaps):
        sl = read(j + r0, rows)
        term = w_ref[ktaps - 1 - j:ktaps - j, :] * sl
        acc = term if acc is None else acc + term
        prod = x_rows * sl
        fold = prod[0:SUBLANES]
        for q in range(1, rows // SUBLANES):
            fold = fold + prod[q * SUBLANES:(q + 1) * SUBLANES]
        tap = ktaps - 1 - j
        dwacc_ref[tap * SUBLANES:(tap + 1) * SUBLANES, :] += fold
    return acc


def _flush_dw(dwacc_ref, dw_ref, ktaps):
    for tap in range(ktaps):
        dw_ref[tap:tap + 1, :] = jnp.sum(dwacc_ref[tap * SUBLANES:(tap + 1) * SUBLANES, :], axis=0, keepdims=True)


def _gateconv_fwd(bcv, cw, cb, name):
    s_len, d3 = bcv.shape
    d = d3 // 3
    ktaps = cw.shape[0]
    pad = SHORT_PAD
    ts = _tile(s_len, 256)

    def body(gb_ref, gc_ref, v_ref, gcp_ref, vp_ref, w_ref, b_ref, o_ref, pbuf):
        gb_ref, gc_ref, v_ref, gcp_ref, vp_ref = map(_F32Loads, (gb_ref, gc_ref, v_ref, gcp_ref, vp_ref))
        s = pl.program_id(0)
        pbuf[0:pad, :] = jnp.where(s > 0, gcp_ref[...] * vp_ref[...], 0.0)
        pbuf[pad:pad + ts, :] = gc_ref[...] * v_ref[...]
        for r0 in range(0, ts, CHUNK):
            q = _conv_fwd_rows(_direct(pbuf), w_ref, b_ref, ktaps, pad, r0, CHUNK)
            o_ref[r0:r0 + CHUNK, :] = (gb_ref[r0:r0 + CHUNK, :] * q).astype(BF16)

    def cur(part):
        return pl.BlockSpec((ts, d), lambda s: (s, part))

    return pl.pallas_call(
        body, name=name, grid=(s_len // ts,),
        in_specs=[cur(0), cur(1), cur(2),
                  _prev_spec(ts, pad, d, lambda g: 1), _prev_spec(ts, pad, d, lambda g: 2),
                  _full((ktaps, d)), _full((1, d))],
        out_specs=pl.BlockSpec((ts, d), lambda s: (s, 0)),
        out_shape=jax.ShapeDtypeStruct((s_len, d), BF16),
        scratch_shapes=[pltpu.VMEM((pad + ts, d), F32)],
        compiler_params=_cp("parallel"),
    )(bcv, bcv, bcv, bcv, bcv, cw, cb)


def _gateconv_bwd(bcv, dy0, cw, cb, name):
    s_len, d3 = bcv.shape
    d = d3 // 3
    ktaps = cw.shape[0]
    pad = SHORT_PAD
    ts = _tile(s_len, 256)
    nsteps = s_len // ts

    def body(gb_ref, gc_ref, v_ref, gcp_ref, vp_ref, gbn_ref, dy_ref, dyn_ref, w_ref, b_ref,
             o_ref, dw_ref, db_ref, pbuf, dqbuf, dwacc):
        gb_ref, gc_ref, v_ref, gcp_ref, vp_ref, gbn_ref, dy_ref, dyn_ref = map(
            _F32Loads, (gb_ref, gc_ref, v_ref, gcp_ref, vp_ref, gbn_ref, dy_ref, dyn_ref))
        s = pl.program_id(0)

        @pl.when(s == 0)
        def _():
            dwacc[...] = jnp.zeros_like(dwacc)
            db_ref[...] = jnp.zeros_like(db_ref)

        pbuf[0:pad, :] = jnp.where(s > 0, gcp_ref[...] * vp_ref[...], 0.0)
        pbuf[pad:pad + ts, :] = gc_ref[...] * v_ref[...]
        dq = dy_ref[...] * gb_ref[...]
        dqbuf[0:ts, :] = dq
        dqbuf[ts:ts + pad, :] = jnp.where(s < nsteps - 1, dyn_ref[...] * gbn_ref[...], 0.0)
        db_ref[...] += jnp.sum(dq, axis=0, keepdims=True)
        for r0 in range(0, ts, CHUNK):
            rows = slice(r0, r0 + CHUNK)
            q = _conv_fwd_rows(_direct(pbuf), w_ref, b_ref, ktaps, pad, r0, CHUNK)
            o_ref[rows, 0:d] = (dy_ref[rows, :] * q).astype(BF16)
            dp = _conv_bwd_rows(_direct(dqbuf), pbuf[pad + r0:pad + r0 + CHUNK, :], w_ref, dwacc, ktaps, r0, CHUNK)
            o_ref[rows, d:2 * d] = (dp * v_ref[rows, :]).astype(BF16)
            o_ref[rows, 2 * d:3 * d] = (dp * gc_ref[rows, :]).astype(BF16)

        @pl.when(s == nsteps - 1)
        def _():
            _flush_dw(dwacc, dw_ref, ktaps)

    def cur(part):
        return pl.BlockSpec((ts, d), lambda s: (s, part))

    return pl.pallas_call(
        body, name=name, grid=(nsteps,),
        in_specs=[cur(0), cur(1), cur(2),
                  _prev_spec(ts, pad, d, lambda g: 1), _prev_spec(ts, pad, d, lambda g: 2),
                  _next_spec(ts, pad, d, lambda g: 0, s_len),
                  cur(0), _next_spec(ts, pad, d, lambda g: 0, s_len),
                  _full((ktaps, d)), _full((1, d))],
        out_specs=[pl.BlockSpec((ts, d3), lambda s: (s, 0)), _full((ktaps, d)), _full((1, d))],
        out_shape=[jax.ShapeDtypeStruct((s_len, d3), BF16), jax.ShapeDtypeStruct((ktaps, d), F32),
                   jax.ShapeDtypeStruct((1, d), F32)],
        scratch_shapes=[pltpu.VMEM((pad + ts, d), F32), pltpu.VMEM((ts + pad, d), F32),
                        pltpu.VMEM((ktaps * SUBLANES, d), F32)],
        compiler_params=_cp("arbitrary"),
    )(bcv, bcv, bcv, bcv, bcv, bcv, dy0, dy0, cw, cb)


def _ffn_mid_fwd(u0, vg, cw, cb, name):
    s_len, f = u0.shape
    ktaps = cw.shape[0]
    pad = SHORT_PAD
    ts = _tile(s_len, 256)
    cbk = 1024 if f % 1024 == 0 else f

    def body(u_ref, up_ref, vg_ref, w_ref, b_ref, o_ref, ubuf):
        u_ref, up_ref, vg_ref = map(_F32Loads, (u_ref, up_ref, vg_ref))
        s = pl.program_id(1)
        ubuf[0:pad, :] = jnp.where(s > 0, up_ref[...], 0.0)
        ubuf[pad:pad + ts, :] = u_ref[...]
        for r0 in range(0, ts, CHUNK):
            u = _conv_fwd_rows(_direct(ubuf), w_ref, b_ref, ktaps, pad, r0, CHUNK)
            o_ref[r0:r0 + CHUNK, :] = (u * _sigmoid(u) * vg_ref[r0:r0 + CHUNK, :]).astype(BF16)

    cur = pl.BlockSpec((ts, cbk), lambda c, s: (s, c))
    return pl.pallas_call(
        body, name=name, grid=(f // cbk, s_len // ts),
        in_specs=[cur, _prev_spec(ts, pad, cbk, lambda g: g[0]), cur,
                  pl.BlockSpec((ktaps, cbk), lambda c, s: (0, c)), pl.BlockSpec((1, cbk), lambda c, s: (0, c))],
        out_specs=cur,
        out_shape=jax.ShapeDtypeStruct((s_len, f), BF16),
        scratch_shapes=[pltpu.VMEM((pad + ts, cbk), F32)],
        compiler_params=_cp("parallel", "parallel"),
    )(u0, u0, vg, cw, cb)


def _ffn_mid_bwd(u0, vg, dt, cw, cb, name):
    s_len, f = u0.shape
    ktaps = cw.shape[0]
    pad = SHORT_PAD
    ts = _tile(s_len, 256)
    nsteps = s_len // ts
    cbk = 1024 if f % 1024 == 0 else f

    def body(u_ref, up_ref, un_ref, vg_ref, vgn_ref, dt_ref, dtn_ref, w_ref, b_ref,
             du0_ref, dvg_ref, dw_ref, db_ref, ubuf, dubuf, dwacc):
        u_ref, up_ref, un_ref, vg_ref, vgn_ref, dt_ref, dtn_ref = map(
            _F32Loads, (u_ref, up_ref, un_ref, vg_ref, vgn_ref, dt_ref, dtn_ref))
        s = pl.program_id(1)

        @pl.when(s == 0)
        def _():
            dwacc[...] = jnp.zeros_like(dwacc)
            db_ref[...] = jnp.zeros_like(db_ref)

        ubuf[0:pad, :] = jnp.where(s > 0, up_ref[...], 0.0)
        ubuf[pad:pad + ts, :] = u_ref[...]
        ubuf[pad + ts:pad + ts + pad, :] = un_ref[...]
        last = s == nsteps - 1
        for r0 in range(0, ts + pad, CHUNK):
            u = _conv_fwd_rows(_direct(ubuf), w_ref, b_ref, ktaps, pad, r0, CHUNK)
            sg = _sigmoid(u)
            if r0 < ts:
                rows = slice(r0, r0 + CHUNK)
                dtr, vgr = dt_ref[rows, :], vg_ref[rows, :]
                dvg_ref[rows, :] = (dtr * u * sg).astype(BF16)
            else:
                rows = slice(r0 - ts, r0 - ts + CHUNK)
                dtr, vgr = jnp.where(last, 0.0, dtn_ref[rows, :]), vgn_ref[rows, :]
            dubuf[r0:r0 + CHUNK, :] = dtr * vgr * (sg * (1.0 + u * (1.0 - sg)))
        db_ref[...] += jnp.sum(dubuf[0:ts, :], axis=0, keepdims=True)
        for r0 in range(0, ts, CHUNK):
            du0 = _conv_bwd_rows(_direct(dubuf), u_ref[r0:r0 + CHUNK, :], w_ref, dwacc, ktaps, r0, CHUNK)
            du0_ref[r0:r0 + CHUNK, :] = du0.astype(BF16)

        @pl.when(last)
        def _():
            _flush_dw(dwacc, dw_ref, ktaps)

    cur = pl.BlockSpec((ts, cbk), lambda c, s: (s, c))
    prv = _prev_spec(ts, pad, cbk, lambda g: g[0])
    nxt = _next_spec(ts, pad, cbk, lambda g: g[0], s_len)
    return pl.pallas_call(
        body, name=name, grid=(f // cbk, nsteps),
        in_specs=[cur, prv, nxt, cur, nxt, cur, nxt,
                  pl.BlockSpec((ktaps, cbk), lambda c, s: (0, c)), pl.BlockSpec((1, cbk), lambda c, s: (0, c))],
        out_specs=[cur, cur, pl.BlockSpec((ktaps, cbk), lambda c, s: (0, c)),
                   pl.BlockSpec((1, cbk), lambda c, s: (0, c))],
        out_shape=[jax.ShapeDtypeStruct((s_len, f), BF16), jax.ShapeDtypeStruct((s_len, f), BF16),
                   jax.ShapeDtypeStruct((ktaps, f), F32), jax.ShapeDtypeStruct((1, f), F32)],
        scratch_shapes=[pltpu.VMEM((pad + ts + pad, cbk), F32), pltpu.VMEM((ts + pad, cbk), F32),
                        pltpu.VMEM((ktaps * SUBLANES, cbk), F32)],
        compiler_params=_cp("parallel", "arbitrary"),
    )(u0, u0, u0, vg, vg, dt, dt, cw, cb)


def _b_mid_fwd(ub, cw, cb, lng, lnb, name):
    s_len, d2 = ub.shape
    d = d2 // 2
    ktaps = cw.shape[0]
    pad = LONG_PAD
    ts = _tile(s_len, 256)

    def body(a_ref, g_ref, ap_ref, gp_ref, w_ref, b_ref, lng_ref, lnb_ref, a2_ref, a4_ref, abuf):
        a_ref, g_ref, ap_ref, gp_ref = map(_F32Loads, (a_ref, g_ref, ap_ref, gp_ref))
        s = pl.program_id(0)
        abuf[0, 0:pad, :] = jnp.where(s > 0, ap_ref[...] * _sigmoid(gp_ref[...]), 0.0)
        abuf[0, pad:pad + ts, :] = a_ref[...] * _sigmoid(g_ref[...])
        _make_shifts(abuf, pad + ts)
        for r0 in range(0, ts, CHUNK):
            a2_ref[r0:r0 + CHUNK, :] = _conv_fwd_rows(_shifted(abuf), w_ref, b_ref, ktaps, pad, r0, CHUNK)
        a2 = a2_ref[...]
        mu = jnp.mean(a2, axis=-1, keepdims=True)
        ac = a2 - mu
        var = jnp.mean(ac * ac, axis=-1, keepdims=True)
        a3 = ac * lax.rsqrt(var + LN_EPS) * lng_ref[...] + lnb_ref[...]
        a4_ref[...] = (a3 * _sigmoid(a3)).astype(BF16)

    def cur(part):
        return pl.BlockSpec((ts, d), lambda s: (s, part))

    vec = _full((1, d))
    return pl.pallas_call(
        body, name=name, grid=(s_len // ts,),
        in_specs=[cur(0), cur(1), _prev_spec(ts, pad, d, lambda g: 0), _prev_spec(ts, pad, d, lambda g: 1),
                  _full((ktaps, d)), vec, vec, vec],
        out_specs=[cur(0), cur(0)],
        out_shape=[jax.ShapeDtypeStruct((s_len, d), F32), jax.ShapeDtypeStruct((s_len, d), BF16)],
        scratch_shapes=[pltpu.VMEM((SUBLANES, pad + ts, d), F32)],
        compiler_params=_cp("parallel"),
    )(ub, ub, ub, ub, cw, cb, lng, lnb)


def _b_mid_bwd(ub, a2, da4, cw, lng, lnb, name):
    s_len, d2 = ub.shape
    d = d2 // 2
    ktaps = cw.shape[0]
    pad = LONG_PAD
    ts = _tile(s_len, 256)
    nsteps = s_len // ts

    def body(a_ref, g_ref, a2_ref, a2n_ref, da4_ref, da4n_ref, w_ref, lng_ref, lnb_ref,
             du_ref, dw_ref, db_ref, dlng_ref, dlnb_ref, dbias_ref, dabuf, dwacc):
        a_ref, g_ref, da4_ref, da4n_ref = map(_F32Loads, (a_ref, g_ref, da4_ref, da4n_ref))
        s = pl.program_id(0)
        last = s == nsteps - 1

        @pl.when(s == 0)
        def _():
            dwacc[...] = jnp.zeros_like(dwacc)
            for ref in (db_ref, dlng_ref, dlnb_ref, dbias_ref):
                ref[...] = jnp.zeros_like(ref)

        def ln_silu_bwd(a2_t, da4_t):
            mu = jnp.mean(a2_t, axis=-1, keepdims=True)
            ac = a2_t - mu
            var = jnp.mean(ac * ac, axis=-1, keepdims=True)
            rstd = lax.rsqrt(var + LN_EPS)
            ah = ac * rstd
            a3 = ah * lng_ref[...] + lnb_ref[...]
            sg = _sigmoid(a3)
            da3 = da4_t * (sg * (1.0 + a3 * (1.0 - sg)))
            dah = da3 * lng_ref[...]
            m1 = jnp.mean(dah, axis=-1, keepdims=True)
            m2 = jnp.mean(dah * ah, axis=-1, keepdims=True)
            return rstd * (dah - m1 - ah * m2), da3, ah

        da2, da3, ah = ln_silu_bwd(a2_ref[...], da4_ref[...])
        dabuf[0, 0:ts, :] = da2
        dlng_ref[...] += jnp.sum(da3 * ah, axis=0, keepdims=True)
        dlnb_ref[...] += jnp.sum(da3, axis=0, keepdims=True)
        db_ref[...] += jnp.sum(da2, axis=0, keepdims=True)
        da2n, _, _ = ln_silu_bwd(a2n_ref[...], jnp.where(last, 0.0, da4n_ref[...]))
        dabuf[0, ts:ts + pad, :] = da2n
        _make_shifts(dabuf, ts + pad)
        for r0 in range(0, ts, CHUNK):
            rows = slice(r0, r0 + CHUNK)
            a_r, g_r = a_ref[rows, :], g_ref[rows, :]
            sg = _sigmoid(g_r)
            da1 = _conv_bwd_rows(_shifted(dabuf), a_r * sg, w_ref, dwacc, ktaps, r0, CHUNK)
            da = da1 * sg
            dg = da1 * a_r * sg * (1.0 - sg)
            du_ref[rows, 0:d] = da.astype(BF16)
            du_ref[rows, d:2 * d] = dg.astype(BF16)
            dbias_ref[:, 0:d] += jnp.sum(da, axis=0, keepdims=True)
            dbias_ref[:, d:2 * d] += jnp.sum(dg, axis=0, keepdims=True)

        @pl.when(last)
        def _():
            _flush_dw(dwacc, dw_ref, ktaps)

    def cur(part):
        return pl.BlockSpec((ts, d), lambda s: (s, part))

    vec = _full((1, d))
    nxt = _next_spec(ts, pad, d, lambda g: 0, s_len)
    return pl.pallas_call(
        body, name=name, grid=(nsteps,),
        in_specs=[cur(0), cur(1), cur(0), nxt, cur(0), nxt, _full((ktaps, d)), vec, vec],
        out_specs=[pl.BlockSpec((ts, d2), lambda s: (s, 0)), _full((ktaps, d)), vec, vec, vec, _full((1, d2))],
        out_shape=[jax.ShapeDtypeStruct((s_len, d2), BF16), jax.ShapeDtypeStruct((ktaps, d), F32),
                   jax.ShapeDtypeStruct((1, d), F32), jax.ShapeDtypeStruct((1, d), F32),
                   jax.ShapeDtypeStruct((1, d), F32), jax.ShapeDtypeStruct((1, d2), F32)],
        scratch_shapes=[pltpu.VMEM((SUBLANES, ts + pad, d), F32), pltpu.VMEM((ktaps * SUBLANES, d), F32)],
        compiler_params=_cp("arbitrary"),
    )(ub, ub, a2, a2, da4, da4, cw, lng, lnb)


def _loss_head(xo, tgt, name):
    s_len, d = xo.shape
    tm = _tile(s_len, 512)

    def body(x_ref, t_ref, d_ref, l_ref):
        @pl.when(pl.program_id(0) == 0)
        def _():
            l_ref[...] = jnp.zeros_like(l_ref)

        e = x_ref[...] - t_ref[...]
        d_ref[...] = e * (1.0 / d)
        per_row = jnp.sum(e * e, axis=-1, keepdims=True) * (1.0 / d)
        l_ref[...] += 0.5 * jnp.sum(per_row, axis=0, keepdims=True)

    row = pl.BlockSpec((tm, d), lambda i: (i, 0))
    return pl.pallas_call(
        body, name=name, grid=(s_len // tm,),
        in_specs=[row, row], out_specs=[row, _full((1, LANES))],
        out_shape=[jax.ShapeDtypeStruct((s_len, d), F32), jax.ShapeDtypeStruct((1, LANES), F32)],
        compiler_params=_cp("arbitrary"),
    )(xo, tgt)


def _ada_fwd(c_all, ada_w, ada_b_loc, name):
    depth, d, n = ada_w.shape

    def body(c_ref, w_ref, b_ref, o_ref):
        c = c_ref[...]
        act = c * _sigmoid(c)
        o_ref[...] = jnp.dot(act, w_ref[...], preferred_element_type=F32,
                             precision=lax.Precision.HIGHEST) + b_ref[...]

    return pl.pallas_call(
        body, name=name, grid=(depth,),
        in_specs=[_full((NDEV, d)), pl.BlockSpec((None, d, n), lambda i: (i, 0, 0)),
                  pl.BlockSpec((None, 1, n), lambda i: (i, 0, 0))],
        out_specs=pl.BlockSpec((None, NDEV, n), lambda i: (i, 0, 0)),
        out_shape=jax.ShapeDtypeStruct((depth, NDEV, n), F32),
        compiler_params=_cp("parallel"),
    )(c_all, ada_w, ada_b_loc.reshape(depth, 1, n))


def _ada_bwd(c_all_t, dmod_cols, name):
    depth, _, n = dmod_cols.shape
    d = c_all_t.shape[0]

    def body(ct_ref, dm_ref, o_ref):
        ct = ct_ref[...]
        act = ct * _sigmoid(ct)
        acc = None
        for b in range(NDEV):
            term = act[:, b:b + 1] * dm_ref[b:b + 1, :]
            acc = term if acc is None else acc + term
        o_ref[...] = acc

    return pl.pallas_call(
        body, name=name, grid=(depth,),
        in_specs=[_full((d, NDEV)), pl.BlockSpec((None, NDEV, n), lambda i: (i, 0, 0))],
        out_specs=pl.BlockSpec((None, d, n), lambda i: (i, 0, 0)),
        out_shape=jax.ShapeDtypeStruct((depth, d, n), F32),
        compiler_params=_cp("parallel"),
    )(c_all_t, dmod_cols)


def _sum_parts(parts, name):
    _, rows, lanes = parts.shape

    def body(p_ref, o_ref):
        acc = p_ref[0]
        for k in range(1, NDEV):
            acc = acc + p_ref[k]
        o_ref[...] = acc

    return pl.pallas_call(
        body, name=name, in_specs=[_full(parts.shape)], out_specs=_full((rows, lanes)), grid=(1,),
        out_shape=jax.ShapeDtypeStruct((rows, lanes), F32), compiler_params=_cp("arbitrary"),
    )(parts)


def _adamw(w, glist, m, v, name):
    nl, rows, cols = w.shape
    tr = _tile(rows, 256, 2 * SUBLANES)

    def body(w_ref, *rest):
        g_refs = rest[:nl]
        m_ref, v_ref, go_ref, d_ref, mo_ref, vo_ref = rest[nl:]
        g = None
        for layer, g_ref in enumerate(g_refs):
            part = g_ref[0].astype(F32)
            for p in range(1, g_ref.shape[0]):
                part = part + g_ref[p].astype(F32)
            g = part if g is None else jnp.where(pl.program_id(0) == layer, part, g)
        m1 = ADAM_B1 * m_ref[...] + (1.0 - ADAM_B1) * g
        v1 = ADAM_B2 * v_ref[...] + (1.0 - ADAM_B2) * (g * g)
        m_hat = m1 / (1.0 - ADAM_B1 ** ADAM_STEP)
        v_hat = v1 / (1.0 - ADAM_B2 ** ADAM_STEP)
        go_ref[...] = g
        mo_ref[...] = m1
        vo_ref[...] = v1
        d_ref[...] = -ADAM_LR * (m_hat / (jnp.sqrt(v_hat) + ADAM_EPS) + ADAM_WD * w_ref[...])

    blk = pl.BlockSpec((None, tr, cols), lambda l, i: (l, i, 0))
    g_specs = [pl.BlockSpec((g.shape[0], tr, cols), lambda l, i: (0, i, 0)) for g in glist]
    return pl.pallas_call(
        body, name=name, grid=(nl, rows // tr),
        in_specs=[blk] + g_specs + [blk, blk],
        out_specs=[blk] * 4, out_shape=[jax.ShapeDtypeStruct((nl, rows, cols), F32)] * 4,
        compiler_params=_cp("parallel", "parallel"),
    )(w, *glist, m, v)


def _pack(pieces):
    flat = jnp.concatenate([p.reshape(-1) for p in pieces])
    unit = SUBLANES * LANES
    padded = -(-flat.shape[0] // unit) * unit
    return jnp.pad(flat, (0, padded - flat.shape[0])).reshape(padded // LANES, LANES)


def _unpack(packed, shapes, lead=()):
    flat = packed.reshape(lead + (-1,))
    out, off = [], 0
    for s in shapes:
        size = 1
        for dim in s:
            size *= dim
        out.append(flat[..., off:off + size].reshape(lead + tuple(s)))
        off += size
    return out


def _pad_last(a, n):
    return jnp.pad(a, [(0, 0)] * (a.ndim - 1) + [(0, n - a.shape[-1])])


def kernel(x, c, ada_w, ada_b, ln_tok_g, ln_tok_b, ln_ch_g, ln_ch_b, a_w_in, a_conv_w, a_conv_b, a_w_out, b_w_pw1, b_b_pw1, b_conv_w, b_conv_b, b_ln_g, b_ln_b, b_w_pw2, b_b_pw2, f_w_up, f_conv_w, f_conv_b, f_w_gate, f_w_down, loss_target, m_ada_w, m_ada_b, m_ln_tok_g, m_ln_tok_b, m_ln_ch_g, m_ln_ch_b, m_a_w_in, m_a_conv_w, m_a_conv_b, m_a_w_out, m_b_w_pw1, m_b_b_pw1, m_b_conv_w, m_b_conv_b, m_b_ln_g, m_b_ln_b, m_b_w_pw2, m_b_b_pw2, m_f_w_up, m_f_conv_w, m_f_conv_b, m_f_w_gate, m_f_w_down, v_ada_w, v_ada_b, v_ln_tok_g, v_ln_tok_b, v_ln_ch_g, v_ln_ch_b, v_a_w_in, v_a_conv_w, v_a_conv_b, v_a_w_out, v_b_w_pw1, v_b_b_pw1, v_b_conv_w, v_b_conv_b, v_b_ln_g, v_b_ln_b, v_b_w_pw2, v_b_b_pw2, v_f_w_up, v_f_conv_w, v_f_conv_b, v_f_w_gate, v_f_w_down):
    weights = dict(ada_w=ada_w, ada_b=ada_b, ln_tok_g=ln_tok_g, ln_tok_b=ln_tok_b, ln_ch_g=ln_ch_g, ln_ch_b=ln_ch_b, a_w_in=a_w_in, a_conv_w=a_conv_w, a_conv_b=a_conv_b, a_w_out=a_w_out, b_w_pw1=b_w_pw1, b_b_pw1=b_b_pw1, b_conv_w=b_conv_w, b_conv_b=b_conv_b, b_ln_g=b_ln_g, b_ln_b=b_ln_b, b_w_pw2=b_w_pw2, b_b_pw2=b_b_pw2, f_w_up=f_w_up, f_conv_w=f_conv_w, f_conv_b=f_conv_b, f_w_gate=f_w_gate, f_w_down=f_w_down)
    mom_m = dict(ada_w=m_ada_w, ada_b=m_ada_b, ln_tok_g=m_ln_tok_g, ln_tok_b=m_ln_tok_b, ln_ch_g=m_ln_ch_g, ln_ch_b=m_ln_ch_b, a_w_in=m_a_w_in, a_conv_w=m_a_conv_w, a_conv_b=m_a_conv_b, a_w_out=m_a_w_out, b_w_pw1=m_b_w_pw1, b_b_pw1=m_b_b_pw1, b_conv_w=m_b_conv_w, b_conv_b=m_b_conv_b, b_ln_g=m_b_ln_g, b_ln_b=m_b_ln_b, b_w_pw2=m_b_w_pw2, b_b_pw2=m_b_b_pw2, f_w_up=m_f_w_up, f_conv_w=m_f_conv_w, f_conv_b=m_f_conv_b, f_w_gate=m_f_w_gate, f_w_down=m_f_w_down)
    mom_v = dict(ada_w=v_ada_w, ada_b=v_ada_b, ln_tok_g=v_ln_tok_g, ln_tok_b=v_ln_tok_b, ln_ch_g=v_ln_ch_g, ln_ch_b=v_ln_ch_b, a_w_in=v_a_w_in, a_conv_w=v_a_conv_w, a_conv_b=v_a_conv_b, a_w_out=v_a_w_out, b_w_pw1=v_b_w_pw1, b_b_pw1=v_b_b_pw1, b_conv_w=v_b_conv_w, b_conv_b=v_b_conv_b, b_ln_g=v_b_ln_g, b_ln_b=v_b_ln_b, b_w_pw2=v_b_w_pw2, b_b_pw2=v_b_b_pw2, f_w_up=v_f_w_up, f_conv_w=v_f_conv_w, f_conv_b=v_f_conv_b, f_w_gate=v_f_w_gate, f_w_down=v_f_w_down)
    names = list(weights)

    depth, d, n_ada = ada_w.shape
    assert depth == 2 and a_w_in.shape[0] == 1 and b_w_pw1.shape[0] == 1
    s_len = x.shape[1]
    f_loc = f_w_up.shape[-1]
    f_pad = -(-f_loc // LANES) * LANES
    f_all = NDEV * f_pad
    d_loc = d // NDEV
    ka, kb, kf = a_conv_w.shape[1], b_conv_w.shape[1], f_conv_w.shape[1]
    alpha = (2.0 * depth) ** 0.25
    assert a_w_in.shape[-1] == f_pad and f_pad % d_loc == 0
    me = 4 * lax.axis_index("x") + 2 * lax.axis_index("y") + lax.axis_index("c")

    small_shapes = [(d,), (ka, d_loc), (2 * d_loc,), (kb, d_loc), (d_loc,), (d_loc,), (d_loc,), (d_loc,),
                    (depth, kf, f_pad)]
    small_loc = _pack([c[0], a_conv_w[0], b_b_pw1[0], b_conv_w[0], b_conv_b[0], b_ln_g[0], b_ln_b[0],
                       b_b_pw2[0], _pad_last(f_conv_w, f_pad)])
    g_small, g_in, _ = _exchange([small_loc, a_w_in.astype(BF16)], True, "gather_first")

    (c_all, acw_g, bb1_g, bcw_g, bcb_g, blg_g, blb_g, bb2_g, fcw_g) = _unpack(g_small, small_shapes, (NDEV,))
    a_cw = acw_g.transpose(1, 0, 2).reshape(ka, d)
    b_cw = bcw_g.transpose(1, 0, 2).reshape(kb, d)
    b_b1 = bb1_g.reshape(1, 2 * d)
    b_cb, b_lg, b_lb, b_b2 = (t.reshape(1, d) for t in (bcb_g, blg_g, blb_g, bb2_g))
    f_cw = fcw_g.transpose(1, 2, 0, 3).reshape(depth, kf, f_all)
    f_cb = _pad_last(f_conv_b.reshape(depth, NDEV, f_loc), f_pad).reshape(depth, 1, f_all)

    ada_b_loc = lax.dynamic_slice(ada_b, (0, me * n_ada), (depth, n_ada))
    mod_part = _ada_fwd(c_all, ada_w, ada_b_loc, "ada_fwd")
    mod_g, mod_done = _exchange([mod_part.reshape(depth * NDEV, n_ada)], True, "gather_mod")
    mod_all = mod_g.reshape(NDEV, depth, NDEV, n_ada).transpose(1, 2, 0, 3).reshape(depth, NDEV, 6 * d)
    mod = lax.dynamic_slice(mod_all, (0, me, 0), (depth, 1, 6 * d))[:, 0]

    gather_out = _exchange_start([_after(a_w_out[0], mod_done).astype(BF16)], True, "gather_out_start")
    up_pad = _pad_last(_after(f_w_up, gather_out[-1]), f_pad).astype(BF16)
    gate_pad = _pad_last(f_w_gate, f_pad).astype(BF16)
    down_pad = jnp.pad(f_w_down, ((0, 0), (0, f_pad - f_loc), (0, 0))).astype(BF16)
    col_f = [jnp.stack([up_pad[i], gate_pad[i]]) for i in range(depth)]
    row_b = jnp.concatenate([down_pad[1], b_w_pw2[0].astype(BF16)], axis=0)
    ridx_pw2 = f_pad // d_loc
    gather_f0 = _exchange_start([col_f[0], down_pad[0]], True, "gather_f0_start")

    def mod_rows(i):
        return [mod[i:i + 1, j * d:(j + 1) * d] for j in range(6)]

    zeros_d = jnp.zeros((1, d), F32)
    zeros_f = jnp.zeros((1, f_all), F32)
    x0 = x[0]

    sh_t0, sc_t0, g_t0, sh_c0, sc_c0, g_c0 = mod_rows(0)
    sh_t1, sc_t1, g_t1, sh_c1, sc_c1, g_c1 = mod_rows(1)

    sc_t0 = _after(sc_t0, gather_f0[-1])
    bcv, = _mm_fwd(x0, sc_t0, sh_t0, jnp.zeros((1, 3 * d), F32), g_in, (0,), "a_in_fwd")
    y0 = _gateconv_fwd(bcv, a_cw, a_conv_b, "a_conv_fwd")
    g_out, _ = _exchange_wait(gather_out, y0, True, "gather_out_wait")
    y_a, x1, xh1, rs1 = _mm_ln(y0, g_out, d_loc, 0, x0, g_t0, ln_tok_g[0:1], ln_tok_b[0:1], zeros_d,
                               alpha, "a_out_ln_fwd")

    def ffn_fwd(xin, sc, sh, gate, gam, bet, g_colf, g_rowf, layer, tag):
        u0, vg = _mm_fwd(xin, sc, sh, zeros_f, g_colf, (0, 1), "f_upgate_fwd" + tag)
        t = _ffn_mid_fwd(u0, vg, f_cw[layer], f_cb[layer], "f_mid_fwd" + tag)
        y, xo, xh, rs = _mm_ln(t, g_rowf, f_pad, 0, xin, gate, gam, bet, zeros_d, alpha, "f_down_ln_fwd" + tag)
        return u0, vg, t, y, xo, xh, rs

    g_colf0, g_rowf0, landed = _exchange_wait(gather_f0, x1, True, "gather_f0_wait")
    gather_1 = _exchange_start([_after(b_w_pw1, landed).astype(BF16), col_f[1], row_b], True, "gather_1_start")
    sc_c0 = _after(sc_c0, gather_1[-1])
    u0_0, vg_0, t_0, y_f0, x2, xh2, rs2 = ffn_fwd(x1, sc_c0, sh_c0, g_c0, ln_ch_g[0:1], ln_ch_b[0:1],
                                                  g_colf0, g_rowf0, 0, "0")

    g_pw1, g_colf1, g_rowb, _ = _exchange_wait(gather_1, x2, True, "gather_1_wait")
    ub, = _mm_fwd(x2, sc_t1, sh_t1, b_b1, g_pw1, (0,), "b_pw1_fwd")
    a2, a4 = _b_mid_fwd(ub, b_cw, b_cb, b_lg, b_lb, "b_mid_fwd")
    y_b, x3, xh3, rs3 = _mm_ln(a4, g_rowb, d_loc, ridx_pw2, x2, g_t1, ln_tok_g[1:2], ln_tok_b[1:2], b_b2,
                               alpha, "b_pw2_ln_fwd")
    u0_1, vg_1, t_1, y_f1, x4, xh4, rs4 = ffn_fwd(x3, sc_c1, sh_c1, g_c1, ln_ch_g[1:2], ln_ch_b[1:2],
                                                  g_colf1, g_rowb, 1, "1")

    dx4, loss_part = _loss_head(x4, loss_target[0], "loss_head")
    loss = lax.psum(loss_part[0, 0], MESH_AXES)

    def ffn_bwd(dxo, xin, sc, sh, gate, gam, u0, vg, t, y, xh, rs, g_colf, g_rowf, layer, tag):
        dy, dres, acc = _ln_bwd(dxo, xh, rs, gam, y, gate, alpha, "f_ln_bwd" + tag)
        dt = _mm_nt_row(dy, g_rowf, f_pad, 0, "f_down_dx" + tag)
        dw_down = _mm_tn_row(t, dy, f_pad, f_loc, "f_down_dw" + tag)
        du0, dvg, dcw, dcb = _ffn_mid_bwd(u0, vg, dt, f_cw[layer], f_cb[layer], "f_mid_bwd" + tag)
        dw_up = _mm_tn_col_t(xin, sc, sh, du0, f_loc, "f_up_dw" + tag)
        dw_gate = _mm_tn_col_t(xin, sc, sh, dvg, f_loc, "f_gate_dw" + tag)
        scatter = _exchange_start([dw_up, dw_gate, dw_down], False, "scatter_f%s_start" % tag)
        dxin, acc2 = _mm_nt_mod([du0, dvg], g_colf, (0, 1), xin, _after(sc, scatter[-1]), dres, "f_upgate_dx" + tag)
        return dxin, acc, acc2, scatter, dcw, dcb

    dx3, accf1, acc2f1, scatter_f1, dfcw1, dfcb1 = ffn_bwd(
        dx4, x3, sc_c1, sh_c1, g_c1, ln_ch_g[1:2], u0_1, vg_1, t_1, y_f1, xh4, rs4, g_colf1, g_rowb, 1, "1")

    dy, dres, accb = _ln_bwd(dx3, xh3, rs3, ln_tok_g[1:2], y_b, g_t1, alpha, "b_ln_bwd")
    da4 = _mm_nt_row(dy, g_rowb, d_loc, ridx_pw2, "b_pw2_dx")
    dw_pw2 = _mm_tn_row(a4, dy, d_loc, d_loc, "b_pw2_dw")
    du, dbcw, dbcb, dblg, dblb, dbb1 = _b_mid_bwd(ub, a2, da4, b_cw, b_lg, b_lb, "b_mid_bwd")
    dw_pw1 = _mm_tn_col(x2, sc_t1, sh_t1, du, "b_pw1_dw")
    scatter_b = _exchange_start([dw_pw1, dw_pw2], False, "scatter_b_start")
    dx2, acc2b = _mm_nt_mod([du], g_pw1, (0,), x2, _after(sc_t1, scatter_b[-1]), dres, "b_pw1_dx")

    dx1, accf0, acc2f0, scatter_f0, dfcw0, dfcb0 = ffn_bwd(
        dx2, x1, sc_c0, sh_c0, g_c0, ln_ch_g[0:1], u0_0, vg_0, t_0, y_f0, xh2, rs2, g_colf0, g_rowf0, 0, "0")

    dy, dres, acca = _ln_bwd(dx1, xh1, rs1, ln_tok_g[0:1], y_a, g_t0, alpha, "a_ln_bwd")
    dy0 = _mm_nt_row(dy, g_out, d_loc, 0, "a_out_dx")
    dbcv, dacw, dacb = _gateconv_bwd(bcv, dy0, a_cw, a_conv_b, "a_conv_bwd")
    dx0, acc2a = _mm_nt_mod([dbcv], g_in, (0,), x0, sc_t0, dres, "a_in_dx")

    def dmod_row(acc2_t, acc_t, acc2_c, acc_c):
        return jnp.concatenate([acc2_t[1], acc2_t[0], acc_t[2], acc2_c[1], acc2_c[0], acc_c[2]])

    dmod = jnp.stack([dmod_row(acc2a, acca, acc2f0, accf0), dmod_row(acc2b, accb, acc2f1, accf1)])

    def unpad_f(a):
        return a.reshape(a.shape[:-1] + (NDEV, f_pad))[..., :f_loc].reshape(a.shape[:-1] + (NDEV * f_loc,))

    small_grads = [
        dmod,
        jnp.stack([acca[0], accb[0]]), jnp.stack([acca[1], accb[1]]),
        jnp.stack([accf0[0], accf1[0]]), jnp.stack([accf0[1], accf1[1]]),
        dacb,
        unpad_f(jnp.concatenate([dfcb0, dfcb1], axis=0)),
        dacw, dbb1, dbcw, dbcb, dblg, dblb, accb[3:4],
        jnp.stack([dfcw0, dfcw1]),
    ]
    small_grad_shapes = [tuple(g.shape) for g in small_grads]
    gather_small = _exchange_start([_pack(small_grads)], True, "gather_small_start")

    dw_in = _mm_tn_col(x0, _after(sc_t0, gather_small[-1]), sh_t0, dbcv, "a_in_dw")
    dw_out = _mm_tn_row(y0, dy, d_loc, d_loc, "a_out_dw")
    scatter_a = _exchange_start([dw_in, dw_out], False, "scatter_a_start")

    grads, deltas, new_m, new_v = {}, {}, {}, {}

    def adamw(k, glist, transposed=False):
        def view(a):
            a = jnp.swapaxes(a, 1, 2) if transposed else a
            return a.reshape(len(glist), -1, a.shape[-1])

        w = view(weights[k])
        outs = _adamw(w, [g.reshape(g.shape[0], -1, w.shape[-1]) for g in glist],
                      view(mom_m[k]), view(mom_v[k]), "adamw_" + k)
        if transposed:
            outs = [jnp.swapaxes(o, 1, 2) for o in outs]
        grads[k], deltas[k], new_m[k], new_v[k] = (o.reshape(weights[k].shape) for o in outs)

    r_up1, r_gate1, r_down1, _ = _exchange_wait(scatter_f1, scatter_a[-1], False, "scatter_f1_wait")
    r_pw1, r_pw2, _ = _exchange_wait(scatter_b, r_down1, False, "scatter_b_wait")
    adamw("b_w_pw1", [r_pw1])
    adamw("b_w_pw2", [r_pw2])
    r_up0, r_gate0, r_down0, _ = _exchange_wait(scatter_f0, deltas["b_w_pw2"], False, "scatter_f0_wait")
    adamw("f_w_up", [r_up0, r_up1], transposed=True)
    adamw("f_w_gate", [r_gate0, r_gate1], transposed=True)
    adamw("f_w_down", [r_down0, r_down1])

    sg_all, _ = _exchange_wait(gather_small, deltas["f_w_down"], True, "gather_small_wait")
    sg_sum = _sum_parts(sg_all, "sum_small_grads")
    (g_ada_b, g_ltg, g_ltb, g_lcg, g_lcb, g_acb, g_fcb, g_acw, g_bb1, g_bcw, g_bcb, g_blg, g_blb, g_bb2,
     g_fcw) = _unpack(sg_sum, small_grad_shapes)

    def my_cols(a, width):
        return lax.dynamic_slice_in_dim(a, me * width, width, axis=a.ndim - 1)

    g_fcw_loc = my_cols(g_fcw, f_pad)[..., :f_loc]
    small = dict(
        ada_b=g_ada_b, ln_tok_g=g_ltg, ln_tok_b=g_ltb, ln_ch_g=g_lcg, ln_ch_b=g_lcb, a_conv_b=g_acb, f_conv_b=g_fcb,
        a_conv_w=my_cols(g_acw, d_loc)[None], b_b_pw1=my_cols(g_bb1, 2 * d_loc), b_conv_w=my_cols(g_bcw, d_loc)[None],
        b_conv_b=my_cols(g_bcb, d_loc), b_ln_g=my_cols(g_blg, d_loc), b_ln_b=my_cols(g_blb, d_loc),
        b_b_pw2=my_cols(g_bb2, d_loc), f_conv_w=g_fcw_loc)

    dmod_all = sg_all.reshape(NDEV, -1)[:, :depth * 6 * d].reshape(NDEV, depth, 6 * d)
    dmod_cols = my_cols(dmod_all, n_ada).transpose(1, 0, 2)
    g_ada_w = _ada_bwd(c_all.T, dmod_cols, "ada_bwd")

    adamw("ada_w", [g_ada_w[0:1], g_ada_w[1:2]])
    for k, g in small.items():
        adamw(k, [g[None]])

    r_in, r_out, _ = _exchange_wait(scatter_a, deltas["ada_w"], False, "scatter_a_wait")
    adamw("a_w_in", [r_in])
    adamw("a_w_out", [r_out])

    return (loss, dx0[None], *[grads[k] for k in names], *[deltas[k] for k in names],
            *[new_m[k] for k in names], *[new_v[k] for k in names])
```

```python
import functools

import jax
import jax.numpy as jnp
from jax import lax
from jax.experimental import pallas as pl
from jax.experimental.pallas import tpu as pltpu

NDEV = 8
MESH_AXES = ("x", "y", "c")
LANES = 128
SUBLANES = 8
VMEM_LIMIT = 56 * 1024 * 1024
LN_EPS = 1e-5
SHORT_PAD = 16
LONG_PAD = 32
CHUNK = 16
ADAM_LR, ADAM_B1, ADAM_B2, ADAM_EPS, ADAM_WD, ADAM_STEP = 0.001, 0.9, 0.999, 1e-08, 0.01, 10

F32 = jnp.float32
BF16 = jnp.bfloat16
MESH = pl.DeviceIdType.MESH
NT = (((1,), (1,)), ((), ()))
TN = (((0,), (0,)), ((), ()))


def _tile(n, target, mult=SUBLANES):
    best = None
    for t in range(mult, min(n, target) + 1, mult):
        if n % t == 0:
            best = t
    return best if best is not None else n


def _full(shape):
    nd = len(shape)
    return pl.BlockSpec(shape, lambda *_: (0,) * nd)


def _cp(*sem):
    return pltpu.CompilerParams(dimension_semantics=sem, vmem_limit_bytes=VMEM_LIMIT)


def _sigmoid(x):
    return 1.0 / (1.0 + jnp.exp(-x))


def _peer(x, y, c, d):
    return ((1 - x) if d & 4 else x, (1 - y) if d & 2 else y, (1 - c) if d & 1 else c)


def _lin(p):
    return 4 * p[0] + 2 * p[1] + p[2]


def _remote_copies(src_refs, land_refs, send_sems, recv_sems, gather):
    x, y, c = (lax.axis_index(a) for a in MESH_AXES)
    me = _lin((x, y, c))
    sends, recvs = [], []
    for i, (src_ref, land_ref) in enumerate(zip(src_refs, land_refs)):
        for d in range(1, NDEV):
            peer = _peer(x, y, c, d)
            k = i * (NDEV - 1) + d - 1
            src = src_ref if gather else src_ref.at[_lin(peer)]
            for slot, out in ((me, sends), (_lin(peer), recvs)):
                out.append(pltpu.make_async_remote_copy(
                    src_ref=src, dst_ref=land_ref.at[slot], send_sem=send_sems.at[k], recv_sem=recv_sems.at[k],
                    device_id=peer, device_id_type=MESH))
    return sends, recvs


def _exchange(srcs, gather, name):
    n = len(srcs)

    def body(*refs):
        src_refs, out_refs, token = refs[:n], refs[n:2 * n], refs[2 * n]
        send_sems, recv_sems, local_sems = refs[2 * n + 1:]
        me = _lin(tuple(lax.axis_index(a) for a in MESH_AXES))
        local = []
        for i in range(n):
            mine = src_refs[i] if gather else src_refs[i].at[me]
            cp = pltpu.make_async_copy(mine, out_refs[i].at[me], local_sems.at[i])
            cp.start()
            local.append(cp)
        sends, recvs = _remote_copies(src_refs, out_refs, send_sems, recv_sems, gather)
        for snd in sends:
            snd.start()
        token[...] = jnp.zeros_like(token)
        for snd, rcv in zip(sends, recvs):
            snd.wait_send()
            rcv.wait_recv()
        for cp in local:
            cp.wait()

    out_shape = [jax.ShapeDtypeStruct(((NDEV,) + s.shape) if gather else s.shape, s.dtype) for s in srcs]
    out_shape.append(jax.ShapeDtypeStruct((SUBLANES, LANES), F32))
    any_spec = pl.BlockSpec(memory_space=pl.ANY)
    return pl.pallas_call(
        body, name=name, out_shape=out_shape,
        in_specs=[any_spec] * n, out_specs=[any_spec] * n + [pl.BlockSpec(memory_space=pltpu.VMEM)],
        scratch_shapes=[pltpu.SemaphoreType.DMA((n * (NDEV - 1),)),
                        pltpu.SemaphoreType.DMA((n * (NDEV - 1),)),
                        pltpu.SemaphoreType.DMA((n,))],
    )(*srcs)


HBM_SPEC = pl.BlockSpec(memory_space=pltpu.HBM)
SEM_SPEC = pl.BlockSpec(memory_space=pltpu.SEMAPHORE)
SIDE_EFFECT = pltpu.SideEffectType.DATAFLOW_SIDE_EFFECTING


def _exchange_start(srcs, gather, name):
    n = len(srcs)
    me = _lin(tuple(lax.axis_index(a) for a in MESH_AXES))
    lands = []
    for s in srcs:
        own = s if gather else lax.dynamic_index_in_dim(s, me, 0, keepdims=False)
        shape = ((NDEV,) + s.shape) if gather else s.shape
        lands.append(lax.dynamic_update_index_in_dim(lax.empty(shape, s.dtype), own, me, 0))

    def body(*refs):
        src_refs, land_refs = refs[:n], refs[n:2 * n]
        send_sems, recv_sems, token = refs[2 * n], refs[2 * n + 1], refs[-1]
        sends, _ = _remote_copies(src_refs, land_refs, send_sems, recv_sems, gather)
        for snd in sends:
            snd.start()
        token[...] = jnp.zeros_like(token)

    operands = [pltpu.with_memory_space_constraint(a, pltpu.HBM) for a in list(srcs) + lands]
    nsem = n * (NDEV - 1)
    return pl.pallas_call(
        body, name=name,
        out_shape=(pltpu.SemaphoreType.DMA((nsem,)), pltpu.SemaphoreType.DMA((nsem,)),
                   *[pltpu.HBM(a.shape, a.dtype) for a in operands],
                   jax.ShapeDtypeStruct((SUBLANES, LANES), F32)),
        in_specs=[HBM_SPEC] * (2 * n),
        out_specs=(SEM_SPEC, SEM_SPEC, *([HBM_SPEC] * (2 * n)), pl.BlockSpec(memory_space=pltpu.VMEM)),
        input_output_aliases={i: 2 + i for i in range(2 * n)},
        compiler_params=pltpu.CompilerParams(has_side_effects=SIDE_EFFECT),
    )(*operands)


def _exchange_wait(handle, after, gather, name):
    send_sems, recv_sems, *thru = handle[:-1]
    n = len(thru) // 2

    def body(*refs):
        src_refs, land_refs = refs[:n], refs[n:2 * n]
        sends, recvs = _remote_copies(src_refs, land_refs, refs[2 * n], refs[2 * n + 1], gather)
        for snd, rcv in zip(sends, recvs):
            snd.wait_send()
            rcv.wait_recv()
        refs[-1][...] = jnp.zeros_like(refs[-1])

    outs = pl.pallas_call(
        body, name=name,
        out_shape=(*[pltpu.HBM(a.shape, a.dtype) for a in thru], jax.ShapeDtypeStruct((SUBLANES, LANES), F32)),
        in_specs=[HBM_SPEC] * (2 * n) + [SEM_SPEC, SEM_SPEC, pl.BlockSpec(memory_space=pl.ANY)],
        out_specs=[HBM_SPEC] * (2 * n) + [pl.BlockSpec(memory_space=pltpu.VMEM)],
        input_output_aliases={i: i for i in range(2 * n)},
        compiler_params=pltpu.CompilerParams(has_side_effects=SIDE_EFFECT),
    )(*thru, send_sems, recv_sems, after)
    return outs[n:]


def _after(value, token):
    return value + token[0, 0]


ANY_SPEC = pl.BlockSpec(memory_space=pl.ANY)


def _load_cols(wg_ref, widx, w_ref, sems):
    n = wg_ref.shape[-1]
    copies = [pltpu.make_async_copy(wg_ref.at[k, widx], w_ref.at[:, pl.ds(k * n, n)], sems.at[k])
              for k in range(NDEV)]
    for cp in copies:
        cp.start()
    for cp in copies:
        cp.wait()


def _load_rows(wg_ref, r, ridx, w_ref, sems):
    copies = [pltpu.make_async_copy(wg_ref.at[k, pl.ds(ridx * r, r)], w_ref.at[pl.ds(k * r, r)], sems.at[k])
              for k in range(NDEV)]
    for cp in copies:
        cp.start()
    for cp in copies:
        cp.wait()


def _mm_fwd(x, sc, sh, bias, wg, widxs, name):
    s_len, kdim = x.shape
    ncol = NDEV * wg.shape[-1]
    tm = _tile(s_len, 512)
    nw = len(widxs)

    def body(x_ref, sc_ref, sh_ref, b_ref, wg_ref, *rest):
        o_refs, w_refs, sems = rest[:nw], rest[nw:2 * nw], rest[2 * nw]

        @pl.when(pl.program_id(0) == 0)
        def _():
            for i, w_ref in enumerate(w_refs):
                _load_cols(wg_ref, widxs[i], w_ref, sems.at[i])

        h = (x_ref[...] * (1.0 + sc_ref[...]) + sh_ref[...]).astype(BF16)
        for w_ref, o_ref in zip(w_refs, o_refs):
            o_ref[...] = (jnp.dot(h, w_ref[...], preferred_element_type=F32) + b_ref[...]).astype(BF16)

    return pl.pallas_call(
        body, name=name, grid=(s_len // tm,),
        in_specs=[pl.BlockSpec((tm, kdim), lambda i: (i, 0)), _full((1, kdim)), _full((1, kdim)),
                  _full((1, ncol)), ANY_SPEC],
        out_specs=[pl.BlockSpec((tm, ncol), lambda i: (i, 0))] * nw,
        out_shape=[jax.ShapeDtypeStruct((s_len, ncol), BF16)] * nw,
        scratch_shapes=[pltpu.VMEM((kdim, ncol), BF16)] * nw + [pltpu.SemaphoreType.DMA((nw, NDEV))],
        compiler_params=_cp("arbitrary"),
    )(x, sc, sh, bias, wg)


def _mm_ln(a, wg, r, ridx, xres, gate, gam, bet, bias, alpha, name):
    s_len = a.shape[0]
    d = wg.shape[-1]
    tm = _tile(s_len, 512)

    def body(a_ref, wg_ref, x_ref, g_ref, gam_ref, bet_ref, b_ref, y_ref, xo_ref, xh_ref, rs_ref, w_ref, sems):
        @pl.when(pl.program_id(0) == 0)
        def _():
            _load_rows(wg_ref, r, ridx, w_ref, sems)

        y = jnp.dot(a_ref[...], w_ref[...], preferred_element_type=F32) + b_ref[...]
        z = alpha * x_ref[...] + g_ref[...] * y
        mu = jnp.mean(z, axis=-1, keepdims=True)
        zc = z - mu
        var = jnp.mean(zc * zc, axis=-1, keepdims=True)
        rstd = lax.rsqrt(var + LN_EPS)
        xh = zc * rstd
        y_ref[...] = y.astype(BF16)
        xh_ref[...] = xh
        rs_ref[...] = rstd
        xo_ref[...] = xh * gam_ref[...] + bet_ref[...]

    row = pl.BlockSpec((tm, d), lambda i: (i, 0))
    vec = _full((1, d))
    return pl.pallas_call(
        body, name=name, grid=(s_len // tm,),
        in_specs=[pl.BlockSpec((tm, NDEV * r), lambda i: (i, 0)), ANY_SPEC, row, vec, vec, vec, vec],
        out_specs=[row, row, row, pl.BlockSpec((tm, 1), lambda i: (i, 0))],
        out_shape=[jax.ShapeDtypeStruct((s_len, d), BF16)] + [jax.ShapeDtypeStruct((s_len, d), F32)] * 2
        + [jax.ShapeDtypeStruct((s_len, 1), F32)],
        scratch_shapes=[pltpu.VMEM((NDEV * r, d), BF16), pltpu.SemaphoreType.DMA((NDEV,))],
        compiler_params=_cp("arbitrary"),
    )(a, wg, xres, gate, gam, bet, bias)


def _ln_bwd(dxo, xh, rstd, gam, y, gate, alpha, name):
    s_len, d = dxo.shape
    tm = _tile(s_len, 256)

    def body(d_ref, xh_ref, rs_ref, gam_ref, y_ref, g_ref, dy_ref, dres_ref, acc_ref):
        @pl.when(pl.program_id(0) == 0)
        def _():
            acc_ref[...] = jnp.zeros_like(acc_ref)

        dxo_t = d_ref[...]
        xh_t = xh_ref[...]
        dxh = dxo_t * gam_ref[...]
        m1 = jnp.mean(dxh, axis=-1, keepdims=True)
        m2 = jnp.mean(dxh * xh_t, axis=-1, keepdims=True)
        dz = rs_ref[...] * (dxh - m1 - xh_t * m2)
        dy = g_ref[...] * dz
        dy_ref[...] = dy.astype(BF16)
        dres_ref[...] = alpha * dz
        acc_ref[0:1, :] += jnp.sum(dxo_t * xh_t, axis=0, keepdims=True)
        acc_ref[1:2, :] += jnp.sum(dxo_t, axis=0, keepdims=True)
        acc_ref[2:3, :] += jnp.sum(dz * y_ref[...].astype(F32), axis=0, keepdims=True)
        acc_ref[3:4, :] += jnp.sum(dy, axis=0, keepdims=True)

    row = pl.BlockSpec((tm, d), lambda i: (i, 0))
    vec = _full((1, d))
    return pl.pallas_call(
        body, name=name, grid=(s_len // tm,),
        in_specs=[row, row, pl.BlockSpec((tm, 1), lambda i: (i, 0)), vec, row, vec],
        out_specs=[row, row, _full((SUBLANES, d))],
        out_shape=[jax.ShapeDtypeStruct((s_len, d), BF16), jax.ShapeDtypeStruct((s_len, d), F32),
                   jax.ShapeDtypeStruct((SUBLANES, d), F32)],
        compiler_params=_cp("arbitrary"),
    )(dxo, xh, rstd, gam, y, gate)


def _mm_nt_row(dy, wg, r, ridx, name):
    s_len, d = dy.shape
    tm = _tile(s_len, 512)

    def body(dy_ref, wg_ref, o_ref, w_ref, sems):
        @pl.when(pl.program_id(0) == 0)
        def _():
            _load_rows(wg_ref, r, ridx, w_ref, sems)

        o_ref[...] = lax.dot_general(dy_ref[...], w_ref[...], NT, preferred_element_type=F32).astype(BF16)

    return pl.pallas_call(
        body, name=name, grid=(s_len // tm,),
        in_specs=[pl.BlockSpec((tm, d), lambda i: (i, 0)), ANY_SPEC],
        out_specs=pl.BlockSpec((tm, NDEV * r), lambda i: (i, 0)),
        out_shape=jax.ShapeDtypeStruct((s_len, NDEV * r), BF16),
        scratch_shapes=[pltpu.VMEM((NDEV * r, d), BF16), pltpu.SemaphoreType.DMA((NDEV,))],
        compiler_params=_cp("arbitrary"),
    )(dy, wg)


def _mm_nt_mod(dos, wg, widxs, xin, sc, dres, name):
    s_len, kdim = xin.shape
    ncol = NDEV * wg.shape[-1]
    tm = _tile(s_len, 512)
    nw = len(widxs)

    def body(*refs):
        do_refs, wg_ref = refs[:nw], refs[nw]
        x_ref, sc_ref, dres_ref, dx_ref, acc_ref = refs[nw + 1:nw + 6]
        w_refs, sems = refs[nw + 6:2 * nw + 6], refs[2 * nw + 6]

        @pl.when(pl.program_id(0) == 0)
        def _():
            acc_ref[...] = jnp.zeros_like(acc_ref)
            for i, w_ref in enumerate(w_refs):
                _load_cols(wg_ref, widxs[i], w_ref, sems.at[i])

        dh = None
        for do_ref, w_ref in zip(do_refs, w_refs):
            p = lax.dot_general(do_ref[...], w_ref[...], NT, preferred_element_type=F32)
            dh = p if dh is None else dh + p
        dx_ref[...] = dh * (1.0 + sc_ref[...]) + dres_ref[...]
        acc_ref[0:1, :] += jnp.sum(dh * x_ref[...], axis=0, keepdims=True)
        acc_ref[1:2, :] += jnp.sum(dh, axis=0, keepdims=True)

    row = pl.BlockSpec((tm, kdim), lambda i: (i, 0))
    return pl.pallas_call(
        body, name=name, grid=(s_len // tm,),
        in_specs=[pl.BlockSpec((tm, ncol), lambda i: (i, 0))] * nw + [ANY_SPEC, row, _full((1, kdim)), row],
        out_specs=[row, _full((SUBLANES, kdim))],
        out_shape=[jax.ShapeDtypeStruct((s_len, kdim), F32), jax.ShapeDtypeStruct((SUBLANES, kdim), F32)],
        scratch_shapes=[pltpu.VMEM((kdim, ncol), BF16)] * nw + [pltpu.SemaphoreType.DMA((nw, NDEV))],
        compiler_params=_cp("arbitrary"),
    )(*dos, wg, xin, sc, dres)


def _mm_tn_col(x, sc, sh, do, name):
    s_len, kdim = x.shape
    n = do.shape[1] // NDEV
    ts = _tile(s_len, 512)
    nsteps = s_len // ts

    def body(x_ref, sc_ref, sh_ref, do_ref, o_ref, acc_ref):
        @pl.when(pl.program_id(0) == 0)
        def _():
            acc_ref[...] = jnp.zeros_like(acc_ref)

        h = (x_ref[...] * (1.0 + sc_ref[...]) + sh_ref[...]).astype(BF16)
        acc_ref[...] += lax.dot_general(h, do_ref[...], TN, preferred_element_type=F32)

        @pl.when(pl.program_id(0) == nsteps - 1)
        def _():
            for k in range(NDEV):
                o_ref[k] = acc_ref[:, k * n:(k + 1) * n].astype(BF16)

    return pl.pallas_call(
        body, name=name, grid=(nsteps,),
        in_specs=[pl.BlockSpec((ts, kdim), lambda i: (i, 0)), _full((1, kdim)), _full((1, kdim)),
                  pl.BlockSpec((ts, NDEV * n), lambda i: (i, 0))],
        out_specs=_full((NDEV, kdim, n)),
        out_shape=jax.ShapeDtypeStruct((NDEV, kdim, n), BF16),
        scratch_shapes=[pltpu.VMEM((kdim, NDEV * n), F32)],
        compiler_params=_cp("arbitrary"),
    )(x, sc, sh, do)


def _mm_tn_col_t(x, sc, sh, do, rows_out, name):
    s_len, kdim = x.shape
    n = do.shape[1] // NDEV
    ts = _tile(s_len, 512)
    nsteps = s_len // ts

    def body(x_ref, sc_ref, sh_ref, do_ref, o_ref, acc_ref):
        @pl.when(pl.program_id(0) == 0)
        def _():
            acc_ref[...] = jnp.zeros_like(acc_ref)

        h = (x_ref[...] * (1.0 + sc_ref[...]) + sh_ref[...]).astype(BF16)
        acc_ref[...] += lax.dot_general(do_ref[...], h, TN, preferred_element_type=F32)

        @pl.when(pl.program_id(0) == nsteps - 1)
        def _():
            for k in range(NDEV):
                o_ref[k] = acc_ref[k * n:k * n + rows_out, :].astype(BF16)

    return pl.pallas_call(
        body, name=name, grid=(nsteps,),
        in_specs=[pl.BlockSpec((ts, kdim), lambda i: (i, 0)), _full((1, kdim)), _full((1, kdim)),
                  pl.BlockSpec((ts, NDEV * n), lambda i: (i, 0))],
        out_specs=_full((NDEV, rows_out, kdim)),
        out_shape=jax.ShapeDtypeStruct((NDEV, rows_out, kdim), BF16),
        scratch_shapes=[pltpu.VMEM((NDEV * n, kdim), F32)],
        compiler_params=_cp("arbitrary"),
    )(x, sc, sh, do)


def _mm_tn_row(a, dy, r, rows_out, name):
    s_len, d = dy.shape
    ts = _tile(s_len, 512)
    nsteps = s_len // ts

    def body(a_ref, dy_ref, o_ref, acc_ref):
        @pl.when(pl.program_id(0) == 0)
        def _():
            acc_ref[...] = jnp.zeros_like(acc_ref)

        acc_ref[...] += lax.dot_general(a_ref[...], dy_ref[...], TN, preferred_element_type=F32)

        @pl.when(pl.program_id(0) == nsteps - 1)
        def _():
            for k in range(NDEV):
                o_ref[k] = acc_ref[k * r:k * r + rows_out, :].astype(BF16)

    return pl.pallas_call(
        body, name=name, grid=(nsteps,),
        in_specs=[pl.BlockSpec((ts, NDEV * r), lambda i: (i, 0)), pl.BlockSpec((ts, d), lambda i: (i, 0))],
        out_specs=_full((NDEV, rows_out, d)),
        out_shape=jax.ShapeDtypeStruct((NDEV, rows_out, d), BF16),
        scratch_shapes=[pltpu.VMEM((NDEV * r, d), F32)],
        compiler_params=_cp("arbitrary"),
    )(a, dy)


def _prev_spec(ts, pad, cb, col):
    return pl.BlockSpec((pad, cb), lambda *g: (jnp.maximum(g[-1] * (ts // pad) - 1, 0), col(g)))


def _next_spec(ts, pad, cb, col, s_len):
    return pl.BlockSpec((pad, cb), lambda *g: (jnp.minimum((g[-1] + 1) * (ts // pad), s_len // pad - 1), col(g)))


class _F32Loads:
    def __init__(self, ref):
        self.ref = ref

    def __getitem__(self, idx):
        return self.ref[idx].astype(F32)


def _direct(buf_ref):
    return lambda off, rows: buf_ref[off:off + rows, :]


def _make_shifts(sh_ref, nrows):
    for r in range(1, SUBLANES):
        sh_ref[r, 0:nrows - SUBLANES, :] = sh_ref[0, r:r + nrows - SUBLANES, :]


def _shifted(sh_ref):
    def read(off, rows):
        r = off % SUBLANES
        return sh_ref[r, off - r:off - r + rows, :]
    return read


def _conv_fwd_rows(read, w_ref, b_ref, ktaps, pad, r0, rows):
    acc = None
    for j in range(ktaps):
        term = w_ref[ktaps - 1 - j:ktaps - j, :] * read(pad - j + r0, rows)
        acc = term if acc is None else acc + term
    return acc + b_ref[...]


def _conv_bwd_rows(read, x_rows, w_ref, dwacc_ref, ktaps, r0, rows):
    acc = None
    for j in range(ktaps):
        sl = read(j + r0, rows)
        term = w_ref[ktaps - 1 - j:ktaps - j, :] * sl
        acc = term if acc is None else acc + term
        prod = x_rows * sl
        fold = prod[0:SUBLANES]
        for q in range(1, rows // SUBLANES):
            fold = fold + prod[q * SUBLANES:(q + 1) * SUBLANES]
        tap = ktaps - 1 - j
        dwacc_ref[tap * SUBLANES:(tap + 1) * SUBLANES, :] += fold
    return acc


def _flush_dw(dwacc_ref, dw_ref, ktaps):
    for tap in range(ktaps):
        dw_ref[tap:tap + 1, :] = jnp.sum(dwacc_ref[tap * SUBLANES:(tap + 1) * SUBLANES, :], axis=0, keepdims=True)


def _gateconv_fwd(bcv, cw, cb, name):
    s_len, d3 = bcv.shape
    d = d3 // 3
    ktaps = cw.shape[0]
    pad = SHORT_PAD
    ts = _tile(s_len, 256)

    def body(gb_ref, gc_ref, v_ref, gcp_ref, vp_ref, w_ref, b_ref, o_ref, pbuf):
        gb_ref, gc_ref, v_ref, gcp_ref, vp_ref = map(_F32Loads, (gb_ref, gc_ref, v_ref, gcp_ref, vp_ref))
        s = pl.program_id(0)
        pbuf[0:pad, :] = jnp.where(s > 0, gcp_ref[...] * vp_ref[...], 0.0)
        pbuf[pad:pad + ts, :] = gc_ref[...] * v_ref[...]
        for r0 in range(0, ts, CHUNK):
            q = _conv_fwd_rows(_direct(pbuf), w_ref, b_ref, ktaps, pad, r0, CHUNK)
            o_ref[r0:r0 + CHUNK, :] = (gb_ref[r0:r0 + CHUNK, :] * q).astype(BF16)

    def cur(part):
        return pl.BlockSpec((ts, d), lambda s: (s, part))

    return pl.pallas_call(
        body, name=name, grid=(s_len // ts,),
        in_specs=[cur(0), cur(1), cur(2),
                  _prev_spec(ts, pad, d, lambda g: 1), _prev_spec(ts, pad, d, lambda g: 2),
                  _full((ktaps, d)), _full((1, d))],
        out_specs=pl.BlockSpec((ts, d), lambda s: (s, 0)),
        out_shape=jax.ShapeDtypeStruct((s_len, d), BF16),
        scratch_shapes=[pltpu.VMEM((pad + ts, d), F32)],
        compiler_params=_cp("parallel"),
    )(bcv, bcv, bcv, bcv, bcv, cw, cb)


def _gateconv_bwd(bcv, dy0, cw, cb, name):
    s_len, d3 = bcv.shape
    d = d3 // 3
    ktaps = cw.shape[0]
    pad = SHORT_PAD
    ts = _tile(s_len, 256)
    nsteps = s_len // ts

    def body(gb_ref, gc_ref, v_ref, gcp_ref, vp_ref, gbn_ref, dy_ref, dyn_ref, w_ref, b_ref,
             o_ref, dw_ref, db_ref, pbuf, dqbuf, dwacc):
        gb_ref, gc_ref, v_ref, gcp_ref, vp_ref, gbn_ref, dy_ref, dyn_ref = map(
            _F32Loads, (gb_ref, gc_ref, v_ref, gcp_ref, vp_ref, gbn_ref, dy_ref, dyn_ref))
        s = pl.program_id(0)

        @pl.when(s == 0)
        def _():
            dwacc[...] = jnp.zeros_like(dwacc)
            db_ref[...] = jnp.zeros_like(db_ref)

        pbuf[0:pad, :] = jnp.where(s > 0, gcp_ref[...] * vp_ref[...], 0.0)
        pbuf[pad:pad + ts, :] = gc_ref[...] * v_ref[...]
        dq = dy_ref[...] * gb_ref[...]
        dqbuf[0:ts, :] = dq
        dqbuf[ts:ts + pad, :] = jnp.where(s < nsteps - 1, dyn_ref[...] * gbn_ref[...], 0.0)
        db_ref[...] += jnp.sum(dq, axis=0, keepdims=True)
        for r0 in range(0, ts, CHUNK):
            rows = slice(r0, r0 + CHUNK)
            q = _conv_fwd_rows(_direct(pbuf), w_ref, b_ref, ktaps, pad, r0, CHUNK)
            o_ref[rows, 0:d] = (dy_ref[rows, :] * q).astype(BF16)
            dp = _conv_bwd_rows(_direct(dqbuf), pbuf[pad + r0:pad + r0 + CHUNK, :], w_ref, dwacc, ktaps, r0, CHUNK)
            o_ref[rows, d:2 * d] = (dp * v_ref[rows, :]).astype(BF16)
            o_ref[rows, 2 * d:3 * d] = (dp * gc_ref[rows, :]).astype(BF16)

        @pl.when(s == nsteps - 1)
        def _():
            _flush_dw(dwacc, dw_ref, ktaps)

    def cur(part):
        return pl.BlockSpec((ts, d), lambda s: (s, part))

    return pl.pallas_call(
        body, name=name, grid=(nsteps,),
        in_specs=[cur(0), cur(1), cur(2),
                  _prev_spec(ts, pad, d, lambda g: 1), _prev_spec(ts, pad, d, lambda g: 2),
                  _next_spec(ts, pad, d, lambda g: 0, s_len),
                  cur(0), _next_spec(ts, pad, d, lambda g: 0, s_len),
                  _full((ktaps, d)), _full((1, d))],
        out_specs=[pl.BlockSpec((ts, d3), lambda s: (s, 0)), _full((ktaps, d)), _full((1, d))],
        out_shape=[jax.ShapeDtypeStruct((s_len, d3), BF16), jax.ShapeDtypeStruct((ktaps, d), F32),
                   jax.ShapeDtypeStruct((1, d), F32)],
        scratch_shapes=[pltpu.VMEM((pad + ts, d), F32), pltpu.VMEM((ts + pad, d), F32),
                        pltpu.VMEM((ktaps * SUBLANES, d), F32)],
        compiler_params=_cp("arbitrary"),
    )(bcv, bcv, bcv, bcv, bcv, bcv, dy0, dy0, cw, cb)


def _ffn_mid_fwd(u0, vg, cw, cb, name):
    s_len, f = u0.shape
    ktaps = cw.shape[0]
    pad = SHORT_PAD
    ts = _tile(s_len, 256)
    cbk = 1024 if f % 1024 == 0 else f

    def body(u_ref, up_ref, vg_ref, w_ref, b_ref, o_ref, ubuf):
        u_ref, up_ref, vg_ref = map(_F32Loads, (u_ref, up_ref, vg_ref))
        s = pl.program_id(1)
        ubuf[0:pad, :] = jnp.where(s > 0, up_ref[...], 0.0)
        ubuf[pad:pad + ts, :] = u_ref[...]
        for r0 in range(0, ts, CHUNK):
            u = _conv_fwd_rows(_direct(ubuf), w_ref, b_ref, ktaps, pad, r0, CHUNK)
            o_ref[r0:r0 + CHUNK, :] = (u * _sigmoid(u) * vg_ref[r0:r0 + CHUNK, :]).astype(BF16)

    cur = pl.BlockSpec((ts, cbk), lambda c, s: (s, c))
    return pl.pallas_call(
        body, name=name, grid=(f // cbk, s_len // ts),
        in_specs=[cur, _prev_spec(ts, pad, cbk, lambda g: g[0]), cur,
                  pl.BlockSpec((ktaps, cbk), lambda c, s: (0, c)), pl.BlockSpec((1, cbk), lambda c, s: (0, c))],
        out_specs=cur,
        out_shape=jax.ShapeDtypeStruct((s_len, f), BF16),
        scratch_shapes=[pltpu.VMEM((pad + ts, cbk), F32)],
        compiler_params=_cp("parallel", "parallel"),
    )(u0, u0, vg, cw, cb)


def _ffn_mid_bwd(u0, vg, dt, cw, cb, name):
    s_len, f = u0.shape
    ktaps = cw.shape[0]
    pad = SHORT_PAD
    ts = _tile(s_len, 256)
    nsteps = s_len // ts
    cbk = 1024 if f % 1024 == 0 else f

    def body(u_ref, up_ref, un_ref, vg_ref, vgn_ref, dt_ref, dtn_ref, w_ref, b_ref,
             du0_ref, dvg_ref, dw_ref, db_ref, ubuf, dubuf, dwacc):
        u_ref, up_ref, un_ref, vg_ref, vgn_ref, dt_ref, dtn_ref = map(
            _F32Loads, (u_ref, up_ref, un_ref, vg_ref, vgn_ref, dt_ref, dtn_ref))
        s = pl.program_id(1)

        @pl.when(s == 0)
        def _():
            dwacc[...] = jnp.zeros_like(dwacc)
            db_ref[...] = jnp.zeros_like(db_ref)

        ubuf[0:pad, :] = jnp.where(s > 0, up_ref[...], 0.0)
        ubuf[pad:pad + ts, :] = u_ref[...]
        ubuf[pad + ts:pad + ts + pad, :] = un_ref[...]
        last = s == nsteps - 1
        for r0 in range(0, ts + pad, CHUNK):
            u = _conv_fwd_rows(_direct(ubuf), w_ref, b_ref, ktaps, pad, r0, CHUNK)
            sg = _sigmoid(u)
            if r0 < ts:
                rows = slice(r0, r0 + CHUNK)
                dtr, vgr = dt_ref[rows, :], vg_ref[rows, :]
                dvg_ref[rows, :] = (dtr * u * sg).astype(BF16)
            else:
                rows = slice(r0 - ts, r0 - ts + CHUNK)
                dtr, vgr = jnp.where(last, 0.0, dtn_ref[rows, :]), vgn_ref[rows, :]
            dubuf[r0:r0 + CHUNK, :] = dtr * vgr * (sg * (1.0 + u * (1.0 - sg)))
        db_ref[...] += jnp.sum(dubuf[0:ts, :], axis=0, keepdims=True)
        for r0 in range(0, ts, CHUNK):
            du0 = _conv_bwd_rows(_direct(dubuf), u_ref[r0:r0 + CHUNK, :], w_ref, dwacc, ktaps, r0, CHUNK)
            du0_ref[r0:r0 + CHUNK, :] = du0.astype(BF16)

        @pl.when(last)
        def _():
            _flush_dw(dwacc, dw_ref, ktaps)

    cur = pl.BlockSpec((ts, cbk), lambda c, s: (s, c))
    prv = _prev_spec(ts, pad, cbk, lambda g: g[0])
    nxt = _next_spec(ts, pad, cbk, lambda g: g[0], s_len)
    return pl.pallas_call(
        body, name=name, grid=(f // cbk, nsteps),
        in_specs=[cur, prv, nxt, cur, nxt, cur, nxt,
                  pl.BlockSpec((ktaps, cbk), lambda c, s: (0, c)), pl.BlockSpec((1, cbk), lambda c, s: (0, c))],
        out_specs=[cur, cur, pl.BlockSpec((ktaps, cbk), lambda c, s: (0, c)),
                   pl.BlockSpec((1, cbk), lambda c, s: (0, c))],
        out_shape=[jax.ShapeDtypeStruct((s_len, f), BF16), jax.ShapeDtypeStruct((s_len, f), BF16),
                   jax.ShapeDtypeStruct((ktaps, f), F32), jax.ShapeDtypeStruct((1, f), F32)],
        scratch_shapes=[pltpu.VMEM((pad + ts + pad, cbk), F32), pltpu.VMEM((ts + pad, cbk), F32),
                        pltpu.VMEM((ktaps * SUBLANES, cbk), F32)],
        compiler_params=_cp("parallel", "arbitrary"),
    )(u0, u0, u0, vg, vg, dt, dt, cw, cb)


def _b_mid_fwd(ub, cw, cb, lng, lnb, name):
    s_len, d2 = ub.shape
    d = d2 // 2
    ktaps = cw.shape[0]
    pad = LONG_PAD
    ts = _tile(s_len, 256)

    def body(a_ref, g_ref, ap_ref, gp_ref, w_ref, b_ref, lng_ref, lnb_ref, a2_ref, a4_ref, abuf):
        a_ref, g_ref, ap_ref, gp_ref = map(_F32Loads, (a_ref, g_ref, ap_ref, gp_ref))
        s = pl.program_id(0)
        abuf[0, 0:pad, :] = jnp.where(s > 0, ap_ref[...] * _sigmoid(gp_ref[...]), 0.0)
        abuf[0, pad:pad + ts, :] = a_ref[...] * _sigmoid(g_ref[...])
        _make_shifts(abuf, pad + ts)
        for r0 in range(0, ts, CHUNK):
            a2_ref[r0:r0 + CHUNK, :] = _conv_fwd_rows(_shifted(abuf), w_ref, b_ref, ktaps, pad, r0, CHUNK)
        a2 = a2_ref[...]
        mu = jnp.mean(a2, axis=-1, keepdims=True)
        ac = a2 - mu
        var = jnp.mean(ac * ac, axis=-1, keepdims=True)
        a3 = ac * lax.rsqrt(var + LN_EPS) * lng_ref[...] + lnb_ref[...]
        a4_ref[...] = (a3 * _sigmoid(a3)).astype(BF16)

    def cur(part):
        return pl.BlockSpec((ts, d), lambda s: (s, part))

    vec = _full((1, d))
    return pl.pallas_call(
        body, name=name, grid=(s_len // ts,),
        in_specs=[cur(0), cur(1), _prev_spec(ts, pad, d, lambda g: 0), _prev_spec(ts, pad, d, lambda g: 1),
                  _full((ktaps, d)), vec, vec, vec],
        out_specs=[cur(0), cur(0)],
        out_shape=[jax.ShapeDtypeStruct((s_len, d), F32), jax.ShapeDtypeStruct((s_len, d), BF16)],
        scratch_shapes=[pltpu.VMEM((SUBLANES, pad + ts, d), F32)],
        compiler_params=_cp("parallel"),
    )(ub, ub, ub, ub, cw, cb, lng, lnb)


def _b_mid_bwd(ub, a2, da4, cw, lng, lnb, name):
    s_len, d2 = ub.shape
    d = d2 // 2
    ktaps = cw.shape[0]
    pad = LONG_PAD
    ts = _tile(s_len, 256)
    nsteps = s_len // ts

    def body(a_ref, g_ref, a2_ref, a2n_ref, da4_ref, da4n_ref, w_ref, lng_ref, lnb_ref,
             du_ref, dw_ref, db_ref, dlng_ref, dlnb_ref, dbias_ref, dabuf, dwacc):
        a_ref, g_ref, da4_ref, da4n_ref = map(_F32Loads, (a_ref, g_ref, da4_ref, da4n_ref))
        s = pl.program_id(0)
        last = s == nsteps - 1

        @pl.when(s == 0)
        def _():
            dwacc[...] = jnp.zeros_like(dwacc)
            for ref in (db_ref, dlng_ref, dlnb_ref, dbias_ref):
                ref[...] = jnp.zeros_like(ref)

        def ln_silu_bwd(a2_t, da4_t):
            mu = jnp.mean(a2_t, axis=-1, keepdims=True)
            ac = a2_t - mu
            var = jnp.mean(ac * ac, axis=-1, keepdims=True)
            rstd = lax.rsqrt(var + LN_EPS)
            ah = ac * rstd
            a3 = ah * lng_ref[...] + lnb_ref[...]
            sg = _sigmoid(a3)
            da3 = da4_t * (sg * (1.0 + a3 * (1.0 - sg)))
            dah = da3 * lng_ref[...]
            m1 = jnp.mean(dah, axis=-1, keepdims=True)
            m2 = jnp.mean(dah * ah, axis=-1, keepdims=True)
            return rstd * (dah - m1 - ah * m2), da3, ah

        da2, da3, ah = ln_silu_bwd(a2_ref[...], da4_ref[...])
        dabuf[0, 0:ts, :] = da2
        dlng_ref[...] += jnp.sum(da3 * ah, axis=0, keepdims=True)
        dlnb_ref[...] += jnp.sum(da3, axis=0, keepdims=True)
        db_ref[...] += jnp.sum(da2, axis=0, keepdims=True)
        da2n, _, _ = ln_silu_bwd(a2n_ref[...], jnp.where(last, 0.0, da4n_ref[...]))
        dabuf[0, ts:ts + pad, :] = da2n
        _make_shifts(dabuf, ts + pad)
        for r0 in range(0, ts, CHUNK):
            rows = slice(r0, r0 + CHUNK)
            a_r, g_r = a_ref[rows, :], g_ref[rows, :]
            sg = _sigmoid(g_r)
            da1 = _conv_bwd_rows(_shifted(dabuf), a_r * sg, w_ref, dwacc, ktaps, r0, CHUNK)
            da = da1 * sg
            dg = da1 * a_r * sg * (1.0 - sg)
            du_ref[rows, 0:d] = da.astype(BF16)
            du_ref[rows, d:2 * d] = dg.astype(BF16)
            dbias_ref[:, 0:d] += jnp.sum(da, axis=0, keepdims=True)
            dbias_ref[:, d:2 * d] += jnp.sum(dg, axis=0, keepdims=True)

        @pl.when(last)
        def _():
            _flush_dw(dwacc, dw_ref, ktaps)

    def cur(part):
        return pl.BlockSpec((ts, d), lambda s: (s, part))

    vec = _full((1, d))
    nxt = _next_spec(ts, pad, d, lambda g: 0, s_len)
    return pl.pallas_call(
        body, name=name, grid=(nsteps,),
        in_specs=[cur(0), cur(1), cur(0), nxt, cur(0), nxt, _full((ktaps, d)), vec, vec],
        out_specs=[pl.BlockSpec((ts, d2), lambda s: (s, 0)), _full((ktaps, d)), vec, vec, vec, _full((1, d2))],
        out_shape=[jax.ShapeDtypeStruct((s_len, d2), BF16), jax.ShapeDtypeStruct((ktaps, d), F32),
                   jax.ShapeDtypeStruct((1, d), F32), jax.ShapeDtypeStruct((1, d), F32),
                   jax.ShapeDtypeStruct((1, d), F32), jax.ShapeDtypeStruct((1, d2), F32)],
        scratch_shapes=[pltpu.VMEM((SUBLANES, ts + pad, d), F32), pltpu.VMEM((ktaps * SUBLANES, d), F32)],
        compiler_params=_cp("arbitrary"),
    )(ub, ub, a2, a2, da4, da4, cw, lng, lnb)


def _loss_head(xo, tgt, name):
    s_len, d = xo.shape
    tm = _tile(s_len, 512)

    def body(x_ref, t_ref, d_ref, l_ref):
        @pl.when(pl.program_id(0) == 0)
        def _():
            l_ref[...] = jnp.zeros_like(l_ref)

        e = x_ref[...] - t_ref[...]
        d_ref[...] = e * (1.0 / d)
        per_row = jnp.sum(e * e, axis=-1, keepdims=True) * (1.0 / d)
        l_ref[...] += 0.5 * jnp.sum(per_row, axis=0, keepdims=True)

    row = pl.BlockSpec((tm, d), lambda i: (i, 0))
    return pl.pallas_call(
        body, name=name, grid=(s_len // tm,),
        in_specs=[row, row], out_specs=[row, _full((1, LANES))],
        out_shape=[jax.ShapeDtypeStruct((s_len, d), F32), jax.ShapeDtypeStruct((1, LANES), F32)],
        compiler_params=_cp("arbitrary"),
    )(xo, tgt)


def _ada_fwd(c_all, ada_w, ada_b_loc, name):
    depth, d, n = ada_w.shape

    def body(c_ref, w_ref, b_ref, o_ref):
        c = c_ref[...]
        act = c * _sigmoid(c)
        o_ref[...] = jnp.dot(act, w_ref[...], preferred_element_type=F32,
                             precision=lax.Precision.HIGHEST) + b_ref[...]

    return pl.pallas_call(
        body, name=name, grid=(depth,),
        in_specs=[_full((NDEV, d)), pl.BlockSpec((None, d, n), lambda i: (i, 0, 0)),
                  pl.BlockSpec((None, 1, n), lambda i: (i, 0, 0))],
        out_specs=pl.BlockSpec((None, NDEV, n), lambda i: (i, 0, 0)),
        out_shape=jax.ShapeDtypeStruct((depth, NDEV, n), F32),
        compiler_params=_cp("parallel"),
    )(c_all, ada_w, ada_b_loc.reshape(depth, 1, n))


def _ada_bwd(c_all_t, dmod_cols, name):
    depth, _, n = dmod_cols.shape
    d = c_all_t.shape[0]

    def body(ct_ref, dm_ref, o_ref):
        ct = ct_ref[...]
        act = ct * _sigmoid(ct)
        acc = None
        for b in range(NDEV):
            term = act[:, b:b + 1] * dm_ref[b:b + 1, :]
            acc = term if acc is None else acc + term
        o_ref[...] = acc

    return pl.pallas_call(
        body, name=name, grid=(depth,),
        in_specs=[_full((d, NDEV)), pl.BlockSpec((None, NDEV, n), lambda i: (i, 0, 0))],
        out_specs=pl.BlockSpec((None, d, n), lambda i: (i, 0, 0)),
        out_shape=jax.ShapeDtypeStruct((depth, d, n), F32),
        compiler_params=_cp("parallel"),
    )(c_all_t, dmod_cols)


def _sum_parts(parts, name):
    _, rows, lanes = parts.shape

    def body(p_ref, o_ref):
        acc = p_ref[0]
        for k in range(1, NDEV):
            acc = acc + p_ref[k]
        o_ref[...] = acc

    return pl.pallas_call(
        body, name=name, in_specs=[_full(parts.shape)], out_specs=_full((rows, lanes)), grid=(1,),
        out_shape=jax.ShapeDtypeStruct((rows, lanes), F32), compiler_params=_cp("arbitrary"),
    )(parts)


def _adamw(w, glist, m, v, name):
    nl, rows, cols = w.shape
    tr = _tile(rows, 256, 2 * SUBLANES)

    def body(w_ref, *rest):
        g_refs = rest[:nl]
        m_ref, v_ref, go_ref, d_ref, mo_ref, vo_ref = rest[nl:]
        g = None
        for layer, g_ref in enumerate(g_refs):
            part = g_ref[0].astype(F32)
            for p in range(1, g_ref.shape[0]):
                part = part + g_ref[p].astype(F32)
            g = part if g is None else jnp.where(pl.program_id(0) == layer, part, g)
        m1 = ADAM_B1 * m_ref[...] + (1.0 - ADAM_B1) * g
        v1 = ADAM_B2 * v_ref[...] + (1.0 - ADAM_B2) * (g * g)
        m_hat = m1 / (1.0 - ADAM_B1 ** ADAM_STEP)
        v_hat = v1 / (1.0 - ADAM_B2 ** ADAM_STEP)
        go_ref[...] = g
        mo_ref[...] = m1
        vo_ref[...] = v1
        d_ref[...] = -ADAM_LR * (m_hat / (jnp.sqrt(v_hat) + ADAM_EPS) + ADAM_WD * w_ref[...])

    blk = pl.BlockSpec((None, tr, cols), lambda l, i: (l, i, 0))
    g_specs = [pl.BlockSpec((g.shape[0], tr, cols), lambda l, i: (0, i, 0)) for g in glist]
    return pl.pallas_call(
        body, name=name, grid=(nl, rows // tr),
        in_specs=[blk] + g_specs + [blk, blk],
        out_specs=[blk] * 4, out_shape=[jax.ShapeDtypeStruct((nl, rows, cols), F32)] * 4,
        compiler_params=_cp("parallel", "parallel"),
    )(w, *glist, m, v)


def _pack(pieces):
    flat = jnp.concatenate([p.reshape(-1) for p in pieces])
    unit = SUBLANES * LANES
    padded = -(-flat.shape[0] // unit) * unit
    return jnp.pad(flat, (0, padded - flat.shape[0])).reshape(padded // LANES, LANES)


def _unpack(packed, shapes, lead=()):
    flat = packed.reshape(lead + (-1,))
    out, off = [], 0
    for s in shapes:
        size = 1
        for dim in s:
            size *= dim
        out.append(flat[..., off:off + size].reshape(lead + tuple(s)))
        off += size
    return out


def _pad_last(a, n):
    return jnp.pad(a, [(0, 0)] * (a.ndim - 1) + [(0, n - a.shape[-1])])


def kernel(x, c, ada_w, ada_b, ln_tok_g, ln_tok_b, ln_ch_g, ln_ch_b, a_w_in, a_conv_w, a_conv_b, a_w_out, b_w_pw1, b_b_pw1, b_conv_w, b_conv_b, b_ln_g, b_ln_b, b_w_pw2, b_b_pw2, f_w_up, f_conv_w, f_conv_b, f_w_gate, f_w_down, loss_target, m_ada_w, m_ada_b, m_ln_tok_g, m_ln_tok_b, m_ln_ch_g, m_ln_ch_b, m_a_w_in, m_a_conv_w, m_a_conv_b, m_a_w_out, m_b_w_pw1, m_b_b_pw1, m_b_conv_w, m_b_conv_b, m_b_ln_g, m_b_ln_b, m_b_w_pw2, m_b_b_pw2, m_f_w_up, m_f_conv_w, m_f_conv_b, m_f_w_gate, m_f_w_down, v_ada_w, v_ada_b, v_ln_tok_g, v_ln_tok_b, v_ln_ch_g, v_ln_ch_b, v_a_w_in, v_a_conv_w, v_a_conv_b, v_a_w_out, v_b_w_pw1, v_b_b_pw1, v_b_conv_w, v_b_conv_b, v_b_ln_g, v_b_ln_b, v_b_w_pw2, v_b_b_pw2, v_f_w_up, v_f_conv_w, v_f_conv_b, v_f_w_gate, v_f_w_down):
    weights = dict(ada_w=ada_w, ada_b=ada_b, ln_tok_g=ln_tok_g, ln_tok_b=ln_tok_b, ln_ch_g=ln_ch_g, ln_ch_b=ln_ch_b, a_w_in=a_w_in, a_conv_w=a_conv_w, a_conv_b=a_conv_b, a_w_out=a_w_out, b_w_pw1=b_w_pw1, b_b_pw1=b_b_pw1, b_conv_w=b_conv_w, b_conv_b=b_conv_b, b_ln_g=b_ln_g, b_ln_b=b_ln_b, b_w_pw2=b_w_pw2, b_b_pw2=b_b_pw2, f_w_up=f_w_up, f_conv_w=f_conv_w, f_conv_b=f_conv_b, f_w_gate=f_w_gate, f_w_down=f_w_down)
    mom_m = dict(ada_w=m_ada_w, ada_b=m_ada_b, ln_tok_g=m_ln_tok_g, ln_tok_b=m_ln_tok_b, ln_ch_g=m_ln_ch_g, ln_ch_b=m_ln_ch_b, a_w_in=m_a_w_in, a_conv_w=m_a_conv_w, a_conv_b=m_a_conv_b, a_w_out=m_a_w_out, b_w_pw1=m_b_w_pw1, b_b_pw1=m_b_b_pw1, b_conv_w=m_b_conv_w, b_conv_b=m_b_conv_b, b_ln_g=m_b_ln_g, b_ln_b=m_b_ln_b, b_w_pw2=m_b_w_pw2, b_b_pw2=m_b_b_pw2, f_w_up=m_f_w_up, f_conv_w=m_f_conv_w, f_conv_b=m_f_conv_b, f_w_gate=m_f_w_gate, f_w_down=m_f_w_down)
    mom_v = dict(ada_w=v_ada_w, ada_b=v_ada_b, ln_tok_g=v_ln_tok_g, ln_tok_b=v_ln_tok_b, ln_ch_g=v_ln_ch_g, ln_ch_b=v_ln_ch_b, a_w_in=v_a_w_in, a_conv_w=v_a_conv_w, a_conv_b=v_a_conv_b, a_w_out=v_a_w_out, b_w_pw1=v_b_w_pw1, b_b_pw1=v_b_b_pw1, b_conv_w=v_b_conv_w, b_conv_b=v_b_conv_b, b_ln_g=v_b_ln_g, b_ln_b=v_b_ln_b, b_w_pw2=v_b_w_pw2, b_b_pw2=v_b_b_pw2, f_w_up=v_f_w_up, f_conv_w=v_f_conv_w, f_conv_b=v_f_conv_b, f_w_gate=v_f_w_gate, f_w_down=v_f_w_down)
    names = list(weights)

    depth, d, n_ada = ada_w.shape
    assert depth == 2 and a_w_in.shape[0] == 1 and b_w_pw1.shape[0] == 1
    s_len = x.shape[1]
    f_loc = f_w_up.shape[-1]
    f_pad = -(-f_loc // LANES) * LANES
    f_all = NDEV * f_pad
    d_loc = d // NDEV
    ka, kb, kf = a_conv_w.shape[1], b_conv_w.shape[1], f_conv_w.shape[1]
    alpha = (2.0 * depth) ** 0.25
    assert a_w_in.shape[-1] == f_pad and f_pad % d_loc == 0
    me = 4 * lax.axis_index("x") + 2 * lax.axis_index("y") + lax.axis_index("c")

    small_shapes = [(d,), (ka, d_loc), (2 * d_loc,), (kb, d_loc), (d_loc,), (d_loc,), (d_loc,), (d_loc,),
                    (depth, kf, f_pad)]
    small_loc = _pack([c[0], a_conv_w[0], b_b_pw1[0], b_conv_w[0], b_conv_b[0], b_ln_g[0], b_ln_b[0],
                       b_b_pw2[0], _pad_last(f_conv_w, f_pad)])
    g_small, g_in, _ = _exchange([small_loc, a_w_in.astype(BF16)], True, "gather_first")

    (c_all, acw_g, bb1_g, bcw_g, bcb_g, blg_g, blb_g, bb2_g, fcw_g) = _unpack(g_small, small_shapes, (NDEV,))
    a_cw = acw_g.transpose(1, 0, 2).reshape(ka, d)
    b_cw = bcw_g.transpose(1, 0, 2).reshape(kb, d)
    b_b1 = bb1_g.reshape(1, 2 * d)
    b_cb, b_lg, b_lb, b_b2 = (t.reshape(1, d) for t in (bcb_g, blg_g, blb_g, bb2_g))
    f_cw = fcw_g.transpose(1, 2, 0, 3).reshape(depth, kf, f_all)
    f_cb = _pad_last(f_conv_b.reshape(depth, NDEV, f_loc), f_pad).reshape(depth, 1, f_all)

    ada_b_loc = lax.dynamic_slice(ada_b, (0, me * n_ada), (depth, n_ada))
    mod_part = _ada_fwd(c_all, ada_w, ada_b_loc, "ada_fwd")
    mod_g, mod_done = _exchange([mod_part.reshape(depth * NDEV, n_ada)], True, "gather_mod")
    mod_all = mod_g.reshape(NDEV, depth, NDEV, n_ada).transpose(1, 2, 0, 3).reshape(depth, NDEV, 6 * d)
    mod = lax.dynamic_slice(mod_all, (0, me, 0), (depth, 1, 6 * d))[:, 0]

    gather_out = _exchange_start([_after(a_w_out[0], mod_done).astype(BF16)], True, "gather_out_start")
    up_pad = _pad_last(_after(f_w_up, gather_out[-1]), f_pad).astype(BF16)
    gate_pad = _pad_last(f_w_gate, f_pad).astype(BF16)
    down_pad = jnp.pad(f_w_down, ((0, 0), (0, f_pad - f_loc), (0, 0))).astype(BF16)
    col_f = [jnp.stack([up_pad[i], gate_pad[i]]) for i in range(depth)]
    row_b = jnp.concatenate([down_pad[1], b_w_pw2[0].astype(BF16)], axis=0)
    ridx_pw2 = f_pad // d_loc
    gather_f0 = _exchange_start([col_f[0], down_pad[0]], True, "gather_f0_start")

    def mod_rows(i):
        return [mod[i:i + 1, j * d:(j + 1) * d] for j in range(6)]

    zeros_d = jnp.zeros((1, d), F32)
    zeros_f = jnp.zeros((1, f_all), F32)
    x0 = x[0]

    sh_t0, sc_t0, g_t0, sh_c0, sc_c0, g_c0 = mod_rows(0)
    sh_t1, sc_t1, g_t1, sh_c1, sc_c1, g_c1 = mod_rows(1)

    sc_t0 = _after(sc_t0, gather_f0[-1])
    bcv, = _mm_fwd(x0, sc_t0, sh_t0, jnp.zeros((1, 3 * d), F32), g_in, (0,), "a_in_fwd")
    y0 = _gateconv_fwd(bcv, a_cw, a_conv_b, "a_conv_fwd")
    g_out, _ = _exchange_wait(gather_out, y0, True, "gather_out_wait")
    y_a, x1, xh1, rs1 = _mm_ln(y0, g_out, d_loc, 0, x0, g_t0, ln_tok_g[0:1], ln_tok_b[0:1], zeros_d,
                               alpha, "a_out_ln_fwd")

    def ffn_fwd(xin, sc, sh, gate, gam, bet, g_colf, g_rowf, layer, tag):
        u0, vg = _mm_fwd(xin, sc, sh, zeros_f, g_colf, (0, 1), "f_upgate_fwd" + tag)
        t = _ffn_mid_fwd(u0, vg, f_cw[layer], f_cb[layer], "f_mid_fwd" + tag)
        y, xo, xh, rs = _mm_ln(t, g_rowf, f_pad, 0, xin, gate, gam, bet, zeros_d, alpha, "f_down_ln_fwd" + tag)
        return u0, vg, t, y, xo, xh, rs

    g_colf0, g_rowf0, landed = _exchange_wait(gather_f0, x1, True, "gather_f0_wait")
    gather_1 = _exchange_start([_after(b_w_pw1, landed).astype(BF16), col_f[1], row_b], True, "gather_1_start")
    sc_c0 = _after(sc_c0, gather_1[-1])
    u0_0, vg_0, t_0, y_f0, x2, xh2, rs2 = ffn_fwd(x1, sc_c0, sh_c0, g_c0, ln_ch_g[0:1], ln_ch_b[0:1],
                                                  g_colf0, g_rowf0, 0, "0")

    g_pw1, g_colf1, g_rowb, _ = _exchange_wait(gather_1, x2, True, "gather_1_wait")
    ub, = _mm_fwd(x2, sc_t1, sh_t1, b_b1, g_pw1, (0,), "b_pw1_fwd")
    a2, a4 = _b_mid_fwd(ub, b_cw, b_cb, b_lg, b_lb, "b_mid_fwd")
    y_b, x3, xh3, rs3 = _mm_ln(a4, g_rowb, d_loc, ridx_pw2, x2, g_t1, ln_tok_g[1:2], ln_tok_b[1:2], b_b2,
                               alpha, "b_pw2_ln_fwd")
    u0_1, vg_1, t_1, y_f1, x4, xh4, rs4 = ffn_fwd(x3, sc_c1, sh_c1, g_c1, ln_ch_g[1:2], ln_ch_b[1:2],
                                                  g_colf1, g_rowb, 1, "1")

    dx4, loss_part = _loss_head(x4, loss_target[0], "loss_head")
    loss = lax.psum(loss_part[0, 0], MESH_AXES)

    def ffn_bwd(dxo, xin, sc, sh, gate, gam, u0, vg, t, y, xh, rs, g_colf, g_rowf, layer, tag):
        dy, dres, acc = _ln_bwd(dxo, xh, rs, gam, y, gate, alpha, "f_ln_bwd" + tag)
        dt = _mm_nt_row(dy, g_rowf, f_pad, 0, "f_down_dx" + tag)
        dw_down = _mm_tn_row(t, dy, f_pad, f_loc, "f_down_dw" + tag)
        du0, dvg, dcw, dcb = _ffn_mid_bwd(u0, vg, dt, f_cw[layer], f_cb[layer], "f_mid_bwd" + tag)
        dw_up = _mm_tn_col_t(xin, sc, sh, du0, f_loc, "f_up_dw" + tag)
        dw_gate = _mm_tn_col_t(xin, sc, sh, dvg, f_loc, "f_gate_dw" + tag)
        scatter = _exchange_start([dw_up, dw_gate, dw_down], False, "scatter_f%s_start" % tag)
        dxin, acc2 = _mm_nt_mod([du0, dvg], g_colf, (0, 1), xin, _after(sc, scatter[-1]), dres, "f_upgate_dx" + tag)
        return dxin, acc, acc2, scatter, dcw, dcb

    dx3, accf1, acc2f1, scatter_f1, dfcw1, dfcb1 = ffn_bwd(
        dx4, x3, sc_c1, sh_c1, g_c1, ln_ch_g[1:2], u0_1, vg_1, t_1, y_f1, xh4, rs4, g_colf1, g_rowb, 1, "1")

    dy, dres, accb = _ln_bwd(dx3, xh3, rs3, ln_tok_g[1:2], y_b, g_t1, alpha, "b_ln_bwd")
    da4 = _mm_nt_row(dy, g_rowb, d_loc, ridx_pw2, "b_pw2_dx")
    dw_pw2 = _mm_tn_row(a4, dy, d_loc, d_loc, "b_pw2_dw")
    du, dbcw, dbcb, dblg, dblb, dbb1 = _b_mid_bwd(ub, a2, da4, b_cw, b_lg, b_lb, "b_mid_bwd")
    dw_pw1 = _mm_tn_col(x2, sc_t1, sh_t1, du, "b_pw1_dw")
    scatter_b = _exchange_start([dw_pw1, dw_pw2], False, "scatter_b_start")
    dx2, acc2b = _mm_nt_mod([du], g_pw1, (0,), x2, _after(sc_t1, scatter_b[-1]), dres, "b_pw1_dx")

    dx1, accf0, acc2f0, scatter_f0, dfcw0, dfcb0 = ffn_bwd(
        dx2, x1, sc_c0, sh_c0, g_c0, ln_ch_g[0:1], u0_0, vg_0, t_0, y_f0, xh2, rs2, g_colf0, g_rowf0, 0, "0")

    dy, dres, acca = _ln_bwd(dx1, xh1, rs1, ln_tok_g[0:1], y_a, g_t0, alpha, "a_ln_bwd")
    dy0 = _mm_nt_row(dy, g_out, d_loc, 0, "a_out_dx")
    dbcv, dacw, dacb = _gateconv_bwd(bcv, dy0, a_cw, a_conv_b, "a_conv_bwd")
    dx0, acc2a = _mm_nt_mod([dbcv], g_in, (0,), x0, sc_t0, dres, "a_in_dx")

    def dmod_row(acc2_t, acc_t, acc2_c, acc_c):
        return jnp.concatenate([acc2_t[1], acc2_t[0], acc_t[2], acc2_c[1], acc2_c[0], acc_c[2]])

    dmod = jnp.stack([dmod_row(acc2a, acca, acc2f0, accf0), dmod_row(acc2b, accb, acc2f1, accf1)])

    def unpad_f(a):
        return a.reshape(a.shape[:-1] + (NDEV, f_pad))[..., :f_loc].reshape(a.shape[:-1] + (NDEV * f_loc,))

    small_grads = [
        dmod,
        jnp.stack([acca[0], accb[0]]), jnp.stack([acca[1], accb[1]]),
        jnp.stack([accf0[0], accf1[0]]), jnp.stack([accf0[1], accf1[1]]),
        dacb,
        unpad_f(jnp.concatenate([dfcb0, dfcb1], axis=0)),
        dacw, dbb1, dbcw, dbcb, dblg, dblb, accb[3:4],
        jnp.stack([dfcw0, dfcw1]),
    ]
    small_grad_shapes = [tuple(g.shape) for g in small_grads]
    gather_small = _exchange_start([_pack(small_grads)], True, "gather_small_start")

    dw_in = _mm_tn_col(x0, _after(sc_t0, gather_small[-1]), sh_t0, dbcv, "a_in_dw")
    dw_out = _mm_tn_row(y0, dy, d_loc, d_loc, "a_out_dw")
    scatter_a = _exchange_start([dw_in, dw_out], False, "scatter_a_start")

    grads, deltas, new_m, new_v = {}, {}, {}, {}

    def adamw(k, glist, transposed=False):
        def view(a):
            a = jnp.swapaxes(a, 1, 2) if transposed else a
            return a.reshape(len(glist), -1, a.shape[-1])

        w = view(weights[k])
        outs = _adamw(w, [g.reshape(g.shape[0], -1, w.shape[-1]) for g in glist],
                      view(mom_m[k]), view(mom_v[k]), "adamw_" + k)
        if transposed:
            outs = [jnp.swapaxes(o, 1, 2) for o in outs]
        grads[k], deltas[k], new_m[k], new_v[k] = (o.reshape(weights[k].shape) for o in outs)

    r_up1, r_gate1, r_down1, _ = _exchange_wait(scatter_f1, scatter_a[-1], False, "scatter_f1_wait")
    r_pw1, r_pw2, _ = _exchange_wait(scatter_b, r_down1, False, "scatter_b_wait")
    adamw("b_w_pw1", [r_pw1])
    adamw("b_w_pw2", [r_pw2])
    r_up0, r_gate0, r_down0, _ = _exchange_wait(scatter_f0, deltas["b_w_pw2"], False, "scatter_f0_wait")
    adamw("f_w_up", [r_up0, r_up1], transposed=True)
    adamw("f_w_gate", [r_gate0, r_gate1], transposed=True)
    adamw("f_w_down", [r_down0, r_down1])

    sg_all, _ = _exchange_wait(gather_small, deltas["f_w_down"], True, "gather_small_wait")
    sg_sum = _sum_parts(sg_all, "sum_small_grads")
    (g_ada_b, g_ltg, g_ltb, g_lcg, g_lcb, g_acb, g_fcb, g_acw, g_bb1, g_bcw, g_bcb, g_blg, g_blb, g_bb2,
     g_fcw) = _unpack(sg_sum, small_grad_shapes)

    def my_cols(a, width):
        return lax.dynamic_slice_in_dim(a, me * width, width, axis=a.ndim - 1)

    g_fcw_loc = my_cols(g_fcw, f_pad)[..., :f_loc]
    small = dict(
        ada_b=g_ada_b, ln_tok_g=g_ltg, ln_tok_b=g_ltb, ln_ch_g=g_lcg, ln_ch_b=g_lcb, a_conv_b=g_acb, f_conv_b=g_fcb,
        a_conv_w=my_cols(g_acw, d_loc)[None], b_b_pw1=my_cols(g_bb1, 2 * d_loc), b_conv_w=my_cols(g_bcw, d_loc)[None],
        b_conv_b=my_cols(g_bcb, d_loc), b_ln_g=my_cols(g_blg, d_loc), b_ln_b=my_cols(g_blb, d_loc),
        b_b_pw2=my_cols(g_bb2, d_loc), f_conv_w=g_fcw_loc)

    dmod_all = sg_all.reshape(NDEV, -1)[:, :depth * 6 * d].reshape(NDEV, depth, 6 * d)
    dmod_cols = my_cols(dmod_all, n_ada).transpose(1, 0, 2)
    g_ada_w = _ada_bwd(c_all.T, dmod_cols, "ada_bwd")

    adamw("ada_w", [g_ada_w[0:1], g_ada_w[1:2]])
    for k, g in small.items():
        adamw(k, [g[None]])

    r_in, r_out, _ = _exchange_wait(scatter_a, deltas["ada_w"], False, "scatter_a_wait")
    adamw("a_w_in", [r_in])
    adamw("a_w_out", [r_out])

    return (loss, dx0[None], *[grads[k] for k in names], *[deltas[k] for k in names],
            *[new_m[k] for k in names], *[new_v[k] for k in names])
```

```python
import functools

import jax
import jax.numpy as jnp
from jax import lax
from jax.experimental import pallas as pl
from jax.experimental.pallas import tpu as pltpu

NDEV = 8
MESH_AXES = ("x", "y", "c")
LANES = 128
SUBLANES = 8
VMEM_LIMIT = 56 * 1024 * 1024
LN_EPS = 1e-5
SHORT_PAD = 16
LONG_PAD = 32
CHUNK = 16
ADAM_LR, ADAM_B1, ADAM_B2, ADAM_EPS, ADAM_WD, ADAM_STEP = 0.001, 0.9, 0.999, 1e-08, 0.01, 10

F32 = jnp.float32
BF16 = jnp.bfloat16
MESH = pl.DeviceIdType.MESH
NT = (((1,), (1,)), ((), ()))
TN = (((0,), (0,)), ((), ()))


def _tile(n, target, mult=SUBLANES):
    best = None
    for t in range(mult, min(n, target) + 1, mult):
        if n % t == 0:
            best = t
    return best if best is not None else n


def _full(shape):
    nd = len(shape)
    return pl.BlockSpec(shape, lambda *_: (0,) * nd)


def _cp(*sem):
    return pltpu.CompilerParams(dimension_semantics=sem, vmem_limit_bytes=VMEM_LIMIT)


def _sigmoid(x):
    return 1.0 / (1.0 + jnp.exp(-x))


def _peer(x, y, c, d):
    return ((1 - x) if d & 4 else x, (1 - y) if d & 2 else y, (1 - c) if d & 1 else c)


def _lin(p):
    return 4 * p[0] + 2 * p[1] + p[2]


CHIP_MASKS = (2, 4, 6)
MODES_PER_ARRAY = {"gather": NDEV - 1, "scatter": NDEV - 1, "gather_chips": 1 + len(CHIP_MASKS),
                   "forward": len(CHIP_MASKS)}


def _transfers(mode):
    x, y, c = (lax.axis_index(a) for a in MESH_AXES)
    me = _lin((x, y, c))
    if mode == "forward":
        sibling = (x, y, 1 - c)
        return [(sibling, ("land", _lin(_peer(x, y, c, q))), _lin(_peer(x, y, c, q)), _lin(_peer(x, y, c, q ^ 1)))
                for q in CHIP_MASKS]
    masks = (1,) + CHIP_MASKS if mode == "gather_chips" else range(1, NDEV)
    out = []
    for d in masks:
        peer = _peer(x, y, c, d)
        source = ("block", _lin(peer)) if mode == "scatter" else ("whole", None)
        out.append((peer, source, me, _lin(peer)))
    return out


def _remote_copies(src_refs, land_refs, send_sems, recv_sems, mode):
    transfers = _transfers(mode)
    sends, recvs = [], []
    for i, land_ref in enumerate(land_refs):
        for t, (peer, (kind, slot), there, here) in enumerate(transfers):
            k = i * len(transfers) + t
            src = land_ref.at[slot] if kind == "land" else src_refs[i].at[slot] if kind == "block" else src_refs[i]
            for dst_slot, out in ((there, sends), (here, recvs)):
                out.append(pltpu.make_async_remote_copy(
                    src_ref=src, dst_ref=land_ref.at[dst_slot], send_sem=send_sems.at[k], recv_sem=recv_sems.at[k],
                    device_id=peer, device_id_type=MESH))
    return sends, recvs


def _exchange(srcs, mode, name):
    n = len(srcs)
    gather = mode == "gather"

    def body(*refs):
        src_refs, out_refs, token = refs[:n], refs[n:2 * n], refs[2 * n]
        send_sems, recv_sems, local_sems = refs[2 * n + 1:]
        me = _lin(tuple(lax.axis_index(a) for a in MESH_AXES))
        local = []
        for i in range(n):
            mine = src_refs[i] if gather else src_refs[i].at[me]
            cp = pltpu.make_async_copy(mine, out_refs[i].at[me], local_sems.at[i])
            cp.start()
            local.append(cp)
        sends, recvs = _remote_copies(src_refs, out_refs, send_sems, recv_sems, mode)
        for snd in sends:
            snd.start()
        token[...] = jnp.zeros_like(token)
        for snd, rcv in zip(sends, recvs):
            snd.wait_send()
            rcv.wait_recv()
        for cp in local:
            cp.wait()

    out_shape = [jax.ShapeDtypeStruct(((NDEV,) + s.shape) if gather else s.shape, s.dtype) for s in srcs]
    out_shape.append(jax.ShapeDtypeStruct((SUBLANES, LANES), F32))
    any_spec = pl.BlockSpec(memory_space=pl.ANY)
    return pl.pallas_call(
        body, name=name, out_shape=out_shape,
        in_specs=[any_spec] * n, out_specs=[any_spec] * n + [pl.BlockSpec(memory_space=pltpu.VMEM)],
        scratch_shapes=[pltpu.SemaphoreType.DMA((n * (NDEV - 1),)),
                        pltpu.SemaphoreType.DMA((n * (NDEV - 1),)),
                        pltpu.SemaphoreType.DMA((n,))],
    )(*srcs)


HBM_SPEC = pl.BlockSpec(memory_space=pltpu.HBM)
SEM_SPEC = pl.BlockSpec(memory_space=pltpu.SEMAPHORE)
SIDE_EFFECT = pltpu.SideEffectType.DATAFLOW_SIDE_EFFECTING


def _exchange_start(arrays, mode, name):
    me = _lin(tuple(lax.axis_index(a) for a in MESH_AXES))
    if mode == "forward":
        srcs, lands = [], list(arrays)
    else:
        srcs, lands = list(arrays), []
        for s in srcs:
            own = lax.dynamic_index_in_dim(s, me, 0, keepdims=False) if mode == "scatter" else s
            shape = s.shape if mode == "scatter" else (NDEV,) + s.shape
            lands.append(lax.dynamic_update_index_in_dim(lax.empty(shape, s.dtype), own, me, 0))
    ns, n = len(srcs), len(lands)

    def body(*refs):
        src_refs, land_refs = refs[:ns], refs[ns:ns + n]
        send_sems, recv_sems, token = refs[ns + n], refs[ns + n + 1], refs[-1]
        sends, _ = _remote_copies(src_refs, land_refs, send_sems, recv_sems, mode)
        for snd in sends:
            snd.start()
        token[...] = jnp.zeros_like(token)

    operands = [pltpu.with_memory_space_constraint(a, pltpu.HBM) for a in srcs + lands]
    nsem = n * MODES_PER_ARRAY[mode]
    return pl.pallas_call(
        body, name=name,
        out_shape=(pltpu.SemaphoreType.DMA((nsem,)), pltpu.SemaphoreType.DMA((nsem,)),
                   *[pltpu.HBM(a.shape, a.dtype) for a in operands],
                   jax.ShapeDtypeStruct((SUBLANES, LANES), F32)),
        in_specs=[HBM_SPEC] * (ns + n),
        out_specs=(SEM_SPEC, SEM_SPEC, *([HBM_SPEC] * (ns + n)), pl.BlockSpec(memory_space=pltpu.VMEM)),
        input_output_aliases={i: 2 + i for i in range(ns + n)},
        compiler_params=pltpu.CompilerParams(has_side_effects=SIDE_EFFECT),
    )(*operands)


def _exchange_wait(handle, after, mode, name):
    send_sems, recv_sems, *thru = handle[:-1]
    n = len(thru) if mode == "forward" else len(thru) // 2
    ns = len(thru) - n

    def body(*refs):
        src_refs, land_refs = refs[:ns], refs[ns:ns + n]
        sends, recvs = _remote_copies(src_refs, land_refs, refs[ns + n], refs[ns + n + 1], mode)
        for snd, rcv in zip(sends, recvs):
            snd.wait_send()
            rcv.wait_recv()
        refs[-1][...] = jnp.zeros_like(refs[-1])

    outs = pl.pallas_call(
        body, name=name,
        out_shape=(*[pltpu.HBM(a.shape, a.dtype) for a in thru], jax.ShapeDtypeStruct((SUBLANES, LANES), F32)),
        in_specs=[HBM_SPEC] * (ns + n) + [SEM_SPEC, SEM_SPEC, pl.BlockSpec(memory_space=pl.ANY)],
        out_specs=[HBM_SPEC] * (ns + n) + [pl.BlockSpec(memory_space=pltpu.VMEM)],
        input_output_aliases={i: i for i in range(ns + n)},
        compiler_params=pltpu.CompilerParams(has_side_effects=SIDE_EFFECT),
    )(*thru, send_sems, recv_sems, after)
    return outs[ns:]


def _gather_two_level(srcs, after, name):
    first = _exchange_start(srcs, "gather_chips", name + "_chips_start")
    *lands, token = _exchange_wait(first, after, "gather_chips", name + "_chips_wait")
    second = _exchange_start(lands, "forward", name + "_forward_start")
    return _exchange_wait(second, token, "forward", name + "_forward_wait")


def _after(value, token):
    return value + token[0, 0]


ANY_SPEC = pl.BlockSpec(memory_space=pl.ANY)


def _load_cols(wg_ref, widx, w_ref, sems):
    n = wg_ref.shape[-1]
    copies = [pltpu.make_async_copy(wg_ref.at[k, widx], w_ref.at[:, pl.ds(k * n, n)], sems.at[k])
              for k in range(NDEV)]
    for cp in copies:
        cp.start()
    for cp in copies:
        cp.wait()


def _load_rows(wg_ref, r, ridx, w_ref, sems):
    copies = [pltpu.make_async_copy(wg_ref.at[k, pl.ds(ridx * r, r)], w_ref.at[pl.ds(k * r, r)], sems.at[k])
              for k in range(NDEV)]
    for cp in copies:
        cp.start()
    for cp in copies:
        cp.wait()


def _mm_fwd(x, sc, sh, bias, wg, widxs, name):
    s_len, kdim = x.shape
    ncol = NDEV * wg.shape[-1]
    tm = _tile(s_len, 512)
    nw = len(widxs)

    def body(x_ref, sc_ref, sh_ref, b_ref, wg_ref, *rest):
        o_refs, w_refs, sems = rest[:nw], rest[nw:2 * nw], rest[2 * nw]

        @pl.when(pl.program_id(0) == 0)
        def _():
            for i, w_ref in enumerate(w_refs):
                _load_cols(wg_ref, widxs[i], w_ref, sems.at[i])

        h = (x_ref[...] * (1.0 + sc_ref[...]) + sh_ref[...]).astype(BF16)
        for w_ref, o_ref in zip(w_refs, o_refs):
            o_ref[...] = (jnp.dot(h, w_ref[...], preferred_element_type=F32) + b_ref[...]).astype(BF16)

    return pl.pallas_call(
        body, name=name, grid=(s_len // tm,),
        in_specs=[pl.BlockSpec((tm, kdim), lambda i: (i, 0)), _full((1, kdim)), _full((1, kdim)),
                  _full((1, ncol)), ANY_SPEC],
        out_specs=[pl.BlockSpec((tm, ncol), lambda i: (i, 0))] * nw,
        out_shape=[jax.ShapeDtypeStruct((s_len, ncol), BF16)] * nw,
        scratch_shapes=[pltpu.VMEM((kdim, ncol), BF16)] * nw + [pltpu.SemaphoreType.DMA((nw, NDEV))],
        compiler_params=_cp("arbitrary"),
    )(x, sc, sh, bias, wg)


def _mm_ln(a, wg, r, ridx, xres, gate, gam, bet, bias, alpha, name):
    s_len = a.shape[0]
    d = wg.shape[-1]
    tm = _tile(s_len, 512)

    def body(a_ref, wg_ref, x_ref, g_ref, gam_ref, bet_ref, b_ref, y_ref, xo_ref, xh_ref, rs_ref, w_ref, sems):
        @pl.when(pl.program_id(0) == 0)
        def _():
            _load_rows(wg_ref, r, ridx, w_ref, sems)

        y = jnp.dot(a_ref[...], w_ref[...], preferred_element_type=F32) + b_ref[...]
        z = alpha * x_ref[...] + g_ref[...] * y
        mu = jnp.mean(z, axis=-1, keepdims=True)
        zc = z - mu
        var = jnp.mean(zc * zc, axis=-1, keepdims=True)
        rstd = lax.rsqrt(var + LN_EPS)
        xh = zc * rstd
        y_ref[...] = y.astype(BF16)
        xh_ref[...] = xh
        rs_ref[...] = rstd
        xo_ref[...] = xh * gam_ref[...] + bet_ref[...]

    row = pl.BlockSpec((tm, d), lambda i: (i, 0))
    vec = _full((1, d))
    return pl.pallas_call(
        body, name=name, grid=(s_len // tm,),
        in_specs=[pl.BlockSpec((tm, NDEV * r), lambda i: (i, 0)), ANY_SPEC, row, vec, vec, vec, vec],
        out_specs=[row, row, row, pl.BlockSpec((tm, 1), lambda i: (i, 0))],
        out_shape=[jax.ShapeDtypeStruct((s_len, d), BF16)] + [jax.ShapeDtypeStruct((s_len, d), F32)] * 2
        + [jax.ShapeDtypeStruct((s_len, 1), F32)],
        scratch_shapes=[pltpu.VMEM((NDEV * r, d), BF16), pltpu.SemaphoreType.DMA((NDEV,))],
        compiler_params=_cp("arbitrary"),
    )(a, wg, xres, gate, gam, bet, bias)


def _ln_bwd(dxo, xh, rstd, gam, y, gate, alpha, name):
    s_len, d = dxo.shape
    tm = _tile(s_len, 256)

    def body(d_ref, xh_ref, rs_ref, gam_ref, y_ref, g_ref, dy_ref, dres_ref, acc_ref):
        @pl.when(pl.program_id(0) == 0)
        def _():
            acc_ref[...] = jnp.zeros_like(acc_ref)

        dxo_t = d_ref[...]
        xh_t = xh_ref[...]
        dxh = dxo_t * gam_ref[...]
        m1 = jnp.mean(dxh, axis=-1, keepdims=True)
        m2 = jnp.mean(dxh * xh_t, axis=-1, keepdims=True)
        dz = rs_ref[...] * (dxh - m1 - xh_t * m2)
        dy = g_ref[...] * dz
        dy_ref[...] = dy.astype(BF16)
        dres_ref[...] = alpha * dz
        acc_ref[0:1, :] += jnp.sum(dxo_t * xh_t, axis=0, keepdims=True)
        acc_ref[1:2, :] += jnp.sum(dxo_t, axis=0, keepdims=True)
        acc_ref[2:3, :] += jnp.sum(dz * y_ref[...].astype(F32), axis=0, keepdims=True)
        acc_ref[3:4, :] += jnp.sum(dy, axis=0, keepdims=True)

    row = pl.BlockSpec((tm, d), lambda i: (i, 0))
    vec = _full((1, d))
    return pl.pallas_call(
        body, name=name, grid=(s_len // tm,),
        in_specs=[row, row, pl.BlockSpec((tm, 1), lambda i: (i, 0)), vec, row, vec],
        out_specs=[row, row, _full((SUBLANES, d))],
        out_shape=[jax.ShapeDtypeStruct((s_len, d), BF16), jax.ShapeDtypeStruct((s_len, d), F32),
                   jax.ShapeDtypeStruct((SUBLANES, d), F32)],
        compiler_params=_cp("arbitrary"),
    )(dxo, xh, rstd, gam, y, gate)


def _mm_nt_row(dy, wg, r, ridx, name):
    s_len, d = dy.shape
    tm = _tile(s_len, 512)

    def body(dy_ref, wg_ref, o_ref, w_ref, sems):
        @pl.when(pl.program_id(0) == 0)
        def _():
            _load_rows(wg_ref, r, ridx, w_ref, sems)

        o_ref[...] = lax.dot_general(dy_ref[...], w_ref[...], NT, preferred_element_type=F32).astype(BF16)

    return pl.pallas_call(
        body, name=name, grid=(s_len // tm,),
        in_specs=[pl.BlockSpec((tm, d), lambda i: (i, 0)), ANY_SPEC],
        out_specs=pl.BlockSpec((tm, NDEV * r), lambda i: (i, 0)),
        out_shape=jax.ShapeDtypeStruct((s_len, NDEV * r), BF16),
        scratch_shapes=[pltpu.VMEM((NDEV * r, d), BF16), pltpu.SemaphoreType.DMA((NDEV,))],
        compiler_params=_cp("arbitrary"),
    )(dy, wg)


def _mm_nt_mod(dos, wg, widxs, xin, sc, dres, name):
    s_len, kdim = xin.shape
    ncol = NDEV * wg.shape[-1]
    tm = _tile(s_len, 512)
    nw = len(widxs)

    def body(*refs):
        do_refs, wg_ref = refs[:nw], refs[nw]
        x_ref, sc_ref, dres_ref, dx_ref, acc_ref = refs[nw + 1:nw + 6]
        w_refs, sems = refs[nw + 6:2 * nw + 6], refs[2 * nw + 6]

        @pl.when(pl.program_id(0) == 0)
        def _():
            acc_ref[...] = jnp.zeros_like(acc_ref)
            for i, w_ref in enumerate(w_refs):
                _load_cols(wg_ref, widxs[i], w_ref, sems.at[i])

        dh = None
        for do_ref, w_ref in zip(do_refs, w_refs):
            p = lax.dot_general(do_ref[...], w_ref[...], NT, preferred_element_type=F32)
            dh = p if dh is None else dh + p
        dx_ref[...] = dh * (1.0 + sc_ref[...]) + dres_ref[...]
        acc_ref[0:1, :] += jnp.sum(dh * x_ref[...], axis=0, keepdims=True)
        acc_ref[1:2, :] += jnp.sum(dh, axis=0, keepdims=True)

    row = pl.BlockSpec((tm, kdim), lambda i: (i, 0))
    return pl.pallas_call(
        body, name=name, grid=(s_len // tm,),
        in_specs=[pl.BlockSpec((tm, ncol), lambda i: (i, 0))] * nw + [ANY_SPEC, row, _full((1, kdim)), row],
        out_specs=[row, _full((SUBLANES, kdim))],
        out_shape=[jax.ShapeDtypeStruct((s_len, kdim), F32), jax.ShapeDtypeStruct((SUBLANES, kdim), F32)],
        scratch_shapes=[pltpu.VMEM((kdim, ncol), BF16)] * nw + [pltpu.SemaphoreType.DMA((nw, NDEV))],
        compiler_params=_cp("arbitrary"),
    )(*dos, wg, xin, sc, dres)


def _mm_tn_col(x, sc, sh, do, name):
    s_len, kdim = x.shape
    n = do.shape[1] // NDEV
    ts = _tile(s_len, 512)
    nsteps = s_len // ts

    def body(x_ref, sc_ref, sh_ref, do_ref, o_ref, acc_ref):
        @pl.when(pl.program_id(0) == 0)
        def _():
            acc_ref[...] = jnp.zeros_like(acc_ref)

        h = (x_ref[...] * (1.0 + sc_ref[...]) + sh_ref[...]).astype(BF16)
        acc_ref[...] += lax.dot_general(h, do_ref[...], TN, preferred_element_type=F32)

        @pl.when(pl.program_id(0) == nsteps - 1)
        def _():
            for k in range(NDEV):
                o_ref[k] = acc_ref[:, k * n:(k + 1) * n].astype(BF16)

    return pl.pallas_call(
        body, name=name, grid=(nsteps,),
        in_specs=[pl.BlockSpec((ts, kdim), lambda i: (i, 0)), _full((1, kdim)), _full((1, kdim)),
                  pl.BlockSpec((ts, NDEV * n), lambda i: (i, 0))],
        out_specs=_full((NDEV, kdim, n)),
        out_shape=jax.ShapeDtypeStruct((NDEV, kdim, n), BF16),
        scratch_shapes=[pltpu.VMEM((kdim, NDEV * n), F32)],
        compiler_params=_cp("arbitrary"),
    )(x, sc, sh, do)


def _mm_tn_col_t(x, sc, sh, do, rows_out, name):
    s_len, kdim = x.shape
    n = do.shape[1] // NDEV
    ts = _tile(s_len, 512)
    nsteps = s_len // ts

    def body(x_ref, sc_ref, sh_ref, do_ref, o_ref, acc_ref):
        @pl.when(pl.program_id(0) == 0)
        def _():
            acc_ref[...] = jnp.zeros_like(acc_ref)

        h = (x_ref[...] * (1.0 + sc_ref[...]) + sh_ref[...]).astype(BF16)
        acc_ref[...] += lax.dot_general(do_ref[...], h, TN, preferred_element_type=F32)

        @pl.when(pl.program_id(0) == nsteps - 1)
        def _():
            for k in range(NDEV):
                o_ref[k] = acc_ref[k * n:k * n + rows_out, :].astype(BF16)

    return pl.pallas_call(
        body, name=name, grid=(nsteps,),
        in_specs=[pl.BlockSpec((ts, kdim), lambda i: (i, 0)), _full((1, kdim)), _full((1, kdim)),
                  pl.BlockSpec((ts, NDEV * n), lambda i: (i, 0))],
        out_specs=_full((NDEV, rows_out, kdim)),
        out_shape=jax.ShapeDtypeStruct((NDEV, rows_out, kdim), BF16),
        scratch_shapes=[pltpu.VMEM((NDEV * n, kdim), F32)],
        compiler_params=_cp("arbitrary"),
    )(x, sc, sh, do)


def _mm_tn_row(a, dy, r, rows_out, name):
    s_len, d = dy.shape
    ts = _tile(s_len, 512)
    nsteps = s_len // ts

    def body(a_ref, dy_ref, o_ref, acc_ref):
        @pl.when(pl.program_id(0) == 0)
        def _():
            acc_ref[...] = jnp.zeros_like(acc_ref)

        acc_ref[...] += lax.dot_general(a_ref[...], dy_ref[...], TN, preferred_element_type=F32)

        @pl.when(pl.program_id(0) == nsteps - 1)
        def _():
            for k in range(NDEV):
                o_ref[k] = acc_ref[k * r:k * r + rows_out, :].astype(BF16)

    return pl.pallas_call(
        body, name=name, grid=(nsteps,),
        in_specs=[pl.BlockSpec((ts, NDEV * r), lambda i: (i, 0)), pl.BlockSpec((ts, d), lambda i: (i, 0))],
        out_specs=_full((NDEV, rows_out, d)),
        out_shape=jax.ShapeDtypeStruct((NDEV, rows_out, d), BF16),
        scratch_shapes=[pltpu.VMEM((NDEV * r, d), F32)],
        compiler_params=_cp("arbitrary"),
    )(a, dy)


def _prev_spec(ts, pad, cb, col):
    return pl.BlockSpec((pad, cb), lambda *g: (jnp.maximum(g[-1] * (ts // pad) - 1, 0), col(g)))


def _next_spec(ts, pad, cb, col, s_len):
    return pl.BlockSpec((pad, cb), lambda *g: (jnp.minimum((g[-1] + 1) * (ts // pad), s_len // pad - 1), col(g)))


class _F32Loads:
    def __init__(self, ref):
        self.ref = ref

    def __getitem__(self, idx):
        return self.ref[idx].astype(F32)


def _direct(buf_ref):
    return lambda off, rows: buf_ref[off:off + rows, :]


def _make_shifts(sh_ref, nrows):
    for r in range(1, SUBLANES):
        sh_ref[r, 0:nrows - SUBLANES, :] = sh_ref[0, r:r + nrows - SUBLANES, :]


def _shifted(sh_ref):
    def read(off, rows):
        r = off % SUBLANES
        return sh_ref[r, off - r:off - r + rows, :]
    return read


def _conv_fwd_rows(read, w_ref, b_ref, ktaps, pad, r0, rows):
    acc = None
    for j in range(ktaps):
        term = w_ref[ktaps - 1 - j:ktaps - j, :] * read(pad - j + r0, rows)
        acc = term if acc is None else acc + term
    return acc + b_ref[...]


def _conv_bwd_rows(read, x_rows, w_ref, dwacc_ref, ktaps, r0, rows):
    acc = None
    for j in range(ktaps):
        sl = read(j + r0, rows)
        term = w_ref[ktaps - 1 - j:ktaps - j, :] * sl
        acc = term if acc is None else acc + term
        prod = x_rows * sl
        fold = prod[0:SUBLANES]
        for q in range(1, rows // SUBLANES):
            fold = fold + prod[q * SUBLANES:(q + 1) * SUBLANES]
        tap = ktaps - 1 - j
        dwacc_ref[tap * SUBLANES:(tap + 1) * SUBLANES, :] += fold
    return acc


def _flush_dw(dwacc_ref, dw_ref, ktaps):
    for tap in range(ktaps):
        dw_ref[tap:tap + 1, :] = jnp.sum(dwacc_ref[tap * SUBLANES:(tap + 1) * SUBLANES, :], axis=0, keepdims=True)


def _gateconv_fwd(bcv, cw, cb, name):
    s_len, d3 = bcv.shape
    d = d3 // 3
    ktaps = cw.shape[0]
    pad = SHORT_PAD
    ts = _tile(s_len, 256)

    def body(gb_ref, gc_ref, v_ref, gcp_ref, vp_ref, w_ref, b_ref, o_ref, pbuf):
        gb_ref, gc_ref, v_ref, gcp_ref, vp_ref = map(_F32Loads, (gb_ref, gc_ref, v_ref, gcp_ref, vp_ref))
        s = pl.program_id(0)
        pbuf[0:pad, :] = jnp.where(s > 0, gcp_ref[...] * vp_ref[...], 0.0)
        pbuf[pad:pad + ts, :] = gc_ref[...] * v_ref[...]
        for r0 in range(0, ts, CHUNK):
            q = _conv_fwd_rows(_direct(pbuf), w_ref, b_ref, ktaps, pad, r0, CHUNK)
            o_ref[r0:r0 + CHUNK, :] = (gb_ref[r0:r0 + CHUNK, :] * q).astype(BF16)

    def cur(part):
        return pl.BlockSpec((ts, d), lambda s: (s, part))

    return pl.pallas_call(
        body, name=name, grid=(s_len // ts,),
        in_specs=[cur(0), cur(1), cur(2),
                  _prev_spec(ts, pad, d, lambda g: 1), _prev_spec(ts, pad, d, lambda g: 2),
                  _full((ktaps, d)), _full((1, d))],
        out_specs=pl.BlockSpec((ts, d), lambda s: (s, 0)),
        out_shape=jax.ShapeDtypeStruct((s_len, d), BF16),
        scratch_shapes=[pltpu.VMEM((pad + ts, d), F32)],
        compiler_params=_cp("parallel"),
    )(bcv, bcv, bcv, bcv, bcv, cw, cb)


def _gateconv_bwd(bcv, dy0, cw, cb, name):
    s_len, d3 = bcv.shape
    d = d3 // 3
    ktaps = cw.shape[0]
    pad = SHORT_PAD
    ts = _tile(s_len, 256)
    nsteps = s_len // ts

    def body(gb_ref, gc_ref, v_ref, gcp_ref, vp_ref, gbn_ref, dy_ref, dyn_ref, w_ref, b_ref,
             o_ref, dw_ref, db_ref, pbuf, dqbuf, dwacc):
        gb_ref, gc_ref, v_ref, gcp_ref, vp_ref, gbn_ref, dy_ref, dyn_ref = map(
            _F32Loads, (gb_ref, gc_ref, v_ref, gcp_ref, vp_ref, gbn_ref, dy_ref, dyn_ref))
        s = pl.program_id(0)

        @pl.when(s == 0)
        def _():
            dwacc[...] = jnp.zeros_like(dwacc)
            db_ref[...] = jnp.zeros_like(db_ref)

        pbuf[0:pad, :] = jnp.where(s > 0, gcp_ref[...] * vp_ref[...], 0.0)
        pbuf[pad:pad + ts, :] = gc_ref[...] * v_ref[...]
        dq = dy_ref[...] * gb_ref[...]
        dqbuf[0:ts, :] = dq
        dqbuf[ts:ts + pad, :] = jnp.where(s < nsteps - 1, dyn_ref[...] * gbn_ref[...], 0.0)
        db_ref[...] += jnp.sum(dq, axis=0, keepdims=True)
        for r0 in range(0, ts, CHUNK):
            rows = slice(r0, r0 + CHUNK)
            q = _conv_fwd_rows(_direct(pbuf), w_ref, b_ref, ktaps, pad, r0, CHUNK)
            o_ref[rows, 0:d] = (dy_ref[rows, :] * q).astype(BF16)
            dp = _conv_bwd_rows(_direct(dqbuf), pbuf[pad + r0:pad + r0 + CHUNK, :], w_ref, dwacc, ktaps, r0, CHUNK)
            o_ref[rows, d:2 * d] = (dp * v_ref[rows, :]).astype(BF16)
            o_ref[rows, 2 * d:3 * d] = (dp * gc_ref[rows, :]).astype(BF16)

        @pl.when(s == nsteps - 1)
        def _():
            _flush_dw(dwacc, dw_ref, ktaps)

    def cur(part):
        return pl.BlockSpec((ts, d), lambda s: (s, part))

    return pl.pallas_call(
        body, name=name, grid=(nsteps,),
        in_specs=[cur(0), cur(1), cur(2),
                  _prev_spec(ts, pad, d, lambda g: 1), _prev_spec(ts, pad, d, lambda g: 2),
                  _next_spec(ts, pad, d, lambda g: 0, s_len),
                  cur(0), _next_spec(ts, pad, d, lambda g: 0, s_len),
                  _full((ktaps, d)), _full((1, d))],
        out_specs=[pl.BlockSpec((ts, d3), lambda s: (s, 0)), _full((ktaps, d)), _full((1, d))],
        out_shape=[jax.ShapeDtypeStruct((s_len, d3), BF16), jax.ShapeDtypeStruct((ktaps, d), F32),
                   jax.ShapeDtypeStruct((1, d), F32)],
        scratch_shapes=[pltpu.VMEM((pad + ts, d), F32), pltpu.VMEM((ts + pad, d), F32),
                        pltpu.VMEM((ktaps * SUBLANES, d), F32)],
        compiler_params=_cp("arbitrary"),
    )(bcv, bcv, bcv, bcv, bcv, bcv, dy0, dy0, cw, cb)


def _ffn_mid_fwd(u0, vg, cw, cb, name):
    s_len, f = u0.shape
    ktaps = cw.shape[0]
    pad = SHORT_PAD
    ts = _tile(s_len, 256)
    cbk = 1024 if f % 1024 == 0 else f

    def body(u_ref, up_ref, vg_ref, w_ref, b_ref, o_ref, ubuf):
        u_ref, up_ref, vg_ref = map(_F32Loads, (u_ref, up_ref, vg_ref))
        s = pl.program_id(1)
        ubuf[0:pad, :] = jnp.where(s > 0, up_ref[...], 0.0)
        ubuf[pad:pad + ts, :] = u_ref[...]
        for r0 in range(0, ts, CHUNK):
            u = _conv_fwd_rows(_direct(ubuf), w_ref, b_ref, ktaps, pad, r0, CHUNK)
            o_ref[r0:r0 + CHUNK, :] = (u * _sigmoid(u) * vg_ref[r0:r0 + CHUNK, :]).astype(BF16)

    cur = pl.BlockSpec((ts, cbk), lambda c, s: (s, c))
    return pl.pallas_call(
        body, name=name, grid=(f // cbk, s_len // ts),
        in_specs=[cur, _prev_spec(ts, pad, cbk, lambda g: g[0]), cur,
                  pl.BlockSpec((ktaps, cbk), lambda c, s: (0, c)), pl.BlockSpec((1, cbk), lambda c, s: (0, c))],
        out_specs=cur,
        out_shape=jax.ShapeDtypeStruct((s_len, f), BF16),
        scratch_shapes=[pltpu.VMEM((pad + ts, cbk), F32)],
        compiler_params=_cp("parallel", "parallel"),
    )(u0, u0, vg, cw, cb)


def _ffn_mid_bwd(u0, vg, dt, cw, cb, name):
    s_len, f = u0.shape
    ktaps = cw.shape[0]
    pad = SHORT_PAD
    ts = _tile(s_len, 256)
    nsteps = s_len // ts
    cbk = 1024 if f % 1024 == 0 else f

    def body(u_ref, up_ref, un_ref, vg_ref, vgn_ref, dt_ref, dtn_ref, w_ref, b_ref,
             du0_ref, dvg_ref, dw_ref, db_ref, ubuf, dubuf, dwacc):
        u_ref, up_ref, un_ref, vg_ref, vgn_ref, dt_ref, dtn_ref = map(
            _F32Loads, (u_ref, up_ref, un_ref, vg_ref, vgn_ref, dt_ref, dtn_ref))
        s = pl.program_id(1)

        @pl.when(s == 0)
        def _():
            dwacc[...] = jnp.zeros_like(dwacc)
            db_ref[...] = jnp.zeros_like(db_ref)

        ubuf[0:pad, :] = jnp.where(s > 0, up_ref[...], 0.0)
        ubuf[pad:pad + ts, :] = u_ref[...]
        ubuf[pad + ts:pad + ts + pad, :] = un_ref[...]
        last = s == nsteps - 1
        for r0 in range(0, ts + pad, CHUNK):
            u = _conv_fwd_rows(_direct(ubuf), w_ref, b_ref, ktaps, pad, r0, CHUNK)
            sg = _sigmoid(u)
            if r0 < ts:
                rows = slice(r0, r0 + CHUNK)
                dtr, vgr = dt_ref[rows, :], vg_ref[rows, :]
                dvg_ref[rows, :] = (dtr * u * sg).astype(BF16)
            else:
                rows = slice(r0 - ts, r0 - ts + CHUNK)
                dtr, vgr = jnp.where(last, 0.0, dtn_ref[rows, :]), vgn_ref[rows, :]
            dubuf[r0:r0 + CHUNK, :] = dtr * vgr * (sg * (1.0 + u * (1.0 - sg)))
        db_ref[...] += jnp.sum(dubuf[0:ts, :], axis=0, keepdims=True)
        for r0 in range(0, ts, CHUNK):
            du0 = _conv_bwd_rows(_direct(dubuf), u_ref[r0:r0 + CHUNK, :], w_ref, dwacc, ktaps, r0, CHUNK)
            du0_ref[r0:r0 + CHUNK, :] = du0.astype(BF16)

        @pl.when(last)
        def _():
            _flush_dw(dwacc, dw_ref, ktaps)

    cur = pl.BlockSpec((ts, cbk), lambda c, s: (s, c))
    prv = _prev_spec(ts, pad, cbk, lambda g: g[0])
    nxt = _next_spec(ts, pad, cbk, lambda g: g[0], s_len)
    return pl.pallas_call(
        body, name=name, grid=(f // cbk, nsteps),
        in_specs=[cur, prv, nxt, cur, nxt, cur, nxt,
                  pl.BlockSpec((ktaps, cbk), lambda c, s: (0, c)), pl.BlockSpec((1, cbk), lambda c, s: (0, c))],
        out_specs=[cur, cur, pl.BlockSpec((ktaps, cbk), lambda c, s: (0, c)),
                   pl.BlockSpec((1, cbk), lambda c, s: (0, c))],
        out_shape=[jax.ShapeDtypeStruct((s_len, f), BF16), jax.ShapeDtypeStruct((s_len, f), BF16),
                   jax.ShapeDtypeStruct((ktaps, f), F32), jax.ShapeDtypeStruct((1, f), F32)],
        scratch_shapes=[pltpu.VMEM((pad + ts + pad, cbk), F32), pltpu.VMEM((ts + pad, cbk), F32),
                        pltpu.VMEM((ktaps * SUBLANES, cbk), F32)],
        compiler_params=_cp("parallel", "arbitrary"),
    )(u0, u0, u0, vg, vg, dt, dt, cw, cb)


def _b_mid_fwd(ub, cw, cb, lng, lnb, name):
    s_len, d2 = ub.shape
    d = d2 // 2
    ktaps = cw.shape[0]
    pad = LONG_PAD
    ts = _tile(s_len, 256)

    def body(a_ref, g_ref, ap_ref, gp_ref, w_ref, b_ref, lng_ref, lnb_ref, a2_ref, a4_ref, abuf):
        a_ref, g_ref, ap_ref, gp_ref = map(_F32Loads, (a_ref, g_ref, ap_ref, gp_ref))
        s = pl.program_id(0)
        abuf[0, 0:pad, :] = jnp.where(s > 0, ap_ref[...] * _sigmoid(gp_ref[...]), 0.0)
        abuf[0, pad:pad + ts, :] = a_ref[...] * _sigmoid(g_ref[...])
        _make_shifts(abuf, pad + ts)
        for r0 in range(0, ts, CHUNK):
            a2_ref[r0:r0 + CHUNK, :] = _conv_fwd_rows(_shifted(abuf), w_ref, b_ref, ktaps, pad, r0, CHUNK)
        a2 = a2_ref[...]
        mu = jnp.mean(a2, axis=-1, keepdims=True)
        ac = a2 - mu
        var = jnp.mean(ac * ac, axis=-1, keepdims=True)
        a3 = ac * lax.rsqrt(var + LN_EPS) * lng_ref[...] + lnb_ref[...]
        a4_ref[...] = (a3 * _sigmoid(a3)).astype(BF16)

    def cur(part):
        return pl.BlockSpec((ts, d), lambda s: (s, part))

    vec = _full((1, d))
    return pl.pallas_call(
        body, name=name, grid=(s_len // ts,),
        in_specs=[cur(0), cur(1), _prev_spec(ts, pad, d, lambda g: 0), _prev_spec(ts, pad, d, lambda g: 1),
                  _full((ktaps, d)), vec, vec, vec],
        out_specs=[cur(0), cur(0)],
        out_shape=[jax.ShapeDtypeStruct((s_len, d), F32), jax.ShapeDtypeStruct((s_len, d), BF16)],
        scratch_shapes=[pltpu.VMEM((SUBLANES, pad + ts, d), F32)],
        compiler_params=_cp("parallel"),
    )(ub, ub, ub, ub, cw, cb, lng, lnb)


def _b_mid_bwd(ub, a2, da4, cw, lng, lnb, name):
    s_len, d2 = ub.shape
    d = d2 // 2
    ktaps = cw.shape[0]
    pad = LONG_PAD
    ts = _tile(s_len, 256)
    nsteps = s_len // ts

    def body(a_ref, g_ref, a2_ref, a2n_ref, da4_ref, da4n_ref, w_ref, lng_ref, lnb_ref,
             du_ref, dw_ref, db_ref, dlng_ref, dlnb_ref, dbias_ref, dabuf, dwacc):
        a_ref, g_ref, da4_ref, da4n_ref = map(_F32Loads, (a_ref, g_ref, da4_ref, da4n_ref))
        s = pl.program_id(0)
        last = s == nsteps - 1

        @pl.when(s == 0)
        def _():
            dwacc[...] = jnp.zeros_like(dwacc)
            for ref in (db_ref, dlng_ref, dlnb_ref, dbias_ref):
                ref[...] = jnp.zeros_like(ref)

        def ln_silu_bwd(a2_t, da4_t):
            mu = jnp.mean(a2_t, axis=-1, keepdims=True)
            ac = a2_t - mu
            var = jnp.mean(ac * ac, axis=-1, keepdims=True)
            rstd = lax.rsqrt(var + LN_EPS)
            ah = ac * rstd
            a3 = ah * lng_ref[...] + lnb_ref[...]
            sg = _sigmoid(a3)
            da3 = da4_t * (sg * (1.0 + a3 * (1.0 - sg)))
            dah = da3 * lng_ref[...]
            m1 = jnp.mean(dah, axis=-1, keepdims=True)
            m2 = jnp.mean(dah * ah, axis=-1, keepdims=True)
            return rstd * (dah - m1 - ah * m2), da3, ah

        da2, da3, ah = ln_silu_bwd(a2_ref[...], da4_ref[...])
        dabuf[0, 0:ts, :] = da2
        dlng_ref[...] += jnp.sum(da3 * ah, axis=0, keepdims=True)
        dlnb_ref[...] += jnp.sum(da3, axis=0, keepdims=True)
        db_ref[...] += jnp.sum(da2, axis=0, keepdims=True)
        da2n, _, _ = ln_silu_bwd(a2n_ref[...], jnp.where(last, 0.0, da4n_ref[...]))
        dabuf[0, ts:ts + pad, :] = da2n
        _make_shifts(dabuf, ts + pad)
        for r0 in range(0, ts, CHUNK):
            rows = slice(r0, r0 + CHUNK)
            a_r, g_r = a_ref[rows, :], g_ref[rows, :]
            sg = _sigmoid(g_r)
            da1 = _conv_bwd_rows(_shifted(dabuf), a_r * sg, w_ref, dwacc, ktaps, r0, CHUNK)
            da = da1 * sg
            dg = da1 * a_r * sg * (1.0 - sg)
            du_ref[rows, 0:d] = da.astype(BF16)
            du_ref[rows, d:2 * d] = dg.astype(BF16)
            dbias_ref[:, 0:d] += jnp.sum(da, axis=0, keepdims=True)
            dbias_ref[:, d:2 * d] += jnp.sum(dg, axis=0, keepdims=True)

        @pl.when(last)
        def _():
            _flush_dw(dwacc, dw_ref, ktaps)

    def cur(part):
        return pl.BlockSpec((ts, d), lambda s: (s, part))

    vec = _full((1, d))
    nxt = _next_spec(ts, pad, d, lambda g: 0, s_len)
    return pl.pallas_call(
        body, name=name, grid=(nsteps,),
        in_specs=[cur(0), cur(1), cur(0), nxt, cur(0), nxt, _full((ktaps, d)), vec, vec],
        out_specs=[pl.BlockSpec((ts, d2), lambda s: (s, 0)), _full((ktaps, d)), vec, vec, vec, _full((1, d2))],
        out_shape=[jax.ShapeDtypeStruct((s_len, d2), BF16), jax.ShapeDtypeStruct((ktaps, d), F32),
                   jax.ShapeDtypeStruct((1, d), F32), jax.ShapeDtypeStruct((1, d), F32),
                   jax.ShapeDtypeStruct((1, d), F32), jax.ShapeDtypeStruct((1, d2), F32)],
        scratch_shapes=[pltpu.VMEM((SUBLANES, ts + pad, d), F32), pltpu.VMEM((ktaps * SUBLANES, d), F32)],
        compiler_params=_cp("arbitrary"),
    )(ub, ub, a2, a2, da4, da4, cw, lng, lnb)


def _loss_head(xo, tgt, name):
    s_len, d = xo.shape
    tm = _tile(s_len, 512)

    def body(x_ref, t_ref, d_ref, l_ref):
        @pl.when(pl.program_id(0) == 0)
        def _():
            l_ref[...] = jnp.zeros_like(l_ref)

        e = x_ref[...] - t_ref[...]
        d_ref[...] = e * (1.0 / d)
        per_row = jnp.sum(e * e, axis=-1, keepdims=True) * (1.0 / d)
        l_ref[...] += 0.5 * jnp.sum(per_row, axis=0, keepdims=True)

    row = pl.BlockSpec((tm, d), lambda i: (i, 0))
    return pl.pallas_call(
        body, name=name, grid=(s_len // tm,),
        in_specs=[row, row], out_specs=[row, _full((1, LANES))],
        out_shape=[jax.ShapeDtypeStruct((s_len, d), F32), jax.ShapeDtypeStruct((1, LANES), F32)],
        compiler_params=_cp("arbitrary"),
    )(xo, tgt)


def _ada_fwd(c_all, ada_w, ada_b_loc, name):
    depth, d, n = ada_w.shape

    def body(c_ref, w_ref, b_ref, o_ref):
        c = c_ref[...]
        act = c * _sigmoid(c)
        o_ref[...] = jnp.dot(act, w_ref[...], preferred_element_type=F32,
                             precision=lax.Precision.HIGHEST) + b_ref[...]

    return pl.pallas_call(
        body, name=name, grid=(depth,),
        in_specs=[_full((NDEV, d)), pl.BlockSpec((None, d, n), lambda i: (i, 0, 0)),
                  pl.BlockSpec((None, 1, n), lambda i: (i, 0, 0))],
        out_specs=pl.BlockSpec((None, NDEV, n), lambda i: (i, 0, 0)),
        out_shape=jax.ShapeDtypeStruct((depth, NDEV, n), F32),
        compiler_params=_cp("parallel"),
    )(c_all, ada_w, ada_b_loc.reshape(depth, 1, n))


def _ada_bwd(c_all_t, dmod_cols, name):
    depth, _, n = dmod_cols.shape
    d = c_all_t.shape[0]

    def body(ct_ref, dm_ref, o_ref):
        ct = ct_ref[...]
        act = ct * _sigmoid(ct)
        acc = None
        for b in range(NDEV):
            term = act[:, b:b + 1] * dm_ref[b:b + 1, :]
            acc = term if acc is None else acc + term
        o_ref[...] = acc

    return pl.pallas_call(
        body, name=name, grid=(depth,),
        in_specs=[_full((d, NDEV)), pl.BlockSpec((None, NDEV, n), lambda i: (i, 0, 0))],
        out_specs=pl.BlockSpec((None, d, n), lambda i: (i, 0, 0)),
        out_shape=jax.ShapeDtypeStruct((depth, d, n), F32),
        compiler_params=_cp("parallel"),
    )(c_all_t, dmod_cols)


def _sum_parts(parts, name):
    _, rows, lanes = parts.shape

    def body(p_ref, o_ref):
        acc = p_ref[0]
        for k in range(1, NDEV):
            acc = acc + p_ref[k]
        o_ref[...] = acc

    return pl.pallas_call(
        body, name=name, in_specs=[_full(parts.shape)], out_specs=_full((rows, lanes)), grid=(1,),
        out_shape=jax.ShapeDtypeStruct((rows, lanes), F32), compiler_params=_cp("arbitrary"),
    )(parts)


def _adamw(w, glist, m, v, name):
    nl, rows, cols = w.shape
    tr = _tile(rows, 256, 2 * SUBLANES)

    def body(w_ref, *rest):
        g_refs = rest[:nl]
        m_ref, v_ref, go_ref, d_ref, mo_ref, vo_ref = rest[nl:]
        g = None
        for layer, g_ref in enumerate(g_refs):
            part = g_ref[0].astype(F32)
            for p in range(1, g_ref.shape[0]):
                part = part + g_ref[p].astype(F32)
            g = part if g is None else jnp.where(pl.program_id(0) == layer, part, g)
        m1 = ADAM_B1 * m_ref[...] + (1.0 - ADAM_B1) * g
        v1 = ADAM_B2 * v_ref[...] + (1.0 - ADAM_B2) * (g * g)
        m_hat = m1 / (1.0 - ADAM_B1 ** ADAM_STEP)
        v_hat = v1 / (1.0 - ADAM_B2 ** ADAM_STEP)
        go_ref[...] = g
        mo_ref[...] = m1
        vo_ref[...] = v1
        d_ref[...] = -ADAM_LR * (m_hat / (jnp.sqrt(v_hat) + ADAM_EPS) + ADAM_WD * w_ref[...])

    blk = pl.BlockSpec((None, tr, cols), lambda l, i: (l, i, 0))
    g_specs = [pl.BlockSpec((g.shape[0], tr, cols), lambda l, i: (0, i, 0)) for g in glist]
    return pl.pallas_call(
        body, name=name, grid=(nl, rows // tr),
        in_specs=[blk] + g_specs + [blk, blk],
        out_specs=[blk] * 4, out_shape=[jax.ShapeDtypeStruct((nl, rows, cols), F32)] * 4,
        compiler_params=_cp("parallel", "parallel"),
    )(w, *glist, m, v)


def _pack(pieces):
    flat = jnp.concatenate([p.reshape(-1) for p in pieces])
    unit = SUBLANES * LANES
    padded = -(-flat.shape[0] // unit) * unit
    return jnp.pad(flat, (0, padded - flat.shape[0])).reshape(padded // LANES, LANES)


def _unpack(packed, shapes, lead=()):
    flat = packed.reshape(lead + (-1,))
    out, off = [], 0
    for s in shapes:
        size = 1
        for dim in s:
            size *= dim
        out.append(flat[..., off:off + size].reshape(lead + tuple(s)))
        off += size
    return out


def _pad_last(a, n):
    return jnp.pad(a, [(0, 0)] * (a.ndim - 1) + [(0, n - a.shape[-1])])


def kernel(x, c, ada_w, ada_b, ln_tok_g, ln_tok_b, ln_ch_g, ln_ch_b, a_w_in, a_conv_w, a_conv_b, a_w_out, b_w_pw1, b_b_pw1, b_conv_w, b_conv_b, b_ln_g, b_ln_b, b_w_pw2, b_b_pw2, f_w_up, f_conv_w, f_conv_b, f_w_gate, f_w_down, loss_target, m_ada_w, m_ada_b, m_ln_tok_g, m_ln_tok_b, m_ln_ch_g, m_ln_ch_b, m_a_w_in, m_a_conv_w, m_a_conv_b, m_a_w_out, m_b_w_pw1, m_b_b_pw1, m_b_conv_w, m_b_conv_b, m_b_ln_g, m_b_ln_b, m_b_w_pw2, m_b_b_pw2, m_f_w_up, m_f_conv_w, m_f_conv_b, m_f_w_gate, m_f_w_down, v_ada_w, v_ada_b, v_ln_tok_g, v_ln_tok_b, v_ln_ch_g, v_ln_ch_b, v_a_w_in, v_a_conv_w, v_a_conv_b, v_a_w_out, v_b_w_pw1, v_b_b_pw1, v_b_conv_w, v_b_conv_b, v_b_ln_g, v_b_ln_b, v_b_w_pw2, v_b_b_pw2, v_f_w_up, v_f_conv_w, v_f_conv_b, v_f_w_gate, v_f_w_down):
    weights = dict(ada_w=ada_w, ada_b=ada_b, ln_tok_g=ln_tok_g, ln_tok_b=ln_tok_b, ln_ch_g=ln_ch_g, ln_ch_b=ln_ch_b, a_w_in=a_w_in, a_conv_w=a_conv_w, a_conv_b=a_conv_b, a_w_out=a_w_out, b_w_pw1=b_w_pw1, b_b_pw1=b_b_pw1, b_conv_w=b_conv_w, b_conv_b=b_conv_b, b_ln_g=b_ln_g, b_ln_b=b_ln_b, b_w_pw2=b_w_pw2, b_b_pw2=b_b_pw2, f_w_up=f_w_up, f_conv_w=f_conv_w, f_conv_b=f_conv_b, f_w_gate=f_w_gate, f_w_down=f_w_down)
    mom_m = dict(ada_w=m_ada_w, ada_b=m_ada_b, ln_tok_g=m_ln_tok_g, ln_tok_b=m_ln_tok_b, ln_ch_g=m_ln_ch_g, ln_ch_b=m_ln_ch_b, a_w_in=m_a_w_in, a_conv_w=m_a_conv_w, a_conv_b=m_a_conv_b, a_w_out=m_a_w_out, b_w_pw1=m_b_w_pw1, b_b_pw1=m_b_b_pw1, b_conv_w=m_b_conv_w, b_conv_b=m_b_conv_b, b_ln_g=m_b_ln_g, b_ln_b=m_b_ln_b, b_w_pw2=m_b_w_pw2, b_b_pw2=m_b_b_pw2, f_w_up=m_f_w_up, f_conv_w=m_f_conv_w, f_conv_b=m_f_conv_b, f_w_gate=m_f_w_gate, f_w_down=m_f_w_down)
    mom_v = dict(ada_w=v_ada_w, ada_b=v_ada_b, ln_tok_g=v_ln_tok_g, ln_tok_b=v_ln_tok_b, ln_ch_g=v_ln_ch_g, ln_ch_b=v_ln_ch_b, a_w_in=v_a_w_in, a_conv_w=v_a_conv_w, a_conv_b=v_a_conv_b, a_w_out=v_a_w_out, b_w_pw1=v_b_w_pw1, b_b_pw1=v_b_b_pw1, b_conv_w=v_b_conv_w, b_conv_b=v_b_conv_b, b_ln_g=v_b_ln_g, b_ln_b=v_b_ln_b, b_w_pw2=v_b_w_pw2, b_b_pw2=v_b_b_pw2, f_w_up=v_f_w_up, f_conv_w=v_f_conv_w, f_conv_b=v_f_conv_b, f_w_gate=v_f_w_gate, f_w_down=v_f_w_down)
    names = list(weights)

    depth, d, n_ada = ada_w.shape
    assert depth == 2 and a_w_in.shape[0] == 1 and b_w_pw1.shape[0] == 1
    s_len = x.shape[1]
    f_loc = f_w_up.shape[-1]
    f_pad = -(-f_loc // LANES) * LANES
    f_all = NDEV * f_pad
    d_loc = d // NDEV
    ka, kb, kf = a_conv_w.shape[1], b_conv_w.shape[1], f_conv_w.shape[1]
    alpha = (2.0 * depth) ** 0.25
    assert a_w_in.shape[-1] == f_pad and f_pad % d_loc == 0
    me = 4 * lax.axis_index("x") + 2 * lax.axis_index("y") + lax.axis_index("c")

    small_shapes = [(d,), (ka, d_loc), (2 * d_loc,), (kb, d_loc), (d_loc,), (d_loc,), (d_loc,), (d_loc,),
                    (depth, kf, f_pad)]
    small_loc = _pack([c[0], a_conv_w[0], b_b_pw1[0], b_conv_w[0], b_conv_b[0], b_ln_g[0], b_ln_b[0],
                       b_b_pw2[0], _pad_last(f_conv_w, f_pad)])
    g_small, g_in, _ = _gather_two_level([small_loc, a_w_in.astype(BF16)], small_loc, "gather_first")

    (c_all, acw_g, bb1_g, bcw_g, bcb_g, blg_g, blb_g, bb2_g, fcw_g) = _unpack(g_small, small_shapes, (NDEV,))
    a_cw = acw_g.transpose(1, 0, 2).reshape(ka, d)
    b_cw = bcw_g.transpose(1, 0, 2).reshape(kb, d)
    b_b1 = bb1_g.reshape(1, 2 * d)
    b_cb, b_lg, b_lb, b_b2 = (t.reshape(1, d) for t in (bcb_g, blg_g, blb_g, bb2_g))
    f_cw = fcw_g.transpose(1, 2, 0, 3).reshape(depth, kf, f_all)
    f_cb = _pad_last(f_conv_b.reshape(depth, NDEV, f_loc), f_pad).reshape(depth, 1, f_all)

    ada_b_loc = lax.dynamic_slice(ada_b, (0, me * n_ada), (depth, n_ada))
    mod_part = _ada_fwd(c_all, ada_w, ada_b_loc, "ada_fwd")
    mod_g, mod_done = _exchange([mod_part.reshape(depth * NDEV, n_ada)], "gather", "gather_mod")
    mod_all = mod_g.reshape(NDEV, depth, NDEV, n_ada).transpose(1, 2, 0, 3).reshape(depth, NDEV, 6 * d)
    mod = lax.dynamic_slice(mod_all, (0, me, 0), (depth, 1, 6 * d))[:, 0]

    gather_out = _exchange_start([_after(a_w_out[0], mod_done).astype(BF16)], "gather_chips", "gather_out_start")
    up_pad = _pad_last(_after(f_w_up, gather_out[-1]), f_pad).astype(BF16)
    gate_pad = _pad_last(f_w_gate, f_pad).astype(BF16)
    down_pad = jnp.pad(f_w_down, ((0, 0), (0, f_pad - f_loc), (0, 0))).astype(BF16)
    col_f = [jnp.stack([up_pad[i], gate_pad[i]]) for i in range(depth)]
    row_b = jnp.concatenate([down_pad[1], b_w_pw2[0].astype(BF16)], axis=0)
    ridx_pw2 = f_pad // d_loc
    gather_f0 = _exchange_start([col_f[0], down_pad[0]], "gather_chips", "gather_f0_start")

    def mod_rows(i):
        return [mod[i:i + 1, j * d:(j + 1) * d] for j in range(6)]

    zeros_d = jnp.zeros((1, d), F32)
    zeros_f = jnp.zeros((1, f_all), F32)
    x0 = x[0]

    sh_t0, sc_t0, g_t0, sh_c0, sc_c0, g_c0 = mod_rows(0)
    sh_t1, sc_t1, g_t1, sh_c1, sc_c1, g_c1 = mod_rows(1)

    sc_t0 = _after(sc_t0, gather_f0[-1])
    bcv, = _mm_fwd(x0, sc_t0, sh_t0, jnp.zeros((1, 3 * d), F32), g_in, (0,), "a_in_fwd")
    y0 = _gateconv_fwd(bcv, a_cw, a_conv_b, "a_conv_fwd")
    g_out, landed = _exchange_wait(gather_out, y0, "gather_chips", "gather_out_wait")
    g_out, _ = _exchange_wait(_exchange_start([g_out], "forward", "gather_out_fwd_start"), landed, "forward",
                              "gather_out_fwd_wait")
    y_a, x1, xh1, rs1 = _mm_ln(y0, g_out, d_loc, 0, x0, g_t0, ln_tok_g[0:1], ln_tok_b[0:1], zeros_d,
                               alpha, "a_out_ln_fwd")

    def ffn_fwd(xin, sc, sh, gate, gam, bet, g_colf, g_rowf, layer, tag, between=None):
        u0, vg = _mm_fwd(xin, sc, sh, zeros_f, g_colf, (0, 1), "f_upgate_fwd" + tag)
        t = _ffn_mid_fwd(u0, vg, f_cw[layer], f_cb[layer], "f_mid_fwd" + tag)
        if between is not None:
            gate = _after(gate, between(t))
        y, xo, xh, rs = _mm_ln(t, g_rowf, f_pad, 0, xin, gate, gam, bet, zeros_d, alpha, "f_down_ln_fwd" + tag)
        return u0, vg, t, y, xo, xh, rs

    g_colf0, g_rowf0, landed = _exchange_wait(gather_f0, x1, "gather_chips", "gather_f0_wait")
    g_colf0, g_rowf0, landed = _exchange_wait(
        _exchange_start([g_colf0, g_rowf0], "forward", "gather_f0_fwd_start"), landed, "forward", "gather_f0_fwd_wait")
    gather_1 = _exchange_start([_after(b_w_pw1, landed).astype(BF16), col_f[1], row_b], "gather_chips",
                               "gather_1_start")
    forward_1 = []

    def forward_layer1(t_0):
        *lands, landed = _exchange_wait(gather_1, t_0, "gather_chips", "gather_1_wait")
        forward_1.append(_exchange_start(lands, "forward", "gather_1_fwd_start"))
        return forward_1[0][-1]

    sc_c0 = _after(sc_c0, gather_1[-1])
    u0_0, vg_0, t_0, y_f0, x2, xh2, rs2 = ffn_fwd(x1, sc_c0, sh_c0, g_c0, ln_ch_g[0:1], ln_ch_b[0:1],
                                                  g_colf0, g_rowf0, 0, "0", between=forward_layer1)

    g_pw1, g_colf1, g_rowb, _ = _exchange_wait(forward_1[0], x2, "forward", "gather_1_fwd_wait")
    ub, = _mm_fwd(x2, sc_t1, sh_t1, b_b1, g_pw1, (0,), "b_pw1_fwd")
    a2, a4 = _b_mid_fwd(ub, b_cw, b_cb, b_lg, b_lb, "b_mid_fwd")
    y_b, x3, xh3, rs3 = _mm_ln(a4, g_rowb, d_loc, ridx_pw2, x2, g_t1, ln_tok_g[1:2], ln_tok_b[1:2], b_b2,
                               alpha, "b_pw2_ln_fwd")
    u0_1, vg_1, t_1, y_f1, x4, xh4, rs4 = ffn_fwd(x3, sc_c1, sh_c1, g_c1, ln_ch_g[1:2], ln_ch_b[1:2],
                                                  g_colf1, g_rowb, 1, "1")

    dx4, loss_part = _loss_head(x4, loss_target[0], "loss_head")
    loss = lax.psum(loss_part[0, 0], MESH_AXES)

    def ffn_bwd(dxo, xin, sc, sh, gate, gam, u0, vg, t, y, xh, rs, g_colf, g_rowf, layer, tag):
        dy, dres, acc = _ln_bwd(dxo, xh, rs, gam, y, gate, alpha, "f_ln_bwd" + tag)
        dt = _mm_nt_row(dy, g_rowf, f_pad, 0, "f_down_dx" + tag)
        dw_down = _mm_tn_row(t, dy, f_pad, f_loc, "f_down_dw" + tag)
        du0, dvg, dcw, dcb = _ffn_mid_bwd(u0, vg, dt, f_cw[layer], f_cb[layer], "f_mid_bwd" + tag)
        dw_up = _mm_tn_col_t(xin, sc, sh, du0, f_loc, "f_up_dw" + tag)
        dw_gate = _mm_tn_col_t(xin, sc, sh, dvg, f_loc, "f_gate_dw" + tag)
        scatter = _exchange_start([dw_up, dw_gate, dw_down], "scatter", "scatter_f%s_start" % tag)
        dxin, acc2 = _mm_nt_mod([du0, dvg], g_colf, (0, 1), xin, _after(sc, scatter[-1]), dres, "f_upgate_dx" + tag)
        return dxin, acc, acc2, scatter, dcw, dcb

    dx3, accf1, acc2f1, scatter_f1, dfcw1, dfcb1 = ffn_bwd(
        dx4, x3, sc_c1, sh_c1, g_c1, ln_ch_g[1:2], u0_1, vg_1, t_1, y_f1, xh4, rs4, g_colf1, g_rowb, 1, "1")

    dy, dres, accb = _ln_bwd(dx3, xh3, rs3, ln_tok_g[1:2], y_b, g_t1, alpha, "b_ln_bwd")
    da4 = _mm_nt_row(dy, g_rowb, d_loc, ridx_pw2, "b_pw2_dx")
    dw_pw2 = _mm_tn_row(a4, dy, d_loc, d_loc, "b_pw2_dw")
    du, dbcw, dbcb, dblg, dblb, dbb1 = _b_mid_bwd(ub, a2, da4, b_cw, b_lg, b_lb, "b_mid_bwd")
    dw_pw1 = _mm_tn_col(x2, sc_t1, sh_t1, du, "b_pw1_dw")
    scatter_b = _exchange_start([dw_pw1, dw_pw2], "scatter", "scatter_b_start")
    dx2, acc2b = _mm_nt_mod([du], g_pw1, (0,), x2, _after(sc_t1, scatter_b[-1]), dres, "b_pw1_dx")

    dx1, accf0, acc2f0, scatter_f0, dfcw0, dfcb0 = ffn_bwd(
        dx2, x1, sc_c0, sh_c0, g_c0, ln_ch_g[0:1], u0_0, vg_0, t_0, y_f0, xh2, rs2, g_colf0, g_rowf0, 0, "0")

    dy, dres, acca = _ln_bwd(dx1, xh1, rs1, ln_tok_g[0:1], y_a, g_t0, alpha, "a_ln_bwd")
    dy0 = _mm_nt_row(dy, g_out, d_loc, 0, "a_out_dx")
    dbcv, dacw, dacb = _gateconv_bwd(bcv, dy0, a_cw, a_conv_b, "a_conv_bwd")
    dx0, acc2a = _mm_nt_mod([dbcv], g_in, (0,), x0, sc_t0, dres, "a_in_dx")

    def dmod_row(acc2_t, acc_t, acc2_c, acc_c):
        return jnp.concatenate([acc2_t[1], acc2_t[0], acc_t[2], acc2_c[1], acc2_c[0], acc_c[2]])

    dmod = jnp.stack([dmod_row(acc2a, acca, acc2f0, accf0), dmod_row(acc2b, accb, acc2f1, accf1)])

    def unpad_f(a):
        return a.reshape(a.shape[:-1] + (NDEV, f_pad))[..., :f_loc].reshape(a.shape[:-1] + (NDEV * f_loc,))

    small_grads = [
        dmod,
        jnp.stack([acca[0], accb[0]]), jnp.stack([acca[1], accb[1]]),
        jnp.stack([accf0[0], accf1[0]]), jnp.stack([accf0[1], accf1[1]]),
        dacb,
        unpad_f(jnp.concatenate([dfcb0, dfcb1], axis=0)),
        dacw, dbb1, dbcw, dbcb, dblg, dblb, accb[3:4],
        jnp.stack([dfcw0, dfcw1]),
    ]
    small_grad_shapes = [tuple(g.shape) for g in small_grads]
    gather_small = _exchange_start([_pack(small_grads)], "gather", "gather_small_start")

    dw_in = _mm_tn_col(x0, _after(sc_t0, gather_small[-1]), sh_t0, dbcv, "a_in_dw")
    dw_out = _mm_tn_row(y0, dy, d_loc, d_loc, "a_out_dw")
    scatter_a = _exchange_start([dw_in, dw_out], "scatter", "scatter_a_start")

    grads, deltas, new_m, new_v = {}, {}, {}, {}

    def adamw(k, glist, transposed=False):
        def view(a):
            a = jnp.swapaxes(a, 1, 2) if transposed else a
            return a.reshape(len(glist), -1, a.shape[-1])

        w = view(weights[k])
        outs = _adamw(w, [g.reshape(g.shape[0], -1, w.shape[-1]) for g in glist],
                      view(mom_m[k]), view(mom_v[k]), "adamw_" + k)
        if transposed:
            outs = [jnp.swapaxes(o, 1, 2) for o in outs]
        grads[k], deltas[k], new_m[k], new_v[k] = (o.reshape(weights[k].shape) for o in outs)

    r_up1, r_gate1, r_down1, _ = _exchange_wait(scatter_f1, scatter_a[-1], "scatter", "scatter_f1_wait")
    r_pw1, r_pw2, _ = _exchange_wait(scatter_b, r_down1, "scatter", "scatter_b_wait")
    adamw("b_w_pw1", [r_pw1])
    adamw("b_w_pw2", [r_pw2])
    r_up0, r_gate0, r_down0, _ = _exchange_wait(scatter_f0, deltas["b_w_pw2"], "scatter", "scatter_f0_wait")
    adamw("f_w_up", [r_up0, r_up1], transposed=True)
    adamw("f_w_gate", [r_gate0, r_gate1], transposed=True)
    adamw("f_w_down", [r_down0, r_down1])

    sg_all, _ = _exchange_wait(gather_small, deltas["f_w_down"], "gather", "gather_small_wait")
    sg_sum = _sum_parts(sg_all, "sum_small_grads")
    (g_ada_b, g_ltg, g_ltb, g_lcg, g_lcb, g_acb, g_fcb, g_acw, g_bb1, g_bcw, g_bcb, g_blg, g_blb, g_bb2,
     g_fcw) = _unpack(sg_sum, small_grad_shapes)

    def my_cols(a, width):
        return lax.dynamic_slice_in_dim(a, me * width, width, axis=a.ndim - 1)

    g_fcw_loc = my_cols(g_fcw, f_pad)[..., :f_loc]
    small = dict(
        ada_b=g_ada_b, ln_tok_g=g_ltg, ln_tok_b=g_ltb, ln_ch_g=g_lcg, ln_ch_b=g_lcb, a_conv_b=g_acb, f_conv_b=g_fcb,
        a_conv_w=my_cols(g_acw, d_loc)[None], b_b_pw1=my_cols(g_bb1, 2 * d_loc), b_conv_w=my_cols(g_bcw, d_loc)[None],
        b_conv_b=my_cols(g_bcb, d_loc), b_ln_g=my_cols(g_blg, d_loc), b_ln_b=my_cols(g_blb, d_loc),
        b_b_pw2=my_cols(g_bb2, d_loc), f_conv_w=g_fcw_loc)

    dmod_all = sg_all.reshape(NDEV, -1)[:, :depth * 6 * d].reshape(NDEV, depth, 6 * d)
    dmod_cols = my_cols(dmod_all, n_ada).transpose(1, 0, 2)
    g_ada_w = _ada_bwd(c_all.T, dmod_cols, "ada_bwd")

    adamw("ada_w", [g_ada_w[0:1], g_ada_w[1:2]])
    for k, g in small.items():
        adamw(k, [g[None]])

    r_in, r_out, _ = _exchange_wait(scatter_a, deltas["ada_w"], "scatter", "scatter_a_wait")
    adamw("a_w_in", [r_in])
    adamw("a_w_out", [r_out])

    return (loss, dx0[None], *[grads[k] for k in names], *[deltas[k] for k in names],
            *[new_m[k] for k in names], *[new_v[k] for k in names])
```

```python
import functools

import jax
import jax.numpy as jnp
from jax import lax
from jax.experimental import pallas as pl
from jax.experimental.pallas import tpu as pltpu

NDEV = 8
MESH_AXES = ("x", "y", "c")
LANES = 128
SUBLANES = 8
VMEM_LIMIT = 56 * 1024 * 1024
LN_EPS = 1e-5
SHORT_PAD = 16
LONG_PAD = 32
CHUNK = 16
ADAM_LR, ADAM_B1, ADAM_B2, ADAM_EPS, ADAM_WD, ADAM_STEP = 0.001, 0.9, 0.999, 1e-08, 0.01, 10

F32 = jnp.float32
BF16 = jnp.bfloat16
MESH = pl.DeviceIdType.MESH
NT = (((1,), (1,)), ((), ()))
TN = (((0,), (0,)), ((), ()))


def _tile(n, target, mult=SUBLANES):
    best = None
    for t in range(mult, min(n, target) + 1, mult):
        if n % t == 0:
            best = t
    return best if best is not None else n


def _full(shape):
    nd = len(shape)
    return pl.BlockSpec(shape, lambda *_: (0,) * nd)


def _cp(*sem):
    return pltpu.CompilerParams(dimension_semantics=sem, vmem_limit_bytes=VMEM_LIMIT)


def _sigmoid(x):
    return 1.0 / (1.0 + jnp.exp(-x))


def _peer(x, y, c, d):
    return ((1 - x) if d & 4 else x, (1 - y) if d & 2 else y, (1 - c) if d & 1 else c)


def _lin(p):
    return 4 * p[0] + 2 * p[1] + p[2]


CHIP_MASKS = (2, 4, 6)
MODES_PER_ARRAY = {"gather": NDEV - 1, "scatter": NDEV - 1, "gather_chips": 1 + len(CHIP_MASKS),
                   "forward": len(CHIP_MASKS)}


def _transfers(mode):
    x, y, c = (lax.axis_index(a) for a in MESH_AXES)
    me = _lin((x, y, c))
    if mode == "forward":
        sibling = (x, y, 1 - c)
        return [(sibling, ("land", _lin(_peer(x, y, c, q))), _lin(_peer(x, y, c, q)), _lin(_peer(x, y, c, q ^ 1)))
                for q in CHIP_MASKS]
    masks = (1,) + CHIP_MASKS if mode == "gather_chips" else range(1, NDEV)
    out = []
    for d in masks:
        peer = _peer(x, y, c, d)
        source = ("block", _lin(peer)) if mode == "scatter" else ("whole", None)
        out.append((peer, source, me, _lin(peer)))
    return out


def _remote_copies(src_refs, land_refs, send_sems, recv_sems, mode):
    transfers = _transfers(mode)
    sends, recvs = [], []
    for i, land_ref in enumerate(land_refs):
        for t, (peer, (kind, slot), there, here) in enumerate(transfers):
            k = i * len(transfers) + t
            src = land_ref.at[slot] if kind == "land" else src_refs[i].at[slot] if kind == "block" else src_refs[i]
            for dst_slot, out in ((there, sends), (here, recvs)):
                out.append(pltpu.make_async_remote_copy(
                    src_ref=src, dst_ref=land_ref.at[dst_slot], send_sem=send_sems.at[k], recv_sem=recv_sems.at[k],
                    device_id=peer, device_id_type=MESH))
    return sends, recvs


def _exchange(srcs, mode, name):
    n = len(srcs)
    gather = mode == "gather"

    def body(*refs):
        src_refs, out_refs, token = refs[:n], refs[n:2 * n], refs[2 * n]
        send_sems, recv_sems, local_sems = refs[2 * n + 1:]
        me = _lin(tuple(lax.axis_index(a) for a in MESH_AXES))
        local = []
        for i in range(n):
            mine = src_refs[i] if gather else src_refs[i].at[me]
            cp = pltpu.make_async_copy(mine, out_refs[i].at[me], local_sems.at[i])
            cp.start()
            local.append(cp)
        sends, recvs = _remote_copies(src_refs, out_refs, send_sems, recv_sems, mode)
        for snd in sends:
            snd.start()
        token[...] = jnp.zeros_like(token)
        for snd, rcv in zip(sends, recvs):
            snd.wait_send()
            rcv.wait_recv()
        for cp in local:
            cp.wait()

    out_shape = [jax.ShapeDtypeStruct(((NDEV,) + s.shape) if gather else s.shape, s.dtype) for s in srcs]
    out_shape.append(jax.ShapeDtypeStruct((SUBLANES, LANES), F32))
    any_spec = pl.BlockSpec(memory_space=pl.ANY)
    return pl.pallas_call(
        body, name=name, out_shape=out_shape,
        in_specs=[any_spec] * n, out_specs=[any_spec] * n + [pl.BlockSpec(memory_space=pltpu.VMEM)],
        scratch_shapes=[pltpu.SemaphoreType.DMA((n * (NDEV - 1),)),
                        pltpu.SemaphoreType.DMA((n * (NDEV - 1),)),
                        pltpu.SemaphoreType.DMA((n,))],
    )(*srcs)


HBM_SPEC = pl.BlockSpec(memory_space=pltpu.HBM)
SEM_SPEC = pl.BlockSpec(memory_space=pltpu.SEMAPHORE)
SIDE_EFFECT = pltpu.SideEffectType.DATAFLOW_SIDE_EFFECTING


def _exchange_start(arrays, mode, name):
    me = _lin(tuple(lax.axis_index(a) for a in MESH_AXES))
    if mode == "forward":
        srcs, lands = [], list(arrays)
    else:
        srcs, lands = list(arrays), []
        for s in srcs:
            own = lax.dynamic_index_in_dim(s, me, 0, keepdims=False) if mode == "scatter" else s
            shape = s.shape if mode == "scatter" else (NDEV,) + s.shape
            lands.append(lax.dynamic_update_index_in_dim(lax.empty(shape, s.dtype), own, me, 0))
    ns, n = len(srcs), len(lands)

    def body(*refs):
        src_refs, land_refs = refs[:ns], refs[ns:ns + n]
        send_sems, recv_sems, token = refs[ns + n], refs[ns + n + 1], refs[-1]
        sends, _ = _remote_copies(src_refs, land_refs, send_sems, recv_sems, mode)
        for snd in sends:
            snd.start()
        token[...] = jnp.zeros_like(token)

    operands = [pltpu.with_memory_space_constraint(a, pltpu.HBM) for a in srcs + lands]
    nsem = n * MODES_PER_ARRAY[mode]
    return pl.pallas_call(
        body, name=name,
        out_shape=(pltpu.SemaphoreType.DMA((nsem,)), pltpu.SemaphoreType.DMA((nsem,)),
                   *[pltpu.HBM(a.shape, a.dtype) for a in operands],
                   jax.ShapeDtypeStruct((SUBLANES, LANES), F32)),
        in_specs=[HBM_SPEC] * (ns + n),
        out_specs=(SEM_SPEC, SEM_SPEC, *([HBM_SPEC] * (ns + n)), pl.BlockSpec(memory_space=pltpu.VMEM)),
        input_output_aliases={i: 2 + i for i in range(ns + n)},
        compiler_params=pltpu.CompilerParams(has_side_effects=SIDE_EFFECT),
    )(*operands)


def _exchange_wait(handle, after, mode, name):
    send_sems, recv_sems, *thru = handle[:-1]
    n = len(thru) if mode == "forward" else len(thru) // 2
    ns = len(thru) - n

    def body(*refs):
        src_refs, land_refs = refs[:ns], refs[ns:ns + n]
        sends, recvs = _remote_copies(src_refs, land_refs, refs[ns + n], refs[ns + n + 1], mode)
        for snd, rcv in zip(sends, recvs):
            snd.wait_send()
            rcv.wait_recv()
        refs[-1][...] = jnp.zeros_like(refs[-1])

    outs = pl.pallas_call(
        body, name=name,
        out_shape=(*[pltpu.HBM(a.shape, a.dtype) for a in thru], jax.ShapeDtypeStruct((SUBLANES, LANES), F32)),
        in_specs=[HBM_SPEC] * (ns + n) + [SEM_SPEC, SEM_SPEC, pl.BlockSpec(memory_space=pl.ANY)],
        out_specs=[HBM_SPEC] * (ns + n) + [pl.BlockSpec(memory_space=pltpu.VMEM)],
        input_output_aliases={i: i for i in range(ns + n)},
        compiler_params=pltpu.CompilerParams(has_side_effects=SIDE_EFFECT),
    )(*thru, send_sems, recv_sems, after)
    return outs[ns:]


def _gather_two_level(srcs, after, name):
    first = _exchange_start(srcs, "gather_chips", name + "_chips_start")
    *lands, token = _exchange_wait(first, after, "gather_chips", name + "_chips_wait")
    second = _exchange_start(lands, "forward", name + "_forward_start")
    return _exchange_wait(second, token, "forward", name + "_forward_wait")


def _after(value, token):
    return value + token[0, 0]


ANY_SPEC = pl.BlockSpec(memory_space=pl.ANY)


def _load_cols(wg_ref, widx, w_ref, sems):
    n = wg_ref.shape[-1]
    copies = [pltpu.make_async_copy(wg_ref.at[k, widx], w_ref.at[:, pl.ds(k * n, n)], sems.at[k])
              for k in range(NDEV)]
    for cp in copies:
        cp.start()
    for cp in copies:
        cp.wait()


def _load_rows(wg_ref, r, ridx, w_ref, sems):
    copies = [pltpu.make_async_copy(wg_ref.at[k, pl.ds(ridx * r, r)], w_ref.at[pl.ds(k * r, r)], sems.at[k])
              for k in range(NDEV)]
    for cp in copies:
        cp.start()
    for cp in copies:
        cp.wait()


def _mm_fwd(x, sc, sh, bias, wg, widxs, name):
    s_len, kdim = x.shape
    ncol = NDEV * wg.shape[-1]
    tm = _tile(s_len, 512)
    nw = len(widxs)

    def body(x_ref, sc_ref, sh_ref, b_ref, wg_ref, *rest):
        o_refs, w_refs, sems = rest[:nw], rest[nw:2 * nw], rest[2 * nw]

        @pl.when(pl.program_id(0) == 0)
        def _():
            for i, w_ref in enumerate(w_refs):
                _load_cols(wg_ref, widxs[i], w_ref, sems.at[i])

        h = (x_ref[...] * (1.0 + sc_ref[...]) + sh_ref[...]).astype(BF16)
        for w_ref, o_ref in zip(w_refs, o_refs):
            o_ref[...] = (jnp.dot(h, w_ref[...], preferred_element_type=F32) + b_ref[...]).astype(BF16)

    return pl.pallas_call(
        body, name=name, grid=(s_len // tm,),
        in_specs=[pl.BlockSpec((tm, kdim), lambda i: (i, 0)), _full((1, kdim)), _full((1, kdim)),
                  _full((1, ncol)), ANY_SPEC],
        out_specs=[pl.BlockSpec((tm, ncol), lambda i: (i, 0))] * nw,
        out_shape=[jax.ShapeDtypeStruct((s_len, ncol), BF16)] * nw,
        scratch_shapes=[pltpu.VMEM((kdim, ncol), BF16)] * nw + [pltpu.SemaphoreType.DMA((nw, NDEV))],
        compiler_params=_cp("arbitrary"),
    )(x, sc, sh, bias, wg)


def _mm_ln(a, wg, r, ridx, xres, gate, gam, bet, bias, alpha, name):
    s_len = a.shape[0]
    d = wg.shape[-1]
    tm = _tile(s_len, 512)

    def body(a_ref, wg_ref, x_ref, g_ref, gam_ref, bet_ref, b_ref, y_ref, xo_ref, xh_ref, rs_ref, w_ref, sems):
        @pl.when(pl.program_id(0) == 0)
        def _():
            _load_rows(wg_ref, r, ridx, w_ref, sems)

        y = jnp.dot(a_ref[...], w_ref[...], preferred_element_type=F32) + b_ref[...]
        z = alpha * x_ref[...] + g_ref[...] * y
        mu = jnp.mean(z, axis=-1, keepdims=True)
        zc = z - mu
        var = jnp.mean(zc * zc, axis=-1, keepdims=True)
        rstd = lax.rsqrt(var + LN_EPS)
        xh = zc * rstd
        y_ref[...] = y.astype(BF16)
        xh_ref[...] = xh
        rs_ref[...] = rstd
        xo_ref[...] = xh * gam_ref[...] + bet_ref[...]

    row = pl.BlockSpec((tm, d), lambda i: (i, 0))
    vec = _full((1, d))
    return pl.pallas_call(
        body, name=name, grid=(s_len // tm,),
        in_specs=[pl.BlockSpec((tm, NDEV * r), lambda i: (i, 0)), ANY_SPEC, row, vec, vec, vec, vec],
        out_specs=[row, row, row, pl.BlockSpec((tm, 1), lambda i: (i, 0))],
        out_shape=[jax.ShapeDtypeStruct((s_len, d), BF16)] + [jax.ShapeDtypeStruct((s_len, d), F32)] * 2
        + [jax.ShapeDtypeStruct((s_len, 1), F32)],
        scratch_shapes=[pltpu.VMEM((NDEV * r, d), BF16), pltpu.SemaphoreType.DMA((NDEV,))],
        compiler_params=_cp("arbitrary"),
    )(a, wg, xres, gate, gam, bet, bias)


def _ln_bwd(dxo, xh, rstd, gam, y, gate, alpha, name):
    s_len, d = dxo.shape
    tm = _tile(s_len, 256)

    def body(d_ref, xh_ref, rs_ref, gam_ref, y_ref, g_ref, dy_ref, dres_ref, acc_ref):
        @pl.when(pl.program_id(0) == 0)
        def _():
            acc_ref[...] = jnp.zeros_like(acc_ref)

        dxo_t = d_ref[...]
        xh_t = xh_ref[...]
        dxh = dxo_t * gam_ref[...]
        m1 = jnp.mean(dxh, axis=-1, keepdims=True)
        m2 = jnp.mean(dxh * xh_t, axis=-1, keepdims=True)
        dz = rs_ref[...] * (dxh - m1 - xh_t * m2)
        dy = g_ref[...] * dz
        dy_ref[...] = dy.astype(BF16)
        dres_ref[...] = alpha * dz
        acc_ref[0:1, :] += jnp.sum(dxo_t * xh_t, axis=0, keepdims=True)
        acc_ref[1:2, :] += jnp.sum(dxo_t, axis=0, keepdims=True)
        acc_ref[2:3, :] += jnp.sum(dz * y_ref[...].astype(F32), axis=0, keepdims=True)
        acc_ref[3:4, :] += jnp.sum(dy, axis=0, keepdims=True)

    row = pl.BlockSpec((tm, d), lambda i: (i, 0))
    vec = _full((1, d))
    return pl.pallas_call(
        body, name=name, grid=(s_len // tm,),
        in_specs=[row, row, pl.BlockSpec((tm, 1), lambda i: (i, 0)), vec, row, vec],
        out_specs=[row, row, _full((SUBLANES, d))],
        out_shape=[jax.ShapeDtypeStruct((s_len, d), BF16), jax.ShapeDtypeStruct((s_len, d), F32),
                   jax.ShapeDtypeStruct((SUBLANES, d), F32)],
        compiler_params=_cp("arbitrary"),
    )(dxo, xh, rstd, gam, y, gate)


def _mm_nt_row(dy, wg, r, ridx, name):
    s_len, d = dy.shape
    tm = _tile(s_len, 512)

    def body(dy_ref, wg_ref, o_ref, w_ref, sems):
        @pl.when(pl.program_id(0) == 0)
        def _():
            _load_rows(wg_ref, r, ridx, w_ref, sems)

        o_ref[...] = lax.dot_general(dy_ref[...], w_ref[...], NT, preferred_element_type=F32).astype(BF16)

    return pl.pallas_call(
        body, name=name, grid=(s_len // tm,),
        in_specs=[pl.BlockSpec((tm, d), lambda i: (i, 0)), ANY_SPEC],
        out_specs=pl.BlockSpec((tm, NDEV * r), lambda i: (i, 0)),
        out_shape=jax.ShapeDtypeStruct((s_len, NDEV * r), BF16),
        scratch_shapes=[pltpu.VMEM((NDEV * r, d), BF16), pltpu.SemaphoreType.DMA((NDEV,))],
        compiler_params=_cp("arbitrary"),
    )(dy, wg)


def _mm_nt_mod(dos, wg, widxs, xin, sc, dres, name):
    s_len, kdim = xin.shape
    ncol = NDEV * wg.shape[-1]
    tm = _tile(s_len, 512)
    nw = len(widxs)

    def body(*refs):
        do_refs, wg_ref = refs[:nw], refs[nw]
        x_ref, sc_ref, dres_ref, dx_ref, acc_ref = refs[nw + 1:nw + 6]
        w_refs, sems = refs[nw + 6:2 * nw + 6], refs[2 * nw + 6]

        @pl.when(pl.program_id(0) == 0)
        def _():
            acc_ref[...] = jnp.zeros_like(acc_ref)
            for i, w_ref in enumerate(w_refs):
                _load_cols(wg_ref, widxs[i], w_ref, sems.at[i])

        dh = None
        for do_ref, w_ref in zip(do_refs, w_refs):
            p = lax.dot_general(do_ref[...], w_ref[...], NT, preferred_element_type=F32)
            dh = p if dh is None else dh + p
        dx_ref[...] = dh * (1.0 + sc_ref[...]) + dres_ref[...]
        acc_ref[0:1, :] += jnp.sum(dh * x_ref[...], axis=0, keepdims=True)
        acc_ref[1:2, :] += jnp.sum(dh, axis=0, keepdims=True)

    row = pl.BlockSpec((tm, kdim), lambda i: (i, 0))
    return pl.pallas_call(
        body, name=name, grid=(s_len // tm,),
        in_specs=[pl.BlockSpec((tm, ncol), lambda i: (i, 0))] * nw + [ANY_SPEC, row, _full((1, kdim)), row],
        out_specs=[row, _full((SUBLANES, kdim))],
        out_shape=[jax.ShapeDtypeStruct((s_len, kdim), F32), jax.ShapeDtypeStruct((SUBLANES, kdim), F32)],
        scratch_shapes=[pltpu.VMEM((kdim, ncol), BF16)] * nw + [pltpu.SemaphoreType.DMA((nw, NDEV))],
        compiler_params=_cp("arbitrary"),
    )(*dos, wg, xin, sc, dres)


def _mm_tn_col(x, sc, sh, do, name):
    s_len, kdim = x.shape
    n = do.shape[1] // NDEV
    ts = _tile(s_len, 512)
    nsteps = s_len // ts

    def body(x_ref, sc_ref, sh_ref, do_ref, o_ref, acc_ref):
        @pl.when(pl.program_id(0) == 0)
        def _():
            acc_ref[...] = jnp.zeros_like(acc_ref)

        h = (x_ref[...] * (1.0 + sc_ref[...]) + sh_ref[...]).astype(BF16)
        acc_ref[...] += lax.dot_general(h, do_ref[...], TN, preferred_element_type=F32)

        @pl.when(pl.program_id(0) == nsteps - 1)
        def _():
            for k in range(NDEV):
                o_ref[k] = acc_ref[:, k * n:(k + 1) * n].astype(BF16)

    return pl.pallas_call(
        body, name=name, grid=(nsteps,),
        in_specs=[pl.BlockSpec((ts, kdim), lambda i: (i, 0)), _full((1, kdim)), _full((1, kdim)),
                  pl.BlockSpec((ts, NDEV * n), lambda i: (i, 0))],
        out_specs=_full((NDEV, kdim, n)),
        out_shape=jax.ShapeDtypeStruct((NDEV, kdim, n), BF16),
        scratch_shapes=[pltpu.VMEM((kdim, NDEV * n), F32)],
        compiler_params=_cp("arbitrary"),
    )(x, sc, sh, do)


def _mm_tn_col_t(x, sc, sh, do, rows_out, name):
    s_len, kdim = x.shape
    n = do.shape[1] // NDEV
    ts = _tile(s_len, 512)
    nsteps = s_len // ts

    def body(x_ref, sc_ref, sh_ref, do_ref, o_ref, acc_ref):
        @pl.when(pl.program_id(0) == 0)
        def _():
            acc_ref[...] = jnp.zeros_like(acc_ref)

        h = (x_ref[...] * (1.0 + sc_ref[...]) + sh_ref[...]).astype(BF16)
        acc_ref[...] += lax.dot_general(do_ref[...], h, TN, preferred_element_type=F32)

        @pl.when(pl.program_id(0) == nsteps - 1)
        def _():
            for k in range(NDEV):
                o_ref[k] = acc_ref[k * n:k * n + rows_out, :].astype(BF16)

    return pl.pallas_call(
        body, name=name, grid=(nsteps,),
        in_specs=[pl.BlockSpec((ts, kdim), lambda i: (i, 0)), _full((1, kdim)), _full((1, kdim)),
                  pl.BlockSpec((ts, NDEV * n), lambda i: (i, 0))],
        out_specs=_full((NDEV, rows_out, kdim)),
        out_shape=jax.ShapeDtypeStruct((NDEV, rows_out, kdim), BF16),
        scratch_shapes=[pltpu.VMEM((NDEV * n, kdim), F32)],
        compiler_params=_cp("arbitrary"),
    )(x, sc, sh, do)


def _mm_tn_row(a, dy, r, rows_out, name):
    s_len, d = dy.shape
    ts = _tile(s_len, 512)
    nsteps = s_len // ts

    def body(a_ref, dy_ref, o_ref, acc_ref):
        @pl.when(pl.program_id(0) == 0)
        def _():
            acc_ref[...] = jnp.zeros_like(acc_ref)

        acc_ref[...] += lax.dot_general(a_ref[...], dy_ref[...], TN, preferred_element_type=F32)

        @pl.when(pl.program_id(0) == nsteps - 1)
        def _():
            for k in range(NDEV):
                o_ref[k] = acc_ref[k * r:k * r + rows_out, :].astype(BF16)

    return pl.pallas_call(
        body, name=name, grid=(nsteps,),
        in_specs=[pl.BlockSpec((ts, NDEV * r), lambda i: (i, 0)), pl.BlockSpec((ts, d), lambda i: (i, 0))],
        out_specs=_full((NDEV, rows_out, d)),
        out_shape=jax.ShapeDtypeStruct((NDEV, rows_out, d), BF16),
        scratch_shapes=[pltpu.VMEM((NDEV * r, d), F32)],
        compiler_params=_cp("arbitrary"),
    )(a, dy)


def _prev_spec(ts, pad, cb, col):
    return pl.BlockSpec((pad, cb), lambda *g: (jnp.maximum(g[-1] * (ts // pad) - 1, 0), col(g)))


def _next_spec(ts, pad, cb, col, s_len):
    return pl.BlockSpec((pad, cb), lambda *g: (jnp.minimum((g[-1] + 1) * (ts // pad), s_len // pad - 1), col(g)))


class _F32Loads:
    def __init__(self, ref):
        self.ref = ref

    def __getitem__(self, idx):
        return self.ref[idx].astype(F32)


def _direct(buf_ref):
    return lambda off, rows: buf_ref[off:off + rows, :]


def _make_shifts(sh_ref, nrows):
    for r in range(1, SUBLANES):
        sh_ref[r, 0:nrows - SUBLANES, :] = sh_ref[0, r:r + nrows - SUBLANES, :]


def _shifted(sh_ref):
    def read(off, rows):
        r = off % SUBLANES
        return sh_ref[r, off - r:off - r + rows, :]
    return read


def _conv_fwd_rows(read, w_ref, b_ref, ktaps, pad, r0, rows):
    acc = None
    for j in range(ktaps):
        term = w_ref[ktaps - 1 - j:ktaps - j, :] * read(pad - j + r0, rows)
        acc = term if acc is None else acc + term
    return acc + b_ref[...]


def _conv_bwd_rows(read, x_rows, w_ref, dwacc_ref, ktaps, r0, rows):
    acc = None
    for j in range(ktaps):
        sl = read(j + r0, rows)
        term = w_ref[ktaps - 1 - j:ktaps - j, :] * sl
        acc = term if acc is None else acc + term
        prod = x_rows * sl
        fold = prod[0:SUBLANES]
        for q in range(1, rows // SUBLANES):
            fold = fold + prod[q * SUBLANES:(q + 1) * SUBLANES]
        tap = ktaps - 1 - j
        dwacc_ref[tap * SUBLANES:(tap + 1) * SUBLANES, :] += fold
    return acc


def _flush_dw(dwacc_ref, dw_ref, ktaps):
    for tap in range(ktaps):
        dw_ref[tap:tap + 1, :] = jnp.sum(dwacc_ref[tap * SUBLANES:(tap + 1) * SUBLANES, :], axis=0, keepdims=True)


def _gateconv_fwd(bcv, cw, cb, name):
    s_len, d3 = bcv.shape
    d = d3 // 3
    ktaps = cw.shape[0]
    pad = SHORT_PAD
    ts = _tile(s_len, 256)

    def body(gb_ref, gc_ref, v_ref, gcp_ref, vp_ref, w_ref, b_ref, o_ref, pbuf):
        gb_ref, gc_ref, v_ref, gcp_ref, vp_ref = map(_F32Loads, (gb_ref, gc_ref, v_ref, gcp_ref, vp_ref))
        s = pl.program_id(0)
        pbuf[0:pad, :] = jnp.where(s > 0, gcp_ref[...] * vp_ref[...], 0.0)
        pbuf[pad:pad + ts, :] = gc_ref[...] * v_ref[...]
        for r0 in range(0, ts, CHUNK):
            q = _conv_fwd_rows(_direct(pbuf), w_ref, b_ref, ktaps, pad, r0, CHUNK)
            o_ref[r0:r0 + CHUNK, :] = (gb_ref[r0:r0 + CHUNK, :] * q).astype(BF16)

    def cur(part):
        return pl.BlockSpec((ts, d), lambda s: (s, part))

    return pl.pallas_call(
        body, name=name, grid=(s_len // ts,),
        in_specs=[cur(0), cur(1), cur(2),
                  _prev_spec(ts, pad, d, lambda g: 1), _prev_spec(ts, pad, d, lambda g: 2),
                  _full((ktaps, d)), _full((1, d))],
        out_specs=pl.BlockSpec((ts, d), lambda s: (s, 0)),
        out_shape=jax.ShapeDtypeStruct((s_len, d), BF16),
        scratch_shapes=[pltpu.VMEM((pad + ts, d), F32)],
        compiler_params=_cp("parallel"),
    )(bcv, bcv, bcv, bcv, bcv, cw, cb)


def _gateconv_bwd(bcv, dy0, cw, cb, name):
    s_len, d3 = bcv.shape
    d = d3 // 3
    ktaps = cw.shape[0]
    pad = SHORT_PAD
    ts = _tile(s_len, 256)
    nsteps = s_len // ts

    def body(gb_ref, gc_ref, v_ref, gcp_ref, vp_ref, gbn_ref, dy_ref, dyn_ref, w_ref, b_ref,
             o_ref, dw_ref, db_ref, pbuf, dqbuf, dwacc):
        gb_ref, gc_ref, v_ref, gcp_ref, vp_ref, gbn_ref, dy_ref, dyn_ref = map(
            _F32Loads, (gb_ref, gc_ref, v_ref, gcp_ref, vp_ref, gbn_ref, dy_ref, dyn_ref))
        s = pl.program_id(0)

        @pl.when(s == 0)
        def _():
            dwacc[...] = jnp.zeros_like(dwacc)
            db_ref[...] = jnp.zeros_like(db_ref)

        pbuf[0:pad, :] = jnp.where(s > 0, gcp_ref[...] * vp_ref[...], 0.0)
        pbuf[pad:pad + ts, :] = gc_ref[...] * v_ref[...]
        dq = dy_ref[...] * gb_ref[...]
        dqbuf[0:ts, :] = dq
        dqbuf[ts:ts + pad, :] = jnp.where(s < nsteps - 1, dyn_ref[...] * gbn_ref[...], 0.0)
        db_ref[...] += jnp.sum(dq, axis=0, keepdims=True)
        for r0 in range(0, ts, CHUNK):
            rows = slice(r0, r0 + CHUNK)
            q = _conv_fwd_rows(_direct(pbuf), w_ref, b_ref, ktaps, pad, r0, CHUNK)
            o_ref[rows, 0:d] = (dy_ref[rows, :] * q).astype(BF16)
            dp = _conv_bwd_rows(_direct(dqbuf), pbuf[pad + r0:pad + r0 + CHUNK, :], w_ref, dwacc, ktaps, r0, CHUNK)
            o_ref[rows, d:2 * d] = (dp * v_ref[rows, :]).astype(BF16)
            o_ref[rows, 2 * d:3 * d] = (dp * gc_ref[rows, :]).astype(BF16)

        @pl.when(s == nsteps - 1)
        def _():
            _flush_dw(dwacc, dw_ref, ktaps)

    def cur(part):
        return pl.BlockSpec((ts, d), lambda s: (s, part))

    return pl.pallas_call(
        body, name=name, grid=(nsteps,),
        in_specs=[cur(0), cur(1), cur(2),
                  _prev_spec(ts, pad, d, lambda g: 1), _prev_spec(ts, pad, d, lambda g: 2),
                  _next_spec(ts, pad, d, lambda g: 0, s_len),
                  cur(0), _next_spec(ts, pad, d, lambda g: 0, s_len),
                  _full((ktaps, d)), _full((1, d))],
        out_specs=[pl.BlockSpec((ts, d3), lambda s: (s, 0)), _full((ktaps, d)), _full((1, d))],
        out_shape=[jax.ShapeDtypeStruct((s_len, d3), BF16), jax.ShapeDtypeStruct((ktaps, d), F32),
                   jax.ShapeDtypeStruct((1, d), F32)],
        scratch_shapes=[pltpu.VMEM((pad + ts, d), F32), pltpu.VMEM((ts + pad, d), F32),
                        pltpu.VMEM((ktaps * SUBLANES, d), F32)],
        compiler_params=_cp("arbitrary"),
    )(bcv, bcv, bcv, bcv, bcv, bcv, dy0, dy0, cw, cb)


def _ffn_mid_fwd(u0, vg, cw, cb, name):
    s_len, f = u0.shape
    ktaps = cw.shape[0]
    pad = SHORT_PAD
    ts = _tile(s_len, 256)
    cbk = 1024 if f % 1024 == 0 else f

    def body(u_ref, up_ref, vg_ref, w_ref, b_ref, o_ref, ubuf):
        u_ref, up_ref, vg_ref = map(_F32Loads, (u_ref, up_ref, vg_ref))
        s = pl.program_id(1)
        ubuf[0:pad, :] = jnp.where(s > 0, up_ref[...], 0.0)
        ubuf[pad:pad + ts, :] = u_ref[...]
        for r0 in range(0, ts, CHUNK):
            u = _conv_fwd_rows(_direct(ubuf), w_ref, b_ref, ktaps, pad, r0, CHUNK)
            o_ref[r0:r0 + CHUNK, :] = (u * _sigmoid(u) * vg_ref[r0:r0 + CHUNK, :]).astype(BF16)

    cur = pl.BlockSpec((ts, cbk), lambda c, s: (s, c))
    return pl.pallas_call(
        body, name=name, grid=(f // cbk, s_len // ts),
        in_specs=[cur, _prev_spec(ts, pad, cbk, lambda g: g[0]), cur,
                  pl.BlockSpec((ktaps, cbk), lambda c, s: (0, c)), pl.BlockSpec((1, cbk), lambda c, s: (0, c))],
        out_specs=cur,
        out_shape=jax.ShapeDtypeStruct((s_len, f), BF16),
        scratch_shapes=[pltpu.VMEM((pad + ts, cbk), F32)],
        compiler_params=_cp("parallel", "parallel"),
    )(u0, u0, vg, cw, cb)


class _Cols:
    def __init__(self, ref, cols):
        self.ref, self.cols = ref, cols

    def __getitem__(self, idx):
        return self.ref[slice(None) if idx is Ellipsis else idx[0], self.cols]


def _ffn_tail_fwd(u0, vg, cw, cb, wg, xres, gate, gam, bet, alpha, name):
    s_len, f = u0.shape
    d = wg.shape[-1]
    r = f // NDEV
    ktaps = cw.shape[0]
    pad = SHORT_PAD
    tm = _tile(s_len, 256)
    cbk = 1024 if f % 1024 == 0 else f

    def body(u_ref, up_ref, vg_ref, cw_ref, cb_ref, wg_ref, x_ref, g_ref, gam_ref, bet_ref,
             t_ref, y_ref, xo_ref, xh_ref, rs_ref, ubuf, w_ref, sems):
        u_ref, up_ref, vg_ref = map(_F32Loads, (u_ref, up_ref, vg_ref))
        s = pl.program_id(0)

        @pl.when(s == 0)
        def _():
            _load_rows(wg_ref, r, 0, w_ref, sems)

        ubuf[0:pad, :] = jnp.where(s > 0, up_ref[...], 0.0)
        ubuf[pad:pad + tm, :] = u_ref[...]
        y = None
        for c0 in range(0, f, cbk):
            cols = slice(c0, c0 + cbk)
            read = _direct(_Cols(ubuf, cols))
            for r0 in range(0, tm, CHUNK):
                rows = slice(r0, r0 + CHUNK)
                u = _conv_fwd_rows(read, _Cols(cw_ref, cols), _Cols(cb_ref, cols), ktaps, pad, r0, CHUNK)
                t_ref[rows, cols] = (u * _sigmoid(u) * vg_ref[rows, cols]).astype(BF16)
            p = jnp.dot(t_ref[:, cols], w_ref[cols, :], preferred_element_type=F32)
            y = p if y is None else y + p
        z = alpha * x_ref[...] + g_ref[...] * y
        mu = jnp.mean(z, axis=-1, keepdims=True)
        zc = z - mu
        var = jnp.mean(zc * zc, axis=-1, keepdims=True)
        rstd = lax.rsqrt(var + LN_EPS)
        xh = zc * rstd
        y_ref[...] = y.astype(BF16)
        xh_ref[...] = xh
        rs_ref[...] = rstd
        xo_ref[...] = xh * gam_ref[...] + bet_ref[...]

    wide = pl.BlockSpec((tm, f), lambda i: (i, 0))
    row = pl.BlockSpec((tm, d), lambda i: (i, 0))
    vec = _full((1, d))
    return pl.pallas_call(
        body, name=name, grid=(s_len // tm,),
        in_specs=[wide, _prev_spec(tm, pad, f, lambda g: 0), wide, _full((ktaps, f)), _full((1, f)), ANY_SPEC,
                  row, vec, vec, vec],
        out_specs=[wide, row, row, row, pl.BlockSpec((tm, 1), lambda i: (i, 0))],
        out_shape=[jax.ShapeDtypeStruct((s_len, f), BF16), jax.ShapeDtypeStruct((s_len, d), BF16),
                   jax.ShapeDtypeStruct((s_len, d), F32), jax.ShapeDtypeStruct((s_len, d), F32),
                   jax.ShapeDtypeStruct((s_len, 1), F32)],
        scratch_shapes=[pltpu.VMEM((pad + tm, f), F32), pltpu.VMEM((f, d), BF16), pltpu.SemaphoreType.DMA((NDEV,))],
        compiler_params=_cp("arbitrary"),
    )(u0, u0, vg, cw, cb, wg, xres, gate, gam, bet)


def _ffn_mid_bwd(u0, vg, dt, cw, cb, name):
    s_len, f = u0.shape
    ktaps = cw.shape[0]
    pad = SHORT_PAD
    ts = _tile(s_len, 256)
    nsteps = s_len // ts
    cbk = 1024 if f % 1024 == 0 else f

    def body(u_ref, up_ref, un_ref, vg_ref, vgn_ref, dt_ref, dtn_ref, w_ref, b_ref,
             du0_ref, dvg_ref, dw_ref, db_ref, ubuf, dubuf, dwacc):
        u_ref, up_ref, un_ref, vg_ref, vgn_ref, dt_ref, dtn_ref = map(
            _F32Loads, (u_ref, up_ref, un_ref, vg_ref, vgn_ref, dt_ref, dtn_ref))
        s = pl.program_id(1)

        @pl.when(s == 0)
        def _():
            dwacc[...] = jnp.zeros_like(dwacc)
            db_ref[...] = jnp.zeros_like(db_ref)

        ubuf[0:pad, :] = jnp.where(s > 0, up_ref[...], 0.0)
        ubuf[pad:pad + ts, :] = u_ref[...]
        ubuf[pad + ts:pad + ts + pad, :] = un_ref[...]
        last = s == nsteps - 1
        for r0 in range(0, ts + pad, CHUNK):
            u = _conv_fwd_rows(_direct(ubuf), w_ref, b_ref, ktaps, pad, r0, CHUNK)
            sg = _sigmoid(u)
            if r0 < ts:
                rows = slice(r0, r0 + CHUNK)
                dtr, vgr = dt_ref[rows, :], vg_ref[rows, :]
                dvg_ref[rows, :] = (dtr * u * sg).astype(BF16)
            else:
                rows = slice(r0 - ts, r0 - ts + CHUNK)
                dtr, vgr = jnp.where(last, 0.0, dtn_ref[rows, :]), vgn_ref[rows, :]
            dubuf[r0:r0 + CHUNK, :] = dtr * vgr * (sg * (1.0 + u * (1.0 - sg)))
        db_ref[...] += jnp.sum(dubuf[0:ts, :], axis=0, keepdims=True)
        for r0 in range(0, ts, CHUNK):
            du0 = _conv_bwd_rows(_direct(dubuf), u_ref[r0:r0 + CHUNK, :], w_ref, dwacc, ktaps, r0, CHUNK)
            du0_ref[r0:r0 + CHUNK, :] = du0.astype(BF16)

        @pl.when(last)
        def _():
            _flush_dw(dwacc, dw_ref, ktaps)

    cur = pl.BlockSpec((ts, cbk), lambda c, s: (s, c))
    prv = _prev_spec(ts, pad, cbk, lambda g: g[0])
    nxt = _next_spec(ts, pad, cbk, lambda g: g[0], s_len)
    return pl.pallas_call(
        body, name=name, grid=(f // cbk, nsteps),
        in_specs=[cur, prv, nxt, cur, nxt, cur, nxt,
                  pl.BlockSpec((ktaps, cbk), lambda c, s: (0, c)), pl.BlockSpec((1, cbk), lambda c, s: (0, c))],
        out_specs=[cur, cur, pl.BlockSpec((ktaps, cbk), lambda c, s: (0, c)),
                   pl.BlockSpec((1, cbk), lambda c, s: (0, c))],
        out_shape=[jax.ShapeDtypeStruct((s_len, f), BF16), jax.ShapeDtypeStruct((s_len, f), BF16),
                   jax.ShapeDtypeStruct((ktaps, f), F32), jax.ShapeDtypeStruct((1, f), F32)],
        scratch_shapes=[pltpu.VMEM((pad + ts + pad, cbk), F32), pltpu.VMEM((ts + pad, cbk), F32),
                        pltpu.VMEM((ktaps * SUBLANES, cbk), F32)],
        compiler_params=_cp("parallel", "arbitrary"),
    )(u0, u0, u0, vg, vg, dt, dt, cw, cb)


def _b_mid_fwd(ub, cw, cb, lng, lnb, name):
    s_len, d2 = ub.shape
    d = d2 // 2
    ktaps = cw.shape[0]
    pad = LONG_PAD
    ts = _tile(s_len, 256)

    def body(a_ref, g_ref, ap_ref, gp_ref, w_ref, b_ref, lng_ref, lnb_ref, a2_ref, a4_ref, abuf):
        a_ref, g_ref, ap_ref, gp_ref = map(_F32Loads, (a_ref, g_ref, ap_ref, gp_ref))
        s = pl.program_id(0)
        abuf[0, 0:pad, :] = jnp.where(s > 0, ap_ref[...] * _sigmoid(gp_ref[...]), 0.0)
        abuf[0, pad:pad + ts, :] = a_ref[...] * _sigmoid(g_ref[...])
        _make_shifts(abuf, pad + ts)
        for r0 in range(0, ts, CHUNK):
            a2_ref[r0:r0 + CHUNK, :] = _conv_fwd_rows(_shifted(abuf), w_ref, b_ref, ktaps, pad, r0, CHUNK)
        a2 = a2_ref[...]
        mu = jnp.mean(a2, axis=-1, keepdims=True)
        ac = a2 - mu
        var = jnp.mean(ac * ac, axis=-1, keepdims=True)
        a3 = ac * lax.rsqrt(var + LN_EPS) * lng_ref[...] + lnb_ref[...]
        a4_ref[...] = (a3 * _sigmoid(a3)).astype(BF16)

    def cur(part):
        return pl.BlockSpec((ts, d), lambda s: (s, part))

    vec = _full((1, d))
    return pl.pallas_call(
        body, name=name, grid=(s_len // ts,),
        in_specs=[cur(0), cur(1), _prev_spec(ts, pad, d, lambda g: 0), _prev_spec(ts, pad, d, lambda g: 1),
                  _full((ktaps, d)), vec, vec, vec],
        out_specs=[cur(0), cur(0)],
        out_shape=[jax.ShapeDtypeStruct((s_len, d), F32), jax.ShapeDtypeStruct((s_len, d), BF16)],
        scratch_shapes=[pltpu.VMEM((SUBLANES, pad + ts, d), F32)],
        compiler_params=_cp("parallel"),
    )(ub, ub, ub, ub, cw, cb, lng, lnb)


def _b_mid_bwd(ub, a2, da4, cw, lng, lnb, name):
    s_len, d2 = ub.shape
    d = d2 // 2
    ktaps = cw.shape[0]
    pad = LONG_PAD
    ts = _tile(s_len, 256)
    nsteps = s_len // ts

    def body(a_ref, g_ref, a2_ref, a2n_ref, da4_ref, da4n_ref, w_ref, lng_ref, lnb_ref,
             du_ref, dw_ref, db_ref, dlng_ref, dlnb_ref, dbias_ref, dabuf, dwacc):
        a_ref, g_ref, da4_ref, da4n_ref = map(_F32Loads, (a_ref, g_ref, da4_ref, da4n_ref))
        s = pl.program_id(0)
        last = s == nsteps - 1

        @pl.when(s == 0)
        def _():
            dwacc[...] = jnp.zeros_like(dwacc)
            for ref in (db_ref, dlng_ref, dlnb_ref, dbias_ref):
                ref[...] = jnp.zeros_like(ref)

        def ln_silu_bwd(a2_t, da4_t):
            mu = jnp.mean(a2_t, axis=-1, keepdims=True)
            ac = a2_t - mu
            var = jnp.mean(ac * ac, axis=-1, keepdims=True)
            rstd = lax.rsqrt(var + LN_EPS)
            ah = ac * rstd
            a3 = ah * lng_ref[...] + lnb_ref[...]
            sg = _sigmoid(a3)
            da3 = da4_t * (sg * (1.0 + a3 * (1.0 - sg)))
            dah = da3 * lng_ref[...]
            m1 = jnp.mean(dah, axis=-1, keepdims=True)
            m2 = jnp.mean(dah * ah, axis=-1, keepdims=True)
            return rstd * (dah - m1 - ah * m2), da3, ah

        da2, da3, ah = ln_silu_bwd(a2_ref[...], da4_ref[...])
        dabuf[0, 0:ts, :] = da2
        dlng_ref[...] += jnp.sum(da3 * ah, axis=0, keepdims=True)
        dlnb_ref[...] += jnp.sum(da3, axis=0, keepdims=True)
        db_ref[...] += jnp.sum(da2, axis=0, keepdims=True)
        da2n, _, _ = ln_silu_bwd(a2n_ref[...], jnp.where(last, 0.0, da4n_ref[...]))
        dabuf[0, ts:ts + pad, :] = da2n
        _make_shifts(dabuf, ts + pad)
        for r0 in range(0, ts, CHUNK):
            rows = slice(r0, r0 + CHUNK)
            a_r, g_r = a_ref[rows, :], g_ref[rows, :]
            sg = _sigmoid(g_r)
            da1 = _conv_bwd_rows(_shifted(dabuf), a_r * sg, w_ref, dwacc, ktaps, r0, CHUNK)
            da = da1 * sg
            dg = da1 * a_r * sg * (1.0 - sg)
            du_ref[rows, 0:d] = da.astype(BF16)
            du_ref[rows, d:2 * d] = dg.astype(BF16)
            dbias_ref[:, 0:d] += jnp.sum(da, axis=0, keepdims=True)
            dbias_ref[:, d:2 * d] += jnp.sum(dg, axis=0, keepdims=True)

        @pl.when(last)
        def _():
            _flush_dw(dwacc, dw_ref, ktaps)

    def cur(part):
        return pl.BlockSpec((ts, d), lambda s: (s, part))

    vec = _full((1, d))
    nxt = _next_spec(ts, pad, d, lambda g: 0, s_len)
    return pl.pallas_call(
        body, name=name, grid=(nsteps,),
        in_specs=[cur(0), cur(1), cur(0), nxt, cur(0), nxt, _full((ktaps, d)), vec, vec],
        out_specs=[pl.BlockSpec((ts, d2), lambda s: (s, 0)), _full((ktaps, d)), vec, vec, vec, _full((1, d2))],
        out_shape=[jax.ShapeDtypeStruct((s_len, d2), BF16), jax.ShapeDtypeStruct((ktaps, d), F32),
                   jax.ShapeDtypeStruct((1, d), F32), jax.ShapeDtypeStruct((1, d), F32),
                   jax.ShapeDtypeStruct((1, d), F32), jax.ShapeDtypeStruct((1, d2), F32)],
        scratch_shapes=[pltpu.VMEM((SUBLANES, ts + pad, d), F32), pltpu.VMEM((ktaps * SUBLANES, d), F32)],
        compiler_params=_cp("arbitrary"),
    )(ub, ub, a2, a2, da4, da4, cw, lng, lnb)


def _loss_head(xo, tgt, name):
    s_len, d = xo.shape
    tm = _tile(s_len, 512)

    def body(x_ref, t_ref, d_ref, l_ref):
        @pl.when(pl.program_id(0) == 0)
        def _():
            l_ref[...] = jnp.zeros_like(l_ref)

        e = x_ref[...] - t_ref[...]
        d_ref[...] = e * (1.0 / d)
        per_row = jnp.sum(e * e, axis=-1, keepdims=True) * (1.0 / d)
        l_ref[...] += 0.5 * jnp.sum(per_row, axis=0, keepdims=True)

    row = pl.BlockSpec((tm, d), lambda i: (i, 0))
    return pl.pallas_call(
        body, name=name, grid=(s_len // tm,),
        in_specs=[row, row], out_specs=[row, _full((1, LANES))],
        out_shape=[jax.ShapeDtypeStruct((s_len, d), F32), jax.ShapeDtypeStruct((1, LANES), F32)],
        compiler_params=_cp("arbitrary"),
    )(xo, tgt)


def _ada_fwd(c_all, ada_w, ada_b_loc, name):
    depth, d, n = ada_w.shape

    def body(c_ref, w_ref, b_ref, o_ref):
        c = c_ref[...]
        act = c * _sigmoid(c)
        o_ref[...] = jnp.dot(act, w_ref[...], preferred_element_type=F32,
                             precision=lax.Precision.HIGHEST) + b_ref[...]

    return pl.pallas_call(
        body, name=name, grid=(depth,),
        in_specs=[_full((NDEV, d)), pl.BlockSpec((None, d, n), lambda i: (i, 0, 0)),
                  pl.BlockSpec((None, 1, n), lambda i: (i, 0, 0))],
        out_specs=pl.BlockSpec((None, NDEV, n), lambda i: (i, 0, 0)),
        out_shape=jax.ShapeDtypeStruct((depth, NDEV, n), F32),
        compiler_params=_cp("parallel"),
    )(c_all, ada_w, ada_b_loc.reshape(depth, 1, n))


def _ada_bwd(c_all_t, dmod_cols, name):
    depth, _, n = dmod_cols.shape
    d = c_all_t.shape[0]

    def body(ct_ref, dm_ref, o_ref):
        ct = ct_ref[...]
        act = ct * _sigmoid(ct)
        acc = None
        for b in range(NDEV):
            term = act[:, b:b + 1] * dm_ref[b:b + 1, :]
            acc = term if acc is None else acc + term
        o_ref[...] = acc

    return pl.pallas_call(
        body, name=name, grid=(depth,),
        in_specs=[_full((d, NDEV)), pl.BlockSpec((None, NDEV, n), lambda i: (i, 0, 0))],
        out_specs=pl.BlockSpec((None, d, n), lambda i: (i, 0, 0)),
        out_shape=jax.ShapeDtypeStruct((depth, d, n), F32),
        compiler_params=_cp("parallel"),
    )(c_all_t, dmod_cols)


def _sum_parts(parts, name):
    _, rows, lanes = parts.shape

    def body(p_ref, o_ref):
        acc = p_ref[0]
        for k in range(1, NDEV):
            acc = acc + p_ref[k]
        o_ref[...] = acc

    return pl.pallas_call(
        body, name=name, in_specs=[_full(parts.shape)], out_specs=_full((rows, lanes)), grid=(1,),
        out_shape=jax.ShapeDtypeStruct((rows, lanes), F32), compiler_params=_cp("arbitrary"),
    )(parts)


def _adamw(w, glist, m, v, name):
    nl, rows, cols = w.shape
    tr = _tile(rows, 256, 2 * SUBLANES)

    def body(w_ref, *rest):
        g_refs = rest[:nl]
        m_ref, v_ref, go_ref, d_ref, mo_ref, vo_ref = rest[nl:]
        g = None
        for layer, g_ref in enumerate(g_refs):
            part = g_ref[0].astype(F32)
            for p in range(1, g_ref.shape[0]):
                part = part + g_ref[p].astype(F32)
            g = part if g is None else jnp.where(pl.program_id(0) == layer, part, g)
        m1 = ADAM_B1 * m_ref[...] + (1.0 - ADAM_B1) * g
        v1 = ADAM_B2 * v_ref[...] + (1.0 - ADAM_B2) * (g * g)
        m_hat = m1 / (1.0 - ADAM_B1 ** ADAM_STEP)
        v_hat = v1 / (1.0 - ADAM_B2 ** ADAM_STEP)
        go_ref[...] = g
        mo_ref[...] = m1
        vo_ref[...] = v1
        d_ref[...] = -ADAM_LR * (m_hat / (jnp.sqrt(v_hat) + ADAM_EPS) + ADAM_WD * w_ref[...])

    blk = pl.BlockSpec((None, tr, cols), lambda l, i: (l, i, 0))
    g_specs = [pl.BlockSpec((g.shape[0], tr, cols), lambda l, i: (0, i, 0)) for g in glist]
    return pl.pallas_call(
        body, name=name, grid=(nl, rows // tr),
        in_specs=[blk] + g_specs + [blk, blk],
        out_specs=[blk] * 4, out_shape=[jax.ShapeDtypeStruct((nl, rows, cols), F32)] * 4,
        compiler_params=_cp("parallel", "parallel"),
    )(w, *glist, m, v)


def _pack(pieces):
    flat = jnp.concatenate([p.reshape(-1) for p in pieces])
    unit = SUBLANES * LANES
    padded = -(-flat.shape[0] // unit) * unit
    return jnp.pad(flat, (0, padded - flat.shape[0])).reshape(padded // LANES, LANES)


def _unpack(packed, shapes, lead=()):
    flat = packed.reshape(lead + (-1,))
    out, off = [], 0
    for s in shapes:
        size = 1
        for dim in s:
            size *= dim
        out.append(flat[..., off:off + size].reshape(lead + tuple(s)))
        off += size
    return out


def _pad_last(a, n):
    return jnp.pad(a, [(0, 0)] * (a.ndim - 1) + [(0, n - a.shape[-1])])


def kernel(x, c, ada_w, ada_b, ln_tok_g, ln_tok_b, ln_ch_g, ln_ch_b, a_w_in, a_conv_w, a_conv_b, a_w_out, b_w_pw1, b_b_pw1, b_conv_w, b_conv_b, b_ln_g, b_ln_b, b_w_pw2, b_b_pw2, f_w_up, f_conv_w, f_conv_b, f_w_gate, f_w_down, loss_target, m_ada_w, m_ada_b, m_ln_tok_g, m_ln_tok_b, m_ln_ch_g, m_ln_ch_b, m_a_w_in, m_a_conv_w, m_a_conv_b, m_a_w_out, m_b_w_pw1, m_b_b_pw1, m_b_conv_w, m_b_conv_b, m_b_ln_g, m_b_ln_b, m_b_w_pw2, m_b_b_pw2, m_f_w_up, m_f_conv_w, m_f_conv_b, m_f_w_gate, m_f_w_down, v_ada_w, v_ada_b, v_ln_tok_g, v_ln_tok_b, v_ln_ch_g, v_ln_ch_b, v_a_w_in, v_a_conv_w, v_a_conv_b, v_a_w_out, v_b_w_pw1, v_b_b_pw1, v_b_conv_w, v_b_conv_b, v_b_ln_g, v_b_ln_b, v_b_w_pw2, v_b_b_pw2, v_f_w_up, v_f_conv_w, v_f_conv_b, v_f_w_gate, v_f_w_down):
    weights = dict(ada_w=ada_w, ada_b=ada_b, ln_tok_g=ln_tok_g, ln_tok_b=ln_tok_b, ln_ch_g=ln_ch_g, ln_ch_b=ln_ch_b, a_w_in=a_w_in, a_conv_w=a_conv_w, a_conv_b=a_conv_b, a_w_out=a_w_out, b_w_pw1=b_w_pw1, b_b_pw1=b_b_pw1, b_conv_w=b_conv_w, b_conv_b=b_conv_b, b_ln_g=b_ln_g, b_ln_b=b_ln_b, b_w_pw2=b_w_pw2, b_b_pw2=b_b_pw2, f_w_up=f_w_up, f_conv_w=f_conv_w, f_conv_b=f_conv_b, f_w_gate=f_w_gate, f_w_down=f_w_down)
    mom_m = dict(ada_w=m_ada_w, ada_b=m_ada_b, ln_tok_g=m_ln_tok_g, ln_tok_b=m_ln_tok_b, ln_ch_g=m_ln_ch_g, ln_ch_b=m_ln_ch_b, a_w_in=m_a_w_in, a_conv_w=m_a_conv_w, a_conv_b=m_a_conv_b, a_w_out=m_a_w_out, b_w_pw1=m_b_w_pw1, b_b_pw1=m_b_b_pw1, b_conv_w=m_b_conv_w, b_conv_b=m_b_conv_b, b_ln_g=m_b_ln_g, b_ln_b=m_b_ln_b, b_w_pw2=m_b_w_pw2, b_b_pw2=m_b_b_pw2, f_w_up=m_f_w_up, f_conv_w=m_f_conv_w, f_conv_b=m_f_conv_b, f_w_gate=m_f_w_gate, f_w_down=m_f_w_down)
    mom_v = dict(ada_w=v_ada_w, ada_b=v_ada_b, ln_tok_g=v_ln_tok_g, ln_tok_b=v_ln_tok_b, ln_ch_g=v_ln_ch_g, ln_ch_b=v_ln_ch_b, a_w_in=v_a_w_in, a_conv_w=v_a_conv_w, a_conv_b=v_a_conv_b, a_w_out=v_a_w_out, b_w_pw1=v_b_w_pw1, b_b_pw1=v_b_b_pw1, b_conv_w=v_b_conv_w, b_conv_b=v_b_conv_b, b_ln_g=v_b_ln_g, b_ln_b=v_b_ln_b, b_w_pw2=v_b_w_pw2, b_b_pw2=v_b_b_pw2, f_w_up=v_f_w_up, f_conv_w=v_f_conv_w, f_conv_b=v_f_conv_b, f_w_gate=v_f_w_gate, f_w_down=v_f_w_down)
    names = list(weights)

    depth, d, n_ada = ada_w.shape
    assert depth == 2 and a_w_in.shape[0] == 1 and b_w_pw1.shape[0] == 1
    s_len = x.shape[1]
    f_loc = f_w_up.shape[-1]
    f_pad = -(-f_loc // LANES) * LANES
    f_all = NDEV * f_pad
    d_loc = d // NDEV
    ka, kb, kf = a_conv_w.shape[1], b_conv_w.shape[1], f_conv_w.shape[1]
    alpha = (2.0 * depth) ** 0.25
    assert a_w_in.shape[-1] == f_pad and f_pad % d_loc == 0
    me = 4 * lax.axis_index("x") + 2 * lax.axis_index("y") + lax.axis_index("c")

    small_shapes = [(d,), (ka, d_loc), (2 * d_loc,), (kb, d_loc), (d_loc,), (d_loc,), (d_loc,), (d_loc,),
                    (depth, kf, f_pad)]
    small_loc = _pack([c[0], a_conv_w[0], b_b_pw1[0], b_conv_w[0], b_conv_b[0], b_ln_g[0], b_ln_b[0],
                       b_b_pw2[0], _pad_last(f_conv_w, f_pad)])
    g_small, g_in, _ = _gather_two_level([small_loc, a_w_in.astype(BF16)], small_loc, "gather_first")

    (c_all, acw_g, bb1_g, bcw_g, bcb_g, blg_g, blb_g, bb2_g, fcw_g) = _unpack(g_small, small_shapes, (NDEV,))
    a_cw = acw_g.transpose(1, 0, 2).reshape(ka, d)
    b_cw = bcw_g.transpose(1, 0, 2).reshape(kb, d)
    b_b1 = bb1_g.reshape(1, 2 * d)
    b_cb, b_lg, b_lb, b_b2 = (t.reshape(1, d) for t in (bcb_g, blg_g, blb_g, bb2_g))
    f_cw = fcw_g.transpose(1, 2, 0, 3).reshape(depth, kf, f_all)
    f_cb = _pad_last(f_conv_b.reshape(depth, NDEV, f_loc), f_pad).reshape(depth, 1, f_all)

    ada_b_loc = lax.dynamic_slice(ada_b, (0, me * n_ada), (depth, n_ada))
    mod_part = _ada_fwd(c_all, ada_w, ada_b_loc, "ada_fwd")
    mod_g, mod_done = _exchange([mod_part.reshape(depth * NDEV, n_ada)], "gather", "gather_mod")
    mod_all = mod_g.reshape(NDEV, depth, NDEV, n_ada).transpose(1, 2, 0, 3).reshape(depth, NDEV, 6 * d)
    mod = lax.dynamic_slice(mod_all, (0, me, 0), (depth, 1, 6 * d))[:, 0]

    gather_out = _exchange_start([_after(a_w_out[0], mod_done).astype(BF16)], "gather_chips", "gather_out_start")
    up_pad = _pad_last(_after(f_w_up, gather_out[-1]), f_pad).astype(BF16)
    gate_pad = _pad_last(f_w_gate, f_pad).astype(BF16)
    down_pad = jnp.pad(f_w_down, ((0, 0), (0, f_pad - f_loc), (0, 0))).astype(BF16)
    col_f = [jnp.stack([up_pad[i], gate_pad[i]]) for i in range(depth)]
    row_b = jnp.concatenate([down_pad[1], b_w_pw2[0].astype(BF16)], axis=0)
    ridx_pw2 = f_pad // d_loc
    gather_f0 = _exchange_start([col_f[0], down_pad[0]], "gather_chips", "gather_f0_start")

    def mod_rows(i):
        return [mod[i:i + 1, j * d:(j + 1) * d] for j in range(6)]

    zeros_d = jnp.zeros((1, d), F32)
    zeros_f = jnp.zeros((1, f_all), F32)
    x0 = x[0]

    sh_t0, sc_t0, g_t0, sh_c0, sc_c0, g_c0 = mod_rows(0)
    sh_t1, sc_t1, g_t1, sh_c1, sc_c1, g_c1 = mod_rows(1)

    sc_t0 = _after(sc_t0, gather_f0[-1])
    bcv, = _mm_fwd(x0, sc_t0, sh_t0, jnp.zeros((1, 3 * d), F32), g_in, (0,), "a_in_fwd")
    y0 = _gateconv_fwd(bcv, a_cw, a_conv_b, "a_conv_fwd")
    g_out, landed = _exchange_wait(gather_out, y0, "gather_chips", "gather_out_wait")
    g_out, _ = _exchange_wait(_exchange_start([g_out], "forward", "gather_out_fwd_start"), landed, "forward",
                              "gather_out_fwd_wait")
    y_a, x1, xh1, rs1 = _mm_ln(y0, g_out, d_loc, 0, x0, g_t0, ln_tok_g[0:1], ln_tok_b[0:1], zeros_d,
                               alpha, "a_out_ln_fwd")

    def ffn_fwd(xin, sc, sh, gate, gam, bet, g_colf, g_rowf, layer, tag):
        u0, vg = _mm_fwd(xin, sc, sh, zeros_f, g_colf, (0, 1), "f_upgate_fwd" + tag)
        t, y, xo, xh, rs = _ffn_tail_fwd(u0, vg, f_cw[layer], f_cb[layer], g_rowf, xin, gate, gam, bet, alpha,
                                         "f_tail_fwd" + tag)
        return u0, vg, t, y, xo, xh, rs

    g_colf0, g_rowf0, landed = _exchange_wait(gather_f0, x1, "gather_chips", "gather_f0_wait")
    g_colf0, g_rowf0, landed = _exchange_wait(
        _exchange_start([g_colf0, g_rowf0], "forward", "gather_f0_fwd_start"), landed, "forward", "gather_f0_fwd_wait")
    gather_1 = _exchange_start([_after(b_w_pw1, landed).astype(BF16), col_f[1], row_b], "gather_chips",
                               "gather_1_start")
    sc_c0 = _after(sc_c0, gather_1[-1])
    u0_0, vg_0, t_0, y_f0, x2, xh2, rs2 = ffn_fwd(x1, sc_c0, sh_c0, g_c0, ln_ch_g[0:1], ln_ch_b[0:1],
                                                  g_colf0, g_rowf0, 0, "0")

    *lands_1, landed = _exchange_wait(gather_1, x2, "gather_chips", "gather_1_wait")
    g_pw1, g_colf1, g_rowb, _ = _exchange_wait(_exchange_start(lands_1, "forward", "gather_1_fwd_start"), landed,
                                                 "forward", "gather_1_fwd_wait")
    ub, = _mm_fwd(x2, sc_t1, sh_t1, b_b1, g_pw1, (0,), "b_pw1_fwd")
    a2, a4 = _b_mid_fwd(ub, b_cw, b_cb, b_lg, b_lb, "b_mid_fwd")
    y_b, x3, xh3, rs3 = _mm_ln(a4, g_rowb, d_loc, ridx_pw2, x2, g_t1, ln_tok_g[1:2], ln_tok_b[1:2], b_b2,
                               alpha, "b_pw2_ln_fwd")
    u0_1, vg_1, t_1, y_f1, x4, xh4, rs4 = ffn_fwd(x3, sc_c1, sh_c1, g_c1, ln_ch_g[1:2], ln_ch_b[1:2],
                                                  g_colf1, g_rowb, 1, "1")

    dx4, loss_part = _loss_head(x4, loss_target[0], "loss_head")

    def ffn_bwd(dxo, xin, sc, sh, gate, gam, u0, vg, t, y, xh, rs, g_colf, g_rowf, layer, tag):
        dy, dres, acc = _ln_bwd(dxo, xh, rs, gam, y, gate, alpha, "f_ln_bwd" + tag)
        dt = _mm_nt_row(dy, g_rowf, f_pad, 0, "f_down_dx" + tag)
        dw_down = _mm_tn_row(t, dy, f_pad, f_loc, "f_down_dw" + tag)
        du0, dvg, dcw, dcb = _ffn_mid_bwd(u0, vg, dt, f_cw[layer], f_cb[layer], "f_mid_bwd" + tag)
        dw_up = _mm_tn_col_t(xin, sc, sh, du0, f_loc, "f_up_dw" + tag)
        dw_gate = _mm_tn_col_t(xin, sc, sh, dvg, f_loc, "f_gate_dw" + tag)
        scatter = _exchange_start([dw_up, dw_gate, dw_down], "scatter", "scatter_f%s_start" % tag)
        dxin, acc2 = _mm_nt_mod([du0, dvg], g_colf, (0, 1), xin, _after(sc, scatter[-1]), dres, "f_upgate_dx" + tag)
        return dxin, acc, acc2, scatter, dcw, dcb

    dx3, accf1, acc2f1, scatter_f1, dfcw1, dfcb1 = ffn_bwd(
        dx4, x3, sc_c1, sh_c1, g_c1, ln_ch_g[1:2], u0_1, vg_1, t_1, y_f1, xh4, rs4, g_colf1, g_rowb, 1, "1")

    dy, dres, accb = _ln_bwd(dx3, xh3, rs3, ln_tok_g[1:2], y_b, g_t1, alpha, "b_ln_bwd")
    da4 = _mm_nt_row(dy, g_rowb, d_loc, ridx_pw2, "b_pw2_dx")
    dw_pw2 = _mm_tn_row(a4, dy, d_loc, d_loc, "b_pw2_dw")
    du, dbcw, dbcb, dblg, dblb, dbb1 = _b_mid_bwd(ub, a2, da4, b_cw, b_lg, b_lb, "b_mid_bwd")
    dw_pw1 = _mm_tn_col(x2, sc_t1, sh_t1, du, "b_pw1_dw")
    scatter_b = _exchange_start([dw_pw1, dw_pw2], "scatter", "scatter_b_start")
    dx2, acc2b = _mm_nt_mod([du], g_pw1, (0,), x2, _after(sc_t1, scatter_b[-1]), dres, "b_pw1_dx")

    dx1, accf0, acc2f0, scatter_f0, dfcw0, dfcb0 = ffn_bwd(
        dx2, x1, sc_c0, sh_c0, g_c0, ln_ch_g[0:1], u0_0, vg_0, t_0, y_f0, xh2, rs2, g_colf0, g_rowf0, 0, "0")

    dy, dres, acca = _ln_bwd(dx1, xh1, rs1, ln_tok_g[0:1], y_a, g_t0, alpha, "a_ln_bwd")
    dy0 = _mm_nt_row(dy, g_out, d_loc, 0, "a_out_dx")
    dbcv, dacw, dacb = _gateconv_bwd(bcv, dy0, a_cw, a_conv_b, "a_conv_bwd")
    dx0, acc2a = _mm_nt_mod([dbcv], g_in, (0,), x0, sc_t0, dres, "a_in_dx")

    def dmod_row(acc2_t, acc_t, acc2_c, acc_c):
        return jnp.concatenate([acc2_t[1], acc2_t[0], acc_t[2], acc2_c[1], acc2_c[0], acc_c[2]])

    dmod = jnp.stack([dmod_row(acc2a, acca, acc2f0, accf0), dmod_row(acc2b, accb, acc2f1, accf1)])

    def unpad_f(a):
        return a.reshape(a.shape[:-1] + (NDEV, f_pad))[..., :f_loc].reshape(a.shape[:-1] + (NDEV * f_loc,))

    small_grads = [
        dmod,
        jnp.stack([acca[0], accb[0]]), jnp.stack([acca[1], accb[1]]),
        jnp.stack([accf0[0], accf1[0]]), jnp.stack([accf0[1], accf1[1]]),
        dacb,
        unpad_f(jnp.concatenate([dfcb0, dfcb1], axis=0)),
        dacw, dbb1, dbcw, dbcb, dblg, dblb, accb[3:4],
        jnp.stack([dfcw0, dfcw1]),
        loss_part[0:1, 0:1],
    ]
    small_grad_shapes = [tuple(g.shape) for g in small_grads]
    gather_small = _exchange_start([_pack(small_grads)], "gather", "gather_small_start")

    dw_in = _mm_tn_col(x0, _after(sc_t0, gather_small[-1]), sh_t0, dbcv, "a_in_dw")
    dw_out = _mm_tn_row(y0, dy, d_loc, d_loc, "a_out_dw")
    scatter_a = _exchange_start([dw_in, dw_out], "scatter", "scatter_a_start")

    grads, deltas, new_m, new_v = {}, {}, {}, {}

    def adamw(k, glist, transposed=False):
        def view(a):
            a = jnp.swapaxes(a, 1, 2) if transposed else a
            return a.reshape(len(glist), -1, a.shape[-1])

        w = view(weights[k])
        outs = _adamw(w, [g.reshape(g.shape[0], -1, w.shape[-1]) for g in glist],
                      view(mom_m[k]), view(mom_v[k]), "adamw_" + k)
        if transposed:
            outs = [jnp.swapaxes(o, 1, 2) for o in outs]
        grads[k], deltas[k], new_m[k], new_v[k] = (o.reshape(weights[k].shape) for o in outs)

    r_up1, r_gate1, r_down1, _ = _exchange_wait(scatter_f1, scatter_a[-1], "scatter", "scatter_f1_wait")
    r_pw1, r_pw2, _ = _exchange_wait(scatter_b, r_down1, "scatter", "scatter_b_wait")
    adamw("b_w_pw1", [r_pw1])
    adamw("b_w_pw2", [r_pw2])
    r_up0, r_gate0, r_down0, _ = _exchange_wait(scatter_f0, deltas["b_w_pw2"], "scatter", "scatter_f0_wait")
    adamw("f_w_up", [r_up0, r_up1], transposed=True)
    adamw("f_w_gate", [r_gate0, r_gate1], transposed=True)
    adamw("f_w_down", [r_down0, r_down1])

    sg_all, _ = _exchange_wait(gather_small, deltas["f_w_down"], "gather", "gather_small_wait")
    sg_sum = _sum_parts(sg_all, "sum_small_grads")
    (g_ada_b, g_ltg, g_ltb, g_lcg, g_lcb, g_acb, g_fcb, g_acw, g_bb1, g_bcw, g_bcb, g_blg, g_blb, g_bb2,
     g_fcw, loss_all) = _unpack(sg_sum, small_grad_shapes)
    loss = loss_all[0, 0]

    def my_cols(a, width):
        return lax.dynamic_slice_in_dim(a, me * width, width, axis=a.ndim - 1)

    g_fcw_loc = my_cols(g_fcw, f_pad)[..., :f_loc]
    small = dict(
        ada_b=g_ada_b, ln_tok_g=g_ltg, ln_tok_b=g_ltb, ln_ch_g=g_lcg, ln_ch_b=g_lcb, a_conv_b=g_acb, f_conv_b=g_fcb,
        a_conv_w=my_cols(g_acw, d_loc)[None], b_b_pw1=my_cols(g_bb1, 2 * d_loc), b_conv_w=my_cols(g_bcw, d_loc)[None],
        b_conv_b=my_cols(g_bcb, d_loc), b_ln_g=my_cols(g_blg, d_loc), b_ln_b=my_cols(g_blb, d_loc),
        b_b_pw2=my_cols(g_bb2, d_loc), f_conv_w=g_fcw_loc)

    dmod_all = sg_all.reshape(NDEV, -1)[:, :depth * 6 * d].reshape(NDEV, depth, 6 * d)
    dmod_cols = my_cols(dmod_all, n_ada).transpose(1, 0, 2)
    g_ada_w = _ada_bwd(c_all.T, dmod_cols, "ada_bwd")

    adamw("ada_w", [g_ada_w[0:1], g_ada_w[1:2]])
    for k, g in small.items():
        adamw(k, [g[None]])

    r_in, r_out, _ = _exchange_wait(scatter_a, deltas["ada_w"], "scatter", "scatter_a_wait")
    adamw("a_w_in", [r_in])
    adamw("a_w_out", [r_out])

    return (loss, dx0[None], *[grads[k] for k in names], *[deltas[k] for k in names],
            *[new_m[k] for k in names], *[new_v[k] for k in names])
```

```python
import functools

import jax
import jax.numpy as jnp
from jax import lax
from jax.experimental import pallas as pl
from jax.experimental.pallas import tpu as pltpu

NDEV = 8
MESH_AXES = ("x", "y", "c")
LANES = 128
SUBLANES = 8
VMEM_LIMIT = 56 * 1024 * 1024
LN_EPS = 1e-5
SHORT_PAD = 16
LONG_PAD = 32
CHUNK = 16
ADAM_LR, ADAM_B1, ADAM_B2, ADAM_EPS, ADAM_WD, ADAM_STEP = 0.001, 0.9, 0.999, 1e-08, 0.01, 10

F32 = jnp.float32
BF16 = jnp.bfloat16
MESH = pl.DeviceIdType.MESH
NT = (((1,), (1,)), ((), ()))
TN = (((0,), (0,)), ((), ()))


def _tile(n, target, mult=SUBLANES):
    best = None
    for t in range(mult, min(n, target) + 1, mult):
        if n % t == 0:
            best = t
    return best if best is not None else n


def _full(shape):
    nd = len(shape)
    return pl.BlockSpec(shape, lambda *_: (0,) * nd)


def _cp(*sem):
    return pltpu.CompilerParams(dimension_semantics=sem, vmem_limit_bytes=VMEM_LIMIT)


def _sigmoid(x):
    return 1.0 / (1.0 + jnp.exp(-x))


def _peer(x, y, c, d):
    return ((1 - x) if d & 4 else x, (1 - y) if d & 2 else y, (1 - c) if d & 1 else c)


def _lin(p):
    return 4 * p[0] + 2 * p[1] + p[2]


CHIP_MASKS = (2, 4, 6)
MODES_PER_ARRAY = {"gather": NDEV - 1, "scatter": NDEV - 1, "gather_chips": 1 + len(CHIP_MASKS),
                   "forward": len(CHIP_MASKS)}


def _transfers(mode):
    x, y, c = (lax.axis_index(a) for a in MESH_AXES)
    me = _lin((x, y, c))
    if mode == "forward":
        sibling = (x, y, 1 - c)
        return [(sibling, ("land", _lin(_peer(x, y, c, q))), _lin(_peer(x, y, c, q)), _lin(_peer(x, y, c, q ^ 1)))
                for q in CHIP_MASKS]
    masks = (1,) + CHIP_MASKS if mode == "gather_chips" else range(1, NDEV)
    out = []
    for d in masks:
        peer = _peer(x, y, c, d)
        source = ("block", _lin(peer)) if mode == "scatter" else ("whole", None)
        out.append((peer, source, me, _lin(peer)))
    return out


def _remote_copies(src_refs, land_refs, send_sems, recv_sems, mode):
    transfers = _transfers(mode)
    sends, recvs = [], []
    for i, land_ref in enumerate(land_refs):
        for t, (peer, (kind, slot), there, here) in enumerate(transfers):
            k = i * len(transfers) + t
            src = land_ref.at[slot] if kind == "land" else src_refs[i].at[slot] if kind == "block" else src_refs[i]
            for dst_slot, out in ((there, sends), (here, recvs)):
                out.append(pltpu.make_async_remote_copy(
                    src_ref=src, dst_ref=land_ref.at[dst_slot], send_sem=send_sems.at[k], recv_sem=recv_sems.at[k],
                    device_id=peer, device_id_type=MESH))
    return sends, recvs


def _exchange(srcs, mode, name):
    n = len(srcs)
    gather = mode == "gather"

    def body(*refs):
        src_refs, out_refs, token = refs[:n], refs[n:2 * n], refs[2 * n]
        send_sems, recv_sems, local_sems = refs[2 * n + 1:]
        me = _lin(tuple(lax.axis_index(a) for a in MESH_AXES))
        local = []
        for i in range(n):
            mine = src_refs[i] if gather else src_refs[i].at[me]
            cp = pltpu.make_async_copy(mine, out_refs[i].at[me], local_sems.at[i])
            cp.start()
            local.append(cp)
        sends, recvs = _remote_copies(src_refs, out_refs, send_sems, recv_sems, mode)
        for snd in sends:
            snd.start()
        token[...] = jnp.zeros_like(token)
        for snd, rcv in zip(sends, recvs):
            snd.wait_send()
            rcv.wait_recv()
        for cp in local:
            cp.wait()

    out_shape = [jax.ShapeDtypeStruct(((NDEV,) + s.shape) if gather else s.shape, s.dtype) for s in srcs]
    out_shape.append(jax.ShapeDtypeStruct((SUBLANES, LANES), F32))
    any_spec = pl.BlockSpec(memory_space=pl.ANY)
    return pl.pallas_call(
        body, name=name, out_shape=out_shape,
        in_specs=[any_spec] * n, out_specs=[any_spec] * n + [pl.BlockSpec(memory_space=pltpu.VMEM)],
        scratch_shapes=[pltpu.SemaphoreType.DMA((n * (NDEV - 1),)),
                        pltpu.SemaphoreType.DMA((n * (NDEV - 1),)),
                        pltpu.SemaphoreType.DMA((n,))],
    )(*srcs)


HBM_SPEC = pl.BlockSpec(memory_space=pltpu.HBM)
SEM_SPEC = pl.BlockSpec(memory_space=pltpu.SEMAPHORE)
SIDE_EFFECT = pltpu.SideEffectType.DATAFLOW_SIDE_EFFECTING


def _exchange_start(arrays, mode, name):
    me = _lin(tuple(lax.axis_index(a) for a in MESH_AXES))
    if mode == "forward":
        srcs, lands = [], list(arrays)
    else:
        srcs, lands = list(arrays), []
        for s in srcs:
            own = lax.dynamic_index_in_dim(s, me, 0, keepdims=False) if mode == "scatter" else s
            shape = s.shape if mode == "scatter" else (NDEV,) + s.shape
            lands.append(lax.dynamic_update_index_in_dim(lax.empty(shape, s.dtype), own, me, 0))
    ns, n = len(srcs), len(lands)

    def body(*refs):
        src_refs, land_refs = refs[:ns], refs[ns:ns + n]
        send_sems, recv_sems, token = refs[ns + n], refs[ns + n + 1], refs[-1]
        sends, _ = _remote_copies(src_refs, land_refs, send_sems, recv_sems, mode)
        for snd in sends:
            snd.start()
        token[...] = jnp.zeros_like(token)

    operands = [pltpu.with_memory_space_constraint(a, pltpu.HBM) for a in srcs + lands]
    nsem = n * MODES_PER_ARRAY[mode]
    return pl.pallas_call(
        body, name=name,
        out_shape=(pltpu.SemaphoreType.DMA((nsem,)), pltpu.SemaphoreType.DMA((nsem,)),
                   *[pltpu.HBM(a.shape, a.dtype) for a in operands],
                   jax.ShapeDtypeStruct((SUBLANES, LANES), F32)),
        in_specs=[HBM_SPEC] * (ns + n),
        out_specs=(SEM_SPEC, SEM_SPEC, *([HBM_SPEC] * (ns + n)), pl.BlockSpec(memory_space=pltpu.VMEM)),
        input_output_aliases={i: 2 + i for i in range(ns + n)},
        compiler_params=pltpu.CompilerParams(has_side_effects=SIDE_EFFECT),
    )(*operands)


def _exchange_wait(handle, after, mode, name):
    send_sems, recv_sems, *thru = handle[:-1]
    n = len(thru) if mode == "forward" else len(thru) // 2
    ns = len(thru) - n

    def body(*refs):
        src_refs, land_refs = refs[:ns], refs[ns:ns + n]
        sends, recvs = _remote_copies(src_refs, land_refs, refs[ns + n], refs[ns + n + 1], mode)
        for snd, rcv in zip(sends, recvs):
            snd.wait_send()
            rcv.wait_recv()
        refs[-1][...] = jnp.zeros_like(refs[-1])

    outs = pl.pallas_call(
        body, name=name,
        out_shape=(*[pltpu.HBM(a.shape, a.dtype) for a in thru], jax.ShapeDtypeStruct((SUBLANES, LANES), F32)),
        in_specs=[HBM_SPEC] * (ns + n) + [SEM_SPEC, SEM_SPEC, pl.BlockSpec(memory_space=pl.ANY)],
        out_specs=[HBM_SPEC] * (ns + n) + [pl.BlockSpec(memory_space=pltpu.VMEM)],
        input_output_aliases={i: i for i in range(ns + n)},
        compiler_params=pltpu.CompilerParams(has_side_effects=SIDE_EFFECT),
    )(*thru, send_sems, recv_sems, after)
    return outs[ns:]


def _gather_two_level(srcs, after, name):
    first = _exchange_start(srcs, "gather_chips", name + "_chips_start")
    *lands, token = _exchange_wait(first, after, "gather_chips", name + "_chips_wait")
    second = _exchange_start(lands, "forward", name + "_forward_start")
    return _exchange_wait(second, token, "forward", name + "_forward_wait")


def _after(value, token):
    return value + token[0, 0]


ANY_SPEC = pl.BlockSpec(memory_space=pl.ANY)


def _load_cols(wg_ref, widx, w_ref, sems):
    n = wg_ref.shape[-1]
    copies = [pltpu.make_async_copy(wg_ref.at[k, widx], w_ref.at[:, pl.ds(k * n, n)], sems.at[k])
              for k in range(NDEV)]
    for cp in copies:
        cp.start()
    for cp in copies:
        cp.wait()


def _load_rows(wg_ref, r, ridx, w_ref, sems):
    copies = [pltpu.make_async_copy(wg_ref.at[k, pl.ds(ridx * r, r)], w_ref.at[pl.ds(k * r, r)], sems.at[k])
              for k in range(NDEV)]
    for cp in copies:
        cp.start()
    for cp in copies:
        cp.wait()


def _mm_fwd(x, sc, sh, bias, wg, widxs, name):
    s_len, kdim = x.shape
    ncol = NDEV * wg.shape[-1]
    tm = _tile(s_len, 512)
    nw = len(widxs)

    def body(x_ref, sc_ref, sh_ref, b_ref, wg_ref, *rest):
        o_refs, w_refs, sems = rest[:nw], rest[nw:2 * nw], rest[2 * nw]

        @pl.when(pl.program_id(0) == 0)
        def _():
            for i, w_ref in enumerate(w_refs):
                _load_cols(wg_ref, widxs[i], w_ref, sems.at[i])

        h = (x_ref[...] * (1.0 + sc_ref[...]) + sh_ref[...]).astype(BF16)
        for w_ref, o_ref in zip(w_refs, o_refs):
            o_ref[...] = (jnp.dot(h, w_ref[...], preferred_element_type=F32) + b_ref[...]).astype(BF16)

    return pl.pallas_call(
        body, name=name, grid=(s_len // tm,),
        in_specs=[pl.BlockSpec((tm, kdim), lambda i: (i, 0)), _full((1, kdim)), _full((1, kdim)),
                  _full((1, ncol)), ANY_SPEC],
        out_specs=[pl.BlockSpec((tm, ncol), lambda i: (i, 0))] * nw,
        out_shape=[jax.ShapeDtypeStruct((s_len, ncol), BF16)] * nw,
        scratch_shapes=[pltpu.VMEM((kdim, ncol), BF16)] * nw + [pltpu.SemaphoreType.DMA((nw, NDEV))],
        compiler_params=_cp("arbitrary"),
    )(x, sc, sh, bias, wg)


def _mm_ln(a, wg, r, ridx, xres, gate, gam, bet, bias, alpha, name):
    s_len = a.shape[0]
    d = wg.shape[-1]
    tm = _tile(s_len, 512)

    def body(a_ref, wg_ref, x_ref, g_ref, gam_ref, bet_ref, b_ref, y_ref, xo_ref, xh_ref, rs_ref, w_ref, sems):
        @pl.when(pl.program_id(0) == 0)
        def _():
            _load_rows(wg_ref, r, ridx, w_ref, sems)

        y = jnp.dot(a_ref[...], w_ref[...], preferred_element_type=F32) + b_ref[...]
        z = alpha * x_ref[...] + g_ref[...] * y
        mu = jnp.mean(z, axis=-1, keepdims=True)
        zc = z - mu
        var = jnp.mean(zc * zc, axis=-1, keepdims=True)
        rstd = lax.rsqrt(var + LN_EPS)
        xh = zc * rstd
        y_ref[...] = y.astype(BF16)
        xh_ref[...] = xh
        rs_ref[...] = rstd
        xo_ref[...] = xh * gam_ref[...] + bet_ref[...]

    row = pl.BlockSpec((tm, d), lambda i: (i, 0))
    vec = _full((1, d))
    return pl.pallas_call(
        body, name=name, grid=(s_len // tm,),
        in_specs=[pl.BlockSpec((tm, NDEV * r), lambda i: (i, 0)), ANY_SPEC, row, vec, vec, vec, vec],
        out_specs=[row, row, row, pl.BlockSpec((tm, 1), lambda i: (i, 0))],
        out_shape=[jax.ShapeDtypeStruct((s_len, d), BF16)] + [jax.ShapeDtypeStruct((s_len, d), F32)] * 2
        + [jax.ShapeDtypeStruct((s_len, 1), F32)],
        scratch_shapes=[pltpu.VMEM((NDEV * r, d), BF16), pltpu.SemaphoreType.DMA((NDEV,))],
        compiler_params=_cp("arbitrary"),
    )(a, wg, xres, gate, gam, bet, bias)


def _ln_bwd(dxo, xh, rstd, gam, y, gate, alpha, name):
    s_len, d = dxo.shape
    tm = _tile(s_len, 256)

    def body(d_ref, xh_ref, rs_ref, gam_ref, y_ref, g_ref, dy_ref, dres_ref, acc_ref):
        @pl.when(pl.program_id(0) == 0)
        def _():
            acc_ref[...] = jnp.zeros_like(acc_ref)

        dxo_t = d_ref[...]
        xh_t = xh_ref[...]
        dxh = dxo_t * gam_ref[...]
        m1 = jnp.mean(dxh, axis=-1, keepdims=True)
        m2 = jnp.mean(dxh * xh_t, axis=-1, keepdims=True)
        dz = rs_ref[...] * (dxh - m1 - xh_t * m2)
        dy = g_ref[...] * dz
        dy_ref[...] = dy.astype(BF16)
        dres_ref[...] = alpha * dz
        acc_ref[0:1, :] += jnp.sum(dxo_t * xh_t, axis=0, keepdims=True)
        acc_ref[1:2, :] += jnp.sum(dxo_t, axis=0, keepdims=True)
        acc_ref[2:3, :] += jnp.sum(dz * y_ref[...].astype(F32), axis=0, keepdims=True)
        acc_ref[3:4, :] += jnp.sum(dy, axis=0, keepdims=True)

    row = pl.BlockSpec((tm, d), lambda i: (i, 0))
    vec = _full((1, d))
    return pl.pallas_call(
        body, name=name, grid=(s_len // tm,),
        in_specs=[row, row, pl.BlockSpec((tm, 1), lambda i: (i, 0)), vec, row, vec],
        out_specs=[row, row, _full((SUBLANES, d))],
        out_shape=[jax.ShapeDtypeStruct((s_len, d), BF16), jax.ShapeDtypeStruct((s_len, d), F32),
                   jax.ShapeDtypeStruct((SUBLANES, d), F32)],
        compiler_params=_cp("arbitrary"),
    )(dxo, xh, rstd, gam, y, gate)


def _mm_nt_row(dy, wg, r, ridx, name):
    s_len, d = dy.shape
    tm = _tile(s_len, 512)

    def body(dy_ref, wg_ref, o_ref, w_ref, sems):
        @pl.when(pl.program_id(0) == 0)
        def _():
            _load_rows(wg_ref, r, ridx, w_ref, sems)

        o_ref[...] = lax.dot_general(dy_ref[...], w_ref[...], NT, preferred_element_type=F32).astype(BF16)

    return pl.pallas_call(
        body, name=name, grid=(s_len // tm,),
        in_specs=[pl.BlockSpec((tm, d), lambda i: (i, 0)), ANY_SPEC],
        out_specs=pl.BlockSpec((tm, NDEV * r), lambda i: (i, 0)),
        out_shape=jax.ShapeDtypeStruct((s_len, NDEV * r), BF16),
        scratch_shapes=[pltpu.VMEM((NDEV * r, d), BF16), pltpu.SemaphoreType.DMA((NDEV,))],
        compiler_params=_cp("arbitrary"),
    )(dy, wg)


def _mm_nt_mod(dos, wg, widxs, xin, sc, dres, name):
    s_len, kdim = xin.shape
    ncol = NDEV * wg.shape[-1]
    tm = _tile(s_len, 512)
    nw = len(widxs)

    def body(*refs):
        do_refs, wg_ref = refs[:nw], refs[nw]
        x_ref, sc_ref, dres_ref, dx_ref, acc_ref = refs[nw + 1:nw + 6]
        w_refs, sems = refs[nw + 6:2 * nw + 6], refs[2 * nw + 6]

        @pl.when(pl.program_id(0) == 0)
        def _():
            acc_ref[...] = jnp.zeros_like(acc_ref)
            for i, w_ref in enumerate(w_refs):
                _load_cols(wg_ref, widxs[i], w_ref, sems.at[i])

        dh = None
        for do_ref, w_ref in zip(do_refs, w_refs):
            p = lax.dot_general(do_ref[...], w_ref[...], NT, preferred_element_type=F32)
            dh = p if dh is None else dh + p
        dx_ref[...] = dh * (1.0 + sc_ref[...]) + dres_ref[...]
        acc_ref[0:1, :] += jnp.sum(dh * x_ref[...], axis=0, keepdims=True)
        acc_ref[1:2, :] += jnp.sum(dh, axis=0, keepdims=True)

    row = pl.BlockSpec((tm, kdim), lambda i: (i, 0))
    return pl.pallas_call(
        body, name=name, grid=(s_len // tm,),
        in_specs=[pl.BlockSpec((tm, ncol), lambda i: (i, 0))] * nw + [ANY_SPEC, row, _full((1, kdim)), row],
        out_specs=[row, _full((SUBLANES, kdim))],
        out_shape=[jax.ShapeDtypeStruct((s_len, kdim), F32), jax.ShapeDtypeStruct((SUBLANES, kdim), F32)],
        scratch_shapes=[pltpu.VMEM((kdim, ncol), BF16)] * nw + [pltpu.SemaphoreType.DMA((nw, NDEV))],
        compiler_params=_cp("arbitrary"),
    )(*dos, wg, xin, sc, dres)


def _mm_tn_col(x, sc, sh, do, name):
    s_len, kdim = x.shape
    n = do.shape[1] // NDEV
    ts = _tile(s_len, 512)
    nsteps = s_len // ts

    def body(x_ref, sc_ref, sh_ref, do_ref, o_ref, acc_ref):
        @pl.when(pl.program_id(0) == 0)
        def _():
            acc_ref[...] = jnp.zeros_like(acc_ref)

        h = (x_ref[...] * (1.0 + sc_ref[...]) + sh_ref[...]).astype(BF16)
        acc_ref[...] += lax.dot_general(h, do_ref[...], TN, preferred_element_type=F32)

        @pl.when(pl.program_id(0) == nsteps - 1)
        def _():
            for k in range(NDEV):
                o_ref[k] = acc_ref[:, k * n:(k + 1) * n].astype(BF16)

    return pl.pallas_call(
        body, name=name, grid=(nsteps,),
        in_specs=[pl.BlockSpec((ts, kdim), lambda i: (i, 0)), _full((1, kdim)), _full((1, kdim)),
                  pl.BlockSpec((ts, NDEV * n), lambda i: (i, 0))],
        out_specs=_full((NDEV, kdim, n)),
        out_shape=jax.ShapeDtypeStruct((NDEV, kdim, n), BF16),
        scratch_shapes=[pltpu.VMEM((kdim, NDEV * n), F32)],
        compiler_params=_cp("arbitrary"),
    )(x, sc, sh, do)


def _mm_tn_col_t(x, sc, sh, do, rows_out, name):
    s_len, kdim = x.shape
    n = do.shape[1] // NDEV
    ts = _tile(s_len, 512)
    nsteps = s_len // ts

    def body(x_ref, sc_ref, sh_ref, do_ref, o_ref, acc_ref):
        @pl.when(pl.program_id(0) == 0)
        def _():
            acc_ref[...] = jnp.zeros_like(acc_ref)

        h = (x_ref[...] * (1.0 + sc_ref[...]) + sh_ref[...]).astype(BF16)
        acc_ref[...] += lax.dot_general(do_ref[...], h, TN, preferred_element_type=F32)

        @pl.when(pl.program_id(0) == nsteps - 1)
        def _():
            for k in range(NDEV):
                o_ref[k] = acc_ref[k * n:k * n + rows_out, :].astype(BF16)

    return pl.pallas_call(
        body, name=name, grid=(nsteps,),
        in_specs=[pl.BlockSpec((ts, kdim), lambda i: (i, 0)), _full((1, kdim)), _full((1, kdim)),
                  pl.BlockSpec((ts, NDEV * n), lambda i: (i, 0))],
        out_specs=_full((NDEV, rows_out, kdim)),
        out_shape=jax.ShapeDtypeStruct((NDEV, rows_out, kdim), BF16),
        scratch_shapes=[pltpu.VMEM((NDEV * n, kdim), F32)],
        compiler_params=_cp("arbitrary"),
    )(x, sc, sh, do)


def _mm_tn_row(a, dy, r, rows_out, name):
    s_len, d = dy.shape
    ts = _tile(s_len, 512)
    nsteps = s_len // ts

    def body(a_ref, dy_ref, o_ref, acc_ref):
        @pl.when(pl.program_id(0) == 0)
        def _():
            acc_ref[...] = jnp.zeros_like(acc_ref)

        acc_ref[...] += lax.dot_general(a_ref[...], dy_ref[...], TN, preferred_element_type=F32)

        @pl.when(pl.program_id(0) == nsteps - 1)
        def _():
            for k in range(NDEV):
                o_ref[k] = acc_ref[k * r:k * r + rows_out, :].astype(BF16)

    return pl.pallas_call(
        body, name=name, grid=(nsteps,),
        in_specs=[pl.BlockSpec((ts, NDEV * r), lambda i: (i, 0)), pl.BlockSpec((ts, d), lambda i: (i, 0))],
        out_specs=_full((NDEV, rows_out, d)),
        out_shape=jax.ShapeDtypeStruct((NDEV, rows_out, d), BF16),
        scratch_shapes=[pltpu.VMEM((NDEV * r, d), F32)],
        compiler_params=_cp("arbitrary"),
    )(a, dy)


def _prev_spec(ts, pad, cb, col):
    return pl.BlockSpec((pad, cb), lambda *g: (jnp.maximum(g[-1] * (ts // pad) - 1, 0), col(g)))


def _next_spec(ts, pad, cb, col, s_len):
    return pl.BlockSpec((pad, cb), lambda *g: (jnp.minimum((g[-1] + 1) * (ts // pad), s_len // pad - 1), col(g)))


class _F32Loads:
    def __init__(self, ref):
        self.ref = ref

    def __getitem__(self, idx):
        return self.ref[idx].astype(F32)


def _direct(buf_ref):
    return lambda off, rows: buf_ref[off:off + rows, :]


def _make_shifts(sh_ref, nrows):
    for r in range(1, SUBLANES):
        sh_ref[r, 0:nrows - SUBLANES, :] = sh_ref[0, r:r + nrows - SUBLANES, :]


def _shifted(sh_ref):
    def read(off, rows):
        r = off % SUBLANES
        return sh_ref[r, off - r:off - r + rows, :]
    return read


def _conv_fwd_rows(read, w_ref, b_ref, ktaps, pad, r0, rows):
    acc = None
    for j in range(ktaps):
        term = w_ref[ktaps - 1 - j:ktaps - j, :] * read(pad - j + r0, rows)
        acc = term if acc is None else acc + term
    return acc + b_ref[...]


def _conv_bwd_rows(read, x_rows, w_ref, dwacc_ref, ktaps, r0, rows):
    acc = None
    for j in range(ktaps):
        sl = read(j + r0, rows)
        term = w_ref[ktaps - 1 - j:ktaps - j, :] * sl
        acc = term if acc is None else acc + term
        prod = x_rows * sl
        fold = prod[0:SUBLANES]
        for q in range(1, rows // SUBLANES):
            fold = fold + prod[q * SUBLANES:(q + 1) * SUBLANES]
        tap = ktaps - 1 - j
        dwacc_ref[tap * SUBLANES:(tap + 1) * SUBLANES, :] += fold
    return acc


def _flush_dw(dwacc_ref, dw_ref, ktaps):
    for tap in range(ktaps):
        dw_ref[tap:tap + 1, :] = jnp.sum(dwacc_ref[tap * SUBLANES:(tap + 1) * SUBLANES, :], axis=0, keepdims=True)


def _gateconv_fwd(bcv, cw, cb, name):
    s_len, d3 = bcv.shape
    d = d3 // 3
    ktaps = cw.shape[0]
    pad = SHORT_PAD
    ts = _tile(s_len, 256)

    def body(gb_ref, gc_ref, v_ref, gcp_ref, vp_ref, w_ref, b_ref, o_ref, pbuf):
        gb_ref, gc_ref, v_ref, gcp_ref, vp_ref = map(_F32Loads, (gb_ref, gc_ref, v_ref, gcp_ref, vp_ref))
        s = pl.program_id(0)
        pbuf[0:pad, :] = jnp.where(s > 0, gcp_ref[...] * vp_ref[...], 0.0)
        pbuf[pad:pad + ts, :] = gc_ref[...] * v_ref[...]
        for r0 in range(0, ts, CHUNK):
            q = _conv_fwd_rows(_direct(pbuf), w_ref, b_ref, ktaps, pad, r0, CHUNK)
            o_ref[r0:r0 + CHUNK, :] = (gb_ref[r0:r0 + CHUNK, :] * q).astype(BF16)

    def cur(part):
        return pl.BlockSpec((ts, d), lambda s: (s, part))

    return pl.pallas_call(
        body, name=name, grid=(s_len // ts,),
        in_specs=[cur(0), cur(1), cur(2),
                  _prev_spec(ts, pad, d, lambda g: 1), _prev_spec(ts, pad, d, lambda g: 2),
                  _full((ktaps, d)), _full((1, d))],
        out_specs=pl.BlockSpec((ts, d), lambda s: (s, 0)),
        out_shape=jax.ShapeDtypeStruct((s_len, d), BF16),
        scratch_shapes=[pltpu.VMEM((pad + ts, d), F32)],
        compiler_params=_cp("parallel"),
    )(bcv, bcv, bcv, bcv, bcv, cw, cb)


def _gateconv_bwd(bcv, dy0, cw, cb, name):
    s_len, d3 = bcv.shape
    d = d3 // 3
    ktaps = cw.shape[0]
    pad = SHORT_PAD
    ts = _tile(s_len, 256)
    nsteps = s_len // ts

    def body(gb_ref, gc_ref, v_ref, gcp_ref, vp_ref, gbn_ref, dy_ref, dyn_ref, w_ref, b_ref,
             o_ref, dw_ref, db_ref, pbuf, dqbuf, dwacc):
        gb_ref, gc_ref, v_ref, gcp_ref, vp_ref, gbn_ref, dy_ref, dyn_ref = map(
            _F32Loads, (gb_ref, gc_ref, v_ref, gcp_ref, vp_ref, gbn_ref, dy_ref, dyn_ref))
        s = pl.program_id(0)

        @pl.when(s == 0)
        def _():
            dwacc[...] = jnp.zeros_like(dwacc)
            db_ref[...] = jnp.zeros_like(db_ref)

        pbuf[0:pad, :] = jnp.where(s > 0, gcp_ref[...] * vp_ref[...], 0.0)
        pbuf[pad:pad + ts, :] = gc_ref[...] * v_ref[...]
        dq = dy_ref[...] * gb_ref[...]
        dqbuf[0:ts, :] = dq
        dqbuf[ts:ts + pad, :] = jnp.where(s < nsteps - 1, dyn_ref[...] * gbn_ref[...], 0.0)
        db_ref[...] += jnp.sum(dq, axis=0, keepdims=True)
        for r0 in range(0, ts, CHUNK):
            rows = slice(r0, r0 + CHUNK)
            q = _conv_fwd_rows(_direct(pbuf), w_ref, b_ref, ktaps, pad, r0, CHUNK)
            o_ref[rows, 0:d] = (dy_ref[rows, :] * q).astype(BF16)
            dp = _conv_bwd_rows(_direct(dqbuf), pbuf[pad + r0:pad + r0 + CHUNK, :], w_ref, dwacc, ktaps, r0, CHUNK)
            o_ref[rows, d:2 * d] = (dp * v_ref[rows, :]).astype(BF16)
            o_ref[rows, 2 * d:3 * d] = (dp * gc_ref[rows, :]).astype(BF16)

        @pl.when(s == nsteps - 1)
        def _():
            _flush_dw(dwacc, dw_ref, ktaps)

    def cur(part):
        return pl.BlockSpec((ts, d), lambda s: (s, part))

    return pl.pallas_call(
        body, name=name, grid=(nsteps,),
        in_specs=[cur(0), cur(1), cur(2),
                  _prev_spec(ts, pad, d, lambda g: 1), _prev_spec(ts, pad, d, lambda g: 2),
                  _next_spec(ts, pad, d, lambda g: 0, s_len),
                  cur(0), _next_spec(ts, pad, d, lambda g: 0, s_len),
                  _full((ktaps, d)), _full((1, d))],
        out_specs=[pl.BlockSpec((ts, d3), lambda s: (s, 0)), _full((ktaps, d)), _full((1, d))],
        out_shape=[jax.ShapeDtypeStruct((s_len, d3), BF16), jax.ShapeDtypeStruct((ktaps, d), F32),
                   jax.ShapeDtypeStruct((1, d), F32)],
        scratch_shapes=[pltpu.VMEM((pad + ts, d), F32), pltpu.VMEM((ts + pad, d), F32),
                        pltpu.VMEM((ktaps * SUBLANES, d), F32)],
        compiler_params=_cp("arbitrary"),
    )(bcv, bcv, bcv, bcv, bcv, bcv, dy0, dy0, cw, cb)


def _ffn_mid_fwd(u0, vg, cw, cb, name):
    s_len, f = u0.shape
    ktaps = cw.shape[0]
    pad = SHORT_PAD
    ts = _tile(s_len, 256)
    cbk = 1024 if f % 1024 == 0 else f

    def body(u_ref, up_ref, vg_ref, w_ref, b_ref, o_ref, ubuf):
        u_ref, up_ref, vg_ref = map(_F32Loads, (u_ref, up_ref, vg_ref))
        s = pl.program_id(1)
        ubuf[0:pad, :] = jnp.where(s > 0, up_ref[...], 0.0)
        ubuf[pad:pad + ts, :] = u_ref[...]
        for r0 in range(0, ts, CHUNK):
            u = _conv_fwd_rows(_direct(ubuf), w_ref, b_ref, ktaps, pad, r0, CHUNK)
            o_ref[r0:r0 + CHUNK, :] = (u * _sigmoid(u) * vg_ref[r0:r0 + CHUNK, :]).astype(BF16)

    cur = pl.BlockSpec((ts, cbk), lambda c, s: (s, c))
    return pl.pallas_call(
        body, name=name, grid=(f // cbk, s_len // ts),
        in_specs=[cur, _prev_spec(ts, pad, cbk, lambda g: g[0]), cur,
                  pl.BlockSpec((ktaps, cbk), lambda c, s: (0, c)), pl.BlockSpec((1, cbk), lambda c, s: (0, c))],
        out_specs=cur,
        out_shape=jax.ShapeDtypeStruct((s_len, f), BF16),
        scratch_shapes=[pltpu.VMEM((pad + ts, cbk), F32)],
        compiler_params=_cp("parallel", "parallel"),
    )(u0, u0, vg, cw, cb)


class _Cols:
    def __init__(self, ref, cols):
        self.ref, self.cols = ref, cols

    def __getitem__(self, idx):
        return self.ref[slice(None) if idx is Ellipsis else idx[0], self.cols]

    def __setitem__(self, idx, value):
        self.ref[idx[0], self.cols] = value


def _ffn_tail_fwd(u0, vg, cw, cb, wg, xres, gate, gam, bet, alpha, name):
    s_len, f = u0.shape
    d = wg.shape[-1]
    r = f // NDEV
    ktaps = cw.shape[0]
    pad = SHORT_PAD
    tm = _tile(s_len, 256)
    cbk = 1024 if f % 1024 == 0 else f

    def body(u_ref, up_ref, vg_ref, cw_ref, cb_ref, wg_ref, x_ref, g_ref, gam_ref, bet_ref,
             t_ref, y_ref, xo_ref, xh_ref, rs_ref, ubuf, w_ref, sems):
        u_ref, up_ref, vg_ref = map(_F32Loads, (u_ref, up_ref, vg_ref))
        s = pl.program_id(0)

        @pl.when(s == 0)
        def _():
            _load_rows(wg_ref, r, 0, w_ref, sems)

        ubuf[0:pad, :] = jnp.where(s > 0, up_ref[...], 0.0)
        ubuf[pad:pad + tm, :] = u_ref[...]
        y = None
        for c0 in range(0, f, cbk):
            cols = slice(c0, c0 + cbk)
            read = _direct(_Cols(ubuf, cols))
            for r0 in range(0, tm, CHUNK):
                rows = slice(r0, r0 + CHUNK)
                u = _conv_fwd_rows(read, _Cols(cw_ref, cols), _Cols(cb_ref, cols), ktaps, pad, r0, CHUNK)
                t_ref[rows, cols] = (u * _sigmoid(u) * vg_ref[rows, cols]).astype(BF16)
            p = jnp.dot(t_ref[:, cols], w_ref[cols, :], preferred_element_type=F32)
            y = p if y is None else y + p
        z = alpha * x_ref[...] + g_ref[...] * y
        mu = jnp.mean(z, axis=-1, keepdims=True)
        zc = z - mu
        var = jnp.mean(zc * zc, axis=-1, keepdims=True)
        rstd = lax.rsqrt(var + LN_EPS)
        xh = zc * rstd
        y_ref[...] = y.astype(BF16)
        xh_ref[...] = xh
        rs_ref[...] = rstd
        xo_ref[...] = xh * gam_ref[...] + bet_ref[...]

    wide = pl.BlockSpec((tm, f), lambda i: (i, 0))
    row = pl.BlockSpec((tm, d), lambda i: (i, 0))
    vec = _full((1, d))
    return pl.pallas_call(
        body, name=name, grid=(s_len // tm,),
        in_specs=[wide, _prev_spec(tm, pad, f, lambda g: 0), wide, _full((ktaps, f)), _full((1, f)), ANY_SPEC,
                  row, vec, vec, vec],
        out_specs=[wide, row, row, row, pl.BlockSpec((tm, 1), lambda i: (i, 0))],
        out_shape=[jax.ShapeDtypeStruct((s_len, f), BF16), jax.ShapeDtypeStruct((s_len, d), BF16),
                   jax.ShapeDtypeStruct((s_len, d), F32), jax.ShapeDtypeStruct((s_len, d), F32),
                   jax.ShapeDtypeStruct((s_len, 1), F32)],
        scratch_shapes=[pltpu.VMEM((pad + tm, f), F32), pltpu.VMEM((f, d), BF16), pltpu.SemaphoreType.DMA((NDEV,))],
        compiler_params=_cp("arbitrary"),
    )(u0, u0, vg, cw, cb, wg, xres, gate, gam, bet)


def _ffn_core_bwd(dy, u0, vg, cw, cb, wg_row, wg_col, xin, sc, dres, name):
    s_len, f = u0.shape
    d = xin.shape[1]
    r = f // NDEV
    ktaps = cw.shape[0]
    pad = SHORT_PAD
    tm = _tile(s_len, 256)
    nsteps = s_len // tm
    cbk = 1024 if f % 1024 == 0 else f

    def body(dy_ref, dyn_ref, u_ref, up_ref, un_ref, vg_ref, vgn_ref, cw_ref, cb_ref, wgr_ref, wgc_ref,
             x_ref, sc_ref, dres_ref, du0_ref, dvg_ref, dw_ref, db_ref, dx_ref, acc_ref,
             ubuf, dtbuf, dubuf, dwacc, wd_ref, wup_ref, wgate_ref, sems):
        u_ref, up_ref, un_ref, vg_ref, vgn_ref = map(_F32Loads, (u_ref, up_ref, un_ref, vg_ref, vgn_ref))
        s = pl.program_id(0)
        last = s == nsteps - 1

        @pl.when(s == 0)
        def _():
            dwacc[...] = jnp.zeros_like(dwacc)
            db_ref[...] = jnp.zeros_like(db_ref)
            acc_ref[...] = jnp.zeros_like(acc_ref)
            _load_rows(wgr_ref, r, 0, wd_ref, sems.at[0])
            _load_cols(wgc_ref, 0, wup_ref, sems.at[1])
            _load_cols(wgc_ref, 1, wgate_ref, sems.at[2])

        ubuf[0:pad, :] = jnp.where(s > 0, up_ref[...], 0.0)
        ubuf[pad:pad + tm, :] = u_ref[...]
        ubuf[pad + tm:pad + tm + pad, :] = un_ref[...]
        dy_cur, dy_nxt = dy_ref[...], dyn_ref[...]
        dh = None
        for c0 in range(0, f, cbk):
            cols = slice(c0, c0 + cbk)
            wd_blk = wd_ref[cols, :]
            dtbuf[0:tm, :] = lax.dot_general(dy_cur, wd_blk, NT, preferred_element_type=F32)
            dtbuf[tm:tm + pad, :] = jnp.where(
                last, 0.0, lax.dot_general(dy_nxt, wd_blk, NT, preferred_element_type=F32))
            read_u = _direct(_Cols(ubuf, cols))
            for r0 in range(0, tm + pad, CHUNK):
                u = _conv_fwd_rows(read_u, _Cols(cw_ref, cols), _Cols(cb_ref, cols), ktaps, pad, r0, CHUNK)
                sg = _sigmoid(u)
                dtr = dtbuf[r0:r0 + CHUNK, :]
                if r0 < tm:
                    vgr = vg_ref[r0:r0 + CHUNK, cols]
                    dvg_ref[r0:r0 + CHUNK, cols] = (dtr * u * sg).astype(BF16)
                else:
                    vgr = vgn_ref[r0 - tm:r0 - tm + CHUNK, cols]
                dubuf[r0:r0 + CHUNK, :] = dtr * vgr * (sg * (1.0 + u * (1.0 - sg)))
            db_ref[:, cols] += jnp.sum(dubuf[0:tm, :], axis=0, keepdims=True)
            for r0 in range(0, tm, CHUNK):
                du0 = _conv_bwd_rows(_direct(dubuf), u_ref[r0:r0 + CHUNK, cols], _Cols(cw_ref, cols),
                                     _Cols(dwacc, cols), ktaps, r0, CHUNK)
                du0_ref[r0:r0 + CHUNK, cols] = du0.astype(BF16)
            p = (lax.dot_general(du0_ref[:, cols], wup_ref[:, cols], NT, preferred_element_type=F32)
                 + lax.dot_general(dvg_ref[:, cols], wgate_ref[:, cols], NT, preferred_element_type=F32))
            dh = p if dh is None else dh + p
        dx_ref[...] = dh * (1.0 + sc_ref[...]) + dres_ref[...]
        acc_ref[0:1, :] += jnp.sum(dh * x_ref[...], axis=0, keepdims=True)
        acc_ref[1:2, :] += jnp.sum(dh, axis=0, keepdims=True)

        @pl.when(last)
        def _():
            _flush_dw(dwacc, dw_ref, ktaps)

    wide = pl.BlockSpec((tm, f), lambda i: (i, 0))
    row = pl.BlockSpec((tm, d), lambda i: (i, 0))
    return pl.pallas_call(
        body, name=name, grid=(nsteps,),
        in_specs=[row, _next_spec(tm, pad, d, lambda g: 0, s_len),
                  wide, _prev_spec(tm, pad, f, lambda g: 0), _next_spec(tm, pad, f, lambda g: 0, s_len),
                  wide, _next_spec(tm, pad, f, lambda g: 0, s_len),
                  _full((ktaps, f)), _full((1, f)), ANY_SPEC, ANY_SPEC, row, _full((1, d)), row],
        out_specs=[wide, wide, _full((ktaps, f)), _full((1, f)), row, _full((SUBLANES, d))],
        out_shape=[jax.ShapeDtypeStruct((s_len, f), BF16), jax.ShapeDtypeStruct((s_len, f), BF16),
                   jax.ShapeDtypeStruct((ktaps, f), F32), jax.ShapeDtypeStruct((1, f), F32),
                   jax.ShapeDtypeStruct((s_len, d), F32), jax.ShapeDtypeStruct((SUBLANES, d), F32)],
        scratch_shapes=[pltpu.VMEM((pad + tm + pad, f), F32), pltpu.VMEM((tm + pad, cbk), F32),
                        pltpu.VMEM((tm + pad, cbk), F32), pltpu.VMEM((ktaps * SUBLANES, f), F32),
                        pltpu.VMEM((f, d), BF16), pltpu.VMEM((d, f), BF16), pltpu.VMEM((d, f), BF16),
                        pltpu.SemaphoreType.DMA((3, NDEV))],
        compiler_params=_cp("arbitrary"),
    )(dy, dy, u0, u0, u0, vg, vg, cw, cb, wg_row, wg_col, xin, sc, dres)


def _ffn_mid_bwd(u0, vg, dt, cw, cb, name):
    s_len, f = u0.shape
    ktaps = cw.shape[0]
    pad = SHORT_PAD
    ts = _tile(s_len, 256)
    nsteps = s_len // ts
    cbk = 1024 if f % 1024 == 0 else f

    def body(u_ref, up_ref, un_ref, vg_ref, vgn_ref, dt_ref, dtn_ref, w_ref, b_ref,
             du0_ref, dvg_ref, dw_ref, db_ref, ubuf, dubuf, dwacc):
        u_ref, up_ref, un_ref, vg_ref, vgn_ref, dt_ref, dtn_ref = map(
            _F32Loads, (u_ref, up_ref, un_ref, vg_ref, vgn_ref, dt_ref, dtn_ref))
        s = pl.program_id(1)

        @pl.when(s == 0)
        def _():
            dwacc[...] = jnp.zeros_like(dwacc)
            db_ref[...] = jnp.zeros_like(db_ref)

        ubuf[0:pad, :] = jnp.where(s > 0, up_ref[...], 0.0)
        ubuf[pad:pad + ts, :] = u_ref[...]
        ubuf[pad + ts:pad + ts + pad, :] = un_ref[...]
        last = s == nsteps - 1
        for r0 in range(0, ts + pad, CHUNK):
            u = _conv_fwd_rows(_direct(ubuf), w_ref, b_ref, ktaps, pad, r0, CHUNK)
            sg = _sigmoid(u)
            if r0 < ts:
                rows = slice(r0, r0 + CHUNK)
                dtr, vgr = dt_ref[rows, :], vg_ref[rows, :]
                dvg_ref[rows, :] = (dtr * u * sg).astype(BF16)
            else:
                rows = slice(r0 - ts, r0 - ts + CHUNK)
                dtr, vgr = jnp.where(last, 0.0, dtn_ref[rows, :]), vgn_ref[rows, :]
            dubuf[r0:r0 + CHUNK, :] = dtr * vgr * (sg * (1.0 + u * (1.0 - sg)))
        db_ref[...] += jnp.sum(dubuf[0:ts, :], axis=0, keepdims=True)
        for r0 in range(0, ts, CHUNK):
            du0 = _conv_bwd_rows(_direct(dubuf), u_ref[r0:r0 + CHUNK, :], w_ref, dwacc, ktaps, r0, CHUNK)
            du0_ref[r0:r0 + CHUNK, :] = du0.astype(BF16)

        @pl.when(last)
        def _():
            _flush_dw(dwacc, dw_ref, ktaps)

    cur = pl.BlockSpec((ts, cbk), lambda c, s: (s, c))
    prv = _prev_spec(ts, pad, cbk, lambda g: g[0])
    nxt = _next_spec(ts, pad, cbk, lambda g: g[0], s_len)
    return pl.pallas_call(
        body, name=name, grid=(f // cbk, nsteps),
        in_specs=[cur, prv, nxt, cur, nxt, cur, nxt,
                  pl.BlockSpec((ktaps, cbk), lambda c, s: (0, c)), pl.BlockSpec((1, cbk), lambda c, s: (0, c))],
        out_specs=[cur, cur, pl.BlockSpec((ktaps, cbk), lambda c, s: (0, c)),
                   pl.BlockSpec((1, cbk), lambda c, s: (0, c))],
        out_shape=[jax.ShapeDtypeStruct((s_len, f), BF16), jax.ShapeDtypeStruct((s_len, f), BF16),
                   jax.ShapeDtypeStruct((ktaps, f), F32), jax.ShapeDtypeStruct((1, f), F32)],
        scratch_shapes=[pltpu.VMEM((pad + ts + pad, cbk), F32), pltpu.VMEM((ts + pad, cbk), F32),
                        pltpu.VMEM((ktaps * SUBLANES, cbk), F32)],
        compiler_params=_cp("parallel", "arbitrary"),
    )(u0, u0, u0, vg, vg, dt, dt, cw, cb)


def _b_mid_fwd(ub, cw, cb, lng, lnb, name):
    s_len, d2 = ub.shape
    d = d2 // 2
    ktaps = cw.shape[0]
    pad = LONG_PAD
    ts = _tile(s_len, 256)

    def body(a_ref, g_ref, ap_ref, gp_ref, w_ref, b_ref, lng_ref, lnb_ref, a2_ref, a4_ref, abuf):
        a_ref, g_ref, ap_ref, gp_ref = map(_F32Loads, (a_ref, g_ref, ap_ref, gp_ref))
        s = pl.program_id(0)
        abuf[0, 0:pad, :] = jnp.where(s > 0, ap_ref[...] * _sigmoid(gp_ref[...]), 0.0)
        abuf[0, pad:pad + ts, :] = a_ref[...] * _sigmoid(g_ref[...])
        _make_shifts(abuf, pad + ts)
        for r0 in range(0, ts, CHUNK):
            a2_ref[r0:r0 + CHUNK, :] = _conv_fwd_rows(_shifted(abuf), w_ref, b_ref, ktaps, pad, r0, CHUNK)
        a2 = a2_ref[...]
        mu = jnp.mean(a2, axis=-1, keepdims=True)
        ac = a2 - mu
        var = jnp.mean(ac * ac, axis=-1, keepdims=True)
        a3 = ac * lax.rsqrt(var + LN_EPS) * lng_ref[...] + lnb_ref[...]
        a4_ref[...] = (a3 * _sigmoid(a3)).astype(BF16)

    def cur(part):
        return pl.BlockSpec((ts, d), lambda s: (s, part))

    vec = _full((1, d))
    return pl.pallas_call(
        body, name=name, grid=(s_len // ts,),
        in_specs=[cur(0), cur(1), _prev_spec(ts, pad, d, lambda g: 0), _prev_spec(ts, pad, d, lambda g: 1),
                  _full((ktaps, d)), vec, vec, vec],
        out_specs=[cur(0), cur(0)],
        out_shape=[jax.ShapeDtypeStruct((s_len, d), F32), jax.ShapeDtypeStruct((s_len, d), BF16)],
        scratch_shapes=[pltpu.VMEM((SUBLANES, pad + ts, d), F32)],
        compiler_params=_cp("parallel"),
    )(ub, ub, ub, ub, cw, cb, lng, lnb)


def _b_mid_bwd(ub, a2, da4, cw, lng, lnb, name):
    s_len, d2 = ub.shape
    d = d2 // 2
    ktaps = cw.shape[0]
    pad = LONG_PAD
    ts = _tile(s_len, 256)
    nsteps = s_len // ts

    def body(a_ref, g_ref, a2_ref, a2n_ref, da4_ref, da4n_ref, w_ref, lng_ref, lnb_ref,
             du_ref, dw_ref, db_ref, dlng_ref, dlnb_ref, dbias_ref, dabuf, dwacc):
        a_ref, g_ref, da4_ref, da4n_ref = map(_F32Loads, (a_ref, g_ref, da4_ref, da4n_ref))
        s = pl.program_id(0)
        last = s == nsteps - 1

        @pl.when(s == 0)
        def _():
            dwacc[...] = jnp.zeros_like(dwacc)
            for ref in (db_ref, dlng_ref, dlnb_ref, dbias_ref):
                ref[...] = jnp.zeros_like(ref)

        def ln_silu_bwd(a2_t, da4_t):
            mu = jnp.mean(a2_t, axis=-1, keepdims=True)
            ac = a2_t - mu
            var = jnp.mean(ac * ac, axis=-1, keepdims=True)
            rstd = lax.rsqrt(var + LN_EPS)
            ah = ac * rstd
            a3 = ah * lng_ref[...] + lnb_ref[...]
            sg = _sigmoid(a3)
            da3 = da4_t * (sg * (1.0 + a3 * (1.0 - sg)))
            dah = da3 * lng_ref[...]
            m1 = jnp.mean(dah, axis=-1, keepdims=True)
            m2 = jnp.mean(dah * ah, axis=-1, keepdims=True)
            return rstd * (dah - m1 - ah * m2), da3, ah

        da2, da3, ah = ln_silu_bwd(a2_ref[...], da4_ref[...])
        dabuf[0, 0:ts, :] = da2
        dlng_ref[...] += jnp.sum(da3 * ah, axis=0, keepdims=True)
        dlnb_ref[...] += jnp.sum(da3, axis=0, keepdims=True)
        db_ref[...] += jnp.sum(da2, axis=0, keepdims=True)
        da2n, _, _ = ln_silu_bwd(a2n_ref[...], jnp.where(last, 0.0, da4n_ref[...]))
        dabuf[0, ts:ts + pad, :] = da2n
        _make_shifts(dabuf, ts + pad)
        for r0 in range(0, ts, CHUNK):
            rows = slice(r0, r0 + CHUNK)
            a_r, g_r = a_ref[rows, :], g_ref[rows, :]
            sg = _sigmoid(g_r)
            da1 = _conv_bwd_rows(_shifted(dabuf), a_r * sg, w_ref, dwacc, ktaps, r0, CHUNK)
            da = da1 * sg
            dg = da1 * a_r * sg * (1.0 - sg)
            du_ref[rows, 0:d] = da.astype(BF16)
            du_ref[rows, d:2 * d] = dg.astype(BF16)
            dbias_ref[:, 0:d] += jnp.sum(da, axis=0, keepdims=True)
            dbias_ref[:, d:2 * d] += jnp.sum(dg, axis=0, keepdims=True)

        @pl.when(last)
        def _():
            _flush_dw(dwacc, dw_ref, ktaps)

    def cur(part):
        return pl.BlockSpec((ts, d), lambda s: (s, part))

    vec = _full((1, d))
    nxt = _next_spec(ts, pad, d, lambda g: 0, s_len)
    return pl.pallas_call(
        body, name=name, grid=(nsteps,),
        in_specs=[cur(0), cur(1), cur(0), nxt, cur(0), nxt, _full((ktaps, d)), vec, vec],
        out_specs=[pl.BlockSpec((ts, d2), lambda s: (s, 0)), _full((ktaps, d)), vec, vec, vec, _full((1, d2))],
        out_shape=[jax.ShapeDtypeStruct((s_len, d2), BF16), jax.ShapeDtypeStruct((ktaps, d), F32),
                   jax.ShapeDtypeStruct((1, d), F32), jax.ShapeDtypeStruct((1, d), F32),
                   jax.ShapeDtypeStruct((1, d), F32), jax.ShapeDtypeStruct((1, d2), F32)],
        scratch_shapes=[pltpu.VMEM((SUBLANES, ts + pad, d), F32), pltpu.VMEM((ktaps * SUBLANES, d), F32)],
        compiler_params=_cp("arbitrary"),
    )(ub, ub, a2, a2, da4, da4, cw, lng, lnb)


def _loss_head(xo, tgt, name):
    s_len, d = xo.shape
    tm = _tile(s_len, 512)

    def body(x_ref, t_ref, d_ref, l_ref):
        @pl.when(pl.program_id(0) == 0)
        def _():
            l_ref[...] = jnp.zeros_like(l_ref)

        e = x_ref[...] - t_ref[...]
        d_ref[...] = e * (1.0 / d)
        per_row = jnp.sum(e * e, axis=-1, keepdims=True) * (1.0 / d)
        l_ref[...] += 0.5 * jnp.sum(per_row, axis=0, keepdims=True)

    row = pl.BlockSpec((tm, d), lambda i: (i, 0))
    return pl.pallas_call(
        body, name=name, grid=(s_len // tm,),
        in_specs=[row, row], out_specs=[row, _full((1, LANES))],
        out_shape=[jax.ShapeDtypeStruct((s_len, d), F32), jax.ShapeDtypeStruct((1, LANES), F32)],
        compiler_params=_cp("arbitrary"),
    )(xo, tgt)


def _ada_fwd(c_all, ada_w, ada_b_loc, name):
    depth, d, n = ada_w.shape

    def body(c_ref, w_ref, b_ref, o_ref):
        c = c_ref[...]
        act = c * _sigmoid(c)
        o_ref[...] = jnp.dot(act, w_ref[...], preferred_element_type=F32,
                             precision=lax.Precision.HIGHEST) + b_ref[...]

    return pl.pallas_call(
        body, name=name, grid=(depth,),
        in_specs=[_full((NDEV, d)), pl.BlockSpec((None, d, n), lambda i: (i, 0, 0)),
                  pl.BlockSpec((None, 1, n), lambda i: (i, 0, 0))],
        out_specs=pl.BlockSpec((None, NDEV, n), lambda i: (i, 0, 0)),
        out_shape=jax.ShapeDtypeStruct((depth, NDEV, n), F32),
        compiler_params=_cp("parallel"),
    )(c_all, ada_w, ada_b_loc.reshape(depth, 1, n))


def _ada_bwd(c_all_t, dmod_cols, name):
    depth, _, n = dmod_cols.shape
    d = c_all_t.shape[0]

    def body(ct_ref, dm_ref, o_ref):
        ct = ct_ref[...]
        act = ct * _sigmoid(ct)
        acc = None
        for b in range(NDEV):
            term = act[:, b:b + 1] * dm_ref[b:b + 1, :]
            acc = term if acc is None else acc + term
        o_ref[...] = acc

    return pl.pallas_call(
        body, name=name, grid=(depth,),
        in_specs=[_full((d, NDEV)), pl.BlockSpec((None, NDEV, n), lambda i: (i, 0, 0))],
        out_specs=pl.BlockSpec((None, d, n), lambda i: (i, 0, 0)),
        out_shape=jax.ShapeDtypeStruct((depth, d, n), F32),
        compiler_params=_cp("parallel"),
    )(c_all_t, dmod_cols)


def _sum_parts(parts, name):
    _, rows, lanes = parts.shape

    def body(p_ref, o_ref):
        acc = p_ref[0]
        for k in range(1, NDEV):
            acc = acc + p_ref[k]
        o_ref[...] = acc

    return pl.pallas_call(
        body, name=name, in_specs=[_full(parts.shape)], out_specs=_full((rows, lanes)), grid=(1,),
        out_shape=jax.ShapeDtypeStruct((rows, lanes), F32), compiler_params=_cp("arbitrary"),
    )(parts)


def _adamw(w, glist, m, v, name):
    nl, rows, cols = w.shape
    tr = _tile(rows, 256, 2 * SUBLANES)

    def body(w_ref, *rest):
        g_refs = rest[:nl]
        m_ref, v_ref, go_ref, d_ref, mo_ref, vo_ref = rest[nl:]
        g = None
        for layer, g_ref in enumerate(g_refs):
            part = g_ref[0].astype(F32)
            for p in range(1, g_ref.shape[0]):
                part = part + g_ref[p].astype(F32)
            g = part if g is None else jnp.where(pl.program_id(0) == layer, part, g)
        m1 = ADAM_B1 * m_ref[...] + (1.0 - ADAM_B1) * g
        v1 = ADAM_B2 * v_ref[...] + (1.0 - ADAM_B2) * (g * g)
        m_hat = m1 / (1.0 - ADAM_B1 ** ADAM_STEP)
        v_hat = v1 / (1.0 - ADAM_B2 ** ADAM_STEP)
        go_ref[...] = g
        mo_ref[...] = m1
        vo_ref[...] = v1
        d_ref[...] = -ADAM_LR * (m_hat / (jnp.sqrt(v_hat) + ADAM_EPS) + ADAM_WD * w_ref[...])

    blk = pl.BlockSpec((None, tr, cols), lambda l, i: (l, i, 0))
    g_specs = [pl.BlockSpec((g.shape[0], tr, cols), lambda l, i: (0, i, 0)) for g in glist]
    return pl.pallas_call(
        body, name=name, grid=(nl, rows // tr),
        in_specs=[blk] + g_specs + [blk, blk],
        out_specs=[blk] * 4, out_shape=[jax.ShapeDtypeStruct((nl, rows, cols), F32)] * 4,
        compiler_params=_cp("parallel", "parallel"),
    )(w, *glist, m, v)


def _pack(pieces):
    flat = jnp.concatenate([p.reshape(-1) for p in pieces])
    unit = SUBLANES * LANES
    padded = -(-flat.shape[0] // unit) * unit
    return jnp.pad(flat, (0, padded - flat.shape[0])).reshape(padded // LANES, LANES)


def _unpack(packed, shapes, lead=()):
    flat = packed.reshape(lead + (-1,))
    out, off = [], 0
    for s in shapes:
        size = 1
        for dim in s:
            size *= dim
        out.append(flat[..., off:off + size].reshape(lead + tuple(s)))
        off += size
    return out


def _pad_last(a, n):
    return jnp.pad(a, [(0, 0)] * (a.ndim - 1) + [(0, n - a.shape[-1])])


def kernel(x, c, ada_w, ada_b, ln_tok_g, ln_tok_b, ln_ch_g, ln_ch_b, a_w_in, a_conv_w, a_conv_b, a_w_out, b_w_pw1, b_b_pw1, b_conv_w, b_conv_b, b_ln_g, b_ln_b, b_w_pw2, b_b_pw2, f_w_up, f_conv_w, f_conv_b, f_w_gate, f_w_down, loss_target, m_ada_w, m_ada_b, m_ln_tok_g, m_ln_tok_b, m_ln_ch_g, m_ln_ch_b, m_a_w_in, m_a_conv_w, m_a_conv_b, m_a_w_out, m_b_w_pw1, m_b_b_pw1, m_b_conv_w, m_b_conv_b, m_b_ln_g, m_b_ln_b, m_b_w_pw2, m_b_b_pw2, m_f_w_up, m_f_conv_w, m_f_conv_b, m_f_w_gate, m_f_w_down, v_ada_w, v_ada_b, v_ln_tok_g, v_ln_tok_b, v_ln_ch_g, v_ln_ch_b, v_a_w_in, v_a_conv_w, v_a_conv_b, v_a_w_out, v_b_w_pw1, v_b_b_pw1, v_b_conv_w, v_b_conv_b, v_b_ln_g, v_b_ln_b, v_b_w_pw2, v_b_b_pw2, v_f_w_up, v_f_conv_w, v_f_conv_b, v_f_w_gate, v_f_w_down):
    weights = dict(ada_w=ada_w, ada_b=ada_b, ln_tok_g=ln_tok_g, ln_tok_b=ln_tok_b, ln_ch_g=ln_ch_g, ln_ch_b=ln_ch_b, a_w_in=a_w_in, a_conv_w=a_conv_w, a_conv_b=a_conv_b, a_w_out=a_w_out, b_w_pw1=b_w_pw1, b_b_pw1=b_b_pw1, b_conv_w=b_conv_w, b_conv_b=b_conv_b, b_ln_g=b_ln_g, b_ln_b=b_ln_b, b_w_pw2=b_w_pw2, b_b_pw2=b_b_pw2, f_w_up=f_w_up, f_conv_w=f_conv_w, f_conv_b=f_conv_b, f_w_gate=f_w_gate, f_w_down=f_w_down)
    mom_m = dict(ada_w=m_ada_w, ada_b=m_ada_b, ln_tok_g=m_ln_tok_g, ln_tok_b=m_ln_tok_b, ln_ch_g=m_ln_ch_g, ln_ch_b=m_ln_ch_b, a_w_in=m_a_w_in, a_conv_w=m_a_conv_w, a_conv_b=m_a_conv_b, a_w_out=m_a_w_out, b_w_pw1=m_b_w_pw1, b_b_pw1=m_b_b_pw1, b_conv_w=m_b_conv_w, b_conv_b=m_b_conv_b, b_ln_g=m_b_ln_g, b_ln_b=m_b_ln_b, b_w_pw2=m_b_w_pw2, b_b_pw2=m_b_b_pw2, f_w_up=m_f_w_up, f_conv_w=m_f_conv_w, f_conv_b=m_f_conv_b, f_w_gate=m_f_w_gate, f_w_down=m_f_w_down)
    mom_v = dict(ada_w=v_ada_w, ada_b=v_ada_b, ln_tok_g=v_ln_tok_g, ln_tok_b=v_ln_tok_b, ln_ch_g=v_ln_ch_g, ln_ch_b=v_ln_ch_b, a_w_in=v_a_w_in, a_conv_w=v_a_conv_w, a_conv_b=v_a_conv_b, a_w_out=v_a_w_out, b_w_pw1=v_b_w_pw1, b_b_pw1=v_b_b_pw1, b_conv_w=v_b_conv_w, b_conv_b=v_b_conv_b, b_ln_g=v_b_ln_g, b_ln_b=v_b_ln_b, b_w_pw2=v_b_w_pw2, b_b_pw2=v_b_b_pw2, f_w_up=v_f_w_up, f_conv_w=v_f_conv_w, f_conv_b=v_f_conv_b, f_w_gate=v_f_w_gate, f_w_down=v_f_w_down)
    names = list(weights)

    depth, d, n_ada = ada_w.shape
    assert depth == 2 and a_w_in.shape[0] == 1 and b_w_pw1.shape[0] == 1
    s_len = x.shape[1]
    f_loc = f_w_up.shape[-1]
    f_pad = -(-f_loc // LANES) * LANES
    f_all = NDEV * f_pad
    d_loc = d // NDEV
    ka, kb, kf = a_conv_w.shape[1], b_conv_w.shape[1], f_conv_w.shape[1]
    alpha = (2.0 * depth) ** 0.25
    assert a_w_in.shape[-1] == f_pad and f_pad % d_loc == 0
    me = 4 * lax.axis_index("x") + 2 * lax.axis_index("y") + lax.axis_index("c")

    small_shapes = [(d,), (ka, d_loc), (2 * d_loc,), (kb, d_loc), (d_loc,), (d_loc,), (d_loc,), (d_loc,),
                    (depth, kf, f_pad)]
    small_loc = _pack([c[0], a_conv_w[0], b_b_pw1[0], b_conv_w[0], b_conv_b[0], b_ln_g[0], b_ln_b[0],
                       b_b_pw2[0], _pad_last(f_conv_w, f_pad)])
    g_small, g_in, _ = _gather_two_level([small_loc, a_w_in.astype(BF16)], small_loc, "gather_first")

    (c_all, acw_g, bb1_g, bcw_g, bcb_g, blg_g, blb_g, bb2_g, fcw_g) = _unpack(g_small, small_shapes, (NDEV,))
    a_cw = acw_g.transpose(1, 0, 2).reshape(ka, d)
    b_cw = bcw_g.transpose(1, 0, 2).reshape(kb, d)
    b_b1 = bb1_g.reshape(1, 2 * d)
    b_cb, b_lg, b_lb, b_b2 = (t.reshape(1, d) for t in (bcb_g, blg_g, blb_g, bb2_g))
    f_cw = fcw_g.transpose(1, 2, 0, 3).reshape(depth, kf, f_all)
    f_cb = _pad_last(f_conv_b.reshape(depth, NDEV, f_loc), f_pad).reshape(depth, 1, f_all)

    ada_b_loc = lax.dynamic_slice(ada_b, (0, me * n_ada), (depth, n_ada))
    mod_part = _ada_fwd(c_all, ada_w, ada_b_loc, "ada_fwd")
    mod_g, mod_done = _exchange([mod_part.reshape(depth * NDEV, n_ada)], "gather", "gather_mod")
    mod_all = mod_g.reshape(NDEV, depth, NDEV, n_ada).transpose(1, 2, 0, 3).reshape(depth, NDEV, 6 * d)
    mod = lax.dynamic_slice(mod_all, (0, me, 0), (depth, 1, 6 * d))[:, 0]

    gather_out = _exchange_start([_after(a_w_out[0], mod_done).astype(BF16)], "gather_chips", "gather_out_start")
    up_pad = _pad_last(_after(f_w_up, gather_out[-1]), f_pad).astype(BF16)
    gate_pad = _pad_last(f_w_gate, f_pad).astype(BF16)
    down_pad = jnp.pad(f_w_down, ((0, 0), (0, f_pad - f_loc), (0, 0))).astype(BF16)
    col_f = [jnp.stack([up_pad[i], gate_pad[i]]) for i in range(depth)]
    row_b = jnp.concatenate([down_pad[1], b_w_pw2[0].astype(BF16)], axis=0)
    ridx_pw2 = f_pad // d_loc
    gather_f0 = _exchange_start([col_f[0], down_pad[0]], "gather_chips", "gather_f0_start")

    def mod_rows(i):
        return [mod[i:i + 1, j * d:(j + 1) * d] for j in range(6)]

    zeros_d = jnp.zeros((1, d), F32)
    zeros_f = jnp.zeros((1, f_all), F32)
    x0 = x[0]

    sh_t0, sc_t0, g_t0, sh_c0, sc_c0, g_c0 = mod_rows(0)
    sh_t1, sc_t1, g_t1, sh_c1, sc_c1, g_c1 = mod_rows(1)

    sc_t0 = _after(sc_t0, gather_f0[-1])
    bcv, = _mm_fwd(x0, sc_t0, sh_t0, jnp.zeros((1, 3 * d), F32), g_in, (0,), "a_in_fwd")
    y0 = _gateconv_fwd(bcv, a_cw, a_conv_b, "a_conv_fwd")
    g_out, landed = _exchange_wait(gather_out, y0, "gather_chips", "gather_out_wait")
    g_out, _ = _exchange_wait(_exchange_start([g_out], "forward", "gather_out_fwd_start"), landed, "forward",
                              "gather_out_fwd_wait")
    y_a, x1, xh1, rs1 = _mm_ln(y0, g_out, d_loc, 0, x0, g_t0, ln_tok_g[0:1], ln_tok_b[0:1], zeros_d,
                               alpha, "a_out_ln_fwd")

    def ffn_fwd(xin, sc, sh, gate, gam, bet, g_colf, g_rowf, layer, tag):
        u0, vg = _mm_fwd(xin, sc, sh, zeros_f, g_colf, (0, 1), "f_upgate_fwd" + tag)
        t, y, xo, xh, rs = _ffn_tail_fwd(u0, vg, f_cw[layer], f_cb[layer], g_rowf, xin, gate, gam, bet, alpha,
                                         "f_tail_fwd" + tag)
        return u0, vg, t, y, xo, xh, rs

    g_colf0, g_rowf0, landed = _exchange_wait(gather_f0, x1, "gather_chips", "gather_f0_wait")
    g_colf0, g_rowf0, landed = _exchange_wait(
        _exchange_start([g_colf0, g_rowf0], "forward", "gather_f0_fwd_start"), landed, "forward", "gather_f0_fwd_wait")
    gather_1 = _exchange_start([_after(b_w_pw1, landed).astype(BF16), col_f[1], row_b], "gather_chips",
                               "gather_1_start")
    sc_c0 = _after(sc_c0, gather_1[-1])
    u0_0, vg_0, t_0, y_f0, x2, xh2, rs2 = ffn_fwd(x1, sc_c0, sh_c0, g_c0, ln_ch_g[0:1], ln_ch_b[0:1],
                                                  g_colf0, g_rowf0, 0, "0")

    *lands_1, landed = _exchange_wait(gather_1, x2, "gather_chips", "gather_1_wait")
    g_pw1, g_colf1, g_rowb, _ = _exchange_wait(_exchange_start(lands_1, "forward", "gather_1_fwd_start"), landed,
                                                 "forward", "gather_1_fwd_wait")
    ub, = _mm_fwd(x2, sc_t1, sh_t1, b_b1, g_pw1, (0,), "b_pw1_fwd")
    a2, a4 = _b_mid_fwd(ub, b_cw, b_cb, b_lg, b_lb, "b_mid_fwd")
    y_b, x3, xh3, rs3 = _mm_ln(a4, g_rowb, d_loc, ridx_pw2, x2, g_t1, ln_tok_g[1:2], ln_tok_b[1:2], b_b2,
                               alpha, "b_pw2_ln_fwd")
    u0_1, vg_1, t_1, y_f1, x4, xh4, rs4 = ffn_fwd(x3, sc_c1, sh_c1, g_c1, ln_ch_g[1:2], ln_ch_b[1:2],
                                                  g_colf1, g_rowb, 1, "1")

    dx4, loss_part = _loss_head(x4, loss_target[0], "loss_head")

    def ffn_bwd(dxo, xin, sc, sh, gate, gam, u0, vg, t, y, xh, rs, g_colf, g_rowf, layer, tag):
        dy, dres, acc = _ln_bwd(dxo, xh, rs, gam, y, gate, alpha, "f_ln_bwd" + tag)
        dw_down = _mm_tn_row(t, dy, f_pad, f_loc, "f_down_dw" + tag)
        du0, dvg, dcw, dcb, dxin, acc2 = _ffn_core_bwd(dy, u0, vg, f_cw[layer], f_cb[layer], g_rowf, g_colf,
                                                       xin, sc, dres, "f_core_bwd" + tag)
        dw_up = _mm_tn_col_t(xin, sc, sh, du0, f_loc, "f_up_dw" + tag)
        dw_gate = _mm_tn_col_t(xin, sc, sh, dvg, f_loc, "f_gate_dw" + tag)
        scatter = _exchange_start([dw_up, dw_gate, dw_down], "scatter", "scatter_f%s_start" % tag)
        return dxin, acc, acc2, scatter, dcw, dcb

    dx3, accf1, acc2f1, scatter_f1, dfcw1, dfcb1 = ffn_bwd(
        dx4, x3, sc_c1, sh_c1, g_c1, ln_ch_g[1:2], u0_1, vg_1, t_1, y_f1, xh4, rs4, g_colf1, g_rowb, 1, "1")

    dy, dres, accb = _ln_bwd(dx3, xh3, rs3, ln_tok_g[1:2], y_b, _after(g_t1, scatter_f1[-1]), alpha, "b_ln_bwd")
    da4 = _mm_nt_row(dy, g_rowb, d_loc, ridx_pw2, "b_pw2_dx")
    dw_pw2 = _mm_tn_row(a4, dy, d_loc, d_loc, "b_pw2_dw")
    du, dbcw, dbcb, dblg, dblb, dbb1 = _b_mid_bwd(ub, a2, da4, b_cw, b_lg, b_lb, "b_mid_bwd")
    dw_pw1 = _mm_tn_col(x2, sc_t1, sh_t1, du, "b_pw1_dw")
    scatter_b = _exchange_start([dw_pw1, dw_pw2], "scatter", "scatter_b_start")
    dx2, acc2b = _mm_nt_mod([du], g_pw1, (0,), x2, _after(sc_t1, scatter_b[-1]), dres, "b_pw1_dx")

    dx1, accf0, acc2f0, scatter_f0, dfcw0, dfcb0 = ffn_bwd(
        dx2, x1, sc_c0, sh_c0, g_c0, ln_ch_g[0:1], u0_0, vg_0, t_0, y_f0, xh2, rs2, g_colf0, g_rowf0, 0, "0")

    dy, dres, acca = _ln_bwd(dx1, xh1, rs1, ln_tok_g[0:1], y_a, _after(g_t0, scatter_f0[-1]), alpha, "a_ln_bwd")
    dy0 = _mm_nt_row(dy, g_out, d_loc, 0, "a_out_dx")
    dbcv, dacw, dacb = _gateconv_bwd(bcv, dy0, a_cw, a_conv_b, "a_conv_bwd")
    dx0, acc2a = _mm_nt_mod([dbcv], g_in, (0,), x0, sc_t0, dres, "a_in_dx")

    def dmod_row(acc2_t, acc_t, acc2_c, acc_c):
        return jnp.concatenate([acc2_t[1], acc2_t[0], acc_t[2], acc2_c[1], acc2_c[0], acc_c[2]])

    dmod = jnp.stack([dmod_row(acc2a, acca, acc2f0, accf0), dmod_row(acc2b, accb, acc2f1, accf1)])

    def unpad_f(a):
        return a.reshape(a.shape[:-1] + (NDEV, f_pad))[..., :f_loc].reshape(a.shape[:-1] + (NDEV * f_loc,))

    small_grads = [
        dmod,
        jnp.stack([acca[0], accb[0]]), jnp.stack([acca[1], accb[1]]),
        jnp.stack([accf0[0], accf1[0]]), jnp.stack([accf0[1], accf1[1]]),
        dacb,
        unpad_f(jnp.concatenate([dfcb0, dfcb1], axis=0)),
        dacw, dbb1, dbcw, dbcb, dblg, dblb, accb[3:4],
        jnp.stack([dfcw0, dfcw1]),
        loss_part[0:1, 0:1],
    ]
    small_grad_shapes = [tuple(g.shape) for g in small_grads]
    gather_small = _exchange_start([_pack(small_grads)], "gather", "gather_small_start")

    dw_in = _mm_tn_col(x0, _after(sc_t0, gather_small[-1]), sh_t0, dbcv, "a_in_dw")
    dw_out = _mm_tn_row(y0, dy, d_loc, d_loc, "a_out_dw")
    scatter_a = _exchange_start([dw_in, dw_out], "scatter", "scatter_a_start")

    grads, deltas, new_m, new_v = {}, {}, {}, {}

    def adamw(k, glist, transposed=False):
        def view(a):
            a = jnp.swapaxes(a, 1, 2) if transposed else a
            return a.reshape(len(glist), -1, a.shape[-1])

        w = view(weights[k])
        outs = _adamw(w, [g.reshape(g.shape[0], -1, w.shape[-1]) for g in glist],
                      view(mom_m[k]), view(mom_v[k]), "adamw_" + k)
        if transposed:
            outs = [jnp.swapaxes(o, 1, 2) for o in outs]
        grads[k], deltas[k], new_m[k], new_v[k] = (o.reshape(weights[k].shape) for o in outs)

    r_up1, r_gate1, r_down1, _ = _exchange_wait(scatter_f1, scatter_a[-1], "scatter", "scatter_f1_wait")
    r_pw1, r_pw2, _ = _exchange_wait(scatter_b, r_down1, "scatter", "scatter_b_wait")
    adamw("b_w_pw1", [r_pw1])
    adamw("b_w_pw2", [r_pw2])
    r_up0, r_gate0, r_down0, _ = _exchange_wait(scatter_f0, deltas["b_w_pw2"], "scatter", "scatter_f0_wait")
    adamw("f_w_up", [r_up0, r_up1], transposed=True)
    adamw("f_w_gate", [r_gate0, r_gate1], transposed=True)
    adamw("f_w_down", [r_down0, r_down1])

    sg_all, _ = _exchange_wait(gather_small, deltas["f_w_down"], "gather", "gather_small_wait")
    sg_sum = _sum_parts(sg_all, "sum_small_grads")
    (g_ada_b, g_ltg, g_ltb, g_lcg, g_lcb, g_acb, g_fcb, g_acw, g_bb1, g_bcw, g_bcb, g_blg, g_blb, g_bb2,
     g_fcw, loss_all) = _unpack(sg_sum, small_grad_shapes)
    loss = loss_all[0, 0]

    def my_cols(a, width):
        return lax.dynamic_slice_in_dim(a, me * width, width, axis=a.ndim - 1)

    g_fcw_loc = my_cols(g_fcw, f_pad)[..., :f_loc]
    small = dict(
        ada_b=g_ada_b, ln_tok_g=g_ltg, ln_tok_b=g_ltb, ln_ch_g=g_lcg, ln_ch_b=g_lcb, a_conv_b=g_acb, f_conv_b=g_fcb,
        a_conv_w=my_cols(g_acw, d_loc)[None], b_b_pw1=my_cols(g_bb1, 2 * d_loc), b_conv_w=my_cols(g_bcw, d_loc)[None],
        b_conv_b=my_cols(g_bcb, d_loc), b_ln_g=my_cols(g_blg, d_loc), b_ln_b=my_cols(g_blb, d_loc),
        b_b_pw2=my_cols(g_bb2, d_loc), f_conv_w=g_fcw_loc)

    dmod_all = sg_all.reshape(NDEV, -1)[:, :depth * 6 * d].reshape(NDEV, depth, 6 * d)
    dmod_cols = my_cols(dmod_all, n_ada).transpose(1, 0, 2)
    g_ada_w = _ada_bwd(c_all.T, dmod_cols, "ada_bwd")

    adamw("ada_w", [g_ada_w[0:1], g_ada_w[1:2]])
    for k, g in small.items():
        adamw(k, [g[None]])

    r_in, r_out, _ = _exchange_wait(scatter_a, deltas["ada_w"], "scatter", "scatter_a_wait")
    adamw("a_w_in", [r_in])
    adamw("a_w_out", [r_out])

    return (loss, dx0[None], *[grads[k] for k in names], *[deltas[k] for k in names],
            *[new_m[k] for k in names], *[new_v[k] for k in names])
```

```python
import functools

import jax
import jax.numpy as jnp
from jax import lax
from jax.experimental import pallas as pl
from jax.experimental.pallas import tpu as pltpu

NDEV = 8
MESH_AXES = ("x", "y", "c")
LANES = 128
SUBLANES = 8
VMEM_LIMIT = 56 * 1024 * 1024
LN_EPS = 1e-5
SHORT_PAD = 16
LONG_PAD = 32
CHUNK = 16
ADAM_LR, ADAM_B1, ADAM_B2, ADAM_EPS, ADAM_WD, ADAM_STEP = 0.001, 0.9, 0.999, 1e-08, 0.01, 10

F32 = jnp.float32
BF16 = jnp.bfloat16
MESH = pl.DeviceIdType.MESH
NT = (((1,), (1,)), ((), ()))
TN = (((0,), (0,)), ((), ()))


def _tile(n, target, mult=SUBLANES):
    best = None
    for t in range(mult, min(n, target) + 1, mult):
        if n % t == 0:
            best = t
    return best if best is not None else n


def _full(shape):
    nd = len(shape)
    return pl.BlockSpec(shape, lambda *_: (0,) * nd)


def _cp(*sem):
    return pltpu.CompilerParams(dimension_semantics=sem, vmem_limit_bytes=VMEM_LIMIT)


def _sigmoid(x):
    return 1.0 / (1.0 + jnp.exp(-x))


def _peer(x, y, c, d):
    return ((1 - x) if d & 4 else x, (1 - y) if d & 2 else y, (1 - c) if d & 1 else c)


def _lin(p):
    return 4 * p[0] + 2 * p[1] + p[2]


CHIP_MASKS = (2, 4, 6)
MODES_PER_ARRAY = {"gather": NDEV - 1, "scatter": NDEV - 1, "gather_chips": 1 + len(CHIP_MASKS),
                   "forward": len(CHIP_MASKS)}


def _transfers(mode):
    x, y, c = (lax.axis_index(a) for a in MESH_AXES)
    me = _lin((x, y, c))
    if mode == "forward":
        sibling = (x, y, 1 - c)
        return [(sibling, ("land", _lin(_peer(x, y, c, q))), _lin(_peer(x, y, c, q)), _lin(_peer(x, y, c, q ^ 1)))
                for q in CHIP_MASKS]
    masks = (1,) + CHIP_MASKS if mode == "gather_chips" else range(1, NDEV)
    out = []
    for d in masks:
        peer = _peer(x, y, c, d)
        source = ("block", _lin(peer)) if mode == "scatter" else ("whole", None)
        out.append((peer, source, me, _lin(peer)))
    return out


def _remote_copies(src_refs, land_refs, send_sems, recv_sems, mode):
    transfers = _transfers(mode)
    sends, recvs = [], []
    for i, land_ref in enumerate(land_refs):
        for t, (peer, (kind, slot), there, here) in enumerate(transfers):
            k = i * len(transfers) + t
            src = land_ref.at[slot] if kind == "land" else src_refs[i].at[slot] if kind == "block" else src_refs[i]
            for dst_slot, out in ((there, sends), (here, recvs)):
                out.append(pltpu.make_async_remote_copy(
                    src_ref=src, dst_ref=land_ref.at[dst_slot], send_sem=send_sems.at[k], recv_sem=recv_sems.at[k],
                    device_id=peer, device_id_type=MESH))
    return sends, recvs


def _exchange(srcs, mode, name):
    n = len(srcs)
    gather = mode == "gather"

    def body(*refs):
        src_refs, out_refs, token = refs[:n], refs[n:2 * n], refs[2 * n]
        send_sems, recv_sems, local_sems = refs[2 * n + 1:]
        me = _lin(tuple(lax.axis_index(a) for a in MESH_AXES))
        local = []
        for i in range(n):
            mine = src_refs[i] if gather else src_refs[i].at[me]
            cp = pltpu.make_async_copy(mine, out_refs[i].at[me], local_sems.at[i])
            cp.start()
            local.append(cp)
        sends, recvs = _remote_copies(src_refs, out_refs, send_sems, recv_sems, mode)
        for snd in sends:
            snd.start()
        token[...] = jnp.zeros_like(token)
        for snd, rcv in zip(sends, recvs):
            snd.wait_send()
            rcv.wait_recv()
        for cp in local:
            cp.wait()

    out_shape = [jax.ShapeDtypeStruct(((NDEV,) + s.shape) if gather else s.shape, s.dtype) for s in srcs]
    out_shape.append(jax.ShapeDtypeStruct((SUBLANES, LANES), F32))
    any_spec = pl.BlockSpec(memory_space=pl.ANY)
    return pl.pallas_call(
        body, name=name, out_shape=out_shape,
        in_specs=[any_spec] * n, out_specs=[any_spec] * n + [pl.BlockSpec(memory_space=pltpu.VMEM)],
        scratch_shapes=[pltpu.SemaphoreType.DMA((n * (NDEV - 1),)),
                        pltpu.SemaphoreType.DMA((n * (NDEV - 1),)),
                        pltpu.SemaphoreType.DMA((n,))],
    )(*srcs)


HBM_SPEC = pl.BlockSpec(memory_space=pltpu.HBM)
SEM_SPEC = pl.BlockSpec(memory_space=pltpu.SEMAPHORE)
SIDE_EFFECT = pltpu.SideEffectType.DATAFLOW_SIDE_EFFECTING


def _exchange_start(arrays, mode, name):
    me = _lin(tuple(lax.axis_index(a) for a in MESH_AXES))
    if mode == "forward":
        srcs, lands = [], list(arrays)
    else:
        srcs, lands = list(arrays), []
        for s in srcs:
            own = lax.dynamic_index_in_dim(s, me, 0, keepdims=False) if mode == "scatter" else s
            shape = s.shape if mode == "scatter" else (NDEV,) + s.shape
            lands.append(lax.dynamic_update_index_in_dim(lax.empty(shape, s.dtype), own, me, 0))
    ns, n = len(srcs), len(lands)

    def body(*refs):
        src_refs, land_refs = refs[:ns], refs[ns:ns + n]
        send_sems, recv_sems, token = refs[ns + n], refs[ns + n + 1], refs[-1]
        sends, _ = _remote_copies(src_refs, land_refs, send_sems, recv_sems, mode)
        for snd in sends:
            snd.start()
        token[...] = jnp.zeros_like(token)

    operands = [pltpu.with_memory_space_constraint(a, pltpu.HBM) for a in srcs + lands]
    nsem = n * MODES_PER_ARRAY[mode]
    return pl.pallas_call(
        body, name=name,
        out_shape=(pltpu.SemaphoreType.DMA((nsem,)), pltpu.SemaphoreType.DMA((nsem,)),
                   *[pltpu.HBM(a.shape, a.dtype) for a in operands],
                   jax.ShapeDtypeStruct((SUBLANES, LANES), F32)),
        in_specs=[HBM_SPEC] * (ns + n),
        out_specs=(SEM_SPEC, SEM_SPEC, *([HBM_SPEC] * (ns + n)), pl.BlockSpec(memory_space=pltpu.VMEM)),
        input_output_aliases={i: 2 + i for i in range(ns + n)},
        compiler_params=pltpu.CompilerParams(has_side_effects=SIDE_EFFECT),
    )(*operands)


def _exchange_wait(handle, after, mode, name):
    send_sems, recv_sems, *thru = handle[:-1]
    n = len(thru) if mode == "forward" else len(thru) // 2
    ns = len(thru) - n

    def body(*refs):
        src_refs, land_refs = refs[:ns], refs[ns:ns + n]
        sends, recvs = _remote_copies(src_refs, land_refs, refs[ns + n], refs[ns + n + 1], mode)
        for snd, rcv in zip(sends, recvs):
            snd.wait_send()
            rcv.wait_recv()
        refs[-1][...] = jnp.zeros_like(refs[-1])

    outs = pl.pallas_call(
        body, name=name,
        out_shape=(*[pltpu.HBM(a.shape, a.dtype) for a in thru], jax.ShapeDtypeStruct((SUBLANES, LANES), F32)),
        in_specs=[HBM_SPEC] * (ns + n) + [SEM_SPEC, SEM_SPEC, pl.BlockSpec(memory_space=pl.ANY)],
        out_specs=[HBM_SPEC] * (ns + n) + [pl.BlockSpec(memory_space=pltpu.VMEM)],
        input_output_aliases={i: i for i in range(ns + n)},
        compiler_params=pltpu.CompilerParams(has_side_effects=SIDE_EFFECT),
    )(*thru, send_sems, recv_sems, after)
    return outs[ns:]


def _gather_two_level(srcs, after, name):
    first = _exchange_start(srcs, "gather_chips", name + "_chips_start")
    *lands, token = _exchange_wait(first, after, "gather_chips", name + "_chips_wait")
    second = _exchange_start(lands, "forward", name + "_forward_start")
    return _exchange_wait(second, token, "forward", name + "_forward_wait")


def _after(value, token):
    return value + token[0, 0]


ANY_SPEC = pl.BlockSpec(memory_space=pl.ANY)


def _load_cols(wg_ref, widx, w_ref, sems):
    n = wg_ref.shape[-1]
    copies = [pltpu.make_async_copy(wg_ref.at[k, widx], w_ref.at[:, pl.ds(k * n, n)], sems.at[k])
              for k in range(NDEV)]
    for cp in copies:
        cp.start()
    for cp in copies:
        cp.wait()


def _load_rows(wg_ref, r, ridx, w_ref, sems):
    copies = [pltpu.make_async_copy(wg_ref.at[k, pl.ds(ridx * r, r)], w_ref.at[pl.ds(k * r, r)], sems.at[k])
              for k in range(NDEV)]
    for cp in copies:
        cp.start()
    for cp in copies:
        cp.wait()


def _mm_fwd(x, sc, sh, bias, wg, widxs, name):
    s_len, kdim = x.shape
    ncol = NDEV * wg.shape[-1]
    tm = _tile(s_len, 512)
    nw = len(widxs)

    def body(x_ref, sc_ref, sh_ref, b_ref, wg_ref, *rest):
        o_refs, w_refs, sems = rest[:nw], rest[nw:2 * nw], rest[2 * nw]

        @pl.when(pl.program_id(0) == 0)
        def _():
            for i, w_ref in enumerate(w_refs):
                _load_cols(wg_ref, widxs[i], w_ref, sems.at[i])

        h = (x_ref[...] * (1.0 + sc_ref[...]) + sh_ref[...]).astype(BF16)
        for w_ref, o_ref in zip(w_refs, o_refs):
            o_ref[...] = (jnp.dot(h, w_ref[...], preferred_element_type=F32) + b_ref[...]).astype(BF16)

    return pl.pallas_call(
        body, name=name, grid=(s_len // tm,),
        in_specs=[pl.BlockSpec((tm, kdim), lambda i: (i, 0)), _full((1, kdim)), _full((1, kdim)),
                  _full((1, ncol)), ANY_SPEC],
        out_specs=[pl.BlockSpec((tm, ncol), lambda i: (i, 0))] * nw,
        out_shape=[jax.ShapeDtypeStruct((s_len, ncol), BF16)] * nw,
        scratch_shapes=[pltpu.VMEM((kdim, ncol), BF16)] * nw + [pltpu.SemaphoreType.DMA((nw, NDEV))],
        compiler_params=_cp("arbitrary"),
    )(x, sc, sh, bias, wg)


def _mm_ln(a, wg, r, ridx, xres, gate, gam, bet, bias, alpha, name):
    s_len = a.shape[0]
    d = wg.shape[-1]
    tm = _tile(s_len, 512)

    def body(a_ref, wg_ref, x_ref, g_ref, gam_ref, bet_ref, b_ref, y_ref, xo_ref, xh_ref, rs_ref, w_ref, sems):
        @pl.when(pl.program_id(0) == 0)
        def _():
            _load_rows(wg_ref, r, ridx, w_ref, sems)

        y = jnp.dot(a_ref[...], w_ref[...], preferred_element_type=F32) + b_ref[...]
        z = alpha * x_ref[...] + g_ref[...] * y
        mu = jnp.mean(z, axis=-1, keepdims=True)
        zc = z - mu
        var = jnp.mean(zc * zc, axis=-1, keepdims=True)
        rstd = lax.rsqrt(var + LN_EPS)
        xh = zc * rstd
        y_ref[...] = y.astype(BF16)
        xh_ref[...] = xh
        rs_ref[...] = rstd
        xo_ref[...] = xh * gam_ref[...] + bet_ref[...]

    row = pl.BlockSpec((tm, d), lambda i: (i, 0))
    vec = _full((1, d))
    return pl.pallas_call(
        body, name=name, grid=(s_len // tm,),
        in_specs=[pl.BlockSpec((tm, NDEV * r), lambda i: (i, 0)), ANY_SPEC, row, vec, vec, vec, vec],
        out_specs=[row, row, row, pl.BlockSpec((tm, 1), lambda i: (i, 0))],
        out_shape=[jax.ShapeDtypeStruct((s_len, d), BF16)] + [jax.ShapeDtypeStruct((s_len, d), F32)] * 2
        + [jax.ShapeDtypeStruct((s_len, 1), F32)],
        scratch_shapes=[pltpu.VMEM((NDEV * r, d), BF16), pltpu.SemaphoreType.DMA((NDEV,))],
        compiler_params=_cp("arbitrary"),
    )(a, wg, xres, gate, gam, bet, bias)


def _ln_in_specs(tm, d):
    row = pl.BlockSpec((tm, d), lambda i: (i, 0))
    return [row, pl.BlockSpec((tm, 1), lambda i: (i, 0)), _full((1, d)), row, _full((1, d))]


def _ln_out_specs(s_len, tm, d):
    row = pl.BlockSpec((tm, d), lambda i: (i, 0))
    return ([row, row, _full((SUBLANES, d))],
            [jax.ShapeDtypeStruct((s_len, d), BF16), jax.ShapeDtypeStruct((s_len, d), F32),
             jax.ShapeDtypeStruct((SUBLANES, d), F32)])


def _ln_bwd_rows(dxo, ln_refs, out_refs, alpha):
    xh_ref, rs_ref, gam_ref, y_ref, g_ref = ln_refs
    dy_ref, dres_ref, acc_ref = out_refs
    xh = xh_ref[...]
    dxh = dxo * gam_ref[...]
    m1 = jnp.mean(dxh, axis=-1, keepdims=True)
    m2 = jnp.mean(dxh * xh, axis=-1, keepdims=True)
    dz = rs_ref[...] * (dxh - m1 - xh * m2)
    dy = g_ref[...] * dz
    dy_ref[...] = dy.astype(BF16)
    dres_ref[...] = alpha * dz
    acc_ref[0:1, :] += jnp.sum(dxo * xh, axis=0, keepdims=True)
    acc_ref[1:2, :] += jnp.sum(dxo, axis=0, keepdims=True)
    acc_ref[2:3, :] += jnp.sum(dz * y_ref[...].astype(F32), axis=0, keepdims=True)
    acc_ref[3:4, :] += jnp.sum(dy, axis=0, keepdims=True)


def _mm_nt_row(dy, wg, r, ridx, name):
    s_len, d = dy.shape
    tm = _tile(s_len, 512)

    def body(dy_ref, wg_ref, o_ref, w_ref, sems):
        @pl.when(pl.program_id(0) == 0)
        def _():
            _load_rows(wg_ref, r, ridx, w_ref, sems)

        o_ref[...] = lax.dot_general(dy_ref[...], w_ref[...], NT, preferred_element_type=F32).astype(BF16)

    return pl.pallas_call(
        body, name=name, grid=(s_len // tm,),
        in_specs=[pl.BlockSpec((tm, d), lambda i: (i, 0)), ANY_SPEC],
        out_specs=pl.BlockSpec((tm, NDEV * r), lambda i: (i, 0)),
        out_shape=jax.ShapeDtypeStruct((s_len, NDEV * r), BF16),
        scratch_shapes=[pltpu.VMEM((NDEV * r, d), BF16), pltpu.SemaphoreType.DMA((NDEV,))],
        compiler_params=_cp("arbitrary"),
    )(dy, wg)


def _mm_nt_mod(dos, wg, widxs, xin, sc, dres, name, ln=None, alpha=None):
    s_len, kdim = xin.shape
    ncol = NDEV * wg.shape[-1]
    tm = _tile(s_len, 512)
    nw = len(widxs)
    nln = 0 if ln is None else len(ln)
    nout = 2 if ln is None else 4

    def body(*refs):
        do_refs, wg_ref = refs[:nw], refs[nw]
        x_ref, sc_ref, dres_ref = refs[nw + 1:nw + 4]
        ln_refs = refs[nw + 4:nw + 4 + nln]
        out_refs = refs[nw + 4 + nln:nw + 4 + nln + nout]
        w_refs, sems = refs[nw + 4 + nln + nout:-1], refs[-1]
        acc_ref = out_refs[-1]

        @pl.when(pl.program_id(0) == 0)
        def _():
            for ref in out_refs[nout // 2:]:
                ref[...] = jnp.zeros_like(ref)
            for i, w_ref in enumerate(w_refs):
                _load_cols(wg_ref, widxs[i], w_ref, sems.at[i])

        dh = None
        for do_ref, w_ref in zip(do_refs, w_refs):
            p = lax.dot_general(do_ref[...], w_ref[...], NT, preferred_element_type=F32)
            dh = p if dh is None else dh + p
        dx = dh * (1.0 + sc_ref[...]) + dres_ref[...]
        if ln is None:
            out_refs[0][...] = dx
        else:
            _ln_bwd_rows(dx, ln_refs, out_refs[0:3], alpha)
        acc_ref[0:1, :] += jnp.sum(dh * x_ref[...], axis=0, keepdims=True)
        acc_ref[1:2, :] += jnp.sum(dh, axis=0, keepdims=True)

    row = pl.BlockSpec((tm, kdim), lambda i: (i, 0))
    if ln is None:
        out_specs, out_shape = [row], [jax.ShapeDtypeStruct((s_len, kdim), F32)]
    else:
        out_specs, out_shape = _ln_out_specs(s_len, tm, kdim)
    return pl.pallas_call(
        body, name=name, grid=(s_len // tm,),
        in_specs=[pl.BlockSpec((tm, ncol), lambda i: (i, 0))] * nw + [ANY_SPEC, row, _full((1, kdim)), row]
        + ([] if ln is None else _ln_in_specs(tm, kdim)),
        out_specs=out_specs + [_full((SUBLANES, kdim))],
        out_shape=out_shape + [jax.ShapeDtypeStruct((SUBLANES, kdim), F32)],
        scratch_shapes=[pltpu.VMEM((kdim, ncol), BF16)] * nw + [pltpu.SemaphoreType.DMA((nw, NDEV))],
        compiler_params=_cp("arbitrary"),
    )(*dos, wg, xin, sc, dres, *([] if ln is None else ln))


def _mm_tn_col(x, sc, sh, do, name):
    s_len, kdim = x.shape
    n = do.shape[1] // NDEV
    ts = _tile(s_len, 512)
    nsteps = s_len // ts

    def body(x_ref, sc_ref, sh_ref, do_ref, o_ref, acc_ref):
        @pl.when(pl.program_id(0) == 0)
        def _():
            acc_ref[...] = jnp.zeros_like(acc_ref)

        h = (x_ref[...] * (1.0 + sc_ref[...]) + sh_ref[...]).astype(BF16)
        acc_ref[...] += lax.dot_general(h, do_ref[...], TN, preferred_element_type=F32)

        @pl.when(pl.program_id(0) == nsteps - 1)
        def _():
            for k in range(NDEV):
                o_ref[k] = acc_ref[:, k * n:(k + 1) * n].astype(BF16)

    return pl.pallas_call(
        body, name=name, grid=(nsteps,),
        in_specs=[pl.BlockSpec((ts, kdim), lambda i: (i, 0)), _full((1, kdim)), _full((1, kdim)),
                  pl.BlockSpec((ts, NDEV * n), lambda i: (i, 0))],
        out_specs=_full((NDEV, kdim, n)),
        out_shape=jax.ShapeDtypeStruct((NDEV, kdim, n), BF16),
        scratch_shapes=[pltpu.VMEM((kdim, NDEV * n), F32)],
        compiler_params=_cp("arbitrary"),
    )(x, sc, sh, do)


def _mm_tn_col_t(x, sc, sh, do, rows_out, name):
    s_len, kdim = x.shape
    n = do.shape[1] // NDEV
    ts = _tile(s_len, 512)
    nsteps = s_len // ts

    def body(x_ref, sc_ref, sh_ref, do_ref, o_ref, acc_ref):
        @pl.when(pl.program_id(0) == 0)
        def _():
            acc_ref[...] = jnp.zeros_like(acc_ref)

        h = (x_ref[...] * (1.0 + sc_ref[...]) + sh_ref[...]).astype(BF16)
        acc_ref[...] += lax.dot_general(do_ref[...], h, TN, preferred_element_type=F32)

        @pl.when(pl.program_id(0) == nsteps - 1)
        def _():
            for k in range(NDEV):
                o_ref[k] = acc_ref[k * n:k * n + rows_out, :].astype(BF16)

    return pl.pallas_call(
        body, name=name, grid=(nsteps,),
        in_specs=[pl.BlockSpec((ts, kdim), lambda i: (i, 0)), _full((1, kdim)), _full((1, kdim)),
                  pl.BlockSpec((ts, NDEV * n), lambda i: (i, 0))],
        out_specs=_full((NDEV, rows_out, kdim)),
        out_shape=jax.ShapeDtypeStruct((NDEV, rows_out, kdim), BF16),
        scratch_shapes=[pltpu.VMEM((NDEV * n, kdim), F32)],
        compiler_params=_cp("arbitrary"),
    )(x, sc, sh, do)


def _mm_tn_row(a, dy, r, rows_out, name):
    s_len, d = dy.shape
    ts = _tile(s_len, 512)
    nsteps = s_len // ts

    def body(a_ref, dy_ref, o_ref, acc_ref):
        @pl.when(pl.program_id(0) == 0)
        def _():
            acc_ref[...] = jnp.zeros_like(acc_ref)

        acc_ref[...] += lax.dot_general(a_ref[...], dy_ref[...], TN, preferred_element_type=F32)

        @pl.when(pl.program_id(0) == nsteps - 1)
        def _():
            for k in range(NDEV):
                o_ref[k] = acc_ref[k * r:k * r + rows_out, :].astype(BF16)

    return pl.pallas_call(
        body, name=name, grid=(nsteps,),
        in_specs=[pl.BlockSpec((ts, NDEV * r), lambda i: (i, 0)), pl.BlockSpec((ts, d), lambda i: (i, 0))],
        out_specs=_full((NDEV, rows_out, d)),
        out_shape=jax.ShapeDtypeStruct((NDEV, rows_out, d), BF16),
        scratch_shapes=[pltpu.VMEM((NDEV * r, d), F32)],
        compiler_params=_cp("arbitrary"),
    )(a, dy)


def _prev_spec(ts, pad, cb, col):
    return pl.BlockSpec((pad, cb), lambda *g: (jnp.maximum(g[-1] * (ts // pad) - 1, 0), col(g)))


def _next_spec(ts, pad, cb, col, s_len):
    return pl.BlockSpec((pad, cb), lambda *g: (jnp.minimum((g[-1] + 1) * (ts // pad), s_len // pad - 1), col(g)))


class _F32Loads:
    def __init__(self, ref):
        self.ref = ref

    def __getitem__(self, idx):
        return self.ref[idx].astype(F32)


def _direct(buf_ref):
    return lambda off, rows: buf_ref[off:off + rows, :]


def _make_shifts(sh_ref, nrows):
    for r in range(1, SUBLANES):
        sh_ref[r, 0:nrows - SUBLANES, :] = sh_ref[0, r:r + nrows - SUBLANES, :]


def _shifted(sh_ref):
    def read(off, rows):
        r = off % SUBLANES
        return sh_ref[r, off - r:off - r + rows, :]
    return read


def _conv_fwd_rows(read, w_ref, b_ref, ktaps, pad, r0, rows):
    acc = None
    for j in range(ktaps):
        term = w_ref[ktaps - 1 - j:ktaps - j, :] * read(pad - j + r0, rows)
        acc = term if acc is None else acc + term
    return acc + b_ref[...]


def _conv_bwd_rows(read, x_rows, w_ref, dwacc_ref, ktaps, r0, rows):
    acc = None
    for j in range(ktaps):
        sl = read(j + r0, rows)
        term = w_ref[ktaps - 1 - j:ktaps - j, :] * sl
        acc = term if acc is None else acc + term
        prod = x_rows * sl
        fold = prod[0:SUBLANES]
        for q in range(1, rows // SUBLANES):
            fold = fold + prod[q * SUBLANES:(q + 1) * SUBLANES]
        tap = ktaps - 1 - j
        dwacc_ref[tap * SUBLANES:(tap + 1) * SUBLANES, :] += fold
    return acc


def _flush_dw(dwacc_ref, dw_ref, ktaps):
    for tap in range(ktaps):
        dw_ref[tap:tap + 1, :] = jnp.sum(dwacc_ref[tap * SUBLANES:(tap + 1) * SUBLANES, :], axis=0, keepdims=True)


def _gateconv_fwd(bcv, cw, cb, name):
    s_len, d3 = bcv.shape
    d = d3 // 3
    ktaps = cw.shape[0]
    pad = SHORT_PAD
    ts = _tile(s_len, 256)

    def body(gb_ref, gc_ref, v_ref, gcp_ref, vp_ref, w_ref, b_ref, o_ref, pbuf):
        gb_ref, gc_ref, v_ref, gcp_ref, vp_ref = map(_F32Loads, (gb_ref, gc_ref, v_ref, gcp_ref, vp_ref))
        s = pl.program_id(0)
        pbuf[0:pad, :] = jnp.where(s > 0, gcp_ref[...] * vp_ref[...], 0.0)
        pbuf[pad:pad + ts, :] = gc_ref[...] * v_ref[...]
        for r0 in range(0, ts, CHUNK):
            q = _conv_fwd_rows(_direct(pbuf), w_ref, b_ref, ktaps, pad, r0, CHUNK)
            o_ref[r0:r0 + CHUNK, :] = (gb_ref[r0:r0 + CHUNK, :] * q).astype(BF16)

    def cur(part):
        return pl.BlockSpec((ts, d), lambda s: (s, part))

    return pl.pallas_call(
        body, name=name, grid=(s_len // ts,),
        in_specs=[cur(0), cur(1), cur(2),
                  _prev_spec(ts, pad, d, lambda g: 1), _prev_spec(ts, pad, d, lambda g: 2),
                  _full((ktaps, d)), _full((1, d))],
        out_specs=pl.BlockSpec((ts, d), lambda s: (s, 0)),
        out_shape=jax.ShapeDtypeStruct((s_len, d), BF16),
        scratch_shapes=[pltpu.VMEM((pad + ts, d), F32)],
        compiler_params=_cp("parallel"),
    )(bcv, bcv, bcv, bcv, bcv, cw, cb)


def _gateconv_bwd(bcv, dy0, cw, cb, name):
    s_len, d3 = bcv.shape
    d = d3 // 3
    ktaps = cw.shape[0]
    pad = SHORT_PAD
    ts = _tile(s_len, 256)
    nsteps = s_len // ts

    def body(gb_ref, gc_ref, v_ref, gcp_ref, vp_ref, gbn_ref, dy_ref, dyn_ref, w_ref, b_ref,
             o_ref, dw_ref, db_ref, pbuf, dqbuf, dwacc):
        gb_ref, gc_ref, v_ref, gcp_ref, vp_ref, gbn_ref, dy_ref, dyn_ref = map(
            _F32Loads, (gb_ref, gc_ref, v_ref, gcp_ref, vp_ref, gbn_ref, dy_ref, dyn_ref))
        s = pl.program_id(0)

        @pl.when(s == 0)
        def _():
            dwacc[...] = jnp.zeros_like(dwacc)
            db_ref[...] = jnp.zeros_like(db_ref)

        pbuf[0:pad, :] = jnp.where(s > 0, gcp_ref[...] * vp_ref[...], 0.0)
        pbuf[pad:pad + ts, :] = gc_ref[...] * v_ref[...]
        dq = dy_ref[...] * gb_ref[...]
        dqbuf[0:ts, :] = dq
        dqbuf[ts:ts + pad, :] = jnp.where(s < nsteps - 1, dyn_ref[...] * gbn_ref[...], 0.0)
        db_ref[...] += jnp.sum(dq, axis=0, keepdims=True)
        for r0 in range(0, ts, CHUNK):
            rows = slice(r0, r0 + CHUNK)
            q = _conv_fwd_rows(_direct(pbuf), w_ref, b_ref, ktaps, pad, r0, CHUNK)
            o_ref[rows, 0:d] = (dy_ref[rows, :] * q).astype(BF16)
            dp = _conv_bwd_rows(_direct(dqbuf), pbuf[pad + r0:pad + r0 + CHUNK, :], w_ref, dwacc, ktaps, r0, CHUNK)
            o_ref[rows, d:2 * d] = (dp * v_ref[rows, :]).astype(BF16)
            o_ref[rows, 2 * d:3 * d] = (dp * gc_ref[rows, :]).astype(BF16)

        @pl.when(s == nsteps - 1)
        def _():
            _flush_dw(dwacc, dw_ref, ktaps)

    def cur(part):
        return pl.BlockSpec((ts, d), lambda s: (s, part))

    return pl.pallas_call(
        body, name=name, grid=(nsteps,),
        in_specs=[cur(0), cur(1), cur(2),
                  _prev_spec(ts, pad, d, lambda g: 1), _prev_spec(ts, pad, d, lambda g: 2),
                  _next_spec(ts, pad, d, lambda g: 0, s_len),
                  cur(0), _next_spec(ts, pad, d, lambda g: 0, s_len),
                  _full((ktaps, d)), _full((1, d))],
        out_specs=[pl.BlockSpec((ts, d3), lambda s: (s, 0)), _full((ktaps, d)), _full((1, d))],
        out_shape=[jax.ShapeDtypeStruct((s_len, d3), BF16), jax.ShapeDtypeStruct((ktaps, d), F32),
                   jax.ShapeDtypeStruct((1, d), F32)],
        scratch_shapes=[pltpu.VMEM((pad + ts, d), F32), pltpu.VMEM((ts + pad, d), F32),
                        pltpu.VMEM((ktaps * SUBLANES, d), F32)],
        compiler_params=_cp("arbitrary"),
    )(bcv, bcv, bcv, bcv, bcv, bcv, dy0, dy0, cw, cb)


def _ffn_mid_fwd(u0, vg, cw, cb, name):
    s_len, f = u0.shape
    ktaps = cw.shape[0]
    pad = SHORT_PAD
    ts = _tile(s_len, 256)
    cbk = 1024 if f % 1024 == 0 else f

    def body(u_ref, up_ref, vg_ref, w_ref, b_ref, o_ref, ubuf):
        u_ref, up_ref, vg_ref = map(_F32Loads, (u_ref, up_ref, vg_ref))
        s = pl.program_id(1)
        ubuf[0:pad, :] = jnp.where(s > 0, up_ref[...], 0.0)
        ubuf[pad:pad + ts, :] = u_ref[...]
        for r0 in range(0, ts, CHUNK):
            u = _conv_fwd_rows(_direct(ubuf), w_ref, b_ref, ktaps, pad, r0, CHUNK)
            o_ref[r0:r0 + CHUNK, :] = (u * _sigmoid(u) * vg_ref[r0:r0 + CHUNK, :]).astype(BF16)

    cur = pl.BlockSpec((ts, cbk), lambda c, s: (s, c))
    return pl.pallas_call(
        body, name=name, grid=(f // cbk, s_len // ts),
        in_specs=[cur, _prev_spec(ts, pad, cbk, lambda g: g[0]), cur,
                  pl.BlockSpec((ktaps, cbk), lambda c, s: (0, c)), pl.BlockSpec((1, cbk), lambda c, s: (0, c))],
        out_specs=cur,
        out_shape=jax.ShapeDtypeStruct((s_len, f), BF16),
        scratch_shapes=[pltpu.VMEM((pad + ts, cbk), F32)],
        compiler_params=_cp("parallel", "parallel"),
    )(u0, u0, vg, cw, cb)


class _Cols:
    def __init__(self, ref, cols):
        self.ref, self.cols = ref, cols

    def __getitem__(self, idx):
        return self.ref[slice(None) if idx is Ellipsis else idx[0], self.cols]

    def __setitem__(self, idx, value):
        self.ref[idx[0], self.cols] = value


def _ffn_tail_fwd(u0, vg, cw, cb, wg, xres, gate, gam, bet, alpha, name):
    s_len, f = u0.shape
    d = wg.shape[-1]
    r = f // NDEV
    ktaps = cw.shape[0]
    pad = SHORT_PAD
    tm = _tile(s_len, 256)
    cbk = 1024 if f % 1024 == 0 else f

    def body(u_ref, up_ref, vg_ref, cw_ref, cb_ref, wg_ref, x_ref, g_ref, gam_ref, bet_ref,
             t_ref, y_ref, xo_ref, xh_ref, rs_ref, ubuf, w_ref, sems):
        u_ref, up_ref, vg_ref = map(_F32Loads, (u_ref, up_ref, vg_ref))
        s = pl.program_id(0)

        @pl.when(s == 0)
        def _():
            _load_rows(wg_ref, r, 0, w_ref, sems)

        ubuf[0:pad, :] = jnp.where(s > 0, up_ref[...], 0.0)
        ubuf[pad:pad + tm, :] = u_ref[...]
        y = None
        for c0 in range(0, f, cbk):
            cols = slice(c0, c0 + cbk)
            read = _direct(_Cols(ubuf, cols))
            for r0 in range(0, tm, CHUNK):
                rows = slice(r0, r0 + CHUNK)
                u = _conv_fwd_rows(read, _Cols(cw_ref, cols), _Cols(cb_ref, cols), ktaps, pad, r0, CHUNK)
                t_ref[rows, cols] = (u * _sigmoid(u) * vg_ref[rows, cols]).astype(BF16)
            p = jnp.dot(t_ref[:, cols], w_ref[cols, :], preferred_element_type=F32)
            y = p if y is None else y + p
        z = alpha * x_ref[...] + g_ref[...] * y
        mu = jnp.mean(z, axis=-1, keepdims=True)
        zc = z - mu
        var = jnp.mean(zc * zc, axis=-1, keepdims=True)
        rstd = lax.rsqrt(var + LN_EPS)
        xh = zc * rstd
        y_ref[...] = y.astype(BF16)
        xh_ref[...] = xh
        rs_ref[...] = rstd
        xo_ref[...] = xh * gam_ref[...] + bet_ref[...]

    wide = pl.BlockSpec((tm, f), lambda i: (i, 0))
    row = pl.BlockSpec((tm, d), lambda i: (i, 0))
    vec = _full((1, d))
    return pl.pallas_call(
        body, name=name, grid=(s_len // tm,),
        in_specs=[wide, _prev_spec(tm, pad, f, lambda g: 0), wide, _full((ktaps, f)), _full((1, f)), ANY_SPEC,
                  row, vec, vec, vec],
        out_specs=[wide, row, row, row, pl.BlockSpec((tm, 1), lambda i: (i, 0))],
        out_shape=[jax.ShapeDtypeStruct((s_len, f), BF16), jax.ShapeDtypeStruct((s_len, d), BF16),
                   jax.ShapeDtypeStruct((s_len, d), F32), jax.ShapeDtypeStruct((s_len, d), F32),
                   jax.ShapeDtypeStruct((s_len, 1), F32)],
        scratch_shapes=[pltpu.VMEM((pad + tm, f), F32), pltpu.VMEM((f, d), BF16), pltpu.SemaphoreType.DMA((NDEV,))],
        compiler_params=_cp("arbitrary"),
    )(u0, u0, vg, cw, cb, wg, xres, gate, gam, bet)


def _ffn_core_bwd(dy, u0, vg, cw, cb, wg_row, wg_col, xin, sc, dres, ln, alpha, name):
    s_len, f = u0.shape
    d = xin.shape[1]
    r = f // NDEV
    ktaps = cw.shape[0]
    pad = SHORT_PAD
    tm = _tile(s_len, 256)
    nsteps = s_len // tm
    cbk = 1024 if f % 1024 == 0 else f

    def body(dy_ref, dyn_ref, u_ref, up_ref, un_ref, vg_ref, vgn_ref, cw_ref, cb_ref, wgr_ref, wgc_ref,
             x_ref, sc_ref, dres_ref, xh_ref, rs_ref, gam_ref, y_ref, g_ref,
             du0_ref, dvg_ref, dw_ref, db_ref, dyo_ref, dreso_ref, lnacc_ref, acc_ref,
             ubuf, dtbuf, dubuf, dwacc, wd_ref, wup_ref, wgate_ref, sems):
        u_ref, up_ref, un_ref, vg_ref, vgn_ref = map(_F32Loads, (u_ref, up_ref, un_ref, vg_ref, vgn_ref))
        s = pl.program_id(0)
        last = s == nsteps - 1

        @pl.when(s == 0)
        def _():
            dwacc[...] = jnp.zeros_like(dwacc)
            db_ref[...] = jnp.zeros_like(db_ref)
            acc_ref[...] = jnp.zeros_like(acc_ref)
            lnacc_ref[...] = jnp.zeros_like(lnacc_ref)
            _load_rows(wgr_ref, r, 0, wd_ref, sems.at[0])
            _load_cols(wgc_ref, 0, wup_ref, sems.at[1])
            _load_cols(wgc_ref, 1, wgate_ref, sems.at[2])

        ubuf[0:pad, :] = jnp.where(s > 0, up_ref[...], 0.0)
        ubuf[pad:pad + tm, :] = u_ref[...]
        ubuf[pad + tm:pad + tm + pad, :] = un_ref[...]
        dy_cur, dy_nxt = dy_ref[...], dyn_ref[...]
        dh = None
        for c0 in range(0, f, cbk):
            cols = slice(c0, c0 + cbk)
            wd_blk = wd_ref[cols, :]
            dtbuf[0:tm, :] = lax.dot_general(dy_cur, wd_blk, NT, preferred_element_type=F32)
            dtbuf[tm:tm + pad, :] = jnp.where(
                last, 0.0, lax.dot_general(dy_nxt, wd_blk, NT, preferred_element_type=F32))
            read_u = _direct(_Cols(ubuf, cols))
            for r0 in range(0, tm + pad, CHUNK):
                u = _conv_fwd_rows(read_u, _Cols(cw_ref, cols), _Cols(cb_ref, cols), ktaps, pad, r0, CHUNK)
                sg = _sigmoid(u)
                dtr = dtbuf[r0:r0 + CHUNK, :]
                if r0 < tm:
                    vgr = vg_ref[r0:r0 + CHUNK, cols]
                    dvg_ref[r0:r0 + CHUNK, cols] = (dtr * u * sg).astype(BF16)
                else:
                    vgr = vgn_ref[r0 - tm:r0 - tm + CHUNK, cols]
                dubuf[r0:r0 + CHUNK, :] = dtr * vgr * (sg * (1.0 + u * (1.0 - sg)))
            db_ref[:, cols] += jnp.sum(dubuf[0:tm, :], axis=0, keepdims=True)
            for r0 in range(0, tm, CHUNK):
                du0 = _conv_bwd_rows(_direct(dubuf), u_ref[r0:r0 + CHUNK, cols], _Cols(cw_ref, cols),
                                     _Cols(dwacc, cols), ktaps, r0, CHUNK)
                du0_ref[r0:r0 + CHUNK, cols] = du0.astype(BF16)
            p = (lax.dot_general(du0_ref[:, cols], wup_ref[:, cols], NT, preferred_element_type=F32)
                 + lax.dot_general(dvg_ref[:, cols], wgate_ref[:, cols], NT, preferred_element_type=F32))
            dh = p if dh is None else dh + p
        dx = dh * (1.0 + sc_ref[...]) + dres_ref[...]
        _ln_bwd_rows(dx, (xh_ref, rs_ref, gam_ref, y_ref, g_ref), (dyo_ref, dreso_ref, lnacc_ref), alpha)
        acc_ref[0:1, :] += jnp.sum(dh * x_ref[...], axis=0, keepdims=True)
        acc_ref[1:2, :] += jnp.sum(dh, axis=0, keepdims=True)

        @pl.when(last)
        def _():
            _flush_dw(dwacc, dw_ref, ktaps)

    wide = pl.BlockSpec((tm, f), lambda i: (i, 0))
    row = pl.BlockSpec((tm, d), lambda i: (i, 0))
    ln_out_specs, ln_out_shape = _ln_out_specs(s_len, tm, d)
    return pl.pallas_call(
        body, name=name, grid=(nsteps,),
        in_specs=[row, _next_spec(tm, pad, d, lambda g: 0, s_len),
                  wide, _prev_spec(tm, pad, f, lambda g: 0), _next_spec(tm, pad, f, lambda g: 0, s_len),
                  wide, _next_spec(tm, pad, f, lambda g: 0, s_len),
                  _full((ktaps, f)), _full((1, f)), ANY_SPEC, ANY_SPEC, row, _full((1, d)), row]
        + _ln_in_specs(tm, d),
        out_specs=[wide, wide, _full((ktaps, f)), _full((1, f))] + ln_out_specs + [_full((SUBLANES, d))],
        out_shape=[jax.ShapeDtypeStruct((s_len, f), BF16), jax.ShapeDtypeStruct((s_len, f), BF16),
                   jax.ShapeDtypeStruct((ktaps, f), F32), jax.ShapeDtypeStruct((1, f), F32)]
        + ln_out_shape + [jax.ShapeDtypeStruct((SUBLANES, d), F32)],
        scratch_shapes=[pltpu.VMEM((pad + tm + pad, f), F32), pltpu.VMEM((tm + pad, cbk), F32),
                        pltpu.VMEM((tm + pad, cbk), F32), pltpu.VMEM((ktaps * SUBLANES, f), F32),
                        pltpu.VMEM((f, d), BF16), pltpu.VMEM((d, f), BF16), pltpu.VMEM((d, f), BF16),
                        pltpu.SemaphoreType.DMA((3, NDEV))],
        compiler_params=_cp("arbitrary"),
    )(dy, dy, u0, u0, u0, vg, vg, cw, cb, wg_row, wg_col, xin, sc, dres, *ln)


def _ffn_mid_bwd(u0, vg, dt, cw, cb, name):
    s_len, f = u0.shape
    ktaps = cw.shape[0]
    pad = SHORT_PAD
    ts = _tile(s_len, 256)
    nsteps = s_len // ts
    cbk = 1024 if f % 1024 == 0 else f

    def body(u_ref, up_ref, un_ref, vg_ref, vgn_ref, dt_ref, dtn_ref, w_ref, b_ref,
             du0_ref, dvg_ref, dw_ref, db_ref, ubuf, dubuf, dwacc):
        u_ref, up_ref, un_ref, vg_ref, vgn_ref, dt_ref, dtn_ref = map(
            _F32Loads, (u_ref, up_ref, un_ref, vg_ref, vgn_ref, dt_ref, dtn_ref))
        s = pl.program_id(1)

        @pl.when(s == 0)
        def _():
            dwacc[...] = jnp.zeros_like(dwacc)
            db_ref[...] = jnp.zeros_like(db_ref)

        ubuf[0:pad, :] = jnp.where(s > 0, up_ref[...], 0.0)
        ubuf[pad:pad + ts, :] = u_ref[...]
        ubuf[pad + ts:pad + ts + pad, :] = un_ref[...]
        last = s == nsteps - 1
        for r0 in range(0, ts + pad, CHUNK):
            u = _conv_fwd_rows(_direct(ubuf), w_ref, b_ref, ktaps, pad, r0, CHUNK)
            sg = _sigmoid(u)
            if r0 < ts:
                rows = slice(r0, r0 + CHUNK)
                dtr, vgr = dt_ref[rows, :], vg_ref[rows, :]
                dvg_ref[rows, :] = (dtr * u * sg).astype(BF16)
            else:
                rows = slice(r0 - ts, r0 - ts + CHUNK)
                dtr, vgr = jnp.where(last, 0.0, dtn_ref[rows, :]), vgn_ref[rows, :]
            dubuf[r0:r0 + CHUNK, :] = dtr * vgr * (sg * (1.0 + u * (1.0 - sg)))
        db_ref[...] += jnp.sum(dubuf[0:ts, :], axis=0, keepdims=True)
        for r0 in range(0, ts, CHUNK):
            du0 = _conv_bwd_rows(_direct(dubuf), u_ref[r0:r0 + CHUNK, :], w_ref, dwacc, ktaps, r0, CHUNK)
            du0_ref[r0:r0 + CHUNK, :] = du0.astype(BF16)

        @pl.when(last)
        def _():
            _flush_dw(dwacc, dw_ref, ktaps)

    cur = pl.BlockSpec((ts, cbk), lambda c, s: (s, c))
    prv = _prev_spec(ts, pad, cbk, lambda g: g[0])
    nxt = _next_spec(ts, pad, cbk, lambda g: g[0], s_len)
    return pl.pallas_call(
        body, name=name, grid=(f // cbk, nsteps),
        in_specs=[cur, prv, nxt, cur, nxt, cur, nxt,
                  pl.BlockSpec((ktaps, cbk), lambda c, s: (0, c)), pl.BlockSpec((1, cbk), lambda c, s: (0, c))],
        out_specs=[cur, cur, pl.BlockSpec((ktaps, cbk), lambda c, s: (0, c)),
                   pl.BlockSpec((1, cbk), lambda c, s: (0, c))],
        out_shape=[jax.ShapeDtypeStruct((s_len, f), BF16), jax.ShapeDtypeStruct((s_len, f), BF16),
                   jax.ShapeDtypeStruct((ktaps, f), F32), jax.ShapeDtypeStruct((1, f), F32)],
        scratch_shapes=[pltpu.VMEM((pad + ts + pad, cbk), F32), pltpu.VMEM((ts + pad, cbk), F32),
                        pltpu.VMEM((ktaps * SUBLANES, cbk), F32)],
        compiler_params=_cp("parallel", "arbitrary"),
    )(u0, u0, u0, vg, vg, dt, dt, cw, cb)


def _b_mid_fwd(ub, cw, cb, lng, lnb, name):
    s_len, d2 = ub.shape
    d = d2 // 2
    ktaps = cw.shape[0]
    pad = LONG_PAD
    ts = _tile(s_len, 256)

    def body(a_ref, g_ref, ap_ref, gp_ref, w_ref, b_ref, lng_ref, lnb_ref, a2_ref, a4_ref, abuf):
        a_ref, g_ref, ap_ref, gp_ref = map(_F32Loads, (a_ref, g_ref, ap_ref, gp_ref))
        s = pl.program_id(0)
        abuf[0, 0:pad, :] = jnp.where(s > 0, ap_ref[...] * _sigmoid(gp_ref[...]), 0.0)
        abuf[0, pad:pad + ts, :] = a_ref[...] * _sigmoid(g_ref[...])
        _make_shifts(abuf, pad + ts)
        for r0 in range(0, ts, CHUNK):
            a2_ref[r0:r0 + CHUNK, :] = _conv_fwd_rows(_shifted(abuf), w_ref, b_ref, ktaps, pad, r0, CHUNK)
        a2 = a2_ref[...]
        mu = jnp.mean(a2, axis=-1, keepdims=True)
        ac = a2 - mu
        var = jnp.mean(ac * ac, axis=-1, keepdims=True)
        a3 = ac * lax.rsqrt(var + LN_EPS) * lng_ref[...] + lnb_ref[...]
        a4_ref[...] = (a3 * _sigmoid(a3)).astype(BF16)

    def cur(part):
        return pl.BlockSpec((ts, d), lambda s: (s, part))

    vec = _full((1, d))
    return pl.pallas_call(
        body, name=name, grid=(s_len // ts,),
        in_specs=[cur(0), cur(1), _prev_spec(ts, pad, d, lambda g: 0), _prev_spec(ts, pad, d, lambda g: 1),
                  _full((ktaps, d)), vec, vec, vec],
        out_specs=[cur(0), cur(0)],
        out_shape=[jax.ShapeDtypeStruct((s_len, d), F32), jax.ShapeDtypeStruct((s_len, d), BF16)],
        scratch_shapes=[pltpu.VMEM((SUBLANES, pad + ts, d), F32)],
        compiler_params=_cp("parallel"),
    )(ub, ub, ub, ub, cw, cb, lng, lnb)


def _b_mid_bwd(ub, a2, da4, cw, lng, lnb, name):
    s_len, d2 = ub.shape
    d = d2 // 2
    ktaps = cw.shape[0]
    pad = LONG_PAD
    ts = _tile(s_len, 256)
    nsteps = s_len // ts

    def body(a_ref, g_ref, a2_ref, a2n_ref, da4_ref, da4n_ref, w_ref, lng_ref, lnb_ref,
             du_ref, dw_ref, db_ref, dlng_ref, dlnb_ref, dbias_ref, dabuf, dwacc):
        a_ref, g_ref, da4_ref, da4n_ref = map(_F32Loads, (a_ref, g_ref, da4_ref, da4n_ref))
        s = pl.program_id(0)
        last = s == nsteps - 1

        @pl.when(s == 0)
        def _():
            dwacc[...] = jnp.zeros_like(dwacc)
            for ref in (db_ref, dlng_ref, dlnb_ref, dbias_ref):
                ref[...] = jnp.zeros_like(ref)

        def ln_silu_bwd(a2_t, da4_t):
            mu = jnp.mean(a2_t, axis=-1, keepdims=True)
            ac = a2_t - mu
            var = jnp.mean(ac * ac, axis=-1, keepdims=True)
            rstd = lax.rsqrt(var + LN_EPS)
            ah = ac * rstd
            a3 = ah * lng_ref[...] + lnb_ref[...]
            sg = _sigmoid(a3)
            da3 = da4_t * (sg * (1.0 + a3 * (1.0 - sg)))
            dah = da3 * lng_ref[...]
            m1 = jnp.mean(dah, axis=-1, keepdims=True)
            m2 = jnp.mean(dah * ah, axis=-1, keepdims=True)
            return rstd * (dah - m1 - ah * m2), da3, ah

        da2, da3, ah = ln_silu_bwd(a2_ref[...], da4_ref[...])
        dabuf[0, 0:ts, :] = da2
        dlng_ref[...] += jnp.sum(da3 * ah, axis=0, keepdims=True)
        dlnb_ref[...] += jnp.sum(da3, axis=0, keepdims=True)
        db_ref[...] += jnp.sum(da2, axis=0, keepdims=True)
        da2n, _, _ = ln_silu_bwd(a2n_ref[...], jnp.where(last, 0.0, da4n_ref[...]))
        dabuf[0, ts:ts + pad, :] = da2n
        _make_shifts(dabuf, ts + pad)
        for r0 in range(0, ts, CHUNK):
            rows = slice(r0, r0 + CHUNK)
            a_r, g_r = a_ref[rows, :], g_ref[rows, :]
            sg = _sigmoid(g_r)
            da1 = _conv_bwd_rows(_shifted(dabuf), a_r * sg, w_ref, dwacc, ktaps, r0, CHUNK)
            da = da1 * sg
            dg = da1 * a_r * sg * (1.0 - sg)
            du_ref[rows, 0:d] = da.astype(BF16)
            du_ref[rows, d:2 * d] = dg.astype(BF16)
            dbias_ref[:, 0:d] += jnp.sum(da, axis=0, keepdims=True)
            dbias_ref[:, d:2 * d] += jnp.sum(dg, axis=0, keepdims=True)

        @pl.when(last)
        def _():
            _flush_dw(dwacc, dw_ref, ktaps)

    def cur(part):
        return pl.BlockSpec((ts, d), lambda s: (s, part))

    vec = _full((1, d))
    nxt = _next_spec(ts, pad, d, lambda g: 0, s_len)
    return pl.pallas_call(
        body, name=name, grid=(nsteps,),
        in_specs=[cur(0), cur(1), cur(0), nxt, cur(0), nxt, _full((ktaps, d)), vec, vec],
        out_specs=[pl.BlockSpec((ts, d2), lambda s: (s, 0)), _full((ktaps, d)), vec, vec, vec, _full((1, d2))],
        out_shape=[jax.ShapeDtypeStruct((s_len, d2), BF16), jax.ShapeDtypeStruct((ktaps, d), F32),
                   jax.ShapeDtypeStruct((1, d), F32), jax.ShapeDtypeStruct((1, d), F32),
                   jax.ShapeDtypeStruct((1, d), F32), jax.ShapeDtypeStruct((1, d2), F32)],
        scratch_shapes=[pltpu.VMEM((SUBLANES, ts + pad, d), F32), pltpu.VMEM((ktaps * SUBLANES, d), F32)],
        compiler_params=_cp("arbitrary"),
    )(ub, ub, a2, a2, da4, da4, cw, lng, lnb)


def _loss_head(xo, tgt, ln, alpha, name):
    s_len, d = xo.shape
    tm = _tile(s_len, 512)

    def body(x_ref, t_ref, xh_ref, rs_ref, gam_ref, y_ref, g_ref, dy_ref, dres_ref, acc_ref, l_ref):
        @pl.when(pl.program_id(0) == 0)
        def _():
            l_ref[...] = jnp.zeros_like(l_ref)
            acc_ref[...] = jnp.zeros_like(acc_ref)

        e = x_ref[...] - t_ref[...]
        per_row = jnp.sum(e * e, axis=-1, keepdims=True) * (1.0 / d)
        l_ref[...] += 0.5 * jnp.sum(per_row, axis=0, keepdims=True)
        _ln_bwd_rows(e * (1.0 / d), (xh_ref, rs_ref, gam_ref, y_ref, g_ref), (dy_ref, dres_ref, acc_ref), alpha)

    row = pl.BlockSpec((tm, d), lambda i: (i, 0))
    ln_out_specs, ln_out_shape = _ln_out_specs(s_len, tm, d)
    return pl.pallas_call(
        body, name=name, grid=(s_len // tm,),
        in_specs=[row, row] + _ln_in_specs(tm, d), out_specs=ln_out_specs + [_full((1, LANES))],
        out_shape=ln_out_shape + [jax.ShapeDtypeStruct((1, LANES), F32)],
        compiler_params=_cp("arbitrary"),
    )(xo, tgt, *ln)


def _ada_fwd(c_all, ada_w, ada_b_loc, name):
    depth, d, n = ada_w.shape

    def body(c_ref, w_ref, b_ref, o_ref):
        c = c_ref[...]
        act = c * _sigmoid(c)
        o_ref[...] = jnp.dot(act, w_ref[...], preferred_element_type=F32,
                             precision=lax.Precision.HIGHEST) + b_ref[...]

    return pl.pallas_call(
        body, name=name, grid=(depth,),
        in_specs=[_full((NDEV, d)), pl.BlockSpec((None, d, n), lambda i: (i, 0, 0)),
                  pl.BlockSpec((None, 1, n), lambda i: (i, 0, 0))],
        out_specs=pl.BlockSpec((None, NDEV, n), lambda i: (i, 0, 0)),
        out_shape=jax.ShapeDtypeStruct((depth, NDEV, n), F32),
        compiler_params=_cp("parallel"),
    )(c_all, ada_w, ada_b_loc.reshape(depth, 1, n))


def _ada_bwd(c_all_t, dmod_cols, name):
    depth, _, n = dmod_cols.shape
    d = c_all_t.shape[0]

    def body(ct_ref, dm_ref, o_ref):
        ct = ct_ref[...]
        act = ct * _sigmoid(ct)
        acc = None
        for b in range(NDEV):
            term = act[:, b:b + 1] * dm_ref[b:b + 1, :]
            acc = term if acc is None else acc + term
        o_ref[...] = acc

    return pl.pallas_call(
        body, name=name, grid=(depth,),
        in_specs=[_full((d, NDEV)), pl.BlockSpec((None, NDEV, n), lambda i: (i, 0, 0))],
        out_specs=pl.BlockSpec((None, d, n), lambda i: (i, 0, 0)),
        out_shape=jax.ShapeDtypeStruct((depth, d, n), F32),
        compiler_params=_cp("parallel"),
    )(c_all_t, dmod_cols)


def _sum_parts(parts, name):
    _, rows, lanes = parts.shape

    def body(p_ref, o_ref):
        acc = p_ref[0]
        for k in range(1, NDEV):
            acc = acc + p_ref[k]
        o_ref[...] = acc

    return pl.pallas_call(
        body, name=name, in_specs=[_full(parts.shape)], out_specs=_full((rows, lanes)), grid=(1,),
        out_shape=jax.ShapeDtypeStruct((rows, lanes), F32), compiler_params=_cp("arbitrary"),
    )(parts)


def _adamw(w, glist, m, v, name):
    nl, rows, cols = w.shape
    tr = _tile(rows, 256, 2 * SUBLANES)

    def body(w_ref, *rest):
        g_refs = rest[:nl]
        m_ref, v_ref, go_ref, d_ref, mo_ref, vo_ref = rest[nl:]
        g = None
        for layer, g_ref in enumerate(g_refs):
            part = g_ref[0].astype(F32)
            for p in range(1, g_ref.shape[0]):
                part = part + g_ref[p].astype(F32)
            g = part if g is None else jnp.where(pl.program_id(0) == layer, part, g)
        m1 = ADAM_B1 * m_ref[...] + (1.0 - ADAM_B1) * g
        v1 = ADAM_B2 * v_ref[...] + (1.0 - ADAM_B2) * (g * g)
        m_hat = m1 / (1.0 - ADAM_B1 ** ADAM_STEP)
        v_hat = v1 / (1.0 - ADAM_B2 ** ADAM_STEP)
        go_ref[...] = g
        mo_ref[...] = m1
        vo_ref[...] = v1
        d_ref[...] = -ADAM_LR * (m_hat / (jnp.sqrt(v_hat) + ADAM_EPS) + ADAM_WD * w_ref[...])

    blk = pl.BlockSpec((None, tr, cols), lambda l, i: (l, i, 0))
    g_specs = [pl.BlockSpec((g.shape[0], tr, cols), lambda l, i: (0, i, 0)) for g in glist]
    return pl.pallas_call(
        body, name=name, grid=(nl, rows // tr),
        in_specs=[blk] + g_specs + [blk, blk],
        out_specs=[blk] * 4, out_shape=[jax.ShapeDtypeStruct((nl, rows, cols), F32)] * 4,
        compiler_params=_cp("parallel", "parallel"),
    )(w, *glist, m, v)


def _pack(pieces):
    flat = jnp.concatenate([p.reshape(-1) for p in pieces])
    unit = SUBLANES * LANES
    padded = -(-flat.shape[0] // unit) * unit
    return jnp.pad(flat, (0, padded - flat.shape[0])).reshape(padded // LANES, LANES)


def _unpack(packed, shapes, lead=()):
    flat = packed.reshape(lead + (-1,))
    out, off = [], 0
    for s in shapes:
        size = 1
        for dim in s:
            size *= dim
        out.append(flat[..., off:off + size].reshape(lead + tuple(s)))
        off += size
    return out


def _pad_last(a, n):
    return jnp.pad(a, [(0, 0)] * (a.ndim - 1) + [(0, n - a.shape[-1])])


def kernel(x, c, ada_w, ada_b, ln_tok_g, ln_tok_b, ln_ch_g, ln_ch_b, a_w_in, a_conv_w, a_conv_b, a_w_out, b_w_pw1, b_b_pw1, b_conv_w, b_conv_b, b_ln_g, b_ln_b, b_w_pw2, b_b_pw2, f_w_up, f_conv_w, f_conv_b, f_w_gate, f_w_down, loss_target, m_ada_w, m_ada_b, m_ln_tok_g, m_ln_tok_b, m_ln_ch_g, m_ln_ch_b, m_a_w_in, m_a_conv_w, m_a_conv_b, m_a_w_out, m_b_w_pw1, m_b_b_pw1, m_b_conv_w, m_b_conv_b, m_b_ln_g, m_b_ln_b, m_b_w_pw2, m_b_b_pw2, m_f_w_up, m_f_conv_w, m_f_conv_b, m_f_w_gate, m_f_w_down, v_ada_w, v_ada_b, v_ln_tok_g, v_ln_tok_b, v_ln_ch_g, v_ln_ch_b, v_a_w_in, v_a_conv_w, v_a_conv_b, v_a_w_out, v_b_w_pw1, v_b_b_pw1, v_b_conv_w, v_b_conv_b, v_b_ln_g, v_b_ln_b, v_b_w_pw2, v_b_b_pw2, v_f_w_up, v_f_conv_w, v_f_conv_b, v_f_w_gate, v_f_w_down):
    weights = dict(ada_w=ada_w, ada_b=ada_b, ln_tok_g=ln_tok_g, ln_tok_b=ln_tok_b, ln_ch_g=ln_ch_g, ln_ch_b=ln_ch_b, a_w_in=a_w_in, a_conv_w=a_conv_w, a_conv_b=a_conv_b, a_w_out=a_w_out, b_w_pw1=b_w_pw1, b_b_pw1=b_b_pw1, b_conv_w=b_conv_w, b_conv_b=b_conv_b, b_ln_g=b_ln_g, b_ln_b=b_ln_b, b_w_pw2=b_w_pw2, b_b_pw2=b_b_pw2, f_w_up=f_w_up, f_conv_w=f_conv_w, f_conv_b=f_conv_b, f_w_gate=f_w_gate, f_w_down=f_w_down)
    mom_m = dict(ada_w=m_ada_w, ada_b=m_ada_b, ln_tok_g=m_ln_tok_g, ln_tok_b=m_ln_tok_b, ln_ch_g=m_ln_ch_g, ln_ch_b=m_ln_ch_b, a_w_in=m_a_w_in, a_conv_w=m_a_conv_w, a_conv_b=m_a_conv_b, a_w_out=m_a_w_out, b_w_pw1=m_b_w_pw1, b_b_pw1=m_b_b_pw1, b_conv_w=m_b_conv_w, b_conv_b=m_b_conv_b, b_ln_g=m_b_ln_g, b_ln_b=m_b_ln_b, b_w_pw2=m_b_w_pw2, b_b_pw2=m_b_b_pw2, f_w_up=m_f_w_up, f_conv_w=m_f_conv_w, f_conv_b=m_f_conv_b, f_w_gate=m_f_w_gate, f_w_down=m_f_w_down)
    mom_v = dict(ada_w=v_ada_w, ada_b=v_ada_b, ln_tok_g=v_ln_tok_g, ln_tok_b=v_ln_tok_b, ln_ch_g=v_ln_ch_g, ln_ch_b=v_ln_ch_b, a_w_in=v_a_w_in, a_conv_w=v_a_conv_w, a_conv_b=v_a_conv_b, a_w_out=v_a_w_out, b_w_pw1=v_b_w_pw1, b_b_pw1=v_b_b_pw1, b_conv_w=v_b_conv_w, b_conv_b=v_b_conv_b, b_ln_g=v_b_ln_g, b_ln_b=v_b_ln_b, b_w_pw2=v_b_w_pw2, b_b_pw2=v_b_b_pw2, f_w_up=v_f_w_up, f_conv_w=v_f_conv_w, f_conv_b=v_f_conv_b, f_w_gate=v_f_w_gate, f_w_down=v_f_w_down)
    names = list(weights)

    depth, d, n_ada = ada_w.shape
    assert depth == 2 and a_w_in.shape[0] == 1 and b_w_pw1.shape[0] == 1
    s_len = x.shape[1]
    f_loc = f_w_up.shape[-1]
    f_pad = -(-f_loc // LANES) * LANES
    f_all = NDEV * f_pad
    d_loc = d // NDEV
    ka, kb, kf = a_conv_w.shape[1], b_conv_w.shape[1], f_conv_w.shape[1]
    alpha = (2.0 * depth) ** 0.25
    assert a_w_in.shape[-1] == f_pad and f_pad % d_loc == 0
    me = 4 * lax.axis_index("x") + 2 * lax.axis_index("y") + lax.axis_index("c")

    small_shapes = [(d,), (ka, d_loc), (2 * d_loc,), (kb, d_loc), (d_loc,), (d_loc,), (d_loc,), (d_loc,),
                    (depth, kf, f_pad)]
    small_loc = _pack([c[0], a_conv_w[0], b_b_pw1[0], b_conv_w[0], b_conv_b[0], b_ln_g[0], b_ln_b[0],
                       b_b_pw2[0], _pad_last(f_conv_w, f_pad)])
    g_small, g_in, _ = _gather_two_level([small_loc, a_w_in.astype(BF16)], small_loc, "gather_first")

    (c_all, acw_g, bb1_g, bcw_g, bcb_g, blg_g, blb_g, bb2_g, fcw_g) = _unpack(g_small, small_shapes, (NDEV,))
    a_cw = acw_g.transpose(1, 0, 2).reshape(ka, d)
    b_cw = bcw_g.transpose(1, 0, 2).reshape(kb, d)
    b_b1 = bb1_g.reshape(1, 2 * d)
    b_cb, b_lg, b_lb, b_b2 = (t.reshape(1, d) for t in (bcb_g, blg_g, blb_g, bb2_g))
    f_cw = fcw_g.transpose(1, 2, 0, 3).reshape(depth, kf, f_all)
    f_cb = _pad_last(f_conv_b.reshape(depth, NDEV, f_loc), f_pad).reshape(depth, 1, f_all)

    ada_b_loc = lax.dynamic_slice(ada_b, (0, me * n_ada), (depth, n_ada))
    mod_part = _ada_fwd(c_all, ada_w, ada_b_loc, "ada_fwd")
    mod_g, mod_done = _exchange([mod_part.reshape(depth * NDEV, n_ada)], "gather", "gather_mod")
    mod_all = mod_g.reshape(NDEV, depth, NDEV, n_ada).transpose(1, 2, 0, 3).reshape(depth, NDEV, 6 * d)
    mod = lax.dynamic_slice(mod_all, (0, me, 0), (depth, 1, 6 * d))[:, 0]

    gather_out = _exchange_start([_after(a_w_out[0], mod_done).astype(BF16)], "gather_chips", "gather_out_start")
    up_pad = _pad_last(_after(f_w_up, gather_out[-1]), f_pad).astype(BF16)
    gate_pad = _pad_last(f_w_gate, f_pad).astype(BF16)
    down_pad = jnp.pad(f_w_down, ((0, 0), (0, f_pad - f_loc), (0, 0))).astype(BF16)
    col_f = [jnp.stack([up_pad[i], gate_pad[i]]) for i in range(depth)]
    row_b = jnp.concatenate([down_pad[1], b_w_pw2[0].astype(BF16)], axis=0)
    ridx_pw2 = f_pad // d_loc
    gather_f0 = _exchange_start([col_f[0], down_pad[0]], "gather_chips", "gather_f0_start")

    def mod_rows(i):
        return [mod[i:i + 1, j * d:(j + 1) * d] for j in range(6)]

    zeros_d = jnp.zeros((1, d), F32)
    zeros_f = jnp.zeros((1, f_all), F32)
    x0 = x[0]

    sh_t0, sc_t0, g_t0, sh_c0, sc_c0, g_c0 = mod_rows(0)
    sh_t1, sc_t1, g_t1, sh_c1, sc_c1, g_c1 = mod_rows(1)

    sc_t0 = _after(sc_t0, gather_f0[-1])
    bcv, = _mm_fwd(x0, sc_t0, sh_t0, jnp.zeros((1, 3 * d), F32), g_in, (0,), "a_in_fwd")
    y0 = _gateconv_fwd(bcv, a_cw, a_conv_b, "a_conv_fwd")
    g_out, landed = _exchange_wait(gather_out, y0, "gather_chips", "gather_out_wait")
    g_out, _ = _exchange_wait(_exchange_start([g_out], "forward", "gather_out_fwd_start"), landed, "forward",
                              "gather_out_fwd_wait")
    y_a, x1, xh1, rs1 = _mm_ln(y0, g_out, d_loc, 0, x0, g_t0, ln_tok_g[0:1], ln_tok_b[0:1], zeros_d,
                               alpha, "a_out_ln_fwd")

    def ffn_fwd(xin, sc, sh, gate, gam, bet, g_colf, g_rowf, layer, tag):
        u0, vg = _mm_fwd(xin, sc, sh, zeros_f, g_colf, (0, 1), "f_upgate_fwd" + tag)
        t, y, xo, xh, rs = _ffn_tail_fwd(u0, vg, f_cw[layer], f_cb[layer], g_rowf, xin, gate, gam, bet, alpha,
                                         "f_tail_fwd" + tag)
        return u0, vg, t, y, xo, xh, rs

    g_colf0, g_rowf0, landed = _exchange_wait(gather_f0, x1, "gather_chips", "gather_f0_wait")
    g_colf0, g_rowf0, landed = _exchange_wait(
        _exchange_start([g_colf0, g_rowf0], "forward", "gather_f0_fwd_start"), landed, "forward", "gather_f0_fwd_wait")
    gather_1 = _exchange_start([_after(b_w_pw1, landed).astype(BF16), col_f[1], row_b], "gather_chips",
                               "gather_1_start")
    sc_c0 = _after(sc_c0, gather_1[-1])
    u0_0, vg_0, t_0, y_f0, x2, xh2, rs2 = ffn_fwd(x1, sc_c0, sh_c0, g_c0, ln_ch_g[0:1], ln_ch_b[0:1],
                                                  g_colf0, g_rowf0, 0, "0")

    *lands_1, landed = _exchange_wait(gather_1, x2, "gather_chips", "gather_1_wait")
    g_pw1, g_colf1, g_rowb, _ = _exchange_wait(_exchange_start(lands_1, "forward", "gather_1_fwd_start"), landed,
                                                 "forward", "gather_1_fwd_wait")
    ub, = _mm_fwd(x2, sc_t1, sh_t1, b_b1, g_pw1, (0,), "b_pw1_fwd")
    a2, a4 = _b_mid_fwd(ub, b_cw, b_cb, b_lg, b_lb, "b_mid_fwd")
    y_b, x3, xh3, rs3 = _mm_ln(a4, g_rowb, d_loc, ridx_pw2, x2, g_t1, ln_tok_g[1:2], ln_tok_b[1:2], b_b2,
                               alpha, "b_pw2_ln_fwd")
    u0_1, vg_1, t_1, y_f1, x4, xh4, rs4 = ffn_fwd(x3, sc_c1, sh_c1, g_c1, ln_ch_g[1:2], ln_ch_b[1:2],
                                                  g_colf1, g_rowb, 1, "1")

    ln_f1 = (xh4, rs4, ln_ch_g[1:2], y_f1, g_c1)
    ln_b = (xh3, rs3, ln_tok_g[1:2], y_b, g_t1)
    ln_f0 = (xh2, rs2, ln_ch_g[0:1], y_f0, g_c0)
    ln_a = (xh1, rs1, ln_tok_g[0:1], y_a, g_t0)
    dy, dres, accf1, loss_part = _loss_head(x4, loss_target[0], ln_f1, alpha, "loss_head")

    def ffn_bwd(dy, dres, xin, sc, sh, u0, vg, t, g_colf, g_rowf, ln_below, layer, tag):
        dw_down = _mm_tn_row(t, dy, f_pad, f_loc, "f_down_dw" + tag)
        du0, dvg, dcw, dcb, dy_below, dres_below, acc_below, acc2 = _ffn_core_bwd(
            dy, u0, vg, f_cw[layer], f_cb[layer], g_rowf, g_colf, xin, sc, dres, ln_below, alpha,
            "f_core_bwd" + tag)
        dw_up = _mm_tn_col_t(xin, sc, sh, du0, f_loc, "f_up_dw" + tag)
        dw_gate = _mm_tn_col_t(xin, sc, sh, dvg, f_loc, "f_gate_dw" + tag)
        scatter = _exchange_start([dw_up, dw_gate, dw_down], "scatter", "scatter_f%s_start" % tag)
        return dy_below, dres_below, acc_below, acc2, scatter, dcw, dcb

    dy, dres, accb, acc2f1, scatter_f1, dfcw1, dfcb1 = ffn_bwd(
        dy, dres, x3, sc_c1, sh_c1, u0_1, vg_1, t_1, g_colf1, g_rowb, ln_b, 1, "1")

    da4 = _mm_nt_row(dy, g_rowb, d_loc, ridx_pw2, "b_pw2_dx")
    dw_pw2 = _mm_tn_row(a4, dy, d_loc, d_loc, "b_pw2_dw")
    du, dbcw, dbcb, dblg, dblb, dbb1 = _b_mid_bwd(ub, a2, da4, b_cw, _after(b_lg, scatter_f1[-1]), b_lb, "b_mid_bwd")
    dw_pw1 = _mm_tn_col(x2, sc_t1, sh_t1, du, "b_pw1_dw")
    scatter_b = _exchange_start([dw_pw1, dw_pw2], "scatter", "scatter_b_start")
    dy, dres, accf0, acc2b = _mm_nt_mod([du], g_pw1, (0,), x2, _after(sc_t1, scatter_b[-1]), dres, "b_pw1_dx",
                                        ln=ln_f0, alpha=alpha)

    dy, dres, acca, acc2f0, scatter_f0, dfcw0, dfcb0 = ffn_bwd(
        dy, dres, x1, sc_c0, sh_c0, u0_0, vg_0, t_0, g_colf0, g_rowf0, ln_a, 0, "0")

    dy0 = _mm_nt_row(dy, g_out, d_loc, 0, "a_out_dx")
    dbcv, dacw, dacb = _gateconv_bwd(bcv, dy0, a_cw, _after(a_conv_b, scatter_f0[-1]), "a_conv_bwd")
    dx0, acc2a = _mm_nt_mod([dbcv], g_in, (0,), x0, sc_t0, dres, "a_in_dx")

    def dmod_row(acc2_t, acc_t, acc2_c, acc_c):
        return jnp.concatenate([acc2_t[1], acc2_t[0], acc_t[2], acc2_c[1], acc2_c[0], acc_c[2]])

    dmod = jnp.stack([dmod_row(acc2a, acca, acc2f0, accf0), dmod_row(acc2b, accb, acc2f1, accf1)])

    def unpad_f(a):
        return a.reshape(a.shape[:-1] + (NDEV, f_pad))[..., :f_loc].reshape(a.shape[:-1] + (NDEV * f_loc,))

    small_grads = [
        dmod,
        jnp.stack([acca[0], accb[0]]), jnp.stack([acca[1], accb[1]]),
        jnp.stack([accf0[0], accf1[0]]), jnp.stack([accf0[1], accf1[1]]),
        dacb,
        unpad_f(jnp.concatenate([dfcb0, dfcb1], axis=0)),
        dacw, dbb1, dbcw, dbcb, dblg, dblb, accb[3:4],
        jnp.stack([dfcw0, dfcw1]),
        loss_part[0:1, 0:1],
    ]
    small_grad_shapes = [tuple(g.shape) for g in small_grads]
    gather_small = _exchange_start([_pack(small_grads)], "gather", "gather_small_start")

    dw_in = _mm_tn_col(x0, _after(sc_t0, gather_small[-1]), sh_t0, dbcv, "a_in_dw")
    dw_out = _mm_tn_row(y0, dy, d_loc, d_loc, "a_out_dw")
    scatter_a = _exchange_start([dw_in, dw_out], "scatter", "scatter_a_start")

    grads, deltas, new_m, new_v = {}, {}, {}, {}

    def adamw(k, glist, transposed=False):
        def view(a):
            a = jnp.swapaxes(a, 1, 2) if transposed else a
            return a.reshape(len(glist), -1, a.shape[-1])

        w = view(weights[k])
        outs = _adamw(w, [g.reshape(g.shape[0], -1, w.shape[-1]) for g in glist],
                      view(mom_m[k]), view(mom_v[k]), "adamw_" + k)
        if transposed:
            outs = [jnp.swapaxes(o, 1, 2) for o in outs]
        grads[k], deltas[k], new_m[k], new_v[k] = (o.reshape(weights[k].shape) for o in outs)

    r_up1, r_gate1, r_down1, _ = _exchange_wait(scatter_f1, scatter_a[-1], "scatter", "scatter_f1_wait")
    r_pw1, r_pw2, _ = _exchange_wait(scatter_b, r_down1, "scatter", "scatter_b_wait")
    adamw("b_w_pw1", [r_pw1])
    adamw("b_w_pw2", [r_pw2])
    r_up0, r_gate0, r_down0, _ = _exchange_wait(scatter_f0, deltas["b_w_pw2"], "scatter", "scatter_f0_wait")
    adamw("f_w_up", [r_up0, r_up1], transposed=True)
    adamw("f_w_gate", [r_gate0, r_gate1], transposed=True)
    adamw("f_w_down", [r_down0, r_down1])

    sg_all, _ = _exchange_wait(gather_small, deltas["f_w_down"], "gather", "gather_small_wait")
    sg_sum = _sum_parts(sg_all, "sum_small_grads")
    (g_ada_b, g_ltg, g_ltb, g_lcg, g_lcb, g_acb, g_fcb, g_acw, g_bb1, g_bcw, g_bcb, g_blg, g_blb, g_bb2,
     g_fcw, loss_all) = _unpack(sg_sum, small_grad_shapes)
    loss = loss_all[0, 0]

    def my_cols(a, width):
        return lax.dynamic_slice_in_dim(a, me * width, width, axis=a.ndim - 1)

    g_fcw_loc = my_cols(g_fcw, f_pad)[..., :f_loc]
    small = dict(
        ada_b=g_ada_b, ln_tok_g=g_ltg, ln_tok_b=g_ltb, ln_ch_g=g_lcg, ln_ch_b=g_lcb, a_conv_b=g_acb, f_conv_b=g_fcb,
        a_conv_w=my_cols(g_acw, d_loc)[None], b_b_pw1=my_cols(g_bb1, 2 * d_loc), b_conv_w=my_cols(g_bcw, d_loc)[None],
        b_conv_b=my_cols(g_bcb, d_loc), b_ln_g=my_cols(g_blg, d_loc), b_ln_b=my_cols(g_blb, d_loc),
        b_b_pw2=my_cols(g_bb2, d_loc), f_conv_w=g_fcw_loc)

    dmod_all = sg_all.reshape(NDEV, -1)[:, :depth * 6 * d].reshape(NDEV, depth, 6 * d)
    dmod_cols = my_cols(dmod_all, n_ada).transpose(1, 0, 2)
    g_ada_w = _ada_bwd(c_all.T, dmod_cols, "ada_bwd")

    adamw("ada_w", [g_ada_w[0:1], g_ada_w[1:2]])
    for k, g in small.items():
        adamw(k, [g[None]])

    r_in, r_out, _ = _exchange_wait(scatter_a, deltas["ada_w"], "scatter", "scatter_a_wait")
    adamw("a_w_in", [r_in])
    adamw("a_w_out", [r_out])

    return (loss, dx0[None], *[grads[k] for k in names], *[deltas[k] for k in names],
            *[new_m[k] for k in names], *[new_v[k] for k in names])
```

```python
import jax
import jax.numpy as jnp
from jax import lax
from jax.experimental import pallas as pl
from jax.experimental.pallas import tpu as pltpu

NDEV = 8
MESH_AXES = ("x", "y", "c")
LANES = 128
SUBLANES = 8
VMEM_LIMIT = 56 * 1024 * 1024
LN_EPS = 1e-5
SHORT_PAD = 16
LONG_PAD = 32
CHUNK = 16
ADAM_LR, ADAM_B1, ADAM_B2, ADAM_EPS, ADAM_WD, ADAM_STEP = 0.001, 0.9, 0.999, 1e-08, 0.01, 10

F32 = jnp.float32
BF16 = jnp.bfloat16
MESH = pl.DeviceIdType.MESH
NT = (((1,), (1,)), ((), ()))
TN = (((0,), (0,)), ((), ()))


def _tile(n, target, mult=SUBLANES):
    best = None
    for t in range(mult, min(n, target) + 1, mult):
        if n % t == 0:
            best = t
    return best if best is not None else n


def _full(shape):
    nd = len(shape)
    return pl.BlockSpec(shape, lambda *_: (0,) * nd)


def _cp(*sem):
    return pltpu.CompilerParams(dimension_semantics=sem, vmem_limit_bytes=VMEM_LIMIT)


def _sigmoid(x):
    return 1.0 / (1.0 + jnp.exp(-x))


def _peer(x, y, c, d):
    return ((1 - x) if d & 4 else x, (1 - y) if d & 2 else y, (1 - c) if d & 1 else c)


def _lin(p):
    return 4 * p[0] + 2 * p[1] + p[2]


CHIP_MASKS = (2, 4, 6)
MODES_PER_ARRAY = {"gather": NDEV - 1, "scatter": NDEV - 1, "gather_chips": 1 + len(CHIP_MASKS),
                   "forward": len(CHIP_MASKS)}


def _transfers(mode):
    x, y, c = (lax.axis_index(a) for a in MESH_AXES)
    me = _lin((x, y, c))
    if mode == "forward":
        sibling = (x, y, 1 - c)
        return [(sibling, ("land", _lin(_peer(x, y, c, q))), _lin(_peer(x, y, c, q)), _lin(_peer(x, y, c, q ^ 1)))
                for q in CHIP_MASKS]
    masks = (1,) + CHIP_MASKS if mode == "gather_chips" else range(1, NDEV)
    out = []
    for d in masks:
        peer = _peer(x, y, c, d)
        source = ("block", _lin(peer)) if mode == "scatter" else ("whole", None)
        out.append((peer, source, me, _lin(peer)))
    return out


def _remote_copies(src_refs, land_refs, send_sems, recv_sems, mode):
    transfers = _transfers(mode)
    sends, recvs = [], []
    for i, land_ref in enumerate(land_refs):
        for t, (peer, (kind, slot), there, here) in enumerate(transfers):
            k = i * len(transfers) + t
            src = land_ref.at[slot] if kind == "land" else src_refs[i].at[slot] if kind == "block" else src_refs[i]
            for dst_slot, out in ((there, sends), (here, recvs)):
                out.append(pltpu.make_async_remote_copy(
                    src_ref=src, dst_ref=land_ref.at[dst_slot], send_sem=send_sems.at[k], recv_sem=recv_sems.at[k],
                    device_id=peer, device_id_type=MESH))
    return sends, recvs


def _exchange(srcs, mode, name):
    n = len(srcs)
    gather = mode == "gather"

    def body(*refs):
        src_refs, out_refs, token = refs[:n], refs[n:2 * n], refs[2 * n]
        send_sems, recv_sems, local_sems = refs[2 * n + 1:]
        me = _lin(tuple(lax.axis_index(a) for a in MESH_AXES))
        local = []
        for i in range(n):
            mine = src_refs[i] if gather else src_refs[i].at[me]
            cp = pltpu.make_async_copy(mine, out_refs[i].at[me], local_sems.at[i])
            cp.start()
            local.append(cp)
        sends, recvs = _remote_copies(src_refs, out_refs, send_sems, recv_sems, mode)
        for snd in sends:
            snd.start()
        token[...] = jnp.zeros_like(token)
        for snd, rcv in zip(sends, recvs):
            snd.wait_send()
            rcv.wait_recv()
        for cp in local:
            cp.wait()

    out_shape = [jax.ShapeDtypeStruct(((NDEV,) + s.shape) if gather else s.shape, s.dtype) for s in srcs]
    out_shape.append(jax.ShapeDtypeStruct((SUBLANES, LANES), F32))
    any_spec = pl.BlockSpec(memory_space=pl.ANY)
    return pl.pallas_call(
        body, name=name, out_shape=out_shape,
        in_specs=[any_spec] * n, out_specs=[any_spec] * n + [pl.BlockSpec(memory_space=pltpu.VMEM)],
        scratch_shapes=[pltpu.SemaphoreType.DMA((n * (NDEV - 1),)),
                        pltpu.SemaphoreType.DMA((n * (NDEV - 1),)),
                        pltpu.SemaphoreType.DMA((n,))],
    )(*srcs)


HBM_SPEC = pl.BlockSpec(memory_space=pltpu.HBM)
SEM_SPEC = pl.BlockSpec(memory_space=pltpu.SEMAPHORE)
SIDE_EFFECT = pltpu.SideEffectType.DATAFLOW_SIDE_EFFECTING


def _exchange_start(arrays, mode, name):
    me = _lin(tuple(lax.axis_index(a) for a in MESH_AXES))
    if mode == "forward":
        srcs, lands = [], list(arrays)
    else:
        srcs, lands = list(arrays), []
        for s in srcs:
            own = lax.dynamic_index_in_dim(s, me, 0, keepdims=False) if mode == "scatter" else s
            shape = s.shape if mode == "scatter" else (NDEV,) + s.shape
            lands.append(lax.dynamic_update_index_in_dim(lax.empty(shape, s.dtype), own, me, 0))
    ns, n = len(srcs), len(lands)

    def body(*refs):
        src_refs, land_refs = refs[:ns], refs[ns:ns + n]
        send_sems, recv_sems, token = refs[ns + n], refs[ns + n + 1], refs[-1]
        sends, _ = _remote_copies(src_refs, land_refs, send_sems, recv_sems, mode)
        for snd in sends:
            snd.start()
        token[...] = jnp.zeros_like(token)

    operands = [pltpu.with_memory_space_constraint(a, pltpu.HBM) for a in srcs + lands]
    nsem = n * MODES_PER_ARRAY[mode]
    return pl.pallas_call(
        body, name=name,
        out_shape=(pltpu.SemaphoreType.DMA((nsem,)), pltpu.SemaphoreType.DMA((nsem,)),
                   *[pltpu.HBM(a.shape, a.dtype) for a in operands],
                   jax.ShapeDtypeStruct((SUBLANES, LANES), F32)),
        in_specs=[HBM_SPEC] * (ns + n),
        out_specs=(SEM_SPEC, SEM_SPEC, *([HBM_SPEC] * (ns + n)), pl.BlockSpec(memory_space=pltpu.VMEM)),
        input_output_aliases={i: 2 + i for i in range(ns + n)},
        compiler_params=pltpu.CompilerParams(has_side_effects=SIDE_EFFECT),
    )(*operands)


def _exchange_wait(handle, after, mode, name):
    send_sems, recv_sems, *thru = handle[:-1]
    n = len(thru) if mode == "forward" else len(thru) // 2
    ns = len(thru) - n

    def body(*refs):
        src_refs, land_refs = refs[:ns], refs[ns:ns + n]
        sends, recvs = _remote_copies(src_refs, land_refs, refs[ns + n], refs[ns + n + 1], mode)
        for snd, rcv in zip(sends, recvs):
            snd.wait_send()
            rcv.wait_recv()
        refs[-1][...] = jnp.zeros_like(refs[-1])

    outs = pl.pallas_call(
        body, name=name,
        out_shape=(*[pltpu.HBM(a.shape, a.dtype) for a in thru], jax.ShapeDtypeStruct((SUBLANES, LANES), F32)),
        in_specs=[HBM_SPEC] * (ns + n) + [SEM_SPEC, SEM_SPEC, pl.BlockSpec(memory_space=pl.ANY)],
        out_specs=[HBM_SPEC] * (ns + n) + [pl.BlockSpec(memory_space=pltpu.VMEM)],
        input_output_aliases={i: i for i in range(ns + n)},
        compiler_params=pltpu.CompilerParams(has_side_effects=SIDE_EFFECT),
    )(*thru, send_sems, recv_sems, after)
    return outs[ns:]


def _gather_two_level(srcs, after, name):
    first = _exchange_start(srcs, "gather_chips", name + "_chips_start")
    *lands, token = _exchange_wait(first, after, "gather_chips", name + "_chips_wait")
    second = _exchange_start(lands, "forward", name + "_forward_start")
    return _exchange_wait(second, token, "forward", name + "_forward_wait")


def _after(value, token):
    return value + token[0, 0]


ANY_SPEC = pl.BlockSpec(memory_space=pl.ANY)


def _load_cols(wg_ref, widx, w_ref, sems):
    n = wg_ref.shape[-1]
    copies = [pltpu.make_async_copy(wg_ref.at[k, widx], w_ref.at[:, pl.ds(k * n, n)], sems.at[k])
              for k in range(NDEV)]
    for cp in copies:
        cp.start()
    for cp in copies:
        cp.wait()


def _load_rows(wg_ref, r, ridx, w_ref, sems):
    copies = [pltpu.make_async_copy(wg_ref.at[k, pl.ds(ridx * r, r)], w_ref.at[pl.ds(k * r, r)], sems.at[k])
              for k in range(NDEV)]
    for cp in copies:
        cp.start()
    for cp in copies:
        cp.wait()


def _mm_fwd(x, sc, sh, bias, wg, widxs, name):
    s_len, kdim = x.shape
    ncol = NDEV * wg.shape[-1]
    tm = _tile(s_len, 512)
    nw = len(widxs)

    def body(x_ref, sc_ref, sh_ref, b_ref, wg_ref, *rest):
        o_refs, w_refs, sems = rest[:nw], rest[nw:2 * nw], rest[2 * nw]

        @pl.when(pl.program_id(0) == 0)
        def _():
            for i, w_ref in enumerate(w_refs):
                _load_cols(wg_ref, widxs[i], w_ref, sems.at[i])

        h = (x_ref[...] * (1.0 + sc_ref[...]) + sh_ref[...]).astype(BF16)
        for w_ref, o_ref in zip(w_refs, o_refs):
            o_ref[...] = (jnp.dot(h, w_ref[...], preferred_element_type=F32) + b_ref[...]).astype(BF16)

    return pl.pallas_call(
        body, name=name, grid=(s_len // tm,),
        in_specs=[pl.BlockSpec((tm, kdim), lambda i: (i, 0)), _full((1, kdim)), _full((1, kdim)),
                  _full((1, ncol)), ANY_SPEC],
        out_specs=[pl.BlockSpec((tm, ncol), lambda i: (i, 0))] * nw,
        out_shape=[jax.ShapeDtypeStruct((s_len, ncol), BF16)] * nw,
        scratch_shapes=[pltpu.VMEM((kdim, ncol), BF16)] * nw + [pltpu.SemaphoreType.DMA((nw, NDEV))],
        compiler_params=_cp("arbitrary"),
    )(x, sc, sh, bias, wg)


def _mm_ln(a, wg, r, ridx, xres, gate, gam, bet, bias, alpha, name):
    s_len = a.shape[0]
    d = wg.shape[-1]
    tm = _tile(s_len, 512)

    def body(a_ref, wg_ref, x_ref, g_ref, gam_ref, bet_ref, b_ref, y_ref, xo_ref, xh_ref, rs_ref, w_ref, sems):
        @pl.when(pl.program_id(0) == 0)
        def _():
            _load_rows(wg_ref, r, ridx, w_ref, sems)

        y = jnp.dot(a_ref[...], w_ref[...], preferred_element_type=F32) + b_ref[...]
        z = alpha * x_ref[...] + g_ref[...] * y
        mu = jnp.mean(z, axis=-1, keepdims=True)
        zc = z - mu
        var = jnp.mean(zc * zc, axis=-1, keepdims=True)
        rstd = lax.rsqrt(var + LN_EPS)
        xh = zc * rstd
        y_ref[...] = y.astype(BF16)
        xh_ref[...] = xh
        rs_ref[...] = rstd
        xo_ref[...] = xh * gam_ref[...] + bet_ref[...]

    row = pl.BlockSpec((tm, d), lambda i: (i, 0))
    vec = _full((1, d))
    return pl.pallas_call(
        body, name=name, grid=(s_len // tm,),
        in_specs=[pl.BlockSpec((tm, NDEV * r), lambda i: (i, 0)), ANY_SPEC, row, vec, vec, vec, vec],
        out_specs=[row, row, row, pl.BlockSpec((tm, 1), lambda i: (i, 0))],
        out_shape=[jax.ShapeDtypeStruct((s_len, d), BF16)] + [jax.ShapeDtypeStruct((s_len, d), F32)] * 2
        + [jax.ShapeDtypeStruct((s_len, 1), F32)],
        scratch_shapes=[pltpu.VMEM((NDEV * r, d), BF16), pltpu.SemaphoreType.DMA((NDEV,))],
        compiler_params=_cp("arbitrary"),
    )(a, wg, xres, gate, gam, bet, bias)


def _ln_in_specs(tm, d):
    row = pl.BlockSpec((tm, d), lambda i: (i, 0))
    return [row, pl.BlockSpec((tm, 1), lambda i: (i, 0)), _full((1, d)), row, _full((1, d))]


def _ln_out_specs(s_len, tm, d):
    row = pl.BlockSpec((tm, d), lambda i: (i, 0))
    return ([row, row, _full((SUBLANES, d))],
            [jax.ShapeDtypeStruct((s_len, d), BF16), jax.ShapeDtypeStruct((s_len, d), F32),
             jax.ShapeDtypeStruct((SUBLANES, d), F32)])


def _ln_bwd_rows(dxo, ln_refs, out_refs, alpha):
    xh_ref, rs_ref, gam_ref, y_ref, g_ref = ln_refs
    dy_ref, dres_ref, acc_ref = out_refs
    xh = xh_ref[...]
    dxh = dxo * gam_ref[...]
    m1 = jnp.mean(dxh, axis=-1, keepdims=True)
    m2 = jnp.mean(dxh * xh, axis=-1, keepdims=True)
    dz = rs_ref[...] * (dxh - m1 - xh * m2)
    dy = g_ref[...] * dz
    dy_ref[...] = dy.astype(BF16)
    dres_ref[...] = alpha * dz
    acc_ref[0:1, :] += jnp.sum(dxo * xh, axis=0, keepdims=True)
    acc_ref[1:2, :] += jnp.sum(dxo, axis=0, keepdims=True)
    acc_ref[2:3, :] += jnp.sum(dz * y_ref[...].astype(F32), axis=0, keepdims=True)
    acc_ref[3:4, :] += jnp.sum(dy, axis=0, keepdims=True)


def _mm_nt_row(dy, wg, r, ridx, name):
    s_len, d = dy.shape
    tm = _tile(s_len, 512)

    def body(dy_ref, wg_ref, o_ref, w_ref, sems):
        @pl.when(pl.program_id(0) == 0)
        def _():
            _load_rows(wg_ref, r, ridx, w_ref, sems)

        o_ref[...] = lax.dot_general(dy_ref[...], w_ref[...], NT, preferred_element_type=F32).astype(BF16)

    return pl.pallas_call(
        body, name=name, grid=(s_len // tm,),
        in_specs=[pl.BlockSpec((tm, d), lambda i: (i, 0)), ANY_SPEC],
        out_specs=pl.BlockSpec((tm, NDEV * r), lambda i: (i, 0)),
        out_shape=jax.ShapeDtypeStruct((s_len, NDEV * r), BF16),
        scratch_shapes=[pltpu.VMEM((NDEV * r, d), BF16), pltpu.SemaphoreType.DMA((NDEV,))],
        compiler_params=_cp("arbitrary"),
    )(dy, wg)


def _mm_nt_mod(dos, wg, widxs, xin, sc, dres, name, ln=None, alpha=None):
    s_len, kdim = xin.shape
    ncol = NDEV * wg.shape[-1]
    tm = _tile(s_len, 512)
    nw = len(widxs)
    nln = 0 if ln is None else len(ln)
    nout = 2 if ln is None else 4

    def body(*refs):
        do_refs, wg_ref = refs[:nw], refs[nw]
        x_ref, sc_ref, dres_ref = refs[nw + 1:nw + 4]
        ln_refs = refs[nw + 4:nw + 4 + nln]
        out_refs = refs[nw + 4 + nln:nw + 4 + nln + nout]
        w_refs, sems = refs[nw + 4 + nln + nout:-1], refs[-1]
        acc_ref = out_refs[-1]

        @pl.when(pl.program_id(0) == 0)
        def _():
            for ref in out_refs[nout // 2:]:
                ref[...] = jnp.zeros_like(ref)
            for i, w_ref in enumerate(w_refs):
                _load_cols(wg_ref, widxs[i], w_ref, sems.at[i])

        dh = None
        for do_ref, w_ref in zip(do_refs, w_refs):
            p = lax.dot_general(do_ref[...], w_ref[...], NT, preferred_element_type=F32)
            dh = p if dh is None else dh + p
        dx = dh * (1.0 + sc_ref[...]) + dres_ref[...]
        if ln is None:
            out_refs[0][...] = dx
        else:
            _ln_bwd_rows(dx, ln_refs, out_refs[0:3], alpha)
        acc_ref[0:1, :] += jnp.sum(dh * x_ref[...], axis=0, keepdims=True)
        acc_ref[1:2, :] += jnp.sum(dh, axis=0, keepdims=True)

    row = pl.BlockSpec((tm, kdim), lambda i: (i, 0))
    if ln is None:
        out_specs, out_shape = [row], [jax.ShapeDtypeStruct((s_len, kdim), F32)]
    else:
        out_specs, out_shape = _ln_out_specs(s_len, tm, kdim)
    return pl.pallas_call(
        body, name=name, grid=(s_len // tm,),
        in_specs=[pl.BlockSpec((tm, ncol), lambda i: (i, 0))] * nw + [ANY_SPEC, row, _full((1, kdim)), row]
        + ([] if ln is None else _ln_in_specs(tm, kdim)),
        out_specs=out_specs + [_full((SUBLANES, kdim))],
        out_shape=out_shape + [jax.ShapeDtypeStruct((SUBLANES, kdim), F32)],
        scratch_shapes=[pltpu.VMEM((kdim, ncol), BF16)] * nw + [pltpu.SemaphoreType.DMA((nw, NDEV))],
        compiler_params=_cp("arbitrary"),
    )(*dos, wg, xin, sc, dres, *([] if ln is None else ln))


def _mm_tn_col(x, sc, sh, do, name):
    s_len, kdim = x.shape
    n = do.shape[1] // NDEV
    ts = _tile(s_len, 512)
    nsteps = s_len // ts

    def body(x_ref, sc_ref, sh_ref, do_ref, o_ref, acc_ref):
        @pl.when(pl.program_id(0) == 0)
        def _():
            acc_ref[...] = jnp.zeros_like(acc_ref)

        h = (x_ref[...] * (1.0 + sc_ref[...]) + sh_ref[...]).astype(BF16)
        acc_ref[...] += lax.dot_general(h, do_ref[...], TN, preferred_element_type=F32)

        @pl.when(pl.program_id(0) == nsteps - 1)
        def _():
            for k in range(NDEV):
                o_ref[k] = acc_ref[:, k * n:(k + 1) * n].astype(BF16)

    return pl.pallas_call(
        body, name=name, grid=(nsteps,),
        in_specs=[pl.BlockSpec((ts, kdim), lambda i: (i, 0)), _full((1, kdim)), _full((1, kdim)),
                  pl.BlockSpec((ts, NDEV * n), lambda i: (i, 0))],
        out_specs=_full((NDEV, kdim, n)),
        out_shape=jax.ShapeDtypeStruct((NDEV, kdim, n), BF16),
        scratch_shapes=[pltpu.VMEM((kdim, NDEV * n), F32)],
        compiler_params=_cp("arbitrary"),
    )(x, sc, sh, do)


def _mm_tn_col_t(x, sc, sh, do, rows_out, name):
    s_len, kdim = x.shape
    n = do.shape[1] // NDEV
    ts = _tile(s_len, 512)
    nsteps = s_len // ts

    def body(x_ref, sc_ref, sh_ref, do_ref, o_ref, acc_ref):
        @pl.when(pl.program_id(0) == 0)
        def _():
            acc_ref[...] = jnp.zeros_like(acc_ref)

        h = (x_ref[...] * (1.0 + sc_ref[...]) + sh_ref[...]).astype(BF16)
        acc_ref[...] += lax.dot_general(do_ref[...], h, TN, preferred_element_type=F32)

        @pl.when(pl.program_id(0) == nsteps - 1)
        def _():
            for k in range(NDEV):
                o_ref[k] = acc_ref[k * n:k * n + rows_out, :].astype(BF16)

    return pl.pallas_call(
        body, name=name, grid=(nsteps,),
        in_specs=[pl.BlockSpec((ts, kdim), lambda i: (i, 0)), _full((1, kdim)), _full((1, kdim)),
                  pl.BlockSpec((ts, NDEV * n), lambda i: (i, 0))],
        out_specs=_full((NDEV, rows_out, kdim)),
        out_shape=jax.ShapeDtypeStruct((NDEV, rows_out, kdim), BF16),
        scratch_shapes=[pltpu.VMEM((NDEV * n, kdim), F32)],
        compiler_params=_cp("arbitrary"),
    )(x, sc, sh, do)


def _mm_tn_row(a, dy, r, rows_out, name):
    s_len, d = dy.shape
    ts = _tile(s_len, 512)
    nsteps = s_len // ts

    def body(a_ref, dy_ref, o_ref, acc_ref):
        @pl.when(pl.program_id(0) == 0)
        def _():
            acc_ref[...] = jnp.zeros_like(acc_ref)

        acc_ref[...] += lax.dot_general(a_ref[...], dy_ref[...], TN, preferred_element_type=F32)

        @pl.when(pl.program_id(0) == nsteps - 1)
        def _():
            for k in range(NDEV):
                o_ref[k] = acc_ref[k * r:k * r + rows_out, :].astype(BF16)

    return pl.pallas_call(
        body, name=name, grid=(nsteps,),
        in_specs=[pl.BlockSpec((ts, NDEV * r), lambda i: (i, 0)), pl.BlockSpec((ts, d), lambda i: (i, 0))],
        out_specs=_full((NDEV, rows_out, d)),
        out_shape=jax.ShapeDtypeStruct((NDEV, rows_out, d), BF16),
        scratch_shapes=[pltpu.VMEM((NDEV * r, d), F32)],
        compiler_params=_cp("arbitrary"),
    )(a, dy)


def _prev_spec(ts, pad, cb, col):
    return pl.BlockSpec((pad, cb), lambda *g: (jnp.maximum(g[-1] * (ts // pad) - 1, 0), col(g)))


def _next_spec(ts, pad, cb, col, s_len):
    return pl.BlockSpec((pad, cb), lambda *g: (jnp.minimum((g[-1] + 1) * (ts // pad), s_len // pad - 1), col(g)))


class _F32Loads:
    def __init__(self, ref):
        self.ref = ref

    def __getitem__(self, idx):
        return self.ref[idx].astype(F32)


def _direct(buf_ref):
    return lambda off, rows: buf_ref[off:off + rows, :]


def _make_shifts(sh_ref, nrows):
    for r in range(1, SUBLANES):
        sh_ref[r, 0:nrows - SUBLANES, :] = sh_ref[0, r:r + nrows - SUBLANES, :]


def _shifted(sh_ref):
    def read(off, rows):
        r = off % SUBLANES
        return sh_ref[r, off - r:off - r + rows, :]
    return read


def _conv_fwd_rows(read, w_ref, b_ref, ktaps, pad, r0, rows):
    acc = None
    for j in range(ktaps):
        term = w_ref[ktaps - 1 - j:ktaps - j, :] * read(pad - j + r0, rows)
        acc = term if acc is None else acc + term
    return acc + b_ref[...]


def _conv_bwd_rows(read, x_rows, w_ref, dwacc_ref, ktaps, r0, rows):
    acc = None
    for j in range(ktaps):
        sl = read(j + r0, rows)
        term = w_ref[ktaps - 1 - j:ktaps - j, :] * sl
        acc = term if acc is None else acc + term
        prod = x_rows * sl
        fold = prod[0:SUBLANES]
        for q in range(1, rows // SUBLANES):
            fold = fold + prod[q * SUBLANES:(q + 1) * SUBLANES]
        tap = ktaps - 1 - j
        dwacc_ref[tap * SUBLANES:(tap + 1) * SUBLANES, :] += fold
    return acc


def _flush_dw(dwacc_ref, dw_ref, ktaps):
    for tap in range(ktaps):
        dw_ref[tap:tap + 1, :] = jnp.sum(dwacc_ref[tap * SUBLANES:(tap + 1) * SUBLANES, :], axis=0, keepdims=True)


def _gateconv_fwd(bcv, cw, cb, name):
    s_len, d3 = bcv.shape
    d = d3 // 3
    ktaps = cw.shape[0]
    pad = SHORT_PAD
    ts = _tile(s_len, 256)

    def body(gb_ref, gc_ref, v_ref, gcp_ref, vp_ref, w_ref, b_ref, o_ref, pbuf):
        gb_ref, gc_ref, v_ref, gcp_ref, vp_ref = map(_F32Loads, (gb_ref, gc_ref, v_ref, gcp_ref, vp_ref))
        s = pl.program_id(0)
        pbuf[0:pad, :] = jnp.where(s > 0, gcp_ref[...] * vp_ref[...], 0.0)
        pbuf[pad:pad + ts, :] = gc_ref[...] * v_ref[...]
        for r0 in range(0, ts, CHUNK):
            q = _conv_fwd_rows(_direct(pbuf), w_ref, b_ref, ktaps, pad, r0, CHUNK)
            o_ref[r0:r0 + CHUNK, :] = (gb_ref[r0:r0 + CHUNK, :] * q).astype(BF16)

    def cur(part):
        return pl.BlockSpec((ts, d), lambda s: (s, part))

    return pl.pallas_call(
        body, name=name, grid=(s_len // ts,),
        in_specs=[cur(0), cur(1), cur(2),
                  _prev_spec(ts, pad, d, lambda g: 1), _prev_spec(ts, pad, d, lambda g: 2),
                  _full((ktaps, d)), _full((1, d))],
        out_specs=pl.BlockSpec((ts, d), lambda s: (s, 0)),
        out_shape=jax.ShapeDtypeStruct((s_len, d), BF16),
        scratch_shapes=[pltpu.VMEM((pad + ts, d), F32)],
        compiler_params=_cp("parallel"),
    )(bcv, bcv, bcv, bcv, bcv, cw, cb)


def _gateconv_bwd(bcv, dy0, cw, cb, name):
    s_len, d3 = bcv.shape
    d = d3 // 3
    ktaps = cw.shape[0]
    pad = SHORT_PAD
    ts = _tile(s_len, 256)
    nsteps = s_len // ts

    def body(gb_ref, gc_ref, v_ref, gcp_ref, vp_ref, gbn_ref, dy_ref, dyn_ref, w_ref, b_ref,
             o_ref, dw_ref, db_ref, pbuf, dqbuf, dwacc):
        gb_ref, gc_ref, v_ref, gcp_ref, vp_ref, gbn_ref, dy_ref, dyn_ref = map(
            _F32Loads, (gb_ref, gc_ref, v_ref, gcp_ref, vp_ref, gbn_ref, dy_ref, dyn_ref))
        s = pl.program_id(0)

        @pl.when(s == 0)
        def _():
            dwacc[...] = jnp.zeros_like(dwacc)
            db_ref[...] = jnp.zeros_like(db_ref)

        pbuf[0:pad, :] = jnp.where(s > 0, gcp_ref[...] * vp_ref[...], 0.0)
        pbuf[pad:pad + ts, :] = gc_ref[...] * v_ref[...]
        dq = dy_ref[...] * gb_ref[...]
        dqbuf[0:ts, :] = dq
        dqbuf[ts:ts + pad, :] = jnp.where(s < nsteps - 1, dyn_ref[...] * gbn_ref[...], 0.0)
        db_ref[...] += jnp.sum(dq, axis=0, keepdims=True)
        for r0 in range(0, ts, CHUNK):
            rows = slice(r0, r0 + CHUNK)
            q = _conv_fwd_rows(_direct(pbuf), w_ref, b_ref, ktaps, pad, r0, CHUNK)
            o_ref[rows, 0:d] = (dy_ref[rows, :] * q).astype(BF16)
            dp = _conv_bwd_rows(_direct(dqbuf), pbuf[pad + r0:pad + r0 + CHUNK, :], w_ref, dwacc, ktaps, r0, CHUNK)
            o_ref[rows, d:2 * d] = (dp * v_ref[rows, :]).astype(BF16)
            o_ref[rows, 2 * d:3 * d] = (dp * gc_ref[rows, :]).astype(BF16)

        @pl.when(s == nsteps - 1)
        def _():
            _flush_dw(dwacc, dw_ref, ktaps)

    def cur(part):
        return pl.BlockSpec((ts, d), lambda s: (s, part))

    return pl.pallas_call(
        body, name=name, grid=(nsteps,),
        in_specs=[cur(0), cur(1), cur(2),
                  _prev_spec(ts, pad, d, lambda g: 1), _prev_spec(ts, pad, d, lambda g: 2),
                  _next_spec(ts, pad, d, lambda g: 0, s_len),
                  cur(0), _next_spec(ts, pad, d, lambda g: 0, s_len),
                  _full((ktaps, d)), _full((1, d))],
        out_specs=[pl.BlockSpec((ts, d3), lambda s: (s, 0)), _full((ktaps, d)), _full((1, d))],
        out_shape=[jax.ShapeDtypeStruct((s_len, d3), BF16), jax.ShapeDtypeStruct((ktaps, d), F32),
                   jax.ShapeDtypeStruct((1, d), F32)],
        scratch_shapes=[pltpu.VMEM((pad + ts, d), F32), pltpu.VMEM((ts + pad, d), F32),
                        pltpu.VMEM((ktaps * SUBLANES, d), F32)],
        compiler_params=_cp("arbitrary"),
    )(bcv, bcv, bcv, bcv, bcv, bcv, dy0, dy0, cw, cb)


class _Cols:
    def __init__(self, ref, cols):
        self.ref, self.cols = ref, cols

    def __getitem__(self, idx):
        return self.ref[slice(None) if idx is Ellipsis else idx[0], self.cols]

    def __setitem__(self, idx, value):
        self.ref[idx[0], self.cols] = value


def _ffn_tail_fwd(u0, vg, cw, cb, wg, xres, gate, gam, bet, alpha, name):
    s_len, f = u0.shape
    d = wg.shape[-1]
    r = f // NDEV
    ktaps = cw.shape[0]
    pad = SHORT_PAD
    tm = _tile(s_len, 256)
    cbk = 1024 if f % 1024 == 0 else f

    def body(u_ref, up_ref, vg_ref, cw_ref, cb_ref, wg_ref, x_ref, g_ref, gam_ref, bet_ref,
             t_ref, y_ref, xo_ref, xh_ref, rs_ref, ubuf, w_ref, sems):
        u_ref, up_ref, vg_ref = map(_F32Loads, (u_ref, up_ref, vg_ref))
        s = pl.program_id(0)

        @pl.when(s == 0)
        def _():
            _load_rows(wg_ref, r, 0, w_ref, sems)

        ubuf[0:pad, :] = jnp.where(s > 0, up_ref[...], 0.0)
        ubuf[pad:pad + tm, :] = u_ref[...]
        y = None
        for c0 in range(0, f, cbk):
            cols = slice(c0, c0 + cbk)
            read = _direct(_Cols(ubuf, cols))
            for r0 in range(0, tm, CHUNK):
                rows = slice(r0, r0 + CHUNK)
                u = _conv_fwd_rows(read, _Cols(cw_ref, cols), _Cols(cb_ref, cols), ktaps, pad, r0, CHUNK)
                t_ref[rows, cols] = (u * _sigmoid(u) * vg_ref[rows, cols]).astype(BF16)
            p = jnp.dot(t_ref[:, cols], w_ref[cols, :], preferred_element_type=F32)
            y = p if y is None else y + p
        z = alpha * x_ref[...] + g_ref[...] * y
        mu = jnp.mean(z, axis=-1, keepdims=True)
        zc = z - mu
        var = jnp.mean(zc * zc, axis=-1, keepdims=True)
        rstd = lax.rsqrt(var + LN_EPS)
        xh = zc * rstd
        y_ref[...] = y.astype(BF16)
        xh_ref[...] = xh
        rs_ref[...] = rstd
        xo_ref[...] = xh * gam_ref[...] + bet_ref[...]

    wide = pl.BlockSpec((tm, f), lambda i: (i, 0))
    row = pl.BlockSpec((tm, d), lambda i: (i, 0))
    vec = _full((1, d))
    return pl.pallas_call(
        body, name=name, grid=(s_len // tm,),
        in_specs=[wide, _prev_spec(tm, pad, f, lambda g: 0), wide, _full((ktaps, f)), _full((1, f)), ANY_SPEC,
                  row, vec, vec, vec],
        out_specs=[wide, row, row, row, pl.BlockSpec((tm, 1), lambda i: (i, 0))],
        out_shape=[jax.ShapeDtypeStruct((s_len, f), BF16), jax.ShapeDtypeStruct((s_len, d), BF16),
                   jax.ShapeDtypeStruct((s_len, d), F32), jax.ShapeDtypeStruct((s_len, d), F32),
                   jax.ShapeDtypeStruct((s_len, 1), F32)],
        scratch_shapes=[pltpu.VMEM((pad + tm, f), F32), pltpu.VMEM((f, d), BF16), pltpu.SemaphoreType.DMA((NDEV,))],
        compiler_params=_cp("arbitrary"),
    )(u0, u0, vg, cw, cb, wg, xres, gate, gam, bet)


def _ffn_core_bwd(dy, u0, vg, cw, cb, wg_row, wg_col, xin, sc, dres, ln, alpha, name):
    s_len, f = u0.shape
    d = xin.shape[1]
    r = f // NDEV
    ktaps = cw.shape[0]
    pad = SHORT_PAD
    tm = _tile(s_len, 256)
    nsteps = s_len // tm
    cbk = 1024 if f % 1024 == 0 else f

    def body(dy_ref, dyn_ref, u_ref, up_ref, un_ref, vg_ref, vgn_ref, cw_ref, cb_ref, wgr_ref, wgc_ref,
             x_ref, sc_ref, dres_ref, xh_ref, rs_ref, gam_ref, y_ref, g_ref,
             du0_ref, dvg_ref, dw_ref, db_ref, dyo_ref, dreso_ref, lnacc_ref, acc_ref,
             ubuf, dtbuf, dubuf, dwacc, wd_ref, wup_ref, wgate_ref, sems):
        u_ref, up_ref, un_ref, vg_ref, vgn_ref = map(_F32Loads, (u_ref, up_ref, un_ref, vg_ref, vgn_ref))
        s = pl.program_id(0)
        last = s == nsteps - 1

        @pl.when(s == 0)
        def _():
            dwacc[...] = jnp.zeros_like(dwacc)
            db_ref[...] = jnp.zeros_like(db_ref)
            acc_ref[...] = jnp.zeros_like(acc_ref)
            lnacc_ref[...] = jnp.zeros_like(lnacc_ref)
            _load_rows(wgr_ref, r, 0, wd_ref, sems.at[0])
            _load_cols(wgc_ref, 0, wup_ref, sems.at[1])
            _load_cols(wgc_ref, 1, wgate_ref, sems.at[2])

        ubuf[0:pad, :] = jnp.where(s > 0, up_ref[...], 0.0)
        ubuf[pad:pad + tm, :] = u_ref[...]
        ubuf[pad + tm:pad + tm + pad, :] = un_ref[...]
        dy_cur, dy_nxt = dy_ref[...], dyn_ref[...]
        dh = None
        for c0 in range(0, f, cbk):
            cols = slice(c0, c0 + cbk)
            wd_blk = wd_ref[cols, :]
            dtbuf[0:tm, :] = lax.dot_general(dy_cur, wd_blk, NT, preferred_element_type=F32)
            dtbuf[tm:tm + pad, :] = jnp.where(
                last, 0.0, lax.dot_general(dy_nxt, wd_blk, NT, preferred_element_type=F32))
            read_u = _direct(_Cols(ubuf, cols))
            for r0 in range(0, tm + pad, CHUNK):
                u = _conv_fwd_rows(read_u, _Cols(cw_ref, cols), _Cols(cb_ref, cols), ktaps, pad, r0, CHUNK)
                sg = _sigmoid(u)
                dtr = dtbuf[r0:r0 + CHUNK, :]
                if r0 < tm:
                    vgr = vg_ref[r0:r0 + CHUNK, cols]
                    dvg_ref[r0:r0 + CHUNK, cols] = (dtr * u * sg).astype(BF16)
                else:
                    vgr = vgn_ref[r0 - tm:r0 - tm + CHUNK, cols]
                dubuf[r0:r0 + CHUNK, :] = dtr * vgr * (sg * (1.0 + u * (1.0 - sg)))
            db_ref[:, cols] += jnp.sum(dubuf[0:tm, :], axis=0, keepdims=True)
            for r0 in range(0, tm, CHUNK):
                du0 = _conv_bwd_rows(_direct(dubuf), u_ref[r0:r0 + CHUNK, cols], _Cols(cw_ref, cols),
                                     _Cols(dwacc, cols), ktaps, r0, CHUNK)
                du0_ref[r0:r0 + CHUNK, cols] = du0.astype(BF16)
            p = (lax.dot_general(du0_ref[:, cols], wup_ref[:, cols], NT, preferred_element_type=F32)
                 + lax.dot_general(dvg_ref[:, cols], wgate_ref[:, cols], NT, preferred_element_type=F32))
            dh = p if dh is None else dh + p
        dx = dh * (1.0 + sc_ref[...]) + dres_ref[...]
        _ln_bwd_rows(dx, (xh_ref, rs_ref, gam_ref, y_ref, g_ref), (dyo_ref, dreso_ref, lnacc_ref), alpha)
        acc_ref[0:1, :] += jnp.sum(dh * x_ref[...], axis=0, keepdims=True)
        acc_ref[1:2, :] += jnp.sum(dh, axis=0, keepdims=True)

        @pl.when(last)
        def _():
            _flush_dw(dwacc, dw_ref, ktaps)

    wide = pl.BlockSpec((tm, f), lambda i: (i, 0))
    row = pl.BlockSpec((tm, d), lambda i: (i, 0))
    ln_out_specs, ln_out_shape = _ln_out_specs(s_len, tm, d)
    return pl.pallas_call(
        body, name=name, grid=(nsteps,),
        in_specs=[row, _next_spec(tm, pad, d, lambda g: 0, s_len),
                  wide, _prev_spec(tm, pad, f, lambda g: 0), _next_spec(tm, pad, f, lambda g: 0, s_len),
                  wide, _next_spec(tm, pad, f, lambda g: 0, s_len),
                  _full((ktaps, f)), _full((1, f)), ANY_SPEC, ANY_SPEC, row, _full((1, d)), row]
        + _ln_in_specs(tm, d),
        out_specs=[wide, wide, _full((ktaps, f)), _full((1, f))] + ln_out_specs + [_full((SUBLANES, d))],
        out_shape=[jax.ShapeDtypeStruct((s_len, f), BF16), jax.ShapeDtypeStruct((s_len, f), BF16),
                   jax.ShapeDtypeStruct((ktaps, f), F32), jax.ShapeDtypeStruct((1, f), F32)]
        + ln_out_shape + [jax.ShapeDtypeStruct((SUBLANES, d), F32)],
        scratch_shapes=[pltpu.VMEM((pad + tm + pad, f), F32), pltpu.VMEM((tm + pad, cbk), F32),
                        pltpu.VMEM((tm + pad, cbk), F32), pltpu.VMEM((ktaps * SUBLANES, f), F32),
                        pltpu.VMEM((f, d), BF16), pltpu.VMEM((d, f), BF16), pltpu.VMEM((d, f), BF16),
                        pltpu.SemaphoreType.DMA((3, NDEV))],
        compiler_params=_cp("arbitrary"),
    )(dy, dy, u0, u0, u0, vg, vg, cw, cb, wg_row, wg_col, xin, sc, dres, *ln)


def _b_mid_fwd(ub, cw, cb, lng, lnb, name):
    s_len, d2 = ub.shape
    d = d2 // 2
    ktaps = cw.shape[0]
    pad = LONG_PAD
    ts = _tile(s_len, 256)

    def body(a_ref, g_ref, ap_ref, gp_ref, w_ref, b_ref, lng_ref, lnb_ref, a2_ref, a4_ref, abuf):
        a_ref, g_ref, ap_ref, gp_ref = map(_F32Loads, (a_ref, g_ref, ap_ref, gp_ref))
        s = pl.program_id(0)
        abuf[0, 0:pad, :] = jnp.where(s > 0, ap_ref[...] * _sigmoid(gp_ref[...]), 0.0)
        abuf[0, pad:pad + ts, :] = a_ref[...] * _sigmoid(g_ref[...])
        _make_shifts(abuf, pad + ts)
        for r0 in range(0, ts, CHUNK):
            a2_ref[r0:r0 + CHUNK, :] = _conv_fwd_rows(_shifted(abuf), w_ref, b_ref, ktaps, pad, r0, CHUNK)
        a2 = a2_ref[...]
        mu = jnp.mean(a2, axis=-1, keepdims=True)
        ac = a2 - mu
        var = jnp.mean(ac * ac, axis=-1, keepdims=True)
        a3 = ac * lax.rsqrt(var + LN_EPS) * lng_ref[...] + lnb_ref[...]
        a4_ref[...] = (a3 * _sigmoid(a3)).astype(BF16)

    def cur(part):
        return pl.BlockSpec((ts, d), lambda s: (s, part))

    vec = _full((1, d))
    return pl.pallas_call(
        body, name=name, grid=(s_len // ts,),
        in_specs=[cur(0), cur(1), _prev_spec(ts, pad, d, lambda g: 0), _prev_spec(ts, pad, d, lambda g: 1),
                  _full((ktaps, d)), vec, vec, vec],
        out_specs=[cur(0), cur(0)],
        out_shape=[jax.ShapeDtypeStruct((s_len, d), F32), jax.ShapeDtypeStruct((s_len, d), BF16)],
        scratch_shapes=[pltpu.VMEM((SUBLANES, pad + ts, d), F32)],
        compiler_params=_cp("parallel"),
    )(ub, ub, ub, ub, cw, cb, lng, lnb)


def _b_mid_bwd(ub, a2, da4, cw, lng, lnb, name):
    s_len, d2 = ub.shape
    d = d2 // 2
    ktaps = cw.shape[0]
    pad = LONG_PAD
    ts = _tile(s_len, 256)
    nsteps = s_len // ts

    def body(a_ref, g_ref, a2_ref, a2n_ref, da4_ref, da4n_ref, w_ref, lng_ref, lnb_ref,
             du_ref, dw_ref, db_ref, dlng_ref, dlnb_ref, dbias_ref, dabuf, dwacc):
        a_ref, g_ref, da4_ref, da4n_ref = map(_F32Loads, (a_ref, g_ref, da4_ref, da4n_ref))
        s = pl.program_id(0)
        last = s == nsteps - 1

        @pl.when(s == 0)
        def _():
            dwacc[...] = jnp.zeros_like(dwacc)
            for ref in (db_ref, dlng_ref, dlnb_ref, dbias_ref):
                ref[...] = jnp.zeros_like(ref)

        def ln_silu_bwd(a2_t, da4_t):
            mu = jnp.mean(a2_t, axis=-1, keepdims=True)
            ac = a2_t - mu
            var = jnp.mean(ac * ac, axis=-1, keepdims=True)
            rstd = lax.rsqrt(var + LN_EPS)
            ah = ac * rstd
            a3 = ah * lng_ref[...] + lnb_ref[...]
            sg = _sigmoid(a3)
            da3 = da4_t * (sg * (1.0 + a3 * (1.0 - sg)))
            dah = da3 * lng_ref[...]
            m1 = jnp.mean(dah, axis=-1, keepdims=True)
            m2 = jnp.mean(dah * ah, axis=-1, keepdims=True)
            return rstd * (dah - m1 - ah * m2), da3, ah

        da2, da3, ah = ln_silu_bwd(a2_ref[...], da4_ref[...])
        dabuf[0, 0:ts, :] = da2
        dlng_ref[...] += jnp.sum(da3 * ah, axis=0, keepdims=True)
        dlnb_ref[...] += jnp.sum(da3, axis=0, keepdims=True)
        db_ref[...] += jnp.sum(da2, axis=0, keepdims=True)
        da2n, _, _ = ln_silu_bwd(a2n_ref[...], jnp.where(last, 0.0, da4n_ref[...]))
        dabuf[0, ts:ts + pad, :] = da2n
        _make_shifts(dabuf, ts + pad)
        for r0 in range(0, ts, CHUNK):
            rows = slice(r0, r0 + CHUNK)
            a_r, g_r = a_ref[rows, :], g_ref[rows, :]
            sg = _sigmoid(g_r)
            da1 = _conv_bwd_rows(_shifted(dabuf), a_r * sg, w_ref, dwacc, ktaps, r0, CHUNK)
            da = da1 * sg
            dg = da1 * a_r * sg * (1.0 - sg)
            du_ref[rows, 0:d] = da.astype(BF16)
            du_ref[rows, d:2 * d] = dg.astype(BF16)
            dbias_ref[:, 0:d] += jnp.sum(da, axis=0, keepdims=True)
            dbias_ref[:, d:2 * d] += jnp.sum(dg, axis=0, keepdims=True)

        @pl.when(last)
        def _():
            _flush_dw(dwacc, dw_ref, ktaps)

    def cur(part):
        return pl.BlockSpec((ts, d), lambda s: (s, part))

    vec = _full((1, d))
    nxt = _next_spec(ts, pad, d, lambda g: 0, s_len)
    return pl.pallas_call(
        body, name=name, grid=(nsteps,),
        in_specs=[cur(0), cur(1), cur(0), nxt, cur(0), nxt, _full((ktaps, d)), vec, vec],
        out_specs=[pl.BlockSpec((ts, d2), lambda s: (s, 0)), _full((ktaps, d)), vec, vec, vec, _full((1, d2))],
        out_shape=[jax.ShapeDtypeStruct((s_len, d2), BF16), jax.ShapeDtypeStruct((ktaps, d), F32),
                   jax.ShapeDtypeStruct((1, d), F32), jax.ShapeDtypeStruct((1, d), F32),
                   jax.ShapeDtypeStruct((1, d), F32), jax.ShapeDtypeStruct((1, d2), F32)],
        scratch_shapes=[pltpu.VMEM((SUBLANES, ts + pad, d), F32), pltpu.VMEM((ktaps * SUBLANES, d), F32)],
        compiler_params=_cp("arbitrary"),
    )(ub, ub, a2, a2, da4, da4, cw, lng, lnb)


def _loss_head(xo, tgt, ln, alpha, name):
    s_len, d = xo.shape
    tm = _tile(s_len, 512)

    def body(x_ref, t_ref, xh_ref, rs_ref, gam_ref, y_ref, g_ref, dy_ref, dres_ref, acc_ref, l_ref):
        @pl.when(pl.program_id(0) == 0)
        def _():
            l_ref[...] = jnp.zeros_like(l_ref)
            acc_ref[...] = jnp.zeros_like(acc_ref)

        e = x_ref[...] - t_ref[...]
        per_row = jnp.sum(e * e, axis=-1, keepdims=True) * (1.0 / d)
        l_ref[...] += 0.5 * jnp.sum(per_row, axis=0, keepdims=True)
        _ln_bwd_rows(e * (1.0 / d), (xh_ref, rs_ref, gam_ref, y_ref, g_ref), (dy_ref, dres_ref, acc_ref), alpha)

    row = pl.BlockSpec((tm, d), lambda i: (i, 0))
    ln_out_specs, ln_out_shape = _ln_out_specs(s_len, tm, d)
    return pl.pallas_call(
        body, name=name, grid=(s_len // tm,),
        in_specs=[row, row] + _ln_in_specs(tm, d), out_specs=ln_out_specs + [_full((1, LANES))],
        out_shape=ln_out_shape + [jax.ShapeDtypeStruct((1, LANES), F32)],
        compiler_params=_cp("arbitrary"),
    )(xo, tgt, *ln)


def _ada_fwd(c_all, ada_w, ada_b_loc, name):
    depth, d, n = ada_w.shape

    def body(c_ref, w_ref, b_ref, o_ref):
        c = c_ref[...]
        act = c * _sigmoid(c)
        o_ref[...] = jnp.dot(act, w_ref[...], preferred_element_type=F32,
                             precision=lax.Precision.HIGHEST) + b_ref[...]

    return pl.pallas_call(
        body, name=name, grid=(depth,),
        in_specs=[_full((NDEV, d)), pl.BlockSpec((None, d, n), lambda i: (i, 0, 0)),
                  pl.BlockSpec((None, 1, n), lambda i: (i, 0, 0))],
        out_specs=pl.BlockSpec((None, NDEV, n), lambda i: (i, 0, 0)),
        out_shape=jax.ShapeDtypeStruct((depth, NDEV, n), F32),
        compiler_params=_cp("parallel"),
    )(c_all, ada_w, ada_b_loc.reshape(depth, 1, n))


def _ada_bwd(c_all_t, dmod_cols, name):
    depth, _, n = dmod_cols.shape
    d = c_all_t.shape[0]

    def body(ct_ref, dm_ref, o_ref):
        ct = ct_ref[...]
        act = ct * _sigmoid(ct)
        acc = None
        for b in range(NDEV):
            term = act[:, b:b + 1] * dm_ref[b:b + 1, :]
            acc = term if acc is None else acc + term
        o_ref[...] = acc

    return pl.pallas_call(
        body, name=name, grid=(depth,),
        in_specs=[_full((d, NDEV)), pl.BlockSpec((None, NDEV, n), lambda i: (i, 0, 0))],
        out_specs=pl.BlockSpec((None, d, n), lambda i: (i, 0, 0)),
        out_shape=jax.ShapeDtypeStruct((depth, d, n), F32),
        compiler_params=_cp("parallel"),
    )(c_all_t, dmod_cols)


def _sum_parts(parts, name):
    _, rows, lanes = parts.shape

    def body(p_ref, o_ref):
        acc = p_ref[0]
        for k in range(1, NDEV):
            acc = acc + p_ref[k]
        o_ref[...] = acc

    return pl.pallas_call(
        body, name=name, in_specs=[_full(parts.shape)], out_specs=_full((rows, lanes)), grid=(1,),
        out_shape=jax.ShapeDtypeStruct((rows, lanes), F32), compiler_params=_cp("arbitrary"),
    )(parts)


def _adamw(w, glist, m, v, name):
    nl, rows, cols = w.shape
    tr = _tile(rows, 256, 2 * SUBLANES)

    def body(w_ref, *rest):
        g_refs = rest[:nl]
        m_ref, v_ref, go_ref, d_ref, mo_ref, vo_ref = rest[nl:]
        g = None
        for layer, g_ref in enumerate(g_refs):
            part = g_ref[0].astype(F32)
            for p in range(1, g_ref.shape[0]):
                part = part + g_ref[p].astype(F32)
            g = part if g is None else jnp.where(pl.program_id(0) == layer, part, g)
        go_ref[...] = g
        d_ref[...], mo_ref[...], vo_ref[...] = _adam_step(w_ref[...], g, m_ref[...], v_ref[...])

    blk = pl.BlockSpec((None, tr, cols), lambda l, i: (l, i, 0))
    g_specs = [pl.BlockSpec((g.shape[0], tr, cols), lambda l, i: (0, i, 0)) for g in glist]
    return pl.pallas_call(
        body, name=name, grid=(nl, rows // tr),
        in_specs=[blk] + g_specs + [blk, blk],
        out_specs=[blk] * 4, out_shape=[jax.ShapeDtypeStruct((nl, rows, cols), F32)] * 4,
        compiler_params=_cp("parallel", "parallel"),
    )(w, *glist, m, v)


def _adam_step(w, g, m, v):
    m1 = ADAM_B1 * m + (1.0 - ADAM_B1) * g
    v1 = ADAM_B2 * v + (1.0 - ADAM_B2) * (g * g)
    m_hat = m1 / (1.0 - ADAM_B1 ** ADAM_STEP)
    v_hat = v1 / (1.0 - ADAM_B2 ** ADAM_STEP)
    return -ADAM_LR * (m_hat / (jnp.sqrt(v_hat) + ADAM_EPS) + ADAM_WD * w), m1, v1


def _adamw_small(ws, gs, ms, vs, name):
    n = len(ws)

    def body(*refs):
        ins, outs = refs[:4 * n], refs[4 * n:]
        for i in range(n):
            w_ref, g_ref, m_ref, v_ref = ins[i], ins[n + i], ins[2 * n + i], ins[3 * n + i]
            delta, m1, v1 = _adam_step(w_ref[...], g_ref[...], m_ref[...], v_ref[...])
            outs[3 * i][...] = delta
            outs[3 * i + 1][...] = m1
            outs[3 * i + 2][...] = v1

    operands = list(ws) + list(gs) + list(ms) + list(vs)
    out_shape = [jax.ShapeDtypeStruct(w.shape, F32) for w in ws for _ in range(3)]
    return pl.pallas_call(
        body, name=name, grid=(1,), in_specs=[_full(a.shape) for a in operands],
        out_specs=[_full(s.shape) for s in out_shape], out_shape=out_shape,
        compiler_params=_cp("arbitrary"),
    )(*operands)


def _pack(pieces):
    flat = jnp.concatenate([p.reshape(-1) for p in pieces])
    unit = SUBLANES * LANES
    padded = -(-flat.shape[0] // unit) * unit
    return jnp.pad(flat, (0, padded - flat.shape[0])).reshape(padded // LANES, LANES)


def _unpack(packed, shapes, lead=()):
    flat = packed.reshape(lead + (-1,))
    out, off = [], 0
    for s in shapes:
        size = 1
        for dim in s:
            size *= dim
        out.append(flat[..., off:off + size].reshape(lead + tuple(s)))
        off += size
    return out


def _pad_last(a, n):
    return jnp.pad(a, [(0, 0)] * (a.ndim - 1) + [(0, n - a.shape[-1])])


def kernel(x, c, ada_w, ada_b, ln_tok_g, ln_tok_b, ln_ch_g, ln_ch_b, a_w_in, a_conv_w, a_conv_b, a_w_out, b_w_pw1, b_b_pw1, b_conv_w, b_conv_b, b_ln_g, b_ln_b, b_w_pw2, b_b_pw2, f_w_up, f_conv_w, f_conv_b, f_w_gate, f_w_down, loss_target, m_ada_w, m_ada_b, m_ln_tok_g, m_ln_tok_b, m_ln_ch_g, m_ln_ch_b, m_a_w_in, m_a_conv_w, m_a_conv_b, m_a_w_out, m_b_w_pw1, m_b_b_pw1, m_b_conv_w, m_b_conv_b, m_b_ln_g, m_b_ln_b, m_b_w_pw2, m_b_b_pw2, m_f_w_up, m_f_conv_w, m_f_conv_b, m_f_w_gate, m_f_w_down, v_ada_w, v_ada_b, v_ln_tok_g, v_ln_tok_b, v_ln_ch_g, v_ln_ch_b, v_a_w_in, v_a_conv_w, v_a_conv_b, v_a_w_out, v_b_w_pw1, v_b_b_pw1, v_b_conv_w, v_b_conv_b, v_b_ln_g, v_b_ln_b, v_b_w_pw2, v_b_b_pw2, v_f_w_up, v_f_conv_w, v_f_conv_b, v_f_w_gate, v_f_w_down):
    weights = dict(ada_w=ada_w, ada_b=ada_b, ln_tok_g=ln_tok_g, ln_tok_b=ln_tok_b, ln_ch_g=ln_ch_g, ln_ch_b=ln_ch_b, a_w_in=a_w_in, a_conv_w=a_conv_w, a_conv_b=a_conv_b, a_w_out=a_w_out, b_w_pw1=b_w_pw1, b_b_pw1=b_b_pw1, b_conv_w=b_conv_w, b_conv_b=b_conv_b, b_ln_g=b_ln_g, b_ln_b=b_ln_b, b_w_pw2=b_w_pw2, b_b_pw2=b_b_pw2, f_w_up=f_w_up, f_conv_w=f_conv_w, f_conv_b=f_conv_b, f_w_gate=f_w_gate, f_w_down=f_w_down)
    mom_m = dict(ada_w=m_ada_w, ada_b=m_ada_b, ln_tok_g=m_ln_tok_g, ln_tok_b=m_ln_tok_b, ln_ch_g=m_ln_ch_g, ln_ch_b=m_ln_ch_b, a_w_in=m_a_w_in, a_conv_w=m_a_conv_w, a_conv_b=m_a_conv_b, a_w_out=m_a_w_out, b_w_pw1=m_b_w_pw1, b_b_pw1=m_b_b_pw1, b_conv_w=m_b_conv_w, b_conv_b=m_b_conv_b, b_ln_g=m_b_ln_g, b_ln_b=m_b_ln_b, b_w_pw2=m_b_w_pw2, b_b_pw2=m_b_b_pw2, f_w_up=m_f_w_up, f_conv_w=m_f_conv_w, f_conv_b=m_f_conv_b, f_w_gate=m_f_w_gate, f_w_down=m_f_w_down)
    mom_v = dict(ada_w=v_ada_w, ada_b=v_ada_b, ln_tok_g=v_ln_tok_g, ln_tok_b=v_ln_tok_b, ln_ch_g=v_ln_ch_g, ln_ch_b=v_ln_ch_b, a_w_in=v_a_w_in, a_conv_w=v_a_conv_w, a_conv_b=v_a_conv_b, a_w_out=v_a_w_out, b_w_pw1=v_b_w_pw1, b_b_pw1=v_b_b_pw1, b_conv_w=v_b_conv_w, b_conv_b=v_b_conv_b, b_ln_g=v_b_ln_g, b_ln_b=v_b_ln_b, b_w_pw2=v_b_w_pw2, b_b_pw2=v_b_b_pw2, f_w_up=v_f_w_up, f_conv_w=v_f_conv_w, f_conv_b=v_f_conv_b, f_w_gate=v_f_w_gate, f_w_down=v_f_w_down)
    names = list(weights)

    depth, d, n_ada = ada_w.shape
    assert depth == 2 and a_w_in.shape[0] == 1 and b_w_pw1.shape[0] == 1
    s_len = x.shape[1]
    f_loc = f_w_up.shape[-1]
    f_pad = -(-f_loc // LANES) * LANES
    f_all = NDEV * f_pad
    d_loc = d // NDEV
    ka, kb, kf = a_conv_w.shape[1], b_conv_w.shape[1], f_conv_w.shape[1]
    alpha = (2.0 * depth) ** 0.25
    assert a_w_in.shape[-1] == f_pad and f_pad % d_loc == 0
    me = 4 * lax.axis_index("x") + 2 * lax.axis_index("y") + lax.axis_index("c")

    small_shapes = [(d,), (ka, d_loc), (2 * d_loc,), (kb, d_loc), (d_loc,), (d_loc,), (d_loc,), (d_loc,),
                    (depth, kf, f_pad)]
    small_loc = _pack([c[0], a_conv_w[0], b_b_pw1[0], b_conv_w[0], b_conv_b[0], b_ln_g[0], b_ln_b[0],
                       b_b_pw2[0], _pad_last(f_conv_w, f_pad)])
    g_small, g_in, _ = _gather_two_level([small_loc, a_w_in.astype(BF16)], small_loc, "gather_first")

    (c_all, acw_g, bb1_g, bcw_g, bcb_g, blg_g, blb_g, bb2_g, fcw_g) = _unpack(g_small, small_shapes, (NDEV,))
    a_cw = acw_g.transpose(1, 0, 2).reshape(ka, d)
    b_cw = bcw_g.transpose(1, 0, 2).reshape(kb, d)
    b_b1 = bb1_g.reshape(1, 2 * d)
    b_cb, b_lg, b_lb, b_b2 = (t.reshape(1, d) for t in (bcb_g, blg_g, blb_g, bb2_g))
    f_cw = fcw_g.transpose(1, 2, 0, 3).reshape(depth, kf, f_all)
    f_cb = _pad_last(f_conv_b.reshape(depth, NDEV, f_loc), f_pad).reshape(depth, 1, f_all)

    ada_b_loc = lax.dynamic_slice(ada_b, (0, me * n_ada), (depth, n_ada))
    mod_part = _ada_fwd(c_all, ada_w, ada_b_loc, "ada_fwd")
    mod_g, mod_done = _exchange([mod_part.reshape(depth * NDEV, n_ada)], "gather", "gather_mod")
    mod_all = mod_g.reshape(NDEV, depth, NDEV, n_ada).transpose(1, 2, 0, 3).reshape(depth, NDEV, 6 * d)
    mod = lax.dynamic_slice(mod_all, (0, me, 0), (depth, 1, 6 * d))[:, 0]

    gather_out = _exchange_start([_after(a_w_out[0], mod_done).astype(BF16)], "gather_chips", "gather_out_start")
    up_pad = _pad_last(_after(f_w_up, gather_out[-1]), f_pad).astype(BF16)
    gate_pad = _pad_last(f_w_gate, f_pad).astype(BF16)
    down_pad = jnp.pad(f_w_down, ((0, 0), (0, f_pad - f_loc), (0, 0))).astype(BF16)
    col_f = [jnp.stack([up_pad[i], gate_pad[i]]) for i in range(depth)]
    row_b = jnp.concatenate([down_pad[1], b_w_pw2[0].astype(BF16)], axis=0)
    ridx_pw2 = f_pad // d_loc
    gather_f0 = _exchange_start([col_f[0], down_pad[0]], "gather_chips", "gather_f0_start")

    def mod_rows(i):
        return [mod[i:i + 1, j * d:(j + 1) * d] for j in range(6)]

    zeros_d = jnp.zeros((1, d), F32)
    zeros_f = jnp.zeros((1, f_all), F32)
    x0 = x[0]

    sh_t0, sc_t0, g_t0, sh_c0, sc_c0, g_c0 = mod_rows(0)
    sh_t1, sc_t1, g_t1, sh_c1, sc_c1, g_c1 = mod_rows(1)

    sc_t0 = _after(sc_t0, gather_f0[-1])
    bcv, = _mm_fwd(x0, sc_t0, sh_t0, jnp.zeros((1, 3 * d), F32), g_in, (0,), "a_in_fwd")
    y0 = _gateconv_fwd(bcv, a_cw, a_conv_b, "a_conv_fwd")
    g_out, landed = _exchange_wait(gather_out, y0, "gather_chips", "gather_out_wait")
    g_out, _ = _exchange_wait(_exchange_start([g_out], "forward", "gather_out_fwd_start"), landed, "forward",
                              "gather_out_fwd_wait")
    y_a, x1, xh1, rs1 = _mm_ln(y0, g_out, d_loc, 0, x0, g_t0, ln_tok_g[0:1], ln_tok_b[0:1], zeros_d,
                               alpha, "a_out_ln_fwd")

    def ffn_fwd(xin, sc, sh, gate, gam, bet, g_colf, g_rowf, layer, tag):
        u0, vg = _mm_fwd(xin, sc, sh, zeros_f, g_colf, (0, 1), "f_upgate_fwd" + tag)
        t, y, xo, xh, rs = _ffn_tail_fwd(u0, vg, f_cw[layer], f_cb[layer], g_rowf, xin, gate, gam, bet, alpha,
                                         "f_tail_fwd" + tag)
        return u0, vg, t, y, xo, xh, rs

    g_colf0, g_rowf0, landed = _exchange_wait(gather_f0, x1, "gather_chips", "gather_f0_wait")
    g_colf0, g_rowf0, landed = _exchange_wait(
        _exchange_start([g_colf0, g_rowf0], "forward", "gather_f0_fwd_start"), landed, "forward", "gather_f0_fwd_wait")
    gather_1 = _exchange_start([_after(b_w_pw1, landed).astype(BF16), col_f[1], row_b], "gather_chips",
                               "gather_1_start")
    sc_c0 = _after(sc_c0, gather_1[-1])
    u0_0, vg_0, t_0, y_f0, x2, xh2, rs2 = ffn_fwd(x1, sc_c0, sh_c0, g_c0, ln_ch_g[0:1], ln_ch_b[0:1],
                                                  g_colf0, g_rowf0, 0, "0")

    *lands_1, landed = _exchange_wait(gather_1, x2, "gather_chips", "gather_1_wait")
    g_pw1, g_colf1, g_rowb, _ = _exchange_wait(_exchange_start(lands_1, "forward", "gather_1_fwd_start"), landed,
                                                 "forward", "gather_1_fwd_wait")
    ub, = _mm_fwd(x2, sc_t1, sh_t1, b_b1, g_pw1, (0,), "b_pw1_fwd")
    a2, a4 = _b_mid_fwd(ub, b_cw, b_cb, b_lg, b_lb, "b_mid_fwd")
    y_b, x3, xh3, rs3 = _mm_ln(a4, g_rowb, d_loc, ridx_pw2, x2, g_t1, ln_tok_g[1:2], ln_tok_b[1:2], b_b2,
                               alpha, "b_pw2_ln_fwd")
    u0_1, vg_1, t_1, y_f1, x4, xh4, rs4 = ffn_fwd(x3, sc_c1, sh_c1, g_c1, ln_ch_g[1:2], ln_ch_b[1:2],
                                                  g_colf1, g_rowb, 1, "1")

    ln_f1 = (xh4, rs4, ln_ch_g[1:2], y_f1, g_c1)
    ln_b = (xh3, rs3, ln_tok_g[1:2], y_b, g_t1)
    ln_f0 = (xh2, rs2, ln_ch_g[0:1], y_f0, g_c0)
    ln_a = (xh1, rs1, ln_tok_g[0:1], y_a, g_t0)
    dy, dres, accf1, loss_part = _loss_head(x4, loss_target[0], ln_f1, alpha, "loss_head")

    def ffn_bwd(dy, dres, xin, sc, sh, u0, vg, t, g_colf, g_rowf, ln_below, layer, tag):
        dw_down = _mm_tn_row(t, dy, f_pad, f_loc, "f_down_dw" + tag)
        scatter_down = _exchange_start([dw_down], "scatter", "scatter_d%s_start" % tag)
        du0, dvg, dcw, dcb, dy_below, dres_below, acc_below, acc2 = _ffn_core_bwd(
            dy, u0, vg, f_cw[layer], f_cb[layer], g_rowf, g_colf, xin, _after(sc, scatter_down[-1]), dres,
            ln_below, alpha, "f_core_bwd" + tag)
        dw_up = _mm_tn_col_t(xin, sc, sh, du0, f_loc, "f_up_dw" + tag)
        dw_gate = _mm_tn_col_t(xin, sc, sh, dvg, f_loc, "f_gate_dw" + tag)
        scatter = _exchange_start([dw_up, dw_gate], "scatter", "scatter_f%s_start" % tag)
        return dy_below, dres_below, acc_below, acc2, (scatter, scatter_down), dcw, dcb

    dy, dres, accb, acc2f1, (scatter_f1, scatter_d1), dfcw1, dfcb1 = ffn_bwd(
        dy, dres, x3, sc_c1, sh_c1, u0_1, vg_1, t_1, g_colf1, g_rowb, ln_b, 1, "1")

    da4 = _mm_nt_row(dy, g_rowb, d_loc, ridx_pw2, "b_pw2_dx")
    dw_pw2 = _mm_tn_row(a4, dy, d_loc, d_loc, "b_pw2_dw")
    du, dbcw, dbcb, dblg, dblb, dbb1 = _b_mid_bwd(ub, a2, da4, b_cw, _after(b_lg, scatter_f1[-1]), b_lb, "b_mid_bwd")
    dw_pw1 = _mm_tn_col(x2, sc_t1, sh_t1, du, "b_pw1_dw")
    scatter_b = _exchange_start([dw_pw1, dw_pw2], "scatter", "scatter_b_start")
    dy, dres, accf0, acc2b = _mm_nt_mod([du], g_pw1, (0,), x2, _after(sc_t1, scatter_b[-1]), dres, "b_pw1_dx",
                                        ln=ln_f0, alpha=alpha)

    dy, dres, acca, acc2f0, (scatter_f0, scatter_d0), dfcw0, dfcb0 = ffn_bwd(
        dy, dres, x1, sc_c0, sh_c0, u0_0, vg_0, t_0, g_colf0, g_rowf0, ln_a, 0, "0")

    dy0 = _mm_nt_row(dy, g_out, d_loc, 0, "a_out_dx")
    dbcv, dacw, dacb = _gateconv_bwd(bcv, dy0, a_cw, _after(a_conv_b, scatter_f0[-1]), "a_conv_bwd")
    dx0, acc2a = _mm_nt_mod([dbcv], g_in, (0,), x0, sc_t0, dres, "a_in_dx")

    def dmod_row(acc2_t, acc_t, acc2_c, acc_c):
        return jnp.concatenate([acc2_t[1], acc2_t[0], acc_t[2], acc2_c[1], acc2_c[0], acc_c[2]])

    dmod = jnp.stack([dmod_row(acc2a, acca, acc2f0, accf0), dmod_row(acc2b, accb, acc2f1, accf1)])

    def unpad_f(a):
        return a.reshape(a.shape[:-1] + (NDEV, f_pad))[..., :f_loc].reshape(a.shape[:-1] + (NDEV * f_loc,))

    small_grads = [
        dmod,
        jnp.stack([acca[0], accb[0]]), jnp.stack([acca[1], accb[1]]),
        jnp.stack([accf0[0], accf1[0]]), jnp.stack([accf0[1], accf1[1]]),
        dacb,
        unpad_f(jnp.concatenate([dfcb0, dfcb1], axis=0)),
        dacw, dbb1, dbcw, dbcb, dblg, dblb, accb[3:4],
        jnp.stack([dfcw0, dfcw1]),
        loss_part[0:1, 0:1],
    ]
    small_grad_shapes = [tuple(g.shape) for g in small_grads]
    gather_small = _exchange_start([_pack(small_grads)], "gather", "gather_small_start")

    dw_in = _mm_tn_col(x0, _after(sc_t0, gather_small[-1]), sh_t0, dbcv, "a_in_dw")
    dw_out = _mm_tn_row(y0, dy, d_loc, d_loc, "a_out_dw")
    scatter_a = _exchange_start([dw_in, dw_out], "scatter", "scatter_a_start")

    grads, deltas, new_m, new_v = {}, {}, {}, {}

    def adamw(k, glist, transposed=False):
        def view(a):
            a = jnp.swapaxes(a, 1, 2) if transposed else a
            return a.reshape(len(glist), -1, a.shape[-1])

        w = view(weights[k])
        outs = _adamw(w, [g.reshape(g.shape[0], -1, w.shape[-1]) for g in glist],
                      view(mom_m[k]), view(mom_v[k]), "adamw_" + k)
        if transposed:
            outs = [jnp.swapaxes(o, 1, 2) for o in outs]
        grads[k], deltas[k], new_m[k], new_v[k] = (o.reshape(weights[k].shape) for o in outs)

    r_up1, r_gate1, _ = _exchange_wait(scatter_f1, scatter_a[-1], "scatter", "scatter_f1_wait")
    r_down1, _ = _exchange_wait(scatter_d1, r_gate1, "scatter", "scatter_d1_wait")
    r_pw1, r_pw2, _ = _exchange_wait(scatter_b, r_down1, "scatter", "scatter_b_wait")
    adamw("b_w_pw1", [r_pw1])
    adamw("b_w_pw2", [r_pw2])
    r_down0, _ = _exchange_wait(scatter_d0, deltas["b_w_pw2"], "scatter", "scatter_d0_wait")
    adamw("f_w_down", [r_down0, r_down1])
    r_up0, r_gate0, _ = _exchange_wait(scatter_f0, deltas["f_w_down"], "scatter", "scatter_f0_wait")
    adamw("f_w_up", [r_up0, r_up1], transposed=True)
    adamw("f_w_gate", [r_gate0, r_gate1], transposed=True)

    sg_all, _ = _exchange_wait(gather_small, deltas["f_w_gate"], "gather", "gather_small_wait")
    sg_sum = _sum_parts(sg_all, "sum_small_grads")
    (g_ada_b, g_ltg, g_ltb, g_lcg, g_lcb, g_acb, g_fcb, g_acw, g_bb1, g_bcw, g_bcb, g_blg, g_blb, g_bb2,
     g_fcw, loss_all) = _unpack(sg_sum, small_grad_shapes)
    loss = loss_all[0, 0]

    def my_cols(a, width):
        return lax.dynamic_slice_in_dim(a, me * width, width, axis=a.ndim - 1)

    g_fcw_loc = my_cols(g_fcw, f_pad)[..., :f_loc]
    small = dict(
        ada_b=g_ada_b, ln_tok_g=g_ltg, ln_tok_b=g_ltb, ln_ch_g=g_lcg, ln_ch_b=g_lcb, a_conv_b=g_acb, f_conv_b=g_fcb,
        a_conv_w=my_cols(g_acw, d_loc)[None], b_b_pw1=my_cols(g_bb1, 2 * d_loc), b_conv_w=my_cols(g_bcw, d_loc)[None],
        b_conv_b=my_cols(g_bcb, d_loc), b_ln_g=my_cols(g_blg, d_loc), b_ln_b=my_cols(g_blb, d_loc),
        b_b_pw2=my_cols(g_bb2, d_loc), f_conv_w=g_fcw_loc)

    dmod_all = sg_all.reshape(NDEV, -1)[:, :depth * 6 * d].reshape(NDEV, depth, 6 * d)
    dmod_cols = my_cols(dmod_all, n_ada).transpose(1, 0, 2)
    g_ada_w = _ada_bwd(c_all.T, dmod_cols, "ada_bwd")

    adamw("ada_w", [g_ada_w[0:1], g_ada_w[1:2]])

    def rows_cols(a):
        return a.reshape(-1, a.shape[-1])

    small_keys = list(small)
    small_outs = _adamw_small([rows_cols(weights[k]) for k in small_keys], [rows_cols(small[k]) for k in small_keys],
                              [rows_cols(mom_m[k]) for k in small_keys], [rows_cols(mom_v[k]) for k in small_keys],
                              "adamw_small")
    for i, k in enumerate(small_keys):
        grads[k] = small[k].reshape(weights[k].shape)
        deltas[k], new_m[k], new_v[k] = (o.reshape(weights[k].shape) for o in small_outs[3 * i:3 * i + 3])

    r_in, r_out, _ = _exchange_wait(scatter_a, deltas["ada_w"], "scatter", "scatter_a_wait")
    adamw("a_w_in", [r_in])
    adamw("a_w_out", [r_out])

    return (loss, dx0[None], *[grads[k] for k in names], *[deltas[k] for k in names],
            *[new_m[k] for k in names], *[new_v[k] for k in names])
```

```python
import jax
import jax.numpy as jnp
from jax import lax
from jax.experimental import pallas as pl
from jax.experimental.pallas import tpu as pltpu

NDEV = 8
MESH_AXES = ("x", "y", "c")
LANES = 128
SUBLANES = 8
VMEM_LIMIT = 56 * 1024 * 1024
LN_EPS = 1e-5
SHORT_PAD = 16
LONG_PAD = 32
CHUNK = 16
DW_ROWS = 1024
ADAM_LR, ADAM_B1, ADAM_B2, ADAM_EPS, ADAM_WD, ADAM_STEP = 0.001, 0.9, 0.999, 1e-08, 0.01, 10

F32 = jnp.float32
BF16 = jnp.bfloat16
MESH = pl.DeviceIdType.MESH
NT = (((1,), (1,)), ((), ()))
TN = (((0,), (0,)), ((), ()))


def _tile(n, target, mult=SUBLANES):
    best = None
    for t in range(mult, min(n, target) + 1, mult):
        if n % t == 0:
            best = t
    return best if best is not None else n


def _full(shape):
    nd = len(shape)
    return pl.BlockSpec(shape, lambda *_: (0,) * nd)


def _cp(*sem):
    return pltpu.CompilerParams(dimension_semantics=sem, vmem_limit_bytes=VMEM_LIMIT)


def _sigmoid(x):
    return 1.0 / (1.0 + jnp.exp(-x))


def _peer(x, y, c, d):
    return ((1 - x) if d & 4 else x, (1 - y) if d & 2 else y, (1 - c) if d & 1 else c)


def _lin(p):
    return 4 * p[0] + 2 * p[1] + p[2]


CHIP_MASKS = (2, 4, 6)
MODES_PER_ARRAY = {"gather": NDEV - 1, "scatter": NDEV - 1, "gather_chips": 1 + len(CHIP_MASKS),
                   "forward": len(CHIP_MASKS)}


def _transfers(mode):
    x, y, c = (lax.axis_index(a) for a in MESH_AXES)
    me = _lin((x, y, c))
    if mode == "forward":
        sibling = (x, y, 1 - c)
        return [(sibling, ("land", _lin(_peer(x, y, c, q))), _lin(_peer(x, y, c, q)), _lin(_peer(x, y, c, q ^ 1)))
                for q in CHIP_MASKS]
    masks = (1,) + CHIP_MASKS if mode == "gather_chips" else range(1, NDEV)
    out = []
    for d in masks:
        peer = _peer(x, y, c, d)
        source = ("block", _lin(peer)) if mode == "scatter" else ("whole", None)
        out.append((peer, source, me, _lin(peer)))
    return out


def _remote_copies(src_refs, land_refs, send_sems, recv_sems, mode):
    transfers = _transfers(mode)
    sends, recvs = [], []
    for i, land_ref in enumerate(land_refs):
        for t, (peer, (kind, slot), there, here) in enumerate(transfers):
            k = i * len(transfers) + t
            src = land_ref.at[slot] if kind == "land" else src_refs[i].at[slot] if kind == "block" else src_refs[i]
            for dst_slot, out in ((there, sends), (here, recvs)):
                out.append(pltpu.make_async_remote_copy(
                    src_ref=src, dst_ref=land_ref.at[dst_slot], send_sem=send_sems.at[k], recv_sem=recv_sems.at[k],
                    device_id=peer, device_id_type=MESH))
    return sends, recvs


def _exchange(srcs, mode, name):
    n = len(srcs)
    gather = mode == "gather"

    def body(*refs):
        src_refs, out_refs, token = refs[:n], refs[n:2 * n], refs[2 * n]
        send_sems, recv_sems, local_sems = refs[2 * n + 1:]
        me = _lin(tuple(lax.axis_index(a) for a in MESH_AXES))
        local = []
        for i in range(n):
            mine = src_refs[i] if gather else src_refs[i].at[me]
            cp = pltpu.make_async_copy(mine, out_refs[i].at[me], local_sems.at[i])
            cp.start()
            local.append(cp)
        sends, recvs = _remote_copies(src_refs, out_refs, send_sems, recv_sems, mode)
        for snd in sends:
            snd.start()
        token[...] = jnp.zeros_like(token)
        for snd, rcv in zip(sends, recvs):
            snd.wait_send()
            rcv.wait_recv()
        for cp in local:
            cp.wait()

    out_shape = [jax.ShapeDtypeStruct(((NDEV,) + s.shape) if gather else s.shape, s.dtype) for s in srcs]
    out_shape.append(jax.ShapeDtypeStruct((SUBLANES, LANES), F32))
    any_spec = pl.BlockSpec(memory_space=pl.ANY)
    return pl.pallas_call(
        body, name=name, out_shape=out_shape,
        in_specs=[any_spec] * n, out_specs=[any_spec] * n + [pl.BlockSpec(memory_space=pltpu.VMEM)],
        scratch_shapes=[pltpu.SemaphoreType.DMA((n * (NDEV - 1),)),
                        pltpu.SemaphoreType.DMA((n * (NDEV - 1),)),
                        pltpu.SemaphoreType.DMA((n,))],
    )(*srcs)


HBM_SPEC = pl.BlockSpec(memory_space=pltpu.HBM)
SEM_SPEC = pl.BlockSpec(memory_space=pltpu.SEMAPHORE)
SIDE_EFFECT = pltpu.SideEffectType.DATAFLOW_SIDE_EFFECTING


def _exchange_start(arrays, mode, name):
    me = _lin(tuple(lax.axis_index(a) for a in MESH_AXES))
    if mode == "forward":
        srcs, lands = [], list(arrays)
    else:
        srcs, lands = list(arrays), []
        for s in srcs:
            own = lax.dynamic_index_in_dim(s, me, 0, keepdims=False) if mode == "scatter" else s
            shape = s.shape if mode == "scatter" else (NDEV,) + s.shape
            lands.append(lax.dynamic_update_index_in_dim(lax.empty(shape, s.dtype), own, me, 0))
    ns, n = len(srcs), len(lands)

    def body(*refs):
        src_refs, land_refs = refs[:ns], refs[ns:ns + n]
        send_sems, recv_sems, token = refs[ns + n], refs[ns + n + 1], refs[-1]
        sends, _ = _remote_copies(src_refs, land_refs, send_sems, recv_sems, mode)
        for snd in sends:
            snd.start()
        token[...] = jnp.zeros_like(token)

    operands = [pltpu.with_memory_space_constraint(a, pltpu.HBM) for a in srcs + lands]
    nsem = n * MODES_PER_ARRAY[mode]
    return pl.pallas_call(
        body, name=name,
        out_shape=(pltpu.SemaphoreType.DMA((nsem,)), pltpu.SemaphoreType.DMA((nsem,)),
                   *[pltpu.HBM(a.shape, a.dtype) for a in operands],
                   jax.ShapeDtypeStruct((SUBLANES, LANES), F32)),
        in_specs=[HBM_SPEC] * (ns + n),
        out_specs=(SEM_SPEC, SEM_SPEC, *([HBM_SPEC] * (ns + n)), pl.BlockSpec(memory_space=pltpu.VMEM)),
        input_output_aliases={i: 2 + i for i in range(ns + n)},
        compiler_params=pltpu.CompilerParams(has_side_effects=SIDE_EFFECT),
    )(*operands)


def _exchange_wait(handle, after, mode, name):
    send_sems, recv_sems, *thru = handle[:-1]
    n = len(thru) if mode == "forward" else len(thru) // 2
    ns = len(thru) - n

    def body(*refs):
        src_refs, land_refs = refs[:ns], refs[ns:ns + n]
        sends, recvs = _remote_copies(src_refs, land_refs, refs[ns + n], refs[ns + n + 1], mode)
        for snd, rcv in zip(sends, recvs):
            snd.wait_send()
            rcv.wait_recv()
        refs[-1][...] = jnp.zeros_like(refs[-1])

    outs = pl.pallas_call(
        body, name=name,
        out_shape=(*[pltpu.HBM(a.shape, a.dtype) for a in thru], jax.ShapeDtypeStruct((SUBLANES, LANES), F32)),
        in_specs=[HBM_SPEC] * (ns + n) + [SEM_SPEC, SEM_SPEC, pl.BlockSpec(memory_space=pl.ANY)],
        out_specs=[HBM_SPEC] * (ns + n) + [pl.BlockSpec(memory_space=pltpu.VMEM)],
        input_output_aliases={i: i for i in range(ns + n)},
        compiler_params=pltpu.CompilerParams(has_side_effects=SIDE_EFFECT),
    )(*thru, send_sems, recv_sems, after)
    return outs[ns:]


def _gather_two_level(srcs, after, name):
    first = _exchange_start(srcs, "gather_chips", name + "_chips_start")
    *lands, token = _exchange_wait(first, after, "gather_chips", name + "_chips_wait")
    second = _exchange_start(lands, "forward", name + "_forward_start")
    return _exchange_wait(second, token, "forward", name + "_forward_wait")


def _after(value, token):
    return value + token[0, 0]


ANY_SPEC = pl.BlockSpec(memory_space=pl.ANY)


def _load_cols(wg_ref, widx, w_ref, sems):
    n = wg_ref.shape[-1]
    copies = [pltpu.make_async_copy(wg_ref.at[k, widx], w_ref.at[:, pl.ds(k * n, n)], sems.at[k])
              for k in range(NDEV)]
    for cp in copies:
        cp.start()
    for cp in copies:
        cp.wait()


def _load_rows(wg_ref, r, ridx, w_ref, sems):
    copies = [pltpu.make_async_copy(wg_ref.at[k, pl.ds(ridx * r, r)], w_ref.at[pl.ds(k * r, r)], sems.at[k])
              for k in range(NDEV)]
    for cp in copies:
        cp.start()
    for cp in copies:
        cp.wait()


def _mm_fwd(x, sc, sh, bias, wg, widxs, name):
    s_len, kdim = x.shape
    ncol = NDEV * wg.shape[-1]
    tm = _tile(s_len, 512)
    nw = len(widxs)

    def body(x_ref, sc_ref, sh_ref, b_ref, wg_ref, *rest):
        o_refs, w_refs, sems = rest[:nw], rest[nw:2 * nw], rest[2 * nw]

        @pl.when(pl.program_id(0) == 0)
        def _():
            for i, w_ref in enumerate(w_refs):
                _load_cols(wg_ref, widxs[i], w_ref, sems.at[i])

        h = (x_ref[...] * (1.0 + sc_ref[...]) + sh_ref[...]).astype(BF16)
        for w_ref, o_ref in zip(w_refs, o_refs):
            o_ref[...] = (jnp.dot(h, w_ref[...], preferred_element_type=F32) + b_ref[...]).astype(BF16)

    return pl.pallas_call(
        body, name=name, grid=(s_len // tm,),
        in_specs=[pl.BlockSpec((tm, kdim), lambda i: (i, 0)), _full((1, kdim)), _full((1, kdim)),
                  _full((1, ncol)), ANY_SPEC],
        out_specs=[pl.BlockSpec((tm, ncol), lambda i: (i, 0))] * nw,
        out_shape=[jax.ShapeDtypeStruct((s_len, ncol), BF16)] * nw,
        scratch_shapes=[pltpu.VMEM((kdim, ncol), BF16)] * nw + [pltpu.SemaphoreType.DMA((nw, NDEV))],
        compiler_params=_cp("arbitrary"),
    )(x, sc, sh, bias, wg)


def _mm_ln(a, wg, r, ridx, xres, gate, gam, bet, bias, alpha, name):
    s_len = a.shape[0]
    d = wg.shape[-1]
    tm = _tile(s_len, 512)

    def body(a_ref, wg_ref, x_ref, g_ref, gam_ref, bet_ref, b_ref, y_ref, xo_ref, xh_ref, rs_ref, w_ref, sems):
        @pl.when(pl.program_id(0) == 0)
        def _():
            _load_rows(wg_ref, r, ridx, w_ref, sems)

        y = jnp.dot(a_ref[...], w_ref[...], preferred_element_type=F32) + b_ref[...]
        z = alpha * x_ref[...] + g_ref[...] * y
        mu = jnp.mean(z, axis=-1, keepdims=True)
        zc = z - mu
        var = jnp.mean(zc * zc, axis=-1, keepdims=True)
        rstd = lax.rsqrt(var + LN_EPS)
        xh = zc * rstd
        y_ref[...] = y.astype(BF16)
        xh_ref[...] = xh
        rs_ref[...] = rstd
        xo_ref[...] = xh * gam_ref[...] + bet_ref[...]

    row = pl.BlockSpec((tm, d), lambda i: (i, 0))
    vec = _full((1, d))
    return pl.pallas_call(
        body, name=name, grid=(s_len // tm,),
        in_specs=[pl.BlockSpec((tm, NDEV * r), lambda i: (i, 0)), ANY_SPEC, row, vec, vec, vec, vec],
        out_specs=[row, row, row, pl.BlockSpec((tm, 1), lambda i: (i, 0))],
        out_shape=[jax.ShapeDtypeStruct((s_len, d), BF16)] + [jax.ShapeDtypeStruct((s_len, d), F32)] * 2
        + [jax.ShapeDtypeStruct((s_len, 1), F32)],
        scratch_shapes=[pltpu.VMEM((NDEV * r, d), BF16), pltpu.SemaphoreType.DMA((NDEV,))],
        compiler_params=_cp("arbitrary"),
    )(a, wg, xres, gate, gam, bet, bias)


def _ln_in_specs(tm, d):
    row = pl.BlockSpec((tm, d), lambda i: (i, 0))
    return [row, pl.BlockSpec((tm, 1), lambda i: (i, 0)), _full((1, d)), row, _full((1, d))]


def _ln_out_specs(s_len, tm, d):
    row = pl.BlockSpec((tm, d), lambda i: (i, 0))
    return ([row, row, _full((SUBLANES, d))],
            [jax.ShapeDtypeStruct((s_len, d), BF16), jax.ShapeDtypeStruct((s_len, d), F32),
             jax.ShapeDtypeStruct((SUBLANES, d), F32)])


def _ln_bwd_rows(dxo, ln_refs, out_refs, alpha):
    xh_ref, rs_ref, gam_ref, y_ref, g_ref = ln_refs
    dy_ref, dres_ref, acc_ref = out_refs
    xh = xh_ref[...]
    dxh = dxo * gam_ref[...]
    m1 = jnp.mean(dxh, axis=-1, keepdims=True)
    m2 = jnp.mean(dxh * xh, axis=-1, keepdims=True)
    dz = rs_ref[...] * (dxh - m1 - xh * m2)
    dy = g_ref[...] * dz
    dy_ref[...] = dy.astype(BF16)
    dres_ref[...] = alpha * dz
    acc_ref[0:1, :] += jnp.sum(dxo * xh, axis=0, keepdims=True)
    acc_ref[1:2, :] += jnp.sum(dxo, axis=0, keepdims=True)
    acc_ref[2:3, :] += jnp.sum(dz * y_ref[...].astype(F32), axis=0, keepdims=True)
    acc_ref[3:4, :] += jnp.sum(dy, axis=0, keepdims=True)


def _mm_nt_row(dy, wg, r, ridx, name):
    s_len, d = dy.shape
    tm = _tile(s_len, 512)

    def body(dy_ref, wg_ref, o_ref, w_ref, sems):
        @pl.when(pl.program_id(0) == 0)
        def _():
            _load_rows(wg_ref, r, ridx, w_ref, sems)

        o_ref[...] = lax.dot_general(dy_ref[...], w_ref[...], NT, preferred_element_type=F32).astype(BF16)

    return pl.pallas_call(
        body, name=name, grid=(s_len // tm,),
        in_specs=[pl.BlockSpec((tm, d), lambda i: (i, 0)), ANY_SPEC],
        out_specs=pl.BlockSpec((tm, NDEV * r), lambda i: (i, 0)),
        out_shape=jax.ShapeDtypeStruct((s_len, NDEV * r), BF16),
        scratch_shapes=[pltpu.VMEM((NDEV * r, d), BF16), pltpu.SemaphoreType.DMA((NDEV,))],
        compiler_params=_cp("arbitrary"),
    )(dy, wg)


def _mm_nt_mod(dos, wg, widxs, xin, sc, dres, name, ln=None, alpha=None):
    s_len, kdim = xin.shape
    ncol = NDEV * wg.shape[-1]
    tm = _tile(s_len, 512)
    nw = len(widxs)
    nln = 0 if ln is None else len(ln)
    nout = 2 if ln is None else 4

    def body(*refs):
        do_refs, wg_ref = refs[:nw], refs[nw]
        x_ref, sc_ref, dres_ref = refs[nw + 1:nw + 4]
        ln_refs = refs[nw + 4:nw + 4 + nln]
        out_refs = refs[nw + 4 + nln:nw + 4 + nln + nout]
        w_refs, sems = refs[nw + 4 + nln + nout:-1], refs[-1]
        acc_ref = out_refs[-1]

        @pl.when(pl.program_id(0) == 0)
        def _():
            for ref in out_refs[nout // 2:]:
                ref[...] = jnp.zeros_like(ref)
            for i, w_ref in enumerate(w_refs):
                _load_cols(wg_ref, widxs[i], w_ref, sems.at[i])

        dh = None
        for do_ref, w_ref in zip(do_refs, w_refs):
            p = lax.dot_general(do_ref[...], w_ref[...], NT, preferred_element_type=F32)
            dh = p if dh is None else dh + p
        dx = dh * (1.0 + sc_ref[...]) + dres_ref[...]
        if ln is None:
            out_refs[0][...] = dx
        else:
            _ln_bwd_rows(dx, ln_refs, out_refs[0:3], alpha)
        acc_ref[0:1, :] += jnp.sum(dh * x_ref[...], axis=0, keepdims=True)
        acc_ref[1:2, :] += jnp.sum(dh, axis=0, keepdims=True)

    row = pl.BlockSpec((tm, kdim), lambda i: (i, 0))
    if ln is None:
        out_specs, out_shape = [row], [jax.ShapeDtypeStruct((s_len, kdim), F32)]
    else:
        out_specs, out_shape = _ln_out_specs(s_len, tm, kdim)
    return pl.pallas_call(
        body, name=name, grid=(s_len // tm,),
        in_specs=[pl.BlockSpec((tm, ncol), lambda i: (i, 0))] * nw + [ANY_SPEC, row, _full((1, kdim)), row]
        + ([] if ln is None else _ln_in_specs(tm, kdim)),
        out_specs=out_specs + [_full((SUBLANES, kdim))],
        out_shape=out_shape + [jax.ShapeDtypeStruct((SUBLANES, kdim), F32)],
        scratch_shapes=[pltpu.VMEM((kdim, ncol), BF16)] * nw + [pltpu.SemaphoreType.DMA((nw, NDEV))],
        compiler_params=_cp("arbitrary"),
    )(*dos, wg, xin, sc, dres, *([] if ln is None else ln))


def _mm_tn_col(x, sc, sh, do, name):
    s_len, kdim = x.shape
    n = do.shape[1] // NDEV
    ts = _tile(s_len, DW_ROWS)
    nsteps = s_len // ts

    def body(x_ref, sc_ref, sh_ref, do_ref, o_ref, acc_ref):
        @pl.when(pl.program_id(0) == 0)
        def _():
            acc_ref[...] = jnp.zeros_like(acc_ref)

        h = (x_ref[...] * (1.0 + sc_ref[...]) + sh_ref[...]).astype(BF16)
        acc_ref[...] += lax.dot_general(h, do_ref[...], TN, preferred_element_type=F32)

        @pl.when(pl.program_id(0) == nsteps - 1)
        def _():
            for k in range(NDEV):
                o_ref[k] = acc_ref[:, k * n:(k + 1) * n].astype(BF16)

    return pl.pallas_call(
        body, name=name, grid=(nsteps,),
        in_specs=[pl.BlockSpec((ts, kdim), lambda i: (i, 0)), _full((1, kdim)), _full((1, kdim)),
                  pl.BlockSpec((ts, NDEV * n), lambda i: (i, 0))],
        out_specs=_full((NDEV, kdim, n)),
        out_shape=jax.ShapeDtypeStruct((NDEV, kdim, n), BF16),
        scratch_shapes=[pltpu.VMEM((kdim, NDEV * n), F32)],
        compiler_params=_cp("arbitrary"),
    )(x, sc, sh, do)


def _mm_tn_col_t(x, sc, sh, do, rows_out, name):
    s_len, kdim = x.shape
    n = do.shape[1] // NDEV
    ts = _tile(s_len, DW_ROWS)
    nsteps = s_len // ts

    def body(x_ref, sc_ref, sh_ref, do_ref, o_ref, acc_ref):
        @pl.when(pl.program_id(0) == 0)
        def _():
            acc_ref[...] = jnp.zeros_like(acc_ref)

        h = (x_ref[...] * (1.0 + sc_ref[...]) + sh_ref[...]).astype(BF16)
        acc_ref[...] += lax.dot_general(do_ref[...], h, TN, preferred_element_type=F32)

        @pl.when(pl.program_id(0) == nsteps - 1)
        def _():
            for k in range(NDEV):
                o_ref[k] = acc_ref[k * n:k * n + rows_out, :].astype(BF16)

    return pl.pallas_call(
        body, name=name, grid=(nsteps,),
        in_specs=[pl.BlockSpec((ts, kdim), lambda i: (i, 0)), _full((1, kdim)), _full((1, kdim)),
                  pl.BlockSpec((ts, NDEV * n), lambda i: (i, 0))],
        out_specs=_full((NDEV, rows_out, kdim)),
        out_shape=jax.ShapeDtypeStruct((NDEV, rows_out, kdim), BF16),
        scratch_shapes=[pltpu.VMEM((NDEV * n, kdim), F32)],
        compiler_params=_cp("arbitrary"),
    )(x, sc, sh, do)


def _mm_tn_row(a, dy, r, rows_out, name):
    s_len, d = dy.shape
    ts = _tile(s_len, DW_ROWS)
    nsteps = s_len // ts

    def body(a_ref, dy_ref, o_ref, acc_ref):
        @pl.when(pl.program_id(0) == 0)
        def _():
            acc_ref[...] = jnp.zeros_like(acc_ref)

        acc_ref[...] += lax.dot_general(a_ref[...], dy_ref[...], TN, preferred_element_type=F32)

        @pl.when(pl.program_id(0) == nsteps - 1)
        def _():
            for k in range(NDEV):
                o_ref[k] = acc_ref[k * r:k * r + rows_out, :].astype(BF16)

    return pl.pallas_call(
        body, name=name, grid=(nsteps,),
        in_specs=[pl.BlockSpec((ts, NDEV * r), lambda i: (i, 0)), pl.BlockSpec((ts, d), lambda i: (i, 0))],
        out_specs=_full((NDEV, rows_out, d)),
        out_shape=jax.ShapeDtypeStruct((NDEV, rows_out, d), BF16),
        scratch_shapes=[pltpu.VMEM((NDEV * r, d), F32)],
        compiler_params=_cp("arbitrary"),
    )(a, dy)


def _prev_spec(ts, pad, cb, col):
    return pl.BlockSpec((pad, cb), lambda *g: (jnp.maximum(g[-1] * (ts // pad) - 1, 0), col(g)))


def _next_spec(ts, pad, cb, col, s_len):
    return pl.BlockSpec((pad, cb), lambda *g: (jnp.minimum((g[-1] + 1) * (ts // pad), s_len // pad - 1), col(g)))


class _F32Loads:
    def __init__(self, ref):
        self.ref = ref

    def __getitem__(self, idx):
        return self.ref[idx].astype(F32)


def _direct(buf_ref):
    return lambda off, rows: buf_ref[off:off + rows, :]


def _make_shifts(sh_ref, nrows):
    for r in range(1, SUBLANES):
        sh_ref[r, 0:nrows - SUBLANES, :] = sh_ref[0, r:r + nrows - SUBLANES, :]


def _shifted(sh_ref):
    def read(off, rows):
        r = off % SUBLANES
        return sh_ref[r, off - r:off - r + rows, :]
    return read


def _conv_fwd_rows(read, w_ref, b_ref, ktaps, pad, r0, rows):
    acc = None
    for j in range(ktaps):
        term = w_ref[ktaps - 1 - j:ktaps - j, :] * read(pad - j + r0, rows)
        acc = term if acc is None else acc + term
    return acc + b_ref[...]


def _conv_bwd_rows(read, x_rows, w_ref, dwacc_ref, ktaps, r0, rows):
    acc = None
    for j in range(ktaps):
        sl = read(j + r0, rows)
        term = w_ref[ktaps - 1 - j:ktaps - j, :] * sl
        acc = term if acc is None else acc + term
        prod = x_rows * sl
        fold = prod[0:SUBLANES]
        for q in range(1, rows // SUBLANES):
            fold = fold + prod[q * SUBLANES:(q + 1) * SUBLANES]
        tap = ktaps - 1 - j
        dwacc_ref[tap * SUBLANES:(tap + 1) * SUBLANES, :] += fold
    return acc


def _flush_dw(dwacc_ref, dw_ref, ktaps):
    for tap in range(ktaps):
        dw_ref[tap:tap + 1, :] = jnp.sum(dwacc_ref[tap * SUBLANES:(tap + 1) * SUBLANES, :], axis=0, keepdims=True)


def _gateconv_fwd(bcv, cw, cb, name):
    s_len, d3 = bcv.shape
    d = d3 // 3
    ktaps = cw.shape[0]
    pad = SHORT_PAD
    ts = _tile(s_len, 256)

    def body(gb_ref, gc_ref, v_ref, gcp_ref, vp_ref, w_ref, b_ref, o_ref, pbuf):
        gb_ref, gc_ref, v_ref, gcp_ref, vp_ref = map(_F32Loads, (gb_ref, gc_ref, v_ref, gcp_ref, vp_ref))
        s = pl.program_id(0)
        pbuf[0:pad, :] = jnp.where(s > 0, gcp_ref[...] * vp_ref[...], 0.0)
        pbuf[pad:pad + ts, :] = gc_ref[...] * v_ref[...]
        for r0 in range(0, ts, CHUNK):
            q = _conv_fwd_rows(_direct(pbuf), w_ref, b_ref, ktaps, pad, r0, CHUNK)
            o_ref[r0:r0 + CHUNK, :] = (gb_ref[r0:r0 + CHUNK, :] * q).astype(BF16)

    def cur(part):
        return pl.BlockSpec((ts, d), lambda s: (s, part))

    return pl.pallas_call(
        body, name=name, grid=(s_len // ts,),
        in_specs=[cur(0), cur(1), cur(2),
                  _prev_spec(ts, pad, d, lambda g: 1), _prev_spec(ts, pad, d, lambda g: 2),
                  _full((ktaps, d)), _full((1, d))],
        out_specs=pl.BlockSpec((ts, d), lambda s: (s, 0)),
        out_shape=jax.ShapeDtypeStruct((s_len, d), BF16),
        scratch_shapes=[pltpu.VMEM((pad + ts, d), F32)],
        compiler_params=_cp("parallel"),
    )(bcv, bcv, bcv, bcv, bcv, cw, cb)


def _gateconv_bwd(bcv, dy0, cw, cb, name):
    s_len, d3 = bcv.shape
    d = d3 // 3
    ktaps = cw.shape[0]
    pad = SHORT_PAD
    ts = _tile(s_len, 256)
    nsteps = s_len // ts

    def body(gb_ref, gc_ref, v_ref, gcp_ref, vp_ref, gbn_ref, dy_ref, dyn_ref, w_ref, b_ref,
             o_ref, dw_ref, db_ref, pbuf, dqbuf, dwacc):
        gb_ref, gc_ref, v_ref, gcp_ref, vp_ref, gbn_ref, dy_ref, dyn_ref = map(
            _F32Loads, (gb_ref, gc_ref, v_ref, gcp_ref, vp_ref, gbn_ref, dy_ref, dyn_ref))
        s = pl.program_id(0)

        @pl.when(s == 0)
        def _():
            dwacc[...] = jnp.zeros_like(dwacc)
            db_ref[...] = jnp.zeros_like(db_ref)

        pbuf[0:pad, :] = jnp.where(s > 0, gcp_ref[...] * vp_ref[...], 0.0)
        pbuf[pad:pad + ts, :] = gc_ref[...] * v_ref[...]
        dq = dy_ref[...] * gb_ref[...]
        dqbuf[0:ts, :] = dq
        dqbuf[ts:ts + pad, :] = jnp.where(s < nsteps - 1, dyn_ref[...] * gbn_ref[...], 0.0)
        db_ref[...] += jnp.sum(dq, axis=0, keepdims=True)
        for r0 in range(0, ts, CHUNK):
            rows = slice(r0, r0 + CHUNK)
            q = _conv_fwd_rows(_direct(pbuf), w_ref, b_ref, ktaps, pad, r0, CHUNK)
            o_ref[rows, 0:d] = (dy_ref[rows, :] * q).astype(BF16)
            dp = _conv_bwd_rows(_direct(dqbuf), pbuf[pad + r0:pad + r0 + CHUNK, :], w_ref, dwacc, ktaps, r0, CHUNK)
            o_ref[rows, d:2 * d] = (dp * v_ref[rows, :]).astype(BF16)
            o_ref[rows, 2 * d:3 * d] = (dp * gc_ref[rows, :]).astype(BF16)

        @pl.when(s == nsteps - 1)
        def _():
            _flush_dw(dwacc, dw_ref, ktaps)

    def cur(part):
        return pl.BlockSpec((ts, d), lambda s: (s, part))

    return pl.pallas_call(
        body, name=name, grid=(nsteps,),
        in_specs=[cur(0), cur(1), cur(2),
                  _prev_spec(ts, pad, d, lambda g: 1), _prev_spec(ts, pad, d, lambda g: 2),
                  _next_spec(ts, pad, d, lambda g: 0, s_len),
                  cur(0), _next_spec(ts, pad, d, lambda g: 0, s_len),
                  _full((ktaps, d)), _full((1, d))],
        out_specs=[pl.BlockSpec((ts, d3), lambda s: (s, 0)), _full((ktaps, d)), _full((1, d))],
        out_shape=[jax.ShapeDtypeStruct((s_len, d3), BF16), jax.ShapeDtypeStruct((ktaps, d), F32),
                   jax.ShapeDtypeStruct((1, d), F32)],
        scratch_shapes=[pltpu.VMEM((pad + ts, d), F32), pltpu.VMEM((ts + pad, d), F32),
                        pltpu.VMEM((ktaps * SUBLANES, d), F32)],
        compiler_params=_cp("arbitrary"),
    )(bcv, bcv, bcv, bcv, bcv, bcv, dy0, dy0, cw, cb)


class _Cols:
    def __init__(self, ref, cols):
        self.ref, self.cols = ref, cols

    def __getitem__(self, idx):
        return self.ref[slice(None) if idx is Ellipsis else idx[0], self.cols]

    def __setitem__(self, idx, value):
        self.ref[idx[0], self.cols] = value


def _ffn_tail_fwd(u0, vg, cw, cb, wg, xres, gate, gam, bet, alpha, name):
    s_len, f = u0.shape
    d = wg.shape[-1]
    r = f // NDEV
    ktaps = cw.shape[0]
    pad = SHORT_PAD
    tm = _tile(s_len, 256)
    cbk = 1024 if f % 1024 == 0 else f

    def body(u_ref, up_ref, vg_ref, cw_ref, cb_ref, wg_ref, x_ref, g_ref, gam_ref, bet_ref,
             t_ref, y_ref, xo_ref, xh_ref, rs_ref, ubuf, w_ref, sems):
        u_ref, up_ref, vg_ref = map(_F32Loads, (u_ref, up_ref, vg_ref))
        s = pl.program_id(0)

        @pl.when(s == 0)
        def _():
            _load_rows(wg_ref, r, 0, w_ref, sems)

        ubuf[0:pad, :] = jnp.where(s > 0, up_ref[...], 0.0)
        ubuf[pad:pad + tm, :] = u_ref[...]
        y = None
        for c0 in range(0, f, cbk):
            cols = slice(c0, c0 + cbk)
            read = _direct(_Cols(ubuf, cols))
            for r0 in range(0, tm, CHUNK):
                rows = slice(r0, r0 + CHUNK)
                u = _conv_fwd_rows(read, _Cols(cw_ref, cols), _Cols(cb_ref, cols), ktaps, pad, r0, CHUNK)
                t_ref[rows, cols] = (u * _sigmoid(u) * vg_ref[rows, cols]).astype(BF16)
            p = jnp.dot(t_ref[:, cols], w_ref[cols, :], preferred_element_type=F32)
            y = p if y is None else y + p
        z = alpha * x_ref[...] + g_ref[...] * y
        mu = jnp.mean(z, axis=-1, keepdims=True)
        zc = z - mu
        var = jnp.mean(zc * zc, axis=-1, keepdims=True)
        rstd = lax.rsqrt(var + LN_EPS)
        xh = zc * rstd
        y_ref[...] = y.astype(BF16)
        xh_ref[...] = xh
        rs_ref[...] = rstd
        xo_ref[...] = xh * gam_ref[...] + bet_ref[...]

    wide = pl.BlockSpec((tm, f), lambda i: (i, 0))
    row = pl.BlockSpec((tm, d), lambda i: (i, 0))
    vec = _full((1, d))
    return pl.pallas_call(
        body, name=name, grid=(s_len // tm,),
        in_specs=[wide, _prev_spec(tm, pad, f, lambda g: 0), wide, _full((ktaps, f)), _full((1, f)), ANY_SPEC,
                  row, vec, vec, vec],
        out_specs=[wide, row, row, row, pl.BlockSpec((tm, 1), lambda i: (i, 0))],
        out_shape=[jax.ShapeDtypeStruct((s_len, f), BF16), jax.ShapeDtypeStruct((s_len, d), BF16),
                   jax.ShapeDtypeStruct((s_len, d), F32), jax.ShapeDtypeStruct((s_len, d), F32),
                   jax.ShapeDtypeStruct((s_len, 1), F32)],
        scratch_shapes=[pltpu.VMEM((pad + tm, f), F32), pltpu.VMEM((f, d), BF16), pltpu.SemaphoreType.DMA((NDEV,))],
        compiler_params=_cp("arbitrary"),
    )(u0, u0, vg, cw, cb, wg, xres, gate, gam, bet)


def _ffn_core_bwd(dy, u0, vg, cw, cb, wg_row, wg_col, xin, sc, dres, ln, alpha, name):
    s_len, f = u0.shape
    d = xin.shape[1]
    r = f // NDEV
    ktaps = cw.shape[0]
    pad = SHORT_PAD
    tm = _tile(s_len, 256)
    nsteps = s_len // tm
    cbk = 1024 if f % 1024 == 0 else f

    def body(dy_ref, dyn_ref, u_ref, up_ref, un_ref, vg_ref, vgn_ref, cw_ref, cb_ref, wgr_ref, wgc_ref,
             x_ref, sc_ref, dres_ref, xh_ref, rs_ref, gam_ref, y_ref, g_ref,
             du0_ref, dvg_ref, dw_ref, db_ref, dyo_ref, dreso_ref, lnacc_ref, acc_ref,
             ubuf, dtbuf, dubuf, dwacc, wd_ref, wup_ref, wgate_ref, sems):
        u_ref, up_ref, un_ref, vg_ref, vgn_ref = map(_F32Loads, (u_ref, up_ref, un_ref, vg_ref, vgn_ref))
        s = pl.program_id(0)
        last = s == nsteps - 1

        @pl.when(s == 0)
        def _():
            dwacc[...] = jnp.zeros_like(dwacc)
            db_ref[...] = jnp.zeros_like(db_ref)
            acc_ref[...] = jnp.zeros_like(acc_ref)
            lnacc_ref[...] = jnp.zeros_like(lnacc_ref)
            _load_rows(wgr_ref, r, 0, wd_ref, sems.at[0])
            _load_cols(wgc_ref, 0, wup_ref, sems.at[1])
            _load_cols(wgc_ref, 1, wgate_ref, sems.at[2])

        ubuf[0:pad, :] = jnp.where(s > 0, up_ref[...], 0.0)
        ubuf[pad:pad + tm, :] = u_ref[...]
        ubuf[pad + tm:pad + tm + pad, :] = un_ref[...]
        dy_cur, dy_nxt = dy_ref[...], dyn_ref[...]
        dh = None
        for c0 in range(0, f, cbk):
            cols = slice(c0, c0 + cbk)
            wd_blk = wd_ref[cols, :]
            dtbuf[0:tm, :] = lax.dot_general(dy_cur, wd_blk, NT, preferred_element_type=F32)
            dtbuf[tm:tm + pad, :] = jnp.where(
                last, 0.0, lax.dot_general(dy_nxt, wd_blk, NT, preferred_element_type=F32))
            read_u = _direct(_Cols(ubuf, cols))
            for r0 in range(0, tm + pad, CHUNK):
                u = _conv_fwd_rows(read_u, _Cols(cw_ref, cols), _Cols(cb_ref, cols), ktaps, pad, r0, CHUNK)
                sg = _sigmoid(u)
                dtr = dtbuf[r0:r0 + CHUNK, :]
                if r0 < tm:
                    vgr = vg_ref[r0:r0 + CHUNK, cols]
                    dvg_ref[r0:r0 + CHUNK, cols] = (dtr * u * sg).astype(BF16)
                else:
                    vgr = vgn_ref[r0 - tm:r0 - tm + CHUNK, cols]
                dubuf[r0:r0 + CHUNK, :] = dtr * vgr * (sg * (1.0 + u * (1.0 - sg)))
            db_ref[:, cols] += jnp.sum(dubuf[0:tm, :], axis=0, keepdims=True)
            for r0 in range(0, tm, CHUNK):
                du0 = _conv_bwd_rows(_direct(dubuf), u_ref[r0:r0 + CHUNK, cols], _Cols(cw_ref, cols),
                                     _Cols(dwacc, cols), ktaps, r0, CHUNK)
                du0_ref[r0:r0 + CHUNK, cols] = du0.astype(BF16)
            p = (lax.dot_general(du0_ref[:, cols], wup_ref[:, cols], NT, preferred_element_type=F32)
                 + lax.dot_general(dvg_ref[:, cols], wgate_ref[:, cols], NT, preferred_element_type=F32))
            dh = p if dh is None else dh + p
        dx = dh * (1.0 + sc_ref[...]) + dres_ref[...]
        _ln_bwd_rows(dx, (xh_ref, rs_ref, gam_ref, y_ref, g_ref), (dyo_ref, dreso_ref, lnacc_ref), alpha)
        acc_ref[0:1, :] += jnp.sum(dh * x_ref[...], axis=0, keepdims=True)
        acc_ref[1:2, :] += jnp.sum(dh, axis=0, keepdims=True)

        @pl.when(last)
        def _():
            _flush_dw(dwacc, dw_ref, ktaps)

    wide = pl.BlockSpec((tm, f), lambda i: (i, 0))
    row = pl.BlockSpec((tm, d), lambda i: (i, 0))
    ln_out_specs, ln_out_shape = _ln_out_specs(s_len, tm, d)
    return pl.pallas_call(
        body, name=name, grid=(nsteps,),
        in_specs=[row, _next_spec(tm, pad, d, lambda g: 0, s_len),
                  wide, _prev_spec(tm, pad, f, lambda g: 0), _next_spec(tm, pad, f, lambda g: 0, s_len),
                  wide, _next_spec(tm, pad, f, lambda g: 0, s_len),
                  _full((ktaps, f)), _full((1, f)), ANY_SPEC, ANY_SPEC, row, _full((1, d)), row]
        + _ln_in_specs(tm, d),
        out_specs=[wide, wide, _full((ktaps, f)), _full((1, f))] + ln_out_specs + [_full((SUBLANES, d))],
        out_shape=[jax.ShapeDtypeStruct((s_len, f), BF16), jax.ShapeDtypeStruct((s_len, f), BF16),
                   jax.ShapeDtypeStruct((ktaps, f), F32), jax.ShapeDtypeStruct((1, f), F32)]
        + ln_out_shape + [jax.ShapeDtypeStruct((SUBLANES, d), F32)],
        scratch_shapes=[pltpu.VMEM((pad + tm + pad, f), F32), pltpu.VMEM((tm + pad, cbk), F32),
                        pltpu.VMEM((tm + pad, cbk), F32), pltpu.VMEM((ktaps * SUBLANES, f), F32),
                        pltpu.VMEM((f, d), BF16), pltpu.VMEM((d, f), BF16), pltpu.VMEM((d, f), BF16),
                        pltpu.SemaphoreType.DMA((3, NDEV))],
        compiler_params=_cp("arbitrary"),
    )(dy, dy, u0, u0, u0, vg, vg, cw, cb, wg_row, wg_col, xin, sc, dres, *ln)


def _b_mid_fwd(ub, cw, cb, lng, lnb, name):
    s_len, d2 = ub.shape
    d = d2 // 2
    ktaps = cw.shape[0]
    pad = LONG_PAD
    ts = _tile(s_len, 256)

    def body(a_ref, g_ref, ap_ref, gp_ref, w_ref, b_ref, lng_ref, lnb_ref, a2_ref, a4_ref, abuf):
        a_ref, g_ref, ap_ref, gp_ref = map(_F32Loads, (a_ref, g_ref, ap_ref, gp_ref))
        s = pl.program_id(0)
        abuf[0, 0:pad, :] = jnp.where(s > 0, ap_ref[...] * _sigmoid(gp_ref[...]), 0.0)
        abuf[0, pad:pad + ts, :] = a_ref[...] * _sigmoid(g_ref[...])
        _make_shifts(abuf, pad + ts)
        for r0 in range(0, ts, CHUNK):
            a2_ref[r0:r0 + CHUNK, :] = _conv_fwd_rows(_shifted(abuf), w_ref, b_ref, ktaps, pad, r0, CHUNK)
        a2 = a2_ref[...]
        mu = jnp.mean(a2, axis=-1, keepdims=True)
        ac = a2 - mu
        var = jnp.mean(ac * ac, axis=-1, keepdims=True)
        a3 = ac * lax.rsqrt(var + LN_EPS) * lng_ref[...] + lnb_ref[...]
        a4_ref[...] = (a3 * _sigmoid(a3)).astype(BF16)

    def cur(part):
        return pl.BlockSpec((ts, d), lambda s: (s, part))

    vec = _full((1, d))
    return pl.pallas_call(
        body, name=name, grid=(s_len // ts,),
        in_specs=[cur(0), cur(1), _prev_spec(ts, pad, d, lambda g: 0), _prev_spec(ts, pad, d, lambda g: 1),
                  _full((ktaps, d)), vec, vec, vec],
        out_specs=[cur(0), cur(0)],
        out_shape=[jax.ShapeDtypeStruct((s_len, d), F32), jax.ShapeDtypeStruct((s_len, d), BF16)],
        scratch_shapes=[pltpu.VMEM((SUBLANES, pad + ts, d), F32)],
        compiler_params=_cp("parallel"),
    )(ub, ub, ub, ub, cw, cb, lng, lnb)


def _b_mid_bwd(ub, a2, da4, cw, lng, lnb, name):
    s_len, d2 = ub.shape
    d = d2 // 2
    ktaps = cw.shape[0]
    pad = LONG_PAD
    ts = _tile(s_len, 256)
    nsteps = s_len // ts

    def body(a_ref, g_ref, a2_ref, a2n_ref, da4_ref, da4n_ref, w_ref, lng_ref, lnb_ref,
             du_ref, dw_ref, db_ref, dlng_ref, dlnb_ref, dbias_ref, dabuf, dwacc):
        a_ref, g_ref, da4_ref, da4n_ref = map(_F32Loads, (a_ref, g_ref, da4_ref, da4n_ref))
        s = pl.program_id(0)
        last = s == nsteps - 1

        @pl.when(s == 0)
        def _():
            dwacc[...] = jnp.zeros_like(dwacc)
            for ref in (db_ref, dlng_ref, dlnb_ref, dbias_ref):
                ref[...] = jnp.zeros_like(ref)

        def ln_silu_bwd(a2_t, da4_t):
            mu = jnp.mean(a2_t, axis=-1, keepdims=True)
            ac = a2_t - mu
            var = jnp.mean(ac * ac, axis=-1, keepdims=True)
            rstd = lax.rsqrt(var + LN_EPS)
            ah = ac * rstd
            a3 = ah * lng_ref[...] + lnb_ref[...]
            sg = _sigmoid(a3)
            da3 = da4_t * (sg * (1.0 + a3 * (1.0 - sg)))
            dah = da3 * lng_ref[...]
            m1 = jnp.mean(dah, axis=-1, keepdims=True)
            m2 = jnp.mean(dah * ah, axis=-1, keepdims=True)
            return rstd * (dah - m1 - ah * m2), da3, ah

        da2, da3, ah = ln_silu_bwd(a2_ref[...], da4_ref[...])
        dabuf[0, 0:ts, :] = da2
        dlng_ref[...] += jnp.sum(da3 * ah, axis=0, keepdims=True)
        dlnb_ref[...] += jnp.sum(da3, axis=0, keepdims=True)
        db_ref[...] += jnp.sum(da2, axis=0, keepdims=True)
        da2n, _, _ = ln_silu_bwd(a2n_ref[...], jnp.where(last, 0.0, da4n_ref[...]))
        dabuf[0, ts:ts + pad, :] = da2n
        _make_shifts(dabuf, ts + pad)
        for r0 in range(0, ts, CHUNK):
            rows = slice(r0, r0 + CHUNK)
            a_r, g_r = a_ref[rows, :], g_ref[rows, :]
            sg = _sigmoid(g_r)
            da1 = _conv_bwd_rows(_shifted(dabuf), a_r * sg, w_ref, dwacc, ktaps, r0, CHUNK)
            da = da1 * sg
            dg = da1 * a_r * sg * (1.0 - sg)
            du_ref[rows, 0:d] = da.astype(BF16)
            du_ref[rows, d:2 * d] = dg.astype(BF16)
            dbias_ref[:, 0:d] += jnp.sum(da, axis=0, keepdims=True)
            dbias_ref[:, d:2 * d] += jnp.sum(dg, axis=0, keepdims=True)

        @pl.when(last)
        def _():
            _flush_dw(dwacc, dw_ref, ktaps)

    def cur(part):
        return pl.BlockSpec((ts, d), lambda s: (s, part))

    vec = _full((1, d))
    nxt = _next_spec(ts, pad, d, lambda g: 0, s_len)
    return pl.pallas_call(
        body, name=name, grid=(nsteps,),
        in_specs=[cur(0), cur(1), cur(0), nxt, cur(0), nxt, _full((ktaps, d)), vec, vec],
        out_specs=[pl.BlockSpec((ts, d2), lambda s: (s, 0)), _full((ktaps, d)), vec, vec, vec, _full((1, d2))],
        out_shape=[jax.ShapeDtypeStruct((s_len, d2), BF16), jax.ShapeDtypeStruct((ktaps, d), F32),
                   jax.ShapeDtypeStruct((1, d), F32), jax.ShapeDtypeStruct((1, d), F32),
                   jax.ShapeDtypeStruct((1, d), F32), jax.ShapeDtypeStruct((1, d2), F32)],
        scratch_shapes=[pltpu.VMEM((SUBLANES, ts + pad, d), F32), pltpu.VMEM((ktaps * SUBLANES, d), F32)],
        compiler_params=_cp("arbitrary"),
    )(ub, ub, a2, a2, da4, da4, cw, lng, lnb)


def _loss_head(xo, tgt, ln, alpha, name):
    s_len, d = xo.shape
    tm = _tile(s_len, 512)

    def body(x_ref, t_ref, xh_ref, rs_ref, gam_ref, y_ref, g_ref, dy_ref, dres_ref, acc_ref, l_ref):
        @pl.when(pl.program_id(0) == 0)
        def _():
            l_ref[...] = jnp.zeros_like(l_ref)
            acc_ref[...] = jnp.zeros_like(acc_ref)

        e = x_ref[...] - t_ref[...]
        per_row = jnp.sum(e * e, axis=-1, keepdims=True) * (1.0 / d)
        l_ref[...] += 0.5 * jnp.sum(per_row, axis=0, keepdims=True)
        _ln_bwd_rows(e * (1.0 / d), (xh_ref, rs_ref, gam_ref, y_ref, g_ref), (dy_ref, dres_ref, acc_ref), alpha)

    row = pl.BlockSpec((tm, d), lambda i: (i, 0))
    ln_out_specs, ln_out_shape = _ln_out_specs(s_len, tm, d)
    return pl.pallas_call(
        body, name=name, grid=(s_len // tm,),
        in_specs=[row, row] + _ln_in_specs(tm, d), out_specs=ln_out_specs + [_full((1, LANES))],
        out_shape=ln_out_shape + [jax.ShapeDtypeStruct((1, LANES), F32)],
        compiler_params=_cp("arbitrary"),
    )(xo, tgt, *ln)


def _ada_fwd(c_all, ada_w, ada_b_loc, name):
    depth, d, n = ada_w.shape

    def body(c_ref, w_ref, b_ref, o_ref):
        c = c_ref[...]
        act = c * _sigmoid(c)
        o_ref[...] = jnp.dot(act, w_ref[...], preferred_element_type=F32,
                             precision=lax.Precision.HIGHEST) + b_ref[...]

    return pl.pallas_call(
        body, name=name, grid=(depth,),
        in_specs=[_full((NDEV, d)), pl.BlockSpec((None, d, n), lambda i: (i, 0, 0)),
                  pl.BlockSpec((None, 1, n), lambda i: (i, 0, 0))],
        out_specs=pl.BlockSpec((None, NDEV, n), lambda i: (i, 0, 0)),
        out_shape=jax.ShapeDtypeStruct((depth, NDEV, n), F32),
        compiler_params=_cp("parallel"),
    )(c_all, ada_w, ada_b_loc.reshape(depth, 1, n))


def _ada_bwd(c_all_t, dmod_cols, name):
    depth, _, n = dmod_cols.shape
    d = c_all_t.shape[0]

    def body(ct_ref, dm_ref, o_ref):
        ct = ct_ref[...]
        act = ct * _sigmoid(ct)
        acc = None
        for b in range(NDEV):
            term = act[:, b:b + 1] * dm_ref[b:b + 1, :]
            acc = term if acc is None else acc + term
        o_ref[...] = acc

    return pl.pallas_call(
        body, name=name, grid=(depth,),
        in_specs=[_full((d, NDEV)), pl.BlockSpec((None, NDEV, n), lambda i: (i, 0, 0))],
        out_specs=pl.BlockSpec((None, d, n), lambda i: (i, 0, 0)),
        out_shape=jax.ShapeDtypeStruct((depth, d, n), F32),
        compiler_params=_cp("parallel"),
    )(c_all_t, dmod_cols)


def _sum_parts(parts, name):
    _, rows, lanes = parts.shape

    def body(p_ref, o_ref):
        acc = p_ref[0]
        for k in range(1, NDEV):
            acc = acc + p_ref[k]
        o_ref[...] = acc

    return pl.pallas_call(
        body, name=name, in_specs=[_full(parts.shape)], out_specs=_full((rows, lanes)), grid=(1,),
        out_shape=jax.ShapeDtypeStruct((rows, lanes), F32), compiler_params=_cp("arbitrary"),
    )(parts)


def _adamw(w, glist, m, v, name):
    nl, rows, cols = w.shape
    tr = _tile(rows, 256, 2 * SUBLANES)

    def body(w_ref, *rest):
        g_refs = rest[:nl]
        m_ref, v_ref, go_ref, d_ref, mo_ref, vo_ref = rest[nl:]
        g = None
        for layer, g_ref in enumerate(g_refs):
            part = g_ref[0].astype(F32)
            for p in range(1, g_ref.shape[0]):
                part = part + g_ref[p].astype(F32)
            g = part if g is None else jnp.where(pl.program_id(0) == layer, part, g)
        go_ref[...] = g
        d_ref[...], mo_ref[...], vo_ref[...] = _adam_step(w_ref[...], g, m_ref[...], v_ref[...])

    blk = pl.BlockSpec((None, tr, cols), lambda l, i: (l, i, 0))
    g_specs = [pl.BlockSpec((g.shape[0], tr, cols), lambda l, i: (0, i, 0)) for g in glist]
    return pl.pallas_call(
        body, name=name, grid=(nl, rows // tr),
        in_specs=[blk] + g_specs + [blk, blk],
        out_specs=[blk] * 4, out_shape=[jax.ShapeDtypeStruct((nl, rows, cols), F32)] * 4,
        compiler_params=_cp("parallel", "parallel"),
    )(w, *glist, m, v)


def _adam_step(w, g, m, v):
    m1 = ADAM_B1 * m + (1.0 - ADAM_B1) * g
    v1 = ADAM_B2 * v + (1.0 - ADAM_B2) * (g * g)
    m_hat = m1 / (1.0 - ADAM_B1 ** ADAM_STEP)
    v_hat = v1 / (1.0 - ADAM_B2 ** ADAM_STEP)
    return -ADAM_LR * (m_hat / (jnp.sqrt(v_hat) + ADAM_EPS) + ADAM_WD * w), m1, v1


def _adamw_small(ws, gs, ms, vs, name):
    n = len(ws)

    def body(*refs):
        ins, outs = refs[:4 * n], refs[4 * n:]
        for i in range(n):
            w_ref, g_ref, m_ref, v_ref = ins[i], ins[n + i], ins[2 * n + i], ins[3 * n + i]
            delta, m1, v1 = _adam_step(w_ref[...], g_ref[...], m_ref[...], v_ref[...])
            outs[3 * i][...] = delta
            outs[3 * i + 1][...] = m1
            outs[3 * i + 2][...] = v1

    operands = list(ws) + list(gs) + list(ms) + list(vs)
    out_shape = [jax.ShapeDtypeStruct(w.shape, F32) for w in ws for _ in range(3)]
    return pl.pallas_call(
        body, name=name, grid=(1,), in_specs=[_full(a.shape) for a in operands],
        out_specs=[_full(s.shape) for s in out_shape], out_shape=out_shape,
        compiler_params=_cp("arbitrary"),
    )(*operands)


def _pack(pieces):
    flat = jnp.concatenate([p.reshape(-1) for p in pieces])
    unit = SUBLANES * LANES
    padded = -(-flat.shape[0] // unit) * unit
    return jnp.pad(flat, (0, padded - flat.shape[0])).reshape(padded // LANES, LANES)


def _unpack(packed, shapes, lead=()):
    flat = packed.reshape(lead + (-1,))
    out, off = [], 0
    for s in shapes:
        size = 1
        for dim in s:
            size *= dim
        out.append(flat[..., off:off + size].reshape(lead + tuple(s)))
        off += size
    return out


def _pad_last(a, n):
    return jnp.pad(a, [(0, 0)] * (a.ndim - 1) + [(0, n - a.shape[-1])])


def kernel(x, c, ada_w, ada_b, ln_tok_g, ln_tok_b, ln_ch_g, ln_ch_b, a_w_in, a_conv_w, a_conv_b, a_w_out, b_w_pw1, b_b_pw1, b_conv_w, b_conv_b, b_ln_g, b_ln_b, b_w_pw2, b_b_pw2, f_w_up, f_conv_w, f_conv_b, f_w_gate, f_w_down, loss_target, m_ada_w, m_ada_b, m_ln_tok_g, m_ln_tok_b, m_ln_ch_g, m_ln_ch_b, m_a_w_in, m_a_conv_w, m_a_conv_b, m_a_w_out, m_b_w_pw1, m_b_b_pw1, m_b_conv_w, m_b_conv_b, m_b_ln_g, m_b_ln_b, m_b_w_pw2, m_b_b_pw2, m_f_w_up, m_f_conv_w, m_f_conv_b, m_f_w_gate, m_f_w_down, v_ada_w, v_ada_b, v_ln_tok_g, v_ln_tok_b, v_ln_ch_g, v_ln_ch_b, v_a_w_in, v_a_conv_w, v_a_conv_b, v_a_w_out, v_b_w_pw1, v_b_b_pw1, v_b_conv_w, v_b_conv_b, v_b_ln_g, v_b_ln_b, v_b_w_pw2, v_b_b_pw2, v_f_w_up, v_f_conv_w, v_f_conv_b, v_f_w_gate, v_f_w_down):
    weights = dict(ada_w=ada_w, ada_b=ada_b, ln_tok_g=ln_tok_g, ln_tok_b=ln_tok_b, ln_ch_g=ln_ch_g, ln_ch_b=ln_ch_b, a_w_in=a_w_in, a_conv_w=a_conv_w, a_conv_b=a_conv_b, a_w_out=a_w_out, b_w_pw1=b_w_pw1, b_b_pw1=b_b_pw1, b_conv_w=b_conv_w, b_conv_b=b_conv_b, b_ln_g=b_ln_g, b_ln_b=b_ln_b, b_w_pw2=b_w_pw2, b_b_pw2=b_b_pw2, f_w_up=f_w_up, f_conv_w=f_conv_w, f_conv_b=f_conv_b, f_w_gate=f_w_gate, f_w_down=f_w_down)
    mom_m = dict(ada_w=m_ada_w, ada_b=m_ada_b, ln_tok_g=m_ln_tok_g, ln_tok_b=m_ln_tok_b, ln_ch_g=m_ln_ch_g, ln_ch_b=m_ln_ch_b, a_w_in=m_a_w_in, a_conv_w=m_a_conv_w, a_conv_b=m_a_conv_b, a_w_out=m_a_w_out, b_w_pw1=m_b_w_pw1, b_b_pw1=m_b_b_pw1, b_conv_w=m_b_conv_w, b_conv_b=m_b_conv_b, b_ln_g=m_b_ln_g, b_ln_b=m_b_ln_b, b_w_pw2=m_b_w_pw2, b_b_pw2=m_b_b_pw2, f_w_up=m_f_w_up, f_conv_w=m_f_conv_w, f_conv_b=m_f_conv_b, f_w_gate=m_f_w_gate, f_w_down=m_f_w_down)
    mom_v = dict(ada_w=v_ada_w, ada_b=v_ada_b, ln_tok_g=v_ln_tok_g, ln_tok_b=v_ln_tok_b, ln_ch_g=v_ln_ch_g, ln_ch_b=v_ln_ch_b, a_w_in=v_a_w_in, a_conv_w=v_a_conv_w, a_conv_b=v_a_conv_b, a_w_out=v_a_w_out, b_w_pw1=v_b_w_pw1, b_b_pw1=v_b_b_pw1, b_conv_w=v_b_conv_w, b_conv_b=v_b_conv_b, b_ln_g=v_b_ln_g, b_ln_b=v_b_ln_b, b_w_pw2=v_b_w_pw2, b_b_pw2=v_b_b_pw2, f_w_up=v_f_w_up, f_conv_w=v_f_conv_w, f_conv_b=v_f_conv_b, f_w_gate=v_f_w_gate, f_w_down=v_f_w_down)
    names = list(weights)

    depth, d, n_ada = ada_w.shape
    assert depth == 2 and a_w_in.shape[0] == 1 and b_w_pw1.shape[0] == 1
    s_len = x.shape[1]
    f_loc = f_w_up.shape[-1]
    f_pad = -(-f_loc // LANES) * LANES
    f_all = NDEV * f_pad
    d_loc = d // NDEV
    ka, kb, kf = a_conv_w.shape[1], b_conv_w.shape[1], f_conv_w.shape[1]
    alpha = (2.0 * depth) ** 0.25
    assert a_w_in.shape[-1] == f_pad and f_pad % d_loc == 0
    me = 4 * lax.axis_index("x") + 2 * lax.axis_index("y") + lax.axis_index("c")

    small_shapes = [(d,), (ka, d_loc), (2 * d_loc,), (kb, d_loc), (d_loc,), (d_loc,), (d_loc,), (d_loc,),
                    (depth, kf, f_pad)]
    small_loc = _pack([c[0], a_conv_w[0], b_b_pw1[0], b_conv_w[0], b_conv_b[0], b_ln_g[0], b_ln_b[0],
                       b_b_pw2[0], _pad_last(f_conv_w, f_pad)])
    g_small, g_in, _ = _gather_two_level([small_loc, a_w_in.astype(BF16)], small_loc, "gather_first")

    (c_all, acw_g, bb1_g, bcw_g, bcb_g, blg_g, blb_g, bb2_g, fcw_g) = _unpack(g_small, small_shapes, (NDEV,))
    a_cw = acw_g.transpose(1, 0, 2).reshape(ka, d)
    b_cw = bcw_g.transpose(1, 0, 2).reshape(kb, d)
    b_b1 = bb1_g.reshape(1, 2 * d)
    b_cb, b_lg, b_lb, b_b2 = (t.reshape(1, d) for t in (bcb_g, blg_g, blb_g, bb2_g))
    f_cw = fcw_g.transpose(1, 2, 0, 3).reshape(depth, kf, f_all)
    f_cb = _pad_last(f_conv_b.reshape(depth, NDEV, f_loc), f_pad).reshape(depth, 1, f_all)

    ada_b_loc = lax.dynamic_slice(ada_b, (0, me * n_ada), (depth, n_ada))
    mod_part = _ada_fwd(c_all, ada_w, ada_b_loc, "ada_fwd")
    mod_g, mod_done = _exchange([mod_part.reshape(depth * NDEV, n_ada)], "gather", "gather_mod")
    mod_all = mod_g.reshape(NDEV, depth, NDEV, n_ada).transpose(1, 2, 0, 3).reshape(depth, NDEV, 6 * d)
    mod = lax.dynamic_slice(mod_all, (0, me, 0), (depth, 1, 6 * d))[:, 0]

    gather_out = _exchange_start([_after(a_w_out[0], mod_done).astype(BF16)], "gather_chips", "gather_out_start")
    up_pad = _pad_last(_after(f_w_up, gather_out[-1]), f_pad).astype(BF16)
    gate_pad = _pad_last(f_w_gate, f_pad).astype(BF16)
    down_pad = jnp.pad(f_w_down, ((0, 0), (0, f_pad - f_loc), (0, 0))).astype(BF16)
    col_f = [jnp.stack([up_pad[i], gate_pad[i]]) for i in range(depth)]
    row_b = jnp.concatenate([down_pad[1], b_w_pw2[0].astype(BF16)], axis=0)
    ridx_pw2 = f_pad // d_loc
    gather_f0 = _exchange_start([col_f[0], down_pad[0]], "gather_chips", "gather_f0_start")

    def mod_rows(i):
        return [mod[i:i + 1, j * d:(j + 1) * d] for j in range(6)]

    zeros_d = jnp.zeros((1, d), F32)
    zeros_f = jnp.zeros((1, f_all), F32)
    x0 = x[0]

    sh_t0, sc_t0, g_t0, sh_c0, sc_c0, g_c0 = mod_rows(0)
    sh_t1, sc_t1, g_t1, sh_c1, sc_c1, g_c1 = mod_rows(1)

    sc_t0 = _after(sc_t0, gather_f0[-1])
    bcv, = _mm_fwd(x0, sc_t0, sh_t0, jnp.zeros((1, 3 * d), F32), g_in, (0,), "a_in_fwd")
    y0 = _gateconv_fwd(bcv, a_cw, a_conv_b, "a_conv_fwd")
    g_out, landed = _exchange_wait(gather_out, y0, "gather_chips", "gather_out_wait")
    g_out, _ = _exchange_wait(_exchange_start([g_out], "forward", "gather_out_fwd_start"), landed, "forward",
                              "gather_out_fwd_wait")
    y_a, x1, xh1, rs1 = _mm_ln(y0, g_out, d_loc, 0, x0, g_t0, ln_tok_g[0:1], ln_tok_b[0:1], zeros_d,
                               alpha, "a_out_ln_fwd")

    def ffn_fwd(xin, sc, sh, gate, gam, bet, g_colf, g_rowf, layer, tag):
        u0, vg = _mm_fwd(xin, sc, sh, zeros_f, g_colf, (0, 1), "f_upgate_fwd" + tag)
        t, y, xo, xh, rs = _ffn_tail_fwd(u0, vg, f_cw[layer], f_cb[layer], g_rowf, xin, gate, gam, bet, alpha,
                                         "f_tail_fwd" + tag)
        return u0, vg, t, y, xo, xh, rs

    g_colf0, g_rowf0, landed = _exchange_wait(gather_f0, x1, "gather_chips", "gather_f0_wait")
    g_colf0, g_rowf0, landed = _exchange_wait(
        _exchange_start([g_colf0, g_rowf0], "forward", "gather_f0_fwd_start"), landed, "forward", "gather_f0_fwd_wait")
    gather_1 = _exchange_start([_after(b_w_pw1, landed).astype(BF16), col_f[1], row_b], "gather_chips",
                               "gather_1_start")
    sc_c0 = _after(sc_c0, gather_1[-1])
    u0_0, vg_0, t_0, y_f0, x2, xh2, rs2 = ffn_fwd(x1, sc_c0, sh_c0, g_c0, ln_ch_g[0:1], ln_ch_b[0:1],
                                                  g_colf0, g_rowf0, 0, "0")

    *lands_1, landed = _exchange_wait(gather_1, x2, "gather_chips", "gather_1_wait")
    g_pw1, g_colf1, g_rowb, _ = _exchange_wait(_exchange_start(lands_1, "forward", "gather_1_fwd_start"), landed,
                                                 "forward", "gather_1_fwd_wait")
    ub, = _mm_fwd(x2, sc_t1, sh_t1, b_b1, g_pw1, (0,), "b_pw1_fwd")
    a2, a4 = _b_mid_fwd(ub, b_cw, b_cb, b_lg, b_lb, "b_mid_fwd")
    y_b, x3, xh3, rs3 = _mm_ln(a4, g_rowb, d_loc, ridx_pw2, x2, g_t1, ln_tok_g[1:2], ln_tok_b[1:2], b_b2,
                               alpha, "b_pw2_ln_fwd")
    u0_1, vg_1, t_1, y_f1, x4, xh4, rs4 = ffn_fwd(x3, sc_c1, sh_c1, g_c1, ln_ch_g[1:2], ln_ch_b[1:2],
                                                  g_colf1, g_rowb, 1, "1")

    ln_f1 = (xh4, rs4, ln_ch_g[1:2], y_f1, g_c1)
    ln_b = (xh3, rs3, ln_tok_g[1:2], y_b, g_t1)
    ln_f0 = (xh2, rs2, ln_ch_g[0:1], y_f0, g_c0)
    ln_a = (xh1, rs1, ln_tok_g[0:1], y_a, g_t0)
    dy, dres, accf1, loss_part = _loss_head(x4, loss_target[0], ln_f1, alpha, "loss_head")

    def ffn_bwd(dy, dres, xin, sc, sh, u0, vg, t, g_colf, g_rowf, ln_below, layer, tag):
        dw_down = _mm_tn_row(t, dy, f_pad, f_loc, "f_down_dw" + tag)
        scatter_down = _exchange_start([dw_down], "scatter", "scatter_d%s_start" % tag)
        du0, dvg, dcw, dcb, dy_below, dres_below, acc_below, acc2 = _ffn_core_bwd(
            dy, u0, vg, f_cw[layer], f_cb[layer], g_rowf, g_colf, xin, _after(sc, scatter_down[-1]), dres,
            ln_below, alpha, "f_core_bwd" + tag)
        dw_up = _mm_tn_col_t(xin, sc, sh, du0, f_loc, "f_up_dw" + tag)
        dw_gate = _mm_tn_col_t(xin, sc, sh, dvg, f_loc, "f_gate_dw" + tag)
        scatter = _exchange_start([dw_up, dw_gate], "scatter", "scatter_f%s_start" % tag)
        return dy_below, dres_below, acc_below, acc2, (scatter, scatter_down), dcw, dcb

    dy, dres, accb, acc2f1, (scatter_f1, scatter_d1), dfcw1, dfcb1 = ffn_bwd(
        dy, dres, x3, sc_c1, sh_c1, u0_1, vg_1, t_1, g_colf1, g_rowb, ln_b, 1, "1")

    da4 = _mm_nt_row(dy, g_rowb, d_loc, ridx_pw2, "b_pw2_dx")
    dw_pw2 = _mm_tn_row(a4, dy, d_loc, d_loc, "b_pw2_dw")
    du, dbcw, dbcb, dblg, dblb, dbb1 = _b_mid_bwd(ub, a2, da4, b_cw, _after(b_lg, scatter_f1[-1]), b_lb, "b_mid_bwd")
    dw_pw1 = _mm_tn_col(x2, sc_t1, sh_t1, du, "b_pw1_dw")
    scatter_b = _exchange_start([dw_pw1, dw_pw2], "scatter", "scatter_b_start")
    dy, dres, accf0, acc2b = _mm_nt_mod([du], g_pw1, (0,), x2, _after(sc_t1, scatter_b[-1]), dres, "b_pw1_dx",
                                        ln=ln_f0, alpha=alpha)

    dy, dres, acca, acc2f0, (scatter_f0, scatter_d0), dfcw0, dfcb0 = ffn_bwd(
        dy, dres, x1, sc_c0, sh_c0, u0_0, vg_0, t_0, g_colf0, g_rowf0, ln_a, 0, "0")

    dy0 = _mm_nt_row(dy, g_out, d_loc, 0, "a_out_dx")
    dbcv, dacw, dacb = _gateconv_bwd(bcv, dy0, a_cw, _after(a_conv_b, scatter_f0[-1]), "a_conv_bwd")
    dx0, acc2a = _mm_nt_mod([dbcv], g_in, (0,), x0, sc_t0, dres, "a_in_dx")

    def dmod_row(acc2_t, acc_t, acc2_c, acc_c):
        return jnp.concatenate([acc2_t[1], acc2_t[0], acc_t[2], acc2_c[1], acc2_c[0], acc_c[2]])

    dmod = jnp.stack([dmod_row(acc2a, acca, acc2f0, accf0), dmod_row(acc2b, accb, acc2f1, accf1)])

    def unpad_f(a):
        return a.reshape(a.shape[:-1] + (NDEV, f_pad))[..., :f_loc].reshape(a.shape[:-1] + (NDEV * f_loc,))

    small_grads = [
        dmod,
        jnp.stack([acca[0], accb[0]]), jnp.stack([acca[1], accb[1]]),
        jnp.stack([accf0[0], accf1[0]]), jnp.stack([accf0[1], accf1[1]]),
        dacb,
        unpad_f(jnp.concatenate([dfcb0, dfcb1], axis=0)),
        dacw, dbb1, dbcw, dbcb, dblg, dblb, accb[3:4],
        jnp.stack([dfcw0, dfcw1]),
        loss_part[0:1, 0:1],
    ]
    small_grad_shapes = [tuple(g.shape) for g in small_grads]
    gather_small = _exchange_start([_pack(small_grads)], "gather", "gather_small_start")

    dw_in = _mm_tn_col(x0, _after(sc_t0, gather_small[-1]), sh_t0, dbcv, "a_in_dw")
    dw_out = _mm_tn_row(y0, dy, d_loc, d_loc, "a_out_dw")
    scatter_a = _exchange_start([dw_in, dw_out], "scatter", "scatter_a_start")

    grads, deltas, new_m, new_v = {}, {}, {}, {}

    def adamw(k, glist, transposed=False):
        def view(a):
            a = jnp.swapaxes(a, 1, 2) if transposed else a
            return a.reshape(len(glist), -1, a.shape[-1])

        w = view(weights[k])
        outs = _adamw(w, [g.reshape(g.shape[0], -1, w.shape[-1]) for g in glist],
                      view(mom_m[k]), view(mom_v[k]), "adamw_" + k)
        if transposed:
            outs = [jnp.swapaxes(o, 1, 2) for o in outs]
        grads[k], deltas[k], new_m[k], new_v[k] = (o.reshape(weights[k].shape) for o in outs)

    r_up1, r_gate1, _ = _exchange_wait(scatter_f1, scatter_a[-1], "scatter", "scatter_f1_wait")
    r_down1, _ = _exchange_wait(scatter_d1, r_gate1, "scatter", "scatter_d1_wait")
    r_pw1, r_pw2, _ = _exchange_wait(scatter_b, r_down1, "scatter", "scatter_b_wait")
    adamw("b_w_pw1", [r_pw1])
    adamw("b_w_pw2", [r_pw2])
    r_down0, _ = _exchange_wait(scatter_d0, deltas["b_w_pw2"], "scatter", "scatter_d0_wait")
    adamw("f_w_down", [r_down0, r_down1])
    r_up0, r_gate0, _ = _exchange_wait(scatter_f0, deltas["f_w_down"], "scatter", "scatter_f0_wait")
    adamw("f_w_up", [r_up0, r_up1], transposed=True)
    adamw("f_w_gate", [r_gate0, r_gate1], transposed=True)

    sg_all, _ = _exchange_wait(gather_small, deltas["f_w_gate"], "gather", "gather_small_wait")
    sg_sum = _sum_parts(sg_all, "sum_small_grads")
    (g_ada_b, g_ltg, g_ltb, g_lcg, g_lcb, g_acb, g_fcb, g_acw, g_bb1, g_bcw, g_bcb, g_blg, g_blb, g_bb2,
     g_fcw, loss_all) = _unpack(sg_sum, small_grad_shapes)
    loss = loss_all[0, 0]

    def my_cols(a, width):
        return lax.dynamic_slice_in_dim(a, me * width, width, axis=a.ndim - 1)

    g_fcw_loc = my_cols(g_fcw, f_pad)[..., :f_loc]
    small = dict(
        ada_b=g_ada_b, ln_tok_g=g_ltg, ln_tok_b=g_ltb, ln_ch_g=g_lcg, ln_ch_b=g_lcb, a_conv_b=g_acb, f_conv_b=g_fcb,
        a_conv_w=my_cols(g_acw, d_loc)[None], b_b_pw1=my_cols(g_bb1, 2 * d_loc), b_conv_w=my_cols(g_bcw, d_loc)[None],
        b_conv_b=my_cols(g_bcb, d_loc), b_ln_g=my_cols(g_blg, d_loc), b_ln_b=my_cols(g_blb, d_loc),
        b_b_pw2=my_cols(g_bb2, d_loc), f_conv_w=g_fcw_loc)

    dmod_all = sg_all.reshape(NDEV, -1)[:, :depth * 6 * d].reshape(NDEV, depth, 6 * d)
    dmod_cols = my_cols(dmod_all, n_ada).transpose(1, 0, 2)
    g_ada_w = _ada_bwd(c_all.T, dmod_cols, "ada_bwd")

    adamw("ada_w", [g_ada_w[0:1], g_ada_w[1:2]])

    def rows_cols(a):
        return a.reshape(-1, a.shape[-1])

    small_keys = list(small)
    small_outs = _adamw_small([rows_cols(weights[k]) for k in small_keys], [rows_cols(small[k]) for k in small_keys],
                              [rows_cols(mom_m[k]) for k in small_keys], [rows_cols(mom_v[k]) for k in small_keys],
                              "adamw_small")
    for i, k in enumerate(small_keys):
        grads[k] = small[k].reshape(weights[k].shape)
        deltas[k], new_m[k], new_v[k] = (o.reshape(weights[k].shape) for o in small_outs[3 * i:3 * i + 3])

    r_in, r_out, _ = _exchange_wait(scatter_a, deltas["ada_w"], "scatter", "scatter_a_wait")
    adamw("a_w_in", [r_in])
    adamw("a_w_out", [r_out])

    return (loss, dx0[None], *[grads[k] for k in names], *[deltas[k] for k in names],
            *[new_m[k] for k in names], *[new_v[k] for k in names])
```

```python
import jax
import jax.numpy as jnp
from jax import lax
from jax.experimental import pallas as pl
from jax.experimental.pallas import tpu as pltpu

NDEV = 8
MESH_AXES = ("x", "y", "c")
LANES = 128
SUBLANES = 8
VMEM_LIMIT = 56 * 1024 * 1024
LN_EPS = 1e-5
SHORT_PAD = 16
LONG_PAD = 32
CHUNK = 16
DW_ROWS = 1024
ADAM_LR, ADAM_B1, ADAM_B2, ADAM_EPS, ADAM_WD, ADAM_STEP = 0.001, 0.9, 0.999, 1e-08, 0.01, 10

F32 = jnp.float32
BF16 = jnp.bfloat16
MESH = pl.DeviceIdType.MESH
NT = (((1,), (1,)), ((), ()))
TN = (((0,), (0,)), ((), ()))


def _tile(n, target, mult=SUBLANES):
    best = None
    for t in range(mult, min(n, target) + 1, mult):
        if n % t == 0:
            best = t
    return best if best is not None else n


def _full(shape):
    nd = len(shape)
    return pl.BlockSpec(shape, lambda *_: (0,) * nd)


def _cp(*sem):
    return pltpu.CompilerParams(dimension_semantics=sem, vmem_limit_bytes=VMEM_LIMIT)


def _sigmoid(x):
    return 1.0 / (1.0 + jnp.exp(-x))


def _peer(x, y, c, d):
    return ((1 - x) if d & 4 else x, (1 - y) if d & 2 else y, (1 - c) if d & 1 else c)


def _lin(p):
    return 4 * p[0] + 2 * p[1] + p[2]


CHIP_MASKS = (2, 4, 6)
MODES_PER_ARRAY = {"gather": NDEV - 1, "scatter": NDEV - 1, "gather_chips": 1 + len(CHIP_MASKS),
                   "forward": len(CHIP_MASKS)}


def _transfers(mode):
    x, y, c = (lax.axis_index(a) for a in MESH_AXES)
    me = _lin((x, y, c))
    if mode == "forward":
        sibling = (x, y, 1 - c)
        return [(sibling, ("land", _lin(_peer(x, y, c, q))), _lin(_peer(x, y, c, q)), _lin(_peer(x, y, c, q ^ 1)))
                for q in CHIP_MASKS]
    masks = (1,) + CHIP_MASKS if mode == "gather_chips" else range(1, NDEV)
    out = []
    for d in masks:
        peer = _peer(x, y, c, d)
        source = ("block", _lin(peer)) if mode == "scatter" else ("whole", None)
        out.append((peer, source, me, _lin(peer)))
    return out


def _remote_copies(src_refs, land_refs, send_sems, recv_sems, mode):
    transfers = _transfers(mode)
    sends, recvs = [], []
    for i, land_ref in enumerate(land_refs):
        for t, (peer, (kind, slot), there, here) in enumerate(transfers):
            k = i * len(transfers) + t
            src = land_ref.at[slot] if kind == "land" else src_refs[i].at[slot] if kind == "block" else src_refs[i]
            for dst_slot, out in ((there, sends), (here, recvs)):
                out.append(pltpu.make_async_remote_copy(
                    src_ref=src, dst_ref=land_ref.at[dst_slot], send_sem=send_sems.at[k], recv_sem=recv_sems.at[k],
                    device_id=peer, device_id_type=MESH))
    return sends, recvs


def _exchange(srcs, mode, name):
    n = len(srcs)
    gather = mode == "gather"

    def body(*refs):
        src_refs, out_refs, token = refs[:n], refs[n:2 * n], refs[2 * n]
        send_sems, recv_sems, local_sems = refs[2 * n + 1:]
        me = _lin(tuple(lax.axis_index(a) for a in MESH_AXES))
        local = []
        for i in range(n):
            mine = src_refs[i] if gather else src_refs[i].at[me]
            cp = pltpu.make_async_copy(mine, out_refs[i].at[me], local_sems.at[i])
            cp.start()
            local.append(cp)
        sends, recvs = _remote_copies(src_refs, out_refs, send_sems, recv_sems, mode)
        for snd in sends:
            snd.start()
        token[...] = jnp.zeros_like(token)
        for snd, rcv in zip(sends, recvs):
            snd.wait_send()
            rcv.wait_recv()
        for cp in local:
            cp.wait()

    out_shape = [jax.ShapeDtypeStruct(((NDEV,) + s.shape) if gather else s.shape, s.dtype) for s in srcs]
    out_shape.append(jax.ShapeDtypeStruct((SUBLANES, LANES), F32))
    any_spec = pl.BlockSpec(memory_space=pl.ANY)
    return pl.pallas_call(
        body, name=name, out_shape=out_shape,
        in_specs=[any_spec] * n, out_specs=[any_spec] * n + [pl.BlockSpec(memory_space=pltpu.VMEM)],
        scratch_shapes=[pltpu.SemaphoreType.DMA((n * (NDEV - 1),)),
                        pltpu.SemaphoreType.DMA((n * (NDEV - 1),)),
                        pltpu.SemaphoreType.DMA((n,))],
    )(*srcs)


HBM_SPEC = pl.BlockSpec(memory_space=pltpu.HBM)
SEM_SPEC = pl.BlockSpec(memory_space=pltpu.SEMAPHORE)
SIDE_EFFECT = pltpu.SideEffectType.DATAFLOW_SIDE_EFFECTING


def _exchange_start(arrays, mode, name):
    if mode == "forward":
        srcs, lands = [], list(arrays)
    else:
        srcs = list(arrays)
        lands = [lax.empty(s.shape if mode == "scatter" else (NDEV,) + s.shape, s.dtype) for s in srcs]
    ns, n = len(srcs), len(lands)

    def body(*refs):
        src_refs, land_refs = refs[:ns], refs[ns:ns + n]
        send_sems, recv_sems, token, local_sems = refs[ns + n], refs[ns + n + 1], refs[-2], refs[-1]
        me = _lin(tuple(lax.axis_index(a) for a in MESH_AXES))
        local = [pltpu.make_async_copy(src_refs[i].at[me] if mode == "scatter" else src_refs[i],
                                       land_refs[i].at[me], local_sems.at[i]) for i in range(ns)]
        for cp in local:
            cp.start()
        sends, _ = _remote_copies(src_refs, land_refs, send_sems, recv_sems, mode)
        for snd in sends:
            snd.start()
        token[...] = jnp.zeros_like(token)
        for cp in local:
            cp.wait()

    operands = [pltpu.with_memory_space_constraint(a, pltpu.HBM) for a in srcs + lands]
    nsem = n * MODES_PER_ARRAY[mode]
    return pl.pallas_call(
        body, name=name,
        out_shape=(pltpu.SemaphoreType.DMA((nsem,)), pltpu.SemaphoreType.DMA((nsem,)),
                   *[pltpu.HBM(a.shape, a.dtype) for a in operands],
                   jax.ShapeDtypeStruct((SUBLANES, LANES), F32)),
        in_specs=[HBM_SPEC] * (ns + n),
        out_specs=(SEM_SPEC, SEM_SPEC, *([HBM_SPEC] * (ns + n)), pl.BlockSpec(memory_space=pltpu.VMEM)),
        input_output_aliases={i: 2 + i for i in range(ns + n)},
        scratch_shapes=[pltpu.SemaphoreType.DMA((max(ns, 1),))],
        compiler_params=pltpu.CompilerParams(has_side_effects=SIDE_EFFECT),
    )(*operands)


def _exchange_wait(handle, after, mode, name):
    send_sems, recv_sems, *thru = handle[:-1]
    n = len(thru) if mode == "forward" else len(thru) // 2
    ns = len(thru) - n

    def body(*refs):
        src_refs, land_refs = refs[:ns], refs[ns:ns + n]
        sends, recvs = _remote_copies(src_refs, land_refs, refs[ns + n], refs[ns + n + 1], mode)
        for snd, rcv in zip(sends, recvs):
            snd.wait_send()
            rcv.wait_recv()
        refs[-1][...] = jnp.zeros_like(refs[-1])

    outs = pl.pallas_call(
        body, name=name,
        out_shape=(*[pltpu.HBM(a.shape, a.dtype) for a in thru], jax.ShapeDtypeStruct((SUBLANES, LANES), F32)),
        in_specs=[HBM_SPEC] * (ns + n) + [SEM_SPEC, SEM_SPEC, pl.BlockSpec(memory_space=pl.ANY)],
        out_specs=[HBM_SPEC] * (ns + n) + [pl.BlockSpec(memory_space=pltpu.VMEM)],
        input_output_aliases={i: i for i in range(ns + n)},
        compiler_params=pltpu.CompilerParams(has_side_effects=SIDE_EFFECT),
    )(*thru, send_sems, recv_sems, after)
    return outs[ns:]


def _gather_two_level(srcs, after, name):
    first = _exchange_start(srcs, "gather_chips", name + "_chips_start")
    *lands, token = _exchange_wait(first, after, "gather_chips", name + "_chips_wait")
    second = _exchange_start(lands, "forward", name + "_forward_start")
    return _exchange_wait(second, token, "forward", name + "_forward_wait")


def _after(value, token):
    return value + token[0, 0]


ANY_SPEC = pl.BlockSpec(memory_space=pl.ANY)


def _load_cols(wg_ref, widx, w_ref, sems):
    n = wg_ref.shape[-1]
    copies = [pltpu.make_async_copy(wg_ref.at[k, widx], w_ref.at[:, pl.ds(k * n, n)], sems.at[k])
              for k in range(NDEV)]
    for cp in copies:
        cp.start()
    for cp in copies:
        cp.wait()


def _load_rows(wg_ref, r, ridx, w_ref, sems):
    copies = [pltpu.make_async_copy(wg_ref.at[k, pl.ds(ridx * r, r)], w_ref.at[pl.ds(k * r, r)], sems.at[k])
              for k in range(NDEV)]
    for cp in copies:
        cp.start()
    for cp in copies:
        cp.wait()


def _mm_fwd(x, sc, sh, bias, wg, widxs, name):
    s_len, kdim = x.shape
    ncol = NDEV * wg.shape[-1]
    tm = _tile(s_len, 512)
    nw = len(widxs)

    def body(x_ref, sc_ref, sh_ref, b_ref, wg_ref, *rest):
        o_refs, w_refs, sems = rest[:nw], rest[nw:2 * nw], rest[2 * nw]

        @pl.when(pl.program_id(0) == 0)
        def _():
            for i, w_ref in enumerate(w_refs):
                _load_cols(wg_ref, widxs[i], w_ref, sems.at[i])

        h = (x_ref[...] * (1.0 + sc_ref[...]) + sh_ref[...]).astype(BF16)
        for w_ref, o_ref in zip(w_refs, o_refs):
            o_ref[...] = (jnp.dot(h, w_ref[...], preferred_element_type=F32) + b_ref[...]).astype(BF16)

    return pl.pallas_call(
        body, name=name, grid=(s_len // tm,),
        in_specs=[pl.BlockSpec((tm, kdim), lambda i: (i, 0)), _full((1, kdim)), _full((1, kdim)),
                  _full((1, ncol)), ANY_SPEC],
        out_specs=[pl.BlockSpec((tm, ncol), lambda i: (i, 0))] * nw,
        out_shape=[jax.ShapeDtypeStruct((s_len, ncol), BF16)] * nw,
        scratch_shapes=[pltpu.VMEM((kdim, ncol), BF16)] * nw + [pltpu.SemaphoreType.DMA((nw, NDEV))],
        compiler_params=_cp("arbitrary"),
    )(x, sc, sh, bias, wg)


def _mm_ln(a, wg, r, ridx, xres, gate, gam, bet, bias, alpha, name):
    s_len = a.shape[0]
    d = wg.shape[-1]
    tm = _tile(s_len, 512)

    def body(a_ref, wg_ref, x_ref, g_ref, gam_ref, bet_ref, b_ref, y_ref, xo_ref, xh_ref, rs_ref, w_ref, sems):
        @pl.when(pl.program_id(0) == 0)
        def _():
            _load_rows(wg_ref, r, ridx, w_ref, sems)

        y = jnp.dot(a_ref[...], w_ref[...], preferred_element_type=F32) + b_ref[...]
        z = alpha * x_ref[...] + g_ref[...] * y
        mu = jnp.mean(z, axis=-1, keepdims=True)
        zc = z - mu
        var = jnp.mean(zc * zc, axis=-1, keepdims=True)
        rstd = lax.rsqrt(var + LN_EPS)
        xh = zc * rstd
        y_ref[...] = y.astype(BF16)
        xh_ref[...] = xh
        rs_ref[...] = rstd
        xo_ref[...] = xh * gam_ref[...] + bet_ref[...]

    row = pl.BlockSpec((tm, d), lambda i: (i, 0))
    vec = _full((1, d))
    return pl.pallas_call(
        body, name=name, grid=(s_len // tm,),
        in_specs=[pl.BlockSpec((tm, NDEV * r), lambda i: (i, 0)), ANY_SPEC, row, vec, vec, vec, vec],
        out_specs=[row, row, row, pl.BlockSpec((tm, 1), lambda i: (i, 0))],
        out_shape=[jax.ShapeDtypeStruct((s_len, d), BF16)] + [jax.ShapeDtypeStruct((s_len, d), F32)] * 2
        + [jax.ShapeDtypeStruct((s_len, 1), F32)],
        scratch_shapes=[pltpu.VMEM((NDEV * r, d), BF16), pltpu.SemaphoreType.DMA((NDEV,))],
        compiler_params=_cp("arbitrary"),
    )(a, wg, xres, gate, gam, bet, bias)


def _ln_in_specs(tm, d):
    row = pl.BlockSpec((tm, d), lambda i: (i, 0))
    return [row, pl.BlockSpec((tm, 1), lambda i: (i, 0)), _full((1, d)), row, _full((1, d))]


def _ln_out_specs(s_len, tm, d):
    row = pl.BlockSpec((tm, d), lambda i: (i, 0))
    return ([row, row, _full((SUBLANES, d))],
            [jax.ShapeDtypeStruct((s_len, d), BF16), jax.ShapeDtypeStruct((s_len, d), F32),
             jax.ShapeDtypeStruct((SUBLANES, d), F32)])


def _ln_bwd_rows(dxo, ln_refs, out_refs, alpha):
    xh_ref, rs_ref, gam_ref, y_ref, g_ref = ln_refs
    dy_ref, dres_ref, acc_ref = out_refs
    xh = xh_ref[...]
    dxh = dxo * gam_ref[...]
    m1 = jnp.mean(dxh, axis=-1, keepdims=True)
    m2 = jnp.mean(dxh * xh, axis=-1, keepdims=True)
    dz = rs_ref[...] * (dxh - m1 - xh * m2)
    dy = g_ref[...] * dz
    dy_ref[...] = dy.astype(BF16)
    dres_ref[...] = alpha * dz
    acc_ref[0:1, :] += jnp.sum(dxo * xh, axis=0, keepdims=True)
    acc_ref[1:2, :] += jnp.sum(dxo, axis=0, keepdims=True)
    acc_ref[2:3, :] += jnp.sum(dz * y_ref[...].astype(F32), axis=0, keepdims=True)
    acc_ref[3:4, :] += jnp.sum(dy, axis=0, keepdims=True)


def _mm_nt_row(dy, wg, r, ridx, name):
    s_len, d = dy.shape
    tm = _tile(s_len, 512)

    def body(dy_ref, wg_ref, o_ref, w_ref, sems):
        @pl.when(pl.program_id(0) == 0)
        def _():
            _load_rows(wg_ref, r, ridx, w_ref, sems)

        o_ref[...] = lax.dot_general(dy_ref[...], w_ref[...], NT, preferred_element_type=F32).astype(BF16)

    return pl.pallas_call(
        body, name=name, grid=(s_len // tm,),
        in_specs=[pl.BlockSpec((tm, d), lambda i: (i, 0)), ANY_SPEC],
        out_specs=pl.BlockSpec((tm, NDEV * r), lambda i: (i, 0)),
        out_shape=jax.ShapeDtypeStruct((s_len, NDEV * r), BF16),
        scratch_shapes=[pltpu.VMEM((NDEV * r, d), BF16), pltpu.SemaphoreType.DMA((NDEV,))],
        compiler_params=_cp("arbitrary"),
    )(dy, wg)


def _mm_nt_mod(dos, wg, widxs, xin, sc, dres, name, ln=None, alpha=None):
    s_len, kdim = xin.shape
    ncol = NDEV * wg.shape[-1]
    tm = _tile(s_len, 512)
    nw = len(widxs)
    nln = 0 if ln is None else len(ln)
    nout = 2 if ln is None else 4

    def body(*refs):
        do_refs, wg_ref = refs[:nw], refs[nw]
        x_ref, sc_ref, dres_ref = refs[nw + 1:nw + 4]
        ln_refs = refs[nw + 4:nw + 4 + nln]
        out_refs = refs[nw + 4 + nln:nw + 4 + nln + nout]
        w_refs, sems = refs[nw + 4 + nln + nout:-1], refs[-1]
        acc_ref = out_refs[-1]

        @pl.when(pl.program_id(0) == 0)
        def _():
            for ref in out_refs[nout // 2:]:
                ref[...] = jnp.zeros_like(ref)
            for i, w_ref in enumerate(w_refs):
                _load_cols(wg_ref, widxs[i], w_ref, sems.at[i])

        dh = None
        for do_ref, w_ref in zip(do_refs, w_refs):
            p = lax.dot_general(do_ref[...], w_ref[...], NT, preferred_element_type=F32)
            dh = p if dh is None else dh + p
        dx = dh * (1.0 + sc_ref[...]) + dres_ref[...]
        if ln is None:
            out_refs[0][...] = dx
        else:
            _ln_bwd_rows(dx, ln_refs, out_refs[0:3], alpha)
        acc_ref[0:1, :] += jnp.sum(dh * x_ref[...], axis=0, keepdims=True)
        acc_ref[1:2, :] += jnp.sum(dh, axis=0, keepdims=True)

    row = pl.BlockSpec((tm, kdim), lambda i: (i, 0))
    if ln is None:
        out_specs, out_shape = [row], [jax.ShapeDtypeStruct((s_len, kdim), F32)]
    else:
        out_specs, out_shape = _ln_out_specs(s_len, tm, kdim)
    return pl.pallas_call(
        body, name=name, grid=(s_len // tm,),
        in_specs=[pl.BlockSpec((tm, ncol), lambda i: (i, 0))] * nw + [ANY_SPEC, row, _full((1, kdim)), row]
        + ([] if ln is None else _ln_in_specs(tm, kdim)),
        out_specs=out_specs + [_full((SUBLANES, kdim))],
        out_shape=out_shape + [jax.ShapeDtypeStruct((SUBLANES, kdim), F32)],
        scratch_shapes=[pltpu.VMEM((kdim, ncol), BF16)] * nw + [pltpu.SemaphoreType.DMA((nw, NDEV))],
        compiler_params=_cp("arbitrary"),
    )(*dos, wg, xin, sc, dres, *([] if ln is None else ln))


def _mm_tn_col(x, sc, sh, do, name):
    s_len, kdim = x.shape
    n = do.shape[1] // NDEV
    ts = _tile(s_len, DW_ROWS)
    nsteps = s_len // ts

    def body(x_ref, sc_ref, sh_ref, do_ref, o_ref, acc_ref):
        @pl.when(pl.program_id(0) == 0)
        def _():
            acc_ref[...] = jnp.zeros_like(acc_ref)

        h = (x_ref[...] * (1.0 + sc_ref[...]) + sh_ref[...]).astype(BF16)
        acc_ref[...] += lax.dot_general(h, do_ref[...], TN, preferred_element_type=F32)

        @pl.when(pl.program_id(0) == nsteps - 1)
        def _():
            for k in range(NDEV):
                o_ref[k] = acc_ref[:, k * n:(k + 1) * n].astype(BF16)

    return pl.pallas_call(
        body, name=name, grid=(nsteps,),
        in_specs=[pl.BlockSpec((ts, kdim), lambda i: (i, 0)), _full((1, kdim)), _full((1, kdim)),
                  pl.BlockSpec((ts, NDEV * n), lambda i: (i, 0))],
        out_specs=_full((NDEV, kdim, n)),
        out_shape=jax.ShapeDtypeStruct((NDEV, kdim, n), BF16),
        scratch_shapes=[pltpu.VMEM((kdim, NDEV * n), F32)],
        compiler_params=_cp("arbitrary"),
    )(x, sc, sh, do)


def _mm_tn_col_t(x, sc, sh, do, rows_out, name):
    s_len, kdim = x.shape
    n = do.shape[1] // NDEV
    ts = _tile(s_len, DW_ROWS)
    nsteps = s_len // ts

    def body(x_ref, sc_ref, sh_ref, do_ref, o_ref, acc_ref):
        @pl.when(pl.program_id(0) == 0)
        def _():
            acc_ref[...] = jnp.zeros_like(acc_ref)

        h = (x_ref[...] * (1.0 + sc_ref[...]) + sh_ref[...]).astype(BF16)
        acc_ref[...] += lax.dot_general(do_ref[...], h, TN, preferred_element_type=F32)

        @pl.when(pl.program_id(0) == nsteps - 1)
        def _():
            for k in range(NDEV):
                o_ref[k] = acc_ref[k * n:k * n + rows_out, :].astype(BF16)

    return pl.pallas_call(
        body, name=name, grid=(nsteps,),
        in_specs=[pl.BlockSpec((ts, kdim), lambda i: (i, 0)), _full((1, kdim)), _full((1, kdim)),
                  pl.BlockSpec((ts, NDEV * n), lambda i: (i, 0))],
        out_specs=_full((NDEV, rows_out, kdim)),
        out_shape=jax.ShapeDtypeStruct((NDEV, rows_out, kdim), BF16),
        scratch_shapes=[pltpu.VMEM((NDEV * n, kdim), F32)],
        compiler_params=_cp("arbitrary"),
    )(x, sc, sh, do)


def _mm_tn_row(a, dy, r, rows_out, name):
    s_len, d = dy.shape
    ts = _tile(s_len, DW_ROWS)
    nsteps = s_len // ts

    def body(a_ref, dy_ref, o_ref, acc_ref):
        @pl.when(pl.program_id(0) == 0)
        def _():
            acc_ref[...] = jnp.zeros_like(acc_ref)

        acc_ref[...] += lax.dot_general(a_ref[...], dy_ref[...], TN, preferred_element_type=F32)

        @pl.when(pl.program_id(0) == nsteps - 1)
        def _():
            for k in range(NDEV):
                o_ref[k] = acc_ref[k * r:k * r + rows_out, :].astype(BF16)

    return pl.pallas_call(
        body, name=name, grid=(nsteps,),
        in_specs=[pl.BlockSpec((ts, NDEV * r), lambda i: (i, 0)), pl.BlockSpec((ts, d), lambda i: (i, 0))],
        out_specs=_full((NDEV, rows_out, d)),
        out_shape=jax.ShapeDtypeStruct((NDEV, rows_out, d), BF16),
        scratch_shapes=[pltpu.VMEM((NDEV * r, d), F32)],
        compiler_params=_cp("arbitrary"),
    )(a, dy)


def _prev_spec(ts, pad, cb, col):
    return pl.BlockSpec((pad, cb), lambda *g: (jnp.maximum(g[-1] * (ts // pad) - 1, 0), col(g)))


def _next_spec(ts, pad, cb, col, s_len):
    return pl.BlockSpec((pad, cb), lambda *g: (jnp.minimum((g[-1] + 1) * (ts // pad), s_len // pad - 1), col(g)))


class _F32Loads:
    def __init__(self, ref):
        self.ref = ref

    def __getitem__(self, idx):
        return self.ref[idx].astype(F32)


def _direct(buf_ref):
    return lambda off, rows: buf_ref[off:off + rows, :]


def _make_shifts(sh_ref, nrows):
    for r in range(1, SUBLANES):
        sh_ref[r, 0:nrows - SUBLANES, :] = sh_ref[0, r:r + nrows - SUBLANES, :]


def _shifted(sh_ref):
    def read(off, rows):
        r = off % SUBLANES
        return sh_ref[r, off - r:off - r + rows, :]
    return read


def _conv_fwd_rows(read, w_ref, b_ref, ktaps, pad, r0, rows):
    acc = None
    for j in range(ktaps):
        term = w_ref[ktaps - 1 - j:ktaps - j, :] * read(pad - j + r0, rows)
        acc = term if acc is None else acc + term
    return acc + b_ref[...]


def _conv_bwd_rows(read, x_rows, w_ref, dwacc_ref, ktaps, r0, rows):
    acc = None
    for j in range(ktaps):
        sl = read(j + r0, rows)
        term = w_ref[ktaps - 1 - j:ktaps - j, :] * sl
        acc = term if acc is None else acc + term
        prod = x_rows * sl
        fold = prod[0:SUBLANES]
        for q in range(1, rows // SUBLANES):
            fold = fold + prod[q * SUBLANES:(q + 1) * SUBLANES]
        tap = ktaps - 1 - j
        dwacc_ref[tap * SUBLANES:(tap + 1) * SUBLANES, :] += fold
    return acc


def _flush_dw(dwacc_ref, dw_ref, ktaps):
    for tap in range(ktaps):
        dw_ref[tap:tap + 1, :] = jnp.sum(dwacc_ref[tap * SUBLANES:(tap + 1) * SUBLANES, :], axis=0, keepdims=True)


def _gateconv_fwd(bcv, cw, cb, name):
    s_len, d3 = bcv.shape
    d = d3 // 3
    ktaps = cw.shape[0]
    pad = SHORT_PAD
    ts = _tile(s_len, 256)

    def body(gb_ref, gc_ref, v_ref, gcp_ref, vp_ref, w_ref, b_ref, o_ref, pbuf):
        gb_ref, gc_ref, v_ref, gcp_ref, vp_ref = map(_F32Loads, (gb_ref, gc_ref, v_ref, gcp_ref, vp_ref))
        s = pl.program_id(0)
        pbuf[0:pad, :] = jnp.where(s > 0, gcp_ref[...] * vp_ref[...], 0.0)
        pbuf[pad:pad + ts, :] = gc_ref[...] * v_ref[...]
        for r0 in range(0, ts, CHUNK):
            q = _conv_fwd_rows(_direct(pbuf), w_ref, b_ref, ktaps, pad, r0, CHUNK)
            o_ref[r0:r0 + CHUNK, :] = (gb_ref[r0:r0 + CHUNK, :] * q).astype(BF16)

    def cur(part):
        return pl.BlockSpec((ts, d), lambda s: (s, part))

    return pl.pallas_call(
        body, name=name, grid=(s_len // ts,),
        in_specs=[cur(0), cur(1), cur(2),
                  _prev_spec(ts, pad, d, lambda g: 1), _prev_spec(ts, pad, d, lambda g: 2),
                  _full((ktaps, d)), _full((1, d))],
        out_specs=pl.BlockSpec((ts, d), lambda s: (s, 0)),
        out_shape=jax.ShapeDtypeStruct((s_len, d), BF16),
        scratch_shapes=[pltpu.VMEM((pad + ts, d), F32)],
        compiler_params=_cp("parallel"),
    )(bcv, bcv, bcv, bcv, bcv, cw, cb)


def _gateconv_bwd(bcv, dy0, cw, cb, name):
    s_len, d3 = bcv.shape
    d = d3 // 3
    ktaps = cw.shape[0]
    pad = SHORT_PAD
    ts = _tile(s_len, 256)
    nsteps = s_len // ts

    def body(gb_ref, gc_ref, v_ref, gcp_ref, vp_ref, gbn_ref, dy_ref, dyn_ref, w_ref, b_ref,
             o_ref, dw_ref, db_ref, pbuf, dqbuf, dwacc):
        gb_ref, gc_ref, v_ref, gcp_ref, vp_ref, gbn_ref, dy_ref, dyn_ref = map(
            _F32Loads, (gb_ref, gc_ref, v_ref, gcp_ref, vp_ref, gbn_ref, dy_ref, dyn_ref))
        s = pl.program_id(0)

        @pl.when(s == 0)
        def _():
            dwacc[...] = jnp.zeros_like(dwacc)
            db_ref[...] = jnp.zeros_like(db_ref)

        pbuf[0:pad, :] = jnp.where(s > 0, gcp_ref[...] * vp_ref[...], 0.0)
        pbuf[pad:pad + ts, :] = gc_ref[...] * v_ref[...]
        dq = dy_ref[...] * gb_ref[...]
        dqbuf[0:ts, :] = dq
        dqbuf[ts:ts + pad, :] = jnp.where(s < nsteps - 1, dyn_ref[...] * gbn_ref[...], 0.0)
        db_ref[...] += jnp.sum(dq, axis=0, keepdims=True)
        for r0 in range(0, ts, CHUNK):
            rows = slice(r0, r0 + CHUNK)
            q = _conv_fwd_rows(_direct(pbuf), w_ref, b_ref, ktaps, pad, r0, CHUNK)
            o_ref[rows, 0:d] = (dy_ref[rows, :] * q).astype(BF16)
            dp = _conv_bwd_rows(_direct(dqbuf), pbuf[pad + r0:pad + r0 + CHUNK, :], w_ref, dwacc, ktaps, r0, CHUNK)
            o_ref[rows, d:2 * d] = (dp * v_ref[rows, :]).astype(BF16)
            o_ref[rows, 2 * d:3 * d] = (dp * gc_ref[rows, :]).astype(BF16)

        @pl.when(s == nsteps - 1)
        def _():
            _flush_dw(dwacc, dw_ref, ktaps)

    def cur(part):
        return pl.BlockSpec((ts, d), lambda s: (s, part))

    return pl.pallas_call(
        body, name=name, grid=(nsteps,),
        in_specs=[cur(0), cur(1), cur(2),
                  _prev_spec(ts, pad, d, lambda g: 1), _prev_spec(ts, pad, d, lambda g: 2),
                  _next_spec(ts, pad, d, lambda g: 0, s_len),
                  cur(0), _next_spec(ts, pad, d, lambda g: 0, s_len),
                  _full((ktaps, d)), _full((1, d))],
        out_specs=[pl.BlockSpec((ts, d3), lambda s: (s, 0)), _full((ktaps, d)), _full((1, d))],
        out_shape=[jax.ShapeDtypeStruct((s_len, d3), BF16), jax.ShapeDtypeStruct((ktaps, d), F32),
                   jax.ShapeDtypeStruct((1, d), F32)],
        scratch_shapes=[pltpu.VMEM((pad + ts, d), F32), pltpu.VMEM((ts + pad, d), F32),
                        pltpu.VMEM((ktaps * SUBLANES, d), F32)],
        compiler_params=_cp("arbitrary"),
    )(bcv, bcv, bcv, bcv, bcv, bcv, dy0, dy0, cw, cb)


class _Cols:
    def __init__(self, ref, cols):
        self.ref, self.cols = ref, cols

    def __getitem__(self, idx):
        return self.ref[slice(None) if idx is Ellipsis else idx[0], self.cols]

    def __setitem__(self, idx, value):
        self.ref[idx[0], self.cols] = value


def _ffn_tail_fwd(u0, vg, cw, cb, wg, xres, gate, gam, bet, alpha, name):
    s_len, f = u0.shape
    d = wg.shape[-1]
    r = f // NDEV
    ktaps = cw.shape[0]
    pad = SHORT_PAD
    tm = _tile(s_len, 256)
    cbk = 1024 if f % 1024 == 0 else f

    def body(u_ref, up_ref, vg_ref, cw_ref, cb_ref, wg_ref, x_ref, g_ref, gam_ref, bet_ref,
             t_ref, uc_ref, y_ref, xo_ref, xh_ref, rs_ref, ubuf, w_ref, sems):
        u_ref, up_ref, vg_ref = map(_F32Loads, (u_ref, up_ref, vg_ref))
        s = pl.program_id(0)

        @pl.when(s == 0)
        def _():
            _load_rows(wg_ref, r, 0, w_ref, sems)

        ubuf[0:pad, :] = jnp.where(s > 0, up_ref[...], 0.0)
        ubuf[pad:pad + tm, :] = u_ref[...]
        y = None
        for c0 in range(0, f, cbk):
            cols = slice(c0, c0 + cbk)
            read = _direct(_Cols(ubuf, cols))
            for r0 in range(0, tm, CHUNK):
                rows = slice(r0, r0 + CHUNK)
                u = _conv_fwd_rows(read, _Cols(cw_ref, cols), _Cols(cb_ref, cols), ktaps, pad, r0, CHUNK)
                t_ref[rows, cols] = (u * _sigmoid(u) * vg_ref[rows, cols]).astype(BF16)
                uc_ref[rows, cols] = u.astype(BF16)
            p = jnp.dot(t_ref[:, cols], w_ref[cols, :], preferred_element_type=F32)
            y = p if y is None else y + p
        z = alpha * x_ref[...] + g_ref[...] * y
        mu = jnp.mean(z, axis=-1, keepdims=True)
        zc = z - mu
        var = jnp.mean(zc * zc, axis=-1, keepdims=True)
        rstd = lax.rsqrt(var + LN_EPS)
        xh = zc * rstd
        y_ref[...] = y.astype(BF16)
        xh_ref[...] = xh
        rs_ref[...] = rstd
        xo_ref[...] = xh * gam_ref[...] + bet_ref[...]

    wide = pl.BlockSpec((tm, f), lambda i: (i, 0))
    row = pl.BlockSpec((tm, d), lambda i: (i, 0))
    vec = _full((1, d))
    return pl.pallas_call(
        body, name=name, grid=(s_len // tm,),
        in_specs=[wide, _prev_spec(tm, pad, f, lambda g: 0), wide, _full((ktaps, f)), _full((1, f)), ANY_SPEC,
                  row, vec, vec, vec],
        out_specs=[wide, wide, row, row, row, pl.BlockSpec((tm, 1), lambda i: (i, 0))],
        out_shape=[jax.ShapeDtypeStruct((s_len, f), BF16), jax.ShapeDtypeStruct((s_len, f), BF16),
                   jax.ShapeDtypeStruct((s_len, d), BF16),
                   jax.ShapeDtypeStruct((s_len, d), F32), jax.ShapeDtypeStruct((s_len, d), F32),
                   jax.ShapeDtypeStruct((s_len, 1), F32)],
        scratch_shapes=[pltpu.VMEM((pad + tm, f), F32), pltpu.VMEM((f, d), BF16), pltpu.SemaphoreType.DMA((NDEV,))],
        compiler_params=_cp("arbitrary"),
    )(u0, u0, vg, cw, cb, wg, xres, gate, gam, bet)


def _ffn_core_bwd(dy, u0, uc, vg, cw, wg_row, wg_col, xin, sc, dres, ln, alpha, name):
    s_len, f = u0.shape
    d = xin.shape[1]
    r = f // NDEV
    ktaps = cw.shape[0]
    pad = SHORT_PAD
    tm = _tile(s_len, 256)
    nsteps = s_len // tm
    cbk = 1024 if f % 1024 == 0 else f

    def body(dy_ref, dyn_ref, u_ref, uc_ref, ucn_ref, vg_ref, vgn_ref, cw_ref, wgr_ref, wgc_ref,
             x_ref, sc_ref, dres_ref, xh_ref, rs_ref, gam_ref, y_ref, g_ref,
             du0_ref, dvg_ref, dw_ref, db_ref, dyo_ref, dreso_ref, lnacc_ref, acc_ref,
             dtbuf, dubuf, dwacc, wd_ref, wup_ref, wgate_ref, sems):
        u_ref, uc_ref, ucn_ref, vg_ref, vgn_ref = map(_F32Loads, (u_ref, uc_ref, ucn_ref, vg_ref, vgn_ref))
        s = pl.program_id(0)
        last = s == nsteps - 1

        @pl.when(s == 0)
        def _():
            dwacc[...] = jnp.zeros_like(dwacc)
            db_ref[...] = jnp.zeros_like(db_ref)
            acc_ref[...] = jnp.zeros_like(acc_ref)
            lnacc_ref[...] = jnp.zeros_like(lnacc_ref)
            _load_rows(wgr_ref, r, 0, wd_ref, sems.at[0])
            _load_cols(wgc_ref, 0, wup_ref, sems.at[1])
            _load_cols(wgc_ref, 1, wgate_ref, sems.at[2])

        dy_cur, dy_nxt = dy_ref[...], dyn_ref[...]
        dh = None
        for c0 in range(0, f, cbk):
            cols = slice(c0, c0 + cbk)
            wd_blk = wd_ref[cols, :]
            dtbuf[0:tm, :] = lax.dot_general(dy_cur, wd_blk, NT, preferred_element_type=F32)
            dtbuf[tm:tm + pad, :] = jnp.where(
                last, 0.0, lax.dot_general(dy_nxt, wd_blk, NT, preferred_element_type=F32))
            for r0 in range(0, tm + pad, CHUNK):
                dtr = dtbuf[r0:r0 + CHUNK, :]
                if r0 < tm:
                    u, vgr = uc_ref[r0:r0 + CHUNK, cols], vg_ref[r0:r0 + CHUNK, cols]
                    sg = _sigmoid(u)
                    dvg_ref[r0:r0 + CHUNK, cols] = (dtr * u * sg).astype(BF16)
                else:
                    u, vgr = ucn_ref[r0 - tm:r0 - tm + CHUNK, cols], vgn_ref[r0 - tm:r0 - tm + CHUNK, cols]
                    sg = _sigmoid(u)
                dubuf[r0:r0 + CHUNK, :] = dtr * vgr * (sg * (1.0 + u * (1.0 - sg)))
            db_ref[:, cols] += jnp.sum(dubuf[0:tm, :], axis=0, keepdims=True)
            for r0 in range(0, tm, CHUNK):
                du0 = _conv_bwd_rows(_direct(dubuf), u_ref[r0:r0 + CHUNK, cols], _Cols(cw_ref, cols),
                                     _Cols(dwacc, cols), ktaps, r0, CHUNK)
                du0_ref[r0:r0 + CHUNK, cols] = du0.astype(BF16)
            p = (lax.dot_general(du0_ref[:, cols], wup_ref[:, cols], NT, preferred_element_type=F32)
                 + lax.dot_general(dvg_ref[:, cols], wgate_ref[:, cols], NT, preferred_element_type=F32))
            dh = p if dh is None else dh + p
        dx = dh * (1.0 + sc_ref[...]) + dres_ref[...]
        _ln_bwd_rows(dx, (xh_ref, rs_ref, gam_ref, y_ref, g_ref), (dyo_ref, dreso_ref, lnacc_ref), alpha)
        acc_ref[0:1, :] += jnp.sum(dh * x_ref[...], axis=0, keepdims=True)
        acc_ref[1:2, :] += jnp.sum(dh, axis=0, keepdims=True)

        @pl.when(last)
        def _():
            _flush_dw(dwacc, dw_ref, ktaps)

    wide = pl.BlockSpec((tm, f), lambda i: (i, 0))
    row = pl.BlockSpec((tm, d), lambda i: (i, 0))
    ln_out_specs, ln_out_shape = _ln_out_specs(s_len, tm, d)
    return pl.pallas_call(
        body, name=name, grid=(nsteps,),
        in_specs=[row, _next_spec(tm, pad, d, lambda g: 0, s_len),
                  wide, wide, _next_spec(tm, pad, f, lambda g: 0, s_len),
                  wide, _next_spec(tm, pad, f, lambda g: 0, s_len),
                  _full((ktaps, f)), ANY_SPEC, ANY_SPEC, row, _full((1, d)), row]
        + _ln_in_specs(tm, d),
        out_specs=[wide, wide, _full((ktaps, f)), _full((1, f))] + ln_out_specs + [_full((SUBLANES, d))],
        out_shape=[jax.ShapeDtypeStruct((s_len, f), BF16), jax.ShapeDtypeStruct((s_len, f), BF16),
                   jax.ShapeDtypeStruct((ktaps, f), F32), jax.ShapeDtypeStruct((1, f), F32)]
        + ln_out_shape + [jax.ShapeDtypeStruct((SUBLANES, d), F32)],
        scratch_shapes=[pltpu.VMEM((tm + pad, cbk), F32),
                        pltpu.VMEM((tm + pad, cbk), F32), pltpu.VMEM((ktaps * SUBLANES, f), F32),
                        pltpu.VMEM((f, d), BF16), pltpu.VMEM((d, f), BF16), pltpu.VMEM((d, f), BF16),
                        pltpu.SemaphoreType.DMA((3, NDEV))],
        compiler_params=_cp("arbitrary"),
    )(dy, dy, u0, uc, uc, vg, vg, cw, wg_row, wg_col, xin, sc, dres, *ln)


def _b_mid_fwd(ub, cw, cb, lng, lnb, name):
    s_len, d2 = ub.shape
    d = d2 // 2
    ktaps = cw.shape[0]
    pad = LONG_PAD
    ts = _tile(s_len, 256)

    def body(a_ref, g_ref, ap_ref, gp_ref, w_ref, b_ref, lng_ref, lnb_ref, a2_ref, a4_ref, abuf):
        a_ref, g_ref, ap_ref, gp_ref = map(_F32Loads, (a_ref, g_ref, ap_ref, gp_ref))
        s = pl.program_id(0)
        abuf[0, 0:pad, :] = jnp.where(s > 0, ap_ref[...] * _sigmoid(gp_ref[...]), 0.0)
        abuf[0, pad:pad + ts, :] = a_ref[...] * _sigmoid(g_ref[...])
        _make_shifts(abuf, pad + ts)
        for r0 in range(0, ts, CHUNK):
            a2_ref[r0:r0 + CHUNK, :] = _conv_fwd_rows(_shifted(abuf), w_ref, b_ref, ktaps, pad, r0, CHUNK)
        a2 = a2_ref[...]
        mu = jnp.mean(a2, axis=-1, keepdims=True)
        ac = a2 - mu
        var = jnp.mean(ac * ac, axis=-1, keepdims=True)
        a3 = ac * lax.rsqrt(var + LN_EPS) * lng_ref[...] + lnb_ref[...]
        a4_ref[...] = (a3 * _sigmoid(a3)).astype(BF16)

    def cur(part):
        return pl.BlockSpec((ts, d), lambda s: (s, part))

    vec = _full((1, d))
    return pl.pallas_call(
        body, name=name, grid=(s_len // ts,),
        in_specs=[cur(0), cur(1), _prev_spec(ts, pad, d, lambda g: 0), _prev_spec(ts, pad, d, lambda g: 1),
                  _full((ktaps, d)), vec, vec, vec],
        out_specs=[cur(0), cur(0)],
        out_shape=[jax.ShapeDtypeStruct((s_len, d), F32), jax.ShapeDtypeStruct((s_len, d), BF16)],
        scratch_shapes=[pltpu.VMEM((SUBLANES, pad + ts, d), F32)],
        compiler_params=_cp("parallel"),
    )(ub, ub, ub, ub, cw, cb, lng, lnb)


def _b_mid_bwd(ub, a2, da4, cw, lng, lnb, name):
    s_len, d2 = ub.shape
    d = d2 // 2
    ktaps = cw.shape[0]
    pad = LONG_PAD
    ts = _tile(s_len, 256)
    nsteps = s_len // ts

    def body(a_ref, g_ref, a2_ref, a2n_ref, da4_ref, da4n_ref, w_ref, lng_ref, lnb_ref,
             du_ref, dw_ref, db_ref, dlng_ref, dlnb_ref, dbias_ref, dabuf, dwacc):
        a_ref, g_ref, da4_ref, da4n_ref = map(_F32Loads, (a_ref, g_ref, da4_ref, da4n_ref))
        s = pl.program_id(0)
        last = s == nsteps - 1

        @pl.when(s == 0)
        def _():
            dwacc[...] = jnp.zeros_like(dwacc)
            for ref in (db_ref, dlng_ref, dlnb_ref, dbias_ref):
                ref[...] = jnp.zeros_like(ref)

        def ln_silu_bwd(a2_t, da4_t):
            mu = jnp.mean(a2_t, axis=-1, keepdims=True)
            ac = a2_t - mu
            var = jnp.mean(ac * ac, axis=-1, keepdims=True)
            rstd = lax.rsqrt(var + LN_EPS)
            ah = ac * rstd
            a3 = ah * lng_ref[...] + lnb_ref[...]
            sg = _sigmoid(a3)
            da3 = da4_t * (sg * (1.0 + a3 * (1.0 - sg)))
            dah = da3 * lng_ref[...]
            m1 = jnp.mean(dah, axis=-1, keepdims=True)
            m2 = jnp.mean(dah * ah, axis=-1, keepdims=True)
            return rstd * (dah - m1 - ah * m2), da3, ah

        da2, da3, ah = ln_silu_bwd(a2_ref[...], da4_ref[...])
        dabuf[0, 0:ts, :] = da2
        dlng_ref[...] += jnp.sum(da3 * ah, axis=0, keepdims=True)
        dlnb_ref[...] += jnp.sum(da3, axis=0, keepdims=True)
        db_ref[...] += jnp.sum(da2, axis=0, keepdims=True)
        da2n, _, _ = ln_silu_bwd(a2n_ref[...], jnp.where(last, 0.0, da4n_ref[...]))
        dabuf[0, ts:ts + pad, :] = da2n
        _make_shifts(dabuf, ts + pad)
        for r0 in range(0, ts, CHUNK):
            rows = slice(r0, r0 + CHUNK)
            a_r, g_r = a_ref[rows, :], g_ref[rows, :]
            sg = _sigmoid(g_r)
            da1 = _conv_bwd_rows(_shifted(dabuf), a_r * sg, w_ref, dwacc, ktaps, r0, CHUNK)
            da = da1 * sg
            dg = da1 * a_r * sg * (1.0 - sg)
            du_ref[rows, 0:d] = da.astype(BF16)
            du_ref[rows, d:2 * d] = dg.astype(BF16)
            dbias_ref[:, 0:d] += jnp.sum(da, axis=0, keepdims=True)
            dbias_ref[:, d:2 * d] += jnp.sum(dg, axis=0, keepdims=True)

        @pl.when(last)
        def _():
            _flush_dw(dwacc, dw_ref, ktaps)

    def cur(part):
        return pl.BlockSpec((ts, d), lambda s: (s, part))

    vec = _full((1, d))
    nxt = _next_spec(ts, pad, d, lambda g: 0, s_len)
    return pl.pallas_call(
        body, name=name, grid=(nsteps,),
        in_specs=[cur(0), cur(1), cur(0), nxt, cur(0), nxt, _full((ktaps, d)), vec, vec],
        out_specs=[pl.BlockSpec((ts, d2), lambda s: (s, 0)), _full((ktaps, d)), vec, vec, vec, _full((1, d2))],
        out_shape=[jax.ShapeDtypeStruct((s_len, d2), BF16), jax.ShapeDtypeStruct((ktaps, d), F32),
                   jax.ShapeDtypeStruct((1, d), F32), jax.ShapeDtypeStruct((1, d), F32),
                   jax.ShapeDtypeStruct((1, d), F32), jax.ShapeDtypeStruct((1, d2), F32)],
        scratch_shapes=[pltpu.VMEM((SUBLANES, ts + pad, d), F32), pltpu.VMEM((ktaps * SUBLANES, d), F32)],
        compiler_params=_cp("arbitrary"),
    )(ub, ub, a2, a2, da4, da4, cw, lng, lnb)


def _loss_head(xo, tgt, ln, alpha, name):
    s_len, d = xo.shape
    tm = _tile(s_len, 512)

    def body(x_ref, t_ref, xh_ref, rs_ref, gam_ref, y_ref, g_ref, dy_ref, dres_ref, acc_ref, l_ref):
        @pl.when(pl.program_id(0) == 0)
        def _():
            l_ref[...] = jnp.zeros_like(l_ref)
            acc_ref[...] = jnp.zeros_like(acc_ref)

        e = x_ref[...] - t_ref[...]
        per_row = jnp.sum(e * e, axis=-1, keepdims=True) * (1.0 / d)
        l_ref[...] += 0.5 * jnp.sum(per_row, axis=0, keepdims=True)
        _ln_bwd_rows(e * (1.0 / d), (xh_ref, rs_ref, gam_ref, y_ref, g_ref), (dy_ref, dres_ref, acc_ref), alpha)

    row = pl.BlockSpec((tm, d), lambda i: (i, 0))
    ln_out_specs, ln_out_shape = _ln_out_specs(s_len, tm, d)
    return pl.pallas_call(
        body, name=name, grid=(s_len // tm,),
        in_specs=[row, row] + _ln_in_specs(tm, d), out_specs=ln_out_specs + [_full((1, LANES))],
        out_shape=ln_out_shape + [jax.ShapeDtypeStruct((1, LANES), F32)],
        compiler_params=_cp("arbitrary"),
    )(xo, tgt, *ln)


def _ada_fwd(c_all, ada_w, ada_b_loc, name):
    depth, d, n = ada_w.shape

    def body(c_ref, w_ref, b_ref, o_ref):
        c = c_ref[...]
        act = c * _sigmoid(c)
        o_ref[...] = jnp.dot(act, w_ref[...], preferred_element_type=F32,
                             precision=lax.Precision.HIGHEST) + b_ref[...]

    return pl.pallas_call(
        body, name=name, grid=(depth,),
        in_specs=[_full((NDEV, d)), pl.BlockSpec((None, d, n), lambda i: (i, 0, 0)),
                  pl.BlockSpec((None, 1, n), lambda i: (i, 0, 0))],
        out_specs=pl.BlockSpec((None, NDEV, n), lambda i: (i, 0, 0)),
        out_shape=jax.ShapeDtypeStruct((depth, NDEV, n), F32),
        compiler_params=_cp("parallel"),
    )(c_all, ada_w, ada_b_loc.reshape(depth, 1, n))


def _ada_bwd(c_all_t, dmod_cols, name):
    depth, _, n = dmod_cols.shape
    d = c_all_t.shape[0]

    def body(ct_ref, dm_ref, o_ref):
        ct = ct_ref[...]
        act = ct * _sigmoid(ct)
        acc = None
        for b in range(NDEV):
            term = act[:, b:b + 1] * dm_ref[b:b + 1, :]
            acc = term if acc is None else acc + term
        o_ref[...] = acc

    return pl.pallas_call(
        body, name=name, grid=(depth,),
        in_specs=[_full((d, NDEV)), pl.BlockSpec((None, NDEV, n), lambda i: (i, 0, 0))],
        out_specs=pl.BlockSpec((None, d, n), lambda i: (i, 0, 0)),
        out_shape=jax.ShapeDtypeStruct((depth, d, n), F32),
        compiler_params=_cp("parallel"),
    )(c_all_t, dmod_cols)


def _sum_parts(parts, name):
    _, rows, lanes = parts.shape

    def body(p_ref, o_ref):
        acc = p_ref[0]
        for k in range(1, NDEV):
            acc = acc + p_ref[k]
        o_ref[...] = acc

    return pl.pallas_call(
        body, name=name, in_specs=[_full(parts.shape)], out_specs=_full((rows, lanes)), grid=(1,),
        out_shape=jax.ShapeDtypeStruct((rows, lanes), F32), compiler_params=_cp("arbitrary"),
    )(parts)


def _adamw(w, glist, m, v, name):
    nl, rows, cols = w.shape
    tr = _tile(rows, 256, 2 * SUBLANES)

    def body(w_ref, *rest):
        g_refs = rest[:nl]
        m_ref, v_ref, go_ref, d_ref, mo_ref, vo_ref = rest[nl:]
        g = None
        for layer, g_ref in enumerate(g_refs):
            part = g_ref[0].astype(F32)
            for p in range(1, g_ref.shape[0]):
                part = part + g_ref[p].astype(F32)
            g = part if g is None else jnp.where(pl.program_id(0) == layer, part, g)
        go_ref[...] = g
        d_ref[...], mo_ref[...], vo_ref[...] = _adam_step(w_ref[...], g, m_ref[...], v_ref[...])

    blk = pl.BlockSpec((None, tr, cols), lambda l, i: (l, i, 0))
    g_specs = [pl.BlockSpec((g.shape[0], tr, cols), lambda l, i: (0, i, 0)) for g in glist]
    return pl.pallas_call(
        body, name=name, grid=(nl, rows // tr),
        in_specs=[blk] + g_specs + [blk, blk],
        out_specs=[blk] * 4, out_shape=[jax.ShapeDtypeStruct((nl, rows, cols), F32)] * 4,
        compiler_params=_cp("parallel", "parallel"),
    )(w, *glist, m, v)


def _adam_step(w, g, m, v):
    m1 = ADAM_B1 * m + (1.0 - ADAM_B1) * g
    v1 = ADAM_B2 * v + (1.0 - ADAM_B2) * (g * g)
    m_hat = m1 / (1.0 - ADAM_B1 ** ADAM_STEP)
    v_hat = v1 / (1.0 - ADAM_B2 ** ADAM_STEP)
    return -ADAM_LR * (m_hat / (jnp.sqrt(v_hat) + ADAM_EPS) + ADAM_WD * w), m1, v1


def _adamw_small(ws, gs, ms, vs, name):
    n = len(ws)

    def body(*refs):
        ins, outs = refs[:4 * n], refs[4 * n:]
        for i in range(n):
            w_ref, g_ref, m_ref, v_ref = ins[i], ins[n + i], ins[2 * n + i], ins[3 * n + i]
            delta, m1, v1 = _adam_step(w_ref[...], g_ref[...], m_ref[...], v_ref[...])
            outs[3 * i][...] = delta
            outs[3 * i + 1][...] = m1
            outs[3 * i + 2][...] = v1

    operands = list(ws) + list(gs) + list(ms) + list(vs)
    out_shape = [jax.ShapeDtypeStruct(w.shape, F32) for w in ws for _ in range(3)]
    return pl.pallas_call(
        body, name=name, grid=(1,), in_specs=[_full(a.shape) for a in operands],
        out_specs=[_full(s.shape) for s in out_shape], out_shape=out_shape,
        compiler_params=_cp("arbitrary"),
    )(*operands)


def _pack(pieces):
    flat = jnp.concatenate([p.reshape(-1) for p in pieces])
    unit = SUBLANES * LANES
    padded = -(-flat.shape[0] // unit) * unit
    return jnp.pad(flat, (0, padded - flat.shape[0])).reshape(padded // LANES, LANES)


def _unpack(packed, shapes, lead=()):
    flat = packed.reshape(lead + (-1,))
    out, off = [], 0
    for s in shapes:
        size = 1
        for dim in s:
            size *= dim
        out.append(flat[..., off:off + size].reshape(lead + tuple(s)))
        off += size
    return out


def _pad_last(a, n):
    return jnp.pad(a, [(0, 0)] * (a.ndim - 1) + [(0, n - a.shape[-1])])


def kernel(x, c, ada_w, ada_b, ln_tok_g, ln_tok_b, ln_ch_g, ln_ch_b, a_w_in, a_conv_w, a_conv_b, a_w_out, b_w_pw1, b_b_pw1, b_conv_w, b_conv_b, b_ln_g, b_ln_b, b_w_pw2, b_b_pw2, f_w_up, f_conv_w, f_conv_b, f_w_gate, f_w_down, loss_target, m_ada_w, m_ada_b, m_ln_tok_g, m_ln_tok_b, m_ln_ch_g, m_ln_ch_b, m_a_w_in, m_a_conv_w, m_a_conv_b, m_a_w_out, m_b_w_pw1, m_b_b_pw1, m_b_conv_w, m_b_conv_b, m_b_ln_g, m_b_ln_b, m_b_w_pw2, m_b_b_pw2, m_f_w_up, m_f_conv_w, m_f_conv_b, m_f_w_gate, m_f_w_down, v_ada_w, v_ada_b, v_ln_tok_g, v_ln_tok_b, v_ln_ch_g, v_ln_ch_b, v_a_w_in, v_a_conv_w, v_a_conv_b, v_a_w_out, v_b_w_pw1, v_b_b_pw1, v_b_conv_w, v_b_conv_b, v_b_ln_g, v_b_ln_b, v_b_w_pw2, v_b_b_pw2, v_f_w_up, v_f_conv_w, v_f_conv_b, v_f_w_gate, v_f_w_down):
    weights = dict(ada_w=ada_w, ada_b=ada_b, ln_tok_g=ln_tok_g, ln_tok_b=ln_tok_b, ln_ch_g=ln_ch_g, ln_ch_b=ln_ch_b, a_w_in=a_w_in, a_conv_w=a_conv_w, a_conv_b=a_conv_b, a_w_out=a_w_out, b_w_pw1=b_w_pw1, b_b_pw1=b_b_pw1, b_conv_w=b_conv_w, b_conv_b=b_conv_b, b_ln_g=b_ln_g, b_ln_b=b_ln_b, b_w_pw2=b_w_pw2, b_b_pw2=b_b_pw2, f_w_up=f_w_up, f_conv_w=f_conv_w, f_conv_b=f_conv_b, f_w_gate=f_w_gate, f_w_down=f_w_down)
    mom_m = dict(ada_w=m_ada_w, ada_b=m_ada_b, ln_tok_g=m_ln_tok_g, ln_tok_b=m_ln_tok_b, ln_ch_g=m_ln_ch_g, ln_ch_b=m_ln_ch_b, a_w_in=m_a_w_in, a_conv_w=m_a_conv_w, a_conv_b=m_a_conv_b, a_w_out=m_a_w_out, b_w_pw1=m_b_w_pw1, b_b_pw1=m_b_b_pw1, b_conv_w=m_b_conv_w, b_conv_b=m_b_conv_b, b_ln_g=m_b_ln_g, b_ln_b=m_b_ln_b, b_w_pw2=m_b_w_pw2, b_b_pw2=m_b_b_pw2, f_w_up=m_f_w_up, f_conv_w=m_f_conv_w, f_conv_b=m_f_conv_b, f_w_gate=m_f_w_gate, f_w_down=m_f_w_down)
    mom_v = dict(ada_w=v_ada_w, ada_b=v_ada_b, ln_tok_g=v_ln_tok_g, ln_tok_b=v_ln_tok_b, ln_ch_g=v_ln_ch_g, ln_ch_b=v_ln_ch_b, a_w_in=v_a_w_in, a_conv_w=v_a_conv_w, a_conv_b=v_a_conv_b, a_w_out=v_a_w_out, b_w_pw1=v_b_w_pw1, b_b_pw1=v_b_b_pw1, b_conv_w=v_b_conv_w, b_conv_b=v_b_conv_b, b_ln_g=v_b_ln_g, b_ln_b=v_b_ln_b, b_w_pw2=v_b_w_pw2, b_b_pw2=v_b_b_pw2, f_w_up=v_f_w_up, f_conv_w=v_f_conv_w, f_conv_b=v_f_conv_b, f_w_gate=v_f_w_gate, f_w_down=v_f_w_down)
    names = list(weights)

    depth, d, n_ada = ada_w.shape
    assert depth == 2 and a_w_in.shape[0] == 1 and b_w_pw1.shape[0] == 1
    s_len = x.shape[1]
    f_loc = f_w_up.shape[-1]
    f_pad = -(-f_loc // LANES) * LANES
    f_all = NDEV * f_pad
    d_loc = d // NDEV
    ka, kb, kf = a_conv_w.shape[1], b_conv_w.shape[1], f_conv_w.shape[1]
    alpha = (2.0 * depth) ** 0.25
    assert a_w_in.shape[-1] == f_pad and f_pad % d_loc == 0
    me = 4 * lax.axis_index("x") + 2 * lax.axis_index("y") + lax.axis_index("c")

    small_shapes = [(d,), (ka, d_loc), (2 * d_loc,), (kb, d_loc), (d_loc,), (d_loc,), (d_loc,), (d_loc,),
                    (depth, kf, f_pad)]
    small_loc = _pack([c[0], a_conv_w[0], b_b_pw1[0], b_conv_w[0], b_conv_b[0], b_ln_g[0], b_ln_b[0],
                       b_b_pw2[0], _pad_last(f_conv_w, f_pad)])
    g_small, g_in, _ = _gather_two_level([small_loc, a_w_in.astype(BF16)], small_loc, "gather_first")

    (c_all, acw_g, bb1_g, bcw_g, bcb_g, blg_g, blb_g, bb2_g, fcw_g) = _unpack(g_small, small_shapes, (NDEV,))
    a_cw = acw_g.transpose(1, 0, 2).reshape(ka, d)
    b_cw = bcw_g.transpose(1, 0, 2).reshape(kb, d)
    b_b1 = bb1_g.reshape(1, 2 * d)
    b_cb, b_lg, b_lb, b_b2 = (t.reshape(1, d) for t in (bcb_g, blg_g, blb_g, bb2_g))
    f_cw = fcw_g.transpose(1, 2, 0, 3).reshape(depth, kf, f_all)
    f_cb = _pad_last(f_conv_b.reshape(depth, NDEV, f_loc), f_pad).reshape(depth, 1, f_all)

    ada_b_loc = lax.dynamic_slice(ada_b, (0, me * n_ada), (depth, n_ada))
    mod_part = _ada_fwd(c_all, ada_w, ada_b_loc, "ada_fwd")
    mod_g, mod_done = _exchange([mod_part.reshape(depth * NDEV, n_ada)], "gather", "gather_mod")
    mod_all = mod_g.reshape(NDEV, depth, NDEV, n_ada).transpose(1, 2, 0, 3).reshape(depth, NDEV, 6 * d)
    mod = lax.dynamic_slice(mod_all, (0, me, 0), (depth, 1, 6 * d))[:, 0]

    gather_out = _exchange_start([_after(a_w_out[0], mod_done).astype(BF16)], "gather_chips", "gather_out_start")
    up_pad = _pad_last(_after(f_w_up, gather_out[-1]), f_pad).astype(BF16)
    gate_pad = _pad_last(f_w_gate, f_pad).astype(BF16)
    down_pad = jnp.pad(f_w_down, ((0, 0), (0, f_pad - f_loc), (0, 0))).astype(BF16)
    col_f = [jnp.stack([up_pad[i], gate_pad[i]]) for i in range(depth)]
    row_b = jnp.concatenate([down_pad[1], b_w_pw2[0].astype(BF16)], axis=0)
    ridx_pw2 = f_pad // d_loc
    gather_f0 = _exchange_start([col_f[0], down_pad[0]], "gather_chips", "gather_f0_start")

    def mod_rows(i):
        return [mod[i:i + 1, j * d:(j + 1) * d] for j in range(6)]

    zeros_d = jnp.zeros((1, d), F32)
    zeros_f = jnp.zeros((1, f_all), F32)
    x0 = x[0]

    sh_t0, sc_t0, g_t0, sh_c0, sc_c0, g_c0 = mod_rows(0)
    sh_t1, sc_t1, g_t1, sh_c1, sc_c1, g_c1 = mod_rows(1)

    sc_t0 = _after(sc_t0, gather_f0[-1])
    bcv, = _mm_fwd(x0, sc_t0, sh_t0, jnp.zeros((1, 3 * d), F32), g_in, (0,), "a_in_fwd")
    y0 = _gateconv_fwd(bcv, a_cw, a_conv_b, "a_conv_fwd")
    g_out, landed = _exchange_wait(gather_out, y0, "gather_chips", "gather_out_wait")
    g_out, _ = _exchange_wait(_exchange_start([g_out], "forward", "gather_out_fwd_start"), landed, "forward",
                              "gather_out_fwd_wait")
    y_a, x1, xh1, rs1 = _mm_ln(y0, g_out, d_loc, 0, x0, g_t0, ln_tok_g[0:1], ln_tok_b[0:1], zeros_d,
                               alpha, "a_out_ln_fwd")

    def ffn_fwd(xin, sc, sh, gate, gam, bet, g_colf, g_rowf, layer, tag):
        u0, vg = _mm_fwd(xin, sc, sh, zeros_f, g_colf, (0, 1), "f_upgate_fwd" + tag)
        t, uc, y, xo, xh, rs = _ffn_tail_fwd(u0, vg, f_cw[layer], f_cb[layer], g_rowf, xin, gate, gam, bet, alpha,
                                             "f_tail_fwd" + tag)
        return (u0, uc), vg, t, y, xo, xh, rs

    g_colf0, g_rowf0, landed = _exchange_wait(gather_f0, x1, "gather_chips", "gather_f0_wait")
    g_colf0, g_rowf0, landed = _exchange_wait(
        _exchange_start([g_colf0, g_rowf0], "forward", "gather_f0_fwd_start"), landed, "forward", "gather_f0_fwd_wait")
    gather_1 = _exchange_start([_after(b_w_pw1, landed).astype(BF16), col_f[1], row_b], "gather_chips",
                               "gather_1_start")
    sc_c0 = _after(sc_c0, gather_1[-1])
    u0_0, vg_0, t_0, y_f0, x2, xh2, rs2 = ffn_fwd(x1, sc_c0, sh_c0, g_c0, ln_ch_g[0:1], ln_ch_b[0:1],
                                                  g_colf0, g_rowf0, 0, "0")

    *lands_1, landed = _exchange_wait(gather_1, x2, "gather_chips", "gather_1_wait")
    g_pw1, g_colf1, g_rowb, _ = _exchange_wait(_exchange_start(lands_1, "forward", "gather_1_fwd_start"), landed,
                                                 "forward", "gather_1_fwd_wait")
    ub, = _mm_fwd(x2, sc_t1, sh_t1, b_b1, g_pw1, (0,), "b_pw1_fwd")
    a2, a4 = _b_mid_fwd(ub, b_cw, b_cb, b_lg, b_lb, "b_mid_fwd")
    y_b, x3, xh3, rs3 = _mm_ln(a4, g_rowb, d_loc, ridx_pw2, x2, g_t1, ln_tok_g[1:2], ln_tok_b[1:2], b_b2,
                               alpha, "b_pw2_ln_fwd")
    u0_1, vg_1, t_1, y_f1, x4, xh4, rs4 = ffn_fwd(x3, sc_c1, sh_c1, g_c1, ln_ch_g[1:2], ln_ch_b[1:2],
                                                  g_colf1, g_rowb, 1, "1")

    ln_f1 = (xh4, rs4, ln_ch_g[1:2], y_f1, g_c1)
    ln_b = (xh3, rs3, ln_tok_g[1:2], y_b, g_t1)
    ln_f0 = (xh2, rs2, ln_ch_g[0:1], y_f0, g_c0)
    ln_a = (xh1, rs1, ln_tok_g[0:1], y_a, g_t0)
    dy, dres, accf1, loss_part = _loss_head(x4, loss_target[0], ln_f1, alpha, "loss_head")

    def ffn_bwd(dy, dres, xin, sc, sh, u0, vg, t, g_colf, g_rowf, ln_below, layer, tag):
        dw_down = _mm_tn_row(t, dy, f_pad, f_loc, "f_down_dw" + tag)
        scatter_down = _exchange_start([dw_down], "scatter", "scatter_d%s_start" % tag)
        du0, dvg, dcw, dcb, dy_below, dres_below, acc_below, acc2 = _ffn_core_bwd(
            dy, u0[0], u0[1], vg, f_cw[layer], g_rowf, g_colf, xin, _after(sc, scatter_down[-1]), dres,
            ln_below, alpha, "f_core_bwd" + tag)
        dw_up = _mm_tn_col_t(xin, sc, sh, du0, f_loc, "f_up_dw" + tag)
        dw_gate = _mm_tn_col_t(xin, sc, sh, dvg, f_loc, "f_gate_dw" + tag)
        scatter = _exchange_start([dw_up, dw_gate], "scatter", "scatter_f%s_start" % tag)
        return dy_below, dres_below, acc_below, acc2, (scatter, scatter_down), dcw, dcb

    dy, dres, accb, acc2f1, (scatter_f1, scatter_d1), dfcw1, dfcb1 = ffn_bwd(
        dy, dres, x3, sc_c1, sh_c1, u0_1, vg_1, t_1, g_colf1, g_rowb, ln_b, 1, "1")

    da4 = _mm_nt_row(dy, g_rowb, d_loc, ridx_pw2, "b_pw2_dx")
    dw_pw2 = _mm_tn_row(a4, dy, d_loc, d_loc, "b_pw2_dw")
    du, dbcw, dbcb, dblg, dblb, dbb1 = _b_mid_bwd(ub, a2, da4, b_cw, _after(b_lg, scatter_f1[-1]), b_lb, "b_mid_bwd")
    dw_pw1 = _mm_tn_col(x2, sc_t1, sh_t1, du, "b_pw1_dw")
    scatter_b = _exchange_start([dw_pw1, dw_pw2], "scatter", "scatter_b_start")
    dy, dres, accf0, acc2b = _mm_nt_mod([du], g_pw1, (0,), x2, _after(sc_t1, scatter_b[-1]), dres, "b_pw1_dx",
                                        ln=ln_f0, alpha=alpha)

    dy, dres, acca, acc2f0, (scatter_f0, scatter_d0), dfcw0, dfcb0 = ffn_bwd(
        dy, dres, x1, sc_c0, sh_c0, u0_0, vg_0, t_0, g_colf0, g_rowf0, ln_a, 0, "0")

    dy0 = _mm_nt_row(dy, g_out, d_loc, 0, "a_out_dx")
    dbcv, dacw, dacb = _gateconv_bwd(bcv, dy0, a_cw, _after(a_conv_b, scatter_f0[-1]), "a_conv_bwd")
    dx0, acc2a = _mm_nt_mod([dbcv], g_in, (0,), x0, sc_t0, dres, "a_in_dx")

    def dmod_row(acc2_t, acc_t, acc2_c, acc_c):
        return jnp.concatenate([acc2_t[1], acc2_t[0], acc_t[2], acc2_c[1], acc2_c[0], acc_c[2]])

    dmod = jnp.stack([dmod_row(acc2a, acca, acc2f0, accf0), dmod_row(acc2b, accb, acc2f1, accf1)])

    def unpad_f(a):
        return a.reshape(a.shape[:-1] + (NDEV, f_pad))[..., :f_loc].reshape(a.shape[:-1] + (NDEV * f_loc,))

    small_grads = [
        dmod,
        jnp.stack([acca[0], accb[0]]), jnp.stack([acca[1], accb[1]]),
        jnp.stack([accf0[0], accf1[0]]), jnp.stack([accf0[1], accf1[1]]),
        dacb,
        unpad_f(jnp.concatenate([dfcb0, dfcb1], axis=0)),
        dacw, dbb1, dbcw, dbcb, dblg, dblb, accb[3:4],
        jnp.stack([dfcw0, dfcw1]),
        loss_part[0:1, 0:1],
    ]
    small_grad_shapes = [tuple(g.shape) for g in small_grads]
    gather_small = _exchange_start([_pack(small_grads)], "gather", "gather_small_start")

    dw_in = _mm_tn_col(x0, _after(sc_t0, gather_small[-1]), sh_t0, dbcv, "a_in_dw")
    dw_out = _mm_tn_row(y0, dy, d_loc, d_loc, "a_out_dw")
    scatter_a = _exchange_start([dw_in, dw_out], "scatter", "scatter_a_start")

    grads, deltas, new_m, new_v = {}, {}, {}, {}

    def adamw(k, glist, transposed=False):
        def view(a):
            a = jnp.swapaxes(a, 1, 2) if transposed else a
            return a.reshape(len(glist), -1, a.shape[-1])

        w = view(weights[k])
        outs = _adamw(w, [g.reshape(g.shape[0], -1, w.shape[-1]) for g in glist],
                      view(mom_m[k]), view(mom_v[k]), "adamw_" + k)
        if transposed:
            outs = [jnp.swapaxes(o, 1, 2) for o in outs]
        grads[k], deltas[k], new_m[k], new_v[k] = (o.reshape(weights[k].shape) for o in outs)

    r_up1, r_gate1, _ = _exchange_wait(scatter_f1, scatter_a[-1], "scatter", "scatter_f1_wait")
    r_down1, _ = _exchange_wait(scatter_d1, r_gate1, "scatter", "scatter_d1_wait")
    r_pw1, r_pw2, _ = _exchange_wait(scatter_b, r_down1, "scatter", "scatter_b_wait")
    adamw("b_w_pw1", [r_pw1])
    adamw("b_w_pw2", [r_pw2])
    r_down0, _ = _exchange_wait(scatter_d0, deltas["b_w_pw2"], "scatter", "scatter_d0_wait")
    adamw("f_w_down", [r_down0, r_down1])
    r_up0, r_gate0, _ = _exchange_wait(scatter_f0, deltas["f_w_down"], "scatter", "scatter_f0_wait")
    adamw("f_w_up", [r_up0, r_up1], transposed=True)
    adamw("f_w_gate", [r_gate0, r_gate1], transposed=True)

    sg_all, _ = _exchange_wait(gather_small, deltas["f_w_gate"], "gather", "gather_small_wait")
    sg_sum = _sum_parts(sg_all, "sum_small_grads")
    (g_ada_b, g_ltg, g_ltb, g_lcg, g_lcb, g_acb, g_fcb, g_acw, g_bb1, g_bcw, g_bcb, g_blg, g_blb, g_bb2,
     g_fcw, loss_all) = _unpack(sg_sum, small_grad_shapes)
    loss = loss_all[0, 0]

    def my_cols(a, width):
        return lax.dynamic_slice_in_dim(a, me * width, width, axis=a.ndim - 1)

    g_fcw_loc = my_cols(g_fcw, f_pad)[..., :f_loc]
    small = dict(
        ada_b=g_ada_b, ln_tok_g=g_ltg, ln_tok_b=g_ltb, ln_ch_g=g_lcg, ln_ch_b=g_lcb, a_conv_b=g_acb, f_conv_b=g_fcb,
        a_conv_w=my_cols(g_acw, d_loc)[None], b_b_pw1=my_cols(g_bb1, 2 * d_loc), b_conv_w=my_cols(g_bcw, d_loc)[None],
        b_conv_b=my_cols(g_bcb, d_loc), b_ln_g=my_cols(g_blg, d_loc), b_ln_b=my_cols(g_blb, d_loc),
        b_b_pw2=my_cols(g_bb2, d_loc), f_conv_w=g_fcw_loc)

    dmod_all = sg_all.reshape(NDEV, -1)[:, :depth * 6 * d].reshape(NDEV, depth, 6 * d)
    dmod_cols = my_cols(dmod_all, n_ada).transpose(1, 0, 2)
    g_ada_w = _ada_bwd(c_all.T, dmod_cols, "ada_bwd")

    adamw("ada_w", [g_ada_w[0:1], g_ada_w[1:2]])

    def rows_cols(a):
        return a.reshape(-1, a.shape[-1])

    small_keys = list(small)
    small_outs = _adamw_small([rows_cols(weights[k]) for k in small_keys], [rows_cols(small[k]) for k in small_keys],
                              [rows_cols(mom_m[k]) for k in small_keys], [rows_cols(mom_v[k]) for k in small_keys],
                              "adamw_small")
    for i, k in enumerate(small_keys):
        grads[k] = small[k].reshape(weights[k].shape)
        deltas[k], new_m[k], new_v[k] = (o.reshape(weights[k].shape) for o in small_outs[3 * i:3 * i + 3])

    r_in, r_out, _ = _exchange_wait(scatter_a, deltas["ada_w"], "scatter", "scatter_a_wait")
    adamw("a_w_in", [r_in])
    adamw("a_w_out", [r_out])

    return (loss, dx0[None], *[grads[k] for k in names], *[deltas[k] for k in names],
            *[new_m[k] for k in names], *[new_v[k] for k in names])
```

```python
import jax
import jax.numpy as jnp
from jax import lax
from jax.experimental import pallas as pl
from jax.experimental.pallas import tpu as pltpu

NDEV = 8
MESH_AXES = ("x", "y", "c")
LANES = 128
SUBLANES = 8
VMEM_LIMIT = 56 * 1024 * 1024
LN_EPS = 1e-5
SHORT_PAD = 16
LONG_PAD = 32
CHUNK = 16
DW_ROWS = 1024
ADAM_LR, ADAM_B1, ADAM_B2, ADAM_EPS, ADAM_WD, ADAM_STEP = 0.001, 0.9, 0.999, 1e-08, 0.01, 10

F32 = jnp.float32
BF16 = jnp.bfloat16
MESH = pl.DeviceIdType.MESH
NT = (((1,), (1,)), ((), ()))
TN = (((0,), (0,)), ((), ()))


def _tile(n, target, mult=SUBLANES):
    best = None
    for t in range(mult, min(n, target) + 1, mult):
        if n % t == 0:
            best = t
    return best if best is not None else n


def _full(shape):
    nd = len(shape)
    return pl.BlockSpec(shape, lambda *_: (0,) * nd)


def _cp(*sem):
    return pltpu.CompilerParams(dimension_semantics=sem, vmem_limit_bytes=VMEM_LIMIT)


def _sigmoid(x):
    return 1.0 / (1.0 + jnp.exp(-x))


def _peer(x, y, c, d):
    return ((1 - x) if d & 4 else x, (1 - y) if d & 2 else y, (1 - c) if d & 1 else c)


def _lin(p):
    return 4 * p[0] + 2 * p[1] + p[2]


CHIP_MASKS = (2, 4, 6)
MODES_PER_ARRAY = {"gather": NDEV - 1, "scatter": NDEV - 1, "gather_chips": 1 + len(CHIP_MASKS),
                   "forward": len(CHIP_MASKS)}


def _transfers(mode):
    x, y, c = (lax.axis_index(a) for a in MESH_AXES)
    me = _lin((x, y, c))
    if mode == "forward":
        sibling = (x, y, 1 - c)
        return [(sibling, ("land", _lin(_peer(x, y, c, q))), _lin(_peer(x, y, c, q)), _lin(_peer(x, y, c, q ^ 1)))
                for q in CHIP_MASKS]
    masks = (1,) + CHIP_MASKS if mode == "gather_chips" else range(1, NDEV)
    out = []
    for d in masks:
        peer = _peer(x, y, c, d)
        source = ("block", _lin(peer)) if mode == "scatter" else ("whole", None)
        out.append((peer, source, me, _lin(peer)))
    return out


def _remote_copies(src_refs, land_refs, send_sems, recv_sems, mode):
    transfers = _transfers(mode)
    sends, recvs = [], []
    for i, land_ref in enumerate(land_refs):
        for t, (peer, (kind, slot), there, here) in enumerate(transfers):
            k = i * len(transfers) + t
            src = land_ref.at[slot] if kind == "land" else src_refs[i].at[slot] if kind == "block" else src_refs[i]
            for dst_slot, out in ((there, sends), (here, recvs)):
                out.append(pltpu.make_async_remote_copy(
                    src_ref=src, dst_ref=land_ref.at[dst_slot], send_sem=send_sems.at[k], recv_sem=recv_sems.at[k],
                    device_id=peer, device_id_type=MESH))
    return sends, recvs


def _exchange(srcs, mode, name):
    n = len(srcs)
    gather = mode == "gather"

    def body(*refs):
        src_refs, out_refs, token = refs[:n], refs[n:2 * n], refs[2 * n]
        send_sems, recv_sems, local_sems = refs[2 * n + 1:]
        me = _lin(tuple(lax.axis_index(a) for a in MESH_AXES))
        local = []
        for i in range(n):
            mine = src_refs[i] if gather else src_refs[i].at[me]
            cp = pltpu.make_async_copy(mine, out_refs[i].at[me], local_sems.at[i])
            cp.start()
            local.append(cp)
        sends, recvs = _remote_copies(src_refs, out_refs, send_sems, recv_sems, mode)
        for snd in sends:
            snd.start()
        token[...] = jnp.zeros_like(token)
        for snd, rcv in zip(sends, recvs):
            snd.wait_send()
            rcv.wait_recv()
        for cp in local:
            cp.wait()

    out_shape = [jax.ShapeDtypeStruct(((NDEV,) + s.shape) if gather else s.shape, s.dtype) for s in srcs]
    out_shape.append(jax.ShapeDtypeStruct((SUBLANES, LANES), F32))
    any_spec = pl.BlockSpec(memory_space=pl.ANY)
    return pl.pallas_call(
        body, name=name, out_shape=out_shape,
        in_specs=[any_spec] * n, out_specs=[any_spec] * n + [pl.BlockSpec(memory_space=pltpu.VMEM)],
        scratch_shapes=[pltpu.SemaphoreType.DMA((n * (NDEV - 1),)),
                        pltpu.SemaphoreType.DMA((n * (NDEV - 1),)),
                        pltpu.SemaphoreType.DMA((n,))],
    )(*srcs)


HBM_SPEC = pl.BlockSpec(memory_space=pltpu.HBM)
SEM_SPEC = pl.BlockSpec(memory_space=pltpu.SEMAPHORE)
SIDE_EFFECT = pltpu.SideEffectType.DATAFLOW_SIDE_EFFECTING


def _exchange_start(arrays, mode, name):
    me = _lin(tuple(lax.axis_index(a) for a in MESH_AXES))
    if mode == "forward":
        srcs, lands = [], list(arrays)
    else:
        srcs, lands = list(arrays), []
        for s in srcs:
            own = lax.dynamic_index_in_dim(s, me, 0, keepdims=False) if mode == "scatter" else s
            shape = s.shape if mode == "scatter" else (NDEV,) + s.shape
            lands.append(lax.dynamic_update_index_in_dim(lax.empty(shape, s.dtype), own, me, 0))
    ns, n = len(srcs), len(lands)

    def body(*refs):
        src_refs, land_refs = refs[:ns], refs[ns:ns + n]
        send_sems, recv_sems, token = refs[ns + n], refs[ns + n + 1], refs[-1]
        sends, _ = _remote_copies(src_refs, land_refs, send_sems, recv_sems, mode)
        for snd in sends:
            snd.start()
        token[...] = jnp.zeros_like(token)

    operands = [pltpu.with_memory_space_constraint(a, pltpu.HBM) for a in srcs + lands]
    nsem = n * MODES_PER_ARRAY[mode]
    return pl.pallas_call(
        body, name=name,
        out_shape=(pltpu.SemaphoreType.DMA((nsem,)), pltpu.SemaphoreType.DMA((nsem,)),
                   *[pltpu.HBM(a.shape, a.dtype) for a in operands],
                   jax.ShapeDtypeStruct((SUBLANES, LANES), F32)),
        in_specs=[HBM_SPEC] * (ns + n),
        out_specs=(SEM_SPEC, SEM_SPEC, *([HBM_SPEC] * (ns + n)), pl.BlockSpec(memory_space=pltpu.VMEM)),
        input_output_aliases={i: 2 + i for i in range(ns + n)},
        compiler_params=pltpu.CompilerParams(has_side_effects=SIDE_EFFECT),
    )(*operands)


def _exchange_wait(handle, after, mode, name):
    send_sems, recv_sems, *thru = handle[:-1]
    n = len(thru) if mode == "forward" else len(thru) // 2
    ns = len(thru) - n

    def body(*refs):
        src_refs, land_refs = refs[:ns], refs[ns:ns + n]
        sends, recvs = _remote_copies(src_refs, land_refs, refs[ns + n], refs[ns + n + 1], mode)
        for snd, rcv in zip(sends, recvs):
            snd.wait_send()
            rcv.wait_recv()
        refs[-1][...] = jnp.zeros_like(refs[-1])

    outs = pl.pallas_call(
        body, name=name,
        out_shape=(*[pltpu.HBM(a.shape, a.dtype) for a in thru], jax.ShapeDtypeStruct((SUBLANES, LANES), F32)),
        in_specs=[HBM_SPEC] * (ns + n) + [SEM_SPEC, SEM_SPEC, pl.BlockSpec(memory_space=pl.ANY)],
        out_specs=[HBM_SPEC] * (ns + n) + [pl.BlockSpec(memory_space=pltpu.VMEM)],
        input_output_aliases={i: i for i in range(ns + n)},
        compiler_params=pltpu.CompilerParams(has_side_effects=SIDE_EFFECT),
    )(*thru, send_sems, recv_sems, after)
    return outs[ns:]


def _gather_two_level(srcs, after, name):
    first = _exchange_start(srcs, "gather_chips", name + "_chips_start")
    *lands, token = _exchange_wait(first, after, "gather_chips", name + "_chips_wait")
    second = _exchange_start(lands, "forward", name + "_forward_start")
    return _exchange_wait(second, token, "forward", name + "_forward_wait")


def _after(value, token):
    return value + token[0, 0]


ANY_SPEC = pl.BlockSpec(memory_space=pl.ANY)


def _load_cols(wg_ref, widx, w_ref, sems):
    n = wg_ref.shape[-1]
    copies = [pltpu.make_async_copy(wg_ref.at[k, widx], w_ref.at[:, pl.ds(k * n, n)], sems.at[k])
              for k in range(NDEV)]
    for cp in copies:
        cp.start()
    for cp in copies:
        cp.wait()


def _load_rows(wg_ref, r, ridx, w_ref, sems):
    copies = [pltpu.make_async_copy(wg_ref.at[k, pl.ds(ridx * r, r)], w_ref.at[pl.ds(k * r, r)], sems.at[k])
              for k in range(NDEV)]
    for cp in copies:
        cp.start()
    for cp in copies:
        cp.wait()


def _mm_fwd(x, sc, sh, bias, wg, widxs, name):
    s_len, kdim = x.shape
    ncol = NDEV * wg.shape[-1]
    tm = _tile(s_len, 512)
    nw = len(widxs)

    def body(x_ref, sc_ref, sh_ref, b_ref, wg_ref, *rest):
        o_refs, w_refs, sems = rest[:nw], rest[nw:2 * nw], rest[2 * nw]

        @pl.when(pl.program_id(0) == 0)
        def _():
            for i, w_ref in enumerate(w_refs):
                _load_cols(wg_ref, widxs[i], w_ref, sems.at[i])

        h = (x_ref[...] * (1.0 + sc_ref[...]) + sh_ref[...]).astype(BF16)
        for w_ref, o_ref in zip(w_refs, o_refs):
            o_ref[...] = (jnp.dot(h, w_ref[...], preferred_element_type=F32) + b_ref[...]).astype(BF16)

    return pl.pallas_call(
        body, name=name, grid=(s_len // tm,),
        in_specs=[pl.BlockSpec((tm, kdim), lambda i: (i, 0)), _full((1, kdim)), _full((1, kdim)),
                  _full((1, ncol)), ANY_SPEC],
        out_specs=[pl.BlockSpec((tm, ncol), lambda i: (i, 0))] * nw,
        out_shape=[jax.ShapeDtypeStruct((s_len, ncol), BF16)] * nw,
        scratch_shapes=[pltpu.VMEM((kdim, ncol), BF16)] * nw + [pltpu.SemaphoreType.DMA((nw, NDEV))],
        compiler_params=_cp("arbitrary"),
    )(x, sc, sh, bias, wg)


def _mm_ln(a, wg, r, ridx, xres, gate, gam, bet, bias, alpha, name):
    s_len = a.shape[0]
    d = wg.shape[-1]
    tm = _tile(s_len, 512)

    def body(a_ref, wg_ref, x_ref, g_ref, gam_ref, bet_ref, b_ref, y_ref, xo_ref, xh_ref, rs_ref, w_ref, sems):
        @pl.when(pl.program_id(0) == 0)
        def _():
            _load_rows(wg_ref, r, ridx, w_ref, sems)

        y = jnp.dot(a_ref[...], w_ref[...], preferred_element_type=F32) + b_ref[...]
        z = alpha * x_ref[...] + g_ref[...] * y
        mu = jnp.mean(z, axis=-1, keepdims=True)
        zc = z - mu
        var = jnp.mean(zc * zc, axis=-1, keepdims=True)
        rstd = lax.rsqrt(var + LN_EPS)
        xh = zc * rstd
        y_ref[...] = y.astype(BF16)
        xh_ref[...] = xh
        rs_ref[...] = rstd
        xo_ref[...] = xh * gam_ref[...] + bet_ref[...]

    row = pl.BlockSpec((tm, d), lambda i: (i, 0))
    vec = _full((1, d))
    return pl.pallas_call(
        body, name=name, grid=(s_len // tm,),
        in_specs=[pl.BlockSpec((tm, NDEV * r), lambda i: (i, 0)), ANY_SPEC, row, vec, vec, vec, vec],
        out_specs=[row, row, row, pl.BlockSpec((tm, 1), lambda i: (i, 0))],
        out_shape=[jax.ShapeDtypeStruct((s_len, d), BF16)] + [jax.ShapeDtypeStruct((s_len, d), F32)] * 2
        + [jax.ShapeDtypeStruct((s_len, 1), F32)],
        scratch_shapes=[pltpu.VMEM((NDEV * r, d), BF16), pltpu.SemaphoreType.DMA((NDEV,))],
        compiler_params=_cp("arbitrary"),
    )(a, wg, xres, gate, gam, bet, bias)


def _ln_in_specs(tm, d):
    row = pl.BlockSpec((tm, d), lambda i: (i, 0))
    return [row, pl.BlockSpec((tm, 1), lambda i: (i, 0)), _full((1, d)), row, _full((1, d))]


def _ln_out_specs(s_len, tm, d):
    row = pl.BlockSpec((tm, d), lambda i: (i, 0))
    return ([row, row, _full((SUBLANES, d))],
            [jax.ShapeDtypeStruct((s_len, d), BF16), jax.ShapeDtypeStruct((s_len, d), F32),
             jax.ShapeDtypeStruct((SUBLANES, d), F32)])


def _ln_bwd_rows(dxo, ln_refs, out_refs, alpha):
    xh_ref, rs_ref, gam_ref, y_ref, g_ref = ln_refs
    dy_ref, dres_ref, acc_ref = out_refs
    xh = xh_ref[...]
    dxh = dxo * gam_ref[...]
    m1 = jnp.mean(dxh, axis=-1, keepdims=True)
    m2 = jnp.mean(dxh * xh, axis=-1, keepdims=True)
    dz = rs_ref[...] * (dxh - m1 - xh * m2)
    dy = g_ref[...] * dz
    dy_ref[...] = dy.astype(BF16)
    dres_ref[...] = alpha * dz
    acc_ref[0:1, :] += jnp.sum(dxo * xh, axis=0, keepdims=True)
    acc_ref[1:2, :] += jnp.sum(dxo, axis=0, keepdims=True)
    acc_ref[2:3, :] += jnp.sum(dz * y_ref[...].astype(F32), axis=0, keepdims=True)
    acc_ref[3:4, :] += jnp.sum(dy, axis=0, keepdims=True)


def _mm_nt_row(dy, wg, r, ridx, name):
    s_len, d = dy.shape
    tm = _tile(s_len, 512)

    def body(dy_ref, wg_ref, o_ref, w_ref, sems):
        @pl.when(pl.program_id(0) == 0)
        def _():
            _load_rows(wg_ref, r, ridx, w_ref, sems)

        o_ref[...] = lax.dot_general(dy_ref[...], w_ref[...], NT, preferred_element_type=F32).astype(BF16)

    return pl.pallas_call(
        body, name=name, grid=(s_len // tm,),
        in_specs=[pl.BlockSpec((tm, d), lambda i: (i, 0)), ANY_SPEC],
        out_specs=pl.BlockSpec((tm, NDEV * r), lambda i: (i, 0)),
        out_shape=jax.ShapeDtypeStruct((s_len, NDEV * r), BF16),
        scratch_shapes=[pltpu.VMEM((NDEV * r, d), BF16), pltpu.SemaphoreType.DMA((NDEV,))],
        compiler_params=_cp("arbitrary"),
    )(dy, wg)


def _mm_nt_mod(dos, wg, widxs, xin, sc, dres, name, ln=None, alpha=None):
    s_len, kdim = xin.shape
    ncol = NDEV * wg.shape[-1]
    tm = _tile(s_len, 512)
    nw = len(widxs)
    nln = 0 if ln is None else len(ln)
    nout = 2 if ln is None else 4

    def body(*refs):
        do_refs, wg_ref = refs[:nw], refs[nw]
        x_ref, sc_ref, dres_ref = refs[nw + 1:nw + 4]
        ln_refs = refs[nw + 4:nw + 4 + nln]
        out_refs = refs[nw + 4 + nln:nw + 4 + nln + nout]
        w_refs, sems = refs[nw + 4 + nln + nout:-1], refs[-1]
        acc_ref = out_refs[-1]

        @pl.when(pl.program_id(0) == 0)
        def _():
            for ref in out_refs[nout // 2:]:
                ref[...] = jnp.zeros_like(ref)
            for i, w_ref in enumerate(w_refs):
                _load_cols(wg_ref, widxs[i], w_ref, sems.at[i])

        dh = None
        for do_ref, w_ref in zip(do_refs, w_refs):
            p = lax.dot_general(do_ref[...], w_ref[...], NT, preferred_element_type=F32)
            dh = p if dh is None else dh + p
        dx = dh * (1.0 + sc_ref[...]) + dres_ref[...]
        if ln is None:
            out_refs[0][...] = dx
        else:
            _ln_bwd_rows(dx, ln_refs, out_refs[0:3], alpha)
        acc_ref[0:1, :] += jnp.sum(dh * x_ref[...], axis=0, keepdims=True)
        acc_ref[1:2, :] += jnp.sum(dh, axis=0, keepdims=True)

    row = pl.BlockSpec((tm, kdim), lambda i: (i, 0))
    if ln is None:
        out_specs, out_shape = [row], [jax.ShapeDtypeStruct((s_len, kdim), F32)]
    else:
        out_specs, out_shape = _ln_out_specs(s_len, tm, kdim)
    return pl.pallas_call(
        body, name=name, grid=(s_len // tm,),
        in_specs=[pl.BlockSpec((tm, ncol), lambda i: (i, 0))] * nw + [ANY_SPEC, row, _full((1, kdim)), row]
        + ([] if ln is None else _ln_in_specs(tm, kdim)),
        out_specs=out_specs + [_full((SUBLANES, kdim))],
        out_shape=out_shape + [jax.ShapeDtypeStruct((SUBLANES, kdim), F32)],
        scratch_shapes=[pltpu.VMEM((kdim, ncol), BF16)] * nw + [pltpu.SemaphoreType.DMA((nw, NDEV))],
        compiler_params=_cp("arbitrary"),
    )(*dos, wg, xin, sc, dres, *([] if ln is None else ln))


def _mm_tn_col(x, sc, sh, do, name):
    s_len, kdim = x.shape
    n = do.shape[1] // NDEV
    ts = _tile(s_len, DW_ROWS)
    nsteps = s_len // ts

    def body(x_ref, sc_ref, sh_ref, do_ref, o_ref, acc_ref):
        @pl.when(pl.program_id(0) == 0)
        def _():
            acc_ref[...] = jnp.zeros_like(acc_ref)

        h = (x_ref[...] * (1.0 + sc_ref[...]) + sh_ref[...]).astype(BF16)
        acc_ref[...] += lax.dot_general(h, do_ref[...], TN, preferred_element_type=F32)

        @pl.when(pl.program_id(0) == nsteps - 1)
        def _():
            for k in range(NDEV):
                o_ref[k] = acc_ref[:, k * n:(k + 1) * n].astype(BF16)

    return pl.pallas_call(
        body, name=name, grid=(nsteps,),
        in_specs=[pl.BlockSpec((ts, kdim), lambda i: (i, 0)), _full((1, kdim)), _full((1, kdim)),
                  pl.BlockSpec((ts, NDEV * n), lambda i: (i, 0))],
        out_specs=_full((NDEV, kdim, n)),
        out_shape=jax.ShapeDtypeStruct((NDEV, kdim, n), BF16),
        scratch_shapes=[pltpu.VMEM((kdim, NDEV * n), F32)],
        compiler_params=_cp("arbitrary"),
    )(x, sc, sh, do)


def _mm_tn_col_t(x, sc, sh, do, rows_out, name):
    s_len, kdim = x.shape
    n = do.shape[1] // NDEV
    ts = _tile(s_len, DW_ROWS)
    nsteps = s_len // ts

    def body(x_ref, sc_ref, sh_ref, do_ref, o_ref, acc_ref):
        @pl.when(pl.program_id(0) == 0)
        def _():
            acc_ref[...] = jnp.zeros_like(acc_ref)

        h = (x_ref[...] * (1.0 + sc_ref[...]) + sh_ref[...]).astype(BF16)
        acc_ref[...] += lax.dot_general(do_ref[...], h, TN, preferred_element_type=F32)

        @pl.when(pl.program_id(0) == nsteps - 1)
        def _():
            for k in range(NDEV):
                o_ref[k] = acc_ref[k * n:k * n + rows_out, :].astype(BF16)

    return pl.pallas_call(
        body, name=name, grid=(nsteps,),
        in_specs=[pl.BlockSpec((ts, kdim), lambda i: (i, 0)), _full((1, kdim)), _full((1, kdim)),
                  pl.BlockSpec((ts, NDEV * n), lambda i: (i, 0))],
        out_specs=_full((NDEV, rows_out, kdim)),
        out_shape=jax.ShapeDtypeStruct((NDEV, rows_out, kdim), BF16),
        scratch_shapes=[pltpu.VMEM((NDEV * n, kdim), F32)],
        compiler_params=_cp("arbitrary"),
    )(x, sc, sh, do)


def _mm_tn_row(a, dy, r, rows_out, name):
    s_len, d = dy.shape
    ts = _tile(s_len, DW_ROWS)
    nsteps = s_len // ts

    def body(a_ref, dy_ref, o_ref, acc_ref):
        @pl.when(pl.program_id(0) == 0)
        def _():
            acc_ref[...] = jnp.zeros_like(acc_ref)

        acc_ref[...] += lax.dot_general(a_ref[...], dy_ref[...], TN, preferred_element_type=F32)

        @pl.when(pl.program_id(0) == nsteps - 1)
        def _():
            for k in range(NDEV):
                o_ref[k] = acc_ref[k * r:k * r + rows_out, :].astype(BF16)

    return pl.pallas_call(
        body, name=name, grid=(nsteps,),
        in_specs=[pl.BlockSpec((ts, NDEV * r), lambda i: (i, 0)), pl.BlockSpec((ts, d), lambda i: (i, 0))],
        out_specs=_full((NDEV, rows_out, d)),
        out_shape=jax.ShapeDtypeStruct((NDEV, rows_out, d), BF16),
        scratch_shapes=[pltpu.VMEM((NDEV * r, d), F32)],
        compiler_params=_cp("arbitrary"),
    )(a, dy)


def _prev_spec(ts, pad, cb, col):
    return pl.BlockSpec((pad, cb), lambda *g: (jnp.maximum(g[-1] * (ts // pad) - 1, 0), col(g)))


def _next_spec(ts, pad, cb, col, s_len):
    return pl.BlockSpec((pad, cb), lambda *g: (jnp.minimum((g[-1] + 1) * (ts // pad), s_len // pad - 1), col(g)))


class _F32Loads:
    def __init__(self, ref):
        self.ref = ref

    def __getitem__(self, idx):
        return self.ref[idx].astype(F32)


def _direct(buf_ref):
    return lambda off, rows: buf_ref[off:off + rows, :]


def _make_shifts(sh_ref, nrows):
    for r in range(1, SUBLANES):
        sh_ref[r, 0:nrows - SUBLANES, :] = sh_ref[0, r:r + nrows - SUBLANES, :]


def _shifted(sh_ref):
    def read(off, rows):
        r = off % SUBLANES
        return sh_ref[r, off - r:off - r + rows, :]
    return read


def _conv_fwd_rows(read, w_ref, b_ref, ktaps, pad, r0, rows):
    acc = None
    for j in range(ktaps):
        term = w_ref[ktaps - 1 - j:ktaps - j, :] * read(pad - j + r0, rows)
        acc = term if acc is None else acc + term
    return acc + b_ref[...]


def _conv_bwd_rows(read, x_rows, w_ref, dwacc_ref, ktaps, r0, rows):
    acc = None
    for j in range(ktaps):
        sl = read(j + r0, rows)
        term = w_ref[ktaps - 1 - j:ktaps - j, :] * sl
        acc = term if acc is None else acc + term
        prod = x_rows * sl
        fold = prod[0:SUBLANES]
        for q in range(1, rows // SUBLANES):
            fold = fold + prod[q * SUBLANES:(q + 1) * SUBLANES]
        tap = ktaps - 1 - j
        dwacc_ref[tap * SUBLANES:(tap + 1) * SUBLANES, :] += fold
    return acc


def _flush_dw(dwacc_ref, dw_ref, ktaps):
    for tap in range(ktaps):
        dw_ref[tap:tap + 1, :] = jnp.sum(dwacc_ref[tap * SUBLANES:(tap + 1) * SUBLANES, :], axis=0, keepdims=True)


def _gateconv_fwd(bcv, cw, cb, name):
    s_len, d3 = bcv.shape
    d = d3 // 3
    ktaps = cw.shape[0]
    pad = SHORT_PAD
    ts = _tile(s_len, 256)

    def body(gb_ref, gc_ref, v_ref, gcp_ref, vp_ref, w_ref, b_ref, o_ref, pbuf):
        gb_ref, gc_ref, v_ref, gcp_ref, vp_ref = map(_F32Loads, (gb_ref, gc_ref, v_ref, gcp_ref, vp_ref))
        s = pl.program_id(0)
        pbuf[0:pad, :] = jnp.where(s > 0, gcp_ref[...] * vp_ref[...], 0.0)
        pbuf[pad:pad + ts, :] = gc_ref[...] * v_ref[...]
        for r0 in range(0, ts, CHUNK):
            q = _conv_fwd_rows(_direct(pbuf), w_ref, b_ref, ktaps, pad, r0, CHUNK)
            o_ref[r0:r0 + CHUNK, :] = (gb_ref[r0:r0 + CHUNK, :] * q).astype(BF16)

    def cur(part):
        return pl.BlockSpec((ts, d), lambda s: (s, part))

    return pl.pallas_call(
        body, name=name, grid=(s_len // ts,),
        in_specs=[cur(0), cur(1), cur(2),
                  _prev_spec(ts, pad, d, lambda g: 1), _prev_spec(ts, pad, d, lambda g: 2),
                  _full((ktaps, d)), _full((1, d))],
        out_specs=pl.BlockSpec((ts, d), lambda s: (s, 0)),
        out_shape=jax.ShapeDtypeStruct((s_len, d), BF16),
        scratch_shapes=[pltpu.VMEM((pad + ts, d), F32)],
        compiler_params=_cp("parallel"),
    )(bcv, bcv, bcv, bcv, bcv, cw, cb)


def _gateconv_bwd(bcv, dy0, cw, cb, name):
    s_len, d3 = bcv.shape
    d = d3 // 3
    ktaps = cw.shape[0]
    pad = SHORT_PAD
    ts = _tile(s_len, 256)
    nsteps = s_len // ts

    def body(gb_ref, gc_ref, v_ref, gcp_ref, vp_ref, gbn_ref, dy_ref, dyn_ref, w_ref, b_ref,
             o_ref, dw_ref, db_ref, pbuf, dqbuf, dwacc):
        gb_ref, gc_ref, v_ref, gcp_ref, vp_ref, gbn_ref, dy_ref, dyn_ref = map(
            _F32Loads, (gb_ref, gc_ref, v_ref, gcp_ref, vp_ref, gbn_ref, dy_ref, dyn_ref))
        s = pl.program_id(0)

        @pl.when(s == 0)
        def _():
            dwacc[...] = jnp.zeros_like(dwacc)
            db_ref[...] = jnp.zeros_like(db_ref)

        pbuf[0:pad, :] = jnp.where(s > 0, gcp_ref[...] * vp_ref[...], 0.0)
        pbuf[pad:pad + ts, :] = gc_ref[...] * v_ref[...]
        dq = dy_ref[...] * gb_ref[...]
        dqbuf[0:ts, :] = dq
        dqbuf[ts:ts + pad, :] = jnp.where(s < nsteps - 1, dyn_ref[...] * gbn_ref[...], 0.0)
        db_ref[...] += jnp.sum(dq, axis=0, keepdims=True)
        for r0 in range(0, ts, CHUNK):
            rows = slice(r0, r0 + CHUNK)
            q = _conv_fwd_rows(_direct(pbuf), w_ref, b_ref, ktaps, pad, r0, CHUNK)
            o_ref[rows, 0:d] = (dy_ref[rows, :] * q).astype(BF16)
            dp = _conv_bwd_rows(_direct(dqbuf), pbuf[pad + r0:pad + r0 + CHUNK, :], w_ref, dwacc, ktaps, r0, CHUNK)
            o_ref[rows, d:2 * d] = (dp * v_ref[rows, :]).astype(BF16)
            o_ref[rows, 2 * d:3 * d] = (dp * gc_ref[rows, :]).astype(BF16)

        @pl.when(s == nsteps - 1)
        def _():
            _flush_dw(dwacc, dw_ref, ktaps)

    def cur(part):
        return pl.BlockSpec((ts, d), lambda s: (s, part))

    return pl.pallas_call(
        body, name=name, grid=(nsteps,),
        in_specs=[cur(0), cur(1), cur(2),
                  _prev_spec(ts, pad, d, lambda g: 1), _prev_spec(ts, pad, d, lambda g: 2),
                  _next_spec(ts, pad, d, lambda g: 0, s_len),
                  cur(0), _next_spec(ts, pad, d, lambda g: 0, s_len),
                  _full((ktaps, d)), _full((1, d))],
        out_specs=[pl.BlockSpec((ts, d3), lambda s: (s, 0)), _full((ktaps, d)), _full((1, d))],
        out_shape=[jax.ShapeDtypeStruct((s_len, d3), BF16), jax.ShapeDtypeStruct((ktaps, d), F32),
                   jax.ShapeDtypeStruct((1, d), F32)],
        scratch_shapes=[pltpu.VMEM((pad + ts, d), F32), pltpu.VMEM((ts + pad, d), F32),
                        pltpu.VMEM((ktaps * SUBLANES, d), F32)],
        compiler_params=_cp("arbitrary"),
    )(bcv, bcv, bcv, bcv, bcv, bcv, dy0, dy0, cw, cb)


class _Cols:
    def __init__(self, ref, cols):
        self.ref, self.cols = ref, cols

    def __getitem__(self, idx):
        return self.ref[slice(None) if idx is Ellipsis else idx[0], self.cols]

    def __setitem__(self, idx, value):
        self.ref[idx[0], self.cols] = value


def _ffn_tail_fwd(u0, vg, cw, cb, wg, xres, gate, gam, bet, alpha, name):
    s_len, f = u0.shape
    d = wg.shape[-1]
    r = f // NDEV
    ktaps = cw.shape[0]
    pad = SHORT_PAD
    tm = _tile(s_len, 256)
    cbk = 1024 if f % 1024 == 0 else f

    def body(u_ref, up_ref, vg_ref, cw_ref, cb_ref, wg_ref, x_ref, g_ref, gam_ref, bet_ref,
             t_ref, uc_ref, y_ref, xo_ref, xh_ref, rs_ref, ubuf, w_ref, sems):
        u_ref, up_ref, vg_ref = map(_F32Loads, (u_ref, up_ref, vg_ref))
        s = pl.program_id(0)

        @pl.when(s == 0)
        def _():
            _load_rows(wg_ref, r, 0, w_ref, sems)

        ubuf[0:pad, :] = jnp.where(s > 0, up_ref[...], 0.0)
        ubuf[pad:pad + tm, :] = u_ref[...]
        y = None
        for c0 in range(0, f, cbk):
            cols = slice(c0, c0 + cbk)
            read = _direct(_Cols(ubuf, cols))
            for r0 in range(0, tm, CHUNK):
                rows = slice(r0, r0 + CHUNK)
                u = _conv_fwd_rows(read, _Cols(cw_ref, cols), _Cols(cb_ref, cols), ktaps, pad, r0, CHUNK)
                t_ref[rows, cols] = (u * _sigmoid(u) * vg_ref[rows, cols]).astype(BF16)
                uc_ref[rows, cols] = u.astype(BF16)
            p = jnp.dot(t_ref[:, cols], w_ref[cols, :], preferred_element_type=F32)
            y = p if y is None else y + p
        z = alpha * x_ref[...] + g_ref[...] * y
        mu = jnp.mean(z, axis=-1, keepdims=True)
        zc = z - mu
        var = jnp.mean(zc * zc, axis=-1, keepdims=True)
        rstd = lax.rsqrt(var + LN_EPS)
        xh = zc * rstd
        y_ref[...] = y.astype(BF16)
        xh_ref[...] = xh
        rs_ref[...] = rstd
        xo_ref[...] = xh * gam_ref[...] + bet_ref[...]

    wide = pl.BlockSpec((tm, f), lambda i: (i, 0))
    row = pl.BlockSpec((tm, d), lambda i: (i, 0))
    vec = _full((1, d))
    return pl.pallas_call(
        body, name=name, grid=(s_len // tm,),
        in_specs=[wide, _prev_spec(tm, pad, f, lambda g: 0), wide, _full((ktaps, f)), _full((1, f)), ANY_SPEC,
                  row, vec, vec, vec],
        out_specs=[wide, wide, row, row, row, pl.BlockSpec((tm, 1), lambda i: (i, 0))],
        out_shape=[jax.ShapeDtypeStruct((s_len, f), BF16), jax.ShapeDtypeStruct((s_len, f), BF16),
                   jax.ShapeDtypeStruct((s_len, d), BF16),
                   jax.ShapeDtypeStruct((s_len, d), F32), jax.ShapeDtypeStruct((s_len, d), F32),
                   jax.ShapeDtypeStruct((s_len, 1), F32)],
        scratch_shapes=[pltpu.VMEM((pad + tm, f), F32), pltpu.VMEM((f, d), BF16), pltpu.SemaphoreType.DMA((NDEV,))],
        compiler_params=_cp("arbitrary"),
    )(u0, u0, vg, cw, cb, wg, xres, gate, gam, bet)


def _ffn_core_bwd(dy, u0, uc, vg, cw, wg_row, wg_col, xin, sc, dres, ln, alpha, name):
    s_len, f = u0.shape
    d = xin.shape[1]
    r = f // NDEV
    ktaps = cw.shape[0]
    pad = SHORT_PAD
    tm = _tile(s_len, 256)
    nsteps = s_len // tm
    cbk = 1024 if f % 1024 == 0 else f

    def body(dy_ref, dyn_ref, u_ref, uc_ref, ucn_ref, vg_ref, vgn_ref, cw_ref, wgr_ref, wgc_ref,
             x_ref, sc_ref, dres_ref, xh_ref, rs_ref, gam_ref, y_ref, g_ref,
             du0_ref, dvg_ref, dw_ref, db_ref, dyo_ref, dreso_ref, lnacc_ref, acc_ref,
             dtbuf, dubuf, dwacc, wd_ref, wup_ref, wgate_ref, sems):
        u_ref, uc_ref, ucn_ref, vg_ref, vgn_ref = map(_F32Loads, (u_ref, uc_ref, ucn_ref, vg_ref, vgn_ref))
        s = pl.program_id(0)
        last = s == nsteps - 1

        @pl.when(s == 0)
        def _():
            dwacc[...] = jnp.zeros_like(dwacc)
            db_ref[...] = jnp.zeros_like(db_ref)
            acc_ref[...] = jnp.zeros_like(acc_ref)
            lnacc_ref[...] = jnp.zeros_like(lnacc_ref)
            _load_rows(wgr_ref, r, 0, wd_ref, sems.at[0])
            _load_cols(wgc_ref, 0, wup_ref, sems.at[1])
            _load_cols(wgc_ref, 1, wgate_ref, sems.at[2])

        dy_cur, dy_nxt = dy_ref[...], dyn_ref[...]
        dh = None
        for c0 in range(0, f, cbk):
            cols = slice(c0, c0 + cbk)
            wd_blk = wd_ref[cols, :]
            dtbuf[0:tm, :] = lax.dot_general(dy_cur, wd_blk, NT, preferred_element_type=F32)
            dtbuf[tm:tm + pad, :] = jnp.where(
                last, 0.0, lax.dot_general(dy_nxt, wd_blk, NT, preferred_element_type=F32))
            for r0 in range(0, tm + pad, CHUNK):
                dtr = dtbuf[r0:r0 + CHUNK, :]
                if r0 < tm:
                    u, vgr = uc_ref[r0:r0 + CHUNK, cols], vg_ref[r0:r0 + CHUNK, cols]
                    sg = _sigmoid(u)
                    dvg_ref[r0:r0 + CHUNK, cols] = (dtr * u * sg).astype(BF16)
                else:
                    u, vgr = ucn_ref[r0 - tm:r0 - tm + CHUNK, cols], vgn_ref[r0 - tm:r0 - tm + CHUNK, cols]
                    sg = _sigmoid(u)
                dubuf[r0:r0 + CHUNK, :] = dtr * vgr * (sg * (1.0 + u * (1.0 - sg)))
            db_ref[:, cols] += jnp.sum(dubuf[0:tm, :], axis=0, keepdims=True)
            for r0 in range(0, tm, CHUNK):
                du0 = _conv_bwd_rows(_direct(dubuf), u_ref[r0:r0 + CHUNK, cols], _Cols(cw_ref, cols),
                                     _Cols(dwacc, cols), ktaps, r0, CHUNK)
                du0_ref[r0:r0 + CHUNK, cols] = du0.astype(BF16)
            p = (lax.dot_general(du0_ref[:, cols], wup_ref[:, cols], NT, preferred_element_type=F32)
                 + lax.dot_general(dvg_ref[:, cols], wgate_ref[:, cols], NT, preferred_element_type=F32))
            dh = p if dh is None else dh + p
        dx = dh * (1.0 + sc_ref[...]) + dres_ref[...]
        _ln_bwd_rows(dx, (xh_ref, rs_ref, gam_ref, y_ref, g_ref), (dyo_ref, dreso_ref, lnacc_ref), alpha)
        acc_ref[0:1, :] += jnp.sum(dh * x_ref[...], axis=0, keepdims=True)
        acc_ref[1:2, :] += jnp.sum(dh, axis=0, keepdims=True)

        @pl.when(last)
        def _():
            _flush_dw(dwacc, dw_ref, ktaps)

    wide = pl.BlockSpec((tm, f), lambda i: (i, 0))
    row = pl.BlockSpec((tm, d), lambda i: (i, 0))
    ln_out_specs, ln_out_shape = _ln_out_specs(s_len, tm, d)
    return pl.pallas_call(
        body, name=name, grid=(nsteps,),
        in_specs=[row, _next_spec(tm, pad, d, lambda g: 0, s_len),
                  wide, wide, _next_spec(tm, pad, f, lambda g: 0, s_len),
                  wide, _next_spec(tm, pad, f, lambda g: 0, s_len),
                  _full((ktaps, f)), ANY_SPEC, ANY_SPEC, row, _full((1, d)), row]
        + _ln_in_specs(tm, d),
        out_specs=[wide, wide, _full((ktaps, f)), _full((1, f))] + ln_out_specs + [_full((SUBLANES, d))],
        out_shape=[jax.ShapeDtypeStruct((s_len, f), BF16), jax.ShapeDtypeStruct((s_len, f), BF16),
                   jax.ShapeDtypeStruct((ktaps, f), F32), jax.ShapeDtypeStruct((1, f), F32)]
        + ln_out_shape + [jax.ShapeDtypeStruct((SUBLANES, d), F32)],
        scratch_shapes=[pltpu.VMEM((tm + pad, cbk), F32),
                        pltpu.VMEM((tm + pad, cbk), F32), pltpu.VMEM((ktaps * SUBLANES, f), F32),
                        pltpu.VMEM((f, d), BF16), pltpu.VMEM((d, f), BF16), pltpu.VMEM((d, f), BF16),
                        pltpu.SemaphoreType.DMA((3, NDEV))],
        compiler_params=_cp("arbitrary"),
    )(dy, dy, u0, uc, uc, vg, vg, cw, wg_row, wg_col, xin, sc, dres, *ln)


def _b_mid_fwd(ub, cw, cb, lng, lnb, name):
    s_len, d2 = ub.shape
    d = d2 // 2
    ktaps = cw.shape[0]
    pad = LONG_PAD
    ts = _tile(s_len, 256)

    def body(a_ref, g_ref, ap_ref, gp_ref, w_ref, b_ref, lng_ref, lnb_ref, a2_ref, a4_ref, abuf):
        a_ref, g_ref, ap_ref, gp_ref = map(_F32Loads, (a_ref, g_ref, ap_ref, gp_ref))
        s = pl.program_id(0)
        abuf[0, 0:pad, :] = jnp.where(s > 0, ap_ref[...] * _sigmoid(gp_ref[...]), 0.0)
        abuf[0, pad:pad + ts, :] = a_ref[...] * _sigmoid(g_ref[...])
        _make_shifts(abuf, pad + ts)
        for r0 in range(0, ts, CHUNK):
            a2_ref[r0:r0 + CHUNK, :] = _conv_fwd_rows(_shifted(abuf), w_ref, b_ref, ktaps, pad, r0, CHUNK)
        a2 = a2_ref[...]
        mu = jnp.mean(a2, axis=-1, keepdims=True)
        ac = a2 - mu
        var = jnp.mean(ac * ac, axis=-1, keepdims=True)
        a3 = ac * lax.rsqrt(var + LN_EPS) * lng_ref[...] + lnb_ref[...]
        a4_ref[...] = (a3 * _sigmoid(a3)).astype(BF16)

    def cur(part):
        return pl.BlockSpec((ts, d), lambda s: (s, part))

    vec = _full((1, d))
    return pl.pallas_call(
        body, name=name, grid=(s_len // ts,),
        in_specs=[cur(0), cur(1), _prev_spec(ts, pad, d, lambda g: 0), _prev_spec(ts, pad, d, lambda g: 1),
                  _full((ktaps, d)), vec, vec, vec],
        out_specs=[cur(0), cur(0)],
        out_shape=[jax.ShapeDtypeStruct((s_len, d), F32), jax.ShapeDtypeStruct((s_len, d), BF16)],
        scratch_shapes=[pltpu.VMEM((SUBLANES, pad + ts, d), F32)],
        compiler_params=_cp("parallel"),
    )(ub, ub, ub, ub, cw, cb, lng, lnb)


def _b_mid_bwd(ub, a2, da4, cw, lng, lnb, name):
    s_len, d2 = ub.shape
    d = d2 // 2
    ktaps = cw.shape[0]
    pad = LONG_PAD
    ts = _tile(s_len, 256)
    nsteps = s_len // ts

    def body(a_ref, g_ref, a2_ref, a2n_ref, da4_ref, da4n_ref, w_ref, lng_ref, lnb_ref,
             du_ref, dw_ref, db_ref, dlng_ref, dlnb_ref, dbias_ref, dabuf, dwacc):
        a_ref, g_ref, da4_ref, da4n_ref = map(_F32Loads, (a_ref, g_ref, da4_ref, da4n_ref))
        s = pl.program_id(0)
        last = s == nsteps - 1

        @pl.when(s == 0)
        def _():
            dwacc[...] = jnp.zeros_like(dwacc)
            for ref in (db_ref, dlng_ref, dlnb_ref, dbias_ref):
                ref[...] = jnp.zeros_like(ref)

        def ln_silu_bwd(a2_t, da4_t):
            mu = jnp.mean(a2_t, axis=-1, keepdims=True)
            ac = a2_t - mu
            var = jnp.mean(ac * ac, axis=-1, keepdims=True)
            rstd = lax.rsqrt(var + LN_EPS)
            ah = ac * rstd
            a3 = ah * lng_ref[...] + lnb_ref[...]
            sg = _sigmoid(a3)
            da3 = da4_t * (sg * (1.0 + a3 * (1.0 - sg)))
            dah = da3 * lng_ref[...]
            m1 = jnp.mean(dah, axis=-1, keepdims=True)
            m2 = jnp.mean(dah * ah, axis=-1, keepdims=True)
            return rstd * (dah - m1 - ah * m2), da3, ah

        da2, da3, ah = ln_silu_bwd(a2_ref[...], da4_ref[...])
        dabuf[0, 0:ts, :] = da2
        dlng_ref[...] += jnp.sum(da3 * ah, axis=0, keepdims=True)
        dlnb_ref[...] += jnp.sum(da3, axis=0, keepdims=True)
        db_ref[...] += jnp.sum(da2, axis=0, keepdims=True)
        da2n, _, _ = ln_silu_bwd(a2n_ref[...], jnp.where(last, 0.0, da4n_ref[...]))
        dabuf[0, ts:ts + pad, :] = da2n
        _make_shifts(dabuf, ts + pad)
        for r0 in range(0, ts, CHUNK):
            rows = slice(r0, r0 + CHUNK)
            a_r, g_r = a_ref[rows, :], g_ref[rows, :]
            sg = _sigmoid(g_r)
            da1 = _conv_bwd_rows(_shifted(dabuf), a_r * sg, w_ref, dwacc, ktaps, r0, CHUNK)
            da = da1 * sg
            dg = da1 * a_r * sg * (1.0 - sg)
            du_ref[rows, 0:d] = da.astype(BF16)
            du_ref[rows, d:2 * d] = dg.astype(BF16)
            dbias_ref[:, 0:d] += jnp.sum(da, axis=0, keepdims=True)
            dbias_ref[:, d:2 * d] += jnp.sum(dg, axis=0, keepdims=True)

        @pl.when(last)
        def _():
            _flush_dw(dwacc, dw_ref, ktaps)

    def cur(part):
        return pl.BlockSpec((ts, d), lambda s: (s, part))

    vec = _full((1, d))
    nxt = _next_spec(ts, pad, d, lambda g: 0, s_len)
    return pl.pallas_call(
        body, name=name, grid=(nsteps,),
        in_specs=[cur(0), cur(1), cur(0), nxt, cur(0), nxt, _full((ktaps, d)), vec, vec],
        out_specs=[pl.BlockSpec((ts, d2), lambda s: (s, 0)), _full((ktaps, d)), vec, vec, vec, _full((1, d2))],
        out_shape=[jax.ShapeDtypeStruct((s_len, d2), BF16), jax.ShapeDtypeStruct((ktaps, d), F32),
                   jax.ShapeDtypeStruct((1, d), F32), jax.ShapeDtypeStruct((1, d), F32),
                   jax.ShapeDtypeStruct((1, d), F32), jax.ShapeDtypeStruct((1, d2), F32)],
        scratch_shapes=[pltpu.VMEM((SUBLANES, ts + pad, d), F32), pltpu.VMEM((ktaps * SUBLANES, d), F32)],
        compiler_params=_cp("arbitrary"),
    )(ub, ub, a2, a2, da4, da4, cw, lng, lnb)


def _loss_head(xo, tgt, ln, alpha, name):
    s_len, d = xo.shape
    tm = _tile(s_len, 512)

    def body(x_ref, t_ref, xh_ref, rs_ref, gam_ref, y_ref, g_ref, dy_ref, dres_ref, acc_ref, l_ref):
        @pl.when(pl.program_id(0) == 0)
        def _():
            l_ref[...] = jnp.zeros_like(l_ref)
            acc_ref[...] = jnp.zeros_like(acc_ref)

        e = x_ref[...] - t_ref[...]
        per_row = jnp.sum(e * e, axis=-1, keepdims=True) * (1.0 / d)
        l_ref[...] += 0.5 * jnp.sum(per_row, axis=0, keepdims=True)
        _ln_bwd_rows(e * (1.0 / d), (xh_ref, rs_ref, gam_ref, y_ref, g_ref), (dy_ref, dres_ref, acc_ref), alpha)

    row = pl.BlockSpec((tm, d), lambda i: (i, 0))
    ln_out_specs, ln_out_shape = _ln_out_specs(s_len, tm, d)
    return pl.pallas_call(
        body, name=name, grid=(s_len // tm,),
        in_specs=[row, row] + _ln_in_specs(tm, d), out_specs=ln_out_specs + [_full((1, LANES))],
        out_shape=ln_out_shape + [jax.ShapeDtypeStruct((1, LANES), F32)],
        compiler_params=_cp("arbitrary"),
    )(xo, tgt, *ln)


def _ada_fwd(c_all, ada_w, ada_b_loc, name):
    depth, d, n = ada_w.shape

    def body(c_ref, w_ref, b_ref, o_ref):
        c = c_ref[...]
        act = c * _sigmoid(c)
        o_ref[...] = jnp.dot(act, w_ref[...], preferred_element_type=F32,
                             precision=lax.Precision.HIGHEST) + b_ref[...]

    return pl.pallas_call(
        body, name=name, grid=(depth,),
        in_specs=[_full((NDEV, d)), pl.BlockSpec((None, d, n), lambda i: (i, 0, 0)),
                  pl.BlockSpec((None, 1, n), lambda i: (i, 0, 0))],
        out_specs=pl.BlockSpec((None, NDEV, n), lambda i: (i, 0, 0)),
        out_shape=jax.ShapeDtypeStruct((depth, NDEV, n), F32),
        compiler_params=_cp("parallel"),
    )(c_all, ada_w, ada_b_loc.reshape(depth, 1, n))


def _ada_bwd(c_all_t, dmod_cols, name):
    depth, _, n = dmod_cols.shape
    d = c_all_t.shape[0]

    def body(ct_ref, dm_ref, o_ref):
        ct = ct_ref[...]
        act = ct * _sigmoid(ct)
        acc = None
        for b in range(NDEV):
            term = act[:, b:b + 1] * dm_ref[b:b + 1, :]
            acc = term if acc is None else acc + term
        o_ref[...] = acc

    return pl.pallas_call(
        body, name=name, grid=(depth,),
        in_specs=[_full((d, NDEV)), pl.BlockSpec((None, NDEV, n), lambda i: (i, 0, 0))],
        out_specs=pl.BlockSpec((None, d, n), lambda i: (i, 0, 0)),
        out_shape=jax.ShapeDtypeStruct((depth, d, n), F32),
        compiler_params=_cp("parallel"),
    )(c_all_t, dmod_cols)


def _sum_parts(parts, name):
    _, rows, lanes = parts.shape

    def body(p_ref, o_ref):
        acc = p_ref[0]
        for k in range(1, NDEV):
            acc = acc + p_ref[k]
        o_ref[...] = acc

    return pl.pallas_call(
        body, name=name, in_specs=[_full(parts.shape)], out_specs=_full((rows, lanes)), grid=(1,),
        out_shape=jax.ShapeDtypeStruct((rows, lanes), F32), compiler_params=_cp("arbitrary"),
    )(parts)


def _adamw(w, glist, m, v, name):
    nl, rows, cols = w.shape
    tr = _tile(rows, 256, 2 * SUBLANES)

    def body(w_ref, *rest):
        g_refs = rest[:nl]
        m_ref, v_ref, go_ref, d_ref, mo_ref, vo_ref = rest[nl:]
        g = None
        for layer, g_ref in enumerate(g_refs):
            part = g_ref[0].astype(F32)
            for p in range(1, g_ref.shape[0]):
                part = part + g_ref[p].astype(F32)
            g = part if g is None else jnp.where(pl.program_id(0) == layer, part, g)
        go_ref[...] = g
        d_ref[...], mo_ref[...], vo_ref[...] = _adam_step(w_ref[...], g, m_ref[...], v_ref[...])

    blk = pl.BlockSpec((None, tr, cols), lambda l, i: (l, i, 0))
    g_specs = [pl.BlockSpec((g.shape[0], tr, cols), lambda l, i: (0, i, 0)) for g in glist]
    return pl.pallas_call(
        body, name=name, grid=(nl, rows // tr),
        in_specs=[blk] + g_specs + [blk, blk],
        out_specs=[blk] * 4, out_shape=[jax.ShapeDtypeStruct((nl, rows, cols), F32)] * 4,
        compiler_params=_cp("parallel", "parallel"),
    )(w, *glist, m, v)


def _adam_step(w, g, m, v):
    m1 = ADAM_B1 * m + (1.0 - ADAM_B1) * g
    v1 = ADAM_B2 * v + (1.0 - ADAM_B2) * (g * g)
    m_hat = m1 / (1.0 - ADAM_B1 ** ADAM_STEP)
    v_hat = v1 / (1.0 - ADAM_B2 ** ADAM_STEP)
    return -ADAM_LR * (m_hat / (jnp.sqrt(v_hat) + ADAM_EPS) + ADAM_WD * w), m1, v1


def _adamw_small(ws, gs, ms, vs, name):
    n = len(ws)

    def body(*refs):
        ins, outs = refs[:4 * n], refs[4 * n:]
        for i in range(n):
            w_ref, g_ref, m_ref, v_ref = ins[i], ins[n + i], ins[2 * n + i], ins[3 * n + i]
            delta, m1, v1 = _adam_step(w_ref[...], g_ref[...], m_ref[...], v_ref[...])
            outs[3 * i][...] = delta
            outs[3 * i + 1][...] = m1
            outs[3 * i + 2][...] = v1

    operands = list(ws) + list(gs) + list(ms) + list(vs)
    out_shape = [jax.ShapeDtypeStruct(w.shape, F32) for w in ws for _ in range(3)]
    return pl.pallas_call(
        body, name=name, grid=(1,), in_specs=[_full(a.shape) for a in operands],
        out_specs=[_full(s.shape) for s in out_shape], out_shape=out_shape,
        compiler_params=_cp("arbitrary"),
    )(*operands)


def _pack(pieces):
    flat = jnp.concatenate([p.reshape(-1) for p in pieces])
    unit = SUBLANES * LANES
    padded = -(-flat.shape[0] // unit) * unit
    return jnp.pad(flat, (0, padded - flat.shape[0])).reshape(padded // LANES, LANES)


def _unpack(packed, shapes, lead=()):
    flat = packed.reshape(lead + (-1,))
    out, off = [], 0
    for s in shapes:
        size = 1
        for dim in s:
            size *= dim
        out.append(flat[..., off:off + size].reshape(lead + tuple(s)))
        off += size
    return out


def _pad_last(a, n):
    return jnp.pad(a, [(0, 0)] * (a.ndim - 1) + [(0, n - a.shape[-1])])


def kernel(x, c, ada_w, ada_b, ln_tok_g, ln_tok_b, ln_ch_g, ln_ch_b, a_w_in, a_conv_w, a_conv_b, a_w_out, b_w_pw1, b_b_pw1, b_conv_w, b_conv_b, b_ln_g, b_ln_b, b_w_pw2, b_b_pw2, f_w_up, f_conv_w, f_conv_b, f_w_gate, f_w_down, loss_target, m_ada_w, m_ada_b, m_ln_tok_g, m_ln_tok_b, m_ln_ch_g, m_ln_ch_b, m_a_w_in, m_a_conv_w, m_a_conv_b, m_a_w_out, m_b_w_pw1, m_b_b_pw1, m_b_conv_w, m_b_conv_b, m_b_ln_g, m_b_ln_b, m_b_w_pw2, m_b_b_pw2, m_f_w_up, m_f_conv_w, m_f_conv_b, m_f_w_gate, m_f_w_down, v_ada_w, v_ada_b, v_ln_tok_g, v_ln_tok_b, v_ln_ch_g, v_ln_ch_b, v_a_w_in, v_a_conv_w, v_a_conv_b, v_a_w_out, v_b_w_pw1, v_b_b_pw1, v_b_conv_w, v_b_conv_b, v_b_ln_g, v_b_ln_b, v_b_w_pw2, v_b_b_pw2, v_f_w_up, v_f_conv_w, v_f_conv_b, v_f_w_gate, v_f_w_down):
    weights = dict(ada_w=ada_w, ada_b=ada_b, ln_tok_g=ln_tok_g, ln_tok_b=ln_tok_b, ln_ch_g=ln_ch_g, ln_ch_b=ln_ch_b, a_w_in=a_w_in, a_conv_w=a_conv_w, a_conv_b=a_conv_b, a_w_out=a_w_out, b_w_pw1=b_w_pw1, b_b_pw1=b_b_pw1, b_conv_w=b_conv_w, b_conv_b=b_conv_b, b_ln_g=b_ln_g, b_ln_b=b_ln_b, b_w_pw2=b_w_pw2, b_b_pw2=b_b_pw2, f_w_up=f_w_up, f_conv_w=f_conv_w, f_conv_b=f_conv_b, f_w_gate=f_w_gate, f_w_down=f_w_down)
    mom_m = dict(ada_w=m_ada_w, ada_b=m_ada_b, ln_tok_g=m_ln_tok_g, ln_tok_b=m_ln_tok_b, ln_ch_g=m_ln_ch_g, ln_ch_b=m_ln_ch_b, a_w_in=m_a_w_in, a_conv_w=m_a_conv_w, a_conv_b=m_a_conv_b, a_w_out=m_a_w_out, b_w_pw1=m_b_w_pw1, b_b_pw1=m_b_b_pw1, b_conv_w=m_b_conv_w, b_conv_b=m_b_conv_b, b_ln_g=m_b_ln_g, b_ln_b=m_b_ln_b, b_w_pw2=m_b_w_pw2, b_b_pw2=m_b_b_pw2, f_w_up=m_f_w_up, f_conv_w=m_f_conv_w, f_conv_b=m_f_conv_b, f_w_gate=m_f_w_gate, f_w_down=m_f_w_down)
    mom_v = dict(ada_w=v_ada_w, ada_b=v_ada_b, ln_tok_g=v_ln_tok_g, ln_tok_b=v_ln_tok_b, ln_ch_g=v_ln_ch_g, ln_ch_b=v_ln_ch_b, a_w_in=v_a_w_in, a_conv_w=v_a_conv_w, a_conv_b=v_a_conv_b, a_w_out=v_a_w_out, b_w_pw1=v_b_w_pw1, b_b_pw1=v_b_b_pw1, b_conv_w=v_b_conv_w, b_conv_b=v_b_conv_b, b_ln_g=v_b_ln_g, b_ln_b=v_b_ln_b, b_w_pw2=v_b_w_pw2, b_b_pw2=v_b_b_pw2, f_w_up=v_f_w_up, f_conv_w=v_f_conv_w, f_conv_b=v_f_conv_b, f_w_gate=v_f_w_gate, f_w_down=v_f_w_down)
    names = list(weights)

    depth, d, n_ada = ada_w.shape
    assert depth == 2 and a_w_in.shape[0] == 1 and b_w_pw1.shape[0] == 1
    s_len = x.shape[1]
    f_loc = f_w_up.shape[-1]
    f_pad = -(-f_loc // LANES) * LANES
    f_all = NDEV * f_pad
    d_loc = d // NDEV
    ka, kb, kf = a_conv_w.shape[1], b_conv_w.shape[1], f_conv_w.shape[1]
    alpha = (2.0 * depth) ** 0.25
    assert a_w_in.shape[-1] == f_pad and f_pad % d_loc == 0
    me = 4 * lax.axis_index("x") + 2 * lax.axis_index("y") + lax.axis_index("c")

    small_shapes = [(d,), (ka, d_loc), (2 * d_loc,), (kb, d_loc), (d_loc,), (d_loc,), (d_loc,), (d_loc,),
                    (depth, kf, f_pad)]
    small_loc = _pack([c[0], a_conv_w[0], b_b_pw1[0], b_conv_w[0], b_conv_b[0], b_ln_g[0], b_ln_b[0],
                       b_b_pw2[0], _pad_last(f_conv_w, f_pad)])
    g_small, g_in, _ = _gather_two_level([small_loc, a_w_in.astype(BF16)], small_loc, "gather_first")

    (c_all, acw_g, bb1_g, bcw_g, bcb_g, blg_g, blb_g, bb2_g, fcw_g) = _unpack(g_small, small_shapes, (NDEV,))
    a_cw = acw_g.transpose(1, 0, 2).reshape(ka, d)
    b_cw = bcw_g.transpose(1, 0, 2).reshape(kb, d)
    b_b1 = bb1_g.reshape(1, 2 * d)
    b_cb, b_lg, b_lb, b_b2 = (t.reshape(1, d) for t in (bcb_g, blg_g, blb_g, bb2_g))
    f_cw = fcw_g.transpose(1, 2, 0, 3).reshape(depth, kf, f_all)
    f_cb = _pad_last(f_conv_b.reshape(depth, NDEV, f_loc), f_pad).reshape(depth, 1, f_all)

    ada_b_loc = lax.dynamic_slice(ada_b, (0, me * n_ada), (depth, n_ada))
    mod_part = _ada_fwd(c_all, ada_w, ada_b_loc, "ada_fwd")
    mod_g, mod_done = _exchange([mod_part.reshape(depth * NDEV, n_ada)], "gather", "gather_mod")
    mod_all = mod_g.reshape(NDEV, depth, NDEV, n_ada).transpose(1, 2, 0, 3).reshape(depth, NDEV, 6 * d)
    mod = lax.dynamic_slice(mod_all, (0, me, 0), (depth, 1, 6 * d))[:, 0]

    gather_out = _exchange_start([_after(a_w_out[0], mod_done).astype(BF16)], "gather_chips", "gather_out_start")
    up_pad = _pad_last(_after(f_w_up, gather_out[-1]), f_pad).astype(BF16)
    gate_pad = _pad_last(f_w_gate, f_pad).astype(BF16)
    down_pad = jnp.pad(f_w_down, ((0, 0), (0, f_pad - f_loc), (0, 0))).astype(BF16)
    col_f = [jnp.stack([up_pad[i], gate_pad[i]]) for i in range(depth)]
    row_b = jnp.concatenate([down_pad[1], b_w_pw2[0].astype(BF16)], axis=0)
    ridx_pw2 = f_pad // d_loc
    gather_f0 = _exchange_start([col_f[0], down_pad[0]], "gather_chips", "gather_f0_start")

    def mod_rows(i):
        return [mod[i:i + 1, j * d:(j + 1) * d] for j in range(6)]

    zeros_d = jnp.zeros((1, d), F32)
    zeros_f = jnp.zeros((1, f_all), F32)
    x0 = x[0]

    sh_t0, sc_t0, g_t0, sh_c0, sc_c0, g_c0 = mod_rows(0)
    sh_t1, sc_t1, g_t1, sh_c1, sc_c1, g_c1 = mod_rows(1)

    sc_t0 = _after(sc_t0, gather_f0[-1])
    bcv, = _mm_fwd(x0, sc_t0, sh_t0, jnp.zeros((1, 3 * d), F32), g_in, (0,), "a_in_fwd")
    y0 = _gateconv_fwd(bcv, a_cw, a_conv_b, "a_conv_fwd")
    g_out, landed = _exchange_wait(gather_out, y0, "gather_chips", "gather_out_wait")
    g_out, _ = _exchange_wait(_exchange_start([g_out], "forward", "gather_out_fwd_start"), landed, "forward",
                              "gather_out_fwd_wait")
    y_a, x1, xh1, rs1 = _mm_ln(y0, g_out, d_loc, 0, x0, g_t0, ln_tok_g[0:1], ln_tok_b[0:1], zeros_d,
                               alpha, "a_out_ln_fwd")

    def ffn_fwd(xin, sc, sh, gate, gam, bet, g_colf, g_rowf, layer, tag):
        u0, vg = _mm_fwd(xin, sc, sh, zeros_f, g_colf, (0, 1), "f_upgate_fwd" + tag)
        t, uc, y, xo, xh, rs = _ffn_tail_fwd(u0, vg, f_cw[layer], f_cb[layer], g_rowf, xin, gate, gam, bet, alpha,
                                             "f_tail_fwd" + tag)
        return (u0, uc), vg, t, y, xo, xh, rs

    g_colf0, g_rowf0, landed = _exchange_wait(gather_f0, x1, "gather_chips", "gather_f0_wait")
    g_colf0, g_rowf0, landed = _exchange_wait(
        _exchange_start([g_colf0, g_rowf0], "forward", "gather_f0_fwd_start"), landed, "forward", "gather_f0_fwd_wait")
    gather_1 = _exchange_start([_after(b_w_pw1, landed).astype(BF16), col_f[1], row_b], "gather_chips",
                               "gather_1_start")
    sc_c0 = _after(sc_c0, gather_1[-1])
    u0_0, vg_0, t_0, y_f0, x2, xh2, rs2 = ffn_fwd(x1, sc_c0, sh_c0, g_c0, ln_ch_g[0:1], ln_ch_b[0:1],
                                                  g_colf0, g_rowf0, 0, "0")

    *lands_1, landed = _exchange_wait(gather_1, x2, "gather_chips", "gather_1_wait")
    g_pw1, g_colf1, g_rowb, _ = _exchange_wait(_exchange_start(lands_1, "forward", "gather_1_fwd_start"), landed,
                                                 "forward", "gather_1_fwd_wait")
    ub, = _mm_fwd(x2, sc_t1, sh_t1, b_b1, g_pw1, (0,), "b_pw1_fwd")
    a2, a4 = _b_mid_fwd(ub, b_cw, b_cb, b_lg, b_lb, "b_mid_fwd")
    y_b, x3, xh3, rs3 = _mm_ln(a4, g_rowb, d_loc, ridx_pw2, x2, g_t1, ln_tok_g[1:2], ln_tok_b[1:2], b_b2,
                               alpha, "b_pw2_ln_fwd")
    u0_1, vg_1, t_1, y_f1, x4, xh4, rs4 = ffn_fwd(x3, sc_c1, sh_c1, g_c1, ln_ch_g[1:2], ln_ch_b[1:2],
                                                  g_colf1, g_rowb, 1, "1")

    ln_f1 = (xh4, rs4, ln_ch_g[1:2], y_f1, g_c1)
    ln_b = (xh3, rs3, ln_tok_g[1:2], y_b, g_t1)
    ln_f0 = (xh2, rs2, ln_ch_g[0:1], y_f0, g_c0)
    ln_a = (xh1, rs1, ln_tok_g[0:1], y_a, g_t0)
    dy, dres, accf1, loss_part = _loss_head(x4, loss_target[0], ln_f1, alpha, "loss_head")

    def ffn_bwd(dy, dres, xin, sc, sh, u0, vg, t, g_colf, g_rowf, ln_below, layer, tag):
        dw_down = _mm_tn_row(t, dy, f_pad, f_loc, "f_down_dw" + tag)
        scatter_down = _exchange_start([dw_down], "scatter", "scatter_d%s_start" % tag)
        du0, dvg, dcw, dcb, dy_below, dres_below, acc_below, acc2 = _ffn_core_bwd(
            dy, u0[0], u0[1], vg, f_cw[layer], g_rowf, g_colf, xin, _after(sc, scatter_down[-1]), dres,
            ln_below, alpha, "f_core_bwd" + tag)
        dw_up = _mm_tn_col_t(xin, sc, sh, du0, f_loc, "f_up_dw" + tag)
        dw_gate = _mm_tn_col_t(xin, sc, sh, dvg, f_loc, "f_gate_dw" + tag)
        scatter = _exchange_start([dw_up, dw_gate], "scatter", "scatter_f%s_start" % tag)
        return dy_below, dres_below, acc_below, acc2, (scatter, scatter_down), dcw, dcb

    dy, dres, accb, acc2f1, (scatter_f1, scatter_d1), dfcw1, dfcb1 = ffn_bwd(
        dy, dres, x3, sc_c1, sh_c1, u0_1, vg_1, t_1, g_colf1, g_rowb, ln_b, 1, "1")

    da4 = _mm_nt_row(dy, g_rowb, d_loc, ridx_pw2, "b_pw2_dx")
    dw_pw2 = _mm_tn_row(a4, dy, d_loc, d_loc, "b_pw2_dw")
    du, dbcw, dbcb, dblg, dblb, dbb1 = _b_mid_bwd(ub, a2, da4, b_cw, _after(b_lg, scatter_f1[-1]), b_lb, "b_mid_bwd")
    dw_pw1 = _mm_tn_col(x2, sc_t1, sh_t1, du, "b_pw1_dw")
    scatter_b = _exchange_start([dw_pw1, dw_pw2], "scatter", "scatter_b_start")
    dy, dres, accf0, acc2b = _mm_nt_mod([du], g_pw1, (0,), x2, _after(sc_t1, scatter_b[-1]), dres, "b_pw1_dx",
                                        ln=ln_f0, alpha=alpha)

    dy, dres, acca, acc2f0, (scatter_f0, scatter_d0), dfcw0, dfcb0 = ffn_bwd(
        dy, dres, x1, sc_c0, sh_c0, u0_0, vg_0, t_0, g_colf0, g_rowf0, ln_a, 0, "0")

    dy0 = _mm_nt_row(dy, g_out, d_loc, 0, "a_out_dx")
    dbcv, dacw, dacb = _gateconv_bwd(bcv, dy0, a_cw, _after(a_conv_b, scatter_f0[-1]), "a_conv_bwd")
    dx0, acc2a = _mm_nt_mod([dbcv], g_in, (0,), x0, sc_t0, dres, "a_in_dx")

    def dmod_row(acc2_t, acc_t, acc2_c, acc_c):
        return jnp.concatenate([acc2_t[1], acc2_t[0], acc_t[2], acc2_c[1], acc2_c[0], acc_c[2]])

    dmod = jnp.stack([dmod_row(acc2a, acca, acc2f0, accf0), dmod_row(acc2b, accb, acc2f1, accf1)])

    def unpad_f(a):
        return a.reshape(a.shape[:-1] + (NDEV, f_pad))[..., :f_loc].reshape(a.shape[:-1] + (NDEV * f_loc,))

    small_grads = [
        dmod,
        jnp.stack([acca[0], accb[0]]), jnp.stack([acca[1], accb[1]]),
        jnp.stack([accf0[0], accf1[0]]), jnp.stack([accf0[1], accf1[1]]),
        dacb,
        unpad_f(jnp.concatenate([dfcb0, dfcb1], axis=0)),
        dacw, dbb1, dbcw, dbcb, dblg, dblb, accb[3:4],
        jnp.stack([dfcw0, dfcw1]),
        loss_part[0:1, 0:1],
    ]
    small_grad_shapes = [tuple(g.shape) for g in small_grads]
    gather_small = _exchange_start([_pack(small_grads)], "gather", "gather_small_start")

    dw_in = _mm_tn_col(x0, _after(sc_t0, gather_small[-1]), sh_t0, dbcv, "a_in_dw")
    dw_out = _mm_tn_row(y0, dy, d_loc, d_loc, "a_out_dw")
    scatter_a = _exchange_start([dw_in, dw_out], "scatter", "scatter_a_start")

    grads, deltas, new_m, new_v = {}, {}, {}, {}

    def adamw(k, glist, transposed=False):
        def view(a):
            a = jnp.swapaxes(a, 1, 2) if transposed else a
            return a.reshape(len(glist), -1, a.shape[-1])

        w = view(weights[k])
        outs = _adamw(w, [g.reshape(g.shape[0], -1, w.shape[-1]) for g in glist],
                      view(mom_m[k]), view(mom_v[k]), "adamw_" + k)
        if transposed:
            outs = [jnp.swapaxes(o, 1, 2) for o in outs]
        grads[k], deltas[k], new_m[k], new_v[k] = (o.reshape(weights[k].shape) for o in outs)

    r_up1, r_gate1, _ = _exchange_wait(scatter_f1, scatter_a[-1], "scatter", "scatter_f1_wait")
    r_down1, _ = _exchange_wait(scatter_d1, r_gate1, "scatter", "scatter_d1_wait")
    r_pw1, r_pw2, _ = _exchange_wait(scatter_b, r_down1, "scatter", "scatter_b_wait")
    adamw("b_w_pw1", [r_pw1])
    adamw("b_w_pw2", [r_pw2])
    r_down0, _ = _exchange_wait(scatter_d0, deltas["b_w_pw2"], "scatter", "scatter_d0_wait")
    adamw("f_w_down", [r_down0, r_down1])
    r_up0, r_gate0, _ = _exchange_wait(scatter_f0, deltas["f_w_down"], "scatter", "scatter_f0_wait")
    adamw("f_w_up", [r_up0, r_up1], transposed=True)
    adamw("f_w_gate", [r_gate0, r_gate1], transposed=True)

    sg_all, _ = _exchange_wait(gather_small, deltas["f_w_gate"], "gather", "gather_small_wait")
    sg_sum = _sum_parts(sg_all, "sum_small_grads")
    (g_ada_b, g_ltg, g_ltb, g_lcg, g_lcb, g_acb, g_fcb, g_acw, g_bb1, g_bcw, g_bcb, g_blg, g_blb, g_bb2,
     g_fcw, loss_all) = _unpack(sg_sum, small_grad_shapes)
    loss = loss_all[0, 0]

    def my_cols(a, width):
        return lax.dynamic_slice_in_dim(a, me * width, width, axis=a.ndim - 1)

    g_fcw_loc = my_cols(g_fcw, f_pad)[..., :f_loc]
    small = dict(
        ada_b=g_ada_b, ln_tok_g=g_ltg, ln_tok_b=g_ltb, ln_ch_g=g_lcg, ln_ch_b=g_lcb, a_conv_b=g_acb, f_conv_b=g_fcb,
        a_conv_w=my_cols(g_acw, d_loc)[None], b_b_pw1=my_cols(g_bb1, 2 * d_loc), b_conv_w=my_cols(g_bcw, d_loc)[None],
        b_conv_b=my_cols(g_bcb, d_loc), b_ln_g=my_cols(g_blg, d_loc), b_ln_b=my_cols(g_blb, d_loc),
        b_b_pw2=my_cols(g_bb2, d_loc), f_conv_w=g_fcw_loc)

    dmod_all = sg_all.reshape(NDEV, -1)[:, :depth * 6 * d].reshape(NDEV, depth, 6 * d)
    dmod_cols = my_cols(dmod_all, n_ada).transpose(1, 0, 2)
    g_ada_w = _ada_bwd(c_all.T, dmod_cols, "ada_bwd")

    adamw("ada_w", [g_ada_w[0:1], g_ada_w[1:2]])

    def rows_cols(a):
        return a.reshape(-1, a.shape[-1])

    small_keys = list(small)
    small_outs = _adamw_small([rows_cols(weights[k]) for k in small_keys], [rows_cols(small[k]) for k in small_keys],
                              [rows_cols(mom_m[k]) for k in small_keys], [rows_cols(mom_v[k]) for k in small_keys],
                              "adamw_small")
    for i, k in enumerate(small_keys):
        grads[k] = small[k].reshape(weights[k].shape)
        deltas[k], new_m[k], new_v[k] = (o.reshape(weights[k].shape) for o in small_outs[3 * i:3 * i + 3])

    r_in, r_out, _ = _exchange_wait(scatter_a, deltas["ada_w"], "scatter", "scatter_a_wait")
    adamw("a_w_in", [r_in])
    adamw("a_w_out", [r_out])

    return (loss, dx0[None], *[grads[k] for k in names], *[deltas[k] for k in names],
            *[new_m[k] for k in names], *[new_v[k] for k in names])
```

```python
import jax
import jax.numpy as jnp
from jax import lax
from jax.experimental import pallas as pl
from jax.experimental.pallas import tpu as pltpu

NDEV = 8
MESH_AXES = ("x", "y", "c")
LANES = 128
SUBLANES = 8
VMEM_LIMIT = 56 * 1024 * 1024
LN_EPS = 1e-5
SHORT_PAD = 16
LONG_PAD = 32
CHUNK = 16
DW_ROWS = 1024
ADAM_LR, ADAM_B1, ADAM_B2, ADAM_EPS, ADAM_WD, ADAM_STEP = 0.001, 0.9, 0.999, 1e-08, 0.01, 10

F32 = jnp.float32
BF16 = jnp.bfloat16
MESH = pl.DeviceIdType.MESH
NT = (((1,), (1,)), ((), ()))
TN = (((0,), (0,)), ((), ()))


def _tile(n, target, mult=SUBLANES):
    best = None
    for t in range(mult, min(n, target) + 1, mult):
        if n % t == 0:
            best = t
    return best if best is not None else n


def _full(shape):
    nd = len(shape)
    return pl.BlockSpec(shape, lambda *_: (0,) * nd)


def _cp(*sem):
    return pltpu.CompilerParams(dimension_semantics=sem, vmem_limit_bytes=VMEM_LIMIT)


def _sigmoid(x):
    return 1.0 / (1.0 + jnp.exp(-x))


def _peer(x, y, c, d):
    return ((1 - x) if d & 4 else x, (1 - y) if d & 2 else y, (1 - c) if d & 1 else c)


def _lin(p):
    return 4 * p[0] + 2 * p[1] + p[2]


CHIP_MASKS = (2, 4, 6)
MODES_PER_ARRAY = {"gather": NDEV - 1, "scatter": NDEV - 1, "gather_chips": 1 + len(CHIP_MASKS),
                   "forward": len(CHIP_MASKS)}


def _transfers(mode):
    x, y, c = (lax.axis_index(a) for a in MESH_AXES)
    me = _lin((x, y, c))
    if mode == "forward":
        sibling = (x, y, 1 - c)
        return [(sibling, ("land", _lin(_peer(x, y, c, q))), _lin(_peer(x, y, c, q)), _lin(_peer(x, y, c, q ^ 1)))
                for q in CHIP_MASKS]
    masks = (1,) + CHIP_MASKS if mode == "gather_chips" else range(1, NDEV)
    out = []
    for d in masks:
        peer = _peer(x, y, c, d)
        source = ("block", _lin(peer)) if mode == "scatter" else ("whole", None)
        out.append((peer, source, me, _lin(peer)))
    return out


def _remote_copies(src_refs, land_refs, send_sems, recv_sems, mode):
    transfers = _transfers(mode)
    sends, recvs = [], []
    for i, land_ref in enumerate(land_refs):
        for t, (peer, (kind, slot), there, here) in enumerate(transfers):
            k = i * len(transfers) + t
            src = land_ref.at[slot] if kind == "land" else src_refs[i].at[slot] if kind == "block" else src_refs[i]
            for dst_slot, out in ((there, sends), (here, recvs)):
                out.append(pltpu.make_async_remote_copy(
                    src_ref=src, dst_ref=land_ref.at[dst_slot], send_sem=send_sems.at[k], recv_sem=recv_sems.at[k],
                    device_id=peer, device_id_type=MESH))
    return sends, recvs


def _exchange(srcs, mode, name):
    n = len(srcs)
    gather = mode == "gather"

    def body(*refs):
        src_refs, out_refs, token = refs[:n], refs[n:2 * n], refs[2 * n]
        send_sems, recv_sems, local_sems = refs[2 * n + 1:]
        me = _lin(tuple(lax.axis_index(a) for a in MESH_AXES))
        local = []
        for i in range(n):
            mine = src_refs[i] if gather else src_refs[i].at[me]
            cp = pltpu.make_async_copy(mine, out_refs[i].at[me], local_sems.at[i])
            cp.start()
            local.append(cp)
        sends, recvs = _remote_copies(src_refs, out_refs, send_sems, recv_sems, mode)
        for snd in sends:
            snd.start()
        token[...] = jnp.zeros_like(token)
        for snd, rcv in zip(sends, recvs):
            snd.wait_send()
            rcv.wait_recv()
        for cp in local:
            cp.wait()

    out_shape = [jax.ShapeDtypeStruct(((NDEV,) + s.shape) if gather else s.shape, s.dtype) for s in srcs]
    out_shape.append(jax.ShapeDtypeStruct((SUBLANES, LANES), F32))
    any_spec = pl.BlockSpec(memory_space=pl.ANY)
    return pl.pallas_call(
        body, name=name, out_shape=out_shape,
        in_specs=[any_spec] * n, out_specs=[any_spec] * n + [pl.BlockSpec(memory_space=pltpu.VMEM)],
        scratch_shapes=[pltpu.SemaphoreType.DMA((n * (NDEV - 1),)),
                        pltpu.SemaphoreType.DMA((n * (NDEV - 1),)),
                        pltpu.SemaphoreType.DMA((n,))],
    )(*srcs)


HBM_SPEC = pl.BlockSpec(memory_space=pltpu.HBM)
SEM_SPEC = pl.BlockSpec(memory_space=pltpu.SEMAPHORE)
SIDE_EFFECT = pltpu.SideEffectType.DATAFLOW_SIDE_EFFECTING


def _exchange_start(arrays, mode, name):
    me = _lin(tuple(lax.axis_index(a) for a in MESH_AXES))
    if mode == "forward":
        srcs, lands = [], list(arrays)
    else:
        srcs, lands = list(arrays), []
        for s in srcs:
            own = lax.dynamic_index_in_dim(s, me, 0, keepdims=False) if mode == "scatter" else s
            shape = s.shape if mode == "scatter" else (NDEV,) + s.shape
            lands.append(lax.dynamic_update_index_in_dim(lax.empty(shape, s.dtype), own, me, 0))
    ns, n = len(srcs), len(lands)

    def body(*refs):
        src_refs, land_refs = refs[:ns], refs[ns:ns + n]
        send_sems, recv_sems, token = refs[ns + n], refs[ns + n + 1], refs[-1]
        sends, _ = _remote_copies(src_refs, land_refs, send_sems, recv_sems, mode)
        for snd in sends:
            snd.start()
        token[...] = jnp.zeros_like(token)

    operands = [pltpu.with_memory_space_constraint(a, pltpu.HBM) for a in srcs + lands]
    nsem = n * MODES_PER_ARRAY[mode]
    return pl.pallas_call(
        body, name=name,
        out_shape=(pltpu.SemaphoreType.DMA((nsem,)), pltpu.SemaphoreType.DMA((nsem,)),
                   *[pltpu.HBM(a.shape, a.dtype) for a in operands],
                   jax.ShapeDtypeStruct((SUBLANES, LANES), F32)),
        in_specs=[HBM_SPEC] * (ns + n),
        out_specs=(SEM_SPEC, SEM_SPEC, *([HBM_SPEC] * (ns + n)), pl.BlockSpec(memory_space=pltpu.VMEM)),
        input_output_aliases={i: 2 + i for i in range(ns + n)},
        compiler_params=pltpu.CompilerParams(has_side_effects=SIDE_EFFECT),
    )(*operands)


def _exchange_wait(handle, after, mode, name):
    send_sems, recv_sems, *thru = handle[:-1]
    n = len(thru) if mode == "forward" else len(thru) // 2
    ns = len(thru) - n

    def body(*refs):
        src_refs, land_refs = refs[:ns], refs[ns:ns + n]
        sends, recvs = _remote_copies(src_refs, land_refs, refs[ns + n], refs[ns + n + 1], mode)
        for snd, rcv in zip(sends, recvs):
            snd.wait_send()
            rcv.wait_recv()
        refs[-1][...] = jnp.zeros_like(refs[-1])

    outs = pl.pallas_call(
        body, name=name,
        out_shape=(*[pltpu.HBM(a.shape, a.dtype) for a in thru], jax.ShapeDtypeStruct((SUBLANES, LANES), F32)),
        in_specs=[HBM_SPEC] * (ns + n) + [SEM_SPEC, SEM_SPEC, pl.BlockSpec(memory_space=pl.ANY)],
        out_specs=[HBM_SPEC] * (ns + n) + [pl.BlockSpec(memory_space=pltpu.VMEM)],
        input_output_aliases={i: i for i in range(ns + n)},
        compiler_params=pltpu.CompilerParams(has_side_effects=SIDE_EFFECT),
    )(*thru, send_sems, recv_sems, after)
    return outs[ns:]


def _after(value, token):
    return value + token[0, 0]


ANY_SPEC = pl.BlockSpec(memory_space=pl.ANY)


def _load_cols(wg_ref, widx, w_ref, sems):
    n = wg_ref.shape[-1]
    copies = [pltpu.make_async_copy(wg_ref.at[k, widx], w_ref.at[:, pl.ds(k * n, n)], sems.at[k])
              for k in range(NDEV)]
    for cp in copies:
        cp.start()
    for cp in copies:
        cp.wait()


def _load_rows(wg_ref, r, ridx, w_ref, sems):
    copies = [pltpu.make_async_copy(wg_ref.at[k, pl.ds(ridx * r, r)], w_ref.at[pl.ds(k * r, r)], sems.at[k])
              for k in range(NDEV)]
    for cp in copies:
        cp.start()
    for cp in copies:
        cp.wait()


def _mm_fwd(x, sc, sh, bias, wg, widxs, name):
    s_len, kdim = x.shape
    ncol = NDEV * wg.shape[-1]
    tm = _tile(s_len, 512)
    nw = len(widxs)

    def body(x_ref, sc_ref, sh_ref, b_ref, wg_ref, *rest):
        o_refs, w_refs, sems = rest[:nw], rest[nw:2 * nw], rest[2 * nw]

        @pl.when(pl.program_id(0) == 0)
        def _():
            for i, w_ref in enumerate(w_refs):
                _load_cols(wg_ref, widxs[i], w_ref, sems.at[i])

        h = (x_ref[...] * (1.0 + sc_ref[...]) + sh_ref[...]).astype(BF16)
        for w_ref, o_ref in zip(w_refs, o_refs):
            o_ref[...] = (jnp.dot(h, w_ref[...], preferred_element_type=F32) + b_ref[...]).astype(BF16)

    return pl.pallas_call(
        body, name=name, grid=(s_len // tm,),
        in_specs=[pl.BlockSpec((tm, kdim), lambda i: (i, 0)), _full((1, kdim)), _full((1, kdim)),
                  _full((1, ncol)), ANY_SPEC],
        out_specs=[pl.BlockSpec((tm, ncol), lambda i: (i, 0))] * nw,
        out_shape=[jax.ShapeDtypeStruct((s_len, ncol), BF16)] * nw,
        scratch_shapes=[pltpu.VMEM((kdim, ncol), BF16)] * nw + [pltpu.SemaphoreType.DMA((nw, NDEV))],
        compiler_params=_cp("arbitrary"),
    )(x, sc, sh, bias, wg)


def _mm_ln(a, wg, r, ridx, xres, gate, gam, bet, bias, alpha, name):
    s_len = a.shape[0]
    d = wg.shape[-1]
    tm = _tile(s_len, 512)

    def body(a_ref, wg_ref, x_ref, g_ref, gam_ref, bet_ref, b_ref, y_ref, xo_ref, xh_ref, rs_ref, w_ref, sems):
        @pl.when(pl.program_id(0) == 0)
        def _():
            _load_rows(wg_ref, r, ridx, w_ref, sems)

        y = jnp.dot(a_ref[...], w_ref[...], preferred_element_type=F32) + b_ref[...]
        z = alpha * x_ref[...] + g_ref[...] * y
        mu = jnp.mean(z, axis=-1, keepdims=True)
        zc = z - mu
        var = jnp.mean(zc * zc, axis=-1, keepdims=True)
        rstd = lax.rsqrt(var + LN_EPS)
        xh = zc * rstd
        y_ref[...] = y.astype(BF16)
        xh_ref[...] = xh
        rs_ref[...] = rstd
        xo_ref[...] = xh * gam_ref[...] + bet_ref[...]

    row = pl.BlockSpec((tm, d), lambda i: (i, 0))
    vec = _full((1, d))
    return pl.pallas_call(
        body, name=name, grid=(s_len // tm,),
        in_specs=[pl.BlockSpec((tm, NDEV * r), lambda i: (i, 0)), ANY_SPEC, row, vec, vec, vec, vec],
        out_specs=[row, row, row, pl.BlockSpec((tm, 1), lambda i: (i, 0))],
        out_shape=[jax.ShapeDtypeStruct((s_len, d), BF16)] + [jax.ShapeDtypeStruct((s_len, d), F32)] * 2
        + [jax.ShapeDtypeStruct((s_len, 1), F32)],
        scratch_shapes=[pltpu.VMEM((NDEV * r, d), BF16), pltpu.SemaphoreType.DMA((NDEV,))],
        compiler_params=_cp("arbitrary"),
    )(a, wg, xres, gate, gam, bet, bias)


def _ln_in_specs(tm, d):
    row = pl.BlockSpec((tm, d), lambda i: (i, 0))
    return [row, pl.BlockSpec((tm, 1), lambda i: (i, 0)), _full((1, d)), row, _full((1, d))]


def _ln_out_specs(s_len, tm, d):
    row = pl.BlockSpec((tm, d), lambda i: (i, 0))
    return ([row, row, _full((SUBLANES, d))],
            [jax.ShapeDtypeStruct((s_len, d), BF16), jax.ShapeDtypeStruct((s_len, d), F32),
             jax.ShapeDtypeStruct((SUBLANES, d), F32)])


def _ln_bwd_rows(dxo, ln_refs, out_refs, alpha):
    xh_ref, rs_ref, gam_ref, y_ref, g_ref = ln_refs
    dy_ref, dres_ref, acc_ref = out_refs
    xh = xh_ref[...]
    dxh = dxo * gam_ref[...]
    m1 = jnp.mean(dxh, axis=-1, keepdims=True)
    m2 = jnp.mean(dxh * xh, axis=-1, keepdims=True)
    dz = rs_ref[...] * (dxh - m1 - xh * m2)
    dy = g_ref[...] * dz
    dy_ref[...] = dy.astype(BF16)
    dres_ref[...] = alpha * dz
    acc_ref[0:1, :] += jnp.sum(dxo * xh, axis=0, keepdims=True)
    acc_ref[1:2, :] += jnp.sum(dxo, axis=0, keepdims=True)
    acc_ref[2:3, :] += jnp.sum(dz * y_ref[...].astype(F32), axis=0, keepdims=True)
    acc_ref[3:4, :] += jnp.sum(dy, axis=0, keepdims=True)


def _mm_nt_row(dy, wg, r, ridx, name):
    s_len, d = dy.shape
    tm = _tile(s_len, 512)

    def body(dy_ref, wg_ref, o_ref, w_ref, sems):
        @pl.when(pl.program_id(0) == 0)
        def _():
            _load_rows(wg_ref, r, ridx, w_ref, sems)

        o_ref[...] = lax.dot_general(dy_ref[...], w_ref[...], NT, preferred_element_type=F32).astype(BF16)

    return pl.pallas_call(
        body, name=name, grid=(s_len // tm,),
        in_specs=[pl.BlockSpec((tm, d), lambda i: (i, 0)), ANY_SPEC],
        out_specs=pl.BlockSpec((tm, NDEV * r), lambda i: (i, 0)),
        out_shape=jax.ShapeDtypeStruct((s_len, NDEV * r), BF16),
        scratch_shapes=[pltpu.VMEM((NDEV * r, d), BF16), pltpu.SemaphoreType.DMA((NDEV,))],
        compiler_params=_cp("arbitrary"),
    )(dy, wg)


def _mm_nt_mod(dos, wg, widxs, xin, sc, dres, name, ln=None, alpha=None):
    s_len, kdim = xin.shape
    ncol = NDEV * wg.shape[-1]
    tm = _tile(s_len, 512)
    nw = len(widxs)
    nln = 0 if ln is None else len(ln)
    nout = 2 if ln is None else 4

    def body(*refs):
        do_refs, wg_ref = refs[:nw], refs[nw]
        x_ref, sc_ref, dres_ref = refs[nw + 1:nw + 4]
        ln_refs = refs[nw + 4:nw + 4 + nln]
        out_refs = refs[nw + 4 + nln:nw + 4 + nln + nout]
        w_refs, sems = refs[nw + 4 + nln + nout:-1], refs[-1]
        acc_ref = out_refs[-1]

        @pl.when(pl.program_id(0) == 0)
        def _():
            for ref in out_refs[nout // 2:]:
                ref[...] = jnp.zeros_like(ref)
            for i, w_ref in enumerate(w_refs):
                _load_cols(wg_ref, widxs[i], w_ref, sems.at[i])

        dh = None
        for do_ref, w_ref in zip(do_refs, w_refs):
            p = lax.dot_general(do_ref[...], w_ref[...], NT, preferred_element_type=F32)
            dh = p if dh is None else dh + p
        dx = dh * (1.0 + sc_ref[...]) + dres_ref[...]
        if ln is None:
            out_refs[0][...] = dx
        else:
            _ln_bwd_rows(dx, ln_refs, out_refs[0:3], alpha)
        acc_ref[0:1, :] += jnp.sum(dh * x_ref[...], axis=0, keepdims=True)
        acc_ref[1:2, :] += jnp.sum(dh, axis=0, keepdims=True)

    row = pl.BlockSpec((tm, kdim), lambda i: (i, 0))
    if ln is None:
        out_specs, out_shape = [row], [jax.ShapeDtypeStruct((s_len, kdim), F32)]
    else:
        out_specs, out_shape = _ln_out_specs(s_len, tm, kdim)
    return pl.pallas_call(
        body, name=name, grid=(s_len // tm,),
        in_specs=[pl.BlockSpec((tm, ncol), lambda i: (i, 0))] * nw + [ANY_SPEC, row, _full((1, kdim)), row]
        + ([] if ln is None else _ln_in_specs(tm, kdim)),
        out_specs=out_specs + [_full((SUBLANES, kdim))],
        out_shape=out_shape + [jax.ShapeDtypeStruct((SUBLANES, kdim), F32)],
        scratch_shapes=[pltpu.VMEM((kdim, ncol), BF16)] * nw + [pltpu.SemaphoreType.DMA((nw, NDEV))],
        compiler_params=_cp("arbitrary"),
    )(*dos, wg, xin, sc, dres, *([] if ln is None else ln))


def _mm_tn_col(x, sc, sh, do, name):
    s_len, kdim = x.shape
    n = do.shape[1] // NDEV
    ts = _tile(s_len, DW_ROWS)
    nsteps = s_len // ts

    def body(x_ref, sc_ref, sh_ref, do_ref, o_ref, acc_ref):
        @pl.when(pl.program_id(0) == 0)
        def _():
            acc_ref[...] = jnp.zeros_like(acc_ref)

        h = (x_ref[...] * (1.0 + sc_ref[...]) + sh_ref[...]).astype(BF16)
        acc_ref[...] += lax.dot_general(h, do_ref[...], TN, preferred_element_type=F32)

        @pl.when(pl.program_id(0) == nsteps - 1)
        def _():
            for k in range(NDEV):
                o_ref[k] = acc_ref[:, k * n:(k + 1) * n].astype(BF16)

    return pl.pallas_call(
        body, name=name, grid=(nsteps,),
        in_specs=[pl.BlockSpec((ts, kdim), lambda i: (i, 0)), _full((1, kdim)), _full((1, kdim)),
                  pl.BlockSpec((ts, NDEV * n), lambda i: (i, 0))],
        out_specs=_full((NDEV, kdim, n)),
        out_shape=jax.ShapeDtypeStruct((NDEV, kdim, n), BF16),
        scratch_shapes=[pltpu.VMEM((kdim, NDEV * n), F32)],
        compiler_params=_cp("arbitrary"),
    )(x, sc, sh, do)


def _mm_tn_col_t(x, sc, sh, do, rows_out, name):
    s_len, kdim = x.shape
    n = do.shape[1] // NDEV
    ts = _tile(s_len, DW_ROWS)
    nsteps = s_len // ts

    def body(x_ref, sc_ref, sh_ref, do_ref, o_ref, acc_ref):
        @pl.when(pl.program_id(0) == 0)
        def _():
            acc_ref[...] = jnp.zeros_like(acc_ref)

        h = (x_ref[...] * (1.0 + sc_ref[...]) + sh_ref[...]).astype(BF16)
        acc_ref[...] += lax.dot_general(do_ref[...], h, TN, preferred_element_type=F32)

        @pl.when(pl.program_id(0) == nsteps - 1)
        def _():
            for k in range(NDEV):
                o_ref[k] = acc_ref[k * n:k * n + rows_out, :].astype(BF16)

    return pl.pallas_call(
        body, name=name, grid=(nsteps,),
        in_specs=[pl.BlockSpec((ts, kdim), lambda i: (i, 0)), _full((1, kdim)), _full((1, kdim)),
                  pl.BlockSpec((ts, NDEV * n), lambda i: (i, 0))],
        out_specs=_full((NDEV, rows_out, kdim)),
        out_shape=jax.ShapeDtypeStruct((NDEV, rows_out, kdim), BF16),
        scratch_shapes=[pltpu.VMEM((NDEV * n, kdim), F32)],
        compiler_params=_cp("arbitrary"),
    )(x, sc, sh, do)


def _mm_tn_row(a, dy, r, rows_out, name):
    s_len, d = dy.shape
    ts = _tile(s_len, DW_ROWS)
    nsteps = s_len // ts

    def body(a_ref, dy_ref, o_ref, acc_ref):
        @pl.when(pl.program_id(0) == 0)
        def _():
            acc_ref[...] = jnp.zeros_like(acc_ref)

        acc_ref[...] += lax.dot_general(a_ref[...], dy_ref[...], TN, preferred_element_type=F32)

        @pl.when(pl.program_id(0) == nsteps - 1)
        def _():
            for k in range(NDEV):
                o_ref[k] = acc_ref[k * r:k * r + rows_out, :].astype(BF16)

    return pl.pallas_call(
        body, name=name, grid=(nsteps,),
        in_specs=[pl.BlockSpec((ts, NDEV * r), lambda i: (i, 0)), pl.BlockSpec((ts, d), lambda i: (i, 0))],
        out_specs=_full((NDEV, rows_out, d)),
        out_shape=jax.ShapeDtypeStruct((NDEV, rows_out, d), BF16),
        scratch_shapes=[pltpu.VMEM((NDEV * r, d), F32)],
        compiler_params=_cp("arbitrary"),
    )(a, dy)


def _prev_spec(ts, pad, cb, col):
    return pl.BlockSpec((pad, cb), lambda *g: (jnp.maximum(g[-1] * (ts // pad) - 1, 0), col(g)))


def _next_spec(ts, pad, cb, col, s_len):
    return pl.BlockSpec((pad, cb), lambda *g: (jnp.minimum((g[-1] + 1) * (ts // pad), s_len // pad - 1), col(g)))


class _F32Loads:
    def __init__(self, ref):
        self.ref = ref

    def __getitem__(self, idx):
        return self.ref[idx].astype(F32)


def _direct(buf_ref):
    return lambda off, rows: buf_ref[off:off + rows, :]


def _make_shifts(sh_ref, nrows):
    for r in range(1, SUBLANES):
        sh_ref[r, 0:nrows - SUBLANES, :] = sh_ref[0, r:r + nrows - SUBLANES, :]


def _shifted(sh_ref):
    def read(off, rows):
        r = off % SUBLANES
        return sh_ref[r, off - r:off - r + rows, :]
    return read


def _conv_fwd_rows(read, w_ref, b_ref, ktaps, pad, r0, rows):
    acc = None
    for j in range(ktaps):
        term = w_ref[ktaps - 1 - j:ktaps - j, :] * read(pad - j + r0, rows)
        acc = term if acc is None else acc + term
    return acc + b_ref[...]


def _conv_bwd_rows(read, x_rows, w_ref, dwacc_ref, ktaps, r0, rows):
    acc = None
    for j in range(ktaps):
        sl = read(j + r0, rows)
        term = w_ref[ktaps - 1 - j:ktaps - j, :] * sl
        acc = term if acc is None else acc + term
        prod = x_rows * sl
        fold = prod[0:SUBLANES]
        for q in range(1, rows // SUBLANES):
            fold = fold + prod[q * SUBLANES:(q + 1) * SUBLANES]
        tap = ktaps - 1 - j
        dwacc_ref[tap * SUBLANES:(tap + 1) * SUBLANES, :] += fold
    return acc


def _flush_dw(dwacc_ref, dw_ref, ktaps):
    for tap in range(ktaps):
        dw_ref[tap:tap + 1, :] = jnp.sum(dwacc_ref[tap * SUBLANES:(tap + 1) * SUBLANES, :], axis=0, keepdims=True)


def _gateconv_fwd(bcv, cw, cb, name):
    s_len, d3 = bcv.shape
    d = d3 // 3
    ktaps = cw.shape[0]
    pad = SHORT_PAD
    ts = _tile(s_len, 256)

    def body(gb_ref, gc_ref, v_ref, gcp_ref, vp_ref, w_ref, b_ref, o_ref, pbuf):
        gb_ref, gc_ref, v_ref, gcp_ref, vp_ref = map(_F32Loads, (gb_ref, gc_ref, v_ref, gcp_ref, vp_ref))
        s = pl.program_id(0)
        pbuf[0:pad, :] = jnp.where(s > 0, gcp_ref[...] * vp_ref[...], 0.0)
        pbuf[pad:pad + ts, :] = gc_ref[...] * v_ref[...]
        for r0 in range(0, ts, CHUNK):
            q = _conv_fwd_rows(_direct(pbuf), w_ref, b_ref, ktaps, pad, r0, CHUNK)
            o_ref[r0:r0 + CHUNK, :] = (gb_ref[r0:r0 + CHUNK, :] * q).astype(BF16)

    def cur(part):
        return pl.BlockSpec((ts, d), lambda s: (s, part))

    return pl.pallas_call(
        body, name=name, grid=(s_len // ts,),
        in_specs=[cur(0), cur(1), cur(2),
                  _prev_spec(ts, pad, d, lambda g: 1), _prev_spec(ts, pad, d, lambda g: 2),
                  _full((ktaps, d)), _full((1, d))],
        out_specs=pl.BlockSpec((ts, d), lambda s: (s, 0)),
        out_shape=jax.ShapeDtypeStruct((s_len, d), BF16),
        scratch_shapes=[pltpu.VMEM((pad + ts, d), F32)],
        compiler_params=_cp("parallel"),
    )(bcv, bcv, bcv, bcv, bcv, cw, cb)


def _gateconv_bwd(bcv, dy0, cw, cb, name):
    s_len, d3 = bcv.shape
    d = d3 // 3
    ktaps = cw.shape[0]
    pad = SHORT_PAD
    ts = _tile(s_len, 256)
    nsteps = s_len // ts

    def body(gb_ref, gc_ref, v_ref, gcp_ref, vp_ref, gbn_ref, dy_ref, dyn_ref, w_ref, b_ref,
             o_ref, dw_ref, db_ref, pbuf, dqbuf, dwacc):
        gb_ref, gc_ref, v_ref, gcp_ref, vp_ref, gbn_ref, dy_ref, dyn_ref = map(
            _F32Loads, (gb_ref, gc_ref, v_ref, gcp_ref, vp_ref, gbn_ref, dy_ref, dyn_ref))
        s = pl.program_id(0)

        @pl.when(s == 0)
        def _():
            dwacc[...] = jnp.zeros_like(dwacc)
            db_ref[...] = jnp.zeros_like(db_ref)

        pbuf[0:pad, :] = jnp.where(s > 0, gcp_ref[...] * vp_ref[...], 0.0)
        pbuf[pad:pad + ts, :] = gc_ref[...] * v_ref[...]
        dq = dy_ref[...] * gb_ref[...]
        dqbuf[0:ts, :] = dq
        dqbuf[ts:ts + pad, :] = jnp.where(s < nsteps - 1, dyn_ref[...] * gbn_ref[...], 0.0)
        db_ref[...] += jnp.sum(dq, axis=0, keepdims=True)
        for r0 in range(0, ts, CHUNK):
            rows = slice(r0, r0 + CHUNK)
            q = _conv_fwd_rows(_direct(pbuf), w_ref, b_ref, ktaps, pad, r0, CHUNK)
            o_ref[rows, 0:d] = (dy_ref[rows, :] * q).astype(BF16)
            dp = _conv_bwd_rows(_direct(dqbuf), pbuf[pad + r0:pad + r0 + CHUNK, :], w_ref, dwacc, ktaps, r0, CHUNK)
            o_ref[rows, d:2 * d] = (dp * v_ref[rows, :]).astype(BF16)
            o_ref[rows, 2 * d:3 * d] = (dp * gc_ref[rows, :]).astype(BF16)

        @pl.when(s == nsteps - 1)
        def _():
            _flush_dw(dwacc, dw_ref, ktaps)

    def cur(part):
        return pl.BlockSpec((ts, d), lambda s: (s, part))

    return pl.pallas_call(
        body, name=name, grid=(nsteps,),
        in_specs=[cur(0), cur(1), cur(2),
                  _prev_spec(ts, pad, d, lambda g: 1), _prev_spec(ts, pad, d, lambda g: 2),
                  _next_spec(ts, pad, d, lambda g: 0, s_len),
                  cur(0), _next_spec(ts, pad, d, lambda g: 0, s_len),
                  _full((ktaps, d)), _full((1, d))],
        out_specs=[pl.BlockSpec((ts, d3), lambda s: (s, 0)), _full((ktaps, d)), _full((1, d))],
        out_shape=[jax.ShapeDtypeStruct((s_len, d3), BF16), jax.ShapeDtypeStruct((ktaps, d), F32),
                   jax.ShapeDtypeStruct((1, d), F32)],
        scratch_shapes=[pltpu.VMEM((pad + ts, d), F32), pltpu.VMEM((ts + pad, d), F32),
                        pltpu.VMEM((ktaps * SUBLANES, d), F32)],
        compiler_params=_cp("arbitrary"),
    )(bcv, bcv, bcv, bcv, bcv, bcv, dy0, dy0, cw, cb)


class _Cols:
    def __init__(self, ref, cols):
        self.ref, self.cols = ref, cols

    def __getitem__(self, idx):
        return self.ref[slice(None) if idx is Ellipsis else idx[0], self.cols]

    def __setitem__(self, idx, value):
        self.ref[idx[0], self.cols] = value


def _ffn_tail_fwd(u0, vg, cw, cb, wg, xres, gate, gam, bet, alpha, name):
    s_len, f = u0.shape
    d = wg.shape[-1]
    r = f // NDEV
    ktaps = cw.shape[0]
    pad = SHORT_PAD
    tm = _tile(s_len, 256)
    cbk = 1024 if f % 1024 == 0 else f

    def body(u_ref, up_ref, vg_ref, cw_ref, cb_ref, wg_ref, x_ref, g_ref, gam_ref, bet_ref,
             t_ref, uc_ref, y_ref, xo_ref, xh_ref, rs_ref, ubuf, w_ref, sems):
        u_ref, up_ref, vg_ref = map(_F32Loads, (u_ref, up_ref, vg_ref))
        s = pl.program_id(0)

        @pl.when(s == 0)
        def _():
            _load_rows(wg_ref, r, 0, w_ref, sems)

        ubuf[0:pad, :] = jnp.where(s > 0, up_ref[...], 0.0)
        ubuf[pad:pad + tm, :] = u_ref[...]
        y = None
        for c0 in range(0, f, cbk):
            cols = slice(c0, c0 + cbk)
            read = _direct(_Cols(ubuf, cols))
            for r0 in range(0, tm, CHUNK):
                rows = slice(r0, r0 + CHUNK)
                u = _conv_fwd_rows(read, _Cols(cw_ref, cols), _Cols(cb_ref, cols), ktaps, pad, r0, CHUNK)
                t_ref[rows, cols] = (u * _sigmoid(u) * vg_ref[rows, cols]).astype(BF16)
                uc_ref[rows, cols] = u.astype(BF16)
            p = jnp.dot(t_ref[:, cols], w_ref[cols, :], preferred_element_type=F32)
            y = p if y is None else y + p
        z = alpha * x_ref[...] + g_ref[...] * y
        mu = jnp.mean(z, axis=-1, keepdims=True)
        zc = z - mu
        var = jnp.mean(zc * zc, axis=-1, keepdims=True)
        rstd = lax.rsqrt(var + LN_EPS)
        xh = zc * rstd
        y_ref[...] = y.astype(BF16)
        xh_ref[...] = xh
        rs_ref[...] = rstd
        xo_ref[...] = xh * gam_ref[...] + bet_ref[...]

    wide = pl.BlockSpec((tm, f), lambda i: (i, 0))
    row = pl.BlockSpec((tm, d), lambda i: (i, 0))
    vec = _full((1, d))
    return pl.pallas_call(
        body, name=name, grid=(s_len // tm,),
        in_specs=[wide, _prev_spec(tm, pad, f, lambda g: 0), wide, _full((ktaps, f)), _full((1, f)), ANY_SPEC,
                  row, vec, vec, vec],
        out_specs=[wide, wide, row, row, row, pl.BlockSpec((tm, 1), lambda i: (i, 0))],
        out_shape=[jax.ShapeDtypeStruct((s_len, f), BF16), jax.ShapeDtypeStruct((s_len, f), BF16),
                   jax.ShapeDtypeStruct((s_len, d), BF16),
                   jax.ShapeDtypeStruct((s_len, d), F32), jax.ShapeDtypeStruct((s_len, d), F32),
                   jax.ShapeDtypeStruct((s_len, 1), F32)],
        scratch_shapes=[pltpu.VMEM((pad + tm, f), F32), pltpu.VMEM((f, d), BF16), pltpu.SemaphoreType.DMA((NDEV,))],
        compiler_params=_cp("arbitrary"),
    )(u0, u0, vg, cw, cb, wg, xres, gate, gam, bet)


def _ffn_core_bwd(dy, u0, uc, vg, cw, wg_row, wg_col, xin, sc, dres, ln, alpha, name):
    s_len, f = u0.shape
    d = xin.shape[1]
    r = f // NDEV
    ktaps = cw.shape[0]
    pad = SHORT_PAD
    tm = _tile(s_len, 256)
    nsteps = s_len // tm
    cbk = 1024 if f % 1024 == 0 else f

    def body(dy_ref, dyn_ref, u_ref, uc_ref, ucn_ref, vg_ref, vgn_ref, cw_ref, wgr_ref, wgc_ref,
             x_ref, sc_ref, dres_ref, xh_ref, rs_ref, gam_ref, y_ref, g_ref,
             du0_ref, dvg_ref, dw_ref, db_ref, dyo_ref, dreso_ref, lnacc_ref, acc_ref,
             dtbuf, dubuf, dwacc, wd_ref, wup_ref, wgate_ref, sems):
        u_ref, uc_ref, ucn_ref, vg_ref, vgn_ref = map(_F32Loads, (u_ref, uc_ref, ucn_ref, vg_ref, vgn_ref))
        s = pl.program_id(0)
        last = s == nsteps - 1

        @pl.when(s == 0)
        def _():
            dwacc[...] = jnp.zeros_like(dwacc)
            db_ref[...] = jnp.zeros_like(db_ref)
            acc_ref[...] = jnp.zeros_like(acc_ref)
            lnacc_ref[...] = jnp.zeros_like(lnacc_ref)
            _load_rows(wgr_ref, r, 0, wd_ref, sems.at[0])
            _load_cols(wgc_ref, 0, wup_ref, sems.at[1])
            _load_cols(wgc_ref, 1, wgate_ref, sems.at[2])

        dy_cur, dy_nxt = dy_ref[...], dyn_ref[...]
        dh = None
        for c0 in range(0, f, cbk):
            cols = slice(c0, c0 + cbk)
            wd_blk = wd_ref[cols, :]
            dtbuf[0:tm, :] = lax.dot_general(dy_cur, wd_blk, NT, preferred_element_type=F32)
            dtbuf[tm:tm + pad, :] = jnp.where(
                last, 0.0, lax.dot_general(dy_nxt, wd_blk, NT, preferred_element_type=F32))
            for r0 in range(0, tm + pad, CHUNK):
                dtr = dtbuf[r0:r0 + CHUNK, :]
                if r0 < tm:
                    u, vgr = uc_ref[r0:r0 + CHUNK, cols], vg_ref[r0:r0 + CHUNK, cols]
                    sg = _sigmoid(u)
                    dvg_ref[r0:r0 + CHUNK, cols] = (dtr * u * sg).astype(BF16)
                else:
                    u, vgr = ucn_ref[r0 - tm:r0 - tm + CHUNK, cols], vgn_ref[r0 - tm:r0 - tm + CHUNK, cols]
                    sg = _sigmoid(u)
                dubuf[r0:r0 + CHUNK, :] = dtr * vgr * (sg * (1.0 + u * (1.0 - sg)))
            db_ref[:, cols] += jnp.sum(dubuf[0:tm, :], axis=0, keepdims=True)
            for r0 in range(0, tm, CHUNK):
                du0 = _conv_bwd_rows(_direct(dubuf), u_ref[r0:r0 + CHUNK, cols], _Cols(cw_ref, cols),
                                     _Cols(dwacc, cols), ktaps, r0, CHUNK)
                du0_ref[r0:r0 + CHUNK, cols] = du0.astype(BF16)
            p = (lax.dot_general(du0_ref[:, cols], wup_ref[:, cols], NT, preferred_element_type=F32)
                 + lax.dot_general(dvg_ref[:, cols], wgate_ref[:, cols], NT, preferred_element_type=F32))
            dh = p if dh is None else dh + p
        dx = dh * (1.0 + sc_ref[...]) + dres_ref[...]
        _ln_bwd_rows(dx, (xh_ref, rs_ref, gam_ref, y_ref, g_ref), (dyo_ref, dreso_ref, lnacc_ref), alpha)
        acc_ref[0:1, :] += jnp.sum(dh * x_ref[...], axis=0, keepdims=True)
        acc_ref[1:2, :] += jnp.sum(dh, axis=0, keepdims=True)

        @pl.when(last)
        def _():
            _flush_dw(dwacc, dw_ref, ktaps)

    wide = pl.BlockSpec((tm, f), lambda i: (i, 0))
    row = pl.BlockSpec((tm, d), lambda i: (i, 0))
    ln_out_specs, ln_out_shape = _ln_out_specs(s_len, tm, d)
    return pl.pallas_call(
        body, name=name, grid=(nsteps,),
        in_specs=[row, _next_spec(tm, pad, d, lambda g: 0, s_len),
                  wide, wide, _next_spec(tm, pad, f, lambda g: 0, s_len),
                  wide, _next_spec(tm, pad, f, lambda g: 0, s_len),
                  _full((ktaps, f)), ANY_SPEC, ANY_SPEC, row, _full((1, d)), row]
        + _ln_in_specs(tm, d),
        out_specs=[wide, wide, _full((ktaps, f)), _full((1, f))] + ln_out_specs + [_full((SUBLANES, d))],
        out_shape=[jax.ShapeDtypeStruct((s_len, f), BF16), jax.ShapeDtypeStruct((s_len, f), BF16),
                   jax.ShapeDtypeStruct((ktaps, f), F32), jax.ShapeDtypeStruct((1, f), F32)]
        + ln_out_shape + [jax.ShapeDtypeStruct((SUBLANES, d), F32)],
        scratch_shapes=[pltpu.VMEM((tm + pad, cbk), F32),
                        pltpu.VMEM((tm + pad, cbk), F32), pltpu.VMEM((ktaps * SUBLANES, f), F32),
                        pltpu.VMEM((f, d), BF16), pltpu.VMEM((d, f), BF16), pltpu.VMEM((d, f), BF16),
                        pltpu.SemaphoreType.DMA((3, NDEV))],
        compiler_params=_cp("arbitrary"),
    )(dy, dy, u0, uc, uc, vg, vg, cw, wg_row, wg_col, xin, sc, dres, *ln)


def _b_mid_fwd(ub, cw, cb, lng, lnb, name):
    s_len, d2 = ub.shape
    d = d2 // 2
    ktaps = cw.shape[0]
    pad = LONG_PAD
    ts = _tile(s_len, 256)

    def body(a_ref, g_ref, ap_ref, gp_ref, w_ref, b_ref, lng_ref, lnb_ref, a2_ref, a4_ref, abuf):
        a_ref, g_ref, ap_ref, gp_ref = map(_F32Loads, (a_ref, g_ref, ap_ref, gp_ref))
        s = pl.program_id(0)
        abuf[0, 0:pad, :] = jnp.where(s > 0, ap_ref[...] * _sigmoid(gp_ref[...]), 0.0)
        abuf[0, pad:pad + ts, :] = a_ref[...] * _sigmoid(g_ref[...])
        _make_shifts(abuf, pad + ts)
        for r0 in range(0, ts, CHUNK):
            a2_ref[r0:r0 + CHUNK, :] = _conv_fwd_rows(_shifted(abuf), w_ref, b_ref, ktaps, pad, r0, CHUNK)
        a2 = a2_ref[...]
        mu = jnp.mean(a2, axis=-1, keepdims=True)
        ac = a2 - mu
        var = jnp.mean(ac * ac, axis=-1, keepdims=True)
        a3 = ac * lax.rsqrt(var + LN_EPS) * lng_ref[...] + lnb_ref[...]
        a4_ref[...] = (a3 * _sigmoid(a3)).astype(BF16)

    def cur(part):
        return pl.BlockSpec((ts, d), lambda s: (s, part))

    vec = _full((1, d))
    return pl.pallas_call(
        body, name=name, grid=(s_len // ts,),
        in_specs=[cur(0), cur(1), _prev_spec(ts, pad, d, lambda g: 0), _prev_spec(ts, pad, d, lambda g: 1),
                  _full((ktaps, d)), vec, vec, vec],
        out_specs=[cur(0), cur(0)],
        out_shape=[jax.ShapeDtypeStruct((s_len, d), F32), jax.ShapeDtypeStruct((s_len, d), BF16)],
        scratch_shapes=[pltpu.VMEM((SUBLANES, pad + ts, d), F32)],
        compiler_params=_cp("parallel"),
    )(ub, ub, ub, ub, cw, cb, lng, lnb)


def _b_mid_bwd(ub, a2, da4, cw, lng, lnb, name):
    s_len, d2 = ub.shape
    d = d2 // 2
    ktaps = cw.shape[0]
    pad = LONG_PAD
    ts = _tile(s_len, 256)
    nsteps = s_len // ts

    def body(a_ref, g_ref, a2_ref, a2n_ref, da4_ref, da4n_ref, w_ref, lng_ref, lnb_ref,
             du_ref, dw_ref, db_ref, dlng_ref, dlnb_ref, dbias_ref, dabuf, dwacc):
        a_ref, g_ref, da4_ref, da4n_ref = map(_F32Loads, (a_ref, g_ref, da4_ref, da4n_ref))
        s = pl.program_id(0)
        last = s == nsteps - 1

        @pl.when(s == 0)
        def _():
            dwacc[...] = jnp.zeros_like(dwacc)
            for ref in (db_ref, dlng_ref, dlnb_ref, dbias_ref):
                ref[...] = jnp.zeros_like(ref)

        def ln_silu_bwd(a2_t, da4_t):
            mu = jnp.mean(a2_t, axis=-1, keepdims=True)
            ac = a2_t - mu
            var = jnp.mean(ac * ac, axis=-1, keepdims=True)
            rstd = lax.rsqrt(var + LN_EPS)
            ah = ac * rstd
            a3 = ah * lng_ref[...] + lnb_ref[...]
            sg = _sigmoid(a3)
            da3 = da4_t * (sg * (1.0 + a3 * (1.0 - sg)))
            dah = da3 * lng_ref[...]
            m1 = jnp.mean(dah, axis=-1, keepdims=True)
            m2 = jnp.mean(dah * ah, axis=-1, keepdims=True)
            return rstd * (dah - m1 - ah * m2), da3, ah

        da2, da3, ah = ln_silu_bwd(a2_ref[...], da4_ref[...])
        dabuf[0, 0:ts, :] = da2
        dlng_ref[...] += jnp.sum(da3 * ah, axis=0, keepdims=True)
        dlnb_ref[...] += jnp.sum(da3, axis=0, keepdims=True)
        db_ref[...] += jnp.sum(da2, axis=0, keepdims=True)
        da2n, _, _ = ln_silu_bwd(a2n_ref[...], jnp.where(last, 0.0, da4n_ref[...]))
        dabuf[0, ts:ts + pad, :] = da2n
        _make_shifts(dabuf, ts + pad)
        for r0 in range(0, ts, CHUNK):
            rows = slice(r0, r0 + CHUNK)
            a_r, g_r = a_ref[rows, :], g_ref[rows, :]
            sg = _sigmoid(g_r)
            da1 = _conv_bwd_rows(_shifted(dabuf), a_r * sg, w_ref, dwacc, ktaps, r0, CHUNK)
            da = da1 * sg
            dg = da1 * a_r * sg * (1.0 - sg)
            du_ref[rows, 0:d] = da.astype(BF16)
            du_ref[rows, d:2 * d] = dg.astype(BF16)
            dbias_ref[:, 0:d] += jnp.sum(da, axis=0, keepdims=True)
            dbias_ref[:, d:2 * d] += jnp.sum(dg, axis=0, keepdims=True)

        @pl.when(last)
        def _():
            _flush_dw(dwacc, dw_ref, ktaps)

    def cur(part):
        return pl.BlockSpec((ts, d), lambda s: (s, part))

    vec = _full((1, d))
    nxt = _next_spec(ts, pad, d, lambda g: 0, s_len)
    return pl.pallas_call(
        body, name=name, grid=(nsteps,),
        in_specs=[cur(0), cur(1), cur(0), nxt, cur(0), nxt, _full((ktaps, d)), vec, vec],
        out_specs=[pl.BlockSpec((ts, d2), lambda s: (s, 0)), _full((ktaps, d)), vec, vec, vec, _full((1, d2))],
        out_shape=[jax.ShapeDtypeStruct((s_len, d2), BF16), jax.ShapeDtypeStruct((ktaps, d), F32),
                   jax.ShapeDtypeStruct((1, d), F32), jax.ShapeDtypeStruct((1, d), F32),
                   jax.ShapeDtypeStruct((1, d), F32), jax.ShapeDtypeStruct((1, d2), F32)],
        scratch_shapes=[pltpu.VMEM((SUBLANES, ts + pad, d), F32), pltpu.VMEM((ktaps * SUBLANES, d), F32)],
        compiler_params=_cp("arbitrary"),
    )(ub, ub, a2, a2, da4, da4, cw, lng, lnb)


def _loss_head(xo, tgt, ln, alpha, name):
    s_len, d = xo.shape
    tm = _tile(s_len, 512)

    def body(x_ref, t_ref, xh_ref, rs_ref, gam_ref, y_ref, g_ref, dy_ref, dres_ref, acc_ref, l_ref):
        @pl.when(pl.program_id(0) == 0)
        def _():
            l_ref[...] = jnp.zeros_like(l_ref)
            acc_ref[...] = jnp.zeros_like(acc_ref)

        e = x_ref[...] - t_ref[...]
        per_row = jnp.sum(e * e, axis=-1, keepdims=True) * (1.0 / d)
        l_ref[...] += 0.5 * jnp.sum(per_row, axis=0, keepdims=True)
        _ln_bwd_rows(e * (1.0 / d), (xh_ref, rs_ref, gam_ref, y_ref, g_ref), (dy_ref, dres_ref, acc_ref), alpha)

    row = pl.BlockSpec((tm, d), lambda i: (i, 0))
    ln_out_specs, ln_out_shape = _ln_out_specs(s_len, tm, d)
    return pl.pallas_call(
        body, name=name, grid=(s_len // tm,),
        in_specs=[row, row] + _ln_in_specs(tm, d), out_specs=ln_out_specs + [_full((1, LANES))],
        out_shape=ln_out_shape + [jax.ShapeDtypeStruct((1, LANES), F32)],
        compiler_params=_cp("arbitrary"),
    )(xo, tgt, *ln)


def _ada_fwd(c_all, ada_w, ada_b_loc, name):
    depth, d, n = ada_w.shape

    def body(c_ref, w_ref, b_ref, o_ref):
        c = c_ref[...]
        act = c * _sigmoid(c)
        o_ref[...] = jnp.dot(act, w_ref[...], preferred_element_type=F32,
                             precision=lax.Precision.HIGHEST) + b_ref[...]

    return pl.pallas_call(
        body, name=name, grid=(depth,),
        in_specs=[_full((NDEV, d)), pl.BlockSpec((None, d, n), lambda i: (i, 0, 0)),
                  pl.BlockSpec((None, 1, n), lambda i: (i, 0, 0))],
        out_specs=pl.BlockSpec((None, NDEV, n), lambda i: (i, 0, 0)),
        out_shape=jax.ShapeDtypeStruct((depth, NDEV, n), F32),
        compiler_params=_cp("parallel"),
    )(c_all, ada_w, ada_b_loc.reshape(depth, 1, n))


def _ada_bwd(c_all_t, dmod_cols, name):
    depth, _, n = dmod_cols.shape
    d = c_all_t.shape[0]

    def body(ct_ref, dm_ref, o_ref):
        ct = ct_ref[...]
        act = ct * _sigmoid(ct)
        acc = None
        for b in range(NDEV):
            term = act[:, b:b + 1] * dm_ref[b:b + 1, :]
            acc = term if acc is None else acc + term
        o_ref[...] = acc

    return pl.pallas_call(
        body, name=name, grid=(depth,),
        in_specs=[_full((d, NDEV)), pl.BlockSpec((None, NDEV, n), lambda i: (i, 0, 0))],
        out_specs=pl.BlockSpec((None, d, n), lambda i: (i, 0, 0)),
        out_shape=jax.ShapeDtypeStruct((depth, d, n), F32),
        compiler_params=_cp("parallel"),
    )(c_all_t, dmod_cols)


def _sum_parts(parts, name):
    _, rows, lanes = parts.shape

    def body(p_ref, o_ref):
        acc = p_ref[0]
        for k in range(1, NDEV):
            acc = acc + p_ref[k]
        o_ref[...] = acc

    return pl.pallas_call(
        body, name=name, in_specs=[_full(parts.shape)], out_specs=_full((rows, lanes)), grid=(1,),
        out_shape=jax.ShapeDtypeStruct((rows, lanes), F32), compiler_params=_cp("arbitrary"),
    )(parts)


def _adamw(w, glist, m, v, name):
    nl, rows, cols = w.shape
    tr = _tile(rows, 256, 2 * SUBLANES)

    def body(w_ref, *rest):
        g_refs = rest[:nl]
        m_ref, v_ref, go_ref, d_ref, mo_ref, vo_ref = rest[nl:]
        g = None
        for layer, g_ref in enumerate(g_refs):
            part = g_ref[0].astype(F32)
            for p in range(1, g_ref.shape[0]):
                part = part + g_ref[p].astype(F32)
            g = part if g is None else jnp.where(pl.program_id(0) == layer, part, g)
        go_ref[...] = g
        d_ref[...], mo_ref[...], vo_ref[...] = _adam_step(w_ref[...], g, m_ref[...], v_ref[...])

    blk = pl.BlockSpec((None, tr, cols), lambda l, i: (l, i, 0))
    g_specs = [pl.BlockSpec((g.shape[0], tr, cols), lambda l, i: (0, i, 0)) for g in glist]
    return pl.pallas_call(
        body, name=name, grid=(nl, rows // tr),
        in_specs=[blk] + g_specs + [blk, blk],
        out_specs=[blk] * 4, out_shape=[jax.ShapeDtypeStruct((nl, rows, cols), F32)] * 4,
        compiler_params=_cp("parallel", "parallel"),
    )(w, *glist, m, v)


def _adam_step(w, g, m, v):
    m1 = ADAM_B1 * m + (1.0 - ADAM_B1) * g
    v1 = ADAM_B2 * v + (1.0 - ADAM_B2) * (g * g)
    m_hat = m1 / (1.0 - ADAM_B1 ** ADAM_STEP)
    v_hat = v1 / (1.0 - ADAM_B2 ** ADAM_STEP)
    return -ADAM_LR * (m_hat / (jnp.sqrt(v_hat) + ADAM_EPS) + ADAM_WD * w), m1, v1


def _adamw_small(ws, gs, ms, vs, name):
    n = len(ws)

    def body(*refs):
        ins, outs = refs[:4 * n], refs[4 * n:]
        for i in range(n):
            w_ref, g_ref, m_ref, v_ref = ins[i], ins[n + i], ins[2 * n + i], ins[3 * n + i]
            delta, m1, v1 = _adam_step(w_ref[...], g_ref[...], m_ref[...], v_ref[...])
            outs[3 * i][...] = delta
            outs[3 * i + 1][...] = m1
            outs[3 * i + 2][...] = v1

    operands = list(ws) + list(gs) + list(ms) + list(vs)
    out_shape = [jax.ShapeDtypeStruct(w.shape, F32) for w in ws for _ in range(3)]
    return pl.pallas_call(
        body, name=name, grid=(1,), in_specs=[_full(a.shape) for a in operands],
        out_specs=[_full(s.shape) for s in out_shape], out_shape=out_shape,
        compiler_params=_cp("arbitrary"),
    )(*operands)


def _pack(pieces):
    flat = jnp.concatenate([p.reshape(-1) for p in pieces])
    unit = SUBLANES * LANES
    padded = -(-flat.shape[0] // unit) * unit
    return jnp.pad(flat, (0, padded - flat.shape[0])).reshape(padded // LANES, LANES)


def _unpack(packed, shapes, lead=()):
    flat = packed.reshape(lead + (-1,))
    out, off = [], 0
    for s in shapes:
        size = 1
        for dim in s:
            size *= dim
        out.append(flat[..., off:off + size].reshape(lead + tuple(s)))
        off += size
    return out


def _pad_last(a, n):
    return jnp.pad(a, [(0, 0)] * (a.ndim - 1) + [(0, n - a.shape[-1])])


def kernel(x, c, ada_w, ada_b, ln_tok_g, ln_tok_b, ln_ch_g, ln_ch_b, a_w_in, a_conv_w, a_conv_b, a_w_out, b_w_pw1, b_b_pw1, b_conv_w, b_conv_b, b_ln_g, b_ln_b, b_w_pw2, b_b_pw2, f_w_up, f_conv_w, f_conv_b, f_w_gate, f_w_down, loss_target, m_ada_w, m_ada_b, m_ln_tok_g, m_ln_tok_b, m_ln_ch_g, m_ln_ch_b, m_a_w_in, m_a_conv_w, m_a_conv_b, m_a_w_out, m_b_w_pw1, m_b_b_pw1, m_b_conv_w, m_b_conv_b, m_b_ln_g, m_b_ln_b, m_b_w_pw2, m_b_b_pw2, m_f_w_up, m_f_conv_w, m_f_conv_b, m_f_w_gate, m_f_w_down, v_ada_w, v_ada_b, v_ln_tok_g, v_ln_tok_b, v_ln_ch_g, v_ln_ch_b, v_a_w_in, v_a_conv_w, v_a_conv_b, v_a_w_out, v_b_w_pw1, v_b_b_pw1, v_b_conv_w, v_b_conv_b, v_b_ln_g, v_b_ln_b, v_b_w_pw2, v_b_b_pw2, v_f_w_up, v_f_conv_w, v_f_conv_b, v_f_w_gate, v_f_w_down):
    weights = dict(ada_w=ada_w, ada_b=ada_b, ln_tok_g=ln_tok_g, ln_tok_b=ln_tok_b, ln_ch_g=ln_ch_g, ln_ch_b=ln_ch_b, a_w_in=a_w_in, a_conv_w=a_conv_w, a_conv_b=a_conv_b, a_w_out=a_w_out, b_w_pw1=b_w_pw1, b_b_pw1=b_b_pw1, b_conv_w=b_conv_w, b_conv_b=b_conv_b, b_ln_g=b_ln_g, b_ln_b=b_ln_b, b_w_pw2=b_w_pw2, b_b_pw2=b_b_pw2, f_w_up=f_w_up, f_conv_w=f_conv_w, f_conv_b=f_conv_b, f_w_gate=f_w_gate, f_w_down=f_w_down)
    mom_m = dict(ada_w=m_ada_w, ada_b=m_ada_b, ln_tok_g=m_ln_tok_g, ln_tok_b=m_ln_tok_b, ln_ch_g=m_ln_ch_g, ln_ch_b=m_ln_ch_b, a_w_in=m_a_w_in, a_conv_w=m_a_conv_w, a_conv_b=m_a_conv_b, a_w_out=m_a_w_out, b_w_pw1=m_b_w_pw1, b_b_pw1=m_b_b_pw1, b_conv_w=m_b_conv_w, b_conv_b=m_b_conv_b, b_ln_g=m_b_ln_g, b_ln_b=m_b_ln_b, b_w_pw2=m_b_w_pw2, b_b_pw2=m_b_b_pw2, f_w_up=m_f_w_up, f_conv_w=m_f_conv_w, f_conv_b=m_f_conv_b, f_w_gate=m_f_w_gate, f_w_down=m_f_w_down)
    mom_v = dict(ada_w=v_ada_w, ada_b=v_ada_b, ln_tok_g=v_ln_tok_g, ln_tok_b=v_ln_tok_b, ln_ch_g=v_ln_ch_g, ln_ch_b=v_ln_ch_b, a_w_in=v_a_w_in, a_conv_w=v_a_conv_w, a_conv_b=v_a_conv_b, a_w_out=v_a_w_out, b_w_pw1=v_b_w_pw1, b_b_pw1=v_b_b_pw1, b_conv_w=v_b_conv_w, b_conv_b=v_b_conv_b, b_ln_g=v_b_ln_g, b_ln_b=v_b_ln_b, b_w_pw2=v_b_w_pw2, b_b_pw2=v_b_b_pw2, f_w_up=v_f_w_up, f_conv_w=v_f_conv_w, f_conv_b=v_f_conv_b, f_w_gate=v_f_w_gate, f_w_down=v_f_w_down)
    names = list(weights)

    depth, d, n_ada = ada_w.shape
    assert depth == 2 and a_w_in.shape[0] == 1 and b_w_pw1.shape[0] == 1
    s_len = x.shape[1]
    f_loc = f_w_up.shape[-1]
    f_pad = -(-f_loc // LANES) * LANES
    f_all = NDEV * f_pad
    d_loc = d // NDEV
    ka, kb, kf = a_conv_w.shape[1], b_conv_w.shape[1], f_conv_w.shape[1]
    alpha = (2.0 * depth) ** 0.25
    assert a_w_in.shape[-1] == f_pad and f_pad % d_loc == 0
    me = 4 * lax.axis_index("x") + 2 * lax.axis_index("y") + lax.axis_index("c")

    small_shapes = [(d,), (ka, d_loc), (2 * d_loc,), (kb, d_loc), (d_loc,), (d_loc,), (d_loc,), (d_loc,),
                    (depth, kf, f_pad)]
    small_loc = _pack([c[0], a_conv_w[0], b_b_pw1[0], b_conv_w[0], b_conv_b[0], b_ln_g[0], b_ln_b[0],
                       b_b_pw2[0], _pad_last(f_conv_w, f_pad)])
    first = _exchange_start([small_loc, a_w_in.astype(BF16)], "gather_chips", "gather_first_chips_start")
    up_pad = _pad_last(_after(f_w_up, first[-1]), f_pad).astype(BF16)
    gate_pad = _pad_last(_after(f_w_gate, first[-1]), f_pad).astype(BF16)
    down_pad = jnp.pad(_after(f_w_down, first[-1]), ((0, 0), (0, f_pad - f_loc), (0, 0))).astype(BF16)
    col_f = [jnp.stack([up_pad[i], gate_pad[i]]) for i in range(depth)]
    row_b = jnp.concatenate([down_pad[1], _after(b_w_pw2[0], first[-1]).astype(BF16)], axis=0)
    out_loc, pw1_loc = _after(a_w_out[0], first[-1]).astype(BF16), _after(b_w_pw1, first[-1]).astype(BF16)
    ridx_pw2 = f_pad // d_loc
    prepared = sum(a.reshape(-1, a.shape[-1])[0:1, 0:LANES].astype(F32)
                   for a in (col_f[0], col_f[1], down_pad[0], row_b, out_loc, pw1_loc))
    *lands, landed = _exchange_wait(first, prepared, "gather_chips", "gather_first_chips_wait")
    g_small, g_in, _ = _exchange_wait(_exchange_start(lands, "forward", "gather_first_forward_start"), landed,
                                      "forward", "gather_first_forward_wait")

    (c_all, acw_g, bb1_g, bcw_g, bcb_g, blg_g, blb_g, bb2_g, fcw_g) = _unpack(g_small, small_shapes, (NDEV,))
    a_cw = acw_g.transpose(1, 0, 2).reshape(ka, d)
    b_cw = bcw_g.transpose(1, 0, 2).reshape(kb, d)
    b_b1 = bb1_g.reshape(1, 2 * d)
    b_cb, b_lg, b_lb, b_b2 = (t.reshape(1, d) for t in (bcb_g, blg_g, blb_g, bb2_g))
    f_cw = fcw_g.transpose(1, 2, 0, 3).reshape(depth, kf, f_all)
    f_cb = _pad_last(f_conv_b.reshape(depth, NDEV, f_loc), f_pad).reshape(depth, 1, f_all)

    ada_b_loc = lax.dynamic_slice(ada_b, (0, me * n_ada), (depth, n_ada))
    mod_part = _ada_fwd(c_all, ada_w, ada_b_loc, "ada_fwd")
    mod_g, mod_done = _exchange([mod_part.reshape(depth * NDEV, n_ada)], "gather", "gather_mod")
    mod_all = mod_g.reshape(NDEV, depth, NDEV, n_ada).transpose(1, 2, 0, 3).reshape(depth, NDEV, 6 * d)
    mod = lax.dynamic_slice(mod_all, (0, me, 0), (depth, 1, 6 * d))[:, 0]

    def behind(a, token):
        return a + token[0, 0].astype(BF16)

    gather_out = _exchange_start([behind(out_loc, mod_done)], "gather_chips", "gather_out_start")
    gather_f0 = _exchange_start([col_f[0], behind(down_pad[0], gather_out[-1])], "gather_chips", "gather_f0_start")

    def mod_rows(i):
        return [mod[i:i + 1, j * d:(j + 1) * d] for j in range(6)]

    zeros_d = jnp.zeros((1, d), F32)
    zeros_f = jnp.zeros((1, f_all), F32)
    x0 = x[0]

    sh_t0, sc_t0, g_t0, sh_c0, sc_c0, g_c0 = mod_rows(0)
    sh_t1, sc_t1, g_t1, sh_c1, sc_c1, g_c1 = mod_rows(1)

    sc_t0 = _after(sc_t0, gather_f0[-1])
    bcv, = _mm_fwd(x0, sc_t0, sh_t0, jnp.zeros((1, 3 * d), F32), g_in, (0,), "a_in_fwd")
    y0 = _gateconv_fwd(bcv, a_cw, a_conv_b, "a_conv_fwd")
    g_out, landed = _exchange_wait(gather_out, y0, "gather_chips", "gather_out_wait")
    g_out, _ = _exchange_wait(_exchange_start([g_out], "forward", "gather_out_fwd_start"), landed, "forward",
                              "gather_out_fwd_wait")
    y_a, x1, xh1, rs1 = _mm_ln(y0, g_out, d_loc, 0, x0, g_t0, ln_tok_g[0:1], ln_tok_b[0:1], zeros_d,
                               alpha, "a_out_ln_fwd")

    def ffn_fwd(xin, sc, sh, gate, gam, bet, g_colf, g_rowf, layer, tag):
        u0, vg = _mm_fwd(xin, sc, sh, zeros_f, g_colf, (0, 1), "f_upgate_fwd" + tag)
        t, uc, y, xo, xh, rs = _ffn_tail_fwd(u0, vg, f_cw[layer], f_cb[layer], g_rowf, xin, gate, gam, bet, alpha,
                                             "f_tail_fwd" + tag)
        return (u0, uc), vg, t, y, xo, xh, rs

    g_colf0, g_rowf0, landed = _exchange_wait(gather_f0, x1, "gather_chips", "gather_f0_wait")
    g_colf0, g_rowf0, landed = _exchange_wait(
        _exchange_start([g_colf0, g_rowf0], "forward", "gather_f0_fwd_start"), landed, "forward", "gather_f0_fwd_wait")
    gather_1 = _exchange_start([behind(pw1_loc, landed), col_f[1], row_b], "gather_chips", "gather_1_start")
    sc_c0 = _after(sc_c0, gather_1[-1])
    u0_0, vg_0, t_0, y_f0, x2, xh2, rs2 = ffn_fwd(x1, sc_c0, sh_c0, g_c0, ln_ch_g[0:1], ln_ch_b[0:1],
                                                  g_colf0, g_rowf0, 0, "0")

    *lands_1, landed = _exchange_wait(gather_1, x2, "gather_chips", "gather_1_wait")
    g_pw1, g_colf1, g_rowb, _ = _exchange_wait(_exchange_start(lands_1, "forward", "gather_1_fwd_start"), landed,
                                                 "forward", "gather_1_fwd_wait")
    ub, = _mm_fwd(x2, sc_t1, sh_t1, b_b1, g_pw1, (0,), "b_pw1_fwd")
    a2, a4 = _b_mid_fwd(ub, b_cw, b_cb, b_lg, b_lb, "b_mid_fwd")
    y_b, x3, xh3, rs3 = _mm_ln(a4, g_rowb, d_loc, ridx_pw2, x2, g_t1, ln_tok_g[1:2], ln_tok_b[1:2], b_b2,
                               alpha, "b_pw2_ln_fwd")
    u0_1, vg_1, t_1, y_f1, x4, xh4, rs4 = ffn_fwd(x3, sc_c1, sh_c1, g_c1, ln_ch_g[1:2], ln_ch_b[1:2],
                                                  g_colf1, g_rowb, 1, "1")

    ln_f1 = (xh4, rs4, ln_ch_g[1:2], y_f1, g_c1)
    ln_b = (xh3, rs3, ln_tok_g[1:2], y_b, g_t1)
    ln_f0 = (xh2, rs2, ln_ch_g[0:1], y_f0, g_c0)
    ln_a = (xh1, rs1, ln_tok_g[0:1], y_a, g_t0)
    dy, dres, accf1, loss_part = _loss_head(x4, loss_target[0], ln_f1, alpha, "loss_head")

    def ffn_bwd(dy, dres, xin, sc, sh, u0, vg, t, g_colf, g_rowf, ln_below, layer, tag):
        dw_down = _mm_tn_row(t, dy, f_pad, f_loc, "f_down_dw" + tag)
        scatter_down = _exchange_start([dw_down], "scatter", "scatter_d%s_start" % tag)
        du0, dvg, dcw, dcb, dy_below, dres_below, acc_below, acc2 = _ffn_core_bwd(
            dy, u0[0], u0[1], vg, f_cw[layer], g_rowf, g_colf, xin, _after(sc, scatter_down[-1]), dres,
            ln_below, alpha, "f_core_bwd" + tag)
        dw_up = _mm_tn_col_t(xin, sc, sh, du0, f_loc, "f_up_dw" + tag)
        dw_gate = _mm_tn_col_t(xin, sc, sh, dvg, f_loc, "f_gate_dw" + tag)
        scatter = _exchange_start([dw_up, dw_gate], "scatter", "scatter_f%s_start" % tag)
        return dy_below, dres_below, acc_below, acc2, (scatter, scatter_down), dcw, dcb

    dy, dres, accb, acc2f1, (scatter_f1, scatter_d1), dfcw1, dfcb1 = ffn_bwd(
        dy, dres, x3, sc_c1, sh_c1, u0_1, vg_1, t_1, g_colf1, g_rowb, ln_b, 1, "1")

    da4 = _mm_nt_row(dy, g_rowb, d_loc, ridx_pw2, "b_pw2_dx")
    dw_pw2 = _mm_tn_row(a4, dy, d_loc, d_loc, "b_pw2_dw")
    du, dbcw, dbcb, dblg, dblb, dbb1 = _b_mid_bwd(ub, a2, da4, b_cw, _after(b_lg, scatter_f1[-1]), b_lb, "b_mid_bwd")
    dw_pw1 = _mm_tn_col(x2, sc_t1, sh_t1, du, "b_pw1_dw")
    scatter_b = _exchange_start([dw_pw1, dw_pw2], "scatter", "scatter_b_start")
    dy, dres, accf0, acc2b = _mm_nt_mod([du], g_pw1, (0,), x2, _after(sc_t1, scatter_b[-1]), dres, "b_pw1_dx",
                                        ln=ln_f0, alpha=alpha)

    dy, dres, acca, acc2f0, (scatter_f0, scatter_d0), dfcw0, dfcb0 = ffn_bwd(
        dy, dres, x1, sc_c0, sh_c0, u0_0, vg_0, t_0, g_colf0, g_rowf0, ln_a, 0, "0")

    dy0 = _mm_nt_row(dy, g_out, d_loc, 0, "a_out_dx")
    dbcv, dacw, dacb = _gateconv_bwd(bcv, dy0, a_cw, _after(a_conv_b, scatter_f0[-1]), "a_conv_bwd")
    dx0, acc2a = _mm_nt_mod([dbcv], g_in, (0,), x0, sc_t0, dres, "a_in_dx")

    def dmod_row(acc2_t, acc_t, acc2_c, acc_c):
        return jnp.concatenate([acc2_t[1], acc2_t[0], acc_t[2], acc2_c[1], acc2_c[0], acc_c[2]])

    dmod = jnp.stack([dmod_row(acc2a, acca, acc2f0, accf0), dmod_row(acc2b, accb, acc2f1, accf1)])

    def unpad_f(a):
        return a.reshape(a.shape[:-1] + (NDEV, f_pad))[..., :f_loc].reshape(a.shape[:-1] + (NDEV * f_loc,))

    small_grads = [
        dmod,
        jnp.stack([acca[0], accb[0]]), jnp.stack([acca[1], accb[1]]),
        jnp.stack([accf0[0], accf1[0]]), jnp.stack([accf0[1], accf1[1]]),
        dacb,
        unpad_f(jnp.concatenate([dfcb0, dfcb1], axis=0)),
        dacw, dbb1, dbcw, dbcb, dblg, dblb, accb[3:4],
        jnp.stack([dfcw0, dfcw1]),
        loss_part[0:1, 0:1],
    ]
    small_grad_shapes = [tuple(g.shape) for g in small_grads]
    gather_small = _exchange_start([_pack(small_grads)], "gather", "gather_small_start")

    dw_in = _mm_tn_col(x0, _after(sc_t0, gather_small[-1]), sh_t0, dbcv, "a_in_dw")
    dw_out = _mm_tn_row(y0, dy, d_loc, d_loc, "a_out_dw")
    scatter_a = _exchange_start([dw_in, dw_out], "scatter", "scatter_a_start")

    grads, deltas, new_m, new_v = {}, {}, {}, {}

    def adamw(k, glist, transposed=False):
        def view(a):
            a = jnp.swapaxes(a, 1, 2) if transposed else a
            return a.reshape(len(glist), -1, a.shape[-1])

        w = view(weights[k])
        outs = _adamw(w, [g.reshape(g.shape[0], -1, w.shape[-1]) for g in glist],
                      view(mom_m[k]), view(mom_v[k]), "adamw_" + k)
        if transposed:
            outs = [jnp.swapaxes(o, 1, 2) for o in outs]
        grads[k], deltas[k], new_m[k], new_v[k] = (o.reshape(weights[k].shape) for o in outs)

    r_up1, r_gate1, _ = _exchange_wait(scatter_f1, scatter_a[-1], "scatter", "scatter_f1_wait")
    r_down1, _ = _exchange_wait(scatter_d1, r_gate1, "scatter", "scatter_d1_wait")
    r_pw1, r_pw2, _ = _exchange_wait(scatter_b, r_down1, "scatter", "scatter_b_wait")
    adamw("b_w_pw1", [r_pw1])
    adamw("b_w_pw2", [r_pw2])
    r_down0, _ = _exchange_wait(scatter_d0, deltas["b_w_pw2"], "scatter", "scatter_d0_wait")
    adamw("f_w_down", [r_down0, r_down1])
    r_up0, r_gate0, _ = _exchange_wait(scatter_f0, deltas["f_w_down"], "scatter", "scatter_f0_wait")
    adamw("f_w_up", [r_up0, r_up1], transposed=True)
    adamw("f_w_gate", [r_gate0, r_gate1], transposed=True)

    sg_all, _ = _exchange_wait(gather_small, deltas["f_w_gate"], "gather", "gather_small_wait")
    sg_sum = _sum_parts(sg_all, "sum_small_grads")
    (g_ada_b, g_ltg, g_ltb, g_lcg, g_lcb, g_acb, g_fcb, g_acw, g_bb1, g_bcw, g_bcb, g_blg, g_blb, g_bb2,
     g_fcw, loss_all) = _unpack(sg_sum, small_grad_shapes)
    loss = loss_all[0, 0]

    def my_cols(a, width):
        return lax.dynamic_slice_in_dim(a, me * width, width, axis=a.ndim - 1)

    g_fcw_loc = my_cols(g_fcw, f_pad)[..., :f_loc]
    small = dict(
        ada_b=g_ada_b, ln_tok_g=g_ltg, ln_tok_b=g_ltb, ln_ch_g=g_lcg, ln_ch_b=g_lcb, a_conv_b=g_acb, f_conv_b=g_fcb,
        a_conv_w=my_cols(g_acw, d_loc)[None], b_b_pw1=my_cols(g_bb1, 2 * d_loc), b_conv_w=my_cols(g_bcw, d_loc)[None],
        b_conv_b=my_cols(g_bcb, d_loc), b_ln_g=my_cols(g_blg, d_loc), b_ln_b=my_cols(g_blb, d_loc),
        b_b_pw2=my_cols(g_bb2, d_loc), f_conv_w=g_fcw_loc)

    dmod_all = sg_all.reshape(NDEV, -1)[:, :depth * 6 * d].reshape(NDEV, depth, 6 * d)
    dmod_cols = my_cols(dmod_all, n_ada).transpose(1, 0, 2)
    g_ada_w = _ada_bwd(c_all.T, dmod_cols, "ada_bwd")

    adamw("ada_w", [g_ada_w[0:1], g_ada_w[1:2]])

    def rows_cols(a):
        return a.reshape(-1, a.shape[-1])

    small_keys = list(small)
    small_outs = _adamw_small([rows_cols(weights[k]) for k in small_keys], [rows_cols(small[k]) for k in small_keys],
                              [rows_cols(mom_m[k]) for k in small_keys], [rows_cols(mom_v[k]) for k in small_keys],
                              "adamw_small")
    for i, k in enumerate(small_keys):
        grads[k] = small[k].reshape(weights[k].shape)
        deltas[k], new_m[k], new_v[k] = (o.reshape(weights[k].shape) for o in small_outs[3 * i:3 * i + 3])

    r_in, r_out, _ = _exchange_wait(scatter_a, deltas["ada_w"], "scatter", "scatter_a_wait")
    adamw("a_w_in", [r_in])
    adamw("a_w_out", [r_out])

    return (loss, dx0[None], *[grads[k] for k in names], *[deltas[k] for k in names],
            *[new_m[k] for k in names], *[new_v[k] for k in names])
```

```python
import jax
import jax.numpy as jnp
from jax import lax
from jax.experimental import pallas as pl
from jax.experimental.pallas import tpu as pltpu

NDEV = 8
MESH_AXES = ("x", "y", "c")
LANES = 128
SUBLANES = 8
VMEM_LIMIT = 56 * 1024 * 1024
LN_EPS = 1e-5
SHORT_PAD = 16
LONG_PAD = 32
CHUNK = 16
DW_ROWS = 1024
MM_ROWS = 512
CONV_ROWS = 256
COL_BLOCK = 1024
ADAM_LR, ADAM_B1, ADAM_B2, ADAM_EPS, ADAM_WD, ADAM_STEP = 0.001, 0.9, 0.999, 1e-08, 0.01, 10

F32 = jnp.float32
BF16 = jnp.bfloat16
MESH = pl.DeviceIdType.MESH
NT = (((1,), (1,)), ((), ()))
TN = (((0,), (0,)), ((), ()))


def _tile(n, target, mult=SUBLANES):
    best = None
    for t in range(mult, min(n, target) + 1, mult):
        if n % t == 0:
            best = t
    return best if best is not None else n


def _full(shape):
    nd = len(shape)
    return pl.BlockSpec(shape, lambda *_: (0,) * nd)


def _cp(*sem):
    return pltpu.CompilerParams(dimension_semantics=sem, vmem_limit_bytes=VMEM_LIMIT)


def _sigmoid(x):
    return 1.0 / (1.0 + jnp.exp(-x))


def _peer(x, y, c, d):
    return ((1 - x) if d & 4 else x, (1 - y) if d & 2 else y, (1 - c) if d & 1 else c)


def _lin(p):
    return 4 * p[0] + 2 * p[1] + p[2]


CHIP_MASKS = (2, 4, 6)
MODES_PER_ARRAY = {"gather": NDEV - 1, "scatter": NDEV - 1, "gather_chips": 1 + len(CHIP_MASKS),
                   "forward": len(CHIP_MASKS)}


def _transfers(mode):
    x, y, c = (lax.axis_index(a) for a in MESH_AXES)
    me = _lin((x, y, c))
    if mode == "forward":
        sibling = (x, y, 1 - c)
        return [(sibling, ("land", _lin(_peer(x, y, c, q))), _lin(_peer(x, y, c, q)), _lin(_peer(x, y, c, q ^ 1)))
                for q in CHIP_MASKS]
    masks = (1,) + CHIP_MASKS if mode == "gather_chips" else range(1, NDEV)
    out = []
    for d in masks:
        peer = _peer(x, y, c, d)
        source = ("block", _lin(peer)) if mode == "scatter" else ("whole", None)
        out.append((peer, source, me, _lin(peer)))
    return out


def _remote_copies(src_refs, land_refs, send_sems, recv_sems, mode):
    transfers = _transfers(mode)
    sends, recvs = [], []
    for i, land_ref in enumerate(land_refs):
        for t, (peer, (kind, slot), there, here) in enumerate(transfers):
            k = i * len(transfers) + t
            src = land_ref.at[slot] if kind == "land" else src_refs[i].at[slot] if kind == "block" else src_refs[i]
            for dst_slot, out in ((there, sends), (here, recvs)):
                out.append(pltpu.make_async_remote_copy(
                    src_ref=src, dst_ref=land_ref.at[dst_slot], send_sem=send_sems.at[k], recv_sem=recv_sems.at[k],
                    device_id=peer, device_id_type=MESH))
    return sends, recvs


def _exchange(srcs, mode, name):
    n = len(srcs)
    gather = mode == "gather"

    def body(*refs):
        src_refs, out_refs, token = refs[:n], refs[n:2 * n], refs[2 * n]
        send_sems, recv_sems, local_sems = refs[2 * n + 1:]
        me = _lin(tuple(lax.axis_index(a) for a in MESH_AXES))
        local = []
        for i in range(n):
            mine = src_refs[i] if gather else src_refs[i].at[me]
            cp = pltpu.make_async_copy(mine, out_refs[i].at[me], local_sems.at[i])
            cp.start()
            local.append(cp)
        sends, recvs = _remote_copies(src_refs, out_refs, send_sems, recv_sems, mode)
        for snd in sends:
            snd.start()
        token[...] = jnp.zeros_like(token)
        for snd, rcv in zip(sends, recvs):
            snd.wait_send()
            rcv.wait_recv()
        for cp in local:
            cp.wait()

    out_shape = [jax.ShapeDtypeStruct(((NDEV,) + s.shape) if gather else s.shape, s.dtype) for s in srcs]
    out_shape.append(jax.ShapeDtypeStruct((SUBLANES, LANES), F32))
    any_spec = pl.BlockSpec(memory_space=pl.ANY)
    return pl.pallas_call(
        body, name=name, out_shape=out_shape,
        in_specs=[any_spec] * n, out_specs=[any_spec] * n + [pl.BlockSpec(memory_space=pltpu.VMEM)],
        scratch_shapes=[pltpu.SemaphoreType.DMA((n * (NDEV - 1),)),
                        pltpu.SemaphoreType.DMA((n * (NDEV - 1),)),
                        pltpu.SemaphoreType.DMA((n,))],
    )(*srcs)


HBM_SPEC = pl.BlockSpec(memory_space=pltpu.HBM)
SEM_SPEC = pl.BlockSpec(memory_space=pltpu.SEMAPHORE)
SIDE_EFFECT = pltpu.SideEffectType.DATAFLOW_SIDE_EFFECTING


def _exchange_start(arrays, mode, name):
    me = _lin(tuple(lax.axis_index(a) for a in MESH_AXES))
    if mode == "forward":
        srcs, lands = [], list(arrays)
    else:
        srcs, lands = list(arrays), []
        for s in srcs:
            own = lax.dynamic_index_in_dim(s, me, 0, keepdims=False) if mode == "scatter" else s
            shape = s.shape if mode == "scatter" else (NDEV,) + s.shape
            lands.append(lax.dynamic_update_index_in_dim(lax.empty(shape, s.dtype), own, me, 0))
    ns, n = len(srcs), len(lands)

    def body(*refs):
        src_refs, land_refs = refs[:ns], refs[ns:ns + n]
        send_sems, recv_sems, token = refs[ns + n], refs[ns + n + 1], refs[-1]
        sends, _ = _remote_copies(src_refs, land_refs, send_sems, recv_sems, mode)
        for snd in sends:
            snd.start()
        token[...] = jnp.zeros_like(token)

    operands = [pltpu.with_memory_space_constraint(a, pltpu.HBM) for a in srcs + lands]
    nsem = n * MODES_PER_ARRAY[mode]
    return pl.pallas_call(
        body, name=name,
        out_shape=(pltpu.SemaphoreType.DMA((nsem,)), pltpu.SemaphoreType.DMA((nsem,)),
                   *[pltpu.HBM(a.shape, a.dtype) for a in operands],
                   jax.ShapeDtypeStruct((SUBLANES, LANES), F32)),
        in_specs=[HBM_SPEC] * (ns + n),
        out_specs=(SEM_SPEC, SEM_SPEC, *([HBM_SPEC] * (ns + n)), pl.BlockSpec(memory_space=pltpu.VMEM)),
        input_output_aliases={i: 2 + i for i in range(ns + n)},
        compiler_params=pltpu.CompilerParams(has_side_effects=SIDE_EFFECT),
    )(*operands)


def _exchange_wait(handle, after, mode, name):
    send_sems, recv_sems, *thru = handle[:-1]
    n = len(thru) if mode == "forward" else len(thru) // 2
    ns = len(thru) - n

    def body(*refs):
        src_refs, land_refs = refs[:ns], refs[ns:ns + n]
        sends, recvs = _remote_copies(src_refs, land_refs, refs[ns + n], refs[ns + n + 1], mode)
        for snd, rcv in zip(sends, recvs):
            snd.wait_send()
            rcv.wait_recv()
        refs[-1][...] = jnp.zeros_like(refs[-1])

    outs = pl.pallas_call(
        body, name=name,
        out_shape=(*[pltpu.HBM(a.shape, a.dtype) for a in thru], jax.ShapeDtypeStruct((SUBLANES, LANES), F32)),
        in_specs=[HBM_SPEC] * (ns + n) + [SEM_SPEC, SEM_SPEC, pl.BlockSpec(memory_space=pl.ANY)],
        out_specs=[HBM_SPEC] * (ns + n) + [pl.BlockSpec(memory_space=pltpu.VMEM)],
        input_output_aliases={i: i for i in range(ns + n)},
        compiler_params=pltpu.CompilerParams(has_side_effects=SIDE_EFFECT),
    )(*thru, send_sems, recv_sems, after)
    return outs[ns:]


def _after(value, token):
    return value + token[0, 0]


ANY_SPEC = pl.BlockSpec(memory_space=pl.ANY)


def _load_cols(wg_ref, widx, w_ref, sems):
    n = wg_ref.shape[-1]
    copies = [pltpu.make_async_copy(wg_ref.at[k, widx], w_ref.at[:, pl.ds(k * n, n)], sems.at[k])
              for k in range(NDEV)]
    for cp in copies:
        cp.start()
    for cp in copies:
        cp.wait()


def _load_rows(wg_ref, r, ridx, w_ref, sems):
    copies = [pltpu.make_async_copy(wg_ref.at[k, pl.ds(ridx * r, r)], w_ref.at[pl.ds(k * r, r)], sems.at[k])
              for k in range(NDEV)]
    for cp in copies:
        cp.start()
    for cp in copies:
        cp.wait()


def _mm_fwd(x, sc, sh, bias, wg, widxs, name):
    s_len, kdim = x.shape
    ncol = NDEV * wg.shape[-1]
    tm = _tile(s_len, MM_ROWS)
    nw = len(widxs)

    def body(x_ref, sc_ref, sh_ref, b_ref, wg_ref, *rest):
        o_refs, w_refs, sems = rest[:nw], rest[nw:2 * nw], rest[2 * nw]

        @pl.when(pl.program_id(0) == 0)
        def _():
            for i, w_ref in enumerate(w_refs):
                _load_cols(wg_ref, widxs[i], w_ref, sems.at[i])

        h = (x_ref[...] * (1.0 + sc_ref[...]) + sh_ref[...]).astype(BF16)
        for w_ref, o_ref in zip(w_refs, o_refs):
            o_ref[...] = (jnp.dot(h, w_ref[...], preferred_element_type=F32) + b_ref[...]).astype(BF16)

    return pl.pallas_call(
        body, name=name, grid=(s_len // tm,),
        in_specs=[pl.BlockSpec((tm, kdim), lambda i: (i, 0)), _full((1, kdim)), _full((1, kdim)),
                  _full((1, ncol)), ANY_SPEC],
        out_specs=[pl.BlockSpec((tm, ncol), lambda i: (i, 0))] * nw,
        out_shape=[jax.ShapeDtypeStruct((s_len, ncol), BF16)] * nw,
        scratch_shapes=[pltpu.VMEM((kdim, ncol), BF16)] * nw + [pltpu.SemaphoreType.DMA((nw, NDEV))],
        compiler_params=_cp("arbitrary"),
    )(x, sc, sh, bias, wg)


def _mm_ln(a, wg, r, ridx, xres, gate, gam, bet, bias, alpha, name):
    s_len = a.shape[0]
    d = wg.shape[-1]
    tm = _tile(s_len, MM_ROWS)

    def body(a_ref, wg_ref, x_ref, g_ref, gam_ref, bet_ref, b_ref, y_ref, xo_ref, xh_ref, rs_ref, w_ref, sems):
        @pl.when(pl.program_id(0) == 0)
        def _():
            _load_rows(wg_ref, r, ridx, w_ref, sems)

        y = jnp.dot(a_ref[...], w_ref[...], preferred_element_type=F32) + b_ref[...]
        z = alpha * x_ref[...] + g_ref[...] * y
        mu = jnp.mean(z, axis=-1, keepdims=True)
        zc = z - mu
        var = jnp.mean(zc * zc, axis=-1, keepdims=True)
        rstd = lax.rsqrt(var + LN_EPS)
        xh = zc * rstd
        y_ref[...] = y.astype(BF16)
        xh_ref[...] = xh
        rs_ref[...] = rstd
        xo_ref[...] = xh * gam_ref[...] + bet_ref[...]

    row = pl.BlockSpec((tm, d), lambda i: (i, 0))
    vec = _full((1, d))
    return pl.pallas_call(
        body, name=name, grid=(s_len // tm,),
        in_specs=[pl.BlockSpec((tm, NDEV * r), lambda i: (i, 0)), ANY_SPEC, row, vec, vec, vec, vec],
        out_specs=[row, row, row, pl.BlockSpec((tm, 1), lambda i: (i, 0))],
        out_shape=[jax.ShapeDtypeStruct((s_len, d), BF16)] + [jax.ShapeDtypeStruct((s_len, d), F32)] * 2
        + [jax.ShapeDtypeStruct((s_len, 1), F32)],
        scratch_shapes=[pltpu.VMEM((NDEV * r, d), BF16), pltpu.SemaphoreType.DMA((NDEV,))],
        compiler_params=_cp("arbitrary"),
    )(a, wg, xres, gate, gam, bet, bias)


def _ln_in_specs(tm, d):
    row = pl.BlockSpec((tm, d), lambda i: (i, 0))
    return [row, pl.BlockSpec((tm, 1), lambda i: (i, 0)), _full((1, d)), row, _full((1, d))]


def _ln_out_specs(s_len, tm, d):
    row = pl.BlockSpec((tm, d), lambda i: (i, 0))
    return ([row, row, _full((SUBLANES, d))],
            [jax.ShapeDtypeStruct((s_len, d), BF16), jax.ShapeDtypeStruct((s_len, d), F32),
             jax.ShapeDtypeStruct((SUBLANES, d), F32)])


def _ln_bwd_rows(dxo, ln_refs, out_refs, alpha):
    xh_ref, rs_ref, gam_ref, y_ref, g_ref = ln_refs
    dy_ref, dres_ref, acc_ref = out_refs
    xh = xh_ref[...]
    dxh = dxo * gam_ref[...]
    m1 = jnp.mean(dxh, axis=-1, keepdims=True)
    m2 = jnp.mean(dxh * xh, axis=-1, keepdims=True)
    dz = rs_ref[...] * (dxh - m1 - xh * m2)
    dy = g_ref[...] * dz
    dy_ref[...] = dy.astype(BF16)
    dres_ref[...] = alpha * dz
    acc_ref[0:1, :] += jnp.sum(dxo * xh, axis=0, keepdims=True)
    acc_ref[1:2, :] += jnp.sum(dxo, axis=0, keepdims=True)
    acc_ref[2:3, :] += jnp.sum(dz * y_ref[...].astype(F32), axis=0, keepdims=True)
    acc_ref[3:4, :] += jnp.sum(dy, axis=0, keepdims=True)


def _mm_nt_row(dy, wg, r, ridx, name):
    s_len, d = dy.shape
    tm = _tile(s_len, MM_ROWS)

    def body(dy_ref, wg_ref, o_ref, w_ref, sems):
        @pl.when(pl.program_id(0) == 0)
        def _():
            _load_rows(wg_ref, r, ridx, w_ref, sems)

        o_ref[...] = lax.dot_general(dy_ref[...], w_ref[...], NT, preferred_element_type=F32).astype(BF16)

    return pl.pallas_call(
        body, name=name, grid=(s_len // tm,),
        in_specs=[pl.BlockSpec((tm, d), lambda i: (i, 0)), ANY_SPEC],
        out_specs=pl.BlockSpec((tm, NDEV * r), lambda i: (i, 0)),
        out_shape=jax.ShapeDtypeStruct((s_len, NDEV * r), BF16),
        scratch_shapes=[pltpu.VMEM((NDEV * r, d), BF16), pltpu.SemaphoreType.DMA((NDEV,))],
        compiler_params=_cp("arbitrary"),
    )(dy, wg)


def _mm_nt_mod(dos, wg, widxs, xin, sc, dres, name, ln=None, alpha=None):
    s_len, kdim = xin.shape
    ncol = NDEV * wg.shape[-1]
    tm = _tile(s_len, MM_ROWS)
    nw = len(widxs)
    nln = 0 if ln is None else len(ln)
    nout = 2 if ln is None else 4

    def body(*refs):
        do_refs, wg_ref = refs[:nw], refs[nw]
        x_ref, sc_ref, dres_ref = refs[nw + 1:nw + 4]
        ln_refs = refs[nw + 4:nw + 4 + nln]
        out_refs = refs[nw + 4 + nln:nw + 4 + nln + nout]
        w_refs, sems = refs[nw + 4 + nln + nout:-1], refs[-1]
        acc_ref = out_refs[-1]

        @pl.when(pl.program_id(0) == 0)
        def _():
            for ref in out_refs[nout // 2:]:
                ref[...] = jnp.zeros_like(ref)
            for i, w_ref in enumerate(w_refs):
                _load_cols(wg_ref, widxs[i], w_ref, sems.at[i])

        dh = None
        for do_ref, w_ref in zip(do_refs, w_refs):
            p = lax.dot_general(do_ref[...], w_ref[...], NT, preferred_element_type=F32)
            dh = p if dh is None else dh + p
        dx = dh * (1.0 + sc_ref[...]) + dres_ref[...]
        if ln is None:
            out_refs[0][...] = dx
        else:
            _ln_bwd_rows(dx, ln_refs, out_refs[0:3], alpha)
        acc_ref[0:1, :] += jnp.sum(dh * x_ref[...], axis=0, keepdims=True)
        acc_ref[1:2, :] += jnp.sum(dh, axis=0, keepdims=True)

    row = pl.BlockSpec((tm, kdim), lambda i: (i, 0))
    if ln is None:
        out_specs, out_shape = [row], [jax.ShapeDtypeStruct((s_len, kdim), F32)]
    else:
        out_specs, out_shape = _ln_out_specs(s_len, tm, kdim)
    return pl.pallas_call(
        body, name=name, grid=(s_len // tm,),
        in_specs=[pl.BlockSpec((tm, ncol), lambda i: (i, 0))] * nw + [ANY_SPEC, row, _full((1, kdim)), row]
        + ([] if ln is None else _ln_in_specs(tm, kdim)),
        out_specs=out_specs + [_full((SUBLANES, kdim))],
        out_shape=out_shape + [jax.ShapeDtypeStruct((SUBLANES, kdim), F32)],
        scratch_shapes=[pltpu.VMEM((kdim, ncol), BF16)] * nw + [pltpu.SemaphoreType.DMA((nw, NDEV))],
        compiler_params=_cp("arbitrary"),
    )(*dos, wg, xin, sc, dres, *([] if ln is None else ln))


def _mm_tn_col(x, sc, sh, do, name):
    s_len, kdim = x.shape
    n = do.shape[1] // NDEV
    ts = _tile(s_len, DW_ROWS)
    nsteps = s_len // ts

    def body(x_ref, sc_ref, sh_ref, do_ref, o_ref, acc_ref):
        @pl.when(pl.program_id(0) == 0)
        def _():
            acc_ref[...] = jnp.zeros_like(acc_ref)

        h = (x_ref[...] * (1.0 + sc_ref[...]) + sh_ref[...]).astype(BF16)
        acc_ref[...] += lax.dot_general(h, do_ref[...], TN, preferred_element_type=F32)

        @pl.when(pl.program_id(0) == nsteps - 1)
        def _():
            for k in range(NDEV):
                o_ref[k] = acc_ref[:, k * n:(k + 1) * n].astype(BF16)

    return pl.pallas_call(
        body, name=name, grid=(nsteps,),
        in_specs=[pl.BlockSpec((ts, kdim), lambda i: (i, 0)), _full((1, kdim)), _full((1, kdim)),
                  pl.BlockSpec((ts, NDEV * n), lambda i: (i, 0))],
        out_specs=_full((NDEV, kdim, n)),
        out_shape=jax.ShapeDtypeStruct((NDEV, kdim, n), BF16),
        scratch_shapes=[pltpu.VMEM((kdim, NDEV * n), F32)],
        compiler_params=_cp("arbitrary"),
    )(x, sc, sh, do)


def _mm_tn_col_t(x, sc, sh, do, rows_out, name):
    s_len, kdim = x.shape
    n = do.shape[1] // NDEV
    ts = _tile(s_len, DW_ROWS)
    nsteps = s_len // ts

    def body(x_ref, sc_ref, sh_ref, do_ref, o_ref, acc_ref):
        @pl.when(pl.program_id(0) == 0)
        def _():
            acc_ref[...] = jnp.zeros_like(acc_ref)

        h = (x_ref[...] * (1.0 + sc_ref[...]) + sh_ref[...]).astype(BF16)
        acc_ref[...] += lax.dot_general(do_ref[...], h, TN, preferred_element_type=F32)

        @pl.when(pl.program_id(0) == nsteps - 1)
        def _():
            for k in range(NDEV):
                o_ref[k] = acc_ref[k * n:k * n + rows_out, :].astype(BF16)

    return pl.pallas_call(
        body, name=name, grid=(nsteps,),
        in_specs=[pl.BlockSpec((ts, kdim), lambda i: (i, 0)), _full((1, kdim)), _full((1, kdim)),
                  pl.BlockSpec((ts, NDEV * n), lambda i: (i, 0))],
        out_specs=_full((NDEV, rows_out, kdim)),
        out_shape=jax.ShapeDtypeStruct((NDEV, rows_out, kdim), BF16),
        scratch_shapes=[pltpu.VMEM((NDEV * n, kdim), F32)],
        compiler_params=_cp("arbitrary"),
    )(x, sc, sh, do)


def _mm_tn_row(a, dy, r, rows_out, name):
    s_len, d = dy.shape
    ts = _tile(s_len, DW_ROWS)
    nsteps = s_len // ts

    def body(a_ref, dy_ref, o_ref, acc_ref):
        @pl.when(pl.program_id(0) == 0)
        def _():
            acc_ref[...] = jnp.zeros_like(acc_ref)

        acc_ref[...] += lax.dot_general(a_ref[...], dy_ref[...], TN, preferred_element_type=F32)

        @pl.when(pl.program_id(0) == nsteps - 1)
        def _():
            for k in range(NDEV):
                o_ref[k] = acc_ref[k * r:k * r + rows_out, :].astype(BF16)

    return pl.pallas_call(
        body, name=name, grid=(nsteps,),
        in_specs=[pl.BlockSpec((ts, NDEV * r), lambda i: (i, 0)), pl.BlockSpec((ts, d), lambda i: (i, 0))],
        out_specs=_full((NDEV, rows_out, d)),
        out_shape=jax.ShapeDtypeStruct((NDEV, rows_out, d), BF16),
        scratch_shapes=[pltpu.VMEM((NDEV * r, d), F32)],
        compiler_params=_cp("arbitrary"),
    )(a, dy)


def _prev_spec(ts, pad, cb, col):
    return pl.BlockSpec((pad, cb), lambda *g: (jnp.maximum(g[-1] * (ts // pad) - 1, 0), col(g)))


def _next_spec(ts, pad, cb, col, s_len):
    return pl.BlockSpec((pad, cb), lambda *g: (jnp.minimum((g[-1] + 1) * (ts // pad), s_len // pad - 1), col(g)))


class _F32Loads:
    def __init__(self, ref):
        self.ref = ref

    def __getitem__(self, idx):
        return self.ref[idx].astype(F32)


def _direct(buf_ref):
    return lambda off, rows: buf_ref[off:off + rows, :]


def _make_shifts(sh_ref, nrows):
    for r in range(1, SUBLANES):
        sh_ref[r, 0:nrows - SUBLANES, :] = sh_ref[0, r:r + nrows - SUBLANES, :]


def _shifted(sh_ref):
    def read(off, rows):
        r = off % SUBLANES
        return sh_ref[r, off - r:off - r + rows, :]
    return read


def _conv_fwd_rows(read, w_ref, b_ref, ktaps, pad, r0, rows):
    acc = None
    for j in range(ktaps):
        term = w_ref[ktaps - 1 - j:ktaps - j, :] * read(pad - j + r0, rows)
        acc = term if acc is None else acc + term
    return acc + b_ref[...]


def _conv_bwd_rows(read, x_rows, w_ref, dwacc_ref, ktaps, r0, rows):
    acc = None
    for j in range(ktaps):
        sl = read(j + r0, rows)
        term = w_ref[ktaps - 1 - j:ktaps - j, :] * sl
        acc = term if acc is None else acc + term
        prod = x_rows * sl
        fold = prod[0:SUBLANES]
        for q in range(1, rows // SUBLANES):
            fold = fold + prod[q * SUBLANES:(q + 1) * SUBLANES]
        tap = ktaps - 1 - j
        dwacc_ref[tap * SUBLANES:(tap + 1) * SUBLANES, :] += fold
    return acc


def _flush_dw(dwacc_ref, dw_ref, ktaps):
    for tap in range(ktaps):
        dw_ref[tap:tap + 1, :] = jnp.sum(dwacc_ref[tap * SUBLANES:(tap + 1) * SUBLANES, :], axis=0, keepdims=True)


def _gateconv_fwd(bcv, cw, cb, name):
    s_len, d3 = bcv.shape
    d = d3 // 3
    ktaps = cw.shape[0]
    pad = SHORT_PAD
    ts = _tile(s_len, CONV_ROWS)

    def body(gb_ref, gc_ref, v_ref, gcp_ref, vp_ref, w_ref, b_ref, o_ref, pbuf):
        gb_ref, gc_ref, v_ref, gcp_ref, vp_ref = map(_F32Loads, (gb_ref, gc_ref, v_ref, gcp_ref, vp_ref))
        s = pl.program_id(0)
        pbuf[0:pad, :] = jnp.where(s > 0, gcp_ref[...] * vp_ref[...], 0.0)
        pbuf[pad:pad + ts, :] = gc_ref[...] * v_ref[...]
        for r0 in range(0, ts, CHUNK):
            q = _conv_fwd_rows(_direct(pbuf), w_ref, b_ref, ktaps, pad, r0, CHUNK)
            o_ref[r0:r0 + CHUNK, :] = (gb_ref[r0:r0 + CHUNK, :] * q).astype(BF16)

    def cur(part):
        return pl.BlockSpec((ts, d), lambda s: (s, part))

    return pl.pallas_call(
        body, name=name, grid=(s_len // ts,),
        in_specs=[cur(0), cur(1), cur(2),
                  _prev_spec(ts, pad, d, lambda g: 1), _prev_spec(ts, pad, d, lambda g: 2),
                  _full((ktaps, d)), _full((1, d))],
        out_specs=pl.BlockSpec((ts, d), lambda s: (s, 0)),
        out_shape=jax.ShapeDtypeStruct((s_len, d), BF16),
        scratch_shapes=[pltpu.VMEM((pad + ts, d), F32)],
        compiler_params=_cp("parallel"),
    )(bcv, bcv, bcv, bcv, bcv, cw, cb)


def _a_tail_fwd(bcv, cw, cb, wg, xres, gate, gam, bet, alpha, name):
    s_len, d3 = bcv.shape
    d = d3 // 3
    r = d // NDEV
    ktaps = cw.shape[0]
    pad = SHORT_PAD
    tm = _tile(s_len, CONV_ROWS)
    cbk = _tile(d, d // 4, LANES)

    def body(gb_ref, gc_ref, v_ref, gcp_ref, vp_ref, cw_ref, cb_ref, wg_ref, x_ref, g_ref, gam_ref, bet_ref,
             y0_ref, y_ref, xo_ref, xh_ref, rs_ref, pbuf, w_ref, sems):
        gb_ref, gc_ref, v_ref, gcp_ref, vp_ref = map(_F32Loads, (gb_ref, gc_ref, v_ref, gcp_ref, vp_ref))
        s = pl.program_id(0)

        @pl.when(s == 0)
        def _():
            _load_rows(wg_ref, r, 0, w_ref, sems)

        pbuf[0:pad, :] = jnp.where(s > 0, gcp_ref[...] * vp_ref[...], 0.0)
        pbuf[pad:pad + tm, :] = gc_ref[...] * v_ref[...]
        y = None
        for c0 in range(0, d, cbk):
            cols = slice(c0, c0 + cbk)
            read = _direct(_Cols(pbuf, cols))
            for r0 in range(0, tm, CHUNK):
                rows = slice(r0, r0 + CHUNK)
                q = _conv_fwd_rows(read, _Cols(cw_ref, cols), _Cols(cb_ref, cols), ktaps, pad, r0, CHUNK)
                y0_ref[rows, cols] = (gb_ref[rows, cols] * q).astype(BF16)
            p = jnp.dot(y0_ref[:, cols], w_ref[cols, :], preferred_element_type=F32)
            y = p if y is None else y + p
        z = alpha * x_ref[...] + g_ref[...] * y
        mu = jnp.mean(z, axis=-1, keepdims=True)
        zc = z - mu
        var = jnp.mean(zc * zc, axis=-1, keepdims=True)
        rstd = lax.rsqrt(var + LN_EPS)
        xh = zc * rstd
        y_ref[...] = y.astype(BF16)
        xh_ref[...] = xh
        rs_ref[...] = rstd
        xo_ref[...] = xh * gam_ref[...] + bet_ref[...]

    def cur(part):
        return pl.BlockSpec((tm, d), lambda s: (s, part))

    row = pl.BlockSpec((tm, d), lambda i: (i, 0))
    vec = _full((1, d))
    return pl.pallas_call(
        body, name=name, grid=(s_len // tm,),
        in_specs=[cur(0), cur(1), cur(2),
                  _prev_spec(tm, pad, d, lambda g: 1), _prev_spec(tm, pad, d, lambda g: 2),
                  _full((ktaps, d)), vec, ANY_SPEC, row, vec, vec, vec],
        out_specs=[row, row, row, row, pl.BlockSpec((tm, 1), lambda i: (i, 0))],
        out_shape=[jax.ShapeDtypeStruct((s_len, d), BF16), jax.ShapeDtypeStruct((s_len, d), BF16),
                   jax.ShapeDtypeStruct((s_len, d), F32), jax.ShapeDtypeStruct((s_len, d), F32),
                   jax.ShapeDtypeStruct((s_len, 1), F32)],
        scratch_shapes=[pltpu.VMEM((pad + tm, d), F32), pltpu.VMEM((d, d), BF16), pltpu.SemaphoreType.DMA((NDEV,))],
        compiler_params=_cp("arbitrary"),
    )(bcv, bcv, bcv, bcv, bcv, cw, cb, wg, xres, gate, gam, bet)


def _gateconv_bwd(bcv, dy0, cw, cb, name):
    s_len, d3 = bcv.shape
    d = d3 // 3
    ktaps = cw.shape[0]
    pad = SHORT_PAD
    ts = _tile(s_len, CONV_ROWS)
    nsteps = s_len // ts

    def body(gb_ref, gc_ref, v_ref, gcp_ref, vp_ref, gbn_ref, dy_ref, dyn_ref, w_ref, b_ref,
             o_ref, dw_ref, db_ref, pbuf, dqbuf, dwacc):
        gb_ref, gc_ref, v_ref, gcp_ref, vp_ref, gbn_ref, dy_ref, dyn_ref = map(
            _F32Loads, (gb_ref, gc_ref, v_ref, gcp_ref, vp_ref, gbn_ref, dy_ref, dyn_ref))
        s = pl.program_id(0)

        @pl.when(s == 0)
        def _():
            dwacc[...] = jnp.zeros_like(dwacc)
            db_ref[...] = jnp.zeros_like(db_ref)

        pbuf[0:pad, :] = jnp.where(s > 0, gcp_ref[...] * vp_ref[...], 0.0)
        pbuf[pad:pad + ts, :] = gc_ref[...] * v_ref[...]
        dq = dy_ref[...] * gb_ref[...]
        dqbuf[0:ts, :] = dq
        dqbuf[ts:ts + pad, :] = jnp.where(s < nsteps - 1, dyn_ref[...] * gbn_ref[...], 0.0)
        db_ref[...] += jnp.sum(dq, axis=0, keepdims=True)
        for r0 in range(0, ts, CHUNK):
            rows = slice(r0, r0 + CHUNK)
            q = _conv_fwd_rows(_direct(pbuf), w_ref, b_ref, ktaps, pad, r0, CHUNK)
            o_ref[rows, 0:d] = (dy_ref[rows, :] * q).astype(BF16)
            dp = _conv_bwd_rows(_direct(dqbuf), pbuf[pad + r0:pad + r0 + CHUNK, :], w_ref, dwacc, ktaps, r0, CHUNK)
            o_ref[rows, d:2 * d] = (dp * v_ref[rows, :]).astype(BF16)
            o_ref[rows, 2 * d:3 * d] = (dp * gc_ref[rows, :]).astype(BF16)

        @pl.when(s == nsteps - 1)
        def _():
            _flush_dw(dwacc, dw_ref, ktaps)

    def cur(part):
        return pl.BlockSpec((ts, d), lambda s: (s, part))

    return pl.pallas_call(
        body, name=name, grid=(nsteps,),
        in_specs=[cur(0), cur(1), cur(2),
                  _prev_spec(ts, pad, d, lambda g: 1), _prev_spec(ts, pad, d, lambda g: 2),
                  _next_spec(ts, pad, d, lambda g: 0, s_len),
                  cur(0), _next_spec(ts, pad, d, lambda g: 0, s_len),
                  _full((ktaps, d)), _full((1, d))],
        out_specs=[pl.BlockSpec((ts, d3), lambda s: (s, 0)), _full((ktaps, d)), _full((1, d))],
        out_shape=[jax.ShapeDtypeStruct((s_len, d3), BF16), jax.ShapeDtypeStruct((ktaps, d), F32),
                   jax.ShapeDtypeStruct((1, d), F32)],
        scratch_shapes=[pltpu.VMEM((pad + ts, d), F32), pltpu.VMEM((ts + pad, d), F32),
                        pltpu.VMEM((ktaps * SUBLANES, d), F32)],
        compiler_params=_cp("arbitrary"),
    )(bcv, bcv, bcv, bcv, bcv, bcv, dy0, dy0, cw, cb)


class _Cols:
    def __init__(self, ref, cols):
        self.ref, self.cols = ref, cols

    def __getitem__(self, idx):
        return self.ref[slice(None) if idx is Ellipsis else idx[0], self.cols]

    def __setitem__(self, idx, value):
        self.ref[idx[0], self.cols] = value


def _ffn_tail_fwd(u0, vg, cw, cb, wg, xres, gate, gam, bet, alpha, name):
    s_len, f = u0.shape
    d = wg.shape[-1]
    r = f // NDEV
    ktaps = cw.shape[0]
    pad = SHORT_PAD
    tm = _tile(s_len, CONV_ROWS)
    cbk = COL_BLOCK if f % COL_BLOCK == 0 else f

    def body(u_ref, up_ref, vg_ref, cw_ref, cb_ref, wg_ref, x_ref, g_ref, gam_ref, bet_ref,
             t_ref, uc_ref, y_ref, xo_ref, xh_ref, rs_ref, ubuf, w_ref, sems):
        u_ref, up_ref, vg_ref = map(_F32Loads, (u_ref, up_ref, vg_ref))
        s = pl.program_id(0)

        @pl.when(s == 0)
        def _():
            _load_rows(wg_ref, r, 0, w_ref, sems)

        ubuf[0:pad, :] = jnp.where(s > 0, up_ref[...], 0.0)
        ubuf[pad:pad + tm, :] = u_ref[...]
        y = None
        for c0 in range(0, f, cbk):
            cols = slice(c0, c0 + cbk)
            read = _direct(_Cols(ubuf, cols))
            for r0 in range(0, tm, CHUNK):
                rows = slice(r0, r0 + CHUNK)
                u = _conv_fwd_rows(read, _Cols(cw_ref, cols), _Cols(cb_ref, cols), ktaps, pad, r0, CHUNK)
                t_ref[rows, cols] = (u * _sigmoid(u) * vg_ref[rows, cols]).astype(BF16)
                uc_ref[rows, cols] = u.astype(BF16)
            p = jnp.dot(t_ref[:, cols], w_ref[cols, :], preferred_element_type=F32)
            y = p if y is None else y + p
        z = alpha * x_ref[...] + g_ref[...] * y
        mu = jnp.mean(z, axis=-1, keepdims=True)
        zc = z - mu
        var = jnp.mean(zc * zc, axis=-1, keepdims=True)
        rstd = lax.rsqrt(var + LN_EPS)
        xh = zc * rstd
        y_ref[...] = y.astype(BF16)
        xh_ref[...] = xh
        rs_ref[...] = rstd
        xo_ref[...] = xh * gam_ref[...] + bet_ref[...]

    wide = pl.BlockSpec((tm, f), lambda i: (i, 0))
    row = pl.BlockSpec((tm, d), lambda i: (i, 0))
    vec = _full((1, d))
    return pl.pallas_call(
        body, name=name, grid=(s_len // tm,),
        in_specs=[wide, _prev_spec(tm, pad, f, lambda g: 0), wide, _full((ktaps, f)), _full((1, f)), ANY_SPEC,
                  row, vec, vec, vec],
        out_specs=[wide, wide, row, row, row, pl.BlockSpec((tm, 1), lambda i: (i, 0))],
        out_shape=[jax.ShapeDtypeStruct((s_len, f), BF16), jax.ShapeDtypeStruct((s_len, f), BF16),
                   jax.ShapeDtypeStruct((s_len, d), BF16),
                   jax.ShapeDtypeStruct((s_len, d), F32), jax.ShapeDtypeStruct((s_len, d), F32),
                   jax.ShapeDtypeStruct((s_len, 1), F32)],
        scratch_shapes=[pltpu.VMEM((pad + tm, f), F32), pltpu.VMEM((f, d), BF16), pltpu.SemaphoreType.DMA((NDEV,))],
        compiler_params=_cp("arbitrary"),
    )(u0, u0, vg, cw, cb, wg, xres, gate, gam, bet)


def _ffn_core_bwd(dy, u0, uc, vg, cw, wg_row, wg_col, xin, sc, dres, ln, alpha, name):
    s_len, f = u0.shape
    d = xin.shape[1]
    r = f // NDEV
    ktaps = cw.shape[0]
    pad = SHORT_PAD
    tm = _tile(s_len, CONV_ROWS)
    nsteps = s_len // tm
    cbk = COL_BLOCK if f % COL_BLOCK == 0 else f

    def body(dy_ref, dyn_ref, u_ref, uc_ref, ucn_ref, vg_ref, vgn_ref, cw_ref, wgr_ref, wgc_ref,
             x_ref, sc_ref, dres_ref, xh_ref, rs_ref, gam_ref, y_ref, g_ref,
             du0_ref, dvg_ref, dw_ref, db_ref, dyo_ref, dreso_ref, lnacc_ref, acc_ref,
             dtbuf, dubuf, dwacc, wd_ref, wup_ref, wgate_ref, sems):
        u_ref, uc_ref, ucn_ref, vg_ref, vgn_ref = map(_F32Loads, (u_ref, uc_ref, ucn_ref, vg_ref, vgn_ref))
        s = pl.program_id(0)
        last = s == nsteps - 1

        @pl.when(s == 0)
        def _():
            dwacc[...] = jnp.zeros_like(dwacc)
            db_ref[...] = jnp.zeros_like(db_ref)
            acc_ref[...] = jnp.zeros_like(acc_ref)
            lnacc_ref[...] = jnp.zeros_like(lnacc_ref)
            _load_rows(wgr_ref, r, 0, wd_ref, sems.at[0])
            _load_cols(wgc_ref, 0, wup_ref, sems.at[1])
            _load_cols(wgc_ref, 1, wgate_ref, sems.at[2])

        dy_cur, dy_nxt = dy_ref[...], dyn_ref[...]
        dh = None
        for c0 in range(0, f, cbk):
            cols = slice(c0, c0 + cbk)
            wd_blk = wd_ref[cols, :]
            dtbuf[0:tm, :] = lax.dot_general(dy_cur, wd_blk, NT, preferred_element_type=F32)
            dtbuf[tm:tm + pad, :] = jnp.where(
                last, 0.0, lax.dot_general(dy_nxt, wd_blk, NT, preferred_element_type=F32))
            for r0 in range(0, tm + pad, CHUNK):
                dtr = dtbuf[r0:r0 + CHUNK, :]
                if r0 < tm:
                    u, vgr = uc_ref[r0:r0 + CHUNK, cols], vg_ref[r0:r0 + CHUNK, cols]
                    sg = _sigmoid(u)
                    dvg_ref[r0:r0 + CHUNK, cols] = (dtr * u * sg).astype(BF16)
                else:
                    u, vgr = ucn_ref[r0 - tm:r0 - tm + CHUNK, cols], vgn_ref[r0 - tm:r0 - tm + CHUNK, cols]
                    sg = _sigmoid(u)
                dubuf[r0:r0 + CHUNK, :] = dtr * vgr * (sg * (1.0 + u * (1.0 - sg)))
            db_ref[:, cols] += jnp.sum(dubuf[0:tm, :], axis=0, keepdims=True)
            for r0 in range(0, tm, CHUNK):
                du0 = _conv_bwd_rows(_direct(dubuf), u_ref[r0:r0 + CHUNK, cols], _Cols(cw_ref, cols),
                                     _Cols(dwacc, cols), ktaps, r0, CHUNK)
                du0_ref[r0:r0 + CHUNK, cols] = du0.astype(BF16)
            p = (lax.dot_general(du0_ref[:, cols], wup_ref[:, cols], NT, preferred_element_type=F32)
                 + lax.dot_general(dvg_ref[:, cols], wgate_ref[:, cols], NT, preferred_element_type=F32))
            dh = p if dh is None else dh + p
        dx = dh * (1.0 + sc_ref[...]) + dres_ref[...]
        _ln_bwd_rows(dx, (xh_ref, rs_ref, gam_ref, y_ref, g_ref), (dyo_ref, dreso_ref, lnacc_ref), alpha)
        acc_ref[0:1, :] += jnp.sum(dh * x_ref[...], axis=0, keepdims=True)
        acc_ref[1:2, :] += jnp.sum(dh, axis=0, keepdims=True)

        @pl.when(last)
        def _():
            _flush_dw(dwacc, dw_ref, ktaps)

    wide = pl.BlockSpec((tm, f), lambda i: (i, 0))
    row = pl.BlockSpec((tm, d), lambda i: (i, 0))
    ln_out_specs, ln_out_shape = _ln_out_specs(s_len, tm, d)
    return pl.pallas_call(
        body, name=name, grid=(nsteps,),
        in_specs=[row, _next_spec(tm, pad, d, lambda g: 0, s_len),
                  wide, wide, _next_spec(tm, pad, f, lambda g: 0, s_len),
                  wide, _next_spec(tm, pad, f, lambda g: 0, s_len),
                  _full((ktaps, f)), ANY_SPEC, ANY_SPEC, row, _full((1, d)), row]
        + _ln_in_specs(tm, d),
        out_specs=[wide, wide, _full((ktaps, f)), _full((1, f))] + ln_out_specs + [_full((SUBLANES, d))],
        out_shape=[jax.ShapeDtypeStruct((s_len, f), BF16), jax.ShapeDtypeStruct((s_len, f), BF16),
                   jax.ShapeDtypeStruct((ktaps, f), F32), jax.ShapeDtypeStruct((1, f), F32)]
        + ln_out_shape + [jax.ShapeDtypeStruct((SUBLANES, d), F32)],
        scratch_shapes=[pltpu.VMEM((tm + pad, cbk), F32),
                        pltpu.VMEM((tm + pad, cbk), F32), pltpu.VMEM((ktaps * SUBLANES, f), F32),
                        pltpu.VMEM((f, d), BF16), pltpu.VMEM((d, f), BF16), pltpu.VMEM((d, f), BF16),
                        pltpu.SemaphoreType.DMA((3, NDEV))],
        compiler_params=_cp("arbitrary"),
    )(dy, dy, u0, uc, uc, vg, vg, cw, wg_row, wg_col, xin, sc, dres, *ln)


def _b_mid_fwd(ub, cw, cb, lng, lnb, name):
    s_len, d2 = ub.shape
    d = d2 // 2
    ktaps = cw.shape[0]
    pad = LONG_PAD
    ts = _tile(s_len, CONV_ROWS)

    def body(a_ref, g_ref, ap_ref, gp_ref, w_ref, b_ref, lng_ref, lnb_ref, a2_ref, a4_ref, abuf):
        a_ref, g_ref, ap_ref, gp_ref = map(_F32Loads, (a_ref, g_ref, ap_ref, gp_ref))
        s = pl.program_id(0)
        abuf[0, 0:pad, :] = jnp.where(s > 0, ap_ref[...] * _sigmoid(gp_ref[...]), 0.0)
        abuf[0, pad:pad + ts, :] = a_ref[...] * _sigmoid(g_ref[...])
        _make_shifts(abuf, pad + ts)
        for r0 in range(0, ts, CHUNK):
            a2_ref[r0:r0 + CHUNK, :] = _conv_fwd_rows(_shifted(abuf), w_ref, b_ref, ktaps, pad, r0, CHUNK)
        a2 = a2_ref[...]
        mu = jnp.mean(a2, axis=-1, keepdims=True)
        ac = a2 - mu
        var = jnp.mean(ac * ac, axis=-1, keepdims=True)
        a3 = ac * lax.rsqrt(var + LN_EPS) * lng_ref[...] + lnb_ref[...]
        a4_ref[...] = (a3 * _sigmoid(a3)).astype(BF16)

    def cur(part):
        return pl.BlockSpec((ts, d), lambda s: (s, part))

    vec = _full((1, d))
    return pl.pallas_call(
        body, name=name, grid=(s_len // ts,),
        in_specs=[cur(0), cur(1), _prev_spec(ts, pad, d, lambda g: 0), _prev_spec(ts, pad, d, lambda g: 1),
                  _full((ktaps, d)), vec, vec, vec],
        out_specs=[cur(0), cur(0)],
        out_shape=[jax.ShapeDtypeStruct((s_len, d), F32), jax.ShapeDtypeStruct((s_len, d), BF16)],
        scratch_shapes=[pltpu.VMEM((SUBLANES, pad + ts, d), F32)],
        compiler_params=_cp("parallel"),
    )(ub, ub, ub, ub, cw, cb, lng, lnb)


def _b_mid_bwd(ub, a2, da4, cw, lng, lnb, name):
    s_len, d2 = ub.shape
    d = d2 // 2
    ktaps = cw.shape[0]
    pad = LONG_PAD
    ts = _tile(s_len, CONV_ROWS)
    nsteps = s_len // ts

    def body(a_ref, g_ref, a2_ref, a2n_ref, da4_ref, da4n_ref, w_ref, lng_ref, lnb_ref,
             du_ref, dw_ref, db_ref, dlng_ref, dlnb_ref, dbias_ref, dabuf, dwacc):
        a_ref, g_ref, da4_ref, da4n_ref = map(_F32Loads, (a_ref, g_ref, da4_ref, da4n_ref))
        s = pl.program_id(0)
        last = s == nsteps - 1

        @pl.when(s == 0)
        def _():
            dwacc[...] = jnp.zeros_like(dwacc)
            for ref in (db_ref, dlng_ref, dlnb_ref, dbias_ref):
                ref[...] = jnp.zeros_like(ref)

        def ln_silu_bwd(a2_t, da4_t):
            mu = jnp.mean(a2_t, axis=-1, keepdims=True)
            ac = a2_t - mu
            var = jnp.mean(ac * ac, axis=-1, keepdims=True)
            rstd = lax.rsqrt(var + LN_EPS)
            ah = ac * rstd
            a3 = ah * lng_ref[...] + lnb_ref[...]
            sg = _sigmoid(a3)
            da3 = da4_t * (sg * (1.0 + a3 * (1.0 - sg)))
            dah = da3 * lng_ref[...]
            m1 = jnp.mean(dah, axis=-1, keepdims=True)
            m2 = jnp.mean(dah * ah, axis=-1, keepdims=True)
            return rstd * (dah - m1 - ah * m2), da3, ah

        da2, da3, ah = ln_silu_bwd(a2_ref[...], da4_ref[...])
        dabuf[0, 0:ts, :] = da2
        dlng_ref[...] += jnp.sum(da3 * ah, axis=0, keepdims=True)
        dlnb_ref[...] += jnp.sum(da3, axis=0, keepdims=True)
        db_ref[...] += jnp.sum(da2, axis=0, keepdims=True)
        da2n, _, _ = ln_silu_bwd(a2n_ref[...], jnp.where(last, 0.0, da4n_ref[...]))
        dabuf[0, ts:ts + pad, :] = da2n
        _make_shifts(dabuf, ts + pad)
        for r0 in range(0, ts, CHUNK):
            rows = slice(r0, r0 + CHUNK)
            a_r, g_r = a_ref[rows, :], g_ref[rows, :]
            sg = _sigmoid(g_r)
            da1 = _conv_bwd_rows(_shifted(dabuf), a_r * sg, w_ref, dwacc, ktaps, r0, CHUNK)
            da = da1 * sg
            dg = da1 * a_r * sg * (1.0 - sg)
            du_ref[rows, 0:d] = da.astype(BF16)
            du_ref[rows, d:2 * d] = dg.astype(BF16)
            dbias_ref[:, 0:d] += jnp.sum(da, axis=0, keepdims=True)
            dbias_ref[:, d:2 * d] += jnp.sum(dg, axis=0, keepdims=True)

        @pl.when(last)
        def _():
            _flush_dw(dwacc, dw_ref, ktaps)

    def cur(part):
        return pl.BlockSpec((ts, d), lambda s: (s, part))

    vec = _full((1, d))
    nxt = _next_spec(ts, pad, d, lambda g: 0, s_len)
    return pl.pallas_call(
        body, name=name, grid=(nsteps,),
        in_specs=[cur(0), cur(1), cur(0), nxt, cur(0), nxt, _full((ktaps, d)), vec, vec],
        out_specs=[pl.BlockSpec((ts, d2), lambda s: (s, 0)), _full((ktaps, d)), vec, vec, vec, _full((1, d2))],
        out_shape=[jax.ShapeDtypeStruct((s_len, d2), BF16), jax.ShapeDtypeStruct((ktaps, d), F32),
                   jax.ShapeDtypeStruct((1, d), F32), jax.ShapeDtypeStruct((1, d), F32),
                   jax.ShapeDtypeStruct((1, d), F32), jax.ShapeDtypeStruct((1, d2), F32)],
        scratch_shapes=[pltpu.VMEM((SUBLANES, ts + pad, d), F32), pltpu.VMEM((ktaps * SUBLANES, d), F32)],
        compiler_params=_cp("arbitrary"),
    )(ub, ub, a2, a2, da4, da4, cw, lng, lnb)


def _loss_head(xo, tgt, ln, alpha, name):
    s_len, d = xo.shape
    tm = _tile(s_len, MM_ROWS)

    def body(x_ref, t_ref, xh_ref, rs_ref, gam_ref, y_ref, g_ref, dy_ref, dres_ref, acc_ref, l_ref):
        @pl.when(pl.program_id(0) == 0)
        def _():
            l_ref[...] = jnp.zeros_like(l_ref)
            acc_ref[...] = jnp.zeros_like(acc_ref)

        e = x_ref[...] - t_ref[...]
        per_row = jnp.sum(e * e, axis=-1, keepdims=True) * (1.0 / d)
        l_ref[...] += 0.5 * jnp.sum(per_row, axis=0, keepdims=True)
        _ln_bwd_rows(e * (1.0 / d), (xh_ref, rs_ref, gam_ref, y_ref, g_ref), (dy_ref, dres_ref, acc_ref), alpha)

    row = pl.BlockSpec((tm, d), lambda i: (i, 0))
    ln_out_specs, ln_out_shape = _ln_out_specs(s_len, tm, d)
    return pl.pallas_call(
        body, name=name, grid=(s_len // tm,),
        in_specs=[row, row] + _ln_in_specs(tm, d), out_specs=ln_out_specs + [_full((1, LANES))],
        out_shape=ln_out_shape + [jax.ShapeDtypeStruct((1, LANES), F32)],
        compiler_params=_cp("arbitrary"),
    )(xo, tgt, *ln)


def _ada_fwd(c_all, ada_w, ada_b_loc, name):
    depth, d, n = ada_w.shape

    def body(c_ref, w_ref, b_ref, o_ref):
        c = c_ref[...]
        act = c * _sigmoid(c)
        o_ref[...] = jnp.dot(act, w_ref[...], preferred_element_type=F32,
                             precision=lax.Precision.HIGHEST) + b_ref[...]

    return pl.pallas_call(
        body, name=name, grid=(depth,),
        in_specs=[_full((NDEV, d)), pl.BlockSpec((None, d, n), lambda i: (i, 0, 0)),
                  pl.BlockSpec((None, 1, n), lambda i: (i, 0, 0))],
        out_specs=pl.BlockSpec((None, NDEV, n), lambda i: (i, 0, 0)),
        out_shape=jax.ShapeDtypeStruct((depth, NDEV, n), F32),
        compiler_params=_cp("parallel"),
    )(c_all, ada_w, ada_b_loc.reshape(depth, 1, n))


def _ada_bwd(c_all_t, dmod_cols, name):
    depth, _, n = dmod_cols.shape
    d = c_all_t.shape[0]

    def body(ct_ref, dm_ref, o_ref):
        ct = ct_ref[...]
        act = ct * _sigmoid(ct)
        acc = None
        for b in range(NDEV):
            term = act[:, b:b + 1] * dm_ref[b:b + 1, :]
            acc = term if acc is None else acc + term
        o_ref[...] = acc

    return pl.pallas_call(
        body, name=name, grid=(depth,),
        in_specs=[_full((d, NDEV)), pl.BlockSpec((None, NDEV, n), lambda i: (i, 0, 0))],
        out_specs=pl.BlockSpec((None, d, n), lambda i: (i, 0, 0)),
        out_shape=jax.ShapeDtypeStruct((depth, d, n), F32),
        compiler_params=_cp("parallel"),
    )(c_all_t, dmod_cols)


def _sum_parts(parts, name):
    _, rows, lanes = parts.shape

    def body(p_ref, o_ref):
        acc = p_ref[0]
        for k in range(1, NDEV):
            acc = acc + p_ref[k]
        o_ref[...] = acc

    return pl.pallas_call(
        body, name=name, in_specs=[_full(parts.shape)], out_specs=_full((rows, lanes)), grid=(1,),
        out_shape=jax.ShapeDtypeStruct((rows, lanes), F32), compiler_params=_cp("arbitrary"),
    )(parts)


def _adamw(w, glist, m, v, name):
    nl, rows, cols = w.shape
    tr = _tile(rows, 256, 2 * SUBLANES)

    def body(w_ref, *rest):
        g_refs = rest[:nl]
        m_ref, v_ref, go_ref, d_ref, mo_ref, vo_ref = rest[nl:]
        g = None
        for layer, g_ref in enumerate(g_refs):
            part = g_ref[0].astype(F32)
            for p in range(1, g_ref.shape[0]):
                part = part + g_ref[p].astype(F32)
            g = part if g is None else jnp.where(pl.program_id(0) == layer, part, g)
        go_ref[...] = g
        d_ref[...], mo_ref[...], vo_ref[...] = _adam_step(w_ref[...], g, m_ref[...], v_ref[...])

    blk = pl.BlockSpec((None, tr, cols), lambda l, i: (l, i, 0))
    g_specs = [pl.BlockSpec((g.shape[0], tr, cols), lambda l, i: (0, i, 0)) for g in glist]
    return pl.pallas_call(
        body, name=name, grid=(nl, rows // tr),
        in_specs=[blk] + g_specs + [blk, blk],
        out_specs=[blk] * 4, out_shape=[jax.ShapeDtypeStruct((nl, rows, cols), F32)] * 4,
        compiler_params=_cp("parallel", "parallel"),
    )(w, *glist, m, v)


def _adam_step(w, g, m, v):
    m1 = ADAM_B1 * m + (1.0 - ADAM_B1) * g
    v1 = ADAM_B2 * v + (1.0 - ADAM_B2) * (g * g)
    m_hat = m1 / (1.0 - ADAM_B1 ** ADAM_STEP)
    v_hat = v1 / (1.0 - ADAM_B2 ** ADAM_STEP)
    return -ADAM_LR * (m_hat / (jnp.sqrt(v_hat) + ADAM_EPS) + ADAM_WD * w), m1, v1


def _adamw_small(ws, gs, ms, vs, name):
    n = len(ws)

    def body(*refs):
        ins, outs = refs[:4 * n], refs[4 * n:]
        for i in range(n):
            w_ref, g_ref, m_ref, v_ref = ins[i], ins[n + i], ins[2 * n + i], ins[3 * n + i]
            delta, m1, v1 = _adam_step(w_ref[...], g_ref[...], m_ref[...], v_ref[...])
            outs[3 * i][...] = delta
            outs[3 * i + 1][...] = m1
            outs[3 * i + 2][...] = v1

    operands = list(ws) + list(gs) + list(ms) + list(vs)
    out_shape = [jax.ShapeDtypeStruct(w.shape, F32) for w in ws for _ in range(3)]
    return pl.pallas_call(
        body, name=name, grid=(1,), in_specs=[_full(a.shape) for a in operands],
        out_specs=[_full(s.shape) for s in out_shape], out_shape=out_shape,
        compiler_params=_cp("arbitrary"),
    )(*operands)


def _pack(pieces):
    flat = jnp.concatenate([p.reshape(-1) for p in pieces])
    unit = SUBLANES * LANES
    padded = -(-flat.shape[0] // unit) * unit
    return jnp.pad(flat, (0, padded - flat.shape[0])).reshape(padded // LANES, LANES)


def _unpack(packed, shapes, lead=()):
    flat = packed.reshape(lead + (-1,))
    out, off = [], 0
    for s in shapes:
        size = 1
        for dim in s:
            size *= dim
        out.append(flat[..., off:off + size].reshape(lead + tuple(s)))
        off += size
    return out


def _pad_last(a, n):
    return jnp.pad(a, [(0, 0)] * (a.ndim - 1) + [(0, n - a.shape[-1])])


def kernel(x, c, ada_w, ada_b, ln_tok_g, ln_tok_b, ln_ch_g, ln_ch_b, a_w_in, a_conv_w, a_conv_b, a_w_out, b_w_pw1, b_b_pw1, b_conv_w, b_conv_b, b_ln_g, b_ln_b, b_w_pw2, b_b_pw2, f_w_up, f_conv_w, f_conv_b, f_w_gate, f_w_down, loss_target, m_ada_w, m_ada_b, m_ln_tok_g, m_ln_tok_b, m_ln_ch_g, m_ln_ch_b, m_a_w_in, m_a_conv_w, m_a_conv_b, m_a_w_out, m_b_w_pw1, m_b_b_pw1, m_b_conv_w, m_b_conv_b, m_b_ln_g, m_b_ln_b, m_b_w_pw2, m_b_b_pw2, m_f_w_up, m_f_conv_w, m_f_conv_b, m_f_w_gate, m_f_w_down, v_ada_w, v_ada_b, v_ln_tok_g, v_ln_tok_b, v_ln_ch_g, v_ln_ch_b, v_a_w_in, v_a_conv_w, v_a_conv_b, v_a_w_out, v_b_w_pw1, v_b_b_pw1, v_b_conv_w, v_b_conv_b, v_b_ln_g, v_b_ln_b, v_b_w_pw2, v_b_b_pw2, v_f_w_up, v_f_conv_w, v_f_conv_b, v_f_w_gate, v_f_w_down):
    weights = dict(ada_w=ada_w, ada_b=ada_b, ln_tok_g=ln_tok_g, ln_tok_b=ln_tok_b, ln_ch_g=ln_ch_g, ln_ch_b=ln_ch_b, a_w_in=a_w_in, a_conv_w=a_conv_w, a_conv_b=a_conv_b, a_w_out=a_w_out, b_w_pw1=b_w_pw1, b_b_pw1=b_b_pw1, b_conv_w=b_conv_w, b_conv_b=b_conv_b, b_ln_g=b_ln_g, b_ln_b=b_ln_b, b_w_pw2=b_w_pw2, b_b_pw2=b_b_pw2, f_w_up=f_w_up, f_conv_w=f_conv_w, f_conv_b=f_conv_b, f_w_gate=f_w_gate, f_w_down=f_w_down)
    mom_m = dict(ada_w=m_ada_w, ada_b=m_ada_b, ln_tok_g=m_ln_tok_g, ln_tok_b=m_ln_tok_b, ln_ch_g=m_ln_ch_g, ln_ch_b=m_ln_ch_b, a_w_in=m_a_w_in, a_conv_w=m_a_conv_w, a_conv_b=m_a_conv_b, a_w_out=m_a_w_out, b_w_pw1=m_b_w_pw1, b_b_pw1=m_b_b_pw1, b_conv_w=m_b_conv_w, b_conv_b=m_b_conv_b, b_ln_g=m_b_ln_g, b_ln_b=m_b_ln_b, b_w_pw2=m_b_w_pw2, b_b_pw2=m_b_b_pw2, f_w_up=m_f_w_up, f_conv_w=m_f_conv_w, f_conv_b=m_f_conv_b, f_w_gate=m_f_w_gate, f_w_down=m_f_w_down)
    mom_v = dict(ada_w=v_ada_w, ada_b=v_ada_b, ln_tok_g=v_ln_tok_g, ln_tok_b=v_ln_tok_b, ln_ch_g=v_ln_ch_g, ln_ch_b=v_ln_ch_b, a_w_in=v_a_w_in, a_conv_w=v_a_conv_w, a_conv_b=v_a_conv_b, a_w_out=v_a_w_out, b_w_pw1=v_b_w_pw1, b_b_pw1=v_b_b_pw1, b_conv_w=v_b_conv_w, b_conv_b=v_b_conv_b, b_ln_g=v_b_ln_g, b_ln_b=v_b_ln_b, b_w_pw2=v_b_w_pw2, b_b_pw2=v_b_b_pw2, f_w_up=v_f_w_up, f_conv_w=v_f_conv_w, f_conv_b=v_f_conv_b, f_w_gate=v_f_w_gate, f_w_down=v_f_w_down)
    names = list(weights)

    depth, d, n_ada = ada_w.shape
    assert depth == 2 and a_w_in.shape[0] == 1 and b_w_pw1.shape[0] == 1
    s_len = x.shape[1]
    f_loc = f_w_up.shape[-1]
    f_pad = -(-f_loc // LANES) * LANES
    f_all = NDEV * f_pad
    d_loc = d // NDEV
    ka, kb, kf = a_conv_w.shape[1], b_conv_w.shape[1], f_conv_w.shape[1]
    alpha = (2.0 * depth) ** 0.25
    assert a_w_in.shape[-1] == f_pad and f_pad % d_loc == 0
    me = 4 * lax.axis_index("x") + 2 * lax.axis_index("y") + lax.axis_index("c")

    small_shapes = [(d,), (ka, d_loc), (2 * d_loc,), (kb, d_loc), (d_loc,), (d_loc,), (d_loc,), (d_loc,),
                    (depth, kf, f_pad)]
    small_loc = _pack([c[0], a_conv_w[0], b_b_pw1[0], b_conv_w[0], b_conv_b[0], b_ln_g[0], b_ln_b[0],
                       b_b_pw2[0], _pad_last(f_conv_w, f_pad)])
    first = _exchange_start([small_loc, a_w_in.astype(BF16)], "gather_chips", "gather_first_chips_start")
    up_pad = _pad_last(_after(f_w_up, first[-1]), f_pad).astype(BF16)
    gate_pad = _pad_last(_after(f_w_gate, first[-1]), f_pad).astype(BF16)
    down_pad = jnp.pad(_after(f_w_down, first[-1]), ((0, 0), (0, f_pad - f_loc), (0, 0))).astype(BF16)
    col_f = [jnp.stack([up_pad[i], gate_pad[i]]) for i in range(depth)]
    row_b = jnp.concatenate([down_pad[1], _after(b_w_pw2[0], first[-1]).astype(BF16)], axis=0)
    out_loc, pw1_loc = _after(a_w_out[0], first[-1]).astype(BF16), _after(b_w_pw1, first[-1]).astype(BF16)
    ridx_pw2 = f_pad // d_loc
    prepared = sum(a.reshape(-1, a.shape[-1])[0:1, 0:LANES].astype(F32)
                   for a in (col_f[0], col_f[1], down_pad[0], row_b, out_loc, pw1_loc))
    *lands, landed = _exchange_wait(first, prepared, "gather_chips", "gather_first_chips_wait")
    g_small, g_in, _ = _exchange_wait(_exchange_start(lands, "forward", "gather_first_forward_start"), landed,
                                      "forward", "gather_first_forward_wait")

    (c_all, acw_g, bb1_g, bcw_g, bcb_g, blg_g, blb_g, bb2_g, fcw_g) = _unpack(g_small, small_shapes, (NDEV,))
    a_cw = acw_g.transpose(1, 0, 2).reshape(ka, d)
    b_cw = bcw_g.transpose(1, 0, 2).reshape(kb, d)
    b_b1 = bb1_g.reshape(1, 2 * d)
    b_cb, b_lg, b_lb, b_b2 = (t.reshape(1, d) for t in (bcb_g, blg_g, blb_g, bb2_g))
    f_cw = fcw_g.transpose(1, 2, 0, 3).reshape(depth, kf, f_all)
    f_cb = _pad_last(f_conv_b.reshape(depth, NDEV, f_loc), f_pad).reshape(depth, 1, f_all)

    ada_b_loc = lax.dynamic_slice(ada_b, (0, me * n_ada), (depth, n_ada))
    mod_part = _ada_fwd(c_all, ada_w, ada_b_loc, "ada_fwd")
    mod_g, mod_done = _exchange([mod_part.reshape(depth * NDEV, n_ada)], "gather", "gather_mod")
    mod_all = mod_g.reshape(NDEV, depth, NDEV, n_ada).transpose(1, 2, 0, 3).reshape(depth, NDEV, 6 * d)
    mod = lax.dynamic_slice(mod_all, (0, me, 0), (depth, 1, 6 * d))[:, 0]

    def behind(a, token):
        return a + token[0, 0].astype(BF16)

    gather_out = _exchange_start([behind(out_loc, mod_done)], "gather_chips", "gather_out_start")
    gather_f0 = _exchange_start([col_f[0], behind(down_pad[0], gather_out[-1])], "gather_chips", "gather_f0_start")

    def mod_rows(i):
        return [mod[i:i + 1, j * d:(j + 1) * d] for j in range(6)]

    zeros_d = jnp.zeros((1, d), F32)
    zeros_f = jnp.zeros((1, f_all), F32)
    x0 = x[0]

    sh_t0, sc_t0, g_t0, sh_c0, sc_c0, g_c0 = mod_rows(0)
    sh_t1, sc_t1, g_t1, sh_c1, sc_c1, g_c1 = mod_rows(1)

    sc_t0 = _after(sc_t0, gather_f0[-1])
    bcv, = _mm_fwd(x0, sc_t0, sh_t0, jnp.zeros((1, 3 * d), F32), g_in, (0,), "a_in_fwd")
    g_out, landed = _exchange_wait(gather_out, bcv, "gather_chips", "gather_out_wait")
    g_out, _ = _exchange_wait(_exchange_start([g_out], "forward", "gather_out_fwd_start"), landed, "forward",
                              "gather_out_fwd_wait")
    y0, y_a, x1, xh1, rs1 = _a_tail_fwd(bcv, a_cw, a_conv_b, g_out, x0, g_t0, ln_tok_g[0:1], ln_tok_b[0:1],
                                        alpha, "a_tail_fwd")

    def ffn_fwd(xin, sc, sh, gate, gam, bet, g_colf, g_rowf, layer, tag):
        u0, vg = _mm_fwd(xin, sc, sh, zeros_f, g_colf, (0, 1), "f_upgate_fwd" + tag)
        t, uc, y, xo, xh, rs = _ffn_tail_fwd(u0, vg, f_cw[layer], f_cb[layer], g_rowf, xin, gate, gam, bet, alpha,
                                             "f_tail_fwd" + tag)
        return (u0, uc), vg, t, y, xo, xh, rs

    g_colf0, g_rowf0, landed = _exchange_wait(gather_f0, x1, "gather_chips", "gather_f0_wait")
    g_colf0, g_rowf0, landed = _exchange_wait(
        _exchange_start([g_colf0, g_rowf0], "forward", "gather_f0_fwd_start"), landed, "forward", "gather_f0_fwd_wait")
    gather_1 = _exchange_start([behind(pw1_loc, landed), col_f[1], row_b], "gather_chips", "gather_1_start")
    sc_c0 = _after(sc_c0, gather_1[-1])
    u0_0, vg_0, t_0, y_f0, x2, xh2, rs2 = ffn_fwd(x1, sc_c0, sh_c0, g_c0, ln_ch_g[0:1], ln_ch_b[0:1],
                                                  g_colf0, g_rowf0, 0, "0")

    *lands_1, landed = _exchange_wait(gather_1, x2, "gather_chips", "gather_1_wait")
    g_pw1, g_colf1, g_rowb, _ = _exchange_wait(_exchange_start(lands_1, "forward", "gather_1_fwd_start"), landed,
                                                 "forward", "gather_1_fwd_wait")
    ub, = _mm_fwd(x2, sc_t1, sh_t1, b_b1, g_pw1, (0,), "b_pw1_fwd")
    a2, a4 = _b_mid_fwd(ub, b_cw, b_cb, b_lg, b_lb, "b_mid_fwd")
    y_b, x3, xh3, rs3 = _mm_ln(a4, g_rowb, d_loc, ridx_pw2, x2, g_t1, ln_tok_g[1:2], ln_tok_b[1:2], b_b2,
                               alpha, "b_pw2_ln_fwd")
    u0_1, vg_1, t_1, y_f1, x4, xh4, rs4 = ffn_fwd(x3, sc_c1, sh_c1, g_c1, ln_ch_g[1:2], ln_ch_b[1:2],
                                                  g_colf1, g_rowb, 1, "1")

    ln_f1 = (xh4, rs4, ln_ch_g[1:2], y_f1, g_c1)
    ln_b = (xh3, rs3, ln_tok_g[1:2], y_b, g_t1)
    ln_f0 = (xh2, rs2, ln_ch_g[0:1], y_f0, g_c0)
    ln_a = (xh1, rs1, ln_tok_g[0:1], y_a, g_t0)
    dy, dres, accf1, loss_part = _loss_head(x4, loss_target[0], ln_f1, alpha, "loss_head")

    def ffn_bwd(dy, dres, xin, sc, sh, u0, vg, t, g_colf, g_rowf, ln_below, layer, tag):
        dw_down = _mm_tn_row(t, dy, f_pad, f_loc, "f_down_dw" + tag)
        scatter_down = _exchange_start([dw_down], "scatter", "scatter_d%s_start" % tag)
        du0, dvg, dcw, dcb, dy_below, dres_below, acc_below, acc2 = _ffn_core_bwd(
            dy, u0[0], u0[1], vg, f_cw[layer], g_rowf, g_colf, xin, _after(sc, scatter_down[-1]), dres,
            ln_below, alpha, "f_core_bwd" + tag)
        dw_up = _mm_tn_col_t(xin, sc, sh, du0, f_loc, "f_up_dw" + tag)
        dw_gate = _mm_tn_col_t(xin, sc, sh, dvg, f_loc, "f_gate_dw" + tag)
        scatter = _exchange_start([dw_up, dw_gate], "scatter", "scatter_f%s_start" % tag)
        return dy_below, dres_below, acc_below, acc2, (scatter, scatter_down), dcw, dcb

    dy, dres, accb, acc2f1, (scatter_f1, scatter_d1), dfcw1, dfcb1 = ffn_bwd(
        dy, dres, x3, sc_c1, sh_c1, u0_1, vg_1, t_1, g_colf1, g_rowb, ln_b, 1, "1")

    da4 = _mm_nt_row(dy, g_rowb, d_loc, ridx_pw2, "b_pw2_dx")
    dw_pw2 = _mm_tn_row(a4, dy, d_loc, d_loc, "b_pw2_dw")
    du, dbcw, dbcb, dblg, dblb, dbb1 = _b_mid_bwd(ub, a2, da4, b_cw, _after(b_lg, scatter_f1[-1]), b_lb, "b_mid_bwd")
    dw_pw1 = _mm_tn_col(x2, sc_t1, sh_t1, du, "b_pw1_dw")
    scatter_b = _exchange_start([dw_pw1, dw_pw2], "scatter", "scatter_b_start")
    dy, dres, accf0, acc2b = _mm_nt_mod([du], g_pw1, (0,), x2, _after(sc_t1, scatter_b[-1]), dres, "b_pw1_dx",
                                        ln=ln_f0, alpha=alpha)

    dy, dres, acca, acc2f0, (scatter_f0, scatter_d0), dfcw0, dfcb0 = ffn_bwd(
        dy, dres, x1, sc_c0, sh_c0, u0_0, vg_0, t_0, g_colf0, g_rowf0, ln_a, 0, "0")

    dy0 = _mm_nt_row(dy, g_out, d_loc, 0, "a_out_dx")
    dbcv, dacw, dacb = _gateconv_bwd(bcv, dy0, a_cw, _after(a_conv_b, scatter_f0[-1]), "a_conv_bwd")
    dx0, acc2a = _mm_nt_mod([dbcv], g_in, (0,), x0, sc_t0, dres, "a_in_dx")

    def dmod_row(acc2_t, acc_t, acc2_c, acc_c):
        return jnp.concatenate([acc2_t[1], acc2_t[0], acc_t[2], acc2_c[1], acc2_c[0], acc_c[2]])

    dmod = jnp.stack([dmod_row(acc2a, acca, acc2f0, accf0), dmod_row(acc2b, accb, acc2f1, accf1)])

    def unpad_f(a):
        return a.reshape(a.shape[:-1] + (NDEV, f_pad))[..., :f_loc].reshape(a.shape[:-1] + (NDEV * f_loc,))

    small_grads = [
        dmod,
        jnp.stack([acca[0], accb[0]]), jnp.stack([acca[1], accb[1]]),
        jnp.stack([accf0[0], accf1[0]]), jnp.stack([accf0[1], accf1[1]]),
        dacb,
        unpad_f(jnp.concatenate([dfcb0, dfcb1], axis=0)),
        dacw, dbb1, dbcw, dbcb, dblg, dblb, accb[3:4],
        jnp.stack([dfcw0, dfcw1]),
        loss_part[0:1, 0:1],
    ]
    small_grad_shapes = [tuple(g.shape) for g in small_grads]
    gather_small = _exchange_start([_pack(small_grads)], "gather", "gather_small_start")

    dw_in = _mm_tn_col(x0, _after(sc_t0, gather_small[-1]), sh_t0, dbcv, "a_in_dw")
    dw_out = _mm_tn_row(y0, dy, d_loc, d_loc, "a_out_dw")
    scatter_a = _exchange_start([dw_in, dw_out], "scatter", "scatter_a_start")

    grads, deltas, new_m, new_v = {}, {}, {}, {}

    def adamw(k, glist, transposed=False):
        def view(a):
            a = jnp.swapaxes(a, 1, 2) if transposed else a
            return a.reshape(len(glist), -1, a.shape[-1])

        w = view(weights[k])
        outs = _adamw(w, [g.reshape(g.shape[0], -1, w.shape[-1]) for g in glist],
                      view(mom_m[k]), view(mom_v[k]), "adamw_" + k)
        if transposed:
            outs = [jnp.swapaxes(o, 1, 2) for o in outs]
        grads[k], deltas[k], new_m[k], new_v[k] = (o.reshape(weights[k].shape) for o in outs)

    r_up1, r_gate1, _ = _exchange_wait(scatter_f1, scatter_a[-1], "scatter", "scatter_f1_wait")
    r_down1, _ = _exchange_wait(scatter_d1, r_gate1, "scatter", "scatter_d1_wait")
    r_pw1, r_pw2, _ = _exchange_wait(scatter_b, r_down1, "scatter", "scatter_b_wait")
    adamw("b_w_pw1", [r_pw1])
    adamw("b_w_pw2", [r_pw2])
    r_down0, _ = _exchange_wait(scatter_d0, deltas["b_w_pw2"], "scatter", "scatter_d0_wait")
    adamw("f_w_down", [r_down0, r_down1])
    r_up0, r_gate0, _ = _exchange_wait(scatter_f0, deltas["f_w_down"], "scatter", "scatter_f0_wait")
    adamw("f_w_up", [r_up0, r_up1], transposed=True)
    adamw("f_w_gate", [r_gate0, r_gate1], transposed=True)

    sg_all, _ = _exchange_wait(gather_small, deltas["f_w_gate"], "gather", "gather_small_wait")
    sg_sum = _sum_parts(sg_all, "sum_small_grads")
    (g_ada_b, g_ltg, g_ltb, g_lcg, g_lcb, g_acb, g_fcb, g_acw, g_bb1, g_bcw, g_bcb, g_blg, g_blb, g_bb2,
     g_fcw, loss_all) = _unpack(sg_sum, small_grad_shapes)
    loss = loss_all[0, 0]

    def my_cols(a, width):
        return lax.dynamic_slice_in_dim(a, me * width, width, axis=a.ndim - 1)

    g_fcw_loc = my_cols(g_fcw, f_pad)[..., :f_loc]
    small = dict(
        ada_b=g_ada_b, ln_tok_g=g_ltg, ln_tok_b=g_ltb, ln_ch_g=g_lcg, ln_ch_b=g_lcb, a_conv_b=g_acb, f_conv_b=g_fcb,
        a_conv_w=my_cols(g_acw, d_loc)[None], b_b_pw1=my_cols(g_bb1, 2 * d_loc), b_conv_w=my_cols(g_bcw, d_loc)[None],
        b_conv_b=my_cols(g_bcb, d_loc), b_ln_g=my_cols(g_blg, d_loc), b_ln_b=my_cols(g_blb, d_loc),
        b_b_pw2=my_cols(g_bb2, d_loc), f_conv_w=g_fcw_loc)

    dmod_all = sg_all.reshape(NDEV, -1)[:, :depth * 6 * d].reshape(NDEV, depth, 6 * d)
    dmod_cols = my_cols(dmod_all, n_ada).transpose(1, 0, 2)
    g_ada_w = _ada_bwd(c_all.T, dmod_cols, "ada_bwd")

    adamw("ada_w", [g_ada_w[0:1], g_ada_w[1:2]])

    def rows_cols(a):
        return a.reshape(-1, a.shape[-1])

    small_keys = list(small)
    small_outs = _adamw_small([rows_cols(weights[k]) for k in small_keys], [rows_cols(small[k]) for k in small_keys],
                              [rows_cols(mom_m[k]) for k in small_keys], [rows_cols(mom_v[k]) for k in small_keys],
                              "adamw_small")
    for i, k in enumerate(small_keys):
        grads[k] = small[k].reshape(weights[k].shape)
        deltas[k], new_m[k], new_v[k] = (o.reshape(weights[k].shape) for o in small_outs[3 * i:3 * i + 3])

    r_in, r_out, _ = _exchange_wait(scatter_a, deltas["ada_w"], "scatter", "scatter_a_wait")
    adamw("a_w_in", [r_in])
    adamw("a_w_out", [r_out])

    return (loss, dx0[None], *[grads[k] for k in names], *[deltas[k] for k in names],
            *[new_m[k] for k in names], *[new_v[k] for k in names])
```

```python
import jax
import jax.numpy as jnp
from jax import lax
from jax.experimental import pallas as pl
from jax.experimental.pallas import tpu as pltpu

NDEV = 8
MESH_AXES = ("x", "y", "c")
LANES = 128
SUBLANES = 8
VMEM_LIMIT = 56 * 1024 * 1024
LN_EPS = 1e-5
SHORT_PAD = 16
LONG_PAD = 32
CHUNK = 16
DW_ROWS = 1024
MM_ROWS = 512
CONV_ROWS = 256
COL_BLOCK = 1024
ADAM_LR, ADAM_B1, ADAM_B2, ADAM_EPS, ADAM_WD, ADAM_STEP = 0.001, 0.9, 0.999, 1e-08, 0.01, 10

F32 = jnp.float32
BF16 = jnp.bfloat16
MESH = pl.DeviceIdType.MESH
NT = (((1,), (1,)), ((), ()))
TN = (((0,), (0,)), ((), ()))


def _tile(n, target, mult=SUBLANES):
    best = None
    for t in range(mult, min(n, target) + 1, mult):
        if n % t == 0:
            best = t
    return best if best is not None else n


def _full(shape):
    nd = len(shape)
    return pl.BlockSpec(shape, lambda *_: (0,) * nd)


def _cp(*sem):
    return pltpu.CompilerParams(dimension_semantics=sem, vmem_limit_bytes=VMEM_LIMIT)


def _sigmoid(x):
    return 1.0 / (1.0 + jnp.exp(-x))


def _peer(x, y, c, d):
    return ((1 - x) if d & 4 else x, (1 - y) if d & 2 else y, (1 - c) if d & 1 else c)


def _lin(p):
    return 4 * p[0] + 2 * p[1] + p[2]


CHIP_MASKS = (2, 4, 6)
MODES_PER_ARRAY = {"gather": NDEV - 1, "scatter": NDEV - 1, "gather_chips": 1 + len(CHIP_MASKS),
                   "forward": len(CHIP_MASKS)}


def _transfers(mode):
    x, y, c = (lax.axis_index(a) for a in MESH_AXES)
    me = _lin((x, y, c))
    if mode == "forward":
        sibling = (x, y, 1 - c)
        return [(sibling, ("land", _lin(_peer(x, y, c, q))), _lin(_peer(x, y, c, q)), _lin(_peer(x, y, c, q ^ 1)))
                for q in CHIP_MASKS]
    masks = (1,) + CHIP_MASKS if mode == "gather_chips" else range(1, NDEV)
    out = []
    for d in masks:
        peer = _peer(x, y, c, d)
        source = ("block", _lin(peer)) if mode == "scatter" else ("whole", None)
        out.append((peer, source, me, _lin(peer)))
    return out


def _remote_copies(src_refs, land_refs, send_sems, recv_sems, mode):
    transfers = _transfers(mode)
    sends, recvs = [], []
    for i, land_ref in enumerate(land_refs):
        for t, (peer, (kind, slot), there, here) in enumerate(transfers):
            k = i * len(transfers) + t
            src = land_ref.at[slot] if kind == "land" else src_refs[i].at[slot] if kind == "block" else src_refs[i]
            for dst_slot, out in ((there, sends), (here, recvs)):
                out.append(pltpu.make_async_remote_copy(
                    src_ref=src, dst_ref=land_ref.at[dst_slot], send_sem=send_sems.at[k], recv_sem=recv_sems.at[k],
                    device_id=peer, device_id_type=MESH))
    return sends, recvs


def _exchange(srcs, mode, name):
    n = len(srcs)
    gather = mode == "gather"

    def body(*refs):
        src_refs, out_refs, token = refs[:n], refs[n:2 * n], refs[2 * n]
        send_sems, recv_sems, local_sems = refs[2 * n + 1:]
        me = _lin(tuple(lax.axis_index(a) for a in MESH_AXES))
        local = []
        for i in range(n):
            mine = src_refs[i] if gather else src_refs[i].at[me]
            cp = pltpu.make_async_copy(mine, out_refs[i].at[me], local_sems.at[i])
            cp.start()
            local.append(cp)
        sends, recvs = _remote_copies(src_refs, out_refs, send_sems, recv_sems, mode)
        for snd in sends:
            snd.start()
        token[...] = jnp.zeros_like(token)
        for snd, rcv in zip(sends, recvs):
            snd.wait_send()
            rcv.wait_recv()
        for cp in local:
            cp.wait()

    out_shape = [jax.ShapeDtypeStruct(((NDEV,) + s.shape) if gather else s.shape, s.dtype) for s in srcs]
    out_shape.append(jax.ShapeDtypeStruct((SUBLANES, LANES), F32))
    any_spec = pl.BlockSpec(memory_space=pl.ANY)
    return pl.pallas_call(
        body, name=name, out_shape=out_shape,
        in_specs=[any_spec] * n, out_specs=[any_spec] * n + [pl.BlockSpec(memory_space=pltpu.VMEM)],
        scratch_shapes=[pltpu.SemaphoreType.DMA((n * (NDEV - 1),)),
                        pltpu.SemaphoreType.DMA((n * (NDEV - 1),)),
                        pltpu.SemaphoreType.DMA((n,))],
    )(*srcs)


HBM_SPEC = pl.BlockSpec(memory_space=pltpu.HBM)
SEM_SPEC = pl.BlockSpec(memory_space=pltpu.SEMAPHORE)
SIDE_EFFECT = pltpu.SideEffectType.DATAFLOW_SIDE_EFFECTING


def _exchange_start(arrays, mode, name):
    me = _lin(tuple(lax.axis_index(a) for a in MESH_AXES))
    if mode == "forward":
        srcs, lands = [], list(arrays)
    else:
        srcs, lands = list(arrays), []
        for s in srcs:
            own = lax.dynamic_index_in_dim(s, me, 0, keepdims=False) if mode == "scatter" else s
            shape = s.shape if mode == "scatter" else (NDEV,) + s.shape
            lands.append(lax.dynamic_update_index_in_dim(lax.empty(shape, s.dtype), own, me, 0))
    ns, n = len(srcs), len(lands)

    def body(*refs):
        src_refs, land_refs = refs[:ns], refs[ns:ns + n]
        send_sems, recv_sems, token = refs[ns + n], refs[ns + n + 1], refs[-1]
        sends, _ = _remote_copies(src_refs, land_refs, send_sems, recv_sems, mode)
        for snd in sends:
            snd.start()
        token[...] = jnp.zeros_like(token)

    operands = [pltpu.with_memory_space_constraint(a, pltpu.HBM) for a in srcs + lands]
    nsem = n * MODES_PER_ARRAY[mode]
    return pl.pallas_call(
        body, name=name,
        out_shape=(pltpu.SemaphoreType.DMA((nsem,)), pltpu.SemaphoreType.DMA((nsem,)),
                   *[pltpu.HBM(a.shape, a.dtype) for a in operands],
                   jax.ShapeDtypeStruct((SUBLANES, LANES), F32)),
        in_specs=[HBM_SPEC] * (ns + n),
        out_specs=(SEM_SPEC, SEM_SPEC, *([HBM_SPEC] * (ns + n)), pl.BlockSpec(memory_space=pltpu.VMEM)),
        input_output_aliases={i: 2 + i for i in range(ns + n)},
        compiler_params=pltpu.CompilerParams(has_side_effects=SIDE_EFFECT),
    )(*operands)


def _exchange_wait(handle, after, mode, name):
    send_sems, recv_sems, *thru = handle[:-1]
    n = len(thru) if mode == "forward" else len(thru) // 2
    ns = len(thru) - n

    def body(*refs):
        src_refs, land_refs = refs[:ns], refs[ns:ns + n]
        sends, recvs = _remote_copies(src_refs, land_refs, refs[ns + n], refs[ns + n + 1], mode)
        for snd, rcv in zip(sends, recvs):
            snd.wait_send()
            rcv.wait_recv()
        refs[-1][...] = jnp.zeros_like(refs[-1])

    outs = pl.pallas_call(
        body, name=name,
        out_shape=(*[pltpu.HBM(a.shape, a.dtype) for a in thru], jax.ShapeDtypeStruct((SUBLANES, LANES), F32)),
        in_specs=[HBM_SPEC] * (ns + n) + [SEM_SPEC, SEM_SPEC, pl.BlockSpec(memory_space=pl.ANY)],
        out_specs=[HBM_SPEC] * (ns + n) + [pl.BlockSpec(memory_space=pltpu.VMEM)],
        input_output_aliases={i: i for i in range(ns + n)},
        compiler_params=pltpu.CompilerParams(has_side_effects=SIDE_EFFECT),
    )(*thru, send_sems, recv_sems, after)
    return outs[ns:]


def _after(value, token):
    return value + token[0, 0]


ANY_SPEC = pl.BlockSpec(memory_space=pl.ANY)


def _load_cols(wg_ref, widx, w_ref, sems):
    n = wg_ref.shape[-1]
    copies = [pltpu.make_async_copy(wg_ref.at[k, widx], w_ref.at[:, pl.ds(k * n, n)], sems.at[k])
              for k in range(NDEV)]
    for cp in copies:
        cp.start()
    for cp in copies:
        cp.wait()


def _load_rows(wg_ref, r, ridx, w_ref, sems):
    copies = [pltpu.make_async_copy(wg_ref.at[k, pl.ds(ridx * r, r)], w_ref.at[pl.ds(k * r, r)], sems.at[k])
              for k in range(NDEV)]
    for cp in copies:
        cp.start()
    for cp in copies:
        cp.wait()


def _mm_fwd(x, sc, sh, bias, wg, widxs, name):
    s_len, kdim = x.shape
    ncol = NDEV * wg.shape[-1]
    tm = _tile(s_len, MM_ROWS)
    nw = len(widxs)

    def body(x_ref, sc_ref, sh_ref, b_ref, wg_ref, *rest):
        o_refs, w_refs, sems = rest[:nw], rest[nw:2 * nw], rest[2 * nw]

        @pl.when(pl.program_id(0) == 0)
        def _():
            for i, w_ref in enumerate(w_refs):
                _load_cols(wg_ref, widxs[i], w_ref, sems.at[i])

        h = (x_ref[...] * (1.0 + sc_ref[...]) + sh_ref[...]).astype(BF16)
        for w_ref, o_ref in zip(w_refs, o_refs):
            o_ref[...] = (jnp.dot(h, w_ref[...], preferred_element_type=F32) + b_ref[...]).astype(BF16)

    return pl.pallas_call(
        body, name=name, grid=(s_len // tm,),
        in_specs=[pl.BlockSpec((tm, kdim), lambda i: (i, 0)), _full((1, kdim)), _full((1, kdim)),
                  _full((1, ncol)), ANY_SPEC],
        out_specs=[pl.BlockSpec((tm, ncol), lambda i: (i, 0))] * nw,
        out_shape=[jax.ShapeDtypeStruct((s_len, ncol), BF16)] * nw,
        scratch_shapes=[pltpu.VMEM((kdim, ncol), BF16)] * nw + [pltpu.SemaphoreType.DMA((nw, NDEV))],
        compiler_params=_cp("arbitrary"),
    )(x, sc, sh, bias, wg)


def _mm_ln(a, wg, r, ridx, xres, gate, gam, bet, bias, alpha, name):
    s_len = a.shape[0]
    d = wg.shape[-1]
    tm = _tile(s_len, MM_ROWS)

    def body(a_ref, wg_ref, x_ref, g_ref, gam_ref, bet_ref, b_ref, y_ref, xo_ref, xh_ref, rs_ref, w_ref, sems):
        @pl.when(pl.program_id(0) == 0)
        def _():
            _load_rows(wg_ref, r, ridx, w_ref, sems)

        y = jnp.dot(a_ref[...], w_ref[...], preferred_element_type=F32) + b_ref[...]
        z = alpha * x_ref[...] + g_ref[...] * y
        mu = jnp.mean(z, axis=-1, keepdims=True)
        zc = z - mu
        var = jnp.mean(zc * zc, axis=-1, keepdims=True)
        rstd = lax.rsqrt(var + LN_EPS)
        xh = zc * rstd
        y_ref[...] = y.astype(BF16)
        xh_ref[...] = xh
        rs_ref[...] = rstd
        xo_ref[...] = xh * gam_ref[...] + bet_ref[...]

    row = pl.BlockSpec((tm, d), lambda i: (i, 0))
    vec = _full((1, d))
    return pl.pallas_call(
        body, name=name, grid=(s_len // tm,),
        in_specs=[pl.BlockSpec((tm, NDEV * r), lambda i: (i, 0)), ANY_SPEC, row, vec, vec, vec, vec],
        out_specs=[row, row, row, pl.BlockSpec((tm, 1), lambda i: (i, 0))],
        out_shape=[jax.ShapeDtypeStruct((s_len, d), BF16)] + [jax.ShapeDtypeStruct((s_len, d), F32)] * 2
        + [jax.ShapeDtypeStruct((s_len, 1), F32)],
        scratch_shapes=[pltpu.VMEM((NDEV * r, d), BF16), pltpu.SemaphoreType.DMA((NDEV,))],
        compiler_params=_cp("arbitrary"),
    )(a, wg, xres, gate, gam, bet, bias)


def _ln_in_specs(tm, d):
    row = pl.BlockSpec((tm, d), lambda i: (i, 0))
    return [row, pl.BlockSpec((tm, 1), lambda i: (i, 0)), _full((1, d)), row, _full((1, d))]


def _ln_out_specs(s_len, tm, d):
    row = pl.BlockSpec((tm, d), lambda i: (i, 0))
    return ([row, row, _full((SUBLANES, d))],
            [jax.ShapeDtypeStruct((s_len, d), BF16), jax.ShapeDtypeStruct((s_len, d), F32),
             jax.ShapeDtypeStruct((SUBLANES, d), F32)])


def _ln_bwd_rows(dxo, ln_refs, out_refs, alpha):
    xh_ref, rs_ref, gam_ref, y_ref, g_ref = ln_refs
    dy_ref, dres_ref, acc_ref = out_refs
    xh = xh_ref[...]
    dxh = dxo * gam_ref[...]
    m1 = jnp.mean(dxh, axis=-1, keepdims=True)
    m2 = jnp.mean(dxh * xh, axis=-1, keepdims=True)
    dz = rs_ref[...] * (dxh - m1 - xh * m2)
    dy = g_ref[...] * dz
    dy_ref[...] = dy.astype(BF16)
    dres_ref[...] = alpha * dz
    acc_ref[0:1, :] += jnp.sum(dxo * xh, axis=0, keepdims=True)
    acc_ref[1:2, :] += jnp.sum(dxo, axis=0, keepdims=True)
    acc_ref[2:3, :] += jnp.sum(dz * y_ref[...].astype(F32), axis=0, keepdims=True)
    acc_ref[3:4, :] += jnp.sum(dy, axis=0, keepdims=True)


def _mm_nt_row(dy, wg, r, ridx, name):
    s_len, d = dy.shape
    tm = _tile(s_len, MM_ROWS)

    def body(dy_ref, wg_ref, o_ref, w_ref, sems):
        @pl.when(pl.program_id(0) == 0)
        def _():
            _load_rows(wg_ref, r, ridx, w_ref, sems)

        o_ref[...] = lax.dot_general(dy_ref[...], w_ref[...], NT, preferred_element_type=F32).astype(BF16)

    return pl.pallas_call(
        body, name=name, grid=(s_len // tm,),
        in_specs=[pl.BlockSpec((tm, d), lambda i: (i, 0)), ANY_SPEC],
        out_specs=pl.BlockSpec((tm, NDEV * r), lambda i: (i, 0)),
        out_shape=jax.ShapeDtypeStruct((s_len, NDEV * r), BF16),
        scratch_shapes=[pltpu.VMEM((NDEV * r, d), BF16), pltpu.SemaphoreType.DMA((NDEV,))],
        compiler_params=_cp("arbitrary"),
    )(dy, wg)


def _mm_nt_mod(dos, wg, widxs, xin, sc, dres, name, ln=None, alpha=None):
    s_len, kdim = xin.shape
    ncol = NDEV * wg.shape[-1]
    tm = _tile(s_len, MM_ROWS)
    nw = len(widxs)
    nln = 0 if ln is None else len(ln)
    nout = 2 if ln is None else 4

    def body(*refs):
        do_refs, wg_ref = refs[:nw], refs[nw]
        x_ref, sc_ref, dres_ref = refs[nw + 1:nw + 4]
        ln_refs = refs[nw + 4:nw + 4 + nln]
        out_refs = refs[nw + 4 + nln:nw + 4 + nln + nout]
        w_refs, sems = refs[nw + 4 + nln + nout:-1], refs[-1]
        acc_ref = out_refs[-1]

        @pl.when(pl.program_id(0) == 0)
        def _():
            for ref in out_refs[nout // 2:]:
                ref[...] = jnp.zeros_like(ref)
            for i, w_ref in enumerate(w_refs):
                _load_cols(wg_ref, widxs[i], w_ref, sems.at[i])

        dh = None
        for do_ref, w_ref in zip(do_refs, w_refs):
            p = lax.dot_general(do_ref[...], w_ref[...], NT, preferred_element_type=F32)
            dh = p if dh is None else dh + p
        dx = dh * (1.0 + sc_ref[...]) + dres_ref[...]
        if ln is None:
            out_refs[0][...] = dx
        else:
            _ln_bwd_rows(dx, ln_refs, out_refs[0:3], alpha)
        acc_ref[0:1, :] += jnp.sum(dh * x_ref[...], axis=0, keepdims=True)
        acc_ref[1:2, :] += jnp.sum(dh, axis=0, keepdims=True)

    row = pl.BlockSpec((tm, kdim), lambda i: (i, 0))
    if ln is None:
        out_specs, out_shape = [row], [jax.ShapeDtypeStruct((s_len, kdim), F32)]
    else:
        out_specs, out_shape = _ln_out_specs(s_len, tm, kdim)
    return pl.pallas_call(
        body, name=name, grid=(s_len // tm,),
        in_specs=[pl.BlockSpec((tm, ncol), lambda i: (i, 0))] * nw + [ANY_SPEC, row, _full((1, kdim)), row]
        + ([] if ln is None else _ln_in_specs(tm, kdim)),
        out_specs=out_specs + [_full((SUBLANES, kdim))],
        out_shape=out_shape + [jax.ShapeDtypeStruct((SUBLANES, kdim), F32)],
        scratch_shapes=[pltpu.VMEM((kdim, ncol), BF16)] * nw + [pltpu.SemaphoreType.DMA((nw, NDEV))],
        compiler_params=_cp("arbitrary"),
    )(*dos, wg, xin, sc, dres, *([] if ln is None else ln))


def _mm_tn_col(x, sc, sh, do, name):
    s_len, kdim = x.shape
    n = do.shape[1] // NDEV
    ts = _tile(s_len, DW_ROWS)
    nsteps = s_len // ts

    def body(x_ref, sc_ref, sh_ref, do_ref, o_ref, acc_ref):
        @pl.when(pl.program_id(0) == 0)
        def _():
            acc_ref[...] = jnp.zeros_like(acc_ref)

        h = (x_ref[...] * (1.0 + sc_ref[...]) + sh_ref[...]).astype(BF16)
        acc_ref[...] += lax.dot_general(h, do_ref[...], TN, preferred_element_type=F32)

        @pl.when(pl.program_id(0) == nsteps - 1)
        def _():
            for k in range(NDEV):
                o_ref[k] = acc_ref[:, k * n:(k + 1) * n].astype(BF16)

    return pl.pallas_call(
        body, name=name, grid=(nsteps,),
        in_specs=[pl.BlockSpec((ts, kdim), lambda i: (i, 0)), _full((1, kdim)), _full((1, kdim)),
                  pl.BlockSpec((ts, NDEV * n), lambda i: (i, 0))],
        out_specs=_full((NDEV, kdim, n)),
        out_shape=jax.ShapeDtypeStruct((NDEV, kdim, n), BF16),
        scratch_shapes=[pltpu.VMEM((kdim, NDEV * n), F32)],
        compiler_params=_cp("arbitrary"),
    )(x, sc, sh, do)


def _mm_tn_col_t(x, sc, sh, do, rows_out, name):
    s_len, kdim = x.shape
    n = do.shape[1] // NDEV
    ts = _tile(s_len, DW_ROWS)
    nsteps = s_len // ts

    def body(x_ref, sc_ref, sh_ref, do_ref, o_ref, acc_ref):
        @pl.when(pl.program_id(0) == 0)
        def _():
            acc_ref[...] = jnp.zeros_like(acc_ref)

        h = (x_ref[...] * (1.0 + sc_ref[...]) + sh_ref[...]).astype(BF16)
        acc_ref[...] += lax.dot_general(do_ref[...], h, TN, preferred_element_type=F32)

        @pl.when(pl.program_id(0) == nsteps - 1)
        def _():
            for k in range(NDEV):
                o_ref[k] = acc_ref[k * n:k * n + rows_out, :].astype(BF16)

    return pl.pallas_call(
        body, name=name, grid=(nsteps,),
        in_specs=[pl.BlockSpec((ts, kdim), lambda i: (i, 0)), _full((1, kdim)), _full((1, kdim)),
                  pl.BlockSpec((ts, NDEV * n), lambda i: (i, 0))],
        out_specs=_full((NDEV, rows_out, kdim)),
        out_shape=jax.ShapeDtypeStruct((NDEV, rows_out, kdim), BF16),
        scratch_shapes=[pltpu.VMEM((NDEV * n, kdim), F32)],
        compiler_params=_cp("arbitrary"),
    )(x, sc, sh, do)


def _mm_tn_row(a, dy, r, rows_out, name):
    s_len, d = dy.shape
    ts = _tile(s_len, DW_ROWS)
    nsteps = s_len // ts

    def body(a_ref, dy_ref, o_ref, acc_ref):
        @pl.when(pl.program_id(0) == 0)
        def _():
            acc_ref[...] = jnp.zeros_like(acc_ref)

        acc_ref[...] += lax.dot_general(a_ref[...], dy_ref[...], TN, preferred_element_type=F32)

        @pl.when(pl.program_id(0) == nsteps - 1)
        def _():
            for k in range(NDEV):
                o_ref[k] = acc_ref[k * r:k * r + rows_out, :].astype(BF16)

    return pl.pallas_call(
        body, name=name, grid=(nsteps,),
        in_specs=[pl.BlockSpec((ts, NDEV * r), lambda i: (i, 0)), pl.BlockSpec((ts, d), lambda i: (i, 0))],
        out_specs=_full((NDEV, rows_out, d)),
        out_shape=jax.ShapeDtypeStruct((NDEV, rows_out, d), BF16),
        scratch_shapes=[pltpu.VMEM((NDEV * r, d), F32)],
        compiler_params=_cp("arbitrary"),
    )(a, dy)


def _prev_spec(ts, pad, cb, col):
    return pl.BlockSpec((pad, cb), lambda *g: (jnp.maximum(g[-1] * (ts // pad) - 1, 0), col(g)))


def _next_spec(ts, pad, cb, col, s_len):
    return pl.BlockSpec((pad, cb), lambda *g: (jnp.minimum((g[-1] + 1) * (ts // pad), s_len // pad - 1), col(g)))


class _F32Loads:
    def __init__(self, ref):
        self.ref = ref

    def __getitem__(self, idx):
        return self.ref[idx].astype(F32)


def _direct(buf_ref):
    return lambda off, rows: buf_ref[off:off + rows, :]


def _make_shifts(sh_ref, nrows):
    for r in range(1, SUBLANES):
        sh_ref[r, 0:nrows - SUBLANES, :] = sh_ref[0, r:r + nrows - SUBLANES, :]


def _shifted(sh_ref):
    def read(off, rows):
        r = off % SUBLANES
        return sh_ref[r, off - r:off - r + rows, :]
    return read


def _conv_fwd_rows(read, w_ref, b_ref, ktaps, pad, r0, rows):
    acc = None
    for j in range(ktaps):
        term = w_ref[ktaps - 1 - j:ktaps - j, :] * read(pad - j + r0, rows)
        acc = term if acc is None else acc + term
    return acc + b_ref[...]


def _conv_bwd_rows(read, x_rows, w_ref, dwacc_ref, ktaps, r0, rows):
    acc = None
    for j in range(ktaps):
        sl = read(j + r0, rows)
        term = w_ref[ktaps - 1 - j:ktaps - j, :] * sl
        acc = term if acc is None else acc + term
        prod = x_rows * sl
        fold = prod[0:SUBLANES]
        for q in range(1, rows // SUBLANES):
            fold = fold + prod[q * SUBLANES:(q + 1) * SUBLANES]
        tap = ktaps - 1 - j
        dwacc_ref[tap * SUBLANES:(tap + 1) * SUBLANES, :] += fold
    return acc


def _flush_dw(dwacc_ref, dw_ref, ktaps):
    for tap in range(ktaps):
        dw_ref[tap:tap + 1, :] = jnp.sum(dwacc_ref[tap * SUBLANES:(tap + 1) * SUBLANES, :], axis=0, keepdims=True)


def _gateconv_fwd(bcv, cw, cb, name):
    s_len, d3 = bcv.shape
    d = d3 // 3
    ktaps = cw.shape[0]
    pad = SHORT_PAD
    ts = _tile(s_len, CONV_ROWS)

    def body(gb_ref, gc_ref, v_ref, gcp_ref, vp_ref, w_ref, b_ref, o_ref, pbuf):
        gb_ref, gc_ref, v_ref, gcp_ref, vp_ref = map(_F32Loads, (gb_ref, gc_ref, v_ref, gcp_ref, vp_ref))
        s = pl.program_id(0)
        pbuf[0:pad, :] = jnp.where(s > 0, gcp_ref[...] * vp_ref[...], 0.0)
        pbuf[pad:pad + ts, :] = gc_ref[...] * v_ref[...]
        for r0 in range(0, ts, CHUNK):
            q = _conv_fwd_rows(_direct(pbuf), w_ref, b_ref, ktaps, pad, r0, CHUNK)
            o_ref[r0:r0 + CHUNK, :] = (gb_ref[r0:r0 + CHUNK, :] * q).astype(BF16)

    def cur(part):
        return pl.BlockSpec((ts, d), lambda s: (s, part))

    return pl.pallas_call(
        body, name=name, grid=(s_len // ts,),
        in_specs=[cur(0), cur(1), cur(2),
                  _prev_spec(ts, pad, d, lambda g: 1), _prev_spec(ts, pad, d, lambda g: 2),
                  _full((ktaps, d)), _full((1, d))],
        out_specs=pl.BlockSpec((ts, d), lambda s: (s, 0)),
        out_shape=jax.ShapeDtypeStruct((s_len, d), BF16),
        scratch_shapes=[pltpu.VMEM((pad + ts, d), F32)],
        compiler_params=_cp("parallel"),
    )(bcv, bcv, bcv, bcv, bcv, cw, cb)


def _a_tail_fwd(bcv, cw, cb, wg, xres, gate, gam, bet, alpha, name):
    s_len, d3 = bcv.shape
    d = d3 // 3
    r = d // NDEV
    ktaps = cw.shape[0]
    pad = SHORT_PAD
    tm = _tile(s_len, CONV_ROWS)
    cbk = _tile(d, d // 4, LANES)

    def body(gb_ref, gc_ref, v_ref, gcp_ref, vp_ref, cw_ref, cb_ref, wg_ref, x_ref, g_ref, gam_ref, bet_ref,
             y0_ref, y_ref, xo_ref, xh_ref, rs_ref, pbuf, w_ref, sems):
        gb_ref, gc_ref, v_ref, gcp_ref, vp_ref = map(_F32Loads, (gb_ref, gc_ref, v_ref, gcp_ref, vp_ref))
        s = pl.program_id(0)

        @pl.when(s == 0)
        def _():
            _load_rows(wg_ref, r, 0, w_ref, sems)

        pbuf[0:pad, :] = jnp.where(s > 0, gcp_ref[...] * vp_ref[...], 0.0)
        pbuf[pad:pad + tm, :] = gc_ref[...] * v_ref[...]
        y = None
        for c0 in range(0, d, cbk):
            cols = slice(c0, c0 + cbk)
            read = _direct(_Cols(pbuf, cols))
            for r0 in range(0, tm, CHUNK):
                rows = slice(r0, r0 + CHUNK)
                q = _conv_fwd_rows(read, _Cols(cw_ref, cols), _Cols(cb_ref, cols), ktaps, pad, r0, CHUNK)
                y0_ref[rows, cols] = (gb_ref[rows, cols] * q).astype(BF16)
            p = jnp.dot(y0_ref[:, cols], w_ref[cols, :], preferred_element_type=F32)
            y = p if y is None else y + p
        z = alpha * x_ref[...] + g_ref[...] * y
        mu = jnp.mean(z, axis=-1, keepdims=True)
        zc = z - mu
        var = jnp.mean(zc * zc, axis=-1, keepdims=True)
        rstd = lax.rsqrt(var + LN_EPS)
        xh = zc * rstd
        y_ref[...] = y.astype(BF16)
        xh_ref[...] = xh
        rs_ref[...] = rstd
        xo_ref[...] = xh * gam_ref[...] + bet_ref[...]

    def cur(part):
        return pl.BlockSpec((tm, d), lambda s: (s, part))

    row = pl.BlockSpec((tm, d), lambda i: (i, 0))
    vec = _full((1, d))
    return pl.pallas_call(
        body, name=name, grid=(s_len // tm,),
        in_specs=[cur(0), cur(1), cur(2),
                  _prev_spec(tm, pad, d, lambda g: 1), _prev_spec(tm, pad, d, lambda g: 2),
                  _full((ktaps, d)), vec, ANY_SPEC, row, vec, vec, vec],
        out_specs=[row, row, row, row, pl.BlockSpec((tm, 1), lambda i: (i, 0))],
        out_shape=[jax.ShapeDtypeStruct((s_len, d), BF16), jax.ShapeDtypeStruct((s_len, d), BF16),
                   jax.ShapeDtypeStruct((s_len, d), F32), jax.ShapeDtypeStruct((s_len, d), F32),
                   jax.ShapeDtypeStruct((s_len, 1), F32)],
        scratch_shapes=[pltpu.VMEM((pad + tm, d), F32), pltpu.VMEM((d, d), BF16), pltpu.SemaphoreType.DMA((NDEV,))],
        compiler_params=_cp("arbitrary"),
    )(bcv, bcv, bcv, bcv, bcv, cw, cb, wg, xres, gate, gam, bet)


def _gateconv_bwd(bcv, dy0, cw, cb, name):
    s_len, d3 = bcv.shape
    d = d3 // 3
    ktaps = cw.shape[0]
    pad = SHORT_PAD
    ts = _tile(s_len, CONV_ROWS)
    nsteps = s_len // ts

    def body(gb_ref, gc_ref, v_ref, gcp_ref, vp_ref, gbn_ref, dy_ref, dyn_ref, w_ref, b_ref,
             o_ref, dw_ref, db_ref, pbuf, dqbuf, dwacc):
        gb_ref, gc_ref, v_ref, gcp_ref, vp_ref, gbn_ref, dy_ref, dyn_ref = map(
            _F32Loads, (gb_ref, gc_ref, v_ref, gcp_ref, vp_ref, gbn_ref, dy_ref, dyn_ref))
        s = pl.program_id(0)

        @pl.when(s == 0)
        def _():
            dwacc[...] = jnp.zeros_like(dwacc)
            db_ref[...] = jnp.zeros_like(db_ref)

        pbuf[0:pad, :] = jnp.where(s > 0, gcp_ref[...] * vp_ref[...], 0.0)
        pbuf[pad:pad + ts, :] = gc_ref[...] * v_ref[...]
        dq = dy_ref[...] * gb_ref[...]
        dqbuf[0:ts, :] = dq
        dqbuf[ts:ts + pad, :] = jnp.where(s < nsteps - 1, dyn_ref[...] * gbn_ref[...], 0.0)
        db_ref[...] += jnp.sum(dq, axis=0, keepdims=True)
        for r0 in range(0, ts, CHUNK):
            rows = slice(r0, r0 + CHUNK)
            q = _conv_fwd_rows(_direct(pbuf), w_ref, b_ref, ktaps, pad, r0, CHUNK)
            o_ref[rows, 0:d] = (dy_ref[rows, :] * q).astype(BF16)
            dp = _conv_bwd_rows(_direct(dqbuf), pbuf[pad + r0:pad + r0 + CHUNK, :], w_ref, dwacc, ktaps, r0, CHUNK)
            o_ref[rows, d:2 * d] = (dp * v_ref[rows, :]).astype(BF16)
            o_ref[rows, 2 * d:3 * d] = (dp * gc_ref[rows, :]).astype(BF16)

        @pl.when(s == nsteps - 1)
        def _():
            _flush_dw(dwacc, dw_ref, ktaps)

    def cur(part):
        return pl.BlockSpec((ts, d), lambda s: (s, part))

    return pl.pallas_call(
        body, name=name, grid=(nsteps,),
        in_specs=[cur(0), cur(1), cur(2),
                  _prev_spec(ts, pad, d, lambda g: 1), _prev_spec(ts, pad, d, lambda g: 2),
                  _next_spec(ts, pad, d, lambda g: 0, s_len),
                  cur(0), _next_spec(ts, pad, d, lambda g: 0, s_len),
                  _full((ktaps, d)), _full((1, d))],
        out_specs=[pl.BlockSpec((ts, d3), lambda s: (s, 0)), _full((ktaps, d)), _full((1, d))],
        out_shape=[jax.ShapeDtypeStruct((s_len, d3), BF16), jax.ShapeDtypeStruct((ktaps, d), F32),
                   jax.ShapeDtypeStruct((1, d), F32)],
        scratch_shapes=[pltpu.VMEM((pad + ts, d), F32), pltpu.VMEM((ts + pad, d), F32),
                        pltpu.VMEM((ktaps * SUBLANES, d), F32)],
        compiler_params=_cp("arbitrary"),
    )(bcv, bcv, bcv, bcv, bcv, bcv, dy0, dy0, cw, cb)


class _Cols:
    def __init__(self, ref, cols):
        self.ref, self.cols = ref, cols

    def __getitem__(self, idx):
        return self.ref[slice(None) if idx is Ellipsis else idx[0], self.cols]

    def __setitem__(self, idx, value):
        self.ref[idx[0], self.cols] = value


def _ffn_tail_fwd(u0, vg, cw, cb, wg, xres, gate, gam, bet, alpha, name):
    s_len, f = u0.shape
    d = wg.shape[-1]
    r = f // NDEV
    ktaps = cw.shape[0]
    pad = SHORT_PAD
    tm = _tile(s_len, CONV_ROWS)
    cbk = COL_BLOCK if f % COL_BLOCK == 0 else f

    def body(u_ref, up_ref, vg_ref, cw_ref, cb_ref, wg_ref, x_ref, g_ref, gam_ref, bet_ref,
             t_ref, uc_ref, y_ref, xo_ref, xh_ref, rs_ref, ubuf, w_ref, sems):
        u_ref, up_ref, vg_ref = map(_F32Loads, (u_ref, up_ref, vg_ref))
        s = pl.program_id(0)

        @pl.when(s == 0)
        def _():
            _load_rows(wg_ref, r, 0, w_ref, sems)

        ubuf[0:pad, :] = jnp.where(s > 0, up_ref[...], 0.0)
        ubuf[pad:pad + tm, :] = u_ref[...]
        y = None
        for c0 in range(0, f, cbk):
            cols = slice(c0, c0 + cbk)
            read = _direct(_Cols(ubuf, cols))
            for r0 in range(0, tm, CHUNK):
                rows = slice(r0, r0 + CHUNK)
                u = _conv_fwd_rows(read, _Cols(cw_ref, cols), _Cols(cb_ref, cols), ktaps, pad, r0, CHUNK)
                t_ref[rows, cols] = (u * _sigmoid(u) * vg_ref[rows, cols]).astype(BF16)
                uc_ref[rows, cols] = u.astype(BF16)
            p = jnp.dot(t_ref[:, cols], w_ref[cols, :], preferred_element_type=F32)
            y = p if y is None else y + p
        z = alpha * x_ref[...] + g_ref[...] * y
        mu = jnp.mean(z, axis=-1, keepdims=True)
        zc = z - mu
        var = jnp.mean(zc * zc, axis=-1, keepdims=True)
        rstd = lax.rsqrt(var + LN_EPS)
        xh = zc * rstd
        y_ref[...] = y.astype(BF16)
        xh_ref[...] = xh
        rs_ref[...] = rstd
        xo_ref[...] = xh * gam_ref[...] + bet_ref[...]

    wide = pl.BlockSpec((tm, f), lambda i: (i, 0))
    row = pl.BlockSpec((tm, d), lambda i: (i, 0))
    vec = _full((1, d))
    return pl.pallas_call(
        body, name=name, grid=(s_len // tm,),
        in_specs=[wide, _prev_spec(tm, pad, f, lambda g: 0), wide, _full((ktaps, f)), _full((1, f)), ANY_SPEC,
                  row, vec, vec, vec],
        out_specs=[wide, wide, row, row, row, pl.BlockSpec((tm, 1), lambda i: (i, 0))],
        out_shape=[jax.ShapeDtypeStruct((s_len, f), BF16), jax.ShapeDtypeStruct((s_len, f), BF16),
                   jax.ShapeDtypeStruct((s_len, d), BF16),
                   jax.ShapeDtypeStruct((s_len, d), F32), jax.ShapeDtypeStruct((s_len, d), F32),
                   jax.ShapeDtypeStruct((s_len, 1), F32)],
        scratch_shapes=[pltpu.VMEM((pad + tm, f), F32), pltpu.VMEM((f, d), BF16), pltpu.SemaphoreType.DMA((NDEV,))],
        compiler_params=_cp("arbitrary"),
    )(u0, u0, vg, cw, cb, wg, xres, gate, gam, bet)


def _ffn_core_bwd(dy, u0, uc, vg, cw, wg_row, wg_col, xin, sc, dres, ln, alpha, name):
    s_len, f = u0.shape
    d = xin.shape[1]
    r = f // NDEV
    ktaps = cw.shape[0]
    pad = SHORT_PAD
    tm = _tile(s_len, CONV_ROWS)
    nsteps = s_len // tm
    cbk = COL_BLOCK if f % COL_BLOCK == 0 else f

    def body(dy_ref, dyn_ref, u_ref, uc_ref, ucn_ref, vg_ref, vgn_ref, cw_ref, wgr_ref, wgc_ref,
             x_ref, sc_ref, dres_ref, xh_ref, rs_ref, gam_ref, y_ref, g_ref,
             du0_ref, dvg_ref, dw_ref, db_ref, dyo_ref, dreso_ref, lnacc_ref, acc_ref,
             dtbuf, dubuf, dwacc, wd_ref, wup_ref, wgate_ref, sems):
        u_ref, uc_ref, ucn_ref, vg_ref, vgn_ref = map(_F32Loads, (u_ref, uc_ref, ucn_ref, vg_ref, vgn_ref))
        s = pl.program_id(0)
        last = s == nsteps - 1

        @pl.when(s == 0)
        def _():
            dwacc[...] = jnp.zeros_like(dwacc)
            db_ref[...] = jnp.zeros_like(db_ref)
            acc_ref[...] = jnp.zeros_like(acc_ref)
            lnacc_ref[...] = jnp.zeros_like(lnacc_ref)
            _load_rows(wgr_ref, r, 0, wd_ref, sems.at[0])
            _load_cols(wgc_ref, 0, wup_ref, sems.at[1])
            _load_cols(wgc_ref, 1, wgate_ref, sems.at[2])

        dy_cur, dy_nxt = dy_ref[...], dyn_ref[...]
        dh = None
        for c0 in range(0, f, cbk):
            cols = slice(c0, c0 + cbk)
            wd_blk = wd_ref[cols, :]
            dtbuf[0:tm, :] = lax.dot_general(dy_cur, wd_blk, NT, preferred_element_type=F32)
            dtbuf[tm:tm + pad, :] = jnp.where(
                last, 0.0, lax.dot_general(dy_nxt, wd_blk, NT, preferred_element_type=F32))
            for r0 in range(0, tm + pad, CHUNK):
                dtr = dtbuf[r0:r0 + CHUNK, :]
                if r0 < tm:
                    u, vgr = uc_ref[r0:r0 + CHUNK, cols], vg_ref[r0:r0 + CHUNK, cols]
                    sg = _sigmoid(u)
                    dvg_ref[r0:r0 + CHUNK, cols] = (dtr * u * sg).astype(BF16)
                else:
                    u, vgr = ucn_ref[r0 - tm:r0 - tm + CHUNK, cols], vgn_ref[r0 - tm:r0 - tm + CHUNK, cols]
                    sg = _sigmoid(u)
                dubuf[r0:r0 + CHUNK, :] = dtr * vgr * (sg * (1.0 + u * (1.0 - sg)))
            db_ref[:, cols] += jnp.sum(dubuf[0:tm, :], axis=0, keepdims=True)
            for r0 in range(0, tm, CHUNK):
                du0 = _conv_bwd_rows(_direct(dubuf), u_ref[r0:r0 + CHUNK, cols], _Cols(cw_ref, cols),
                                     _Cols(dwacc, cols), ktaps, r0, CHUNK)
                du0_ref[r0:r0 + CHUNK, cols] = du0.astype(BF16)
            p = (lax.dot_general(du0_ref[:, cols], wup_ref[:, cols], NT, preferred_element_type=F32)
                 + lax.dot_general(dvg_ref[:, cols], wgate_ref[:, cols], NT, preferred_element_type=F32))
            dh = p if dh is None else dh + p
        dx = dh * (1.0 + sc_ref[...]) + dres_ref[...]
        _ln_bwd_rows(dx, (xh_ref, rs_ref, gam_ref, y_ref, g_ref), (dyo_ref, dreso_ref, lnacc_ref), alpha)
        acc_ref[0:1, :] += jnp.sum(dh * x_ref[...], axis=0, keepdims=True)
        acc_ref[1:2, :] += jnp.sum(dh, axis=0, keepdims=True)

        @pl.when(last)
        def _():
            _flush_dw(dwacc, dw_ref, ktaps)

    wide = pl.BlockSpec((tm, f), lambda i: (i, 0))
    row = pl.BlockSpec((tm, d), lambda i: (i, 0))
    ln_out_specs, ln_out_shape = _ln_out_specs(s_len, tm, d)
    return pl.pallas_call(
        body, name=name, grid=(nsteps,),
        in_specs=[row, _next_spec(tm, pad, d, lambda g: 0, s_len),
                  wide, wide, _next_spec(tm, pad, f, lambda g: 0, s_len),
                  wide, _next_spec(tm, pad, f, lambda g: 0, s_len),
                  _full((ktaps, f)), ANY_SPEC, ANY_SPEC, row, _full((1, d)), row]
        + _ln_in_specs(tm, d),
        out_specs=[wide, wide, _full((ktaps, f)), _full((1, f))] + ln_out_specs + [_full((SUBLANES, d))],
        out_shape=[jax.ShapeDtypeStruct((s_len, f), BF16), jax.ShapeDtypeStruct((s_len, f), BF16),
                   jax.ShapeDtypeStruct((ktaps, f), F32), jax.ShapeDtypeStruct((1, f), F32)]
        + ln_out_shape + [jax.ShapeDtypeStruct((SUBLANES, d), F32)],
        scratch_shapes=[pltpu.VMEM((tm + pad, cbk), F32),
                        pltpu.VMEM((tm + pad, cbk), F32), pltpu.VMEM((ktaps * SUBLANES, f), F32),
                        pltpu.VMEM((f, d), BF16), pltpu.VMEM((d, f), BF16), pltpu.VMEM((d, f), BF16),
                        pltpu.SemaphoreType.DMA((3, NDEV))],
        compiler_params=_cp("arbitrary"),
    )(dy, dy, u0, uc, uc, vg, vg, cw, wg_row, wg_col, xin, sc, dres, *ln)


def _b_mid_fwd(ub, cw, cb, lng, lnb, name):
    s_len, d2 = ub.shape
    d = d2 // 2
    ktaps = cw.shape[0]
    pad = LONG_PAD
    ts = _tile(s_len, CONV_ROWS)

    def body(a_ref, g_ref, ap_ref, gp_ref, w_ref, b_ref, lng_ref, lnb_ref, a2_ref, a4_ref, abuf):
        a_ref, g_ref, ap_ref, gp_ref = map(_F32Loads, (a_ref, g_ref, ap_ref, gp_ref))
        s = pl.program_id(0)
        abuf[0, 0:pad, :] = jnp.where(s > 0, ap_ref[...] * _sigmoid(gp_ref[...]), 0.0)
        abuf[0, pad:pad + ts, :] = a_ref[...] * _sigmoid(g_ref[...])
        _make_shifts(abuf, pad + ts)
        for r0 in range(0, ts, CHUNK):
            a2_ref[r0:r0 + CHUNK, :] = _conv_fwd_rows(_shifted(abuf), w_ref, b_ref, ktaps, pad, r0, CHUNK)
        a2 = a2_ref[...]
        mu = jnp.mean(a2, axis=-1, keepdims=True)
        ac = a2 - mu
        var = jnp.mean(ac * ac, axis=-1, keepdims=True)
        a3 = ac * lax.rsqrt(var + LN_EPS) * lng_ref[...] + lnb_ref[...]
        a4_ref[...] = (a3 * _sigmoid(a3)).astype(BF16)

    def cur(part):
        return pl.BlockSpec((ts, d), lambda s: (s, part))

    vec = _full((1, d))
    return pl.pallas_call(
        body, name=name, grid=(s_len // ts,),
        in_specs=[cur(0), cur(1), _prev_spec(ts, pad, d, lambda g: 0), _prev_spec(ts, pad, d, lambda g: 1),
                  _full((ktaps, d)), vec, vec, vec],
        out_specs=[cur(0), cur(0)],
        out_shape=[jax.ShapeDtypeStruct((s_len, d), F32), jax.ShapeDtypeStruct((s_len, d), BF16)],
        scratch_shapes=[pltpu.VMEM((SUBLANES, pad + ts, d), F32)],
        compiler_params=_cp("parallel"),
    )(ub, ub, ub, ub, cw, cb, lng, lnb)


def _b_mid_bwd(ub, a2, da4, cw, lng, lnb, name):
    s_len, d2 = ub.shape
    d = d2 // 2
    ktaps = cw.shape[0]
    pad = LONG_PAD
    ts = _tile(s_len, CONV_ROWS)
    nsteps = s_len // ts

    def body(a_ref, g_ref, a2_ref, a2n_ref, da4_ref, da4n_ref, w_ref, lng_ref, lnb_ref,
             du_ref, dw_ref, db_ref, dlng_ref, dlnb_ref, dbias_ref, dabuf, dwacc):
        a_ref, g_ref, da4_ref, da4n_ref = map(_F32Loads, (a_ref, g_ref, da4_ref, da4n_ref))
        s = pl.program_id(0)
        last = s == nsteps - 1

        @pl.when(s == 0)
        def _():
            dwacc[...] = jnp.zeros_like(dwacc)
            for ref in (db_ref, dlng_ref, dlnb_ref, dbias_ref):
                ref[...] = jnp.zeros_like(ref)

        def ln_silu_bwd(a2_t, da4_t):
            mu = jnp.mean(a2_t, axis=-1, keepdims=True)
            ac = a2_t - mu
            var = jnp.mean(ac * ac, axis=-1, keepdims=True)
            rstd = lax.rsqrt(var + LN_EPS)
            ah = ac * rstd
            a3 = ah * lng_ref[...] + lnb_ref[...]
            sg = _sigmoid(a3)
            da3 = da4_t * (sg * (1.0 + a3 * (1.0 - sg)))
            dah = da3 * lng_ref[...]
            m1 = jnp.mean(dah, axis=-1, keepdims=True)
            m2 = jnp.mean(dah * ah, axis=-1, keepdims=True)
            return rstd * (dah - m1 - ah * m2), da3, ah

        da2, da3, ah = ln_silu_bwd(a2_ref[...], da4_ref[...])
        dabuf[0, 0:ts, :] = da2
        dlng_ref[...] += jnp.sum(da3 * ah, axis=0, keepdims=True)
        dlnb_ref[...] += jnp.sum(da3, axis=0, keepdims=True)
        db_ref[...] += jnp.sum(da2, axis=0, keepdims=True)
        da2n, _, _ = ln_silu_bwd(a2n_ref[...], jnp.where(last, 0.0, da4n_ref[...]))
        dabuf[0, ts:ts + pad, :] = da2n
        _make_shifts(dabuf, ts + pad)
        for r0 in range(0, ts, CHUNK):
            rows = slice(r0, r0 + CHUNK)
            a_r, g_r = a_ref[rows, :], g_ref[rows, :]
            sg = _sigmoid(g_r)
            da1 = _conv_bwd_rows(_shifted(dabuf), a_r * sg, w_ref, dwacc, ktaps, r0, CHUNK)
            da = da1 * sg
            dg = da1 * a_r * sg * (1.0 - sg)
            du_ref[rows, 0:d] = da.astype(BF16)
            du_ref[rows, d:2 * d] = dg.astype(BF16)
            dbias_ref[:, 0:d] += jnp.sum(da, axis=0, keepdims=True)
            dbias_ref[:, d:2 * d] += jnp.sum(dg, axis=0, keepdims=True)

        @pl.when(last)
        def _():
            _flush_dw(dwacc, dw_ref, ktaps)

    def cur(part):
        return pl.BlockSpec((ts, d), lambda s: (s, part))

    vec = _full((1, d))
    nxt = _next_spec(ts, pad, d, lambda g: 0, s_len)
    return pl.pallas_call(
        body, name=name, grid=(nsteps,),
        in_specs=[cur(0), cur(1), cur(0), nxt, cur(0), nxt, _full((ktaps, d)), vec, vec],
        out_specs=[pl.BlockSpec((ts, d2), lambda s: (s, 0)), _full((ktaps, d)), vec, vec, vec, _full((1, d2))],
        out_shape=[jax.ShapeDtypeStruct((s_len, d2), BF16), jax.ShapeDtypeStruct((ktaps, d), F32),
                   jax.ShapeDtypeStruct((1, d), F32), jax.ShapeDtypeStruct((1, d), F32),
                   jax.ShapeDtypeStruct((1, d), F32), jax.ShapeDtypeStruct((1, d2), F32)],
        scratch_shapes=[pltpu.VMEM((SUBLANES, ts + pad, d), F32), pltpu.VMEM((ktaps * SUBLANES, d), F32)],
        compiler_params=_cp("arbitrary"),
    )(ub, ub, a2, a2, da4, da4, cw, lng, lnb)


def _loss_head(xo, tgt, ln, alpha, name):
    s_len, d = xo.shape
    tm = _tile(s_len, MM_ROWS)

    def body(x_ref, t_ref, xh_ref, rs_ref, gam_ref, y_ref, g_ref, dy_ref, dres_ref, acc_ref, l_ref):
        @pl.when(pl.program_id(0) == 0)
        def _():
            l_ref[...] = jnp.zeros_like(l_ref)
            acc_ref[...] = jnp.zeros_like(acc_ref)

        e = x_ref[...] - t_ref[...]
        per_row = jnp.sum(e * e, axis=-1, keepdims=True) * (1.0 / d)
        l_ref[...] += 0.5 * jnp.sum(per_row, axis=0, keepdims=True)
        _ln_bwd_rows(e * (1.0 / d), (xh_ref, rs_ref, gam_ref, y_ref, g_ref), (dy_ref, dres_ref, acc_ref), alpha)

    row = pl.BlockSpec((tm, d), lambda i: (i, 0))
    ln_out_specs, ln_out_shape = _ln_out_specs(s_len, tm, d)
    return pl.pallas_call(
        body, name=name, grid=(s_len // tm,),
        in_specs=[row, row] + _ln_in_specs(tm, d), out_specs=ln_out_specs + [_full((1, LANES))],
        out_shape=ln_out_shape + [jax.ShapeDtypeStruct((1, LANES), F32)],
        compiler_params=_cp("arbitrary"),
    )(xo, tgt, *ln)


def _ada_fwd(c_all, ada_w, ada_b_loc, name):
    depth, d, n = ada_w.shape

    def body(c_ref, w_ref, b_ref, o_ref):
        c = c_ref[...]
        act = c * _sigmoid(c)
        o_ref[...] = jnp.dot(act, w_ref[...], preferred_element_type=F32,
                             precision=lax.Precision.HIGHEST) + b_ref[...]

    return pl.pallas_call(
        body, name=name, grid=(depth,),
        in_specs=[_full((NDEV, d)), pl.BlockSpec((None, d, n), lambda i: (i, 0, 0)),
                  pl.BlockSpec((None, 1, n), lambda i: (i, 0, 0))],
        out_specs=pl.BlockSpec((None, NDEV, n), lambda i: (i, 0, 0)),
        out_shape=jax.ShapeDtypeStruct((depth, NDEV, n), F32),
        compiler_params=_cp("parallel"),
    )(c_all, ada_w, ada_b_loc.reshape(depth, 1, n))


def _ada_bwd(c_all_t, dmod_cols, name):
    depth, _, n = dmod_cols.shape
    d = c_all_t.shape[0]

    def body(ct_ref, dm_ref, o_ref):
        ct = ct_ref[...]
        act = ct * _sigmoid(ct)
        acc = None
        for b in range(NDEV):
            term = act[:, b:b + 1] * dm_ref[b:b + 1, :]
            acc = term if acc is None else acc + term
        o_ref[...] = acc

    return pl.pallas_call(
        body, name=name, grid=(depth,),
        in_specs=[_full((d, NDEV)), pl.BlockSpec((None, NDEV, n), lambda i: (i, 0, 0))],
        out_specs=pl.BlockSpec((None, d, n), lambda i: (i, 0, 0)),
        out_shape=jax.ShapeDtypeStruct((depth, d, n), F32),
        compiler_params=_cp("parallel"),
    )(c_all_t, dmod_cols)


def _sum_parts(parts, name):
    _, rows, lanes = parts.shape

    def body(p_ref, o_ref):
        acc = p_ref[0]
        for k in range(1, NDEV):
            acc = acc + p_ref[k]
        o_ref[...] = acc

    return pl.pallas_call(
        body, name=name, in_specs=[_full(parts.shape)], out_specs=_full((rows, lanes)), grid=(1,),
        out_shape=jax.ShapeDtypeStruct((rows, lanes), F32), compiler_params=_cp("arbitrary"),
    )(parts)


def _adamw(w, glist, m, v, name):
    nl, rows, cols = w.shape
    tr = _tile(rows, 256, 2 * SUBLANES)

    def body(w_ref, *rest):
        g_refs = rest[:nl]
        m_ref, v_ref, go_ref, d_ref, mo_ref, vo_ref = rest[nl:]
        g = None
        for layer, g_ref in enumerate(g_refs):
            part = g_ref[0].astype(F32)
            for p in range(1, g_ref.shape[0]):
                part = part + g_ref[p].astype(F32)
            g = part if g is None else jnp.where(pl.program_id(0) == layer, part, g)
        go_ref[...] = g
        d_ref[...], mo_ref[...], vo_ref[...] = _adam_step(w_ref[...], g, m_ref[...], v_ref[...])

    blk = pl.BlockSpec((None, tr, cols), lambda l, i: (l, i, 0))
    g_specs = [pl.BlockSpec((g.shape[0], tr, cols), lambda l, i: (0, i, 0)) for g in glist]
    return pl.pallas_call(
        body, name=name, grid=(nl, rows // tr),
        in_specs=[blk] + g_specs + [blk, blk],
        out_specs=[blk] * 4, out_shape=[jax.ShapeDtypeStruct((nl, rows, cols), F32)] * 4,
        compiler_params=_cp("parallel", "parallel"),
    )(w, *glist, m, v)


def _adam_step(w, g, m, v):
    m1 = ADAM_B1 * m + (1.0 - ADAM_B1) * g
    v1 = ADAM_B2 * v + (1.0 - ADAM_B2) * (g * g)
    m_hat = m1 / (1.0 - ADAM_B1 ** ADAM_STEP)
    v_hat = v1 / (1.0 - ADAM_B2 ** ADAM_STEP)
    return -ADAM_LR * (m_hat / (jnp.sqrt(v_hat) + ADAM_EPS) + ADAM_WD * w), m1, v1


def _adamw_small(ws, gs, ms, vs, name):
    n = len(ws)

    def body(*refs):
        ins, outs = refs[:4 * n], refs[4 * n:]
        for i in range(n):
            w_ref, g_ref, m_ref, v_ref = ins[i], ins[n + i], ins[2 * n + i], ins[3 * n + i]
            delta, m1, v1 = _adam_step(w_ref[...], g_ref[...], m_ref[...], v_ref[...])
            outs[3 * i][...] = delta
            outs[3 * i + 1][...] = m1
            outs[3 * i + 2][...] = v1

    operands = list(ws) + list(gs) + list(ms) + list(vs)
    out_shape = [jax.ShapeDtypeStruct(w.shape, F32) for w in ws for _ in range(3)]
    return pl.pallas_call(
        body, name=name, grid=(1,), in_specs=[_full(a.shape) for a in operands],
        out_specs=[_full(s.shape) for s in out_shape], out_shape=out_shape,
        compiler_params=_cp("arbitrary"),
    )(*operands)


def _pack(pieces):
    flat = jnp.concatenate([p.reshape(-1) for p in pieces])
    unit = SUBLANES * LANES
    padded = -(-flat.shape[0] // unit) * unit
    return jnp.pad(flat, (0, padded - flat.shape[0])).reshape(padded // LANES, LANES)


def _unpack(packed, shapes, lead=()):
    flat = packed.reshape(lead + (-1,))
    out, off = [], 0
    for s in shapes:
        size = 1
        for dim in s:
            size *= dim
        out.append(flat[..., off:off + size].reshape(lead + tuple(s)))
        off += size
    return out


def _pad_last(a, n):
    return jnp.pad(a, [(0, 0)] * (a.ndim - 1) + [(0, n - a.shape[-1])])


def kernel(x, c, ada_w, ada_b, ln_tok_g, ln_tok_b, ln_ch_g, ln_ch_b, a_w_in, a_conv_w, a_conv_b, a_w_out, b_w_pw1, b_b_pw1, b_conv_w, b_conv_b, b_ln_g, b_ln_b, b_w_pw2, b_b_pw2, f_w_up, f_conv_w, f_conv_b, f_w_gate, f_w_down, loss_target, m_ada_w, m_ada_b, m_ln_tok_g, m_ln_tok_b, m_ln_ch_g, m_ln_ch_b, m_a_w_in, m_a_conv_w, m_a_conv_b, m_a_w_out, m_b_w_pw1, m_b_b_pw1, m_b_conv_w, m_b_conv_b, m_b_ln_g, m_b_ln_b, m_b_w_pw2, m_b_b_pw2, m_f_w_up, m_f_conv_w, m_f_conv_b, m_f_w_gate, m_f_w_down, v_ada_w, v_ada_b, v_ln_tok_g, v_ln_tok_b, v_ln_ch_g, v_ln_ch_b, v_a_w_in, v_a_conv_w, v_a_conv_b, v_a_w_out, v_b_w_pw1, v_b_b_pw1, v_b_conv_w, v_b_conv_b, v_b_ln_g, v_b_ln_b, v_b_w_pw2, v_b_b_pw2, v_f_w_up, v_f_conv_w, v_f_conv_b, v_f_w_gate, v_f_w_down):
    weights = dict(ada_w=ada_w, ada_b=ada_b, ln_tok_g=ln_tok_g, ln_tok_b=ln_tok_b, ln_ch_g=ln_ch_g, ln_ch_b=ln_ch_b, a_w_in=a_w_in, a_conv_w=a_conv_w, a_conv_b=a_conv_b, a_w_out=a_w_out, b_w_pw1=b_w_pw1, b_b_pw1=b_b_pw1, b_conv_w=b_conv_w, b_conv_b=b_conv_b, b_ln_g=b_ln_g, b_ln_b=b_ln_b, b_w_pw2=b_w_pw2, b_b_pw2=b_b_pw2, f_w_up=f_w_up, f_conv_w=f_conv_w, f_conv_b=f_conv_b, f_w_gate=f_w_gate, f_w_down=f_w_down)
    mom_m = dict(ada_w=m_ada_w, ada_b=m_ada_b, ln_tok_g=m_ln_tok_g, ln_tok_b=m_ln_tok_b, ln_ch_g=m_ln_ch_g, ln_ch_b=m_ln_ch_b, a_w_in=m_a_w_in, a_conv_w=m_a_conv_w, a_conv_b=m_a_conv_b, a_w_out=m_a_w_out, b_w_pw1=m_b_w_pw1, b_b_pw1=m_b_b_pw1, b_conv_w=m_b_conv_w, b_conv_b=m_b_conv_b, b_ln_g=m_b_ln_g, b_ln_b=m_b_ln_b, b_w_pw2=m_b_w_pw2, b_b_pw2=m_b_b_pw2, f_w_up=m_f_w_up, f_conv_w=m_f_conv_w, f_conv_b=m_f_conv_b, f_w_gate=m_f_w_gate, f_w_down=m_f_w_down)
    mom_v = dict(ada_w=v_ada_w, ada_b=v_ada_b, ln_tok_g=v_ln_tok_g, ln_tok_b=v_ln_tok_b, ln_ch_g=v_ln_ch_g, ln_ch_b=v_ln_ch_b, a_w_in=v_a_w_in, a_conv_w=v_a_conv_w, a_conv_b=v_a_conv_b, a_w_out=v_a_w_out, b_w_pw1=v_b_w_pw1, b_b_pw1=v_b_b_pw1, b_conv_w=v_b_conv_w, b_conv_b=v_b_conv_b, b_ln_g=v_b_ln_g, b_ln_b=v_b_ln_b, b_w_pw2=v_b_w_pw2, b_b_pw2=v_b_b_pw2, f_w_up=v_f_w_up, f_conv_w=v_f_conv_w, f_conv_b=v_f_conv_b, f_w_gate=v_f_w_gate, f_w_down=v_f_w_down)
    names = list(weights)

    depth, d, n_ada = ada_w.shape
    assert depth == 2 and a_w_in.shape[0] == 1 and b_w_pw1.shape[0] == 1
    s_len = x.shape[1]
    f_loc = f_w_up.shape[-1]
    f_pad = -(-f_loc // LANES) * LANES
    f_all = NDEV * f_pad
    d_loc = d // NDEV
    ka, kb, kf = a_conv_w.shape[1], b_conv_w.shape[1], f_conv_w.shape[1]
    alpha = (2.0 * depth) ** 0.25
    assert a_w_in.shape[-1] == f_pad and f_pad % d_loc == 0
    me = 4 * lax.axis_index("x") + 2 * lax.axis_index("y") + lax.axis_index("c")

    small_shapes = [(d,), (ka, d_loc), (2 * d_loc,), (kb, d_loc), (d_loc,), (d_loc,), (d_loc,), (d_loc,),
                    (depth, kf, f_pad)]
    small_loc = _pack([c[0], a_conv_w[0], b_b_pw1[0], b_conv_w[0], b_conv_b[0], b_ln_g[0], b_ln_b[0],
                       b_b_pw2[0], _pad_last(f_conv_w, f_pad)])
    first = _exchange_start([small_loc, a_w_in.astype(BF16)], "gather_chips", "gather_first_chips_start")
    up_pad = _pad_last(_after(f_w_up, first[-1]), f_pad).astype(BF16)
    gate_pad = _pad_last(_after(f_w_gate, first[-1]), f_pad).astype(BF16)
    down_pad = jnp.pad(_after(f_w_down, first[-1]), ((0, 0), (0, f_pad - f_loc), (0, 0))).astype(BF16)
    col_f = [jnp.stack([up_pad[i], gate_pad[i]]) for i in range(depth)]
    row_b = jnp.concatenate([down_pad[1], _after(b_w_pw2[0], first[-1]).astype(BF16)], axis=0)
    out_loc, pw1_loc = _after(a_w_out[0], first[-1]).astype(BF16), _after(b_w_pw1, first[-1]).astype(BF16)
    ridx_pw2 = f_pad // d_loc
    prepared = sum(a.reshape(-1, a.shape[-1])[0:1, 0:LANES].astype(F32)
                   for a in (col_f[0], col_f[1], down_pad[0], row_b, out_loc, pw1_loc))
    *lands, landed = _exchange_wait(first, prepared, "gather_chips", "gather_first_chips_wait")
    g_small, g_in, _ = _exchange_wait(_exchange_start(lands, "forward", "gather_first_forward_start"), landed,
                                      "forward", "gather_first_forward_wait")

    (c_all, acw_g, bb1_g, bcw_g, bcb_g, blg_g, blb_g, bb2_g, fcw_g) = _unpack(g_small, small_shapes, (NDEV,))
    a_cw = acw_g.transpose(1, 0, 2).reshape(ka, d)
    b_cw = bcw_g.transpose(1, 0, 2).reshape(kb, d)
    b_b1 = bb1_g.reshape(1, 2 * d)
    b_cb, b_lg, b_lb, b_b2 = (t.reshape(1, d) for t in (bcb_g, blg_g, blb_g, bb2_g))
    f_cw = fcw_g.transpose(1, 2, 0, 3).reshape(depth, kf, f_all)
    f_cb = _pad_last(f_conv_b.reshape(depth, NDEV, f_loc), f_pad).reshape(depth, 1, f_all)

    ada_b_loc = lax.dynamic_slice(ada_b, (0, me * n_ada), (depth, n_ada))
    mod_part = _ada_fwd(c_all, ada_w, ada_b_loc, "ada_fwd")
    mod_g, mod_done = _exchange([mod_part.reshape(depth * NDEV, n_ada)], "gather", "gather_mod")
    mod_all = mod_g.reshape(NDEV, depth, NDEV, n_ada).transpose(1, 2, 0, 3).reshape(depth, NDEV, 6 * d)
    mod = lax.dynamic_slice(mod_all, (0, me, 0), (depth, 1, 6 * d))[:, 0]

    def behind(a, token):
        return a + token[0, 0].astype(BF16)

    gather_out = _exchange_start([behind(out_loc, mod_done), down_pad[0]], "gather_chips", "gather_out_start")
    gather_f0 = _exchange_start([behind(col_f[0], gather_out[-1])], "gather_chips", "gather_f0_start")

    def mod_rows(i):
        return [mod[i:i + 1, j * d:(j + 1) * d] for j in range(6)]

    zeros_d = jnp.zeros((1, d), F32)
    zeros_f = jnp.zeros((1, f_all), F32)
    x0 = x[0]

    sh_t0, sc_t0, g_t0, sh_c0, sc_c0, g_c0 = mod_rows(0)
    sh_t1, sc_t1, g_t1, sh_c1, sc_c1, g_c1 = mod_rows(1)

    sc_t0 = _after(sc_t0, gather_f0[-1])
    bcv, = _mm_fwd(x0, sc_t0, sh_t0, jnp.zeros((1, 3 * d), F32), g_in, (0,), "a_in_fwd")
    g_out, g_rowf0, landed = _exchange_wait(gather_out, bcv, "gather_chips", "gather_out_wait")
    g_out, g_rowf0, _ = _exchange_wait(_exchange_start([g_out, g_rowf0], "forward", "gather_out_fwd_start"), landed,
                                       "forward", "gather_out_fwd_wait")
    y0, y_a, x1, xh1, rs1 = _a_tail_fwd(bcv, a_cw, a_conv_b, g_out, x0, g_t0, ln_tok_g[0:1], ln_tok_b[0:1],
                                        alpha, "a_tail_fwd")

    def ffn_fwd(xin, sc, sh, gate, gam, bet, g_colf, g_rowf, layer, tag):
        u0, vg = _mm_fwd(xin, sc, sh, zeros_f, g_colf, (0, 1), "f_upgate_fwd" + tag)
        t, uc, y, xo, xh, rs = _ffn_tail_fwd(u0, vg, f_cw[layer], f_cb[layer], g_rowf, xin, gate, gam, bet, alpha,
                                             "f_tail_fwd" + tag)
        return (u0, uc), vg, t, y, xo, xh, rs

    g_colf0, landed = _exchange_wait(gather_f0, x1, "gather_chips", "gather_f0_wait")
    g_colf0, landed = _exchange_wait(
        _exchange_start([g_colf0], "forward", "gather_f0_fwd_start"), landed, "forward", "gather_f0_fwd_wait")
    gather_1 = _exchange_start([behind(pw1_loc, landed), col_f[1], row_b], "gather_chips", "gather_1_start")
    sc_c0 = _after(sc_c0, gather_1[-1])
    u0_0, vg_0, t_0, y_f0, x2, xh2, rs2 = ffn_fwd(x1, sc_c0, sh_c0, g_c0, ln_ch_g[0:1], ln_ch_b[0:1],
                                                  g_colf0, g_rowf0, 0, "0")

    *lands_1, landed = _exchange_wait(gather_1, x2, "gather_chips", "gather_1_wait")
    g_pw1, g_colf1, g_rowb, _ = _exchange_wait(_exchange_start(lands_1, "forward", "gather_1_fwd_start"), landed,
                                                 "forward", "gather_1_fwd_wait")
    ub, = _mm_fwd(x2, sc_t1, sh_t1, b_b1, g_pw1, (0,), "b_pw1_fwd")
    a2, a4 = _b_mid_fwd(ub, b_cw, b_cb, b_lg, b_lb, "b_mid_fwd")
    y_b, x3, xh3, rs3 = _mm_ln(a4, g_rowb, d_loc, ridx_pw2, x2, g_t1, ln_tok_g[1:2], ln_tok_b[1:2], b_b2,
                               alpha, "b_pw2_ln_fwd")
    u0_1, vg_1, t_1, y_f1, x4, xh4, rs4 = ffn_fwd(x3, sc_c1, sh_c1, g_c1, ln_ch_g[1:2], ln_ch_b[1:2],
                                                  g_colf1, g_rowb, 1, "1")

    ln_f1 = (xh4, rs4, ln_ch_g[1:2], y_f1, g_c1)
    ln_b = (xh3, rs3, ln_tok_g[1:2], y_b, g_t1)
    ln_f0 = (xh2, rs2, ln_ch_g[0:1], y_f0, g_c0)
    ln_a = (xh1, rs1, ln_tok_g[0:1], y_a, g_t0)
    dy, dres, accf1, loss_part = _loss_head(x4, loss_target[0], ln_f1, alpha, "loss_head")

    def ffn_bwd(dy, dres, xin, sc, sh, u0, vg, t, g_colf, g_rowf, ln_below, layer, tag):
        dw_down = _mm_tn_row(t, dy, f_pad, f_loc, "f_down_dw" + tag)
        scatter_down = _exchange_start([dw_down], "scatter", "scatter_d%s_start" % tag)
        du0, dvg, dcw, dcb, dy_below, dres_below, acc_below, acc2 = _ffn_core_bwd(
            dy, u0[0], u0[1], vg, f_cw[layer], g_rowf, g_colf, xin, _after(sc, scatter_down[-1]), dres,
            ln_below, alpha, "f_core_bwd" + tag)
        dw_up = _mm_tn_col_t(xin, sc, sh, du0, f_loc, "f_up_dw" + tag)
        dw_gate = _mm_tn_col_t(xin, sc, sh, dvg, f_loc, "f_gate_dw" + tag)
        scatter = _exchange_start([dw_up, dw_gate], "scatter", "scatter_f%s_start" % tag)
        return dy_below, dres_below, acc_below, acc2, (scatter, scatter_down), dcw, dcb

    dy, dres, accb, acc2f1, (scatter_f1, scatter_d1), dfcw1, dfcb1 = ffn_bwd(
        dy, dres, x3, sc_c1, sh_c1, u0_1, vg_1, t_1, g_colf1, g_rowb, ln_b, 1, "1")

    da4 = _mm_nt_row(dy, g_rowb, d_loc, ridx_pw2, "b_pw2_dx")
    dw_pw2 = _mm_tn_row(a4, dy, d_loc, d_loc, "b_pw2_dw")
    du, dbcw, dbcb, dblg, dblb, dbb1 = _b_mid_bwd(ub, a2, da4, b_cw, _after(b_lg, scatter_f1[-1]), b_lb, "b_mid_bwd")
    dw_pw1 = _mm_tn_col(x2, sc_t1, sh_t1, du, "b_pw1_dw")
    scatter_b = _exchange_start([dw_pw1, dw_pw2], "scatter", "scatter_b_start")
    dy, dres, accf0, acc2b = _mm_nt_mod([du], g_pw1, (0,), x2, _after(sc_t1, scatter_b[-1]), dres, "b_pw1_dx",
                                        ln=ln_f0, alpha=alpha)

    dy, dres, acca, acc2f0, (scatter_f0, scatter_d0), dfcw0, dfcb0 = ffn_bwd(
        dy, dres, x1, sc_c0, sh_c0, u0_0, vg_0, t_0, g_colf0, g_rowf0, ln_a, 0, "0")

    dy0 = _mm_nt_row(dy, g_out, d_loc, 0, "a_out_dx")
    dbcv, dacw, dacb = _gateconv_bwd(bcv, dy0, a_cw, _after(a_conv_b, scatter_f0[-1]), "a_conv_bwd")
    dx0, acc2a = _mm_nt_mod([dbcv], g_in, (0,), x0, sc_t0, dres, "a_in_dx")

    def dmod_row(acc2_t, acc_t, acc2_c, acc_c):
        return jnp.concatenate([acc2_t[1], acc2_t[0], acc_t[2], acc2_c[1], acc2_c[0], acc_c[2]])

    dmod = jnp.stack([dmod_row(acc2a, acca, acc2f0, accf0), dmod_row(acc2b, accb, acc2f1, accf1)])

    def unpad_f(a):
        return a.reshape(a.shape[:-1] + (NDEV, f_pad))[..., :f_loc].reshape(a.shape[:-1] + (NDEV * f_loc,))

    small_grads = [
        dmod,
        jnp.stack([acca[0], accb[0]]), jnp.stack([acca[1], accb[1]]),
        jnp.stack([accf0[0], accf1[0]]), jnp.stack([accf0[1], accf1[1]]),
        dacb,
        unpad_f(jnp.concatenate([dfcb0, dfcb1], axis=0)),
        dacw, dbb1, dbcw, dbcb, dblg, dblb, accb[3:4],
        jnp.stack([dfcw0, dfcw1]),
        loss_part[0:1, 0:1],
    ]
    small_grad_shapes = [tuple(g.shape) for g in small_grads]
    gather_small = _exchange_start([_pack(small_grads)], "gather", "gather_small_start")

    dw_in = _mm_tn_col(x0, _after(sc_t0, gather_small[-1]), sh_t0, dbcv, "a_in_dw")
    dw_out = _mm_tn_row(y0, dy, d_loc, d_loc, "a_out_dw")
    scatter_a = _exchange_start([dw_in, dw_out], "scatter", "scatter_a_start")

    grads, deltas, new_m, new_v = {}, {}, {}, {}

    def adamw(k, glist, transposed=False):
        def view(a):
            a = jnp.swapaxes(a, 1, 2) if transposed else a
            return a.reshape(len(glist), -1, a.shape[-1])

        w = view(weights[k])
        outs = _adamw(w, [g.reshape(g.shape[0], -1, w.shape[-1]) for g in glist],
                      view(mom_m[k]), view(mom_v[k]), "adamw_" + k)
        if transposed:
            outs = [jnp.swapaxes(o, 1, 2) for o in outs]
        grads[k], deltas[k], new_m[k], new_v[k] = (o.reshape(weights[k].shape) for o in outs)

    r_up1, r_gate1, _ = _exchange_wait(scatter_f1, scatter_a[-1], "scatter", "scatter_f1_wait")
    r_down1, _ = _exchange_wait(scatter_d1, r_gate1, "scatter", "scatter_d1_wait")
    r_pw1, r_pw2, _ = _exchange_wait(scatter_b, r_down1, "scatter", "scatter_b_wait")
    adamw("b_w_pw1", [r_pw1])
    adamw("b_w_pw2", [r_pw2])
    r_down0, _ = _exchange_wait(scatter_d0, deltas["b_w_pw2"], "scatter", "scatter_d0_wait")
    adamw("f_w_down", [r_down0, r_down1])
    r_up0, r_gate0, _ = _exchange_wait(scatter_f0, deltas["f_w_down"], "scatter", "scatter_f0_wait")
    adamw("f_w_up", [r_up0, r_up1], transposed=True)
    adamw("f_w_gate", [r_gate0, r_gate1], transposed=True)

    sg_all, _ = _exchange_wait(gather_small, deltas["f_w_gate"], "gather", "gather_small_wait")
    sg_sum = _sum_parts(sg_all, "sum_small_grads")
    (g_ada_b, g_ltg, g_ltb, g_lcg, g_lcb, g_acb, g_fcb, g_acw, g_bb1, g_bcw, g_bcb, g_blg, g_blb, g_bb2,
     g_fcw, loss_all) = _unpack(sg_sum, small_grad_shapes)
    loss = loss_all[0, 0]

    def my_cols(a, width):
        return lax.dynamic_slice_in_dim(a, me * width, width, axis=a.ndim - 1)

    g_fcw_loc = my_cols(g_fcw, f_pad)[..., :f_loc]
    small = dict(
        ada_b=g_ada_b, ln_tok_g=g_ltg, ln_tok_b=g_ltb, ln_ch_g=g_lcg, ln_ch_b=g_lcb, a_conv_b=g_acb, f_conv_b=g_fcb,
        a_conv_w=my_cols(g_acw, d_loc)[None], b_b_pw1=my_cols(g_bb1, 2 * d_loc), b_conv_w=my_cols(g_bcw, d_loc)[None],
        b_conv_b=my_cols(g_bcb, d_loc), b_ln_g=my_cols(g_blg, d_loc), b_ln_b=my_cols(g_blb, d_loc),
        b_b_pw2=my_cols(g_bb2, d_loc), f_conv_w=g_fcw_loc)

    dmod_all = sg_all.reshape(NDEV, -1)[:, :depth * 6 * d].reshape(NDEV, depth, 6 * d)
    dmod_cols = my_cols(dmod_all, n_ada).transpose(1, 0, 2)
    g_ada_w = _ada_bwd(c_all.T, dmod_cols, "ada_bwd")

    adamw("ada_w", [g_ada_w[0:1], g_ada_w[1:2]])

    def rows_cols(a):
        return a.reshape(-1, a.shape[-1])

    small_keys = list(small)
    small_outs = _adamw_small([rows_cols(weights[k]) for k in small_keys], [rows_cols(small[k]) for k in small_keys],
                              [rows_cols(mom_m[k]) for k in small_keys], [rows_cols(mom_v[k]) for k in small_keys],
                              "adamw_small")
    for i, k in enumerate(small_keys):
        grads[k] = small[k].reshape(weights[k].shape)
        deltas[k], new_m[k], new_v[k] = (o.reshape(weights[k].shape) for o in small_outs[3 * i:3 * i + 3])

    r_in, r_out, _ = _exchange_wait(scatter_a, deltas["ada_w"], "scatter", "scatter_a_wait")
    adamw("a_w_in", [r_in])
    adamw("a_w_out", [r_out])

    return (loss, dx0[None], *[grads[k] for k in names], *[deltas[k] for k in names],
            *[new_m[k] for k in names], *[new_v[k] for k in names])
```

```python
import jax
import jax.numpy as jnp
from jax import lax
from jax.experimental import pallas as pl
from jax.experimental.pallas import tpu as pltpu

NDEV = 8
MESH_AXES = ("x", "y", "c")
LANES = 128
SUBLANES = 8
VMEM_LIMIT = 56 * 1024 * 1024
LN_EPS = 1e-5
SHORT_PAD = 16
LONG_PAD = 32
CHUNK = 16
DW_ROWS = 1024
MM_ROWS = 512
CONV_ROWS = 256
COL_BLOCK = 1024
ADAM_LR, ADAM_B1, ADAM_B2, ADAM_EPS, ADAM_WD, ADAM_STEP = 0.001, 0.9, 0.999, 1e-08, 0.01, 10

F32 = jnp.float32
BF16 = jnp.bfloat16
MESH = pl.DeviceIdType.MESH
NT = (((1,), (1,)), ((), ()))
TN = (((0,), (0,)), ((), ()))


def _tile(n, target, mult=SUBLANES):
    best = None
    for t in range(mult, min(n, target) + 1, mult):
        if n % t == 0:
            best = t
    return best if best is not None else n


def _full(shape):
    nd = len(shape)
    return pl.BlockSpec(shape, lambda *_: (0,) * nd)


def _cp(*sem):
    return pltpu.CompilerParams(dimension_semantics=sem, vmem_limit_bytes=VMEM_LIMIT)


def _sigmoid(x):
    return 1.0 / (1.0 + jnp.exp(-x))


def _peer(x, y, c, d):
    return ((1 - x) if d & 4 else x, (1 - y) if d & 2 else y, (1 - c) if d & 1 else c)


def _lin(p):
    return 4 * p[0] + 2 * p[1] + p[2]


CHIP_MASKS = (2, 4, 6)
MODES_PER_ARRAY = {"gather": NDEV - 1, "scatter": NDEV - 1, "gather_chips": 1 + len(CHIP_MASKS),
                   "forward": len(CHIP_MASKS)}


def _transfers(mode):
    x, y, c = (lax.axis_index(a) for a in MESH_AXES)
    me = _lin((x, y, c))
    if mode == "forward":
        sibling = (x, y, 1 - c)
        return [(sibling, ("land", _lin(_peer(x, y, c, q))), _lin(_peer(x, y, c, q)), _lin(_peer(x, y, c, q ^ 1)))
                for q in CHIP_MASKS]
    masks = (1,) + CHIP_MASKS if mode == "gather_chips" else range(1, NDEV)
    out = []
    for d in masks:
        peer = _peer(x, y, c, d)
        source = ("block", _lin(peer)) if mode == "scatter" else ("whole", None)
        out.append((peer, source, me, _lin(peer)))
    return out


def _remote_copies(src_refs, land_refs, send_sems, recv_sems, mode):
    transfers = _transfers(mode)
    sends, recvs = [], []
    for i, land_ref in enumerate(land_refs):
        for t, (peer, (kind, slot), there, here) in enumerate(transfers):
            k = i * len(transfers) + t
            src = land_ref.at[slot] if kind == "land" else src_refs[i].at[slot] if kind == "block" else src_refs[i]
            for dst_slot, out in ((there, sends), (here, recvs)):
                out.append(pltpu.make_async_remote_copy(
                    src_ref=src, dst_ref=land_ref.at[dst_slot], send_sem=send_sems.at[k], recv_sem=recv_sems.at[k],
                    device_id=peer, device_id_type=MESH))
    return sends, recvs


def _exchange(srcs, mode, name):
    n = len(srcs)
    gather = mode == "gather"

    def body(*refs):
        src_refs, out_refs, token = refs[:n], refs[n:2 * n], refs[2 * n]
        send_sems, recv_sems, local_sems = refs[2 * n + 1:]
        me = _lin(tuple(lax.axis_index(a) for a in MESH_AXES))
        local = []
        for i in range(n):
            mine = src_refs[i] if gather else src_refs[i].at[me]
            cp = pltpu.make_async_copy(mine, out_refs[i].at[me], local_sems.at[i])
            cp.start()
            local.append(cp)
        sends, recvs = _remote_copies(src_refs, out_refs, send_sems, recv_sems, mode)
        for snd in sends:
            snd.start()
        token[...] = jnp.zeros_like(token)
        for snd, rcv in zip(sends, recvs):
            snd.wait_send()
            rcv.wait_recv()
        for cp in local:
            cp.wait()

    out_shape = [jax.ShapeDtypeStruct(((NDEV,) + s.shape) if gather else s.shape, s.dtype) for s in srcs]
    out_shape.append(jax.ShapeDtypeStruct((SUBLANES, LANES), F32))
    any_spec = pl.BlockSpec(memory_space=pl.ANY)
    return pl.pallas_call(
        body, name=name, out_shape=out_shape,
        in_specs=[any_spec] * n, out_specs=[any_spec] * n + [pl.BlockSpec(memory_space=pltpu.VMEM)],
        scratch_shapes=[pltpu.SemaphoreType.DMA((n * (NDEV - 1),)),
                        pltpu.SemaphoreType.DMA((n * (NDEV - 1),)),
                        pltpu.SemaphoreType.DMA((n,))],
    )(*srcs)


HBM_SPEC = pl.BlockSpec(memory_space=pltpu.HBM)
SEM_SPEC = pl.BlockSpec(memory_space=pltpu.SEMAPHORE)
SIDE_EFFECT = pltpu.SideEffectType.DATAFLOW_SIDE_EFFECTING


def _exchange_start(arrays, mode, name):
    me = _lin(tuple(lax.axis_index(a) for a in MESH_AXES))
    if mode == "forward":
        srcs, lands = [], list(arrays)
    else:
        srcs, lands = list(arrays), []
        for s in srcs:
            own = lax.dynamic_index_in_dim(s, me, 0, keepdims=False) if mode == "scatter" else s
            shape = s.shape if mode == "scatter" else (NDEV,) + s.shape
            lands.append(lax.dynamic_update_index_in_dim(lax.empty(shape, s.dtype), own, me, 0))
    ns, n = len(srcs), len(lands)

    def body(*refs):
        src_refs, land_refs = refs[:ns], refs[ns:ns + n]
        send_sems, recv_sems, token = refs[ns + n], refs[ns + n + 1], refs[-1]
        sends, _ = _remote_copies(src_refs, land_refs, send_sems, recv_sems, mode)
        for snd in sends:
            snd.start()
        token[...] = jnp.zeros_like(token)

    operands = [pltpu.with_memory_space_constraint(a, pltpu.HBM) for a in srcs + lands]
    nsem = n * MODES_PER_ARRAY[mode]
    return pl.pallas_call(
        body, name=name,
        out_shape=(pltpu.SemaphoreType.DMA((nsem,)), pltpu.SemaphoreType.DMA((nsem,)),
                   *[pltpu.HBM(a.shape, a.dtype) for a in operands],
                   jax.ShapeDtypeStruct((SUBLANES, LANES), F32)),
        in_specs=[HBM_SPEC] * (ns + n),
        out_specs=(SEM_SPEC, SEM_SPEC, *([HBM_SPEC] * (ns + n)), pl.BlockSpec(memory_space=pltpu.VMEM)),
        input_output_aliases={i: 2 + i for i in range(ns + n)},
        compiler_params=pltpu.CompilerParams(has_side_effects=SIDE_EFFECT),
    )(*operands)


def _exchange_wait(handle, after, mode, name):
    send_sems, recv_sems, *thru = handle[:-1]
    n = len(thru) if mode == "forward" else len(thru) // 2
    ns = len(thru) - n

    def body(*refs):
        src_refs, land_refs = refs[:ns], refs[ns:ns + n]
        sends, recvs = _remote_copies(src_refs, land_refs, refs[ns + n], refs[ns + n + 1], mode)
        for snd, rcv in zip(sends, recvs):
            snd.wait_send()
            rcv.wait_recv()
        refs[-1][...] = jnp.zeros_like(refs[-1])

    outs = pl.pallas_call(
        body, name=name,
        out_shape=(*[pltpu.HBM(a.shape, a.dtype) for a in thru], jax.ShapeDtypeStruct((SUBLANES, LANES), F32)),
        in_specs=[HBM_SPEC] * (ns + n) + [SEM_SPEC, SEM_SPEC, pl.BlockSpec(memory_space=pl.ANY)],
        out_specs=[HBM_SPEC] * (ns + n) + [pl.BlockSpec(memory_space=pltpu.VMEM)],
        input_output_aliases={i: i for i in range(ns + n)},
        compiler_params=pltpu.CompilerParams(has_side_effects=SIDE_EFFECT),
    )(*thru, send_sems, recv_sems, after)
    return outs[ns:]


def _after(value, token):
    return value + token[0, 0]


ANY_SPEC = pl.BlockSpec(memory_space=pl.ANY)


def _load_cols(wg_ref, widx, w_ref, sems):
    n = wg_ref.shape[-1]
    copies = [pltpu.make_async_copy(wg_ref.at[k, widx], w_ref.at[:, pl.ds(k * n, n)], sems.at[k])
              for k in range(NDEV)]
    for cp in copies:
        cp.start()
    for cp in copies:
        cp.wait()


def _cols_copies(wg_ref, widx, w_ref, sems):
    n = wg_ref.shape[-1]
    return [pltpu.make_async_copy(wg_ref.at[k, widx], w_ref.at[:, pl.ds(k * n, n)], sems.at[k])
            for k in range(NDEV)]


def _rows_copies(wg_ref, r, ridx, w_ref, sems):
    return [pltpu.make_async_copy(wg_ref.at[k, pl.ds(ridx * r, r)], w_ref.at[pl.ds(k * r, r)], sems.at[k])
            for k in range(NDEV)]


def _load_rows(wg_ref, r, ridx, w_ref, sems):
    copies = _rows_copies(wg_ref, r, ridx, w_ref, sems)
    for cp in copies:
        cp.start()
    for cp in copies:
        cp.wait()


def _mm_fwd(x, sc, sh, bias, wg, widxs, name):
    s_len, kdim = x.shape
    ncol = NDEV * wg.shape[-1]
    tm = _tile(s_len, MM_ROWS)
    nw = len(widxs)

    def body(x_ref, sc_ref, sh_ref, b_ref, wg_ref, *rest):
        o_refs, w_refs, sems = rest[:nw], rest[nw:2 * nw], rest[2 * nw]

        @pl.when(pl.program_id(0) == 0)
        def _():
            for i, w_ref in enumerate(w_refs):
                _load_cols(wg_ref, widxs[i], w_ref, sems.at[i])

        h = (x_ref[...] * (1.0 + sc_ref[...]) + sh_ref[...]).astype(BF16)
        for w_ref, o_ref in zip(w_refs, o_refs):
            o_ref[...] = (jnp.dot(h, w_ref[...], preferred_element_type=F32) + b_ref[...]).astype(BF16)

    return pl.pallas_call(
        body, name=name, grid=(s_len // tm,),
        in_specs=[pl.BlockSpec((tm, kdim), lambda i: (i, 0)), _full((1, kdim)), _full((1, kdim)),
                  _full((1, ncol)), ANY_SPEC],
        out_specs=[pl.BlockSpec((tm, ncol), lambda i: (i, 0))] * nw,
        out_shape=[jax.ShapeDtypeStruct((s_len, ncol), BF16)] * nw,
        scratch_shapes=[pltpu.VMEM((kdim, ncol), BF16)] * nw + [pltpu.SemaphoreType.DMA((nw, NDEV))],
        compiler_params=_cp("arbitrary"),
    )(x, sc, sh, bias, wg)


def _mm_ln(a, wg, r, ridx, xres, gate, gam, bet, bias, alpha, name):
    s_len = a.shape[0]
    d = wg.shape[-1]
    tm = _tile(s_len, MM_ROWS)

    def body(a_ref, wg_ref, x_ref, g_ref, gam_ref, bet_ref, b_ref, y_ref, xo_ref, xh_ref, rs_ref, w_ref, sems):
        @pl.when(pl.program_id(0) == 0)
        def _():
            _load_rows(wg_ref, r, ridx, w_ref, sems)

        y = jnp.dot(a_ref[...], w_ref[...], preferred_element_type=F32) + b_ref[...]
        z = alpha * x_ref[...] + g_ref[...] * y
        mu = jnp.mean(z, axis=-1, keepdims=True)
        zc = z - mu
        var = jnp.mean(zc * zc, axis=-1, keepdims=True)
        rstd = lax.rsqrt(var + LN_EPS)
        xh = zc * rstd
        y_ref[...] = y.astype(BF16)
        xh_ref[...] = xh
        rs_ref[...] = rstd
        xo_ref[...] = xh * gam_ref[...] + bet_ref[...]

    row = pl.BlockSpec((tm, d), lambda i: (i, 0))
    vec = _full((1, d))
    return pl.pallas_call(
        body, name=name, grid=(s_len // tm,),
        in_specs=[pl.BlockSpec((tm, NDEV * r), lambda i: (i, 0)), ANY_SPEC, row, vec, vec, vec, vec],
        out_specs=[row, row, row, pl.BlockSpec((tm, 1), lambda i: (i, 0))],
        out_shape=[jax.ShapeDtypeStruct((s_len, d), BF16)] + [jax.ShapeDtypeStruct((s_len, d), F32)] * 2
        + [jax.ShapeDtypeStruct((s_len, 1), F32)],
        scratch_shapes=[pltpu.VMEM((NDEV * r, d), BF16), pltpu.SemaphoreType.DMA((NDEV,))],
        compiler_params=_cp("arbitrary"),
    )(a, wg, xres, gate, gam, bet, bias)


def _ln_in_specs(tm, d):
    row = pl.BlockSpec((tm, d), lambda i: (i, 0))
    return [row, pl.BlockSpec((tm, 1), lambda i: (i, 0)), _full((1, d)), row, _full((1, d))]


def _ln_out_specs(s_len, tm, d):
    row = pl.BlockSpec((tm, d), lambda i: (i, 0))
    return ([row, row, _full((SUBLANES, d))],
            [jax.ShapeDtypeStruct((s_len, d), BF16), jax.ShapeDtypeStruct((s_len, d), F32),
             jax.ShapeDtypeStruct((SUBLANES, d), F32)])


def _ln_bwd_rows(dxo, ln_refs, out_refs, alpha):
    xh_ref, rs_ref, gam_ref, y_ref, g_ref = ln_refs
    dy_ref, dres_ref, acc_ref = out_refs
    xh = xh_ref[...]
    dxh = dxo * gam_ref[...]
    m1 = jnp.mean(dxh, axis=-1, keepdims=True)
    m2 = jnp.mean(dxh * xh, axis=-1, keepdims=True)
    dz = rs_ref[...] * (dxh - m1 - xh * m2)
    dy = g_ref[...] * dz
    dy_ref[...] = dy.astype(BF16)
    dres_ref[...] = alpha * dz
    acc_ref[0:1, :] += jnp.sum(dxo * xh, axis=0, keepdims=True)
    acc_ref[1:2, :] += jnp.sum(dxo, axis=0, keepdims=True)
    acc_ref[2:3, :] += jnp.sum(dz * y_ref[...].astype(F32), axis=0, keepdims=True)
    acc_ref[3:4, :] += jnp.sum(dy, axis=0, keepdims=True)


def _mm_nt_row(dy, wg, r, ridx, name):
    s_len, d = dy.shape
    tm = _tile(s_len, MM_ROWS)

    def body(dy_ref, wg_ref, o_ref, w_ref, sems):
        @pl.when(pl.program_id(0) == 0)
        def _():
            _load_rows(wg_ref, r, ridx, w_ref, sems)

        o_ref[...] = lax.dot_general(dy_ref[...], w_ref[...], NT, preferred_element_type=F32).astype(BF16)

    return pl.pallas_call(
        body, name=name, grid=(s_len // tm,),
        in_specs=[pl.BlockSpec((tm, d), lambda i: (i, 0)), ANY_SPEC],
        out_specs=pl.BlockSpec((tm, NDEV * r), lambda i: (i, 0)),
        out_shape=jax.ShapeDtypeStruct((s_len, NDEV * r), BF16),
        scratch_shapes=[pltpu.VMEM((NDEV * r, d), BF16), pltpu.SemaphoreType.DMA((NDEV,))],
        compiler_params=_cp("arbitrary"),
    )(dy, wg)


def _mm_nt_mod(dos, wg, widxs, xin, sc, dres, name, ln=None, alpha=None):
    s_len, kdim = xin.shape
    ncol = NDEV * wg.shape[-1]
    tm = _tile(s_len, MM_ROWS)
    nw = len(widxs)
    nln = 0 if ln is None else len(ln)
    nout = 2 if ln is None else 4

    def body(*refs):
        do_refs, wg_ref = refs[:nw], refs[nw]
        x_ref, sc_ref, dres_ref = refs[nw + 1:nw + 4]
        ln_refs = refs[nw + 4:nw + 4 + nln]
        out_refs = refs[nw + 4 + nln:nw + 4 + nln + nout]
        w_refs, sems = refs[nw + 4 + nln + nout:-1], refs[-1]
        acc_ref = out_refs[-1]

        @pl.when(pl.program_id(0) == 0)
        def _():
            for ref in out_refs[nout // 2:]:
                ref[...] = jnp.zeros_like(ref)
            for i, w_ref in enumerate(w_refs):
                _load_cols(wg_ref, widxs[i], w_ref, sems.at[i])

        dh = None
        for do_ref, w_ref in zip(do_refs, w_refs):
            p = lax.dot_general(do_ref[...], w_ref[...], NT, preferred_element_type=F32)
            dh = p if dh is None else dh + p
        dx = dh * (1.0 + sc_ref[...]) + dres_ref[...]
        if ln is None:
            out_refs[0][...] = dx
        else:
            _ln_bwd_rows(dx, ln_refs, out_refs[0:3], alpha)
        acc_ref[0:1, :] += jnp.sum(dh * x_ref[...], axis=0, keepdims=True)
        acc_ref[1:2, :] += jnp.sum(dh, axis=0, keepdims=True)

    row = pl.BlockSpec((tm, kdim), lambda i: (i, 0))
    if ln is None:
        out_specs, out_shape = [row], [jax.ShapeDtypeStruct((s_len, kdim), F32)]
    else:
        out_specs, out_shape = _ln_out_specs(s_len, tm, kdim)
    return pl.pallas_call(
        body, name=name, grid=(s_len // tm,),
        in_specs=[pl.BlockSpec((tm, ncol), lambda i: (i, 0))] * nw + [ANY_SPEC, row, _full((1, kdim)), row]
        + ([] if ln is None else _ln_in_specs(tm, kdim)),
        out_specs=out_specs + [_full((SUBLANES, kdim))],
        out_shape=out_shape + [jax.ShapeDtypeStruct((SUBLANES, kdim), F32)],
        scratch_shapes=[pltpu.VMEM((kdim, ncol), BF16)] * nw + [pltpu.SemaphoreType.DMA((nw, NDEV))],
        compiler_params=_cp("arbitrary"),
    )(*dos, wg, xin, sc, dres, *([] if ln is None else ln))


def _mm_tn_col(x, sc, sh, do, name):
    s_len, kdim = x.shape
    n = do.shape[1] // NDEV
    ts = _tile(s_len, DW_ROWS)
    nsteps = s_len // ts

    def body(x_ref, sc_ref, sh_ref, do_ref, o_ref, acc_ref):
        @pl.when(pl.program_id(0) == 0)
        def _():
            acc_ref[...] = jnp.zeros_like(acc_ref)

        h = (x_ref[...] * (1.0 + sc_ref[...]) + sh_ref[...]).astype(BF16)
        acc_ref[...] += lax.dot_general(h, do_ref[...], TN, preferred_element_type=F32)

        @pl.when(pl.program_id(0) == nsteps - 1)
        def _():
            for k in range(NDEV):
                o_ref[k] = acc_ref[:, k * n:(k + 1) * n].astype(BF16)

    return pl.pallas_call(
        body, name=name, grid=(nsteps,),
        in_specs=[pl.BlockSpec((ts, kdim), lambda i: (i, 0)), _full((1, kdim)), _full((1, kdim)),
                  pl.BlockSpec((ts, NDEV * n), lambda i: (i, 0))],
        out_specs=_full((NDEV, kdim, n)),
        out_shape=jax.ShapeDtypeStruct((NDEV, kdim, n), BF16),
        scratch_shapes=[pltpu.VMEM((kdim, NDEV * n), F32)],
        compiler_params=_cp("arbitrary"),
    )(x, sc, sh, do)


def _mm_tn_col_t(x, sc, sh, do, rows_out, name):
    s_len, kdim = x.shape
    n = do.shape[1] // NDEV
    ts = _tile(s_len, DW_ROWS)
    nsteps = s_len // ts

    def body(x_ref, sc_ref, sh_ref, do_ref, o_ref, acc_ref):
        @pl.when(pl.program_id(0) == 0)
        def _():
            acc_ref[...] = jnp.zeros_like(acc_ref)

        h = (x_ref[...] * (1.0 + sc_ref[...]) + sh_ref[...]).astype(BF16)
        acc_ref[...] += lax.dot_general(do_ref[...], h, TN, preferred_element_type=F32)

        @pl.when(pl.program_id(0) == nsteps - 1)
        def _():
            for k in range(NDEV):
                o_ref[k] = acc_ref[k * n:k * n + rows_out, :].astype(BF16)

    return pl.pallas_call(
        body, name=name, grid=(nsteps,),
        in_specs=[pl.BlockSpec((ts, kdim), lambda i: (i, 0)), _full((1, kdim)), _full((1, kdim)),
                  pl.BlockSpec((ts, NDEV * n), lambda i: (i, 0))],
        out_specs=_full((NDEV, rows_out, kdim)),
        out_shape=jax.ShapeDtypeStruct((NDEV, rows_out, kdim), BF16),
        scratch_shapes=[pltpu.VMEM((NDEV * n, kdim), F32)],
        compiler_params=_cp("arbitrary"),
    )(x, sc, sh, do)


def _mm_tn_row(a, dy, r, rows_out, name):
    s_len, d = dy.shape
    ts = _tile(s_len, DW_ROWS)
    nsteps = s_len // ts

    def body(a_ref, dy_ref, o_ref, acc_ref):
        @pl.when(pl.program_id(0) == 0)
        def _():
            acc_ref[...] = jnp.zeros_like(acc_ref)

        acc_ref[...] += lax.dot_general(a_ref[...], dy_ref[...], TN, preferred_element_type=F32)

        @pl.when(pl.program_id(0) == nsteps - 1)
        def _():
            for k in range(NDEV):
                o_ref[k] = acc_ref[k * r:k * r + rows_out, :].astype(BF16)

    return pl.pallas_call(
        body, name=name, grid=(nsteps,),
        in_specs=[pl.BlockSpec((ts, NDEV * r), lambda i: (i, 0)), pl.BlockSpec((ts, d), lambda i: (i, 0))],
        out_specs=_full((NDEV, rows_out, d)),
        out_shape=jax.ShapeDtypeStruct((NDEV, rows_out, d), BF16),
        scratch_shapes=[pltpu.VMEM((NDEV * r, d), F32)],
        compiler_params=_cp("arbitrary"),
    )(a, dy)


def _prev_spec(ts, pad, cb, col):
    return pl.BlockSpec((pad, cb), lambda *g: (jnp.maximum(g[-1] * (ts // pad) - 1, 0), col(g)))


def _next_spec(ts, pad, cb, col, s_len):
    return pl.BlockSpec((pad, cb), lambda *g: (jnp.minimum((g[-1] + 1) * (ts // pad), s_len // pad - 1), col(g)))


class _F32Loads:
    def __init__(self, ref):
        self.ref = ref

    def __getitem__(self, idx):
        return self.ref[idx].astype(F32)


def _direct(buf_ref):
    return lambda off, rows: buf_ref[off:off + rows, :]


def _make_shifts(sh_ref, nrows):
    for r in range(1, SUBLANES):
        sh_ref[r, 0:nrows - SUBLANES, :] = sh_ref[0, r:r + nrows - SUBLANES, :]


def _shifted(sh_ref):
    def read(off, rows):
        r = off % SUBLANES
        return sh_ref[r, off - r:off - r + rows, :]
    return read


def _conv_fwd_rows(read, w_ref, b_ref, ktaps, pad, r0, rows):
    acc = None
    for j in range(ktaps):
        term = w_ref[ktaps - 1 - j:ktaps - j, :] * read(pad - j + r0, rows)
        acc = term if acc is None else acc + term
    return acc + b_ref[...]


def _conv_bwd_rows(read, x_rows, w_ref, dwacc_ref, ktaps, r0, rows):
    acc = None
    for j in range(ktaps):
        sl = read(j + r0, rows)
        term = w_ref[ktaps - 1 - j:ktaps - j, :] * sl
        acc = term if acc is None else acc + term
        prod = x_rows * sl
        fold = prod[0:SUBLANES]
        for q in range(1, rows // SUBLANES):
            fold = fold + prod[q * SUBLANES:(q + 1) * SUBLANES]
        tap = ktaps - 1 - j
        dwacc_ref[tap * SUBLANES:(tap + 1) * SUBLANES, :] += fold
    return acc


def _flush_dw(dwacc_ref, dw_ref, ktaps):
    for tap in range(ktaps):
        dw_ref[tap:tap + 1, :] = jnp.sum(dwacc_ref[tap * SUBLANES:(tap + 1) * SUBLANES, :], axis=0, keepdims=True)


def _gateconv_fwd(bcv, cw, cb, name):
    s_len, d3 = bcv.shape
    d = d3 // 3
    ktaps = cw.shape[0]
    pad = SHORT_PAD
    ts = _tile(s_len, CONV_ROWS)

    def body(gb_ref, gc_ref, v_ref, gcp_ref, vp_ref, w_ref, b_ref, o_ref, pbuf):
        gb_ref, gc_ref, v_ref, gcp_ref, vp_ref = map(_F32Loads, (gb_ref, gc_ref, v_ref, gcp_ref, vp_ref))
        s = pl.program_id(0)
        pbuf[0:pad, :] = jnp.where(s > 0, gcp_ref[...] * vp_ref[...], 0.0)
        pbuf[pad:pad + ts, :] = gc_ref[...] * v_ref[...]
        for r0 in range(0, ts, CHUNK):
            q = _conv_fwd_rows(_direct(pbuf), w_ref, b_ref, ktaps, pad, r0, CHUNK)
            o_ref[r0:r0 + CHUNK, :] = (gb_ref[r0:r0 + CHUNK, :] * q).astype(BF16)

    def cur(part):
        return pl.BlockSpec((ts, d), lambda s: (s, part))

    return pl.pallas_call(
        body, name=name, grid=(s_len // ts,),
        in_specs=[cur(0), cur(1), cur(2),
                  _prev_spec(ts, pad, d, lambda g: 1), _prev_spec(ts, pad, d, lambda g: 2),
                  _full((ktaps, d)), _full((1, d))],
        out_specs=pl.BlockSpec((ts, d), lambda s: (s, 0)),
        out_shape=jax.ShapeDtypeStruct((s_len, d), BF16),
        scratch_shapes=[pltpu.VMEM((pad + ts, d), F32)],
        compiler_params=_cp("parallel"),
    )(bcv, bcv, bcv, bcv, bcv, cw, cb)


def _gateconv_bwd(bcv, dy0, cw, cb, name):
    s_len, d3 = bcv.shape
    d = d3 // 3
    ktaps = cw.shape[0]
    pad = SHORT_PAD
    ts = _tile(s_len, CONV_ROWS)
    nsteps = s_len // ts

    def body(gb_ref, gc_ref, v_ref, gcp_ref, vp_ref, gbn_ref, dy_ref, dyn_ref, w_ref, b_ref,
             o_ref, dw_ref, db_ref, pbuf, dqbuf, dwacc):
        gb_ref, gc_ref, v_ref, gcp_ref, vp_ref, gbn_ref, dy_ref, dyn_ref = map(
            _F32Loads, (gb_ref, gc_ref, v_ref, gcp_ref, vp_ref, gbn_ref, dy_ref, dyn_ref))
        s = pl.program_id(0)

        @pl.when(s == 0)
        def _():
            dwacc[...] = jnp.zeros_like(dwacc)
            db_ref[...] = jnp.zeros_like(db_ref)

        pbuf[0:pad, :] = jnp.where(s > 0, gcp_ref[...] * vp_ref[...], 0.0)
        pbuf[pad:pad + ts, :] = gc_ref[...] * v_ref[...]
        dq = dy_ref[...] * gb_ref[...]
        dqbuf[0:ts, :] = dq
        dqbuf[ts:ts + pad, :] = jnp.where(s < nsteps - 1, dyn_ref[...] * gbn_ref[...], 0.0)
        db_ref[...] += jnp.sum(dq, axis=0, keepdims=True)
        for r0 in range(0, ts, CHUNK):
            rows = slice(r0, r0 + CHUNK)
            q = _conv_fwd_rows(_direct(pbuf), w_ref, b_ref, ktaps, pad, r0, CHUNK)
            o_ref[rows, 0:d] = (dy_ref[rows, :] * q).astype(BF16)
            dp = _conv_bwd_rows(_direct(dqbuf), pbuf[pad + r0:pad + r0 + CHUNK, :], w_ref, dwacc, ktaps, r0, CHUNK)
            o_ref[rows, d:2 * d] = (dp * v_ref[rows, :]).astype(BF16)
            o_ref[rows, 2 * d:3 * d] = (dp * gc_ref[rows, :]).astype(BF16)

        @pl.when(s == nsteps - 1)
        def _():
            _flush_dw(dwacc, dw_ref, ktaps)

    def cur(part):
        return pl.BlockSpec((ts, d), lambda s: (s, part))

    return pl.pallas_call(
        body, name=name, grid=(nsteps,),
        in_specs=[cur(0), cur(1), cur(2),
                  _prev_spec(ts, pad, d, lambda g: 1), _prev_spec(ts, pad, d, lambda g: 2),
                  _next_spec(ts, pad, d, lambda g: 0, s_len),
                  cur(0), _next_spec(ts, pad, d, lambda g: 0, s_len),
                  _full((ktaps, d)), _full((1, d))],
        out_specs=[pl.BlockSpec((ts, d3), lambda s: (s, 0)), _full((ktaps, d)), _full((1, d))],
        out_shape=[jax.ShapeDtypeStruct((s_len, d3), BF16), jax.ShapeDtypeStruct((ktaps, d), F32),
                   jax.ShapeDtypeStruct((1, d), F32)],
        scratch_shapes=[pltpu.VMEM((pad + ts, d), F32), pltpu.VMEM((ts + pad, d), F32),
                        pltpu.VMEM((ktaps * SUBLANES, d), F32)],
        compiler_params=_cp("arbitrary"),
    )(bcv, bcv, bcv, bcv, bcv, bcv, dy0, dy0, cw, cb)


class _Cols:
    def __init__(self, ref, cols):
        self.ref, self.cols = ref, cols

    def __getitem__(self, idx):
        return self.ref[slice(None) if idx is Ellipsis else idx[0], self.cols]

    def __setitem__(self, idx, value):
        self.ref[idx[0], self.cols] = value


def _ffn_tail_fwd(u0, vg, cw, cb, wg, xres, gate, gam, bet, alpha, name):
    s_len, f = u0.shape
    d = wg.shape[-1]
    r = f // NDEV
    ktaps = cw.shape[0]
    pad = SHORT_PAD
    tm = _tile(s_len, CONV_ROWS)
    cbk = COL_BLOCK if f % COL_BLOCK == 0 else f

    def body(u_ref, up_ref, vg_ref, cw_ref, cb_ref, wg_ref, x_ref, g_ref, gam_ref, bet_ref,
             t_ref, uc_ref, y_ref, xo_ref, xh_ref, rs_ref, ubuf, w_ref, sems):
        u_ref, up_ref, vg_ref = map(_F32Loads, (u_ref, up_ref, vg_ref))
        s = pl.program_id(0)

        @pl.when(s == 0)
        def _():
            for cp in _rows_copies(wg_ref, r, 0, w_ref, sems):
                cp.start()

        ubuf[0:pad, :] = jnp.where(s > 0, up_ref[...], 0.0)
        ubuf[pad:pad + tm, :] = u_ref[...]
        y = None
        for c0 in range(0, f, cbk):
            cols = slice(c0, c0 + cbk)
            read = _direct(_Cols(ubuf, cols))
            for r0 in range(0, tm, CHUNK):
                rows = slice(r0, r0 + CHUNK)
                u = _conv_fwd_rows(read, _Cols(cw_ref, cols), _Cols(cb_ref, cols), ktaps, pad, r0, CHUNK)
                t_ref[rows, cols] = (u * _sigmoid(u) * vg_ref[rows, cols]).astype(BF16)
                uc_ref[rows, cols] = u.astype(BF16)
            if c0 == 0:
                @pl.when(s == 0)
                def _():
                    for cp in _rows_copies(wg_ref, r, 0, w_ref, sems):
                        cp.wait()
            p = jnp.dot(t_ref[:, cols], w_ref[cols, :], preferred_element_type=F32)
            y = p if y is None else y + p
        z = alpha * x_ref[...] + g_ref[...] * y
        mu = jnp.mean(z, axis=-1, keepdims=True)
        zc = z - mu
        var = jnp.mean(zc * zc, axis=-1, keepdims=True)
        rstd = lax.rsqrt(var + LN_EPS)
        xh = zc * rstd
        y_ref[...] = y.astype(BF16)
        xh_ref[...] = xh
        rs_ref[...] = rstd
        xo_ref[...] = xh * gam_ref[...] + bet_ref[...]

    wide = pl.BlockSpec((tm, f), lambda i: (i, 0))
    row = pl.BlockSpec((tm, d), lambda i: (i, 0))
    vec = _full((1, d))
    return pl.pallas_call(
        body, name=name, grid=(s_len // tm,),
        in_specs=[wide, _prev_spec(tm, pad, f, lambda g: 0), wide, _full((ktaps, f)), _full((1, f)), ANY_SPEC,
                  row, vec, vec, vec],
        out_specs=[wide, wide, row, row, row, pl.BlockSpec((tm, 1), lambda i: (i, 0))],
        out_shape=[jax.ShapeDtypeStruct((s_len, f), BF16), jax.ShapeDtypeStruct((s_len, f), BF16),
                   jax.ShapeDtypeStruct((s_len, d), BF16),
                   jax.ShapeDtypeStruct((s_len, d), F32), jax.ShapeDtypeStruct((s_len, d), F32),
                   jax.ShapeDtypeStruct((s_len, 1), F32)],
        scratch_shapes=[pltpu.VMEM((pad + tm, f), F32), pltpu.VMEM((f, d), BF16), pltpu.SemaphoreType.DMA((NDEV,))],
        compiler_params=_cp("arbitrary"),
    )(u0, u0, vg, cw, cb, wg, xres, gate, gam, bet)


def _ffn_core_bwd(dy, u0, uc, vg, cw, wg_row, wg_col, xin, sc, dres, ln, alpha, name):
    s_len, f = u0.shape
    d = xin.shape[1]
    r = f // NDEV
    ktaps = cw.shape[0]
    pad = SHORT_PAD
    tm = _tile(s_len, CONV_ROWS)
    nsteps = s_len // tm
    cbk = COL_BLOCK if f % COL_BLOCK == 0 else f

    def body(dy_ref, dyn_ref, u_ref, uc_ref, ucn_ref, vg_ref, vgn_ref, cw_ref, wgr_ref, wgc_ref,
             x_ref, sc_ref, dres_ref, xh_ref, rs_ref, gam_ref, y_ref, g_ref,
             du0_ref, dvg_ref, dw_ref, db_ref, dyo_ref, dreso_ref, lnacc_ref, acc_ref,
             dtbuf, dubuf, dwacc, wd_ref, wup_ref, wgate_ref, sems):
        u_ref, uc_ref, ucn_ref, vg_ref, vgn_ref = map(_F32Loads, (u_ref, uc_ref, ucn_ref, vg_ref, vgn_ref))
        s = pl.program_id(0)
        last = s == nsteps - 1

        @pl.when(s == 0)
        def _():
            dwacc[...] = jnp.zeros_like(dwacc)
            db_ref[...] = jnp.zeros_like(db_ref)
            acc_ref[...] = jnp.zeros_like(acc_ref)
            lnacc_ref[...] = jnp.zeros_like(lnacc_ref)
            later = _cols_copies(wgc_ref, 0, wup_ref, sems.at[1]) + _cols_copies(wgc_ref, 1, wgate_ref, sems.at[2])
            for cp in later:
                cp.start()
            _load_rows(wgr_ref, r, 0, wd_ref, sems.at[0])

        dy_cur, dy_nxt = dy_ref[...], dyn_ref[...]
        dh = None
        for c0 in range(0, f, cbk):
            cols = slice(c0, c0 + cbk)
            wd_blk = wd_ref[cols, :]
            dtbuf[0:tm, :] = lax.dot_general(dy_cur, wd_blk, NT, preferred_element_type=F32)
            dtbuf[tm:tm + pad, :] = jnp.where(
                last, 0.0, lax.dot_general(dy_nxt, wd_blk, NT, preferred_element_type=F32))
            for r0 in range(0, tm + pad, CHUNK):
                dtr = dtbuf[r0:r0 + CHUNK, :]
                if r0 < tm:
                    u, vgr = uc_ref[r0:r0 + CHUNK, cols], vg_ref[r0:r0 + CHUNK, cols]
                    sg = _sigmoid(u)
                    dvg_ref[r0:r0 + CHUNK, cols] = (dtr * u * sg).astype(BF16)
                else:
                    u, vgr = ucn_ref[r0 - tm:r0 - tm + CHUNK, cols], vgn_ref[r0 - tm:r0 - tm + CHUNK, cols]
                    sg = _sigmoid(u)
                dubuf[r0:r0 + CHUNK, :] = dtr * vgr * (sg * (1.0 + u * (1.0 - sg)))
            db_ref[:, cols] += jnp.sum(dubuf[0:tm, :], axis=0, keepdims=True)
            for r0 in range(0, tm, CHUNK):
                du0 = _conv_bwd_rows(_direct(dubuf), u_ref[r0:r0 + CHUNK, cols], _Cols(cw_ref, cols),
                                     _Cols(dwacc, cols), ktaps, r0, CHUNK)
                du0_ref[r0:r0 + CHUNK, cols] = du0.astype(BF16)
            if c0 == 0:
                @pl.when(s == 0)
                def _():
                    for cp in (_cols_copies(wgc_ref, 0, wup_ref, sems.at[1])
                               + _cols_copies(wgc_ref, 1, wgate_ref, sems.at[2])):
                        cp.wait()
            p = (lax.dot_general(du0_ref[:, cols], wup_ref[:, cols], NT, preferred_element_type=F32)
                 + lax.dot_general(dvg_ref[:, cols], wgate_ref[:, cols], NT, preferred_element_type=F32))
            dh = p if dh is None else dh + p
        dx = dh * (1.0 + sc_ref[...]) + dres_ref[...]
        _ln_bwd_rows(dx, (xh_ref, rs_ref, gam_ref, y_ref, g_ref), (dyo_ref, dreso_ref, lnacc_ref), alpha)
        acc_ref[0:1, :] += jnp.sum(dh * x_ref[...], axis=0, keepdims=True)
        acc_ref[1:2, :] += jnp.sum(dh, axis=0, keepdims=True)

        @pl.when(last)
        def _():
            _flush_dw(dwacc, dw_ref, ktaps)

    wide = pl.BlockSpec((tm, f), lambda i: (i, 0))
    row = pl.BlockSpec((tm, d), lambda i: (i, 0))
    ln_out_specs, ln_out_shape = _ln_out_specs(s_len, tm, d)
    return pl.pallas_call(
        body, name=name, grid=(nsteps,),
        in_specs=[row, _next_spec(tm, pad, d, lambda g: 0, s_len),
                  wide, wide, _next_spec(tm, pad, f, lambda g: 0, s_len),
                  wide, _next_spec(tm, pad, f, lambda g: 0, s_len),
                  _full((ktaps, f)), ANY_SPEC, ANY_SPEC, row, _full((1, d)), row]
        + _ln_in_specs(tm, d),
        out_specs=[wide, wide, _full((ktaps, f)), _full((1, f))] + ln_out_specs + [_full((SUBLANES, d))],
        out_shape=[jax.ShapeDtypeStruct((s_len, f), BF16), jax.ShapeDtypeStruct((s_len, f), BF16),
                   jax.ShapeDtypeStruct((ktaps, f), F32), jax.ShapeDtypeStruct((1, f), F32)]
        + ln_out_shape + [jax.ShapeDtypeStruct((SUBLANES, d), F32)],
        scratch_shapes=[pltpu.VMEM((tm + pad, cbk), F32),
                        pltpu.VMEM((tm + pad, cbk), F32), pltpu.VMEM((ktaps * SUBLANES, f), F32),
                        pltpu.VMEM((f, d), BF16), pltpu.VMEM((d, f), BF16), pltpu.VMEM((d, f), BF16),
                        pltpu.SemaphoreType.DMA((3, NDEV))],
        compiler_params=_cp("arbitrary"),
    )(dy, dy, u0, uc, uc, vg, vg, cw, wg_row, wg_col, xin, sc, dres, *ln)


def _b_mid_fwd(ub, cw, cb, lng, lnb, name):
    s_len, d2 = ub.shape
    d = d2 // 2
    ktaps = cw.shape[0]
    pad = LONG_PAD
    ts = _tile(s_len, CONV_ROWS)

    def body(a_ref, g_ref, ap_ref, gp_ref, w_ref, b_ref, lng_ref, lnb_ref, a2_ref, a4_ref, abuf):
        a_ref, g_ref, ap_ref, gp_ref = map(_F32Loads, (a_ref, g_ref, ap_ref, gp_ref))
        s = pl.program_id(0)
        abuf[0, 0:pad, :] = jnp.where(s > 0, ap_ref[...] * _sigmoid(gp_ref[...]), 0.0)
        abuf[0, pad:pad + ts, :] = a_ref[...] * _sigmoid(g_ref[...])
        _make_shifts(abuf, pad + ts)
        for r0 in range(0, ts, CHUNK):
            a2_ref[r0:r0 + CHUNK, :] = _conv_fwd_rows(_shifted(abuf), w_ref, b_ref, ktaps, pad, r0, CHUNK)
        a2 = a2_ref[...]
        mu = jnp.mean(a2, axis=-1, keepdims=True)
        ac = a2 - mu
        var = jnp.mean(ac * ac, axis=-1, keepdims=True)
        a3 = ac * lax.rsqrt(var + LN_EPS) * lng_ref[...] + lnb_ref[...]
        a4_ref[...] = (a3 * _sigmoid(a3)).astype(BF16)

    def cur(part):
        return pl.BlockSpec((ts, d), lambda s: (s, part))

    vec = _full((1, d))
    return pl.pallas_call(
        body, name=name, grid=(s_len // ts,),
        in_specs=[cur(0), cur(1), _prev_spec(ts, pad, d, lambda g: 0), _prev_spec(ts, pad, d, lambda g: 1),
                  _full((ktaps, d)), vec, vec, vec],
        out_specs=[cur(0), cur(0)],
        out_shape=[jax.ShapeDtypeStruct((s_len, d), F32), jax.ShapeDtypeStruct((s_len, d), BF16)],
        scratch_shapes=[pltpu.VMEM((SUBLANES, pad + ts, d), F32)],
        compiler_params=_cp("parallel"),
    )(ub, ub, ub, ub, cw, cb, lng, lnb)


def _b_mid_bwd(ub, a2, da4, cw, lng, lnb, name):
    s_len, d2 = ub.shape
    d = d2 // 2
    ktaps = cw.shape[0]
    pad = LONG_PAD
    ts = _tile(s_len, CONV_ROWS)
    nsteps = s_len // ts

    def body(a_ref, g_ref, a2_ref, a2n_ref, da4_ref, da4n_ref, w_ref, lng_ref, lnb_ref,
             du_ref, dw_ref, db_ref, dlng_ref, dlnb_ref, dbias_ref, dabuf, dwacc):
        a_ref, g_ref, da4_ref, da4n_ref = map(_F32Loads, (a_ref, g_ref, da4_ref, da4n_ref))
        s = pl.program_id(0)
        last = s == nsteps - 1

        @pl.when(s == 0)
        def _():
            dwacc[...] = jnp.zeros_like(dwacc)
            for ref in (db_ref, dlng_ref, dlnb_ref, dbias_ref):
                ref[...] = jnp.zeros_like(ref)

        def ln_silu_bwd(a2_t, da4_t):
            mu = jnp.mean(a2_t, axis=-1, keepdims=True)
            ac = a2_t - mu
            var = jnp.mean(ac * ac, axis=-1, keepdims=True)
            rstd = lax.rsqrt(var + LN_EPS)
            ah = ac * rstd
            a3 = ah * lng_ref[...] + lnb_ref[...]
            sg = _sigmoid(a3)
            da3 = da4_t * (sg * (1.0 + a3 * (1.0 - sg)))
            dah = da3 * lng_ref[...]
            m1 = jnp.mean(dah, axis=-1, keepdims=True)
            m2 = jnp.mean(dah * ah, axis=-1, keepdims=True)
            return rstd * (dah - m1 - ah * m2), da3, ah

        da2, da3, ah = ln_silu_bwd(a2_ref[...], da4_ref[...])
        dabuf[0, 0:ts, :] = da2
        dlng_ref[...] += jnp.sum(da3 * ah, axis=0, keepdims=True)
        dlnb_ref[...] += jnp.sum(da3, axis=0, keepdims=True)
        db_ref[...] += jnp.sum(da2, axis=0, keepdims=True)
        da2n, _, _ = ln_silu_bwd(a2n_ref[...], jnp.where(last, 0.0, da4n_ref[...]))
        dabuf[0, ts:ts + pad, :] = da2n
        _make_shifts(dabuf, ts + pad)
        for r0 in range(0, ts, CHUNK):
            rows = slice(r0, r0 + CHUNK)
            a_r, g_r = a_ref[rows, :], g_ref[rows, :]
            sg = _sigmoid(g_r)
            da1 = _conv_bwd_rows(_shifted(dabuf), a_r * sg, w_ref, dwacc, ktaps, r0, CHUNK)
            da = da1 * sg
            dg = da1 * a_r * sg * (1.0 - sg)
            du_ref[rows, 0:d] = da.astype(BF16)
            du_ref[rows, d:2 * d] = dg.astype(BF16)
            dbias_ref[:, 0:d] += jnp.sum(da, axis=0, keepdims=True)
            dbias_ref[:, d:2 * d] += jnp.sum(dg, axis=0, keepdims=True)

        @pl.when(last)
        def _():
            _flush_dw(dwacc, dw_ref, ktaps)

    def cur(part):
        return pl.BlockSpec((ts, d), lambda s: (s, part))

    vec = _full((1, d))
    nxt = _next_spec(ts, pad, d, lambda g: 0, s_len)
    return pl.pallas_call(
        body, name=name, grid=(nsteps,),
        in_specs=[cur(0), cur(1), cur(0), nxt, cur(0), nxt, _full((ktaps, d)), vec, vec],
        out_specs=[pl.BlockSpec((ts, d2), lambda s: (s, 0)), _full((ktaps, d)), vec, vec, vec, _full((1, d2))],
        out_shape=[jax.ShapeDtypeStruct((s_len, d2), BF16), jax.ShapeDtypeStruct((ktaps, d), F32),
                   jax.ShapeDtypeStruct((1, d), F32), jax.ShapeDtypeStruct((1, d), F32),
                   jax.ShapeDtypeStruct((1, d), F32), jax.ShapeDtypeStruct((1, d2), F32)],
        scratch_shapes=[pltpu.VMEM((SUBLANES, ts + pad, d), F32), pltpu.VMEM((ktaps * SUBLANES, d), F32)],
        compiler_params=_cp("arbitrary"),
    )(ub, ub, a2, a2, da4, da4, cw, lng, lnb)


def _loss_head(xo, tgt, ln, alpha, name):
    s_len, d = xo.shape
    tm = _tile(s_len, MM_ROWS)

    def body(x_ref, t_ref, xh_ref, rs_ref, gam_ref, y_ref, g_ref, dy_ref, dres_ref, acc_ref, l_ref):
        @pl.when(pl.program_id(0) == 0)
        def _():
            l_ref[...] = jnp.zeros_like(l_ref)
            acc_ref[...] = jnp.zeros_like(acc_ref)

        e = x_ref[...] - t_ref[...]
        per_row = jnp.sum(e * e, axis=-1, keepdims=True) * (1.0 / d)
        l_ref[...] += 0.5 * jnp.sum(per_row, axis=0, keepdims=True)
        _ln_bwd_rows(e * (1.0 / d), (xh_ref, rs_ref, gam_ref, y_ref, g_ref), (dy_ref, dres_ref, acc_ref), alpha)

    row = pl.BlockSpec((tm, d), lambda i: (i, 0))
    ln_out_specs, ln_out_shape = _ln_out_specs(s_len, tm, d)
    return pl.pallas_call(
        body, name=name, grid=(s_len // tm,),
        in_specs=[row, row] + _ln_in_specs(tm, d), out_specs=ln_out_specs + [_full((1, LANES))],
        out_shape=ln_out_shape + [jax.ShapeDtypeStruct((1, LANES), F32)],
        compiler_params=_cp("arbitrary"),
    )(xo, tgt, *ln)


def _ada_fwd(c_all, ada_w, ada_b_loc, name):
    depth, d, n = ada_w.shape

    def body(c_ref, w_ref, b_ref, o_ref):
        c = c_ref[...]
        act = c * _sigmoid(c)
        o_ref[...] = jnp.dot(act, w_ref[...], preferred_element_type=F32,
                             precision=lax.Precision.HIGHEST) + b_ref[...]

    return pl.pallas_call(
        body, name=name, grid=(depth,),
        in_specs=[_full((NDEV, d)), pl.BlockSpec((None, d, n), lambda i: (i, 0, 0)),
                  pl.BlockSpec((None, 1, n), lambda i: (i, 0, 0))],
        out_specs=pl.BlockSpec((None, NDEV, n), lambda i: (i, 0, 0)),
        out_shape=jax.ShapeDtypeStruct((depth, NDEV, n), F32),
        compiler_params=_cp("parallel"),
    )(c_all, ada_w, ada_b_loc.reshape(depth, 1, n))


def _ada_bwd(c_all_t, dmod_cols, name):
    depth, _, n = dmod_cols.shape
    d = c_all_t.shape[0]

    def body(ct_ref, dm_ref, o_ref):
        ct = ct_ref[...]
        act = ct * _sigmoid(ct)
        acc = None
        for b in range(NDEV):
            term = act[:, b:b + 1] * dm_ref[b:b + 1, :]
            acc = term if acc is None else acc + term
        o_ref[...] = acc

    return pl.pallas_call(
        body, name=name, grid=(depth,),
        in_specs=[_full((d, NDEV)), pl.BlockSpec((None, NDEV, n), lambda i: (i, 0, 0))],
        out_specs=pl.BlockSpec((None, d, n), lambda i: (i, 0, 0)),
        out_shape=jax.ShapeDtypeStruct((depth, d, n), F32),
        compiler_params=_cp("parallel"),
    )(c_all_t, dmod_cols)


def _sum_parts(parts, name):
    _, rows, lanes = parts.shape

    def body(p_ref, o_ref):
        acc = p_ref[0]
        for k in range(1, NDEV):
            acc = acc + p_ref[k]
        o_ref[...] = acc

    return pl.pallas_call(
        body, name=name, in_specs=[_full(parts.shape)], out_specs=_full((rows, lanes)), grid=(1,),
        out_shape=jax.ShapeDtypeStruct((rows, lanes), F32), compiler_params=_cp("arbitrary"),
    )(parts)


def _adamw(w, glist, m, v, name):
    nl, rows, cols = w.shape
    tr = _tile(rows, 256, 2 * SUBLANES)

    def body(w_ref, *rest):
        g_refs = rest[:nl]
        m_ref, v_ref, go_ref, d_ref, mo_ref, vo_ref = rest[nl:]
        g = None
        for layer, g_ref in enumerate(g_refs):
            part = g_ref[0].astype(F32)
            for p in range(1, g_ref.shape[0]):
                part = part + g_ref[p].astype(F32)
            g = part if g is None else jnp.where(pl.program_id(0) == layer, part, g)
        go_ref[...] = g
        d_ref[...], mo_ref[...], vo_ref[...] = _adam_step(w_ref[...], g, m_ref[...], v_ref[...])

    blk = pl.BlockSpec((None, tr, cols), lambda l, i: (l, i, 0))
    g_specs = [pl.BlockSpec((g.shape[0], tr, cols), lambda l, i: (0, i, 0)) for g in glist]
    return pl.pallas_call(
        body, name=name, grid=(nl, rows // tr),
        in_specs=[blk] + g_specs + [blk, blk],
        out_specs=[blk] * 4, out_shape=[jax.ShapeDtypeStruct((nl, rows, cols), F32)] * 4,
        compiler_params=_cp("parallel", "parallel"),
    )(w, *glist, m, v)


def _adam_step(w, g, m, v):
    m1 = ADAM_B1 * m + (1.0 - ADAM_B1) * g
    v1 = ADAM_B2 * v + (1.0 - ADAM_B2) * (g * g)
    m_hat = m1 / (1.0 - ADAM_B1 ** ADAM_STEP)
    v_hat = v1 / (1.0 - ADAM_B2 ** ADAM_STEP)
    return -ADAM_LR * (m_hat / (jnp.sqrt(v_hat) + ADAM_EPS) + ADAM_WD * w), m1, v1


def _adamw_small(ws, gs, ms, vs, name):
    n = len(ws)

    def body(*refs):
        ins, outs = refs[:4 * n], refs[4 * n:]
        for i in range(n):
            w_ref, g_ref, m_ref, v_ref = ins[i], ins[n + i], ins[2 * n + i], ins[3 * n + i]
            delta, m1, v1 = _adam_step(w_ref[...], g_ref[...], m_ref[...], v_ref[...])
            outs[3 * i][...] = delta
            outs[3 * i + 1][...] = m1
            outs[3 * i + 2][...] = v1

    operands = list(ws) + list(gs) + list(ms) + list(vs)
    out_shape = [jax.ShapeDtypeStruct(w.shape, F32) for w in ws for _ in range(3)]
    return pl.pallas_call(
        body, name=name, grid=(1,), in_specs=[_full(a.shape) for a in operands],
        out_specs=[_full(s.shape) for s in out_shape], out_shape=out_shape,
        compiler_params=_cp("arbitrary"),
    )(*operands)


def _pack(pieces):
    flat = jnp.concatenate([p.reshape(-1) for p in pieces])
    unit = SUBLANES * LANES
    padded = -(-flat.shape[0] // unit) * unit
    return jnp.pad(flat, (0, padded - flat.shape[0])).reshape(padded // LANES, LANES)


def _unpack(packed, shapes, lead=()):
    flat = packed.reshape(lead + (-1,))
    out, off = [], 0
    for s in shapes:
        size = 1
        for dim in s:
            size *= dim
        out.append(flat[..., off:off + size].reshape(lead + tuple(s)))
        off += size
    return out


def _pad_last(a, n):
    return jnp.pad(a, [(0, 0)] * (a.ndim - 1) + [(0, n - a.shape[-1])])


def kernel(x, c, ada_w, ada_b, ln_tok_g, ln_tok_b, ln_ch_g, ln_ch_b, a_w_in, a_conv_w, a_conv_b, a_w_out, b_w_pw1, b_b_pw1, b_conv_w, b_conv_b, b_ln_g, b_ln_b, b_w_pw2, b_b_pw2, f_w_up, f_conv_w, f_conv_b, f_w_gate, f_w_down, loss_target, m_ada_w, m_ada_b, m_ln_tok_g, m_ln_tok_b, m_ln_ch_g, m_ln_ch_b, m_a_w_in, m_a_conv_w, m_a_conv_b, m_a_w_out, m_b_w_pw1, m_b_b_pw1, m_b_conv_w, m_b_conv_b, m_b_ln_g, m_b_ln_b, m_b_w_pw2, m_b_b_pw2, m_f_w_up, m_f_conv_w, m_f_conv_b, m_f_w_gate, m_f_w_down, v_ada_w, v_ada_b, v_ln_tok_g, v_ln_tok_b, v_ln_ch_g, v_ln_ch_b, v_a_w_in, v_a_conv_w, v_a_conv_b, v_a_w_out, v_b_w_pw1, v_b_b_pw1, v_b_conv_w, v_b_conv_b, v_b_ln_g, v_b_ln_b, v_b_w_pw2, v_b_b_pw2, v_f_w_up, v_f_conv_w, v_f_conv_b, v_f_w_gate, v_f_w_down):
    weights = dict(ada_w=ada_w, ada_b=ada_b, ln_tok_g=ln_tok_g, ln_tok_b=ln_tok_b, ln_ch_g=ln_ch_g, ln_ch_b=ln_ch_b, a_w_in=a_w_in, a_conv_w=a_conv_w, a_conv_b=a_conv_b, a_w_out=a_w_out, b_w_pw1=b_w_pw1, b_b_pw1=b_b_pw1, b_conv_w=b_conv_w, b_conv_b=b_conv_b, b_ln_g=b_ln_g, b_ln_b=b_ln_b, b_w_pw2=b_w_pw2, b_b_pw2=b_b_pw2, f_w_up=f_w_up, f_conv_w=f_conv_w, f_conv_b=f_conv_b, f_w_gate=f_w_gate, f_w_down=f_w_down)
    mom_m = dict(ada_w=m_ada_w, ada_b=m_ada_b, ln_tok_g=m_ln_tok_g, ln_tok_b=m_ln_tok_b, ln_ch_g=m_ln_ch_g, ln_ch_b=m_ln_ch_b, a_w_in=m_a_w_in, a_conv_w=m_a_conv_w, a_conv_b=m_a_conv_b, a_w_out=m_a_w_out, b_w_pw1=m_b_w_pw1, b_b_pw1=m_b_b_pw1, b_conv_w=m_b_conv_w, b_conv_b=m_b_conv_b, b_ln_g=m_b_ln_g, b_ln_b=m_b_ln_b, b_w_pw2=m_b_w_pw2, b_b_pw2=m_b_b_pw2, f_w_up=m_f_w_up, f_conv_w=m_f_conv_w, f_conv_b=m_f_conv_b, f_w_gate=m_f_w_gate, f_w_down=m_f_w_down)
    mom_v = dict(ada_w=v_ada_w, ada_b=v_ada_b, ln_tok_g=v_ln_tok_g, ln_tok_b=v_ln_tok_b, ln_ch_g=v_ln_ch_g, ln_ch_b=v_ln_ch_b, a_w_in=v_a_w_in, a_conv_w=v_a_conv_w, a_conv_b=v_a_conv_b, a_w_out=v_a_w_out, b_w_pw1=v_b_w_pw1, b_b_pw1=v_b_b_pw1, b_conv_w=v_b_conv_w, b_conv_b=v_b_conv_b, b_ln_g=v_b_ln_g, b_ln_b=v_b_ln_b, b_w_pw2=v_b_w_pw2, b_b_pw2=v_b_b_pw2, f_w_up=v_f_w_up, f_conv_w=v_f_conv_w, f_conv_b=v_f_conv_b, f_w_gate=v_f_w_gate, f_w_down=v_f_w_down)
    names = list(weights)

    depth, d, n_ada = ada_w.shape
    assert depth == 2 and a_w_in.shape[0] == 1 and b_w_pw1.shape[0] == 1
    s_len = x.shape[1]
    f_loc = f_w_up.shape[-1]
    f_pad = -(-f_loc // LANES) * LANES
    f_all = NDEV * f_pad
    d_loc = d // NDEV
    ka, kb, kf = a_conv_w.shape[1], b_conv_w.shape[1], f_conv_w.shape[1]
    alpha = (2.0 * depth) ** 0.25
    assert a_w_in.shape[-1] == f_pad and f_pad % d_loc == 0
    me = 4 * lax.axis_index("x") + 2 * lax.axis_index("y") + lax.axis_index("c")

    small_shapes = [(d,), (ka, d_loc), (2 * d_loc,), (kb, d_loc), (d_loc,), (d_loc,), (d_loc,), (d_loc,),
                    (depth, kf, f_pad)]
    small_loc = _pack([c[0], a_conv_w[0], b_b_pw1[0], b_conv_w[0], b_conv_b[0], b_ln_g[0], b_ln_b[0],
                       b_b_pw2[0], _pad_last(f_conv_w, f_pad)])
    first = _exchange_start([small_loc, a_w_in.astype(BF16)], "gather_chips", "gather_first_chips_start")
    up_pad = _pad_last(_after(f_w_up, first[-1]), f_pad).astype(BF16)
    gate_pad = _pad_last(_after(f_w_gate, first[-1]), f_pad).astype(BF16)
    down_pad = jnp.pad(_after(f_w_down, first[-1]), ((0, 0), (0, f_pad - f_loc), (0, 0))).astype(BF16)
    col_f = [jnp.stack([up_pad[i], gate_pad[i]]) for i in range(depth)]
    row_b = jnp.concatenate([down_pad[1], _after(b_w_pw2[0], first[-1]).astype(BF16)], axis=0)
    out_loc, pw1_loc = _after(a_w_out[0], first[-1]).astype(BF16), _after(b_w_pw1, first[-1]).astype(BF16)
    ridx_pw2 = f_pad // d_loc
    prepared = sum(a.reshape(-1, a.shape[-1])[0:1, 0:LANES].astype(F32)
                   for a in (col_f[0], col_f[1], down_pad[0], row_b, out_loc, pw1_loc))
    *lands, landed = _exchange_wait(first, prepared, "gather_chips", "gather_first_chips_wait")
    g_small, g_in, _ = _exchange_wait(_exchange_start(lands, "forward", "gather_first_forward_start"), landed,
                                      "forward", "gather_first_forward_wait")

    (c_all, acw_g, bb1_g, bcw_g, bcb_g, blg_g, blb_g, bb2_g, fcw_g) = _unpack(g_small, small_shapes, (NDEV,))
    a_cw = acw_g.transpose(1, 0, 2).reshape(ka, d)
    b_cw = bcw_g.transpose(1, 0, 2).reshape(kb, d)
    b_b1 = bb1_g.reshape(1, 2 * d)
    b_cb, b_lg, b_lb, b_b2 = (t.reshape(1, d) for t in (bcb_g, blg_g, blb_g, bb2_g))
    f_cw = fcw_g.transpose(1, 2, 0, 3).reshape(depth, kf, f_all)
    f_cb = _pad_last(f_conv_b.reshape(depth, NDEV, f_loc), f_pad).reshape(depth, 1, f_all)

    ada_b_loc = lax.dynamic_slice(ada_b, (0, me * n_ada), (depth, n_ada))
    mod_part = _ada_fwd(c_all, ada_w, ada_b_loc, "ada_fwd")
    mod_g, mod_done = _exchange([mod_part.reshape(depth * NDEV, n_ada)], "gather", "gather_mod")
    mod_all = mod_g.reshape(NDEV, depth, NDEV, n_ada).transpose(1, 2, 0, 3).reshape(depth, NDEV, 6 * d)
    mod = lax.dynamic_slice(mod_all, (0, me, 0), (depth, 1, 6 * d))[:, 0]

    def behind(a, token):
        return a + token[0, 0].astype(BF16)

    gather_out = _exchange_start([behind(out_loc, mod_done)], "gather_chips", "gather_out_start")
    gather_f0 = _exchange_start([col_f[0], behind(down_pad[0], gather_out[-1])], "gather_chips", "gather_f0_start")

    def mod_rows(i):
        return [mod[i:i + 1, j * d:(j + 1) * d] for j in range(6)]

    zeros_d = jnp.zeros((1, d), F32)
    zeros_f = jnp.zeros((1, f_all), F32)
    x0 = x[0]

    sh_t0, sc_t0, g_t0, sh_c0, sc_c0, g_c0 = mod_rows(0)
    sh_t1, sc_t1, g_t1, sh_c1, sc_c1, g_c1 = mod_rows(1)

    sc_t0 = _after(sc_t0, gather_f0[-1])
    bcv, = _mm_fwd(x0, sc_t0, sh_t0, jnp.zeros((1, 3 * d), F32), g_in, (0,), "a_in_fwd")
    y0 = _gateconv_fwd(bcv, a_cw, a_conv_b, "a_conv_fwd")
    g_out, landed = _exchange_wait(gather_out, y0, "gather_chips", "gather_out_wait")
    g_out, _ = _exchange_wait(_exchange_start([g_out], "forward", "gather_out_fwd_start"), landed, "forward",
                              "gather_out_fwd_wait")
    y_a, x1, xh1, rs1 = _mm_ln(y0, g_out, d_loc, 0, x0, g_t0, ln_tok_g[0:1], ln_tok_b[0:1], zeros_d,
                               alpha, "a_out_ln_fwd")

    def ffn_fwd(xin, sc, sh, gate, gam, bet, g_colf, g_rowf, layer, tag):
        u0, vg = _mm_fwd(xin, sc, sh, zeros_f, g_colf, (0, 1), "f_upgate_fwd" + tag)
        t, uc, y, xo, xh, rs = _ffn_tail_fwd(u0, vg, f_cw[layer], f_cb[layer], g_rowf, xin, gate, gam, bet, alpha,
                                             "f_tail_fwd" + tag)
        return (u0, uc), vg, t, y, xo, xh, rs

    g_colf0, g_rowf0, landed = _exchange_wait(gather_f0, x1, "gather_chips", "gather_f0_wait")
    g_colf0, g_rowf0, landed = _exchange_wait(
        _exchange_start([g_colf0, g_rowf0], "forward", "gather_f0_fwd_start"), landed, "forward", "gather_f0_fwd_wait")
    gather_1 = _exchange_start([behind(pw1_loc, landed), col_f[1], row_b], "gather_chips", "gather_1_start")
    sc_c0 = _after(sc_c0, gather_1[-1])
    u0_0, vg_0, t_0, y_f0, x2, xh2, rs2 = ffn_fwd(x1, sc_c0, sh_c0, g_c0, ln_ch_g[0:1], ln_ch_b[0:1],
                                                  g_colf0, g_rowf0, 0, "0")

    *lands_1, landed = _exchange_wait(gather_1, x2, "gather_chips", "gather_1_wait")
    g_pw1, g_colf1, g_rowb, _ = _exchange_wait(_exchange_start(lands_1, "forward", "gather_1_fwd_start"), landed,
                                                 "forward", "gather_1_fwd_wait")
    ub, = _mm_fwd(x2, sc_t1, sh_t1, b_b1, g_pw1, (0,), "b_pw1_fwd")
    a2, a4 = _b_mid_fwd(ub, b_cw, b_cb, b_lg, b_lb, "b_mid_fwd")
    y_b, x3, xh3, rs3 = _mm_ln(a4, g_rowb, d_loc, ridx_pw2, x2, g_t1, ln_tok_g[1:2], ln_tok_b[1:2], b_b2,
                               alpha, "b_pw2_ln_fwd")
    u0_1, vg_1, t_1, y_f1, x4, xh4, rs4 = ffn_fwd(x3, sc_c1, sh_c1, g_c1, ln_ch_g[1:2], ln_ch_b[1:2],
                                                  g_colf1, g_rowb, 1, "1")

    ln_f1 = (xh4, rs4, ln_ch_g[1:2], y_f1, g_c1)
    ln_b = (xh3, rs3, ln_tok_g[1:2], y_b, g_t1)
    ln_f0 = (xh2, rs2, ln_ch_g[0:1], y_f0, g_c0)
    ln_a = (xh1, rs1, ln_tok_g[0:1], y_a, g_t0)
    dy, dres, accf1, loss_part = _loss_head(x4, loss_target[0], ln_f1, alpha, "loss_head")

    def ffn_bwd(dy, dres, xin, sc, sh, u0, vg, t, g_colf, g_rowf, ln_below, layer, tag):
        dw_down = _mm_tn_row(t, dy, f_pad, f_loc, "f_down_dw" + tag)
        scatter_down = _exchange_start([dw_down], "scatter", "scatter_d%s_start" % tag)
        du0, dvg, dcw, dcb, dy_below, dres_below, acc_below, acc2 = _ffn_core_bwd(
            dy, u0[0], u0[1], vg, f_cw[layer], g_rowf, g_colf, xin, _after(sc, scatter_down[-1]), dres,
            ln_below, alpha, "f_core_bwd" + tag)
        dw_up = _mm_tn_col_t(xin, sc, sh, du0, f_loc, "f_up_dw" + tag)
        dw_gate = _mm_tn_col_t(xin, sc, sh, dvg, f_loc, "f_gate_dw" + tag)
        scatter = _exchange_start([dw_up, dw_gate], "scatter", "scatter_f%s_start" % tag)
        return dy_below, dres_below, acc_below, acc2, (scatter, scatter_down), dcw, dcb

    dy, dres, accb, acc2f1, (scatter_f1, scatter_d1), dfcw1, dfcb1 = ffn_bwd(
        dy, dres, x3, sc_c1, sh_c1, u0_1, vg_1, t_1, g_colf1, g_rowb, ln_b, 1, "1")

    da4 = _mm_nt_row(dy, g_rowb, d_loc, ridx_pw2, "b_pw2_dx")
    dw_pw2 = _mm_tn_row(a4, dy, d_loc, d_loc, "b_pw2_dw")
    du, dbcw, dbcb, dblg, dblb, dbb1 = _b_mid_bwd(ub, a2, da4, b_cw, _after(b_lg, scatter_f1[-1]), b_lb, "b_mid_bwd")
    dw_pw1 = _mm_tn_col(x2, sc_t1, sh_t1, du, "b_pw1_dw")
    scatter_b = _exchange_start([dw_pw1, dw_pw2], "scatter", "scatter_b_start")
    dy, dres, accf0, acc2b = _mm_nt_mod([du], g_pw1, (0,), x2, _after(sc_t1, scatter_b[-1]), dres, "b_pw1_dx",
                                        ln=ln_f0, alpha=alpha)

    dy, dres, acca, acc2f0, (scatter_f0, scatter_d0), dfcw0, dfcb0 = ffn_bwd(
        dy, dres, x1, sc_c0, sh_c0, u0_0, vg_0, t_0, g_colf0, g_rowf0, ln_a, 0, "0")

    dy0 = _mm_nt_row(dy, g_out, d_loc, 0, "a_out_dx")
    dbcv, dacw, dacb = _gateconv_bwd(bcv, dy0, a_cw, _after(a_conv_b, scatter_f0[-1]), "a_conv_bwd")
    dx0, acc2a = _mm_nt_mod([dbcv], g_in, (0,), x0, sc_t0, dres, "a_in_dx")

    def dmod_row(acc2_t, acc_t, acc2_c, acc_c):
        return jnp.concatenate([acc2_t[1], acc2_t[0], acc_t[2], acc2_c[1], acc2_c[0], acc_c[2]])

    dmod = jnp.stack([dmod_row(acc2a, acca, acc2f0, accf0), dmod_row(acc2b, accb, acc2f1, accf1)])

    def unpad_f(a):
        return a.reshape(a.shape[:-1] + (NDEV, f_pad))[..., :f_loc].reshape(a.shape[:-1] + (NDEV * f_loc,))

    small_grads = [
        dmod,
        jnp.stack([acca[0], accb[0]]), jnp.stack([acca[1], accb[1]]),
        jnp.stack([accf0[0], accf1[0]]), jnp.stack([accf0[1], accf1[1]]),
        dacb,
        unpad_f(jnp.concatenate([dfcb0, dfcb1], axis=0)),
        dacw, dbb1, dbcw, dbcb, dblg, dblb, accb[3:4],
        jnp.stack([dfcw0, dfcw1]),
        loss_part[0:1, 0:1],
    ]
    small_grad_shapes = [tuple(g.shape) for g in small_grads]
    gather_small = _exchange_start([_pack(small_grads)], "gather", "gather_small_start")

    dw_in = _mm_tn_col(x0, _after(sc_t0, gather_small[-1]), sh_t0, dbcv, "a_in_dw")
    dw_out = _mm_tn_row(y0, dy, d_loc, d_loc, "a_out_dw")
    scatter_a = _exchange_start([dw_in, dw_out], "scatter", "scatter_a_start")

    grads, deltas, new_m, new_v = {}, {}, {}, {}

    def adamw(k, glist, transposed=False):
        def view(a):
            a = jnp.swapaxes(a, 1, 2) if transposed else a
            return a.reshape(len(glist), -1, a.shape[-1])

        w = view(weights[k])
        outs = _adamw(w, [g.reshape(g.shape[0], -1, w.shape[-1]) for g in glist],
                      view(mom_m[k]), view(mom_v[k]), "adamw_" + k)
        if transposed:
            outs = [jnp.swapaxes(o, 1, 2) for o in outs]
        grads[k], deltas[k], new_m[k], new_v[k] = (o.reshape(weights[k].shape) for o in outs)

    r_up1, r_gate1, _ = _exchange_wait(scatter_f1, scatter_a[-1], "scatter", "scatter_f1_wait")
    r_down1, _ = _exchange_wait(scatter_d1, r_gate1, "scatter", "scatter_d1_wait")
    r_pw1, r_pw2, _ = _exchange_wait(scatter_b, r_down1, "scatter", "scatter_b_wait")
    adamw("b_w_pw1", [r_pw1])
    adamw("b_w_pw2", [r_pw2])
    r_down0, _ = _exchange_wait(scatter_d0, deltas["b_w_pw2"], "scatter", "scatter_d0_wait")
    adamw("f_w_down", [r_down0, r_down1])
    r_up0, r_gate0, _ = _exchange_wait(scatter_f0, deltas["f_w_down"], "scatter", "scatter_f0_wait")
    adamw("f_w_up", [r_up0, r_up1], transposed=True)
    adamw("f_w_gate", [r_gate0, r_gate1], transposed=True)

    sg_all, _ = _exchange_wait(gather_small, deltas["f_w_gate"], "gather", "gather_small_wait")
    sg_sum = _sum_parts(sg_all, "sum_small_grads")
    (g_ada_b, g_ltg, g_ltb, g_lcg, g_lcb, g_acb, g_fcb, g_acw, g_bb1, g_bcw, g_bcb, g_blg, g_blb, g_bb2,
     g_fcw, loss_all) = _unpack(sg_sum, small_grad_shapes)
    loss = loss_all[0, 0]

    def my_cols(a, width):
        return lax.dynamic_slice_in_dim(a, me * width, width, axis=a.ndim - 1)

    g_fcw_loc = my_cols(g_fcw, f_pad)[..., :f_loc]
    small = dict(
        ada_b=g_ada_b, ln_tok_g=g_ltg, ln_tok_b=g_ltb, ln_ch_g=g_lcg, ln_ch_b=g_lcb, a_conv_b=g_acb, f_conv_b=g_fcb,
        a_conv_w=my_cols(g_acw, d_loc)[None], b_b_pw1=my_cols(g_bb1, 2 * d_loc), b_conv_w=my_cols(g_bcw, d_loc)[None],
        b_conv_b=my_cols(g_bcb, d_loc), b_ln_g=my_cols(g_blg, d_loc), b_ln_b=my_cols(g_blb, d_loc),
        b_b_pw2=my_cols(g_bb2, d_loc), f_conv_w=g_fcw_loc)

    dmod_all = sg_all.reshape(NDEV, -1)[:, :depth * 6 * d].reshape(NDEV, depth, 6 * d)
    dmod_cols = my_cols(dmod_all, n_ada).transpose(1, 0, 2)
    g_ada_w = _ada_bwd(c_all.T, dmod_cols, "ada_bwd")

    adamw("ada_w", [g_ada_w[0:1], g_ada_w[1:2]])

    def rows_cols(a):
        return a.reshape(-1, a.shape[-1])

    small_keys = list(small)
    small_outs = _adamw_small([rows_cols(weights[k]) for k in small_keys], [rows_cols(small[k]) for k in small_keys],
                              [rows_cols(mom_m[k]) for k in small_keys], [rows_cols(mom_v[k]) for k in small_keys],
                              "adamw_small")
    for i, k in enumerate(small_keys):
        grads[k] = small[k].reshape(weights[k].shape)
        deltas[k], new_m[k], new_v[k] = (o.reshape(weights[k].shape) for o in small_outs[3 * i:3 * i + 3])

    r_in, r_out, _ = _exchange_wait(scatter_a, deltas["ada_w"], "scatter", "scatter_a_wait")
    adamw("a_w_in", [r_in])
    adamw("a_w_out", [r_out])

    return (loss, dx0[None], *[grads[k] for k in names], *[deltas[k] for k in names],
            *[new_m[k] for k in names], *[new_v[k] for k in names])
```
